```python
import math
import jax
import jax.numpy as jnp
from jax import lax
import numpy as np

D_MODEL = 1024
BATCH = 4
SEQ = 8192
DEPTH = 2

DEEPNORM_ALPHA = (2.0 * DEPTH) ** 0.25
DEEPNORM_BETA = (8.0 * DEPTH) ** -0.25
LN_EPS = 1e-5

N_ATTN_HEADS = 8
REL_BUCKETS = 32
REL_MAX_DIST = 2048

A_HEADS = 8
A_HEAD_DIM = D_MODEL // 16
A_WIDTH = A_HEADS * A_HEAD_DIM
A_PATTERNS = ((128, 1), (512, 4), (2048, 16))
A_BLOCK = 128

B_HEADS = 4
B_HEAD_DIM = D_MODEL // 8
B_WIDTH = B_HEADS * B_HEAD_DIM
B_CHUNK = 128
B_CONV = 4

AB_AQ = 0
AB_AK = AB_AQ + A_WIDTH
AB_AV = AB_AK + A_WIDTH
AB_BQ = AB_AV + A_WIDTH
AB_BK = AB_BQ + B_WIDTH
AB_BV = AB_BK + B_WIDTH
AB_BO = AB_BV + B_WIDTH
AB_BI = AB_BO + B_WIDTH
AB_BF = AB_BI + B_HEADS
AB_PROJ = AB_BF + B_HEADS
AB_MIX = A_WIDTH + B_WIDTH

C_HEADS = 8
C_GROUPS = 2
C_HPG = C_HEADS // C_GROUPS
C_HEAD_DIM = D_MODEL // 8
C_WIDTH = C_HEADS * C_HEAD_DIM
CMP_BLOCK = 32
CMP_STRIDE = 16
CMP_HIDDEN = 256
SLC_BLOCK = 64
SLC_TOP_N = 16
WIN = 512
C_QBLOCK = 64
C_OFF_KV = C_WIDTH
C_OFF_GATE = C_OFF_KV + 3 * 2 * C_GROUPS * C_HEAD_DIM
C_PROJ = C_OFF_GATE + 3 * C_HEADS

N_GROUPS = 4
EXPERTS_PER_GROUP = 4
N_EXPERTS = N_GROUPS * EXPERTS_PER_GROUP
EXPERT_HIDDEN = D_MODEL // 2
MOE_TOP_K = 2
MOE_CHUNK = 128

kernel_name = "hybrid_dilated_mlstm_nsa_hmoe_deepnorm"


def layer_norm(x, g, b):
    xf = x.astype(jnp.float32)
    mu = xf.mean(-1, keepdims=True)
    var = jnp.square(xf - mu).mean(-1, keepdims=True)
    y = (xf - mu) * lax.rsqrt(var + LN_EPS) * g.astype(jnp.float32) + b.astype(jnp.float32)
    return y.astype(x.dtype)


def _t5_bucket(dist):
    n = jnp.maximum(dist, 0)
    exact = REL_BUCKETS // 2
    nf = jnp.maximum(n, 1).astype(jnp.float32)
    large = exact + (jnp.log(nf / exact) / math.log(REL_MAX_DIST / exact)
                     * (REL_BUCKETS - exact)).astype(jnp.int32)
    return jnp.where(n < exact, n, jnp.minimum(large, REL_BUCKETS - 1))


def _t5_bias(table, dist):
    return table.astype(jnp.float32)[_t5_bucket(dist)]


def masked_softmax(logits, valid):
    masked = jnp.where(valid, logits.astype(jnp.float32), -jnp.inf)
    m = jnp.max(masked, axis=-1, keepdims=True)
    m = jnp.where(jnp.isfinite(m), m, 0.0)
    p = jnp.exp(masked - m)
    s = p.sum(-1, keepdims=True)
    return p / jnp.maximum(s, 1e-30)


def causal_conv(x, w):
    K, C = w.shape
    return lax.conv_general_dilated(x, w[:, None, :].astype(x.dtype), window_strides=(1,),
                                    padding=[(K - 1, 0)],
                                    dimension_numbers=('NWC', 'WIO', 'NWC'),
                                    feature_group_count=C)


def _dilated_branch(q, k, v, rel_bias, window, dilation):
    Bn, S, H, E = q.shape
    nk = window // dilation
    L = -(-S // dilation)
    nb = -(-L // A_BLOCK)
    Lp = nb * A_BLOCK
    pad = Lp * dilation - S

    def to_res(t):
        t = jnp.pad(t, ((0, 0), (0, pad), (0, 0), (0, 0)))
        t = t.reshape(Bn, Lp, dilation, H, E).transpose(0, 2, 1, 3, 4)
        return t.reshape(Bn, dilation, nb, A_BLOCK, H, E)

    def with_prev(t):
        prev = jnp.pad(t, ((0, 0), (0, 0), (1, 0), (0, 0), (0, 0), (0, 0)))[:, :, :-1]
        return jnp.concatenate([prev, t], axis=3)

    qb = to_res(q)
    kk = with_prev(to_res(k))
    vv = with_prev(to_res(v))
    logits = jnp.einsum('brnqhe,brnkhe->brnhqk', qb, kk).astype(jnp.float32) * (E ** -0.5)
    qi = jnp.arange(A_BLOCK)[:, None]
    ki = jnp.arange(2 * A_BLOCK)[None, :]
    j = qi + A_BLOCK - ki
    valid = (j >= 0) & (j <= nk)
    bias = _t5_bias(rel_bias, jnp.maximum(j, 0) * dilation).transpose(2, 0, 1)
    first = (jnp.arange(nb)[:, None, None] > 0) | (ki[None] >= A_BLOCK)
    valid = valid[None] & first
    logits = jnp.where(valid[None, None, :, None], logits + bias, -jnp.inf)
    m = logits.max(-1, keepdims=True)
    p = jnp.exp(logits - m)
    s = p.sum(-1)
    o = jnp.einsum('brnhqk,brnkhe->brnhqe', p, vv.astype(jnp.float32)) / s[..., None]
    lse = m[..., 0] + jnp.log(s)
    o = o.transpose(0, 2, 4, 1, 3, 5).reshape(Bn, Lp * dilation, H, E)[:, :S]
    lse = lse.transpose(0, 2, 4, 1, 3).reshape(Bn, Lp * dilation, H)[:, :S]
    return o, lse


def dilated_attention(q, k, v, rel_bias):
    outs, lses = [], []
    for window, dilation in A_PATTERNS:
        o, lse = _dilated_branch(q, k, v, rel_bias, window, dilation)
        outs.append(o)
        lses.append(lse)
    wts = jax.nn.softmax(jnp.stack(lses, 0), axis=0)
    return jnp.einsum('pbsh,pbshe->bshe', wts, jnp.stack(outs, 0))


def mlstm(q, k, v, i_pre, f_pre):
    f32 = jnp.float32
    Bn, S, H, dk = q.shape
    dv = v.shape[-1]
    L = B_CHUNK
    nc = S // L

    def chunks(t):
        t = t.astype(f32).reshape((Bn, nc, L) + t.shape[2:])
        return t.transpose((1, 0, 3, 2) + tuple(range(4, t.ndim)))

    qc = chunks(q) * (dk ** -0.5)
    kc, vc = chunks(k), chunks(v)
    ic = chunks(i_pre)
    fc = chunks(jax.nn.log_sigmoid(f_pre.astype(f32)))
    tri = jnp.tril(jnp.ones((L, L), bool))

    def step(carry, inp):
        C, n, m = carry
        qt, kt, vt, it, ft = inp
        b = jnp.cumsum(ft, axis=-1)
        Dm = jnp.where(tri, b[..., :, None] - b[..., None, :] + it[..., None, :], -jnp.inf)
        inter = b + m[..., None]
        m_t = jnp.maximum(inter, Dm.max(-1))
        P = jnp.exp(Dm - m_t[..., None])
        Sqk = jnp.einsum('bhtd,bhsd->bhts', qt, kt) * P
        sc = jnp.exp(inter - m_t)
        num = sc[..., None] * jnp.einsum('bhtd,bhdv->bhtv', qt, C) + jnp.einsum('bhts,bhsv->bhtv', Sqk, vt)
        den = sc * jnp.einsum('bhtd,bhd->bht', qt, n) + Sqk.sum(-1)
        h = num / jnp.maximum(jnp.abs(den), jnp.exp(-m_t))[..., None]
        bL = b[..., -1]
        g = bL[..., None] - b + it
        m_new = jnp.maximum(bL + m, g.max(-1))
        wk = jnp.exp(g - m_new[..., None])
        decay = jnp.exp(bL + m - m_new)
        C_new = decay[..., None, None] * C + jnp.einsum('bhs,bhsd,bhsv->bhdv', wk, kt, vt)
        n_new = decay[..., None] * n + jnp.einsum('bhs,bhsd->bhd', wk, kt)
        return (C_new, n_new, m_new), h

    init = (jnp.zeros((Bn, H, dk, dv), f32), jnp.zeros((Bn, H, dk), f32), jnp.zeros((Bn, H), f32))
    _, hs = lax.scan(step, init, (qc, kc, vc, ic, fc))
    return hs.transpose(1, 0, 3, 2, 4).reshape(Bn, S, H, dv)


def head_norm(h, g):
    mu = h.mean(-1, keepdims=True)
    var = jnp.square(h - mu).mean(-1, keepdims=True)
    y = (h - mu) * lax.rsqrt(var + LN_EPS)
    return y.reshape(h.shape[0], h.shape[1], -1) * g.astype(jnp.float32)


def mixer_ab(x, w_in, gate_b, conv_w, head_g, w_out, rel_bias):
    Bn, S, _ = x.shape
    p = jnp.einsum('bsd,de->bse', x, w_in)
    aq = p[..., AB_AQ:AB_AK].reshape(Bn, S, A_HEADS, A_HEAD_DIM)
    ak = p[..., AB_AK:AB_AV].reshape(Bn, S, A_HEADS, A_HEAD_DIM)
    av = p[..., AB_AV:AB_BQ].reshape(Bn, S, A_HEADS, A_HEAD_DIM)
    qk = jax.nn.silu(causal_conv(p[..., AB_BQ:AB_BV], conv_w))
    bq = qk[..., :B_WIDTH].reshape(Bn, S, B_HEADS, B_HEAD_DIM)
    bk = qk[..., B_WIDTH:].reshape(Bn, S, B_HEADS, B_HEAD_DIM)
    bv = p[..., AB_BV:AB_BO].reshape(Bn, S, B_HEADS, B_HEAD_DIM)
    bo = p[..., AB_BO:AB_BI]
    bi = p[..., AB_BI:AB_BF] + gate_b[:B_HEADS]
    bf = p[..., AB_BF:AB_PROJ] + gate_b[B_HEADS:]
    ya = dilated_attention(aq, ak, av, rel_bias).reshape(Bn, S, A_WIDTH)
    yb = jax.nn.sigmoid(bo.astype(jnp.float32)) * head_norm(mlstm(bq, bk, bv, bi, bf), head_g)
    y = jnp.concatenate([ya.astype(x.dtype), yb.astype(x.dtype)], axis=-1)
    return jnp.einsum('bse,ed->bsd', y, w_out)


def mixer_c(x, w_in, gate_b, cmp_pos, cmp_w1, cmp_w2, w_out, rel_bias):
    f32 = jnp.float32
    Bn, S, _ = x.shape
    G, J, E = C_GROUPS, C_HPG, C_HEAD_DIM
    p = jnp.einsum('bsd,de->bse', x, w_in)
    q = p[..., :C_OFF_KV].reshape(Bn, S, G, J, E).transpose(0, 2, 3, 1, 4).astype(f32) * (E ** -0.5)
    kv = p[..., C_OFF_KV:C_OFF_GATE].reshape(Bn, S, 3, 2, G, E).astype(f32)
    gates = jax.nn.sigmoid((p[..., C_OFF_GATE:] + gate_b).astype(f32)).reshape(Bn, S, 3, G, J)

    n_cmp = (S - CMP_BLOCK) // CMP_STRIDE + 1
    cidx = jnp.arange(n_cmp)[:, None] * CMP_STRIDE + jnp.arange(CMP_BLOCK)[None, :]
    cmp_start, cmp_end = cidx[:, 0], cidx[:, -1]

    def compress(t, pos, w1, w2):
        blocks = t[:, cidx] + pos[None, None, :, None, :]
        flat = blocks.transpose(0, 3, 1, 2, 4).reshape(Bn, G, n_cmp, CMP_BLOCK * E)
        return (jax.nn.gelu(flat @ w1) @ w2).astype(f32)

    k_cmp = compress(kv[:, :, 0, 0], cmp_pos[0], cmp_w1[0], cmp_w2[0])
    v_cmp = compress(kv[:, :, 0, 1], cmp_pos[1], cmp_w1[1], cmp_w2[1])

    n_slc = S // SLC_BLOCK
    top_n = min(SLC_TOP_N, n_slc)
    slc_start = jnp.arange(n_slc) * SLC_BLOCK

    def blocks_of(t):
        return t.transpose(0, 2, 1, 3).reshape(Bn, G, n_slc, SLC_BLOCK * E)

    k_sb, v_sb = blocks_of(kv[:, :, 1, 0]), blocks_of(kv[:, :, 1, 1])
    overlap = ((cmp_start[:, None] < slc_start[None, :] + SLC_BLOCK)
               & (cmp_end[:, None] >= slc_start[None, :])).astype(f32)

    def pad_win(t):
        return jnp.pad(t.transpose(0, 2, 1, 3), ((0, 0), (0, 0), (WIN, 0), (0, 0)))

    k_w, v_w = pad_win(kv[:, :, 2, 0]), pad_win(kv[:, :, 2, 1])

    tb = rel_bias.astype(f32).reshape(REL_BUCKETS, G, J)
    tb_g = tb.transpose(1, 0, 2)
    gsel = jnp.arange(G)[None, :, None, None]

    def grouped_bias(dist2d):
        return tb[_t5_bucket(dist2d)].transpose(2, 3, 0, 1)

    def block(blk):
        s0 = blk * C_QBLOCK
        qpos = s0 + jnp.arange(C_QBLOCK)
        qb = lax.dynamic_slice_in_dim(q, s0, C_QBLOCK, axis=3)
        dist_c = qpos[:, None] - cmp_end[None, :]
        lc = jnp.einsum('bgjqe,bgne->bgjqn', qb, k_cmp) + grouped_bias(dist_c)
        p_c = masked_softmax(lc, dist_c >= 0)
        o_c = jnp.einsum('bgjqn,bgne->bgjqe', p_c, v_cmp)
        imp = jnp.einsum('bgjqn,nm->bgqm', p_c, overlap)
        mblk = jnp.arange(n_slc)[None, :]
        qblk = (qpos // SLC_BLOCK)[:, None]
        forced = (mblk == 0) | (mblk == qblk) | (mblk == qblk - 1)
        score = jnp.where(forced, jnp.inf,
                          jnp.where(slc_start[None, :] <= qpos[:, None], imp, -jnp.inf))
        _, sel = lax.top_k(score, top_n)
        flat_sel = sel.reshape(Bn, G, C_QBLOCK * top_n, 1)
        T = top_n * SLC_BLOCK
        ks = jnp.take_along_axis(k_sb, flat_sel, axis=2).reshape(Bn, G, C_QBLOCK, T, E)
        vs = jnp.take_along_axis(v_sb, flat_sel, axis=2).reshape(Bn, G, C_QBLOCK, T, E)
        kpos_s = (sel[..., None] * SLC_BLOCK + jnp.arange(SLC_BLOCK)).reshape(Bn, G, C_QBLOCK, T)
        dist_s = qpos[None, None, :, None] - kpos_s
        bias_s = jnp.moveaxis(tb_g[gsel, _t5_bucket(dist_s)], -1, 2)
        ls = jnp.einsum('bgjqe,bgqte->bgjqt', qb, ks) + bias_s
        p_s = masked_softmax(ls, (dist_s >= 0)[:, :, None])
        o_s = jnp.einsum('bgjqt,bgqte->bgjqe', p_s, vs)
        kw = lax.dynamic_slice_in_dim(k_w, s0, WIN + C_QBLOCK, axis=2)
        vw = lax.dynamic_slice_in_dim(v_w, s0, WIN + C_QBLOCK, axis=2)
        kpos_w = s0 - WIN + jnp.arange(WIN + C_QBLOCK)
        dist_w = qpos[:, None] - kpos_w[None, :]
        valid_w = (dist_w >= 0) & (dist_w < WIN) & (kpos_w[None, :] >= 0)
        lw = jnp.einsum('bgjqe,bgte->bgjqt', qb, kw) + grouped_bias(dist_w)
        p_w = masked_softmax(lw, valid_w)
        o_w = jnp.einsum('bgjqt,bgte->bgjqe', p_w, vw)
        g = lax.dynamic_slice_in_dim(gates, s0, C_QBLOCK, axis=1).transpose(2, 0, 3, 4, 1)
        return g[0][..., None] * o_c + g[1][..., None] * o_s + g[2][..., None] * o_w

    out = lax.map(block, jnp.arange(S // C_QBLOCK))
    out = out.transpose(1, 0, 4, 2, 3, 5).reshape(Bn, S, C_WIDTH)
    return jnp.einsum('bse,ed->bsd', out.astype(x.dtype), w_out)


def hier_moe(x, wr_g, br_g, wr_e, br_e, w_gate, w_up, w_down):
    f32 = jnp.float32
    Bn, S, D = x.shape
    xt = x.reshape(-1, D)
    g_logits = (xt @ wr_g + br_g).astype(f32)
    g_prob = jax.nn.softmax(g_logits, axis=-1)
    _, g_idx = lax.top_k(g_logits, 1)
    g_w = jnp.take_along_axis(g_prob, g_idx, axis=-1)[:, 0]
    e_all = (jnp.einsum('nd,gde->nge', xt, wr_e) + br_e).astype(f32)
    e_logits = jnp.take_along_axis(e_all, g_idx[:, :, None], axis=1)[:, 0]
    top_v, top_i = lax.top_k(e_logits, MOE_TOP_K)
    e_w = jax.nn.softmax(top_v, axis=-1) * g_w[:, None]
    expert_id = g_idx * EXPERTS_PER_GROUP + top_i
    gates = jnp.einsum('nk,nke->ne', e_w, jax.nn.one_hot(expert_id, N_EXPERTS, dtype=f32))
    wg = w_gate.transpose(1, 0, 2).reshape(D, N_EXPERTS * EXPERT_HIDDEN)
    wu = w_up.transpose(1, 0, 2).reshape(D, N_EXPERTS * EXPERT_HIDDEN)
    wd = w_down.reshape(N_EXPERTS * EXPERT_HIDDEN, D)

    def chunk_fn(args):
        xc, gc = args
        h = jax.nn.silu(xc @ wg) * (xc @ wu)
        h = (h.reshape(-1, N_EXPERTS, EXPERT_HIDDEN) * gc[..., None]).reshape(-1, N_EXPERTS * EXPERT_HIDDEN)
        return h @ wd

    out = lax.map(chunk_fn, (xt.reshape(-1, MOE_CHUNK, D), gates.reshape(-1, MOE_CHUNK, N_EXPERTS)))
    return out.reshape(Bn, S, D).astype(x.dtype)


def setup_inputs(seed: int = 0) -> dict:
    key = jax.random.key(seed)
    ks = iter(jax.random.split(key, 32))

    def nrm(shape, scale):
        return jax.random.normal(next(ks), shape, jnp.float32) * scale

    n_even = (DEPTH + 1) // 2
    n_odd = DEPTH // 2
    D = D_MODEL
    ab_col = np.ones(AB_PROJ, np.float32)
    ab_col[AB_AV:AB_AV + A_WIDTH] = DEEPNORM_BETA
    ab_col[AB_BV:AB_BV + B_WIDTH] = DEEPNORM_BETA
    kv_col = np.ones((3, 2, C_GROUPS * C_HEAD_DIM), np.float32)
    kv_col[:, 1] = DEEPNORM_BETA
    c_col = np.concatenate([np.ones(C_WIDTH, np.float32), kv_col.ravel(), np.ones(3 * C_HEADS, np.float32)])
    return {
        "x": nrm((BATCH, SEQ, D), 1.0),
        "rel_bias": nrm((REL_BUCKETS, N_ATTN_HEADS), 0.2),
        "ln_g": 1.0 + nrm((DEPTH, 2, D), 0.02),
        "ln_b": nrm((DEPTH, 2, D), 0.02),
        "ab_w_in": nrm((n_even, D, AB_PROJ), D ** -0.5) * jnp.asarray(ab_col),
        "ab_gate_b": jnp.concatenate([nrm((n_even, B_HEADS), 0.1),
                                      3.0 + nrm((n_even, B_HEADS), 0.5)], axis=-1),
        "ab_conv": nrm((n_even, B_CONV, 2 * B_WIDTH), B_CONV ** -0.5),
        "ab_head_norm": 1.0 + nrm((n_even, B_WIDTH), 0.02),
        "ab_w_out": nrm((n_even, AB_MIX, D), AB_MIX ** -0.5 * DEEPNORM_BETA),
        "c_w_in": nrm((n_odd, D, C_PROJ), D ** -0.5) * jnp.asarray(c_col),
        "c_gate_b": nrm((n_odd, 3 * C_HEADS), 0.1),
        "c_cmp_pos": nrm((n_odd, 2, CMP_BLOCK, C_HEAD_DIM), 0.1),
        "c_cmp_w1": nrm((n_odd, 2, CMP_BLOCK * C_HEAD_DIM, CMP_HIDDEN), (CMP_BLOCK * C_HEAD_DIM) ** -0.5),
        "c_cmp_w2": nrm((n_odd, 2, CMP_HIDDEN, C_HEAD_DIM), CMP_HIDDEN ** -0.5),
        "c_w_out": nrm((n_odd, C_WIDTH, D), C_WIDTH ** -0.5 * DEEPNORM_BETA),
        "moe_wr_g": nrm((DEPTH, D, N_GROUPS), D ** -0.5),
        "moe_br_g": nrm((DEPTH, N_GROUPS), 0.01),
        "moe_wr_e": nrm((DEPTH, N_GROUPS, D, EXPERTS_PER_GROUP), D ** -0.5),
        "moe_br_e": nrm((DEPTH, N_GROUPS, EXPERTS_PER_GROUP), 0.01),
        "moe_w_gate": nrm((DEPTH, N_EXPERTS, D, EXPERT_HIDDEN), D ** -0.5),
        "moe_w_up": nrm((DEPTH, N_EXPERTS, D, EXPERT_HIDDEN), D ** -0.5),
        "moe_w_down": nrm((DEPTH, N_EXPERTS, EXPERT_HIDDEN, D), EXPERT_HIDDEN ** -0.5 * DEEPNORM_BETA),
    }


def reference(x, rel_bias, ln_g, ln_b, ab_w_in, ab_gate_b, ab_conv, ab_head_norm, ab_w_out,
              c_w_in, c_gate_b, c_cmp_pos, c_cmp_w1, c_cmp_w2, c_w_out,
              moe_wr_g, moe_br_g, moe_wr_e, moe_br_e, moe_w_gate, moe_w_up, moe_w_down):
    h = x
    for layer in range(DEPTH):
        j = layer // 2
        if layer % 2 == 0:
            y = mixer_ab(h, ab_w_in[j], ab_gate_b[j], ab_conv[j], ab_head_norm[j], ab_w_out[j], rel_bias)
        else:
            y = mixer_c(h, c_w_in[j], c_gate_b[j], c_cmp_pos[j], c_cmp_w1[j], c_cmp_w2[j], c_w_out[j], rel_bias)
        h = layer_norm(DEEPNORM_ALPHA * h + y, ln_g[layer, 0], ln_b[layer, 0])
        y = hier_moe(h, moe_wr_g[layer], moe_br_g[layer], moe_wr_e[layer], moe_br_e[layer],
                     moe_w_gate[layer], moe_w_up[layer], moe_w_down[layer])
        h = layer_norm(DEEPNORM_ALPHA * h + y, ln_g[layer, 1], ln_b[layer, 1])
    return h
```

```python
import functools
import math

import numpy as np
import jax
import jax.numpy as jnp
from jax import lax
from jax.experimental import pallas as pl
from jax.experimental.pallas import tpu as pltpu

F32 = jnp.float32
BF16 = jnp.bfloat16
NEG = -1e30
VMEM_LIMIT = 48 * 1024 * 1024

D_MODEL = 1024
DEPTH = 2
ALPHA = (2.0 * DEPTH) ** 0.25
LN_EPS = 1e-5
REL_BUCKETS = 32
REL_MAX_DIST = 2048

A_HEADS, A_DIM, A_W = 8, 64, 512
A_PATTERNS = ((128, 1), (512, 4), (2048, 16))
A_BLOCK = 128
B_HEADS, B_DIM, B_W = 4, 128, 512
B_CHUNK = 128
B_CONV = 4
AB_PROJ = 3592
AB_PAD = 3712

C_HEADS, C_GROUPS, C_HPG, C_DIM, C_W = 8, 2, 4, 128, 1024
CMP_BLOCK, CMP_STRIDE, CMP_HIDDEN = 32, 16, 256
SLC_BLOCK, SLC_TOP_N, WIN = 64, 16, 512
C_PROJ = 2584
TQ = 128
CMP_PAD = 128

N_GROUPS, EPG, N_EXPERTS, E_HID = 4, 4, 16, 512
N_BUCKETS = N_GROUPS * 6
TM = 256


def _dot(a, b):
    return jnp.dot(a, b, preferred_element_type=F32)


def _dot_nt(a, b):
    return lax.dot_general(a, b, (((1,), (1,)), ((), ())), preferred_element_type=F32)


def _params(sem):
    return pltpu.CompilerParams(dimension_semantics=sem, vmem_limit_bytes=VMEM_LIMIT)


def _bucket_np(n):
    n = np.maximum(n, 0)
    exact = REL_BUCKETS // 2
    nf = np.maximum(n, 1).astype(np.float64)
    large = exact + (np.log(nf / exact) / math.log(REL_MAX_DIST / exact)
                     * (REL_BUCKETS - exact)).astype(np.int64)
    return np.where(n < exact, n, np.minimum(large, REL_BUCKETS - 1)).astype(np.int32)


def _bias_tab_kernel(tab_ref, idx_ref, out_ref, *, shift):
    R = idx_ref.shape[1]
    RC = 32

    def body(i, carry):
        r0 = pl.multiple_of(i * RC, RC)
        idx = idx_ref[0, pl.ds(r0, RC), :]
        for h in range(8):
            base = tab_ref[REL_BUCKETS - 1, h] if shift else 0.0
            val = jnp.full(idx.shape, tab_ref[0, h] - base, F32)
            for b in range(1, REL_BUCKETS):
                val = jnp.where(idx == b, tab_ref[b, h] - base, val)
            out_ref[0, h, pl.ds(r0, RC), :] = jnp.where(idx < 0, NEG, val)
        return carry

    lax.fori_loop(0, R // RC, body, 0)


def _bias_tables(rel_bias, idx_np, shift):
    T, R, C = idx_np.shape
    return pl.pallas_call(
        functools.partial(_bias_tab_kernel, shift=shift),
        grid=(T,),
        in_specs=[pl.BlockSpec(memory_space=pltpu.SMEM),
                  pl.BlockSpec((1, R, C), lambda t: (t, 0, 0))],
        out_specs=pl.BlockSpec((1, 8, R, C), lambda t: (t, 0, 0, 0)),
        out_shape=jax.ShapeDtypeStruct((T, 8, R, C), F32),
        compiler_params=_params(("parallel",)),
        name="bias_tables",
    )(rel_bias.astype(F32), jnp.asarray(idx_np))


def _dilated_idx():
    qi = np.arange(A_BLOCK)[:, None]
    ki = np.arange(2 * A_BLOCK)[None, :]
    j = qi + A_BLOCK - ki
    out = []
    for window, dilation in A_PATTERNS:
        nk = window // dilation
        valid = (j >= 0) & (j <= nk)
        out.append(np.where(valid, _bucket_np(np.maximum(j, 0) * dilation), -1))
    return np.stack(out).astype(np.int32)


def _sel_idx():
    a = np.arange(TQ)[:, None]
    c = np.arange(TQ)[None, :]
    n_delta = -(-(_far_dist() + TQ) // TQ)
    out = []
    for delta in range(n_delta + 1):
        dist = delta * TQ + a - c
        out.append(np.where(dist >= 0, _bucket_np(dist), -1))
    return np.stack(out).astype(np.int32)


def _far_dist():
    n = np.arange(0, 4 * REL_MAX_DIST)
    b = _bucket_np(n)
    return int(np.max(n[b < REL_BUCKETS - 1])) + 1


def _win_idx():
    a = np.arange(TQ)[:, None]
    c = np.arange(TQ)[None, :]
    out = []
    for delta in range(WIN // TQ + 1):
        dist = delta * TQ + a - c
        out.append(np.where((dist >= 0) & (dist < WIN), _bucket_np(dist), -1))
    return np.stack(out).astype(np.int32)


CMP_PER_TILE = TQ // CMP_STRIDE
CMP_CLASSES = TQ // CMP_PER_TILE
CMP_SPLIT = 13


def _cmp_window_start(qb):
    return qb // CMP_CLASSES + (1 if qb % CMP_CLASSES >= CMP_SPLIT else 0)


def _cmp_idx():
    a = np.arange(TQ)[:, None]
    c = np.arange(2 * TQ)[None, :]
    out = []
    for r in range(CMP_CLASSES):
        qb = CMP_CLASSES + r
        i0 = _cmp_window_start(qb) * TQ - CMP_PAD
        dist = qb * TQ + a - ((i0 + c) * CMP_STRIDE + CMP_BLOCK - 1)
        out.append(np.where(dist >= 0, _bucket_np(dist), -1))
    return np.stack(out).astype(np.int32)


def _check_cmp_windows(S):
    far = _far_dist()
    for qb in range(S // TQ):
        i0 = _cmp_window_start(qb) * TQ - CMP_PAD
        s0 = qb * TQ
        assert s0 - ((i0 - 1) * CMP_STRIDE + CMP_BLOCK - 1) >= far
        assert s0 + TQ - 1 - ((i0 + 2 * TQ) * CMP_STRIDE + CMP_BLOCK - 1) < 0


def _ab_proj_kernel(x_ref, xh_ref, w_ref, cw_ref, gb_ref,
                    aq_ref, ak_ref, av_ref, bq_ref, bk_ref, bv_ref, bo_ref, g_ref,
                    pre_ref, *, tiles_per_seq):
    i = pl.program_id(0)
    tm = x_ref.shape[0]
    xb = x_ref[...].astype(BF16)
    aq_ref[...] = (_dot(xb, w_ref[:, 0:512]) * (A_DIM ** -0.5)).astype(BF16)
    ak_ref[...] = _dot(xb, w_ref[:, 512:1024]).astype(BF16)
    av_ref[...] = _dot(xb, w_ref[:, 1024:1536]).astype(BF16)
    bv_ref[...] = _dot(xb, w_ref[:, 2560:3072]).astype(BF16)
    bo_ref[...] = _dot(xb, w_ref[:, 3072:3584])
    g_ref[...] = _dot(xb, w_ref[:, 3584:AB_PAD]) + gb_ref[...]
    halo = _dot(xh_ref[...].astype(BF16), w_ref[:, 1536:2560])
    halo = jnp.where(i % tiles_per_seq == 0, 0.0, halo)
    pre_ref[0:8, :] = halo
    pre_ref[8:8 + tm, :] = _dot(xb, w_ref[:, 1536:2560])
    y = pre_ref[8:8 + tm, :] * cw_ref[B_CONV - 1:B_CONV, :]
    for k in range(B_CONV - 1):
        s = B_CONV - 1 - k
        y = y + pre_ref[8 - s:8 - s + tm, :] * cw_ref[k:k + 1, :]
    y = y / (1.0 + jnp.exp(-y))
    bq_ref[...] = (y[:, :B_W] * (B_DIM ** -0.5)).astype(BF16)
    bk_ref[...] = y[:, B_W:].astype(BF16)


def _ab_proj(x2, w_pad, conv_w, gate_b_pad, S):
    N = x2.shape[0]
    tm = TM
    tps = S // tm
    row = lambda i: (i, 0)
    fix = lambda i: (0, 0)
    outs = [jax.ShapeDtypeStruct((N, 512), BF16)] * 6 + [
        jax.ShapeDtypeStruct((N, 512), F32), jax.ShapeDtypeStruct((N, 128), F32)]
    o_specs = [pl.BlockSpec((tm, 512), row)] * 7 + [pl.BlockSpec((tm, 128), row)]
    return pl.pallas_call(
        functools.partial(_ab_proj_kernel, tiles_per_seq=tps),
        grid=(N // tm,),
        in_specs=[pl.BlockSpec((tm, D_MODEL), row),
                  pl.BlockSpec((8, D_MODEL), lambda i: (jnp.maximum(i * (tm // 8) - 1, 0), 0)),
                  pl.BlockSpec((D_MODEL, AB_PAD), fix),
                  pl.BlockSpec((B_CONV, 2 * B_W), fix),
                  pl.BlockSpec((1, 128), fix)],
        out_specs=o_specs,
        out_shape=outs,
        scratch_shapes=[pltpu.VMEM((tm + 8, 2 * B_W), F32)],
        compiler_params=_params(("parallel",)),
        name="ab_proj",
    )(x2, x2, w_pad, conv_w, gate_b_pad)


def _dilated_kernel(*refs, has_prev, is_last):
    if has_prev:
        q_ref, kp_ref, kc_ref, vp_ref, vc_ref, tab_ref, op_ref, lp_ref = refs[:8]
        outs = refs[8:]
    else:
        q_ref, kp_ref, kc_ref, vp_ref, vc_ref, tab_ref = refs[:6]
        outs = refs[6:]
    o_ref = outs[0]
    n = pl.program_id(2)
    first = jnp.where(n == 0, NEG, 0.0)
    lane = lax.broadcasted_iota(jnp.int32, (A_BLOCK, 128), 1)
    lse_tile = jnp.zeros((A_BLOCK, 128), F32)
    for h in range(A_HEADS):
        cs = slice(h * A_DIM, (h + 1) * A_DIM)
        q = q_ref[0, :, cs]
        sp = _dot_nt(q, kp_ref[0, :, cs]) + tab_ref[0, h, :, 0:A_BLOCK] + first
        sc = _dot_nt(q, kc_ref[0, :, cs]) + tab_ref[0, h, :, A_BLOCK:2 * A_BLOCK]
        m = jnp.maximum(jnp.max(sp, axis=1, keepdims=True), jnp.max(sc, axis=1, keepdims=True))
        pp = jnp.exp(sp - m)
        pc = jnp.exp(sc - m)
        s = jnp.sum(pp, axis=1, keepdims=True) + jnp.sum(pc, axis=1, keepdims=True)
        o = (_dot(pp.astype(BF16), vp_ref[0, :, cs]) + _dot(pc.astype(BF16), vc_ref[0, :, cs])) / s
        lse = m + jnp.log(s)
        if has_prev:
            lp = lp_ref[0, :, 16 * h:16 * h + 1]
            mm = jnp.maximum(lp, lse)
            wp = jnp.exp(lp - mm)
            wc = jnp.exp(lse - mm)
            tot = wp + wc
            o = (op_ref[0, :, cs] * wp + o * wc) / tot
            lse = mm + jnp.log(tot)
        o_ref[0, :, cs] = o.astype(o_ref.dtype)
        if not is_last:
            lse_tile = jnp.where(jnp.right_shift(lane, 4) == h, lse, lse_tile)
    if not is_last:
        outs[1][0] = lse_tile


def _dilated_call(q, k, v, tab, prev, pattern_idx, dilation, B, S, is_last):
    d = dilation
    L = S // d
    nb = L // A_BLOCK
    qv = q.reshape(B, L, d * A_W)
    kv = k.reshape(B, L, d * A_W)
    vv = v.reshape(B, L, d * A_W)
    cur = lambda b, r, n: (b, n, r)
    prv = lambda b, r, n: (b, jnp.maximum(n - 1, 0), r)
    blk = pl.BlockSpec((1, A_BLOCK, A_W), cur)
    in_specs = [blk, pl.BlockSpec((1, A_BLOCK, A_W), prv), blk,
                pl.BlockSpec((1, A_BLOCK, A_W), prv), blk,
                pl.BlockSpec((1, 8, A_BLOCK, 2 * A_BLOCK), lambda b, r, n: (pattern_idx, 0, 0, 0))]
    args = [qv, kv, kv, vv, vv, tab]
    has_prev = prev is not None
    if has_prev:
        in_specs += [blk, pl.BlockSpec((1, A_BLOCK, 128), cur)]
        args += [prev[0].reshape(B, L, d * A_W), prev[1].reshape(B, L, d * 128)]
    if is_last:
        out_shape = [jax.ShapeDtypeStruct((B, L, d * A_W), BF16)]
        out_specs = [blk]
    else:
        out_shape = [jax.ShapeDtypeStruct((B, L, d * A_W), F32),
                     jax.ShapeDtypeStruct((B, L, d * 128), F32)]
        out_specs = [blk, pl.BlockSpec((1, A_BLOCK, 128), cur)]
    res = pl.pallas_call(
        functools.partial(_dilated_kernel, has_prev=has_prev, is_last=is_last),
        grid=(B, d, nb),
        in_specs=in_specs, out_specs=out_specs, out_shape=out_shape,
        compiler_params=_params(("parallel", "parallel", "arbitrary")),
        name="dilated_d%d" % d,
    )(*args)
    return [r.reshape(B * S, -1) for r in res]


def _dilated_attention(aq, ak, av, tab, B, S):
    prev = None
    for p, (window, d) in enumerate(A_PATTERNS):
        assert window // d == A_BLOCK and S % (d * A_BLOCK) == 0
        last = p == len(A_PATTERNS) - 1
        prev = _dilated_call(aq, ak, av, tab, prev, p, d, B, S, last)
    return prev[0]


def _split3(x):
    hi = x.astype(BF16)
    r = x - hi.astype(F32)
    mid = r.astype(BF16)
    lo = (r - mid.astype(F32)).astype(BF16)
    return hi, mid, lo


def _mlstm_kernel(q_ref, k_ref, v_ref, g_ref, bo_ref, hg_ref, y_ref, c_ref, m_ref):
    L = B_CHUNK
    c = pl.program_id(1)

    @pl.when(c == 0)
    def _():
        c_ref[...] = jnp.zeros_like(c_ref)
        m_ref[...] = jnp.zeros_like(m_ref)

    lane = lax.broadcasted_iota(jnp.int32, (L, 128), 1)
    row = lax.broadcasted_iota(jnp.int32, (L, L), 0)
    col = lax.broadcasted_iota(jnp.int32, (L, L), 1)
    tri = row >= col
    g = g_ref[0]
    is_f = (lane >= B_HEADS) & (lane < 2 * B_HEADS)
    logf = jnp.minimum(g, 0.0) - jnp.log(1.0 + jnp.exp(-jnp.abs(g)))
    gl = jnp.where(is_f, logf, jnp.where(lane < B_HEADS, g, 0.0))
    tril = jnp.where(tri, 1.0, 0.0).astype(BF16)
    hi, mid, lo = _split3(gl)
    cum = _dot(tril, hi) + _dot(tril, mid) + _dot(tril, lo)
    cum_t = cum.T
    gl_t = gl.T
    ones = jnp.ones((L, B_DIM), BF16)
    for h in range(B_HEADS):
        cs = slice(h * B_DIM, (h + 1) * B_DIM)
        q = q_ref[0, :, cs]
        k = k_ref[0, :, cs]
        v_ext = jnp.concatenate([v_ref[0, :, cs], ones], axis=1)
        b_col = cum[:, B_HEADS + h:B_HEADS + h + 1]
        b_row = cum_t[B_HEADS + h:B_HEADS + h + 1, :]
        i_col = gl[:, h:h + 1]
        i_row = gl_t[h:h + 1, :]
        m_prev = m_ref[h, 0:1, 0:1]
        dm = jnp.where(tri, b_col - b_row + i_row, NEG)
        inter = b_col + m_prev
        m_t = jnp.maximum(inter, jnp.max(dm, axis=1, keepdims=True))
        p = jnp.exp(dm - m_t)
        sqk = _dot_nt(q, k) * p
        sc = jnp.exp(inter - m_t)
        lhs = jnp.concatenate([(sc * q.astype(F32)).astype(BF16), sqk.astype(BF16)], axis=1)
        c_ext = c_ref[h]
        rhs = jnp.concatenate([c_ext.astype(BF16), v_ext], axis=0)
        res = _dot(lhs, rhs)
        num = res[:, :B_DIM]
        den = res[:, B_DIM:]
        hh = num / jnp.maximum(jnp.abs(den), jnp.exp(-m_t))
        b_last = b_col[L - 1:L, :]
        gk = b_last - b_col + i_col
        m_new = jnp.maximum(b_last + m_prev, jnp.max(gk, axis=0, keepdims=True))
        wk = jnp.exp(gk - m_new)
        decay = jnp.exp(b_last + m_prev - m_new)
        kw_t = (wk * k.astype(F32)).T.astype(BF16)
        c_ref[h] = decay * c_ext + _dot(kw_t, v_ext)
        m_ref[h] = jnp.broadcast_to(m_new, (8, 128))
        mu = jnp.mean(hh, axis=1, keepdims=True)
        xc = hh - mu
        var = jnp.mean(xc * xc, axis=1, keepdims=True)
        hn = xc * lax.rsqrt(var + LN_EPS) * hg_ref[:, cs]
        bo = bo_ref[0, :, cs]
        y_ref[0, :, cs] = (hn / (1.0 + jnp.exp(-bo))).astype(BF16)


def _mlstm(bq, bk, bv, gates, bo, head_g, B, S):
    nc = S // B_CHUNK
    r3 = lambda a: a.reshape(B, S, a.shape[-1])
    blk = pl.BlockSpec((1, B_CHUNK, B_W), lambda b, c: (b, c, 0))
    y = pl.pallas_call(
        _mlstm_kernel,
        grid=(B, nc),
        in_specs=[blk, blk, blk,
                  pl.BlockSpec((1, B_CHUNK, 128), lambda b, c: (b, c, 0)),
                  blk,
                  pl.BlockSpec((1, B_W), lambda b, c: (0, 0))],
        out_specs=blk,
        out_shape=jax.ShapeDtypeStruct((B, S, B_W), BF16),
        scratch_shapes=[pltpu.VMEM((B_HEADS, B_DIM, 2 * B_DIM), F32),
                        pltpu.VMEM((B_HEADS, 8, 128), F32)],
        compiler_params=_params(("parallel", "arbitrary")),
        name="mlstm",
    )(r3(bq), r3(bk), r3(bv), r3(gates), r3(bo), head_g.reshape(1, B_W).astype(F32))
    return y.reshape(B * S, B_W)


def _layer_norm(z, g, b):
    mu = jnp.mean(z, axis=1, keepdims=True)
    zc = z - mu
    var = jnp.mean(zc * zc, axis=1, keepdims=True)
    return zc * lax.rsqrt(var + LN_EPS) * g + b


def _route(logits):
    col = lambda c: logits[:, c:c + 1]
    gl = [col(c) for c in range(N_GROUPS)]
    gmax = functools.reduce(jnp.maximum, gl)
    gsum = sum(jnp.exp(x - gmax) for x in gl)
    g_idx = jnp.full(gmax.shape, N_GROUPS - 1, jnp.int32)
    for c in range(N_GROUPS - 2, -1, -1):
        g_idx = jnp.where(gl[c] == gmax, c, g_idx)
    g_w = 1.0 / gsum
    el = []
    for k in range(EPG):
        x = col(N_GROUPS + (N_GROUPS - 1) * EPG + k)
        for g in range(N_GROUPS - 2, -1, -1):
            x = jnp.where(g_idx == g, col(N_GROUPS + g * EPG + k), x)
        el.append(x)
    v1 = functools.reduce(jnp.maximum, el)
    i1 = jnp.full(v1.shape, EPG - 1, jnp.int32)
    for k in range(EPG - 2, -1, -1):
        i1 = jnp.where(el[k] == v1, k, i1)
    el2 = [jnp.where(i1 == k, -jnp.inf, el[k]) for k in range(EPG)]
    v2 = functools.reduce(jnp.maximum, el2)
    i2 = jnp.full(v2.shape, EPG - 1, jnp.int32)
    for k in range(EPG - 2, -1, -1):
        i2 = jnp.where((el2[k] == v2) & (i1 != k), k, i2)
    t = jnp.exp(v2 - v1)
    w1 = g_w / (1.0 + t)
    w2 = w1 * t
    a = jnp.minimum(i1, i2)
    b = jnp.maximum(i1, i2)
    pair = jnp.where(a == 0, b - 1, jnp.where(a == 1, b + 1, 5))
    bucket = (g_idx * 6 + pair).astype(F32)
    w_lo = jnp.where(i1 < i2, w1, w2)
    w_hi = jnp.where(i1 < i2, w2, w1)
    lane = lax.broadcasted_iota(jnp.int32, logits.shape, 1)
    return jnp.where(lane == 0, bucket, jnp.where(lane == 1, w_lo, jnp.where(lane == 2, w_hi, 0.0)))


def _out_ln_route_kernel(*refs, n_in):
    y_refs = refs[:n_in]
    w_refs = refs[n_in:2 * n_in]
    x_ref, g_ref, b_ref, wrh_ref, wrl_ref, br_ref, h_ref, r_ref = refs[2 * n_in:]
    y = _dot(y_refs[0][...], w_refs[0][...])
    for i in range(1, n_in):
        y = y + _dot(y_refs[i][...], w_refs[i][...])
    hn = _layer_norm(ALPHA * x_ref[...] + y, g_ref[...], b_ref[...])
    h_ref[...] = hn
    hi = hn.astype(BF16)
    lo = (hn - hi.astype(F32)).astype(BF16)
    logits = _dot(hi, wrh_ref[...]) + _dot(lo, wrh_ref[...]) + _dot(hi, wrl_ref[...]) + br_ref[...]
    r_ref[...] = _route(logits)


def _out_ln_route(ys, ws, x2, ln_g, ln_b, wr, br):
    N = x2.shape[0]
    tm = TM
    row = lambda i: (i, 0)
    fix = lambda i: (0, 0)
    wr_hi = wr.astype(BF16)
    wr_lo = (wr - wr_hi.astype(F32)).astype(BF16)
    in_specs = ([pl.BlockSpec((tm, y.shape[1]), row) for y in ys]
                + [pl.BlockSpec(w.shape, fix) for w in ws]
                + [pl.BlockSpec((tm, D_MODEL), row),
                   pl.BlockSpec((1, D_MODEL), fix), pl.BlockSpec((1, D_MODEL), fix),
                   pl.BlockSpec((D_MODEL, 128), fix), pl.BlockSpec((D_MODEL, 128), fix),
                   pl.BlockSpec((1, 128), fix)])
    return pl.pallas_call(
        functools.partial(_out_ln_route_kernel, n_in=len(ys)),
        grid=(N // tm,),
        in_specs=in_specs,
        out_specs=[pl.BlockSpec((tm, D_MODEL), row), pl.BlockSpec((tm, 128), row)],
        out_shape=[jax.ShapeDtypeStruct((N, D_MODEL), F32), jax.ShapeDtypeStruct((N, 128), F32)],
        compiler_params=_params(("parallel",)),
        name="out_ln_route",
    )(*ys, *ws, x2, ln_g.reshape(1, -1), ln_b.reshape(1, -1), wr_hi, wr_lo, br)


def _router_weights(wr_g, br_g, wr_e, br_e):
    we = wr_e.transpose(1, 0, 2).reshape(D_MODEL, N_GROUPS * EPG)
    w = jnp.concatenate([wr_g, we], axis=1)
    w = jnp.pad(w, ((0, 0), (0, 128 - w.shape[1])))
    b = jnp.concatenate([br_g, br_e.reshape(-1)])
    b = jnp.pad(b, (0, 128 - b.shape[0])).reshape(1, 128)
    return w.astype(F32), b.astype(F32)


_PAIRS = ((0, 1), (0, 2), (0, 3), (1, 2), (1, 3), (2, 3))


def _moe_kernel(elo_ref, ehi_ref, chg_ref, nt_ref,
                x_ref, r_ref, wgl_ref, wul_ref, wdl_ref, wgh_ref, wuh_ref, wdh_ref,
                g_ref, b_ref, o_ref, wg_s, wu_s, wd_s):
    t = pl.program_id(0)

    @pl.when(chg_ref[t] == 1)
    def _():
        wg_s[0] = wgl_ref[0].astype(BF16)
        wu_s[0] = wul_ref[0].astype(BF16)
        wd_s[0] = wdl_ref[0].astype(BF16)
        wg_s[1] = wgh_ref[0].astype(BF16)
        wu_s[1] = wuh_ref[0].astype(BF16)
        wd_s[1] = wdh_ref[0].astype(BF16)

    @pl.when(t < nt_ref[0])
    def _():
        x = x_ref[...]
        xb = x.astype(BF16)
        r = r_ref[...]
        acc = None
        for e in range(2):
            a = _dot(xb, wg_s[e])
            u = _dot(xb, wu_s[e])
            hcur = (a / (1.0 + jnp.exp(-a))) * u * r[:, 1 + e:2 + e]
            y = _dot(hcur.astype(BF16), wd_s[e])
            acc = y if acc is None else acc + y
        o_ref[...] = _layer_norm(ALPHA * x + acc, g_ref[...], b_ref[...])

    @pl.when(t >= nt_ref[0])
    def _():
        o_ref[...] = jnp.zeros_like(o_ref)


def _moe(h2, route, w_gate, w_up, w_down, ln_g, ln_b):
    N = h2.shape[0]
    tm = TM
    n_tiles = N // tm + N_BUCKETS
    n_pad = n_tiles * tm
    bucket = route[:, 0].astype(jnp.int32)
    onehot = (bucket[:, None] == jnp.arange(N_BUCKETS)[None, :]).astype(jnp.int32)
    csum = jnp.cumsum(onehot, axis=0)
    counts = csum[-1]
    rank = jnp.take_along_axis(csum, bucket[:, None], axis=1)[:, 0] - 1
    padded = ((counts + tm - 1) // tm) * tm
    ends = jnp.cumsum(padded)
    offs = ends - padded
    dest = offs[bucket] + rank
    src = jnp.zeros((n_pad,), jnp.int32).at[dest].set(jnp.arange(N, dtype=jnp.int32))
    tile_start = jnp.arange(n_tiles, dtype=jnp.int32) * tm
    n_used = (ends[-1] // tm).astype(jnp.int32)
    tb = jnp.searchsorted(ends, tile_start, side="right").astype(jnp.int32)
    tb_last = jnp.take(tb, jnp.maximum(n_used - 1, 0))
    tb = jnp.where(tile_start < ends[-1], tb, tb_last)
    pairs = jnp.asarray(_PAIRS, jnp.int32)
    elo = (tb // 6) * EPG + pairs[tb % 6, 0]
    ehi = (tb // 6) * EPG + pairs[tb % 6, 1]
    chg = jnp.concatenate([jnp.ones((1,), jnp.int32), (tb[1:] != tb[:-1]).astype(jnp.int32)])
    xs = jnp.take(h2, src, axis=0)
    rs = jnp.take(route, src, axis=0)

    row = lambda t, *_: (t, 0)
    fix = lambda t, *_: (0, 0)
    wlo = lambda t, elo, ehi, chg, nt: (elo[t], 0, 0)
    whi = lambda t, elo, ehi, chg, nt: (ehi[t], 0, 0)
    up_spec = lambda im: pl.BlockSpec((1, D_MODEL, E_HID), im)
    dn_spec = lambda im: pl.BlockSpec((1, E_HID, D_MODEL), im)
    grid_spec = pltpu.PrefetchScalarGridSpec(
        num_scalar_prefetch=4,
        grid=(n_tiles,),
        in_specs=[pl.BlockSpec((tm, D_MODEL), row), pl.BlockSpec((tm, 128), row),
                  up_spec(wlo), up_spec(wlo), dn_spec(wlo),
                  up_spec(whi), up_spec(whi), dn_spec(whi),
                  pl.BlockSpec((1, D_MODEL), fix), pl.BlockSpec((1, D_MODEL), fix)],
        out_specs=pl.BlockSpec((tm, D_MODEL), row),
        scratch_shapes=[pltpu.VMEM((2, D_MODEL, E_HID), BF16),
                        pltpu.VMEM((2, D_MODEL, E_HID), BF16),
                        pltpu.VMEM((2, E_HID, D_MODEL), BF16)])
    out_sorted = pl.pallas_call(
        _moe_kernel,
        grid_spec=grid_spec,
        out_shape=jax.ShapeDtypeStruct((n_pad, D_MODEL), F32),
        compiler_params=_params(("arbitrary",)),
        name="moe",
    )(elo, ehi, chg, n_used.reshape(1), xs, rs, w_gate, w_up, w_down, w_gate, w_up, w_down,
      ln_g.reshape(1, -1), ln_b.reshape(1, -1))
    return jnp.take(out_sorted, dest, axis=0)


def _c_proj_kernel(x_ref, w_ref, gb_ref, q_ref, kc_ref, kv_ref, g_ref):
    xb = x_ref[...].astype(BF16)
    q_ref[...] = (_dot(xb, w_ref[:, 0:C_W]) * (C_DIM ** -0.5)).astype(BF16)
    for i in range(4):
        kc_ref[i] = _dot(xb, w_ref[:, C_W + i * 128:C_W + (i + 1) * 128]).astype(BF16)
    kv_ref[...] = _dot(xb, w_ref[:, C_W + 512:C_W + 1536]).astype(BF16)
    z = _dot(xb, w_ref[:, C_W + 1536:C_W + 1792]) + gb_ref[...]
    g_ref[...] = 1.0 / (1.0 + jnp.exp(-z))


def _c_proj(x2, w_pad, gb_pad):
    N = x2.shape[0]
    tm = TM
    row = lambda i: (i, 0)
    fix = lambda i: (0, 0)
    wcols = w_pad.shape[1]
    return pl.pallas_call(
        _c_proj_kernel,
        grid=(N // tm,),
        in_specs=[pl.BlockSpec((tm, D_MODEL), row), pl.BlockSpec((D_MODEL, wcols), fix),
                  pl.BlockSpec((1, 256), fix)],
        out_specs=[pl.BlockSpec((tm, C_W), row), pl.BlockSpec((4, tm, 128), lambda i: (0, i, 0)),
                   pl.BlockSpec((tm, 1024), row), pl.BlockSpec((tm, 256), row)],
        out_shape=[jax.ShapeDtypeStruct((N, C_W), BF16), jax.ShapeDtypeStruct((4, N, 128), BF16),
                   jax.ShapeDtypeStruct((N, 1024), BF16), jax.ShapeDtypeStruct((N, 256), F32)],
        compiler_params=_params(("parallel",)),
        name="c_proj",
    )(x2, w_pad, gb_pad)


def _c_weights(w_in, gate_b):
    gcols = []
    gb = []
    for g in range(C_GROUPS):
        idx = [C_PROJ - 3 * C_HEADS + br * C_HEADS + g * C_HPG + j for br in range(3) for j in range(C_HPG)]
        gcols.append(jnp.pad(w_in[:, np.asarray(idx)], ((0, 0), (0, 128 - len(idx)))))
        gb.append(jnp.pad(gate_b[np.asarray(idx) - (C_PROJ - 3 * C_HEADS)], (0, 128 - len(idx))))
    w = jnp.concatenate([w_in[:, :C_PROJ - 3 * C_HEADS]] + gcols, axis=1).astype(BF16)
    return w, jnp.concatenate(gb).reshape(1, 256).astype(F32)


def _compress_kernel(seg_ref, w1_ref, pos_ref, w1f_ref, w2_ref, o_ref):
    n_seg = seg_ref.shape[1]
    ul = _dot(seg_ref[0], w1_ref[0])
    u = ul[:, :CMP_HIDDEN]
    lnext = pltpu.roll(ul[:, CMP_HIDDEN:], n_seg - 1, 0)
    cpos = _dot(pos_ref[0], w1f_ref[0])[0:1, :]
    pre = u + lnext + cpos
    act = 0.5 * pre * (1.0 + jnp.tanh(math.sqrt(2.0 / math.pi) * (pre + 0.044715 * pre * pre * pre)))
    o_ref[0, 0, 0:CMP_PAD, :] = jnp.zeros((CMP_PAD, C_DIM), BF16)
    o_ref[0, 0, CMP_PAD:CMP_PAD + n_seg, :] = _dot(act.astype(BF16), w2_ref[0]).astype(BF16)


def _compress(kc, cmp_pos, cmp_w1, cmp_w2, B, S):
    n_seg = S // CMP_STRIDE
    half = CMP_STRIDE * C_DIM
    seg = kc.reshape(4 * B, n_seg, half)
    w1 = cmp_w1.astype(BF16)
    w1_ul = jnp.concatenate([w1[:, :half], w1[:, half:]], axis=2)
    pos = jnp.broadcast_to(cmp_pos.reshape(2, 1, CMP_BLOCK * C_DIM), (2, 8, CMP_BLOCK * C_DIM)).astype(BF16)
    out = pl.pallas_call(
        _compress_kernel,
        grid=(4, B),
        in_specs=[pl.BlockSpec((1, n_seg, half), lambda i, b: (i * B + b, 0, 0)),
                  pl.BlockSpec((1, half, 2 * CMP_HIDDEN), lambda i, b: (i // 2, 0, 0)),
                  pl.BlockSpec((1, 8, CMP_BLOCK * C_DIM), lambda i, b: (i // 2, 0, 0)),
                  pl.BlockSpec((1, CMP_BLOCK * C_DIM, CMP_HIDDEN), lambda i, b: (i // 2, 0, 0)),
                  pl.BlockSpec((1, CMP_HIDDEN, C_DIM), lambda i, b: (i // 2, 0, 0))],
        out_specs=pl.BlockSpec((1, 1, CMP_PAD + n_seg, C_DIM), lambda i, b: (i, b, 0, 0)),
        out_shape=jax.ShapeDtypeStruct((4, B, CMP_PAD + n_seg, C_DIM), BF16),
        compiler_params=_params(("parallel", "parallel")),
        name="compress",
    )(seg, w1_ul, pos, w1, cmp_w2.astype(BF16))
    return out


def _overlap_np(n_cmp_pad, n_slc):
    i = np.arange(n_cmp_pad)[:, None] - CMP_PAD
    m = np.arange(n_slc)[None, :]
    start = i * CMP_STRIDE
    ov = (start < (m + 1) * SLC_BLOCK) & (start + CMP_BLOCK - 1 >= m * SLC_BLOCK) & (i >= 0)
    return ov.astype(np.float32)


def _flash_step(s, vext, m_ref, l_ref, acc_ref, h):
    rs = slice(h * TQ, (h + 1) * TQ)
    m_prev = m_ref[rs, :]
    m_new = jnp.maximum(m_prev, jnp.max(s, axis=1, keepdims=True))
    alpha = jnp.exp(m_prev - m_new)
    p = jnp.exp(s - m_new)
    l_ref[rs, :] = alpha * l_ref[rs, :] + jnp.sum(p, axis=1, keepdims=True)
    m_ref[rs, :] = m_new
    w = vext.shape[1]
    a = alpha if w == 128 else jnp.concatenate([alpha] * (w // 128), axis=1)
    acc_ref[rs, 0:w] = a * acc_ref[rs, 0:w] + _dot(p.astype(BF16), vext)


def _nsa_kernel(q_ref, ks_ref, vs_ref, kw_ref, vw_ref, kc_ref, vc_ref, ov_ref,
                tsel_ref, twin_ref, tcmp_ref, g_ref, o_ref,
                m_ref, l_ref, acc_ref, oc_ref, os_ref, *, n_sel_delta, n_slc, top_n):
    qb = pl.program_id(2)
    H = C_HPG
    NB = ov_ref.shape[1]
    q_all = jnp.concatenate([q_ref[:, h * C_DIM:(h + 1) * C_DIM] for h in range(H)], axis=0)

    def reset():
        m_ref[...] = jnp.full(m_ref.shape, NEG, F32)
        l_ref[...] = jnp.zeros(l_ref.shape, F32)
        acc_ref[...] = jnp.zeros(acc_ref.shape, F32)

    def finish(w):
        inv = jnp.where(m_ref[...] > 0.5 * NEG, 1.0 / l_ref[...], 0.0)
        inv = inv if w == 128 else jnp.concatenate([inv] * (w // 128), axis=1)
        return acc_ref[:, 0:w] * inv

    reset()
    t0 = qb // CMP_CLASSES + jnp.where(qb % CMP_CLASSES >= CMP_SPLIT, 1, 0)
    n_ct = kc_ref.shape[2] // TQ
    t_hi = jnp.minimum(t0 + 1, n_ct - 1)

    def cmp_body(t, carry):
        r0 = pl.multiple_of(t * TQ, TQ)
        kt = kc_ref[0, 0, pl.ds(r0, TQ), :]
        vext = jnp.concatenate([vc_ref[0, 0, pl.ds(r0, TQ), :], ov_ref[pl.ds(r0, TQ), :]], axis=1)
        s_all = _dot_nt(q_all, kt)
        in_a = t == t0
        in_b = t == t0 + 1
        for h in range(H):
            bias = jnp.where(in_a, tcmp_ref[0, h, :, 0:TQ],
                             jnp.where(in_b, tcmp_ref[0, h, :, TQ:2 * TQ], 0.0))
            _flash_step(s_all[h * TQ:(h + 1) * TQ] + bias, vext, m_ref, l_ref, acc_ref, h)
        return carry

    lax.fori_loop(1, t_hi + 1, cmp_body, 0)
    res = finish(2 * C_DIM)
    oc_ref[...] = res[:, :C_DIM]
    imp = res[0:TQ, C_DIM:]
    for h in range(1, H):
        imp = imp + res[h * TQ:(h + 1) * TQ, C_DIM:]

    shift = SLC_BLOCK.bit_length() - 1
    qpos = qb * TQ + lax.broadcasted_iota(jnp.int32, (TQ, NB), 0)
    mblk = lax.broadcasted_iota(jnp.int32, (TQ, NB), 1)
    qblk = jnp.right_shift(qpos, shift)
    forced = (mblk == 0) | (mblk == qblk) | (mblk == qblk - 1)
    score = jnp.where(forced, 3e38, jnp.where(jnp.left_shift(mblk, shift) <= qpos, imp, NEG))
    score_t = score.T
    blk_t = lax.broadcasted_iota(jnp.int32, (NB, TQ), 0).astype(F32)
    sel_t = jnp.zeros((NB, TQ), F32)
    for _ in range(top_n):
        mx = jnp.max(score_t, axis=0, keepdims=True)
        idx = jnp.min(jnp.where(score_t == mx, blk_t, float(NB)), axis=0, keepdims=True)
        pick = blk_t == idx
        sel_t = jnp.where(pick, 1.0, sel_t)
        score_t = jnp.where(pick, -3e38, score_t)
    sel = sel_t.T.astype(BF16)

    reset()
    erow = lax.broadcasted_iota(jnp.int32, (NB, TQ), 0)
    ecol = jnp.right_shift(lax.broadcasted_iota(jnp.int32, (NB, TQ), 1), shift)

    def sel_body(kb, carry):
        r0 = pl.multiple_of(kb * TQ, TQ)
        kt = ks_ref[pl.ds(r0, TQ), :]
        vt = vs_ref[pl.ds(r0, TQ), :]
        expand = jnp.where(erow == kb * (TQ // SLC_BLOCK) + ecol, 1.0, 0.0).astype(BF16)
        addmask = (_dot(sel, expand) - 1.0) * (-NEG)
        s_all = _dot_nt(q_all, kt)
        dcl = jnp.minimum(qb - kb, n_sel_delta - 1)
        for h in range(H):
            bias = tsel_ref[dcl, h] + addmask
            _flash_step(s_all[h * TQ:(h + 1) * TQ] + bias, vt, m_ref, l_ref, acc_ref, h)
        return carry

    lax.fori_loop(0, qb + 1, sel_body, 0)
    os_ref[...] = finish(C_DIM)

    reset()
    n_wt = WIN // TQ

    def win_body(kb, carry):
        r0 = pl.multiple_of(kb * TQ, TQ)
        kt = kw_ref[pl.ds(r0, TQ), :]
        vt = vw_ref[pl.ds(r0, TQ), :]
        s_all = _dot_nt(q_all, kt)
        delta = qb - kb
        for h in range(H):
            _flash_step(s_all[h * TQ:(h + 1) * TQ] + twin_ref[delta, h], vt, m_ref, l_ref, acc_ref, h)
        return carry

    lax.fori_loop(jnp.maximum(qb - n_wt, 0), qb + 1, win_body, 0)
    out_w = finish(C_DIM)

    g = g_ref[...]
    for h in range(H):
        rs = slice(h * TQ, (h + 1) * TQ)
        o = (g[:, h:h + 1] * oc_ref[rs, :] + g[:, H + h:H + h + 1] * os_ref[rs, :]
             + g[:, 2 * H + h:2 * H + h + 1] * out_w[rs])
        o_ref[:, h * C_DIM:(h + 1) * C_DIM] = o.astype(BF16)


def _nsa_attention(q, kv, kvc, gates, tsel, twin, tcmp, B, S):
    N = B * S
    QT = S // TQ
    n_slc = S // SLC_BLOCK
    n_cmp_pad = kvc.shape[2]
    NB = 128
    assert n_slc <= NB and n_cmp_pad % TQ == 0
    ov = jnp.asarray(_overlap_np(n_cmp_pad, NB), BF16)
    n_sel_delta = tsel.shape[0]
    kvspec = lambda c: pl.BlockSpec((S, C_DIM), lambda b, g, t: (b, c + g))
    cspec = lambda kvi: pl.BlockSpec((1, 1, n_cmp_pad, C_DIM), lambda b, g, t: (kvi * 2 + g, b, 0, 0))
    return pl.pallas_call(
        functools.partial(_nsa_kernel, n_sel_delta=n_sel_delta, n_slc=n_slc,
                          top_n=min(SLC_TOP_N, n_slc)),
        grid=(B, C_GROUPS, QT),
        in_specs=[pl.BlockSpec((TQ, C_HPG * C_DIM), lambda b, g, t: (b * QT + t, g)),
                  kvspec(0), kvspec(2), kvspec(4), kvspec(6),
                  cspec(0), cspec(1),
                  pl.BlockSpec((n_cmp_pad, NB), lambda b, g, t: (0, 0)),
                  pl.BlockSpec((n_sel_delta, C_HPG, TQ, TQ), lambda b, g, t: (0, g, 0, 0)),
                  pl.BlockSpec((twin.shape[0], C_HPG, TQ, TQ), lambda b, g, t: (0, g, 0, 0)),
                  pl.BlockSpec((1, C_HPG, TQ, 2 * TQ), lambda b, g, t: (t % CMP_CLASSES, g, 0, 0)),
                  pl.BlockSpec((TQ, 128), lambda b, g, t: (b * QT + t, g))],
        out_specs=pl.BlockSpec((TQ, C_HPG * C_DIM), lambda b, g, t: (b * QT + t, g)),
        out_shape=jax.ShapeDtypeStruct((N, C_W), BF16),
        scratch_shapes=[pltpu.VMEM((C_HPG * TQ, 128), F32), pltpu.VMEM((C_HPG * TQ, 128), F32),
                        pltpu.VMEM((C_HPG * TQ, 2 * C_DIM), F32),
                        pltpu.VMEM((C_HPG * TQ, C_DIM), F32), pltpu.VMEM((C_HPG * TQ, C_DIM), F32)],
        compiler_params=_params(("parallel", "parallel", "arbitrary")),
        name="nsa",
    )(q, kv, kv, kv, kv, kvc, kvc, ov, tsel, twin, tcmp, gates)


def _layer_ab(h2, B, S, w_in, gate_b, conv_w, head_g, w_out, dil_tab):
    w_pad = jnp.pad(w_in, ((0, 0), (0, AB_PAD - AB_PROJ))).astype(BF16)
    gb_pad = jnp.pad(gate_b, (0, 128 - gate_b.shape[0])).reshape(1, 128).astype(F32)
    aq, ak, av, bq, bk, bv, bo, gates = _ab_proj(h2, w_pad, conv_w.astype(F32), gb_pad, S)
    ya = _dilated_attention(aq, ak, av, dil_tab, B, S)
    yb = _mlstm(bq, bk, bv, gates, bo, head_g, B, S)
    wo = w_out.astype(BF16)
    return [ya, yb], [wo[:A_W], wo[A_W:]]


def _layer_c(h2, B, S, w_in, gate_b, cmp_pos, cmp_w1, cmp_w2, w_out, tsel, twin, tcmp):
    w_pad, gb_pad = _c_weights(w_in, gate_b)
    q, kc, kv, gates = _c_proj(h2, w_pad, gb_pad)
    kvc = _compress(kc, cmp_pos, cmp_w1, cmp_w2, B, S)
    out = _nsa_attention(q, kv, kvc, gates, tsel, twin, tcmp, B, S)
    return [out], [w_out.astype(BF16)]


def kernel(x, rel_bias, ln_g, ln_b, ab_w_in, ab_gate_b, ab_conv, ab_head_norm, ab_w_out,
           c_w_in, c_gate_b, c_cmp_pos, c_cmp_w1, c_cmp_w2, c_w_out,
           moe_wr_g, moe_br_g, moe_wr_e, moe_br_e, moe_w_gate, moe_w_up, moe_w_down):
    B, S, D = x.shape
    assert D == D_MODEL and S % (TM) == 0 and S % (16 * A_BLOCK) == 0
    _check_cmp_windows(S)
    dil_tab = _bias_tables(rel_bias, _dilated_idx(), shift=False)
    tsel = _bias_tables(rel_bias, _sel_idx(), shift=True)
    twin = _bias_tables(rel_bias, _win_idx(), shift=False)
    tcmp = _bias_tables(rel_bias, _cmp_idx(), shift=True)
    h = x.reshape(B * S, D)
    for layer in range(DEPTH):
        j = layer // 2
        if layer % 2 == 0:
            ys, ws = _layer_ab(h, B, S, ab_w_in[j], ab_gate_b[j], ab_conv[j], ab_head_norm[j],
                               ab_w_out[j], dil_tab)
        else:
            ys, ws = _layer_c(h, B, S, c_w_in[j], c_gate_b[j], c_cmp_pos[j], c_cmp_w1[j], c_cmp_w2[j],
                              c_w_out[j], tsel, twin, tcmp)
        wr, br = _router_weights(moe_wr_g[layer], moe_br_g[layer], moe_wr_e[layer], moe_br_e[layer])
        h, route = _out_ln_route(ys, ws, h, ln_g[layer, 0], ln_b[layer, 0], wr, br)
        h = _moe(h, route, moe_w_gate[layer], moe_w_up[layer], moe_w_down[layer],
                 ln_g[layer, 1], ln_b[layer, 1])
    return h.reshape(B, S, D)
```

```python
import functools
import math

import numpy as np
import jax
import jax.numpy as jnp
from jax import lax
from jax.experimental import pallas as pl
from jax.experimental.pallas import tpu as pltpu

F32 = jnp.float32
BF16 = jnp.bfloat16
NEG = -1e30
LOG2E = math.log2(math.e)
VMEM_LIMIT = 48 * 1024 * 1024

D_MODEL = 1024
DEPTH = 2
ALPHA = (2.0 * DEPTH) ** 0.25
LN_EPS = 1e-5
REL_BUCKETS = 32
REL_MAX_DIST = 2048

A_HEADS, A_DIM, A_W = 8, 64, 512
A_PATTERNS = ((128, 1), (512, 4), (2048, 16))
A_BLOCK = 128
B_HEADS, B_DIM, B_W = 4, 128, 512
B_CHUNK = 128
B_CONV = 4
AB_PROJ = 3592
AB_PAD = 3712

C_HEADS, C_GROUPS, C_HPG, C_DIM, C_W = 8, 2, 4, 128, 1024
CMP_BLOCK, CMP_STRIDE, CMP_HIDDEN = 32, 16, 256
SLC_BLOCK, SLC_TOP_N, WIN = 64, 16, 512
C_PROJ = 2584
TQ = 128
CMP_PAD = 128

N_GROUPS, EPG, N_EXPERTS, E_HID = 4, 4, 16, 512
N_BUCKETS = N_GROUPS * 6
TM = 256


def _dot(a, b):
    return jnp.dot(a, b, preferred_element_type=F32)


def _dot_nt(a, b):
    return lax.dot_general(a, b, (((1,), (1,)), ((), ())), preferred_element_type=F32)


def _params(sem):
    return pltpu.CompilerParams(dimension_semantics=sem, vmem_limit_bytes=VMEM_LIMIT)


def _bucket_np(n):
    n = np.maximum(n, 0)
    exact = REL_BUCKETS // 2
    nf = np.maximum(n, 1).astype(np.float64)
    large = exact + (np.log(nf / exact) / math.log(REL_MAX_DIST / exact)
                     * (REL_BUCKETS - exact)).astype(np.int64)
    return np.where(n < exact, n, np.minimum(large, REL_BUCKETS - 1)).astype(np.int32)


def _bias_tab_kernel(tab_ref, idx_ref, out_ref, *, shift, scale):
    R = idx_ref.shape[1]
    RC = 32

    def body(i, carry):
        r0 = pl.multiple_of(i * RC, RC)
        idx = idx_ref[0, pl.ds(r0, RC), :]
        for h in range(8):
            base = tab_ref[REL_BUCKETS - 1, h] if shift else 0.0
            val = jnp.full(idx.shape, (tab_ref[0, h] - base) * scale, F32)
            for b in range(1, REL_BUCKETS):
                val = jnp.where(idx == b, (tab_ref[b, h] - base) * scale, val)
            out_ref[0, h, pl.ds(r0, RC), :] = jnp.where(idx < 0, NEG, val)
        return carry

    lax.fori_loop(0, R // RC, body, 0)


def _bias_tables(rel_bias, idx_np, shift, scale=1.0):
    T, R, C = idx_np.shape
    return pl.pallas_call(
        functools.partial(_bias_tab_kernel, shift=shift, scale=scale),
        grid=(T,),
        in_specs=[pl.BlockSpec(memory_space=pltpu.SMEM),
                  pl.BlockSpec((1, R, C), lambda t: (t, 0, 0))],
        out_specs=pl.BlockSpec((1, 8, R, C), lambda t: (t, 0, 0, 0)),
        out_shape=jax.ShapeDtypeStruct((T, 8, R, C), F32),
        compiler_params=_params(("parallel",)),
        name="bias_tables",
    )(rel_bias.astype(F32), jnp.asarray(idx_np))


def _dilated_idx():
    qi = np.arange(A_BLOCK)[:, None]
    ki = np.arange(2 * A_BLOCK)[None, :]
    j = qi + A_BLOCK - ki
    out = []
    for window, dilation in A_PATTERNS:
        nk = window // dilation
        valid = (j >= 0) & (j <= nk)
        out.append(np.where(valid, _bucket_np(np.maximum(j, 0) * dilation), -1))
    return np.stack(out).astype(np.int32)


def _sel_idx():
    a = np.arange(TQ)[:, None]
    c = np.arange(TQ)[None, :]
    n_delta = -(-(_far_dist() + TQ) // TQ)
    out = []
    for delta in range(-1, n_delta + 1):
        dist = delta * TQ + a - c
        out.append(np.where(dist >= 0, _bucket_np(dist), -1))
    return np.stack(out).astype(np.int32)


def _far_dist():
    n = np.arange(0, 4 * REL_MAX_DIST)
    b = _bucket_np(n)
    return int(np.max(n[b < REL_BUCKETS - 1])) + 1


def _win_idx():
    a = np.arange(TQ)[:, None]
    c = np.arange(TQ)[None, :]
    out = []
    for delta in range(-1, WIN // TQ + 1):
        dist = delta * TQ + a - c
        out.append(np.where((dist >= 0) & (dist < WIN), _bucket_np(dist), -1))
    return np.stack(out).astype(np.int32)


CMP_PER_TILE = TQ // CMP_STRIDE
CMP_CLASSES = TQ // CMP_PER_TILE
CMP_SPLIT = 13


def _cmp_window_start(qb):
    return qb // CMP_CLASSES + (1 if qb % CMP_CLASSES >= CMP_SPLIT else 0)


def _cmp_idx():
    a = np.arange(TQ)[:, None]
    c = np.arange(TQ)[None, :]
    out = []
    for r in range(CMP_CLASSES):
        qb = CMP_CLASSES + r
        i0 = _cmp_window_start(qb) * TQ - CMP_PAD
        for half in range(2):
            dist = qb * TQ + a - ((i0 + half * TQ + c) * CMP_STRIDE + CMP_BLOCK - 1)
            out.append(np.where(dist >= 0, _bucket_np(dist), -1))
        out.append(np.full((TQ, TQ), REL_BUCKETS - 1))
        out.append(np.full((TQ, TQ), -1))
    return np.stack(out).astype(np.int32)


def _nsa_tables(rel_bias):
    tsel = _bias_tables(rel_bias, _sel_idx(), shift=True, scale=LOG2E)
    twin = _bias_tables(rel_bias, _win_idx(), shift=False, scale=LOG2E)
    tcmp = _bias_tables(rel_bias, _cmp_idx(), shift=True, scale=LOG2E)
    return tsel, twin, tcmp.reshape(CMP_CLASSES, 4, 8, TQ, TQ)


def _check_cmp_windows(S):
    far = _far_dist()
    for qb in range(S // TQ):
        i0 = _cmp_window_start(qb) * TQ - CMP_PAD
        s0 = qb * TQ
        assert s0 - ((i0 - 1) * CMP_STRIDE + CMP_BLOCK - 1) >= far
        assert s0 + TQ - 1 - ((i0 + 2 * TQ) * CMP_STRIDE + CMP_BLOCK - 1) < 0


def _ab_proj_kernel(x_ref, xh_ref, w_ref, cw_ref, gb_ref,
                    aq_ref, ak_ref, av_ref, bq_ref, bk_ref, bv_ref, bo_ref, g_ref,
                    pre_ref, *, tiles_per_seq):
    i = pl.program_id(0)
    tm = x_ref.shape[0]
    xb = x_ref[...].astype(BF16)
    aq_ref[...] = (_dot(xb, w_ref[:, 0:512]) * (A_DIM ** -0.5)).astype(BF16)
    ak_ref[...] = _dot(xb, w_ref[:, 512:1024]).astype(BF16)
    av_ref[...] = _dot(xb, w_ref[:, 1024:1536]).astype(BF16)
    bv_ref[...] = _dot(xb, w_ref[:, 2560:3072]).astype(BF16)
    bo_ref[...] = _dot(xb, w_ref[:, 3072:3584])
    g_ref[...] = _dot(xb, w_ref[:, 3584:AB_PAD]) + gb_ref[...]
    halo = _dot(xh_ref[...].astype(BF16), w_ref[:, 1536:2560])
    halo = jnp.where(i % tiles_per_seq == 0, 0.0, halo)
    pre_ref[0:8, :] = halo
    pre_ref[8:8 + tm, :] = _dot(xb, w_ref[:, 1536:2560])
    y = pre_ref[8:8 + tm, :] * cw_ref[B_CONV - 1:B_CONV, :]
    for k in range(B_CONV - 1):
        s = B_CONV - 1 - k
        y = y + pre_ref[8 - s:8 - s + tm, :] * cw_ref[k:k + 1, :]
    y = y / (1.0 + jnp.exp(-y))
    bq_ref[...] = (y[:, :B_W] * (B_DIM ** -0.5)).astype(BF16)
    bk_ref[...] = y[:, B_W:].astype(BF16)


def _ab_proj(x2, w_pad, conv_w, gate_b_pad, S):
    N = x2.shape[0]
    tm = TM
    tps = S // tm
    row = lambda i: (i, 0)
    fix = lambda i: (0, 0)
    outs = [jax.ShapeDtypeStruct((N, 512), BF16)] * 6 + [
        jax.ShapeDtypeStruct((N, 512), F32), jax.ShapeDtypeStruct((N, 128), F32)]
    o_specs = [pl.BlockSpec((tm, 512), row)] * 7 + [pl.BlockSpec((tm, 128), row)]
    return pl.pallas_call(
        functools.partial(_ab_proj_kernel, tiles_per_seq=tps),
        grid=(N // tm,),
        in_specs=[pl.BlockSpec((tm, D_MODEL), row),
                  pl.BlockSpec((8, D_MODEL), lambda i: (jnp.maximum(i * (tm // 8) - 1, 0), 0)),
                  pl.BlockSpec((D_MODEL, AB_PAD), fix),
                  pl.BlockSpec((B_CONV, 2 * B_W), fix),
                  pl.BlockSpec((1, 128), fix)],
        out_specs=o_specs,
        out_shape=outs,
        scratch_shapes=[pltpu.VMEM((tm + 8, 2 * B_W), F32)],
        compiler_params=_params(("parallel",)),
        name="ab_proj",
    )(x2, x2, w_pad, conv_w, gate_b_pad)


def _dilated_kernel(*refs, has_prev, is_last):
    if has_prev:
        q_ref, kp_ref, kc_ref, vp_ref, vc_ref, tab_ref, op_ref, lp_ref = refs[:8]
        outs = refs[8:]
    else:
        q_ref, kp_ref, kc_ref, vp_ref, vc_ref, tab_ref = refs[:6]
        outs = refs[6:]
    o_ref = outs[0]
    n = pl.program_id(2)
    first = jnp.where(n == 0, NEG, 0.0)
    lane = lax.broadcasted_iota(jnp.int32, (A_BLOCK, 128), 1)
    keep_side = [jnp.where(lane < A_DIM, 1.0, 0.0).astype(BF16), jnp.where(lane < A_DIM, 0.0, 1.0).astype(BF16)]
    ones = jnp.ones((A_BLOCK, 128), BF16)
    m_tile = jnp.zeros((A_BLOCK, 128), F32)
    l_tile = jnp.ones((A_BLOCK, 128), F32)
    unnorm = []
    for j in range(A_HEADS // 2):
        cs = slice(j * 128, (j + 1) * 128)
        q2 = q_ref[0, :, cs]
        kp, kc, vp, vc = kp_ref[0, :, cs], kc_ref[0, :, cs], vp_ref[0, :, cs], vc_ref[0, :, cs]
        acc = None
        for side in range(2):
            h = 2 * j + side
            keep = keep_side[side]
            q = q2 * keep
            sp = _dot_nt(q, kp) + tab_ref[0, h, :, 0:A_BLOCK] + first
            sc = _dot_nt(q, kc) + tab_ref[0, h, :, A_BLOCK:2 * A_BLOCK]
            m = jnp.max(jnp.maximum(sp, sc), axis=1, keepdims=True)
            pp = jnp.exp(sp - m).astype(BF16)
            pc = jnp.exp(sc - m).astype(BF16)
            vpe = jnp.concatenate([vp * keep, ones], axis=1)
            vce = jnp.concatenate([vc * keep, ones], axis=1)
            r = _dot(pp, vpe) + _dot(pc, vce)
            acc = r[:, 0:128] if acc is None else acc + r[:, 0:128]
            m_tile = jnp.where(lane == A_DIM + h, m, m_tile)
            l_tile = jnp.where(lane == A_DIM + h, r[:, 128:256], l_tile)
        unnorm.append(acc)
    stat = (lane >= A_DIM) & (lane < A_DIM + A_HEADS)
    lse = m_tile + jnp.log(l_tile)
    if has_prev:
        lp = lp_ref[0]
        mm = jnp.maximum(lp, lse)
        wp = jnp.exp(lp - mm)
        wc = jnp.exp(lse - mm)
        tot = wp + wc
        scale_prev = jnp.where(stat, wp / tot, 0.0)
        scale_cur = jnp.where(stat, wc / (tot * l_tile), 0.0)
        lse = mm + jnp.log(tot)
    else:
        scale_cur = jnp.where(stat, 1.0 / l_tile, 0.0)
    erow = lax.broadcasted_iota(jnp.int32, (128, A_W), 0)
    ecol = lax.broadcasted_iota(jnp.int32, (128, A_W), 1)
    expand = jnp.where(erow - A_DIM == jnp.right_shift(ecol, A_DIM.bit_length() - 1), 1.0, 0.0).astype(BF16)

    def spread(t):
        hi, mid, lo = _split3(t)
        return _dot(hi, expand) + _dot(mid, expand) + _dot(lo, expand)

    o = jnp.concatenate(unnorm, axis=1) * spread(scale_cur)
    if has_prev:
        o = o + op_ref[0] * spread(scale_prev)
    o_ref[0] = o.astype(o_ref.dtype)
    if not is_last:
        outs[1][0] = jnp.where(stat, lse, 0.0)


def _dilated_call(q, k, v, tab, prev, pattern_idx, dilation, B, S, is_last):
    d = dilation
    L = S // d
    nb = L // A_BLOCK
    qv = q.reshape(B, L, d * A_W)
    kv = k.reshape(B, L, d * A_W)
    vv = v.reshape(B, L, d * A_W)
    cur = lambda b, r, n: (b, n, r)
    prv = lambda b, r, n: (b, jnp.maximum(n - 1, 0), r)
    blk = pl.BlockSpec((1, A_BLOCK, A_W), cur)
    in_specs = [blk, pl.BlockSpec((1, A_BLOCK, A_W), prv), blk,
                pl.BlockSpec((1, A_BLOCK, A_W), prv), blk,
                pl.BlockSpec((1, 8, A_BLOCK, 2 * A_BLOCK), lambda b, r, n: (pattern_idx, 0, 0, 0))]
    args = [qv, kv, kv, vv, vv, tab]
    has_prev = prev is not None
    if has_prev:
        in_specs += [blk, pl.BlockSpec((1, A_BLOCK, 128), cur)]
        args += [prev[0].reshape(B, L, d * A_W), prev[1].reshape(B, L, d * 128)]
    if is_last:
        out_shape = [jax.ShapeDtypeStruct((B, L, d * A_W), BF16)]
        out_specs = [blk]
    else:
        out_shape = [jax.ShapeDtypeStruct((B, L, d * A_W), F32),
                     jax.ShapeDtypeStruct((B, L, d * 128), F32)]
        out_specs = [blk, pl.BlockSpec((1, A_BLOCK, 128), cur)]
    res = pl.pallas_call(
        functools.partial(_dilated_kernel, has_prev=has_prev, is_last=is_last),
        grid=(B, d, nb),
        in_specs=in_specs, out_specs=out_specs, out_shape=out_shape,
        compiler_params=_params(("parallel", "parallel", "arbitrary")),
        name="dilated_d%d" % d,
    )(*args)
    return [r.reshape(B * S, -1) for r in res]


def _dilated_attention(aq, ak, av, tab, B, S):
    prev = None
    for p, (window, d) in enumerate(A_PATTERNS):
        assert window // d == A_BLOCK and S % (d * A_BLOCK) == 0
        last = p == len(A_PATTERNS) - 1
        prev = _dilated_call(aq, ak, av, tab, prev, p, d, B, S, last)
    return prev[0]


def _split3(x):
    hi = x.astype(BF16)
    r = x - hi.astype(F32)
    mid = r.astype(BF16)
    lo = (r - mid.astype(F32)).astype(BF16)
    return hi, mid, lo


def _mlstm_kernel(q_ref, k_ref, v_ref, g_ref, bo_ref, hg_ref, y_ref, c_ref, m_ref):
    L = B_CHUNK
    c = pl.program_id(1)

    @pl.when(c == 0)
    def _():
        c_ref[...] = jnp.zeros_like(c_ref)
        m_ref[...] = jnp.zeros_like(m_ref)

    lane = lax.broadcasted_iota(jnp.int32, (L, 128), 1)
    row = lax.broadcasted_iota(jnp.int32, (L, L), 0)
    col = lax.broadcasted_iota(jnp.int32, (L, L), 1)
    tri = row >= col
    g = g_ref[0]
    is_f = (lane >= B_HEADS) & (lane < 2 * B_HEADS)
    logf = jnp.minimum(g, 0.0) - jnp.log(1.0 + jnp.exp(-jnp.abs(g)))
    gl = jnp.where(is_f, logf, jnp.where(lane < B_HEADS, g, 0.0))
    tril = jnp.where(tri, 1.0, 0.0).astype(BF16)
    hi, mid, lo = _split3(gl)
    cum = _dot(tril, hi) + _dot(tril, mid) + _dot(tril, lo)
    cum_t = cum.T
    gl_t = gl.T
    ones = jnp.ones((L, B_DIM), BF16)
    for h in range(B_HEADS):
        cs = slice(h * B_DIM, (h + 1) * B_DIM)
        q = q_ref[0, :, cs]
        k = k_ref[0, :, cs]
        v_ext = jnp.concatenate([v_ref[0, :, cs], ones], axis=1)
        b_col = cum[:, B_HEADS + h:B_HEADS + h + 1]
        b_row = cum_t[B_HEADS + h:B_HEADS + h + 1, :]
        i_col = gl[:, h:h + 1]
        i_row = gl_t[h:h + 1, :]
        m_prev = m_ref[h, 0:1, 0:1]
        dm = jnp.where(tri, b_col - b_row + i_row, NEG)
        inter = b_col + m_prev
        m_t = jnp.maximum(inter, jnp.max(dm, axis=1, keepdims=True))
        p = jnp.exp(dm - m_t)
        sqk = _dot_nt(q, k) * p
        sc = jnp.exp(inter - m_t)
        lhs = jnp.concatenate([(sc * q.astype(F32)).astype(BF16), sqk.astype(BF16)], axis=1)
        c_ext = c_ref[h]
        rhs = jnp.concatenate([c_ext.astype(BF16), v_ext], axis=0)
        res = _dot(lhs, rhs)
        num = res[:, :B_DIM]
        den = res[:, B_DIM:]
        hh = num / jnp.maximum(jnp.abs(den), jnp.exp(-m_t))
        b_last = b_col[L - 1:L, :]
        gk = b_last - b_col + i_col
        m_new = jnp.maximum(b_last + m_prev, jnp.max(gk, axis=0, keepdims=True))
        wk = jnp.exp(gk - m_new)
        decay = jnp.exp(b_last + m_prev - m_new)
        kw_t = (wk * k.astype(F32)).T.astype(BF16)
        c_ref[h] = decay * c_ext + _dot(kw_t, v_ext)
        m_ref[h] = jnp.broadcast_to(m_new, (8, 128))
        mu = jnp.mean(hh, axis=1, keepdims=True)
        xc = hh - mu
        var = jnp.mean(xc * xc, axis=1, keepdims=True)
        hn = xc * lax.rsqrt(var + LN_EPS) * hg_ref[:, cs]
        bo = bo_ref[0, :, cs]
        y_ref[0, :, cs] = (hn / (1.0 + jnp.exp(-bo))).astype(BF16)


def _mlstm(bq, bk, bv, gates, bo, head_g, B, S):
    nc = S // B_CHUNK
    r3 = lambda a: a.reshape(B, S, a.shape[-1])
    blk = pl.BlockSpec((1, B_CHUNK, B_W), lambda b, c: (b, c, 0))
    y = pl.pallas_call(
        _mlstm_kernel,
        grid=(B, nc),
        in_specs=[blk, blk, blk,
                  pl.BlockSpec((1, B_CHUNK, 128), lambda b, c: (b, c, 0)),
                  blk,
                  pl.BlockSpec((1, B_W), lambda b, c: (0, 0))],
        out_specs=blk,
        out_shape=jax.ShapeDtypeStruct((B, S, B_W), BF16),
        scratch_shapes=[pltpu.VMEM((B_HEADS, B_DIM, 2 * B_DIM), F32),
                        pltpu.VMEM((B_HEADS, 8, 128), F32)],
        compiler_params=_params(("parallel", "arbitrary")),
        name="mlstm",
    )(r3(bq), r3(bk), r3(bv), r3(gates), r3(bo), head_g.reshape(1, B_W).astype(F32))
    return y.reshape(B * S, B_W)


def _layer_norm(z, g, b):
    mu = jnp.mean(z, axis=1, keepdims=True)
    zc = z - mu
    var = jnp.mean(zc * zc, axis=1, keepdims=True)
    return zc * lax.rsqrt(var + LN_EPS) * g + b


def _route(logits):
    col = lambda c: logits[:, c:c + 1]
    gl = [col(c) for c in range(N_GROUPS)]
    gmax = functools.reduce(jnp.maximum, gl)
    gsum = sum(jnp.exp(x - gmax) for x in gl)
    g_idx = jnp.full(gmax.shape, N_GROUPS - 1, jnp.int32)
    for c in range(N_GROUPS - 2, -1, -1):
        g_idx = jnp.where(gl[c] == gmax, c, g_idx)
    g_w = 1.0 / gsum
    el = []
    for k in range(EPG):
        x = col(N_GROUPS + (N_GROUPS - 1) * EPG + k)
        for g in range(N_GROUPS - 2, -1, -1):
            x = jnp.where(g_idx == g, col(N_GROUPS + g * EPG + k), x)
        el.append(x)
    v1 = functools.reduce(jnp.maximum, el)
    i1 = jnp.full(v1.shape, EPG - 1, jnp.int32)
    for k in range(EPG - 2, -1, -1):
        i1 = jnp.where(el[k] == v1, k, i1)
    el2 = [jnp.where(i1 == k, -jnp.inf, el[k]) for k in range(EPG)]
    v2 = functools.reduce(jnp.maximum, el2)
    i2 = jnp.full(v2.shape, EPG - 1, jnp.int32)
    for k in range(EPG - 2, -1, -1):
        i2 = jnp.where((el2[k] == v2) & (i1 != k), k, i2)
    t = jnp.exp(v2 - v1)
    w1 = g_w / (1.0 + t)
    w2 = w1 * t
    a = jnp.minimum(i1, i2)
    b = jnp.maximum(i1, i2)
    pair = jnp.where(a == 0, b - 1, jnp.where(a == 1, b + 1, 5))
    bucket = (g_idx * 6 + pair).astype(F32)
    w_lo = jnp.where(i1 < i2, w1, w2)
    w_hi = jnp.where(i1 < i2, w2, w1)
    lane = lax.broadcasted_iota(jnp.int32, logits.shape, 1)
    return jnp.where(lane == 0, bucket, jnp.where(lane == 1, w_lo, jnp.where(lane == 2, w_hi, 0.0)))


def _out_ln_route_kernel(*refs, n_in):
    y_refs = refs[:n_in]
    w_refs = refs[n_in:2 * n_in]
    x_ref, g_ref, b_ref, wrh_ref, wrl_ref, br_ref, h_ref, r_ref = refs[2 * n_in:]
    y = _dot(y_refs[0][...], w_refs[0][...])
    for i in range(1, n_in):
        y = y + _dot(y_refs[i][...], w_refs[i][...])
    hn = _layer_norm(ALPHA * x_ref[...] + y, g_ref[...], b_ref[...])
    h_ref[...] = hn
    hi = hn.astype(BF16)
    lo = (hn - hi.astype(F32)).astype(BF16)
    logits = _dot(hi, wrh_ref[...]) + _dot(lo, wrh_ref[...]) + _dot(hi, wrl_ref[...]) + br_ref[...]
    r_ref[...] = _route(logits)


def _out_ln_route(ys, ws, x2, ln_g, ln_b, wr, br):
    N = x2.shape[0]
    tm = TM
    row = lambda i: (i, 0)
    fix = lambda i: (0, 0)
    wr_hi = wr.astype(BF16)
    wr_lo = (wr - wr_hi.astype(F32)).astype(BF16)
    in_specs = ([pl.BlockSpec((tm, y.shape[1]), row) for y in ys]
                + [pl.BlockSpec(w.shape, fix) for w in ws]
                + [pl.BlockSpec((tm, D_MODEL), row),
                   pl.BlockSpec((1, D_MODEL), fix), pl.BlockSpec((1, D_MODEL), fix),
                   pl.BlockSpec((D_MODEL, 128), fix), pl.BlockSpec((D_MODEL, 128), fix),
                   pl.BlockSpec((1, 128), fix)])
    return pl.pallas_call(
        functools.partial(_out_ln_route_kernel, n_in=len(ys)),
        grid=(N // tm,),
        in_specs=in_specs,
        out_specs=[pl.BlockSpec((tm, D_MODEL), row), pl.BlockSpec((tm, 128), row)],
        out_shape=[jax.ShapeDtypeStruct((N, D_MODEL), F32), jax.ShapeDtypeStruct((N, 128), F32)],
        compiler_params=_params(("parallel",)),
        name="out_ln_route",
    )(*ys, *ws, x2, ln_g.reshape(1, -1), ln_b.reshape(1, -1), wr_hi, wr_lo, br)


def _router_weights(wr_g, br_g, wr_e, br_e):
    we = wr_e.transpose(1, 0, 2).reshape(D_MODEL, N_GROUPS * EPG)
    w = jnp.concatenate([wr_g, we], axis=1)
    w = jnp.pad(w, ((0, 0), (0, 128 - w.shape[1])))
    b = jnp.concatenate([br_g, br_e.reshape(-1)])
    b = jnp.pad(b, (0, 128 - b.shape[0])).reshape(1, 128)
    return w.astype(F32), b.astype(F32)


_PAIRS = ((0, 1), (0, 2), (0, 3), (1, 2), (1, 3), (2, 3))


def _moe_kernel(elo_ref, ehi_ref, chg_ref, nt_ref,
                x_ref, r_ref, wgl_ref, wul_ref, wdl_ref, wgh_ref, wuh_ref, wdh_ref,
                g_ref, b_ref, o_ref, wg_s, wu_s, wd_s):
    t = pl.program_id(0)

    @pl.when(chg_ref[t] == 1)
    def _():
        wg_s[0] = wgl_ref[0, 0].astype(BF16)
        wu_s[0] = wul_ref[0, 0].astype(BF16)
        wd_s[0] = wdl_ref[0, 0].astype(BF16)
        wg_s[1] = wgh_ref[0, 0].astype(BF16)
        wu_s[1] = wuh_ref[0, 0].astype(BF16)
        wd_s[1] = wdh_ref[0, 0].astype(BF16)

    @pl.when(t < nt_ref[0])
    def _():
        x = x_ref[...]
        xb = x.astype(BF16)
        r = r_ref[...]
        acc = None
        for e in range(2):
            a = _dot(xb, wg_s[e])
            u = _dot(xb, wu_s[e])
            hcur = (a / (1.0 + jnp.exp(-a))) * u * r[:, 1 + e:2 + e]
            y = _dot(hcur.astype(BF16), wd_s[e])
            acc = y if acc is None else acc + y
        o_ref[...] = _layer_norm(ALPHA * x + acc, g_ref[...], b_ref[...])

    @pl.when(t >= nt_ref[0])
    def _():
        o_ref[...] = jnp.zeros_like(o_ref)


def _moe(h2, route, layer, w_gate, w_up, w_down, ln_g, ln_b):
    N = h2.shape[0]
    tm = TM
    n_tiles = N // tm + N_BUCKETS
    n_pad = n_tiles * tm
    bucket = route[:, 0].astype(jnp.int32)
    onehot = (bucket[:, None] == jnp.arange(N_BUCKETS)[None, :]).astype(jnp.int32)
    csum = jnp.cumsum(onehot, axis=0)
    counts = csum[-1]
    rank = jnp.take_along_axis(csum, bucket[:, None], axis=1)[:, 0] - 1
    padded = ((counts + tm - 1) // tm) * tm
    ends = jnp.cumsum(padded)
    offs = ends - padded
    dest = offs[bucket] + rank
    src = jnp.zeros((n_pad,), jnp.int32).at[dest].set(jnp.arange(N, dtype=jnp.int32))
    tile_start = jnp.arange(n_tiles, dtype=jnp.int32) * tm
    n_used = (ends[-1] // tm).astype(jnp.int32)
    tb = jnp.searchsorted(ends, tile_start, side="right").astype(jnp.int32)
    tb_last = jnp.take(tb, jnp.maximum(n_used - 1, 0))
    tb = jnp.where(tile_start < ends[-1], tb, tb_last)
    pairs = jnp.asarray(_PAIRS, jnp.int32)
    elo = (tb // 6) * EPG + pairs[tb % 6, 0]
    ehi = (tb // 6) * EPG + pairs[tb % 6, 1]
    chg = jnp.concatenate([jnp.ones((1,), jnp.int32), (tb[1:] != tb[:-1]).astype(jnp.int32)])
    xs = jnp.take(h2, src, axis=0)
    rs = jnp.take(route, src, axis=0)

    row = lambda t, *_: (t, 0)
    fix = lambda t, *_: (0, 0)
    wlo = lambda t, elo, ehi, chg, nt: (layer, elo[t], 0, 0)
    whi = lambda t, elo, ehi, chg, nt: (layer, ehi[t], 0, 0)
    up_spec = lambda im: pl.BlockSpec((1, 1, D_MODEL, E_HID), im)
    dn_spec = lambda im: pl.BlockSpec((1, 1, E_HID, D_MODEL), im)
    grid_spec = pltpu.PrefetchScalarGridSpec(
        num_scalar_prefetch=4,
        grid=(n_tiles,),
        in_specs=[pl.BlockSpec((tm, D_MODEL), row), pl.BlockSpec((tm, 128), row),
                  up_spec(wlo), up_spec(wlo), dn_spec(wlo),
                  up_spec(whi), up_spec(whi), dn_spec(whi),
                  pl.BlockSpec((1, D_MODEL), fix), pl.BlockSpec((1, D_MODEL), fix)],
        out_specs=pl.BlockSpec((tm, D_MODEL), row),
        scratch_shapes=[pltpu.VMEM((2, D_MODEL, E_HID), BF16),
                        pltpu.VMEM((2, D_MODEL, E_HID), BF16),
                        pltpu.VMEM((2, E_HID, D_MODEL), BF16)])
    out_sorted = pl.pallas_call(
        _moe_kernel,
        grid_spec=grid_spec,
        out_shape=jax.ShapeDtypeStruct((n_pad, D_MODEL), F32),
        compiler_params=_params(("arbitrary",)),
        name="moe",
    )(elo, ehi, chg, n_used.reshape(1), xs, rs, w_gate, w_up, w_down, w_gate, w_up, w_down,
      ln_g.reshape(1, -1), ln_b.reshape(1, -1))
    return jnp.take(out_sorted, dest, axis=0)


def _c_proj_kernel(x_ref, w_ref, gb_ref, q_ref, kc_ref, kv_ref, g_ref):
    xb = x_ref[...].astype(BF16)
    q_ref[...] = (_dot(xb, w_ref[:, 0:C_W]) * (C_DIM ** -0.5 * LOG2E)).astype(BF16)
    for i in range(4):
        kc_ref[i] = _dot(xb, w_ref[:, C_W + i * 128:C_W + (i + 1) * 128]).astype(BF16)
    kv_ref[...] = _dot(xb, w_ref[:, C_W + 512:C_W + 1536]).astype(BF16)
    z = _dot(xb, w_ref[:, C_W + 1536:C_W + 1792]) + gb_ref[...]
    g_ref[...] = 1.0 / (1.0 + jnp.exp(-z))


def _c_proj(x2, w_pad, gb_pad):
    N = x2.shape[0]
    tm = TM
    row = lambda i: (i, 0)
    fix = lambda i: (0, 0)
    wcols = w_pad.shape[1]
    return pl.pallas_call(
        _c_proj_kernel,
        grid=(N // tm,),
        in_specs=[pl.BlockSpec((tm, D_MODEL), row), pl.BlockSpec((D_MODEL, wcols), fix),
                  pl.BlockSpec((1, 256), fix)],
        out_specs=[pl.BlockSpec((tm, C_W), row), pl.BlockSpec((4, tm, 128), lambda i: (0, i, 0)),
                   pl.BlockSpec((tm, 1024), row), pl.BlockSpec((tm, 256), row)],
        out_shape=[jax.ShapeDtypeStruct((N, C_W), BF16), jax.ShapeDtypeStruct((4, N, 128), BF16),
                   jax.ShapeDtypeStruct((N, 1024), BF16), jax.ShapeDtypeStruct((N, 256), F32)],
        compiler_params=_params(("parallel",)),
        name="c_proj",
    )(x2, w_pad, gb_pad)


def _c_weights(w_in, gate_b):
    gcols = []
    gb = []
    for g in range(C_GROUPS):
        idx = [C_PROJ - 3 * C_HEADS + br * C_HEADS + g * C_HPG + j for br in range(3) for j in range(C_HPG)]
        gcols.append(jnp.pad(w_in[:, np.asarray(idx)], ((0, 0), (0, 128 - len(idx)))))
        gb.append(jnp.pad(gate_b[np.asarray(idx) - (C_PROJ - 3 * C_HEADS)], (0, 128 - len(idx))))
    w = jnp.concatenate([w_in[:, :C_PROJ - 3 * C_HEADS]] + gcols, axis=1).astype(BF16)
    return w, jnp.concatenate(gb).reshape(1, 256).astype(F32)


def _compress_kernel(seg_ref, w1_ref, pos_ref, w1f_ref, w2_ref, o_ref):
    n_seg = seg_ref.shape[1]
    ul = _dot(seg_ref[0], w1_ref[0])
    u = ul[:, :CMP_HIDDEN]
    lnext = pltpu.roll(ul[:, CMP_HIDDEN:], n_seg - 1, 0)
    cpos = _dot(pos_ref[0], w1f_ref[0])[0:1, :]
    pre = u + lnext + cpos
    act = 0.5 * pre * (1.0 + jnp.tanh(math.sqrt(2.0 / math.pi) * (pre + 0.044715 * pre * pre * pre)))
    o_ref[0, 0, 0:CMP_PAD, :] = jnp.zeros((CMP_PAD, C_DIM), BF16)
    o_ref[0, 0, CMP_PAD:CMP_PAD + n_seg, :] = _dot(act.astype(BF16), w2_ref[0]).astype(BF16)


def _compress(kc, cmp_pos, cmp_w1, cmp_w2, B, S):
    n_seg = S // CMP_STRIDE
    half = CMP_STRIDE * C_DIM
    seg = kc.reshape(4 * B, n_seg, half)
    w1 = cmp_w1.astype(BF16)
    w1_ul = jnp.concatenate([w1[:, :half], w1[:, half:]], axis=2)
    pos = jnp.broadcast_to(cmp_pos.reshape(2, 1, CMP_BLOCK * C_DIM), (2, 8, CMP_BLOCK * C_DIM)).astype(BF16)
    out = pl.pallas_call(
        _compress_kernel,
        grid=(4, B),
        in_specs=[pl.BlockSpec((1, n_seg, half), lambda i, b: (i * B + b, 0, 0)),
                  pl.BlockSpec((1, half, 2 * CMP_HIDDEN), lambda i, b: (i // 2, 0, 0)),
                  pl.BlockSpec((1, 8, CMP_BLOCK * C_DIM), lambda i, b: (i // 2, 0, 0)),
                  pl.BlockSpec((1, CMP_BLOCK * C_DIM, CMP_HIDDEN), lambda i, b: (i // 2, 0, 0)),
                  pl.BlockSpec((1, CMP_HIDDEN, C_DIM), lambda i, b: (i // 2, 0, 0))],
        out_specs=pl.BlockSpec((1, 1, CMP_PAD + n_seg, C_DIM), lambda i, b: (i, b, 0, 0)),
        out_shape=jax.ShapeDtypeStruct((4, B, CMP_PAD + n_seg, C_DIM), BF16),
        compiler_params=_params(("parallel", "parallel")),
        name="compress",
    )(seg, w1_ul, pos, w1, cmp_w2.astype(BF16))
    return out


def _overlap_np(n_cmp_pad, n_slc):
    i = np.arange(n_cmp_pad)[:, None] - CMP_PAD
    m = np.arange(n_slc)[None, :]
    start = i * CMP_STRIDE
    ov = (start < (m + 1) * SLC_BLOCK) & (start + CMP_BLOCK - 1 >= m * SLC_BLOCK) & (i >= 0)
    return ov.astype(np.float32)


def _flash_update(s_heads, vext, m_ref, acc_ref, p_ref):
    W = s_heads[0].shape[1]
    nw = W // 128
    alphas = []
    for h, s in enumerate(s_heads):
        rs = slice(h * TQ, (h + 1) * TQ)
        m_prev = m_ref[rs, :]
        smax = functools.reduce(jnp.maximum, [s[:, i * 128:(i + 1) * 128] for i in range(nw)])
        m_new = jnp.maximum(m_prev, jnp.max(smax, axis=1, keepdims=True))
        alpha = jnp.exp2(m_prev - m_new)
        p = jnp.exp2(s - (m_new if nw == 1 else jnp.concatenate([m_new] * nw, axis=1)))
        m_ref[rs, :] = m_new
        p_ref[rs, 0:W] = p.astype(BF16)
        alphas.append(alpha)
    a = jnp.concatenate(alphas, axis=0)
    acc_ref[...] = jnp.concatenate([a, a], axis=1) * acc_ref[...] + _dot(p_ref[:, 0:W], vext)


def _nsa_kernel(q_ref, ks_ref, vs_ref, kw_ref, vw_ref, kc_ref, vc_ref, ov_ref,
                tsel_ref, twin_ref, tcmp_ref, g_ref, o_ref,
                m_ref, acc_ref, p_ref, pw_ref, sa_ref, sb_ref, oc_ref, ow_ref, *, n_sel_tab, top_n):
    qb = pl.program_id(2)
    H = C_HPG
    NB = ov_ref.shape[1]
    q_all = jnp.concatenate([q_ref[:, h * C_DIM:(h + 1) * C_DIM] for h in range(H)], axis=0)
    heads = lambda x: [x[h * TQ:(h + 1) * TQ] for h in range(H)]

    def reset():
        m_ref[...] = jnp.full(m_ref.shape, NEG, F32)
        acc_ref[...] = jnp.zeros(acc_ref.shape, F32)

    def softmax_rows(s):
        nw = s.shape[1] // 128
        smax = functools.reduce(jnp.maximum, [s[:, i * 128:(i + 1) * 128] for i in range(nw)])
        m = jnp.broadcast_to(jnp.max(smax, axis=1, keepdims=True), (TQ, 128))
        return jnp.exp2(s - jnp.concatenate([m] * nw, axis=1)), m

    t0 = qb // CMP_CLASSES + jnp.where(qb % CMP_CLASSES >= CMP_SPLIT, 1, 0)
    n_ct = kc_ref.shape[2] // TQ
    wc = n_ct * TQ
    tile_kind = [3] + [jnp.where(t == t0, 0, jnp.where(t == t0 + 1, 1, jnp.where(t < t0, 2, 3)))
                       for t in range(1, n_ct)]
    s_c = _dot_nt(q_all, kc_ref[0, 0])
    vext_c = jnp.concatenate([vc_ref[0, 0], ov_ref[...]], axis=1)
    inv_c = []
    for h in range(H):
        bias = jnp.concatenate([tcmp_ref[0, tile_kind[t], h] for t in range(n_ct)], axis=1)
        p, m = softmax_rows(s_c[h * TQ:(h + 1) * TQ] + bias)
        l = jnp.sum(p, axis=1, keepdims=True)
        inv_c.append(jnp.where(m > 0.5 * NEG, 1.0 / l, 0.0))
        p_ref[h * TQ:(h + 1) * TQ, 0:wc] = p.astype(BF16)
    res_c = _dot(p_ref[:, 0:wc], vext_c)
    imp = None
    for h in range(H):
        r = res_c[h * TQ:(h + 1) * TQ] * jnp.concatenate([inv_c[h], inv_c[h]], axis=1)
        oc_ref[h * TQ:(h + 1) * TQ, :] = r[:, 0:C_DIM]
        imp = r[:, C_DIM:] if imp is None else imp + r[:, C_DIM:]

    n_wt = WIN // TQ + 1
    ww = n_wt * TQ
    st = jnp.maximum(qb - (n_wt - 1), 0)
    r0w = pl.multiple_of(st * TQ, TQ)
    s_w = _dot_nt(q_all, kw_ref[pl.ds(r0w, ww), :])
    vext_w = jnp.concatenate([vw_ref[pl.ds(r0w, ww), :], jnp.ones((ww, C_DIM), BF16)], axis=1)
    widx = [jnp.maximum(qb - (st + c) + 1, 0) for c in range(n_wt)]
    for h in range(H):
        bias = jnp.concatenate([twin_ref[i, h] for i in widx], axis=1)
        p, _ = softmax_rows(s_w[h * TQ:(h + 1) * TQ] + bias)
        pw_ref[h * TQ:(h + 1) * TQ, :] = p.astype(BF16)
    res_w = _dot(pw_ref[...], vext_w)
    ow_ref[...] = res_w[:, 0:C_DIM] / res_w[:, C_DIM:]

    shift = SLC_BLOCK.bit_length() - 1
    qpos = qb * TQ + lax.broadcasted_iota(jnp.int32, (TQ, NB), 0)
    mblk = lax.broadcasted_iota(jnp.int32, (TQ, NB), 1)
    qblk = jnp.right_shift(qpos, shift)
    forced = (mblk == 0) | (mblk == qblk) | (mblk == qblk - 1)
    score = jnp.where(forced, 3e38, jnp.where(jnp.left_shift(mblk, shift) <= qpos, imp, NEG))
    score_t = score.T
    blk_t = lax.broadcasted_iota(jnp.int32, (NB, TQ), 0).astype(F32)
    sel_t = jnp.zeros((NB, TQ), F32)
    for _ in range(top_n):
        mx = jnp.max(score_t, axis=0, keepdims=True)
        idx = jnp.min(jnp.where(score_t == mx, blk_t, float(NB)), axis=0, keepdims=True)
        pick = blk_t == idx
        sel_t = jnp.where(pick, 1.0, sel_t)
        score_t = jnp.where(pick, -3e38, score_t)
    sel = sel_t.T

    reset()
    NT = 4
    TK = NT * TQ
    bpk = TK // SLC_BLOCK
    erow = lax.broadcasted_iota(jnp.int32, (NB, TK), 0)
    ecol = jnp.right_shift(lax.broadcasted_iota(jnp.int32, (NB, TK), 1), shift)
    expand0 = jnp.where(erow == ecol, 1.0, 0.0).astype(BF16)
    ones_k = jnp.ones((TK, C_DIM), BF16)
    n_steps = qb // NT + 1

    def sel_logits(kq, s_ref, near):
        kc = jnp.minimum(kq, n_steps - 1)
        r0 = pl.multiple_of(kc * TK, TK)
        s = _dot_nt(q_all, ks_ref[pl.ds(r0, TK), :])
        sel_k = pltpu.roll(sel, (NB - kc * bpk) % NB, 1).astype(BF16)
        addmask = _dot(sel_k, expand0) * (-NEG) + NEG
        if near:
            addmask = jnp.where(kq < n_steps, addmask, NEG)
            idx = [jnp.clip(qb - (NT * kc + c) + 1, 0, n_sel_tab - 1) for c in range(NT)]
        for h in range(H):
            bias = addmask
            if near:
                bias = bias + jnp.concatenate([tsel_ref[i, h] for i in idx], axis=1)
            s_ref[h * TQ:(h + 1) * TQ, :] = s[h * TQ:(h + 1) * TQ] + bias

    def sel_consume(kq, s_ref):
        r0 = pl.multiple_of(jnp.minimum(kq, n_steps - 1) * TK, TK)
        vext = jnp.concatenate([vs_ref[pl.ds(r0, TK), :], ones_k], axis=1)
        _flash_update([s_ref[h * TQ:(h + 1) * TQ, :] for h in range(H)], vext, m_ref, acc_ref, p_ref)

    def sel_run(k_lo, n2, near):
        @pl.when(n2 > 0)
        def _():
            sel_logits(k_lo, sa_ref, near)

        def body(j, carry):
            k = k_lo + 2 * j
            sel_logits(k + 1, sb_ref, near)
            sel_consume(k, sa_ref)
            sel_logits(k + 2, sa_ref, near)
            sel_consume(k + 1, sb_ref)
            return carry

        lax.fori_loop(0, n2, body, 0)

    n_far = jnp.maximum((qb + 1 - (n_sel_tab - 2)) // NT, 0)
    far2 = n_far // 2
    sel_run(0, far2, near=False)
    sel_run(2 * far2, (n_steps - 2 * far2 + 1) // 2, near=True)
    g = g_ref[...]
    for h in range(H):
        rs = slice(h * TQ, (h + 1) * TQ)
        out_s = acc_ref[rs, 0:C_DIM] / acc_ref[rs, C_DIM:2 * C_DIM]
        o = (g[:, h:h + 1] * oc_ref[rs, :] + g[:, H + h:H + h + 1] * out_s
             + g[:, 2 * H + h:2 * H + h + 1] * ow_ref[rs, :])
        o_ref[:, h * C_DIM:(h + 1) * C_DIM] = o.astype(BF16)


def _nsa_attention(q, kv, kvc, gates, tsel, twin, tcmp, B, S):
    N = B * S
    QT = S // TQ
    n_slc = S // SLC_BLOCK
    n_cmp_pad = kvc.shape[2]
    NB = 128
    assert n_slc <= NB and n_cmp_pad % TQ == 0 and QT % 4 == 0
    ov = jnp.asarray(_overlap_np(n_cmp_pad, NB), BF16)
    n_sel_delta = tsel.shape[0]
    kvspec = lambda c: pl.BlockSpec((S, C_DIM), lambda b, g, t: (b, c + g))
    cspec = lambda kvi: pl.BlockSpec((1, 1, n_cmp_pad, C_DIM), lambda b, g, t: (kvi * 2 + g, b, 0, 0))
    return pl.pallas_call(
        functools.partial(_nsa_kernel, n_sel_tab=n_sel_delta, top_n=min(SLC_TOP_N, n_slc)),
        grid=(B, C_GROUPS, QT),
        in_specs=[pl.BlockSpec((TQ, C_HPG * C_DIM), lambda b, g, t: (b * QT + t, g)),
                  kvspec(0), kvspec(2), kvspec(4), kvspec(6),
                  cspec(0), cspec(1),
                  pl.BlockSpec((n_cmp_pad, NB), lambda b, g, t: (0, 0)),
                  pl.BlockSpec((n_sel_delta, C_HPG, TQ, TQ), lambda b, g, t: (0, g, 0, 0)),
                  pl.BlockSpec((twin.shape[0], C_HPG, TQ, TQ), lambda b, g, t: (0, g, 0, 0)),
                  pl.BlockSpec((1, 4, C_HPG, TQ, TQ), lambda b, g, t: (t % CMP_CLASSES, 0, g, 0, 0)),
                  pl.BlockSpec((TQ, 128), lambda b, g, t: (b * QT + t, g))],
        out_specs=pl.BlockSpec((TQ, C_HPG * C_DIM), lambda b, g, t: (b * QT + t, g)),
        out_shape=jax.ShapeDtypeStruct((N, C_W), BF16),
        scratch_shapes=[pltpu.VMEM((C_HPG * TQ, 128), F32),
                        pltpu.VMEM((C_HPG * TQ, 2 * C_DIM), F32),
                        pltpu.VMEM((C_HPG * TQ, max(n_cmp_pad, 4 * TQ)), BF16),
                        pltpu.VMEM((C_HPG * TQ, WIN + TQ), BF16),
                        pltpu.VMEM((C_HPG * TQ, 4 * TQ), F32), pltpu.VMEM((C_HPG * TQ, 4 * TQ), F32),
                        pltpu.VMEM((C_HPG * TQ, C_DIM), F32), pltpu.VMEM((C_HPG * TQ, C_DIM), F32)],
        compiler_params=_params(("parallel", "parallel", "arbitrary")),
        name="nsa",
    )(q, kv, kv, kv, kv, kvc, kvc, ov, tsel, twin, tcmp, gates)


def _layer_ab(h2, B, S, w_in, gate_b, conv_w, head_g, w_out, dil_tab):
    w_pad = jnp.pad(w_in, ((0, 0), (0, AB_PAD - AB_PROJ))).astype(BF16)
    gb_pad = jnp.pad(gate_b, (0, 128 - gate_b.shape[0])).reshape(1, 128).astype(F32)
    aq, ak, av, bq, bk, bv, bo, gates = _ab_proj(h2, w_pad, conv_w.astype(F32), gb_pad, S)
    ya = _dilated_attention(aq, ak, av, dil_tab, B, S)
    yb = _mlstm(bq, bk, bv, gates, bo, head_g, B, S)
    wo = w_out.astype(BF16)
    return [ya, yb], [wo[:A_W], wo[A_W:]]


def _layer_c(h2, B, S, w_in, gate_b, cmp_pos, cmp_w1, cmp_w2, w_out, tsel, twin, tcmp):
    w_pad, gb_pad = _c_weights(w_in, gate_b)
    q, kc, kv, gates = _c_proj(h2, w_pad, gb_pad)
    kvc = _compress(kc, cmp_pos, cmp_w1, cmp_w2, B, S)
    out = _nsa_attention(q, kv, kvc, gates, tsel, twin, tcmp, B, S)
    return [out], [w_out.astype(BF16)]


def kernel(x, rel_bias, ln_g, ln_b, ab_w_in, ab_gate_b, ab_conv, ab_head_norm, ab_w_out,
           c_w_in, c_gate_b, c_cmp_pos, c_cmp_w1, c_cmp_w2, c_w_out,
           moe_wr_g, moe_br_g, moe_wr_e, moe_br_e, moe_w_gate, moe_w_up, moe_w_down):
    B, S, D = x.shape
    assert D == D_MODEL and S % (TM) == 0 and S % (16 * A_BLOCK) == 0
    _check_cmp_windows(S)
    dil_tab = _bias_tables(rel_bias, _dilated_idx(), shift=False)
    tsel, twin, tcmp = _nsa_tables(rel_bias)
    h = x.reshape(B * S, D)
    for layer in range(DEPTH):
        j = layer // 2
        if layer % 2 == 0:
            ys, ws = _layer_ab(h, B, S, ab_w_in[j], ab_gate_b[j], ab_conv[j], ab_head_norm[j],
                               ab_w_out[j], dil_tab)
        else:
            ys, ws = _layer_c(h, B, S, c_w_in[j], c_gate_b[j], c_cmp_pos[j], c_cmp_w1[j], c_cmp_w2[j],
                              c_w_out[j], tsel, twin, tcmp)
        wr, br = _router_weights(moe_wr_g[layer], moe_br_g[layer], moe_wr_e[layer], moe_br_e[layer])
        h, route = _out_ln_route(ys, ws, h, ln_g[layer, 0], ln_b[layer, 0], wr, br)
        h = _moe(h, route, layer, moe_w_gate, moe_w_up, moe_w_down, ln_g[layer, 1], ln_b[layer, 1])
    return h.reshape(B, S, D)
```

```python
import functools
import math

import numpy as np
import jax
import jax.numpy as jnp
from jax import lax
from jax.experimental import pallas as pl
from jax.experimental.pallas import tpu as pltpu

F32 = jnp.float32
BF16 = jnp.bfloat16
NEG = -1e30
LOG2E = math.log2(math.e)
VMEM_LIMIT = 48 * 1024 * 1024

D_MODEL = 1024
DEPTH = 2
ALPHA = (2.0 * DEPTH) ** 0.25
LN_EPS = 1e-5
REL_BUCKETS = 32
REL_MAX_DIST = 2048

A_HEADS, A_DIM, A_W = 8, 64, 512
A_PATTERNS = ((128, 1), (512, 4), (2048, 16))
A_BLOCK = 128
B_HEADS, B_DIM, B_W = 4, 128, 512
B_CHUNK = 128
B_CONV = 4
AB_PROJ = 3592
AB_PAD = 3712

C_HEADS, C_GROUPS, C_HPG, C_DIM, C_W = 8, 2, 4, 128, 1024
CMP_BLOCK, CMP_STRIDE, CMP_HIDDEN = 32, 16, 256
SLC_BLOCK, SLC_TOP_N, WIN = 64, 16, 512
C_PROJ = 2584
TQ = 128
CMP_PAD = 128

N_GROUPS, EPG, N_EXPERTS, E_HID = 4, 4, 16, 512
N_BUCKETS = N_GROUPS * 6
TM = 256
MOE_CHUNKS = 2


def _dot(a, b):
    return jnp.dot(a, b, preferred_element_type=F32)


def _dot_nt(a, b):
    return lax.dot_general(a, b, (((1,), (1,)), ((), ())), preferred_element_type=F32)


def _params(sem):
    return pltpu.CompilerParams(dimension_semantics=sem, vmem_limit_bytes=VMEM_LIMIT)


def _bucket_np(n):
    n = np.maximum(n, 0)
    exact = REL_BUCKETS // 2
    nf = np.maximum(n, 1).astype(np.float64)
    large = exact + (np.log(nf / exact) / math.log(REL_MAX_DIST / exact)
                     * (REL_BUCKETS - exact)).astype(np.int64)
    return np.where(n < exact, n, np.minimum(large, REL_BUCKETS - 1)).astype(np.int32)


def _bias_tab_kernel(tab_ref, idx_ref, out_ref, *, shift, scale):
    R = idx_ref.shape[1]
    RC = 32

    def body(i, carry):
        r0 = pl.multiple_of(i * RC, RC)
        idx = idx_ref[0, pl.ds(r0, RC), :]
        for h in range(8):
            base = tab_ref[REL_BUCKETS - 1, h] if shift else 0.0
            val = jnp.full(idx.shape, (tab_ref[0, h] - base) * scale, F32)
            for b in range(1, REL_BUCKETS):
                val = jnp.where(idx == b, (tab_ref[b, h] - base) * scale, val)
            out_ref[0, h, pl.ds(r0, RC), :] = jnp.where(idx < 0, NEG, val)
        return carry

    lax.fori_loop(0, R // RC, body, 0)


def _bias_tables(rel_bias, idx_np, shift, scale=1.0):
    T, R, C = idx_np.shape
    return pl.pallas_call(
        functools.partial(_bias_tab_kernel, shift=shift, scale=scale),
        grid=(T,),
        in_specs=[pl.BlockSpec(memory_space=pltpu.SMEM),
                  pl.BlockSpec((1, R, C), lambda t: (t, 0, 0))],
        out_specs=pl.BlockSpec((1, 8, R, C), lambda t: (t, 0, 0, 0)),
        out_shape=jax.ShapeDtypeStruct((T, 8, R, C), F32),
        compiler_params=_params(("parallel",)),
        name="bias_tables",
    )(rel_bias.astype(F32), jnp.asarray(idx_np))


def _dilated_idx():
    qi = np.arange(A_BLOCK)[:, None]
    ki = np.arange(2 * A_BLOCK)[None, :]
    j = qi + A_BLOCK - ki
    out = []
    for window, dilation in A_PATTERNS:
        nk = window // dilation
        valid = (j >= 0) & (j <= nk)
        out.append(np.where(valid, _bucket_np(np.maximum(j, 0) * dilation), -1))
    return np.stack(out).astype(np.int32)


def _sel_idx():
    a = np.arange(TQ)[:, None]
    c = np.arange(TQ)[None, :]
    n_delta = -(-(_far_dist() + TQ) // TQ)
    out = []
    for delta in range(-1, n_delta + 1):
        dist = delta * TQ + a - c
        out.append(np.where(dist >= 0, _bucket_np(dist), -1))
    return np.stack(out).astype(np.int32)


def _far_dist():
    n = np.arange(0, 4 * REL_MAX_DIST)
    b = _bucket_np(n)
    return int(np.max(n[b < REL_BUCKETS - 1])) + 1


def _win_idx():
    a = np.arange(TQ)[:, None]
    c = np.arange(TQ)[None, :]
    out = []
    for delta in range(-1, WIN // TQ + 1):
        dist = delta * TQ + a - c
        out.append(np.where((dist >= 0) & (dist < WIN), _bucket_np(dist), -1))
    return np.stack(out).astype(np.int32)


CMP_PER_TILE = TQ // CMP_STRIDE
CMP_CLASSES = TQ // CMP_PER_TILE
CMP_SPLIT = 13


def _cmp_window_start(qb):
    return qb // CMP_CLASSES + (1 if qb % CMP_CLASSES >= CMP_SPLIT else 0)


def _cmp_idx():
    a = np.arange(TQ)[:, None]
    c = np.arange(TQ)[None, :]
    out = []
    for r in range(CMP_CLASSES):
        qb = CMP_CLASSES + r
        i0 = _cmp_window_start(qb) * TQ - CMP_PAD
        for half in range(2):
            dist = qb * TQ + a - ((i0 + half * TQ + c) * CMP_STRIDE + CMP_BLOCK - 1)
            out.append(np.where(dist >= 0, _bucket_np(dist), -1))
        out.append(np.full((TQ, TQ), REL_BUCKETS - 1))
        out.append(np.full((TQ, TQ), -1))
    return np.stack(out).astype(np.int32)


def _nsa_tables(rel_bias):
    tsel = _bias_tables(rel_bias, _sel_idx(), shift=True, scale=LOG2E)
    twin = _bias_tables(rel_bias, _win_idx(), shift=False, scale=LOG2E)
    tcmp = _bias_tables(rel_bias, _cmp_idx(), shift=True, scale=LOG2E)
    return tsel, twin, tcmp.reshape(CMP_CLASSES, 4, 8, TQ, TQ)


def _check_cmp_windows(S):
    far = _far_dist()
    for qb in range(S // TQ):
        i0 = _cmp_window_start(qb) * TQ - CMP_PAD
        s0 = qb * TQ
        assert s0 - ((i0 - 1) * CMP_STRIDE + CMP_BLOCK - 1) >= far
        assert s0 + TQ - 1 - ((i0 + 2 * TQ) * CMP_STRIDE + CMP_BLOCK - 1) < 0


def _ab_proj_kernel(x_ref, xh_ref, w_ref, cw_ref, gb_ref,
                    aq_ref, ak_ref, av_ref, bq_ref, bk_ref, bv_ref, bo_ref, g_ref,
                    pre_ref, *, tiles_per_seq):
    i = pl.program_id(0)
    tm = x_ref.shape[0]
    xb = x_ref[...].astype(BF16)
    aq_ref[...] = (_dot(xb, w_ref[:, 0:512]) * (A_DIM ** -0.5)).astype(BF16)
    ak_ref[...] = _dot(xb, w_ref[:, 512:1024]).astype(BF16)
    av_ref[...] = _dot(xb, w_ref[:, 1024:1536]).astype(BF16)
    bv_ref[...] = _dot(xb, w_ref[:, 2560:3072]).astype(BF16)
    bo_ref[...] = _dot(xb, w_ref[:, 3072:3584])
    g_ref[...] = _dot(xb, w_ref[:, 3584:AB_PAD]) + gb_ref[...]
    halo = _dot(xh_ref[...].astype(BF16), w_ref[:, 1536:2560])
    halo = jnp.where(i % tiles_per_seq == 0, 0.0, halo)
    pre_ref[0:8, :] = halo
    pre_ref[8:8 + tm, :] = _dot(xb, w_ref[:, 1536:2560])
    y = pre_ref[8:8 + tm, :] * cw_ref[B_CONV - 1:B_CONV, :]
    for k in range(B_CONV - 1):
        s = B_CONV - 1 - k
        y = y + pre_ref[8 - s:8 - s + tm, :] * cw_ref[k:k + 1, :]
    y = y / (1.0 + jnp.exp(-y))
    bq_ref[...] = (y[:, :B_W] * (B_DIM ** -0.5)).astype(BF16)
    bk_ref[...] = y[:, B_W:].astype(BF16)


def _ab_proj(x2, w_pad, conv_w, gate_b_pad, S):
    N = x2.shape[0]
    tm = TM
    tps = S // tm
    row = lambda i: (i, 0)
    fix = lambda i: (0, 0)
    outs = [jax.ShapeDtypeStruct((N, 512), BF16)] * 6 + [
        jax.ShapeDtypeStruct((N, 512), F32), jax.ShapeDtypeStruct((N, 128), F32)]
    o_specs = [pl.BlockSpec((tm, 512), row)] * 7 + [pl.BlockSpec((tm, 128), row)]
    return pl.pallas_call(
        functools.partial(_ab_proj_kernel, tiles_per_seq=tps),
        grid=(N // tm,),
        in_specs=[pl.BlockSpec((tm, D_MODEL), row),
                  pl.BlockSpec((8, D_MODEL), lambda i: (jnp.maximum(i * (tm // 8) - 1, 0), 0)),
                  pl.BlockSpec((D_MODEL, AB_PAD), fix),
                  pl.BlockSpec((B_CONV, 2 * B_W), fix),
                  pl.BlockSpec((1, 128), fix)],
        out_specs=o_specs,
        out_shape=outs,
        scratch_shapes=[pltpu.VMEM((tm + 8, 2 * B_W), F32)],
        compiler_params=_params(("parallel",)),
        name="ab_proj",
    )(x2, x2, w_pad, conv_w, gate_b_pad)


def _dilated_kernel(*refs, has_prev, is_last):
    if has_prev:
        q_ref, kp_ref, kc_ref, vp_ref, vc_ref, tab_ref, op_ref, lp_ref = refs[:8]
        outs = refs[8:]
    else:
        q_ref, kp_ref, kc_ref, vp_ref, vc_ref, tab_ref = refs[:6]
        outs = refs[6:]
    o_ref = outs[0]
    n = pl.program_id(2)
    first = jnp.where(n == 0, NEG, 0.0)
    lane = lax.broadcasted_iota(jnp.int32, (A_BLOCK, 128), 1)
    keep_side = [jnp.where(lane < A_DIM, 1.0, 0.0).astype(BF16), jnp.where(lane < A_DIM, 0.0, 1.0).astype(BF16)]
    ones = jnp.ones((A_BLOCK, 128), BF16)
    m_tile = jnp.zeros((A_BLOCK, 128), F32)
    l_tile = jnp.ones((A_BLOCK, 128), F32)
    unnorm = []
    for j in range(A_HEADS // 2):
        cs = slice(j * 128, (j + 1) * 128)
        q2 = q_ref[0, :, cs]
        kp, kc, vp, vc = kp_ref[0, :, cs], kc_ref[0, :, cs], vp_ref[0, :, cs], vc_ref[0, :, cs]
        acc = None
        for side in range(2):
            h = 2 * j + side
            keep = keep_side[side]
            q = q2 * keep
            sp = _dot_nt(q, kp) + tab_ref[0, h, :, 0:A_BLOCK] + first
            sc = _dot_nt(q, kc) + tab_ref[0, h, :, A_BLOCK:2 * A_BLOCK]
            m = jnp.max(jnp.maximum(sp, sc), axis=1, keepdims=True)
            pp = jnp.exp(sp - m).astype(BF16)
            pc = jnp.exp(sc - m).astype(BF16)
            vpe = jnp.concatenate([vp * keep, ones], axis=1)
            vce = jnp.concatenate([vc * keep, ones], axis=1)
            r = _dot(pp, vpe) + _dot(pc, vce)
            acc = r[:, 0:128] if acc is None else acc + r[:, 0:128]
            m_tile = jnp.where(lane == A_DIM + h, m, m_tile)
            l_tile = jnp.where(lane == A_DIM + h, r[:, 128:256], l_tile)
        unnorm.append(acc)
    stat = (lane >= A_DIM) & (lane < A_DIM + A_HEADS)
    lse = m_tile + jnp.log(l_tile)
    if has_prev:
        lp = lp_ref[0]
        mm = jnp.maximum(lp, lse)
        wp = jnp.exp(lp - mm)
        wc = jnp.exp(lse - mm)
        tot = wp + wc
        scale_prev = jnp.where(stat, wp / tot, 0.0)
        scale_cur = jnp.where(stat, wc / (tot * l_tile), 0.0)
        lse = mm + jnp.log(tot)
    else:
        scale_cur = jnp.where(stat, 1.0 / l_tile, 0.0)
    erow = lax.broadcasted_iota(jnp.int32, (128, A_W), 0)
    ecol = lax.broadcasted_iota(jnp.int32, (128, A_W), 1)
    expand = jnp.where(erow - A_DIM == jnp.right_shift(ecol, A_DIM.bit_length() - 1), 1.0, 0.0).astype(BF16)

    def spread(t):
        hi, mid, lo = _split3(t)
        return _dot(hi, expand) + _dot(mid, expand) + _dot(lo, expand)

    o = jnp.concatenate(unnorm, axis=1) * spread(scale_cur)
    if has_prev:
        o = o + op_ref[0] * spread(scale_prev)
    o_ref[0] = o.astype(o_ref.dtype)
    if not is_last:
        outs[1][0] = jnp.where(stat, lse, 0.0)


def _dilated_call(q, k, v, tab, prev, pattern_idx, dilation, B, S, is_last):
    d = dilation
    L = S // d
    nb = L // A_BLOCK
    qv = q.reshape(B, L, d * A_W)
    kv = k.reshape(B, L, d * A_W)
    vv = v.reshape(B, L, d * A_W)
    cur = lambda b, r, n: (b, n, r)
    prv = lambda b, r, n: (b, jnp.maximum(n - 1, 0), r)
    blk = pl.BlockSpec((1, A_BLOCK, A_W), cur)
    in_specs = [blk, pl.BlockSpec((1, A_BLOCK, A_W), prv), blk,
                pl.BlockSpec((1, A_BLOCK, A_W), prv), blk,
                pl.BlockSpec((1, 8, A_BLOCK, 2 * A_BLOCK), lambda b, r, n: (pattern_idx, 0, 0, 0))]
    args = [qv, kv, kv, vv, vv, tab]
    has_prev = prev is not None
    if has_prev:
        in_specs += [blk, pl.BlockSpec((1, A_BLOCK, 128), cur)]
        args += [prev[0].reshape(B, L, d * A_W), prev[1].reshape(B, L, d * 128)]
    if is_last:
        out_shape = [jax.ShapeDtypeStruct((B, L, d * A_W), BF16)]
        out_specs = [blk]
    else:
        out_shape = [jax.ShapeDtypeStruct((B, L, d * A_W), F32),
                     jax.ShapeDtypeStruct((B, L, d * 128), F32)]
        out_specs = [blk, pl.BlockSpec((1, A_BLOCK, 128), cur)]
    res = pl.pallas_call(
        functools.partial(_dilated_kernel, has_prev=has_prev, is_last=is_last),
        grid=(B, d, nb),
        in_specs=in_specs, out_specs=out_specs, out_shape=out_shape,
        compiler_params=_params(("parallel", "parallel", "arbitrary")),
        name="dilated_d%d" % d,
    )(*args)
    return [r.reshape(B * S, -1) for r in res]


def _dilated_attention(aq, ak, av, tab, B, S):
    prev = None
    for p, (window, d) in enumerate(A_PATTERNS):
        assert window // d == A_BLOCK and S % (d * A_BLOCK) == 0
        last = p == len(A_PATTERNS) - 1
        prev = _dilated_call(aq, ak, av, tab, prev, p, d, B, S, last)
    return prev[0]


def _split3(x):
    hi = x.astype(BF16)
    r = x - hi.astype(F32)
    mid = r.astype(BF16)
    lo = (r - mid.astype(F32)).astype(BF16)
    return hi, mid, lo


def _mlstm_kernel(q_ref, k_ref, v_ref, g_ref, bo_ref, hg_ref, y_ref, c_ref, m_ref):
    L = B_CHUNK
    c = pl.program_id(1)

    @pl.when(c == 0)
    def _():
        c_ref[...] = jnp.zeros_like(c_ref)
        m_ref[...] = jnp.zeros_like(m_ref)

    lane = lax.broadcasted_iota(jnp.int32, (L, 128), 1)
    row = lax.broadcasted_iota(jnp.int32, (L, L), 0)
    col = lax.broadcasted_iota(jnp.int32, (L, L), 1)
    tri = row >= col
    g = g_ref[0]
    is_f = (lane >= B_HEADS) & (lane < 2 * B_HEADS)
    logf = jnp.minimum(g, 0.0) - jnp.log(1.0 + jnp.exp(-jnp.abs(g)))
    gl = jnp.where(is_f, logf, jnp.where(lane < B_HEADS, g, 0.0))
    tril = jnp.where(tri, 1.0, 0.0).astype(BF16)
    hi, mid, lo = _split3(gl)
    cum = _dot(tril, hi) + _dot(tril, mid) + _dot(tril, lo)
    cum_t = cum.T
    gl_t = gl.T
    ones = jnp.ones((L, B_DIM), BF16)
    for h in range(B_HEADS):
        cs = slice(h * B_DIM, (h + 1) * B_DIM)
        q = q_ref[0, :, cs]
        k = k_ref[0, :, cs]
        v_ext = jnp.concatenate([v_ref[0, :, cs], ones], axis=1)
        b_col = cum[:, B_HEADS + h:B_HEADS + h + 1]
        b_row = cum_t[B_HEADS + h:B_HEADS + h + 1, :]
        i_col = gl[:, h:h + 1]
        i_row = gl_t[h:h + 1, :]
        m_prev = m_ref[h, 0:1, 0:1]
        dm = jnp.where(tri, b_col - b_row + i_row, NEG)
        inter = b_col + m_prev
        m_t = jnp.maximum(inter, jnp.max(dm, axis=1, keepdims=True))
        p = jnp.exp(dm - m_t)
        sqk = _dot_nt(q, k) * p
        sc = jnp.exp(inter - m_t)
        lhs = jnp.concatenate([(sc * q.astype(F32)).astype(BF16), sqk.astype(BF16)], axis=1)
        c_ext = c_ref[h]
        rhs = jnp.concatenate([c_ext.astype(BF16), v_ext], axis=0)
        res = _dot(lhs, rhs)
        num = res[:, :B_DIM]
        den = res[:, B_DIM:]
        hh = num / jnp.maximum(jnp.abs(den), jnp.exp(-m_t))
        b_last = b_col[L - 1:L, :]
        gk = b_last - b_col + i_col
        m_new = jnp.maximum(b_last + m_prev, jnp.max(gk, axis=0, keepdims=True))
        wk = jnp.exp(gk - m_new)
        decay = jnp.exp(b_last + m_prev - m_new)
        kw_t = (wk * k.astype(F32)).T.astype(BF16)
        c_ref[h] = decay * c_ext + _dot(kw_t, v_ext)
        m_ref[h] = jnp.broadcast_to(m_new, (8, 128))
        mu = jnp.mean(hh, axis=1, keepdims=True)
        xc = hh - mu
        var = jnp.mean(xc * xc, axis=1, keepdims=True)
        hn = xc * lax.rsqrt(var + LN_EPS) * hg_ref[:, cs]
        bo = bo_ref[0, :, cs]
        y_ref[0, :, cs] = (hn / (1.0 + jnp.exp(-bo))).astype(BF16)


def _mlstm(bq, bk, bv, gates, bo, head_g, B, S):
    nc = S // B_CHUNK
    r3 = lambda a: a.reshape(B, S, a.shape[-1])
    blk = pl.BlockSpec((1, B_CHUNK, B_W), lambda b, c: (b, c, 0))
    y = pl.pallas_call(
        _mlstm_kernel,
        grid=(B, nc),
        in_specs=[blk, blk, blk,
                  pl.BlockSpec((1, B_CHUNK, 128), lambda b, c: (b, c, 0)),
                  blk,
                  pl.BlockSpec((1, B_W), lambda b, c: (0, 0))],
        out_specs=blk,
        out_shape=jax.ShapeDtypeStruct((B, S, B_W), BF16),
        scratch_shapes=[pltpu.VMEM((B_HEADS, B_DIM, 2 * B_DIM), F32),
                        pltpu.VMEM((B_HEADS, 8, 128), F32)],
        compiler_params=_params(("parallel", "arbitrary")),
        name="mlstm",
    )(r3(bq), r3(bk), r3(bv), r3(gates), r3(bo), head_g.reshape(1, B_W).astype(F32))
    return y.reshape(B * S, B_W)


def _layer_norm(z, g, b):
    mu = jnp.mean(z, axis=1, keepdims=True)
    zc = z - mu
    var = jnp.mean(zc * zc, axis=1, keepdims=True)
    return zc * lax.rsqrt(var + LN_EPS) * g + b


def _route(logits, cnt_ref):
    tm = logits.shape[0]
    lt = logits.T
    col = lambda c: lt[c:c + 1, :]
    gl = [col(c) for c in range(N_GROUPS)]
    gmax = functools.reduce(jnp.maximum, gl)
    gsum = sum(jnp.exp(x - gmax) for x in gl)
    g_idx = jnp.full(gmax.shape, N_GROUPS - 1, jnp.int32)
    for c in range(N_GROUPS - 2, -1, -1):
        g_idx = jnp.where(gl[c] == gmax, c, g_idx)
    g_w = 1.0 / gsum
    el = []
    for k in range(EPG):
        x = col(N_GROUPS + (N_GROUPS - 1) * EPG + k)
        for g in range(N_GROUPS - 2, -1, -1):
            x = jnp.where(g_idx == g, col(N_GROUPS + g * EPG + k), x)
        el.append(x)
    v1 = functools.reduce(jnp.maximum, el)
    i1 = jnp.full(v1.shape, EPG - 1, jnp.int32)
    for k in range(EPG - 2, -1, -1):
        i1 = jnp.where(el[k] == v1, k, i1)
    el2 = [jnp.where(i1 == k, -jnp.inf, el[k]) for k in range(EPG)]
    v2 = functools.reduce(jnp.maximum, el2)
    i2 = jnp.full(v2.shape, EPG - 1, jnp.int32)
    for k in range(EPG - 2, -1, -1):
        i2 = jnp.where((el2[k] == v2) & (i1 != k), k, i2)
    t = jnp.exp(v2 - v1)
    w1 = g_w / (1.0 + t)
    w2 = w1 * t
    a = jnp.minimum(i1, i2)
    b = jnp.maximum(i1, i2)
    pair = jnp.where(a == 0, b - 1, jnp.where(a == 1, b + 1, 5))
    bucket = (g_idx * 6 + pair).astype(F32)
    w_lo = jnp.where(i1 < i2, w1, w2)
    w_hi = jnp.where(i1 < i2, w2, w1)
    sub = lax.broadcasted_iota(jnp.int32, (128, tm), 0)
    onehot_t = jnp.where(sub.astype(F32) == bucket, 1.0, 0.0)
    srow = lax.broadcasted_iota(jnp.int32, (tm, tm), 0)
    scol = lax.broadcasted_iota(jnp.int32, (tm, tm), 1)
    before = jnp.where(srow < scol, 1.0, 0.0).astype(BF16)
    oh = onehot_t.astype(BF16)
    carry = cnt_ref[...]
    prior = _dot(oh, before) + jnp.concatenate([carry] * (tm // 128), axis=1)
    rank = jnp.sum(onehot_t * prior, axis=0, keepdims=True)
    cnt_ref[...] = carry + _dot(oh, jnp.ones((tm, 128), BF16))
    out_t = jnp.where(sub == 0, bucket, jnp.where(sub == 1, w_lo, jnp.where(sub == 2, w_hi,
                      jnp.where(sub == 3, rank, 0.0))))
    return out_t.T


def _out_ln_route_kernel(*refs, n_in):
    y_refs = refs[:n_in]
    w_refs = refs[n_in:2 * n_in]
    x_ref, g_ref, b_ref, wrh_ref, wrl_ref, br_ref, h_ref, c_ref, cnt_ref = refs[2 * n_in:]

    @pl.when(pl.program_id(0) == 0)
    def _():
        cnt_ref[...] = jnp.zeros_like(cnt_ref)

    y = _dot(y_refs[0][...], w_refs[0][...])
    for i in range(1, n_in):
        y = y + _dot(y_refs[i][...], w_refs[i][...])
    hn = _layer_norm(ALPHA * x_ref[...] + y, g_ref[...], b_ref[...])
    h_ref[:, 0:D_MODEL] = hn
    hi = hn.astype(BF16)
    lo = (hn - hi.astype(F32)).astype(BF16)
    logits = _dot(hi, wrh_ref[...]) + _dot(lo, wrh_ref[...]) + _dot(hi, wrl_ref[...]) + br_ref[...]
    h_ref[:, D_MODEL:D_MODEL + 128] = _route(logits, cnt_ref)
    c_ref[...] = cnt_ref[...]


def _out_ln_route(ys, ws, x2, ln_g, ln_b, wr, br, row_start, N):
    tm = TM
    off = row_start // tm
    row = lambda i: (i + off, 0)
    fix = lambda i: (0, 0)
    wr_hi = wr.astype(BF16)
    wr_lo = (wr - wr_hi.astype(F32)).astype(BF16)
    in_specs = ([pl.BlockSpec((tm, y.shape[1]), row) for y in ys]
                + [pl.BlockSpec(w.shape, fix) for w in ws]
                + [pl.BlockSpec((tm, D_MODEL), row),
                   pl.BlockSpec((1, D_MODEL), fix), pl.BlockSpec((1, D_MODEL), fix),
                   pl.BlockSpec((D_MODEL, 128), fix), pl.BlockSpec((D_MODEL, 128), fix),
                   pl.BlockSpec((1, 128), fix)])
    return pl.pallas_call(
        functools.partial(_out_ln_route_kernel, n_in=len(ys)),
        grid=(N // tm,),
        in_specs=in_specs,
        out_specs=[pl.BlockSpec((tm, D_MODEL + 128), lambda i: (i, 0)), pl.BlockSpec((128, 128), fix)],
        out_shape=[jax.ShapeDtypeStruct((N, D_MODEL + 128), F32), jax.ShapeDtypeStruct((128, 128), F32)],
        scratch_shapes=[pltpu.VMEM((128, 128), F32)],
        compiler_params=_params(("arbitrary",)),
        name="out_ln_route",
    )(*ys, *ws, x2, ln_g.reshape(1, -1), ln_b.reshape(1, -1), wr_hi, wr_lo, br)


def _router_weights(wr_g, br_g, wr_e, br_e):
    we = wr_e.transpose(1, 0, 2).reshape(D_MODEL, N_GROUPS * EPG)
    w = jnp.concatenate([wr_g, we], axis=1)
    w = jnp.pad(w, ((0, 0), (0, 128 - w.shape[1])))
    b = jnp.concatenate([br_g, br_e.reshape(-1)])
    b = jnp.pad(b, (0, 128 - b.shape[0])).reshape(1, 128)
    return w.astype(F32), b.astype(F32)


_PAIRS = ((0, 1), (0, 2), (0, 3), (1, 2), (1, 3), (2, 3))


def _moe_kernel(elo_ref, ehi_ref, chg_ref, nt_ref,
                x_ref, wgl_ref, wul_ref, wdl_ref, wgh_ref, wuh_ref, wdh_ref,
                g_ref, b_ref, o_ref, wg_s, wu_s, wd_s):
    t = pl.program_id(0)

    @pl.when(chg_ref[t] == 1)
    def _():
        wg_s[0] = wgl_ref[0, 0].astype(BF16)
        wu_s[0] = wul_ref[0, 0].astype(BF16)
        wd_s[0] = wdl_ref[0, 0].astype(BF16)
        wg_s[1] = wgh_ref[0, 0].astype(BF16)
        wu_s[1] = wuh_ref[0, 0].astype(BF16)
        wd_s[1] = wdh_ref[0, 0].astype(BF16)

    @pl.when(t < nt_ref[0])
    def _():
        x = x_ref[:, 0:D_MODEL]
        xb = x.astype(BF16)
        r = x_ref[:, D_MODEL:D_MODEL + 128]
        acc = None
        for e in range(2):
            a = _dot(xb, wg_s[e])
            u = _dot(xb, wu_s[e])
            hcur = (a / (1.0 + jnp.exp(-a))) * u * r[:, 1 + e:2 + e]
            y = _dot(hcur.astype(BF16), wd_s[e])
            acc = y if acc is None else acc + y
        o_ref[...] = _layer_norm(ALPHA * x + acc, g_ref[...], b_ref[...])

    @pl.when(t >= nt_ref[0])
    def _():
        o_ref[...] = jnp.zeros_like(o_ref)


def _moe(hx, cnt, layer, w_gate, w_up, w_down, ln_g, ln_b):
    N = hx.shape[0]
    tm = TM
    n_tiles = N // tm + N_BUCKETS
    n_pad = n_tiles * tm
    bucket = hx[:, D_MODEL].astype(jnp.int32)
    rank = hx[:, D_MODEL + 3].astype(jnp.int32)
    counts = cnt[:N_BUCKETS, 0].astype(jnp.int32)
    padded = ((counts + tm - 1) // tm) * tm
    ends = jnp.cumsum(padded)
    offs = ends - padded
    dest = offs[bucket] + rank
    src = jnp.zeros((n_pad,), jnp.int32).at[dest].set(jnp.arange(N, dtype=jnp.int32))
    tile_start = jnp.arange(n_tiles, dtype=jnp.int32) * tm
    n_used = (ends[-1] // tm).astype(jnp.int32)
    tb = jnp.searchsorted(ends, tile_start, side="right").astype(jnp.int32)
    tb_last = jnp.take(tb, jnp.maximum(n_used - 1, 0))
    tb = jnp.where(tile_start < ends[-1], tb, tb_last)
    pairs = jnp.asarray(_PAIRS, jnp.int32)
    elo = (tb // 6) * EPG + pairs[tb % 6, 0]
    ehi = (tb // 6) * EPG + pairs[tb % 6, 1]
    chg = jnp.concatenate([jnp.ones((1,), jnp.int32), (tb[1:] != tb[:-1]).astype(jnp.int32)])
    xs = jnp.take(hx, src, axis=0)

    row = lambda t, *_: (t, 0)
    fix = lambda t, *_: (0, 0)
    wlo = lambda t, elo, ehi, chg, nt: (layer, elo[t], 0, 0)
    whi = lambda t, elo, ehi, chg, nt: (layer, ehi[t], 0, 0)
    up_spec = lambda im: pl.BlockSpec((1, 1, D_MODEL, E_HID), im)
    dn_spec = lambda im: pl.BlockSpec((1, 1, E_HID, D_MODEL), im)
    grid_spec = pltpu.PrefetchScalarGridSpec(
        num_scalar_prefetch=4,
        grid=(n_tiles,),
        in_specs=[pl.BlockSpec((tm, D_MODEL + 128), row),
                  up_spec(wlo), up_spec(wlo), dn_spec(wlo),
                  up_spec(whi), up_spec(whi), dn_spec(whi),
                  pl.BlockSpec((1, D_MODEL), fix), pl.BlockSpec((1, D_MODEL), fix)],
        out_specs=pl.BlockSpec((tm, D_MODEL), row),
        scratch_shapes=[pltpu.VMEM((2, D_MODEL, E_HID), BF16),
                        pltpu.VMEM((2, D_MODEL, E_HID), BF16),
                        pltpu.VMEM((2, E_HID, D_MODEL), BF16)])
    out_sorted = pl.pallas_call(
        _moe_kernel,
        grid_spec=grid_spec,
        out_shape=jax.ShapeDtypeStruct((n_pad, D_MODEL), F32),
        compiler_params=_params(("arbitrary",)),
        name="moe",
    )(elo, ehi, chg, n_used.reshape(1), xs, w_gate, w_up, w_down, w_gate, w_up, w_down,
      ln_g.reshape(1, -1), ln_b.reshape(1, -1))
    return jnp.take(out_sorted, dest, axis=0)


def _c_proj_kernel(x_ref, w_ref, gb_ref, q_ref, kc_ref, kv_ref, g_ref):
    xb = x_ref[...].astype(BF16)
    q_ref[...] = (_dot(xb, w_ref[:, 0:C_W]) * (C_DIM ** -0.5 * LOG2E)).astype(BF16)
    for i in range(4):
        kc_ref[i] = _dot(xb, w_ref[:, C_W + i * 128:C_W + (i + 1) * 128]).astype(BF16)
    kv_ref[...] = _dot(xb, w_ref[:, C_W + 512:C_W + 1536]).astype(BF16)
    z = _dot(xb, w_ref[:, C_W + 1536:C_W + 1792]) + gb_ref[...]
    g_ref[...] = 1.0 / (1.0 + jnp.exp(-z))


def _c_proj(x2, w_pad, gb_pad):
    N = x2.shape[0]
    tm = TM
    row = lambda i: (i, 0)
    fix = lambda i: (0, 0)
    wcols = w_pad.shape[1]
    return pl.pallas_call(
        _c_proj_kernel,
        grid=(N // tm,),
        in_specs=[pl.BlockSpec((tm, D_MODEL), row), pl.BlockSpec((D_MODEL, wcols), fix),
                  pl.BlockSpec((1, 256), fix)],
        out_specs=[pl.BlockSpec((tm, C_W), row), pl.BlockSpec((4, tm, 128), lambda i: (0, i, 0)),
                   pl.BlockSpec((tm, 1024), row), pl.BlockSpec((tm, 256), row)],
        out_shape=[jax.ShapeDtypeStruct((N, C_W), BF16), jax.ShapeDtypeStruct((4, N, 128), BF16),
                   jax.ShapeDtypeStruct((N, 1024), BF16), jax.ShapeDtypeStruct((N, 256), F32)],
        compiler_params=_params(("parallel",)),
        name="c_proj",
    )(x2, w_pad, gb_pad)


def _c_weights(w_in, gate_b):
    gcols = []
    gb = []
    for g in range(C_GROUPS):
        idx = [C_PROJ - 3 * C_HEADS + br * C_HEADS + g * C_HPG + j for br in range(3) for j in range(C_HPG)]
        gcols.append(jnp.pad(w_in[:, np.asarray(idx)], ((0, 0), (0, 128 - len(idx)))))
        gb.append(jnp.pad(gate_b[np.asarray(idx) - (C_PROJ - 3 * C_HEADS)], (0, 128 - len(idx))))
    w = jnp.concatenate([w_in[:, :C_PROJ - 3 * C_HEADS]] + gcols, axis=1).astype(BF16)
    return w, jnp.concatenate(gb).reshape(1, 256).astype(F32)


def _compress_kernel(seg_ref, w1_ref, pos_ref, w1f_ref, w2_ref, o_ref):
    n_seg = seg_ref.shape[1]
    ul = _dot(seg_ref[0], w1_ref[0])
    u = ul[:, :CMP_HIDDEN]
    lnext = pltpu.roll(ul[:, CMP_HIDDEN:], n_seg - 1, 0)
    cpos = _dot(pos_ref[0], w1f_ref[0])[0:1, :]
    pre = u + lnext + cpos
    act = 0.5 * pre * (1.0 + jnp.tanh(math.sqrt(2.0 / math.pi) * (pre + 0.044715 * pre * pre * pre)))
    o_ref[0, 0, 0:CMP_PAD, :] = jnp.zeros((CMP_PAD, C_DIM), BF16)
    o_ref[0, 0, CMP_PAD:CMP_PAD + n_seg, :] = _dot(act.astype(BF16), w2_ref[0]).astype(BF16)


def _compress(kc, cmp_pos, cmp_w1, cmp_w2, B, S):
    n_seg = S // CMP_STRIDE
    half = CMP_STRIDE * C_DIM
    seg = kc.reshape(4 * B, n_seg, half)
    w1 = cmp_w1.astype(BF16)
    w1_ul = jnp.concatenate([w1[:, :half], w1[:, half:]], axis=2)
    pos = jnp.broadcast_to(cmp_pos.reshape(2, 1, CMP_BLOCK * C_DIM), (2, 8, CMP_BLOCK * C_DIM)).astype(BF16)
    out = pl.pallas_call(
        _compress_kernel,
        grid=(4, B),
        in_specs=[pl.BlockSpec((1, n_seg, half), lambda i, b: (i * B + b, 0, 0)),
                  pl.BlockSpec((1, half, 2 * CMP_HIDDEN), lambda i, b: (i // 2, 0, 0)),
                  pl.BlockSpec((1, 8, CMP_BLOCK * C_DIM), lambda i, b: (i // 2, 0, 0)),
                  pl.BlockSpec((1, CMP_BLOCK * C_DIM, CMP_HIDDEN), lambda i, b: (i // 2, 0, 0)),
                  pl.BlockSpec((1, CMP_HIDDEN, C_DIM), lambda i, b: (i // 2, 0, 0))],
        out_specs=pl.BlockSpec((1, 1, CMP_PAD + n_seg, C_DIM), lambda i, b: (i, b, 0, 0)),
        out_shape=jax.ShapeDtypeStruct((4, B, CMP_PAD + n_seg, C_DIM), BF16),
        compiler_params=_params(("parallel", "parallel")),
        name="compress",
    )(seg, w1_ul, pos, w1, cmp_w2.astype(BF16))
    return out


def _overlap_np(n_cmp_pad, n_slc):
    i = np.arange(n_cmp_pad)[:, None] - CMP_PAD
    m = np.arange(n_slc)[None, :]
    start = i * CMP_STRIDE
    ov = (start < (m + 1) * SLC_BLOCK) & (start + CMP_BLOCK - 1 >= m * SLC_BLOCK) & (i >= 0)
    return ov.astype(np.float32)


def _flash_update(s_heads, vext, m_ref, acc_ref, p_ref):
    W = s_heads[0].shape[1]
    nw = W // 128
    alphas = []
    for h, s in enumerate(s_heads):
        rs = slice(h * TQ, (h + 1) * TQ)
        m_prev = m_ref[rs, :]
        smax = functools.reduce(jnp.maximum, [s[:, i * 128:(i + 1) * 128] for i in range(nw)])
        m_new = jnp.maximum(m_prev, jnp.max(smax, axis=1, keepdims=True))
        alpha = jnp.exp2(m_prev - m_new)
        p = jnp.exp2(s - (m_new if nw == 1 else jnp.concatenate([m_new] * nw, axis=1)))
        m_ref[rs, :] = m_new
        p_ref[rs, 0:W] = p.astype(BF16)
        alphas.append(alpha)
    a = jnp.concatenate(alphas, axis=0)
    acc_ref[...] = jnp.concatenate([a, a], axis=1) * acc_ref[...] + _dot(p_ref[:, 0:W], vext)


def _nsa_kernel(q_ref, ks_ref, vs_ref, kw_ref, vw_ref, kc_ref, vc_ref, ov_ref,
                tsel_ref, twin_ref, tcmp_ref, g_ref, o_ref,
                m_ref, acc_ref, p_ref, pw_ref, sa_ref, sb_ref, oc_ref, ow_ref, *, n_sel_tab, top_n):
    qb = pl.program_id(2)
    H = C_HPG
    NB = ov_ref.shape[1]
    q_all = jnp.concatenate([q_ref[:, h * C_DIM:(h + 1) * C_DIM] for h in range(H)], axis=0)
    heads = lambda x: [x[h * TQ:(h + 1) * TQ] for h in range(H)]

    def reset():
        m_ref[...] = jnp.full(m_ref.shape, NEG, F32)
        acc_ref[...] = jnp.zeros(acc_ref.shape, F32)

    def softmax_rows(s):
        nw = s.shape[1] // 128
        smax = functools.reduce(jnp.maximum, [s[:, i * 128:(i + 1) * 128] for i in range(nw)])
        m = jnp.broadcast_to(jnp.max(smax, axis=1, keepdims=True), (TQ, 128))
        return jnp.exp2(s - jnp.concatenate([m] * nw, axis=1)), m

    t0 = qb // CMP_CLASSES + jnp.where(qb % CMP_CLASSES >= CMP_SPLIT, 1, 0)
    n_ct = kc_ref.shape[2] // TQ
    wc = n_ct * TQ
    tile_kind = [3] + [jnp.where(t == t0, 0, jnp.where(t == t0 + 1, 1, jnp.where(t < t0, 2, 3)))
                       for t in range(1, n_ct)]
    s_c = _dot_nt(q_all, kc_ref[0, 0])
    vext_c = jnp.concatenate([vc_ref[0, 0], ov_ref[...]], axis=1)
    inv_c = []
    for h in range(H):
        bias = jnp.concatenate([tcmp_ref[0, tile_kind[t], h] for t in range(n_ct)], axis=1)
        p, m = softmax_rows(s_c[h * TQ:(h + 1) * TQ] + bias)
        l = jnp.sum(p, axis=1, keepdims=True)
        inv_c.append(jnp.where(m > 0.5 * NEG, 1.0 / l, 0.0))
        p_ref[h * TQ:(h + 1) * TQ, 0:wc] = p.astype(BF16)
    res_c = _dot(p_ref[:, 0:wc], vext_c)
    imp = None
    for h in range(H):
        r = res_c[h * TQ:(h + 1) * TQ] * jnp.concatenate([inv_c[h], inv_c[h]], axis=1)
        oc_ref[h * TQ:(h + 1) * TQ, :] = r[:, 0:C_DIM]
        imp = r[:, C_DIM:] if imp is None else imp + r[:, C_DIM:]

    n_wt = WIN // TQ + 1
    ww = n_wt * TQ
    st = jnp.maximum(qb - (n_wt - 1), 0)
    r0w = pl.multiple_of(st * TQ, TQ)
    s_w = _dot_nt(q_all, kw_ref[pl.ds(r0w, ww), :])
    vext_w = jnp.concatenate([vw_ref[pl.ds(r0w, ww), :], jnp.ones((ww, C_DIM), BF16)], axis=1)
    widx = [jnp.maximum(qb - (st + c) + 1, 0) for c in range(n_wt)]
    for h in range(H):
        bias = jnp.concatenate([twin_ref[i, h] for i in widx], axis=1)
        p, _ = softmax_rows(s_w[h * TQ:(h + 1) * TQ] + bias)
        pw_ref[h * TQ:(h + 1) * TQ, :] = p.astype(BF16)
    res_w = _dot(pw_ref[...], vext_w)
    ow_ref[...] = res_w[:, 0:C_DIM] / res_w[:, C_DIM:]

    shift = SLC_BLOCK.bit_length() - 1
    qpos = qb * TQ + lax.broadcasted_iota(jnp.int32, (TQ, NB), 0)
    mblk = lax.broadcasted_iota(jnp.int32, (TQ, NB), 1)
    qblk = jnp.right_shift(qpos, shift)
    forced = (mblk == 0) | (mblk == qblk) | (mblk == qblk - 1)
    score = jnp.where(forced, 3e38, jnp.where(jnp.left_shift(mblk, shift) <= qpos, imp, NEG))
    score_t = score.T
    blk_t = lax.broadcasted_iota(jnp.int32, (NB, TQ), 0).astype(F32)
    sel_t = jnp.zeros((NB, TQ), F32)
    for _ in range(top_n):
        mx = jnp.max(score_t, axis=0, keepdims=True)
        idx = jnp.min(jnp.where(score_t == mx, blk_t, float(NB)), axis=0, keepdims=True)
        pick = blk_t == idx
        sel_t = jnp.where(pick, 1.0, sel_t)
        score_t = jnp.where(pick, -3e38, score_t)
    sel = sel_t.T

    reset()
    NT = 4
    TK = NT * TQ
    bpk = TK // SLC_BLOCK
    erow = lax.broadcasted_iota(jnp.int32, (NB, TK), 0)
    ecol = jnp.right_shift(lax.broadcasted_iota(jnp.int32, (NB, TK), 1), shift)
    expand0 = jnp.where(erow == ecol, 1.0, 0.0).astype(BF16)
    ones_k = jnp.ones((TK, C_DIM), BF16)
    n_steps = qb // NT + 1

    def sel_logits(kq, s_ref, near):
        kc = jnp.minimum(kq, n_steps - 1)
        r0 = pl.multiple_of(kc * TK, TK)
        s = _dot_nt(q_all, ks_ref[pl.ds(r0, TK), :])
        sel_k = pltpu.roll(sel, (NB - kc * bpk) % NB, 1).astype(BF16)
        addmask = _dot(sel_k, expand0) * (-NEG) + NEG
        if near:
            addmask = jnp.where(kq < n_steps, addmask, NEG)
            idx = [jnp.clip(qb - (NT * kc + c) + 1, 0, n_sel_tab - 1) for c in range(NT)]
        for h in range(H):
            bias = addmask
            if near:
                bias = bias + jnp.concatenate([tsel_ref[i, h] for i in idx], axis=1)
            s_ref[h * TQ:(h + 1) * TQ, :] = s[h * TQ:(h + 1) * TQ] + bias

    def sel_consume(kq, s_ref):
        r0 = pl.multiple_of(jnp.minimum(kq, n_steps - 1) * TK, TK)
        vext = jnp.concatenate([vs_ref[pl.ds(r0, TK), :], ones_k], axis=1)
        _flash_update([s_ref[h * TQ:(h + 1) * TQ, :] for h in range(H)], vext, m_ref, acc_ref, p_ref)

    def sel_run(k_lo, n2, near):
        @pl.when(n2 > 0)
        def _():
            sel_logits(k_lo, sa_ref, near)

        def body(j, carry):
            k = k_lo + 2 * j
            sel_logits(k + 1, sb_ref, near)
            sel_consume(k, sa_ref)
            sel_logits(k + 2, sa_ref, near)
            sel_consume(k + 1, sb_ref)
            return carry

        lax.fori_loop(0, n2, body, 0)

    n_far = jnp.maximum((qb + 1 - (n_sel_tab - 2)) // NT, 0)
    far2 = n_far // 2
    sel_run(0, far2, near=False)
    sel_run(2 * far2, (n_steps - 2 * far2 + 1) // 2, near=True)
    g = g_ref[...]
    for h in range(H):
        rs = slice(h * TQ, (h + 1) * TQ)
        out_s = acc_ref[rs, 0:C_DIM] / acc_ref[rs, C_DIM:2 * C_DIM]
        o = (g[:, h:h + 1] * oc_ref[rs, :] + g[:, H + h:H + h + 1] * out_s
             + g[:, 2 * H + h:2 * H + h + 1] * ow_ref[rs, :])
        o_ref[:, h * C_DIM:(h + 1) * C_DIM] = o.astype(BF16)


def _nsa_attention(q, kv, kvc, gates, tsel, twin, tcmp, B, S):
    N = B * S
    QT = S // TQ
    n_slc = S // SLC_BLOCK
    n_cmp_pad = kvc.shape[2]
    NB = 128
    assert n_slc <= NB and n_cmp_pad % TQ == 0 and QT % 4 == 0
    ov = jnp.asarray(_overlap_np(n_cmp_pad, NB), BF16)
    n_sel_delta = tsel.shape[0]
    kvspec = lambda c: pl.BlockSpec((S, C_DIM), lambda b, g, t: (b, c + g))
    cspec = lambda kvi: pl.BlockSpec((1, 1, n_cmp_pad, C_DIM), lambda b, g, t: (kvi * 2 + g, b, 0, 0))
    return pl.pallas_call(
        functools.partial(_nsa_kernel, n_sel_tab=n_sel_delta, top_n=min(SLC_TOP_N, n_slc)),
        grid=(B, C_GROUPS, QT),
        in_specs=[pl.BlockSpec((TQ, C_HPG * C_DIM), lambda b, g, t: (b * QT + t, g)),
                  kvspec(0), kvspec(2), kvspec(4), kvspec(6),
                  cspec(0), cspec(1),
                  pl.BlockSpec((n_cmp_pad, NB), lambda b, g, t: (0, 0)),
                  pl.BlockSpec((n_sel_delta, C_HPG, TQ, TQ), lambda b, g, t: (0, g, 0, 0)),
                  pl.BlockSpec((twin.shape[0], C_HPG, TQ, TQ), lambda b, g, t: (0, g, 0, 0)),
                  pl.BlockSpec((1, 4, C_HPG, TQ, TQ), lambda b, g, t: (t % CMP_CLASSES, 0, g, 0, 0)),
                  pl.BlockSpec((TQ, 128), lambda b, g, t: (b * QT + t, g))],
        out_specs=pl.BlockSpec((TQ, C_HPG * C_DIM), lambda b, g, t: (b * QT + t, g)),
        out_shape=jax.ShapeDtypeStruct((N, C_W), BF16),
        scratch_shapes=[pltpu.VMEM((C_HPG * TQ, 128), F32),
                        pltpu.VMEM((C_HPG * TQ, 2 * C_DIM), F32),
                        pltpu.VMEM((C_HPG * TQ, max(n_cmp_pad, 4 * TQ)), BF16),
                        pltpu.VMEM((C_HPG * TQ, WIN + TQ), BF16),
                        pltpu.VMEM((C_HPG * TQ, 4 * TQ), F32), pltpu.VMEM((C_HPG * TQ, 4 * TQ), F32),
                        pltpu.VMEM((C_HPG * TQ, C_DIM), F32), pltpu.VMEM((C_HPG * TQ, C_DIM), F32)],
        compiler_params=_params(("parallel", "parallel", "arbitrary")),
        name="nsa",
    )(q, kv, kv, kv, kv, kvc, kvc, ov, tsel, twin, tcmp, gates)


def _layer_ab(h2, B, S, w_in, gate_b, conv_w, head_g, w_out, dil_tab):
    w_pad = jnp.pad(w_in, ((0, 0), (0, AB_PAD - AB_PROJ))).astype(BF16)
    gb_pad = jnp.pad(gate_b, (0, 128 - gate_b.shape[0])).reshape(1, 128).astype(F32)
    aq, ak, av, bq, bk, bv, bo, gates = _ab_proj(h2, w_pad, conv_w.astype(F32), gb_pad, S)
    ya = _dilated_attention(aq, ak, av, dil_tab, B, S)
    yb = _mlstm(bq, bk, bv, gates, bo, head_g, B, S)
    wo = w_out.astype(BF16)
    return [ya, yb], [wo[:A_W], wo[A_W:]]


def _layer_c(h2, B, S, w_in, gate_b, cmp_pos, cmp_w1, cmp_w2, w_out, tsel, twin, tcmp):
    w_pad, gb_pad = _c_weights(w_in, gate_b)
    q, kc, kv, gates = _c_proj(h2, w_pad, gb_pad)
    kvc = _compress(kc, cmp_pos, cmp_w1, cmp_w2, B, S)
    out = _nsa_attention(q, kv, kvc, gates, tsel, twin, tcmp, B, S)
    return [out], [w_out.astype(BF16)]


def kernel(x, rel_bias, ln_g, ln_b, ab_w_in, ab_gate_b, ab_conv, ab_head_norm, ab_w_out,
           c_w_in, c_gate_b, c_cmp_pos, c_cmp_w1, c_cmp_w2, c_w_out,
           moe_wr_g, moe_br_g, moe_wr_e, moe_br_e, moe_w_gate, moe_w_up, moe_w_down):
    B, S, D = x.shape
    assert D == D_MODEL and S % (TM) == 0 and S % (16 * A_BLOCK) == 0
    _check_cmp_windows(S)
    dil_tab = _bias_tables(rel_bias, _dilated_idx(), shift=False)
    tsel, twin, tcmp = _nsa_tables(rel_bias)
    h = x.reshape(B * S, D)
    for layer in range(DEPTH):
        j = layer // 2
        if layer % 2 == 0:
            ys, ws = _layer_ab(h, B, S, ab_w_in[j], ab_gate_b[j], ab_conv[j], ab_head_norm[j],
                               ab_w_out[j], dil_tab)
        else:
            ys, ws = _layer_c(h, B, S, c_w_in[j], c_gate_b[j], c_cmp_pos[j], c_cmp_w1[j], c_cmp_w2[j],
                              c_w_out[j], tsel, twin, tcmp)
        wr, br = _router_weights(moe_wr_g[layer], moe_br_g[layer], moe_wr_e[layer], moe_br_e[layer])
        parts = []
        rows = (B * S) // MOE_CHUNKS
        for c in range(MOE_CHUNKS):
            hx, cnt = _out_ln_route(ys, ws, h, ln_g[layer, 0], ln_b[layer, 0], wr, br, c * rows, rows)
            parts.append(_moe(hx, cnt, layer, moe_w_gate, moe_w_up, moe_w_down,
                              ln_g[layer, 1], ln_b[layer, 1]))
        h = jnp.concatenate(parts, axis=0)
    return h.reshape(B, S, D)
```

```python
import functools
import math

import numpy as np
import jax
import jax.numpy as jnp
from jax import lax
from jax.experimental import pallas as pl
from jax.experimental.pallas import tpu as pltpu

F32 = jnp.float32
BF16 = jnp.bfloat16
NEG = -1e30
LOG2E = math.log2(math.e)
VMEM_LIMIT = 48 * 1024 * 1024

D_MODEL = 1024
DEPTH = 2
ALPHA = (2.0 * DEPTH) ** 0.25
LN_EPS = 1e-5
REL_BUCKETS = 32
REL_MAX_DIST = 2048

A_HEADS, A_DIM, A_W = 8, 64, 512
A_PATTERNS = ((128, 1), (512, 4), (2048, 16))
A_BLOCK = 128
B_HEADS, B_DIM, B_W = 4, 128, 512
B_CHUNK = 128
B_CONV = 4
AB_PROJ = 3592
AB_PAD = 3712

C_HEADS, C_GROUPS, C_HPG, C_DIM, C_W = 8, 2, 4, 128, 1024
CMP_BLOCK, CMP_STRIDE, CMP_HIDDEN = 32, 16, 256
SLC_BLOCK, SLC_TOP_N, WIN = 64, 16, 512
C_PROJ = 2584
TQ = 128
CMP_PAD = 128

N_GROUPS, EPG, N_EXPERTS, E_HID = 4, 4, 16, 512
N_BUCKETS = N_GROUPS * 6
TM = 256
MOE_CHUNKS = 1


def _dot(a, b):
    return jnp.dot(a, b, preferred_element_type=F32)


def _dot_nt(a, b):
    return lax.dot_general(a, b, (((1,), (1,)), ((), ())), preferred_element_type=F32)


def _params(sem):
    return pltpu.CompilerParams(dimension_semantics=sem, vmem_limit_bytes=VMEM_LIMIT)


def _bucket_np(n):
    n = np.maximum(n, 0)
    exact = REL_BUCKETS // 2
    nf = np.maximum(n, 1).astype(np.float64)
    large = exact + (np.log(nf / exact) / math.log(REL_MAX_DIST / exact)
                     * (REL_BUCKETS - exact)).astype(np.int64)
    return np.where(n < exact, n, np.minimum(large, REL_BUCKETS - 1)).astype(np.int32)


def _bias_tab_kernel(tab_ref, idx_ref, out_ref, *, shift, scale):
    R = idx_ref.shape[1]
    RC = 32

    def body(i, carry):
        r0 = pl.multiple_of(i * RC, RC)
        idx = idx_ref[0, pl.ds(r0, RC), :]
        for h in range(8):
            base = tab_ref[REL_BUCKETS - 1, h] if shift else 0.0
            val = jnp.full(idx.shape, (tab_ref[0, h] - base) * scale, F32)
            for b in range(1, REL_BUCKETS):
                val = jnp.where(idx == b, (tab_ref[b, h] - base) * scale, val)
            out_ref[0, h, pl.ds(r0, RC), :] = jnp.where(idx < 0, NEG, val)
        return carry

    lax.fori_loop(0, R // RC, body, 0)


def _bias_tables(rel_bias, idx_np, shift, scale=1.0):
    T, R, C = idx_np.shape
    return pl.pallas_call(
        functools.partial(_bias_tab_kernel, shift=shift, scale=scale),
        grid=(T,),
        in_specs=[pl.BlockSpec(memory_space=pltpu.SMEM),
                  pl.BlockSpec((1, R, C), lambda t: (t, 0, 0))],
        out_specs=pl.BlockSpec((1, 8, R, C), lambda t: (t, 0, 0, 0)),
        out_shape=jax.ShapeDtypeStruct((T, 8, R, C), F32),
        compiler_params=_params(("parallel",)),
        name="bias_tables",
    )(rel_bias.astype(F32), jnp.asarray(idx_np))


def _dilated_idx():
    qi = np.arange(A_BLOCK)[:, None]
    ki = np.arange(2 * A_BLOCK)[None, :]
    j = qi + A_BLOCK - ki
    out = []
    for window, dilation in A_PATTERNS:
        nk = window // dilation
        valid = (j >= 0) & (j <= nk)
        out.append(np.where(valid, _bucket_np(np.maximum(j, 0) * dilation), -1))
    return np.stack(out).astype(np.int32)


def _sel_idx():
    a = np.arange(TQ)[:, None]
    c = np.arange(TQ)[None, :]
    n_delta = -(-(_far_dist() + TQ) // TQ)
    out = []
    for delta in range(-1, n_delta + 1):
        dist = delta * TQ + a - c
        out.append(np.where(dist >= 0, _bucket_np(dist), -1))
    return np.stack(out).astype(np.int32)


def _far_dist():
    n = np.arange(0, 4 * REL_MAX_DIST)
    b = _bucket_np(n)
    return int(np.max(n[b < REL_BUCKETS - 1])) + 1


def _win_idx():
    a = np.arange(TQ)[:, None]
    c = np.arange(TQ)[None, :]
    out = []
    for delta in range(-1, WIN // TQ + 1):
        dist = delta * TQ + a - c
        out.append(np.where((dist >= 0) & (dist < WIN), _bucket_np(dist), -1))
    return np.stack(out).astype(np.int32)


CMP_PER_TILE = TQ // CMP_STRIDE
CMP_CLASSES = TQ // CMP_PER_TILE
CMP_SPLIT = 13


def _cmp_window_start(qb):
    return qb // CMP_CLASSES + (1 if qb % CMP_CLASSES >= CMP_SPLIT else 0)


def _cmp_idx():
    a = np.arange(TQ)[:, None]
    c = np.arange(TQ)[None, :]
    out = []
    for r in range(CMP_CLASSES):
        qb = CMP_CLASSES + r
        i0 = _cmp_window_start(qb) * TQ - CMP_PAD
        for half in range(2):
            dist = qb * TQ + a - ((i0 + half * TQ + c) * CMP_STRIDE + CMP_BLOCK - 1)
            out.append(np.where(dist >= 0, _bucket_np(dist), -1))
        out.append(np.full((TQ, TQ), REL_BUCKETS - 1))
        out.append(np.full((TQ, TQ), -1))
    return np.stack(out).astype(np.int32)


def _nsa_tables(rel_bias):
    tsel = _bias_tables(rel_bias, _sel_idx(), shift=True, scale=LOG2E)
    twin = _bias_tables(rel_bias, _win_idx(), shift=False, scale=LOG2E)
    tcmp = _bias_tables(rel_bias, _cmp_idx(), shift=True, scale=LOG2E)
    return tsel, twin, tcmp.reshape(CMP_CLASSES, 4, 8, TQ, TQ)


def _check_cmp_windows(S):
    far = _far_dist()
    for qb in range(S // TQ):
        i0 = _cmp_window_start(qb) * TQ - CMP_PAD
        s0 = qb * TQ
        assert s0 - ((i0 - 1) * CMP_STRIDE + CMP_BLOCK - 1) >= far
        assert s0 + TQ - 1 - ((i0 + 2 * TQ) * CMP_STRIDE + CMP_BLOCK - 1) < 0


def _ab_proj_kernel(x_ref, xh_ref, w_ref, cw_ref, gb_ref,
                    aq_ref, ak_ref, av_ref, bq_ref, bk_ref, bv_ref, bo_ref, g_ref,
                    pre_ref, *, tiles_per_seq):
    i = pl.program_id(0)
    tm = x_ref.shape[0]
    xb = x_ref[...].astype(BF16)
    aq_ref[...] = (_dot(xb, w_ref[:, 0:512]) * (A_DIM ** -0.5)).astype(BF16)
    ak_ref[...] = _dot(xb, w_ref[:, 512:1024]).astype(BF16)
    av_ref[...] = _dot(xb, w_ref[:, 1024:1536]).astype(BF16)
    bv_ref[...] = _dot(xb, w_ref[:, 2560:3072]).astype(BF16)
    bo_ref[...] = _dot(xb, w_ref[:, 3072:3584])
    g_ref[...] = _dot(xb, w_ref[:, 3584:AB_PAD]) + gb_ref[...]
    halo = _dot(xh_ref[...].astype(BF16), w_ref[:, 1536:2560])
    halo = jnp.where(i % tiles_per_seq == 0, 0.0, halo)
    pre_ref[0:8, :] = halo
    pre_ref[8:8 + tm, :] = _dot(xb, w_ref[:, 1536:2560])
    y = pre_ref[8:8 + tm, :] * cw_ref[B_CONV - 1:B_CONV, :]
    for k in range(B_CONV - 1):
        s = B_CONV - 1 - k
        y = y + pre_ref[8 - s:8 - s + tm, :] * cw_ref[k:k + 1, :]
    y = y / (1.0 + jnp.exp(-y))
    bq_ref[...] = (y[:, :B_W] * (B_DIM ** -0.5)).astype(BF16)
    bk_ref[...] = y[:, B_W:].astype(BF16)


def _ab_proj(x2, w_pad, conv_w, gate_b_pad, S):
    N = x2.shape[0]
    tm = TM
    tps = S // tm
    row = lambda i: (i, 0)
    fix = lambda i: (0, 0)
    outs = [jax.ShapeDtypeStruct((N, 512), BF16)] * 6 + [
        jax.ShapeDtypeStruct((N, 512), F32), jax.ShapeDtypeStruct((N, 128), F32)]
    o_specs = [pl.BlockSpec((tm, 512), row)] * 7 + [pl.BlockSpec((tm, 128), row)]
    return pl.pallas_call(
        functools.partial(_ab_proj_kernel, tiles_per_seq=tps),
        grid=(N // tm,),
        in_specs=[pl.BlockSpec((tm, D_MODEL), row),
                  pl.BlockSpec((8, D_MODEL), lambda i: (jnp.maximum(i * (tm // 8) - 1, 0), 0)),
                  pl.BlockSpec((D_MODEL, AB_PAD), fix),
                  pl.BlockSpec((B_CONV, 2 * B_W), fix),
                  pl.BlockSpec((1, 128), fix)],
        out_specs=o_specs,
        out_shape=outs,
        scratch_shapes=[pltpu.VMEM((tm + 8, 2 * B_W), F32)],
        compiler_params=_params(("parallel",)),
        name="ab_proj",
    )(x2, x2, w_pad, conv_w, gate_b_pad)


def _dilated_kernel(*refs, has_prev, is_last):
    if has_prev:
        q_ref, kp_ref, kc_ref, vp_ref, vc_ref, tab_ref, op_ref, lp_ref = refs[:8]
        outs = refs[8:]
    else:
        q_ref, kp_ref, kc_ref, vp_ref, vc_ref, tab_ref = refs[:6]
        outs = refs[6:]
    o_ref = outs[0]
    n = pl.program_id(2)
    first = jnp.where(n == 0, NEG, 0.0)
    lane = lax.broadcasted_iota(jnp.int32, (A_BLOCK, 128), 1)
    keep_side = [jnp.where(lane < A_DIM, 1.0, 0.0).astype(BF16), jnp.where(lane < A_DIM, 0.0, 1.0).astype(BF16)]
    ones = jnp.ones((A_BLOCK, 128), BF16)
    m_tile = jnp.zeros((A_BLOCK, 128), F32)
    l_tile = jnp.ones((A_BLOCK, 128), F32)
    unnorm = []
    for j in range(A_HEADS // 2):
        cs = slice(j * 128, (j + 1) * 128)
        q2 = q_ref[0, :, cs]
        kp, kc, vp, vc = kp_ref[0, :, cs], kc_ref[0, :, cs], vp_ref[0, :, cs], vc_ref[0, :, cs]
        acc = None
        for side in range(2):
            h = 2 * j + side
            keep = keep_side[side]
            q = q2 * keep
            sp = _dot_nt(q, kp) + tab_ref[0, h, :, 0:A_BLOCK] + first
            sc = _dot_nt(q, kc) + tab_ref[0, h, :, A_BLOCK:2 * A_BLOCK]
            m = jnp.max(jnp.maximum(sp, sc), axis=1, keepdims=True)
            pp = jnp.exp(sp - m).astype(BF16)
            pc = jnp.exp(sc - m).astype(BF16)
            vpe = jnp.concatenate([vp * keep, ones], axis=1)
            vce = jnp.concatenate([vc * keep, ones], axis=1)
            r = _dot(pp, vpe) + _dot(pc, vce)
            acc = r[:, 0:128] if acc is None else acc + r[:, 0:128]
            m_tile = jnp.where(lane == A_DIM + h, m, m_tile)
            l_tile = jnp.where(lane == A_DIM + h, r[:, 128:256], l_tile)
        unnorm.append(acc)
    stat = (lane >= A_DIM) & (lane < A_DIM + A_HEADS)
    lse = m_tile + jnp.log(l_tile)
    if has_prev:
        lp = lp_ref[0]
        mm = jnp.maximum(lp, lse)
        wp = jnp.exp(lp - mm)
        wc = jnp.exp(lse - mm)
        tot = wp + wc
        scale_prev = jnp.where(stat, wp / tot, 0.0)
        scale_cur = jnp.where(stat, wc / (tot * l_tile), 0.0)
        lse = mm + jnp.log(tot)
    else:
        scale_cur = jnp.where(stat, 1.0 / l_tile, 0.0)
    erow = lax.broadcasted_iota(jnp.int32, (128, A_W), 0)
    ecol = lax.broadcasted_iota(jnp.int32, (128, A_W), 1)
    expand = jnp.where(erow - A_DIM == jnp.right_shift(ecol, A_DIM.bit_length() - 1), 1.0, 0.0).astype(BF16)

    def spread(t):
        hi, mid, lo = _split3(t)
        return _dot(hi, expand) + _dot(mid, expand) + _dot(lo, expand)

    o = jnp.concatenate(unnorm, axis=1) * spread(scale_cur)
    if has_prev:
        o = o + op_ref[0] * spread(scale_prev)
    o_ref[0] = o.astype(o_ref.dtype)
    if not is_last:
        outs[1][0] = jnp.where(stat, lse, 0.0)


def _dilated_call(q, k, v, tab, prev, pattern_idx, dilation, B, S, is_last):
    d = dilation
    L = S // d
    nb = L // A_BLOCK
    qv = q.reshape(B, L, d * A_W)
    kv = k.reshape(B, L, d * A_W)
    vv = v.reshape(B, L, d * A_W)
    cur = lambda b, r, n: (b, n, r)
    prv = lambda b, r, n: (b, jnp.maximum(n - 1, 0), r)
    blk = pl.BlockSpec((1, A_BLOCK, A_W), cur)
    in_specs = [blk, pl.BlockSpec((1, A_BLOCK, A_W), prv), blk,
                pl.BlockSpec((1, A_BLOCK, A_W), prv), blk,
                pl.BlockSpec((1, 8, A_BLOCK, 2 * A_BLOCK), lambda b, r, n: (pattern_idx, 0, 0, 0))]
    args = [qv, kv, kv, vv, vv, tab]
    has_prev = prev is not None
    if has_prev:
        in_specs += [blk, pl.BlockSpec((1, A_BLOCK, 128), cur)]
        args += [prev[0].reshape(B, L, d * A_W), prev[1].reshape(B, L, d * 128)]
    if is_last:
        out_shape = [jax.ShapeDtypeStruct((B, L, d * A_W), BF16)]
        out_specs = [blk]
    else:
        out_shape = [jax.ShapeDtypeStruct((B, L, d * A_W), F32),
                     jax.ShapeDtypeStruct((B, L, d * 128), F32)]
        out_specs = [blk, pl.BlockSpec((1, A_BLOCK, 128), cur)]
    res = pl.pallas_call(
        functools.partial(_dilated_kernel, has_prev=has_prev, is_last=is_last),
        grid=(B, d, nb),
        in_specs=in_specs, out_specs=out_specs, out_shape=out_shape,
        compiler_params=_params(("parallel", "parallel", "arbitrary")),
        name="dilated_d%d" % d,
    )(*args)
    return [r.reshape(B * S, -1) for r in res]


def _dilated_attention(aq, ak, av, tab, B, S):
    prev = None
    for p, (window, d) in enumerate(A_PATTERNS):
        assert window // d == A_BLOCK and S % (d * A_BLOCK) == 0
        last = p == len(A_PATTERNS) - 1
        prev = _dilated_call(aq, ak, av, tab, prev, p, d, B, S, last)
    return prev[0]


def _split3(x):
    hi = x.astype(BF16)
    r = x - hi.astype(F32)
    mid = r.astype(BF16)
    lo = (r - mid.astype(F32)).astype(BF16)
    return hi, mid, lo


def _mlstm_kernel(q_ref, k_ref, v_ref, g_ref, bo_ref, hg_ref, y_ref, c_ref, m_ref):
    L = B_CHUNK
    c = pl.program_id(1)

    @pl.when(c == 0)
    def _():
        c_ref[...] = jnp.zeros_like(c_ref)
        m_ref[...] = jnp.zeros_like(m_ref)

    lane = lax.broadcasted_iota(jnp.int32, (L, 128), 1)
    row = lax.broadcasted_iota(jnp.int32, (L, L), 0)
    col = lax.broadcasted_iota(jnp.int32, (L, L), 1)
    tri = row >= col
    g = g_ref[0]
    is_f = (lane >= B_HEADS) & (lane < 2 * B_HEADS)
    logf = jnp.minimum(g, 0.0) - jnp.log(1.0 + jnp.exp(-jnp.abs(g)))
    gl = jnp.where(is_f, logf, jnp.where(lane < B_HEADS, g, 0.0))
    tril = jnp.where(tri, 1.0, 0.0).astype(BF16)
    hi, mid, lo = _split3(gl)
    cum = _dot(tril, hi) + _dot(tril, mid) + _dot(tril, lo)
    cum_t = cum.T
    gl_t = gl.T
    ones = jnp.ones((L, B_DIM), BF16)
    for h in range(B_HEADS):
        cs = slice(h * B_DIM, (h + 1) * B_DIM)
        q = q_ref[0, :, cs]
        k = k_ref[0, :, cs]
        v_ext = jnp.concatenate([v_ref[0, :, cs], ones], axis=1)
        b_col = cum[:, B_HEADS + h:B_HEADS + h + 1]
        b_row = cum_t[B_HEADS + h:B_HEADS + h + 1, :]
        i_col = gl[:, h:h + 1]
        i_row = gl_t[h:h + 1, :]
        m_prev = m_ref[h, 0:1, 0:1]
        dm = jnp.where(tri, b_col - b_row + i_row, NEG)
        inter = b_col + m_prev
        m_t = jnp.maximum(inter, jnp.max(dm, axis=1, keepdims=True))
        p = jnp.exp(dm - m_t)
        sqk = _dot_nt(q, k) * p
        sc = jnp.exp(inter - m_t)
        lhs = jnp.concatenate([(sc * q.astype(F32)).astype(BF16), sqk.astype(BF16)], axis=1)
        c_ext = c_ref[h]
        rhs = jnp.concatenate([c_ext.astype(BF16), v_ext], axis=0)
        res = _dot(lhs, rhs)
        num = res[:, :B_DIM]
        den = res[:, B_DIM:]
        hh = num / jnp.maximum(jnp.abs(den), jnp.exp(-m_t))
        b_last = b_col[L - 1:L, :]
        gk = b_last - b_col + i_col
        m_new = jnp.maximum(b_last + m_prev, jnp.max(gk, axis=0, keepdims=True))
        wk = jnp.exp(gk - m_new)
        decay = jnp.exp(b_last + m_prev - m_new)
        kw_t = (wk * k.astype(F32)).T.astype(BF16)
        c_ref[h] = decay * c_ext + _dot(kw_t, v_ext)
        m_ref[h] = jnp.broadcast_to(m_new, (8, 128))
        mu = jnp.mean(hh, axis=1, keepdims=True)
        xc = hh - mu
        var = jnp.mean(xc * xc, axis=1, keepdims=True)
        hn = xc * lax.rsqrt(var + LN_EPS) * hg_ref[:, cs]
        bo = bo_ref[0, :, cs]
        y_ref[0, :, cs] = (hn / (1.0 + jnp.exp(-bo))).astype(BF16)


def _mlstm(bq, bk, bv, gates, bo, head_g, B, S):
    nc = S // B_CHUNK
    r3 = lambda a: a.reshape(B, S, a.shape[-1])
    blk = pl.BlockSpec((1, B_CHUNK, B_W), lambda b, c: (b, c, 0))
    y = pl.pallas_call(
        _mlstm_kernel,
        grid=(B, nc),
        in_specs=[blk, blk, blk,
                  pl.BlockSpec((1, B_CHUNK, 128), lambda b, c: (b, c, 0)),
                  blk,
                  pl.BlockSpec((1, B_W), lambda b, c: (0, 0))],
        out_specs=blk,
        out_shape=jax.ShapeDtypeStruct((B, S, B_W), BF16),
        scratch_shapes=[pltpu.VMEM((B_HEADS, B_DIM, 2 * B_DIM), F32),
                        pltpu.VMEM((B_HEADS, 8, 128), F32)],
        compiler_params=_params(("parallel", "arbitrary")),
        name="mlstm",
    )(r3(bq), r3(bk), r3(bv), r3(gates), r3(bo), head_g.reshape(1, B_W).astype(F32))
    return y.reshape(B * S, B_W)


def _layer_norm(z, g, b):
    mu = jnp.mean(z, axis=1, keepdims=True)
    zc = z - mu
    var = jnp.mean(zc * zc, axis=1, keepdims=True)
    return zc * lax.rsqrt(var + LN_EPS) * g + b


def _route(logits, cnt_ref):
    tm = logits.shape[0]
    lt = logits.T
    col = lambda c: lt[c:c + 1, :]
    gl = [col(c) for c in range(N_GROUPS)]
    gmax = functools.reduce(jnp.maximum, gl)
    gsum = sum(jnp.exp(x - gmax) for x in gl)
    g_idx = jnp.full(gmax.shape, N_GROUPS - 1, jnp.int32)
    for c in range(N_GROUPS - 2, -1, -1):
        g_idx = jnp.where(gl[c] == gmax, c, g_idx)
    g_w = 1.0 / gsum
    el = []
    for k in range(EPG):
        x = col(N_GROUPS + (N_GROUPS - 1) * EPG + k)
        for g in range(N_GROUPS - 2, -1, -1):
            x = jnp.where(g_idx == g, col(N_GROUPS + g * EPG + k), x)
        el.append(x)
    v1 = functools.reduce(jnp.maximum, el)
    i1 = jnp.full(v1.shape, EPG - 1, jnp.int32)
    for k in range(EPG - 2, -1, -1):
        i1 = jnp.where(el[k] == v1, k, i1)
    el2 = [jnp.where(i1 == k, -jnp.inf, el[k]) for k in range(EPG)]
    v2 = functools.reduce(jnp.maximum, el2)
    i2 = jnp.full(v2.shape, EPG - 1, jnp.int32)
    for k in range(EPG - 2, -1, -1):
        i2 = jnp.where((el2[k] == v2) & (i1 != k), k, i2)
    t = jnp.exp(v2 - v1)
    w1 = g_w / (1.0 + t)
    w2 = w1 * t
    a = jnp.minimum(i1, i2)
    b = jnp.maximum(i1, i2)
    pair = jnp.where(a == 0, b - 1, jnp.where(a == 1, b + 1, 5))
    bucket = (g_idx * 6 + pair).astype(F32)
    w_lo = jnp.where(i1 < i2, w1, w2)
    w_hi = jnp.where(i1 < i2, w2, w1)
    sub = lax.broadcasted_iota(jnp.int32, (128, tm), 0)
    onehot_t = jnp.where(sub.astype(F32) == bucket, 1.0, 0.0)
    srow = lax.broadcasted_iota(jnp.int32, (tm, tm), 0)
    scol = lax.broadcasted_iota(jnp.int32, (tm, tm), 1)
    before = jnp.where(srow < scol, 1.0, 0.0).astype(BF16)
    oh = onehot_t.astype(BF16)
    carry = cnt_ref[...]
    prior = _dot(oh, before) + jnp.concatenate([carry] * (tm // 128), axis=1)
    rank = jnp.sum(onehot_t * prior, axis=0, keepdims=True)
    cnt_ref[...] = carry + _dot(oh, jnp.ones((tm, 128), BF16))
    out_t = jnp.where(sub == 0, bucket, jnp.where(sub == 1, w_lo, jnp.where(sub == 2, w_hi,
                      jnp.where(sub == 3, rank, 0.0))))
    return out_t.T


def _out_ln_route_kernel(*refs, n_in):
    y_refs = refs[:n_in]
    w_refs = refs[n_in:2 * n_in]
    x_ref, g_ref, b_ref, wrh_ref, wrl_ref, br_ref, h_ref, c_ref, cnt_ref = refs[2 * n_in:]

    @pl.when(pl.program_id(0) == 0)
    def _():
        cnt_ref[...] = jnp.zeros_like(cnt_ref)

    y = _dot(y_refs[0][...], w_refs[0][...])
    for i in range(1, n_in):
        y = y + _dot(y_refs[i][...], w_refs[i][...])
    hn = _layer_norm(ALPHA * x_ref[...] + y, g_ref[...], b_ref[...])
    h_ref[:, 0:D_MODEL] = hn
    hi = hn.astype(BF16)
    lo = (hn - hi.astype(F32)).astype(BF16)
    logits = (_dot_nt(hi, wrh_ref[...]) + _dot_nt(lo, wrh_ref[...]) + _dot_nt(hi, wrl_ref[...])
              + br_ref[...])
    h_ref[:, D_MODEL:D_MODEL + 128] = _route(logits, cnt_ref)
    c_ref[...] = cnt_ref[...]


def _out_ln_route(ys, ws, x2, ln_g, ln_b, wr, br, row_start, N):
    tm = TM
    off = row_start // tm
    row = lambda i: (i + off, 0)
    fix = lambda i: (0, 0)
    wr_hi = wr.astype(BF16)
    wr_lo = (wr - wr_hi.astype(F32)).astype(BF16)
    in_specs = ([pl.BlockSpec((tm, y.shape[1]), row) for y in ys]
                + [pl.BlockSpec(w.shape, fix) for w in ws]
                + [pl.BlockSpec((tm, D_MODEL), row),
                   pl.BlockSpec((1, D_MODEL), fix), pl.BlockSpec((1, D_MODEL), fix),
                   pl.BlockSpec((128, D_MODEL), fix), pl.BlockSpec((128, D_MODEL), fix),
                   pl.BlockSpec((1, 128), fix)])
    return pl.pallas_call(
        functools.partial(_out_ln_route_kernel, n_in=len(ys)),
        grid=(N // tm,),
        in_specs=in_specs,
        out_specs=[pl.BlockSpec((tm, D_MODEL + 128), lambda i: (i, 0)), pl.BlockSpec((128, 128), fix)],
        out_shape=[jax.ShapeDtypeStruct((N, D_MODEL + 128), F32), jax.ShapeDtypeStruct((128, 128), F32)],
        scratch_shapes=[pltpu.VMEM((128, 128), F32)],
        compiler_params=_params(("arbitrary",)),
        name="out_ln_route",
    )(*ys, *ws, x2, ln_g.reshape(1, -1), ln_b.reshape(1, -1), wr_hi, wr_lo, br)


def _router_weights(wr_g, br_g, wr_e, br_e):
    we = wr_e.transpose(0, 2, 1).reshape(N_GROUPS * EPG, D_MODEL)
    w = jnp.concatenate([wr_g.T, we], axis=0)
    w = jnp.pad(w, ((0, 128 - w.shape[0]), (0, 0)))
    b = jnp.concatenate([br_g, br_e.reshape(-1)])
    b = jnp.pad(b, (0, 128 - b.shape[0])).reshape(1, 128)
    return w.astype(F32), b.astype(F32)


_PAIRS = ((0, 1), (0, 2), (0, 3), (1, 2), (1, 3), (2, 3))


def _moe_kernel(elo_ref, ehi_ref, chg_ref, nt_ref,
                x_ref, wgl_ref, wul_ref, wdl_ref, wgh_ref, wuh_ref, wdh_ref,
                g_ref, b_ref, o_ref, wg_s, wu_s, wd_s):
    t = pl.program_id(0)

    @pl.when(chg_ref[t] == 1)
    def _():
        wg_s[0] = wgl_ref[0, 0].astype(BF16)
        wu_s[0] = wul_ref[0, 0].astype(BF16)
        wd_s[0] = wdl_ref[0, 0].astype(BF16)
        wg_s[1] = wgh_ref[0, 0].astype(BF16)
        wu_s[1] = wuh_ref[0, 0].astype(BF16)
        wd_s[1] = wdh_ref[0, 0].astype(BF16)

    @pl.when(t < nt_ref[0])
    def _():
        x = x_ref[:, 0:D_MODEL]
        xb = x.astype(BF16)
        r = x_ref[:, D_MODEL:D_MODEL + 128]
        acc = None
        for e in range(2):
            a = _dot(xb, wg_s[e])
            u = _dot(xb, wu_s[e])
            hcur = (a / (1.0 + jnp.exp(-a))) * u * r[:, 1 + e:2 + e]
            y = _dot(hcur.astype(BF16), wd_s[e])
            acc = y if acc is None else acc + y
        o_ref[...] = _layer_norm(ALPHA * x + acc, g_ref[...], b_ref[...])

    @pl.when(t >= nt_ref[0])
    def _():
        o_ref[...] = jnp.zeros_like(o_ref)


def _moe(hx, cnt, layer, w_gate, w_up, w_down, ln_g, ln_b):
    N = hx.shape[0]
    tm = TM
    n_tiles = N // tm + N_BUCKETS
    n_pad = n_tiles * tm
    bucket = hx[:, D_MODEL].astype(jnp.int32)
    rank = hx[:, D_MODEL + 3].astype(jnp.int32)
    counts = cnt[:N_BUCKETS, 0].astype(jnp.int32)
    padded = ((counts + tm - 1) // tm) * tm
    ends = jnp.cumsum(padded)
    offs = ends - padded
    dest = offs[bucket] + rank
    src = jnp.zeros((n_pad,), jnp.int32).at[dest].set(
        jnp.arange(N, dtype=jnp.int32), mode="promise_in_bounds", unique_indices=True)
    tile_start = jnp.arange(n_tiles, dtype=jnp.int32) * tm
    n_used = (ends[-1] // tm).astype(jnp.int32)
    tb = jnp.searchsorted(ends, tile_start, side="right").astype(jnp.int32)
    tb_last = jnp.take(tb, jnp.maximum(n_used - 1, 0))
    tb = jnp.where(tile_start < ends[-1], tb, tb_last)
    pairs = jnp.asarray(_PAIRS, jnp.int32)
    elo = (tb // 6) * EPG + pairs[tb % 6, 0]
    ehi = (tb // 6) * EPG + pairs[tb % 6, 1]
    chg = jnp.concatenate([jnp.ones((1,), jnp.int32), (tb[1:] != tb[:-1]).astype(jnp.int32)])
    xs = hx.at[src].get(mode="promise_in_bounds")

    row = lambda t, *_: (t, 0)
    fix = lambda t, *_: (0, 0)
    wlo = lambda t, elo, ehi, chg, nt: (layer, elo[t], 0, 0)
    whi = lambda t, elo, ehi, chg, nt: (layer, ehi[t], 0, 0)
    up_spec = lambda im: pl.BlockSpec((1, 1, D_MODEL, E_HID), im)
    dn_spec = lambda im: pl.BlockSpec((1, 1, E_HID, D_MODEL), im)
    grid_spec = pltpu.PrefetchScalarGridSpec(
        num_scalar_prefetch=4,
        grid=(n_tiles,),
        in_specs=[pl.BlockSpec((tm, D_MODEL + 128), row),
                  up_spec(wlo), up_spec(wlo), dn_spec(wlo),
                  up_spec(whi), up_spec(whi), dn_spec(whi),
                  pl.BlockSpec((1, D_MODEL), fix), pl.BlockSpec((1, D_MODEL), fix)],
        out_specs=pl.BlockSpec((tm, D_MODEL), row),
        scratch_shapes=[pltpu.VMEM((2, D_MODEL, E_HID), BF16),
                        pltpu.VMEM((2, D_MODEL, E_HID), BF16),
                        pltpu.VMEM((2, E_HID, D_MODEL), BF16)])
    out_sorted = pl.pallas_call(
        _moe_kernel,
        grid_spec=grid_spec,
        out_shape=jax.ShapeDtypeStruct((n_pad, D_MODEL), F32),
        compiler_params=_params(("arbitrary",)),
        name="moe",
    )(elo, ehi, chg, n_used.reshape(1), xs, w_gate, w_up, w_down, w_gate, w_up, w_down,
      ln_g.reshape(1, -1), ln_b.reshape(1, -1))
    return out_sorted.at[dest].get(mode="promise_in_bounds", unique_indices=True)


def _c_proj_kernel(x_ref, w_ref, gb_ref, q_ref, kc_ref, kv_ref, g_ref):
    xb = x_ref[...].astype(BF16)
    q_ref[...] = (_dot(xb, w_ref[:, 0:C_W]) * (C_DIM ** -0.5 * LOG2E)).astype(BF16)
    for i in range(4):
        kc_ref[i] = _dot(xb, w_ref[:, C_W + i * 128:C_W + (i + 1) * 128]).astype(BF16)
    kv_ref[...] = _dot(xb, w_ref[:, C_W + 512:C_W + 1536]).astype(BF16)
    z = _dot(xb, w_ref[:, C_W + 1536:C_W + 1792]) + gb_ref[...]
    g_ref[...] = 1.0 / (1.0 + jnp.exp(-z))


def _c_proj(x2, w_pad, gb_pad):
    N = x2.shape[0]
    tm = TM
    row = lambda i: (i, 0)
    fix = lambda i: (0, 0)
    wcols = w_pad.shape[1]
    return pl.pallas_call(
        _c_proj_kernel,
        grid=(N // tm,),
        in_specs=[pl.BlockSpec((tm, D_MODEL), row), pl.BlockSpec((D_MODEL, wcols), fix),
                  pl.BlockSpec((1, 256), fix)],
        out_specs=[pl.BlockSpec((tm, C_W), row), pl.BlockSpec((4, tm, 128), lambda i: (0, i, 0)),
                   pl.BlockSpec((tm, 1024), row), pl.BlockSpec((tm, 256), row)],
        out_shape=[jax.ShapeDtypeStruct((N, C_W), BF16), jax.ShapeDtypeStruct((4, N, 128), BF16),
                   jax.ShapeDtypeStruct((N, 1024), BF16), jax.ShapeDtypeStruct((N, 256), F32)],
        compiler_params=_params(("parallel",)),
        name="c_proj",
    )(x2, w_pad, gb_pad)


def _c_weights(w_in, gate_b):
    gcols = []
    gb = []
    for g in range(C_GROUPS):
        idx = [C_PROJ - 3 * C_HEADS + br * C_HEADS + g * C_HPG + j for br in range(3) for j in range(C_HPG)]
        gcols.append(jnp.pad(w_in[:, np.asarray(idx)], ((0, 0), (0, 128 - len(idx)))))
        gb.append(jnp.pad(gate_b[np.asarray(idx) - (C_PROJ - 3 * C_HEADS)], (0, 128 - len(idx))))
    w = jnp.concatenate([w_in[:, :C_PROJ - 3 * C_HEADS]] + gcols, axis=1).astype(BF16)
    return w, jnp.concatenate(gb).reshape(1, 256).astype(F32)


def _compress_kernel(seg_ref, w1_ref, pos_ref, w1f_ref, w2_ref, o_ref):
    n_seg = seg_ref.shape[1]
    ul = _dot(seg_ref[0], w1_ref[0])
    u = ul[:, :CMP_HIDDEN]
    lnext = pltpu.roll(ul[:, CMP_HIDDEN:], n_seg - 1, 0)
    cpos = _dot(pos_ref[0], w1f_ref[0])[0:1, :]
    pre = u + lnext + cpos
    act = 0.5 * pre * (1.0 + jnp.tanh(math.sqrt(2.0 / math.pi) * (pre + 0.044715 * pre * pre * pre)))
    o_ref[0, 0, 0:CMP_PAD, :] = jnp.zeros((CMP_PAD, C_DIM), BF16)
    o_ref[0, 0, CMP_PAD:CMP_PAD + n_seg, :] = _dot(act.astype(BF16), w2_ref[0]).astype(BF16)


def _compress(kc, cmp_pos, cmp_w1, cmp_w2, B, S):
    n_seg = S // CMP_STRIDE
    half = CMP_STRIDE * C_DIM
    seg = kc.reshape(4 * B, n_seg, half)
    w1 = cmp_w1.astype(BF16)
    w1_ul = jnp.concatenate([w1[:, :half], w1[:, half:]], axis=2)
    pos = jnp.broadcast_to(cmp_pos.reshape(2, 1, CMP_BLOCK * C_DIM), (2, 8, CMP_BLOCK * C_DIM)).astype(BF16)
    out = pl.pallas_call(
        _compress_kernel,
        grid=(4, B),
        in_specs=[pl.BlockSpec((1, n_seg, half), lambda i, b: (i * B + b, 0, 0)),
                  pl.BlockSpec((1, half, 2 * CMP_HIDDEN), lambda i, b: (i // 2, 0, 0)),
                  pl.BlockSpec((1, 8, CMP_BLOCK * C_DIM), lambda i, b: (i // 2, 0, 0)),
                  pl.BlockSpec((1, CMP_BLOCK * C_DIM, CMP_HIDDEN), lambda i, b: (i // 2, 0, 0)),
                  pl.BlockSpec((1, CMP_HIDDEN, C_DIM), lambda i, b: (i // 2, 0, 0))],
        out_specs=pl.BlockSpec((1, 1, CMP_PAD + n_seg, C_DIM), lambda i, b: (i, b, 0, 0)),
        out_shape=jax.ShapeDtypeStruct((4, B, CMP_PAD + n_seg, C_DIM), BF16),
        compiler_params=_params(("parallel", "parallel")),
        name="compress",
    )(seg, w1_ul, pos, w1, cmp_w2.astype(BF16))
    return out


def _overlap_np(n_cmp_pad, n_slc):
    i = np.arange(n_cmp_pad)[:, None] - CMP_PAD
    m = np.arange(n_slc)[None, :]
    start = i * CMP_STRIDE
    ov = (start < (m + 1) * SLC_BLOCK) & (start + CMP_BLOCK - 1 >= m * SLC_BLOCK) & (i >= 0)
    return ov.astype(np.float32)


def _flash_update(s_heads, vext, m_ref, acc_ref, p_ref):
    W = s_heads[0].shape[1]
    nw = W // 128
    alphas = []
    for h, s in enumerate(s_heads):
        rs = slice(h * TQ, (h + 1) * TQ)
        m_prev = m_ref[rs, :]
        smax = functools.reduce(jnp.maximum, [s[:, i * 128:(i + 1) * 128] for i in range(nw)])
        m_new = jnp.maximum(m_prev, jnp.max(smax, axis=1, keepdims=True))
        alpha = jnp.exp2(m_prev - m_new)
        p = jnp.exp2(s - (m_new if nw == 1 else jnp.concatenate([m_new] * nw, axis=1)))
        m_ref[rs, :] = m_new
        p_ref[rs, 0:W] = p.astype(BF16)
        alphas.append(alpha)
    a = jnp.concatenate(alphas, axis=0)
    acc_ref[...] = jnp.concatenate([a, a], axis=1) * acc_ref[...] + _dot(p_ref[:, 0:W], vext)


def _nsa_kernel(q_ref, ks_ref, vs_ref, kw_ref, vw_ref, kc_ref, vc_ref, ov_ref,
                tsel_ref, twin_ref, tcmp_ref, g_ref, o_ref,
                m_ref, acc_ref, p_ref, pw_ref, sa_ref, sb_ref, oc_ref, ow_ref, *, n_sel_tab, top_n):
    qb = pl.program_id(2)
    H = C_HPG
    NB = ov_ref.shape[1]
    q_all = jnp.concatenate([q_ref[:, h * C_DIM:(h + 1) * C_DIM] for h in range(H)], axis=0)
    heads = lambda x: [x[h * TQ:(h + 1) * TQ] for h in range(H)]

    def reset():
        m_ref[...] = jnp.full(m_ref.shape, NEG, F32)
        acc_ref[...] = jnp.zeros(acc_ref.shape, F32)

    def softmax_rows(s):
        nw = s.shape[1] // 128
        smax = functools.reduce(jnp.maximum, [s[:, i * 128:(i + 1) * 128] for i in range(nw)])
        m = jnp.broadcast_to(jnp.max(smax, axis=1, keepdims=True), (TQ, 128))
        return jnp.exp2(s - jnp.concatenate([m] * nw, axis=1)), m

    t0 = qb // CMP_CLASSES + jnp.where(qb % CMP_CLASSES >= CMP_SPLIT, 1, 0)
    n_ct = kc_ref.shape[2] // TQ
    wc = n_ct * TQ
    tile_kind = [3] + [jnp.where(t == t0, 0, jnp.where(t == t0 + 1, 1, jnp.where(t < t0, 2, 3)))
                       for t in range(1, n_ct)]
    s_c = _dot_nt(q_all, kc_ref[0, 0])
    vext_c = jnp.concatenate([vc_ref[0, 0], ov_ref[...]], axis=1)
    inv_c = []
    for h in range(H):
        bias = jnp.concatenate([tcmp_ref[0, tile_kind[t], h] for t in range(n_ct)], axis=1)
        p, m = softmax_rows(s_c[h * TQ:(h + 1) * TQ] + bias)
        l = jnp.sum(p, axis=1, keepdims=True)
        inv_c.append(jnp.where(m > 0.5 * NEG, 1.0 / l, 0.0))
        p_ref[h * TQ:(h + 1) * TQ, 0:wc] = p.astype(BF16)
    res_c = _dot(p_ref[:, 0:wc], vext_c)
    imp = None
    for h in range(H):
        r = res_c[h * TQ:(h + 1) * TQ] * jnp.concatenate([inv_c[h], inv_c[h]], axis=1)
        oc_ref[h * TQ:(h + 1) * TQ, :] = r[:, 0:C_DIM]
        imp = r[:, C_DIM:] if imp is None else imp + r[:, C_DIM:]

    n_wt = WIN // TQ + 1
    ww = n_wt * TQ
    st = jnp.maximum(qb - (n_wt - 1), 0)
    r0w = pl.multiple_of(st * TQ, TQ)
    s_w = _dot_nt(q_all, kw_ref[pl.ds(r0w, ww), :])
    vext_w = jnp.concatenate([vw_ref[pl.ds(r0w, ww), :], jnp.ones((ww, C_DIM), BF16)], axis=1)
    widx = [jnp.maximum(qb - (st + c) + 1, 0) for c in range(n_wt)]
    for h in range(H):
        bias = jnp.concatenate([twin_ref[i, h] for i in widx], axis=1)
        p, _ = softmax_rows(s_w[h * TQ:(h + 1) * TQ] + bias)
        pw_ref[h * TQ:(h + 1) * TQ, :] = p.astype(BF16)
    res_w = _dot(pw_ref[...], vext_w)
    ow_ref[...] = res_w[:, 0:C_DIM] / res_w[:, C_DIM:]

    shift = SLC_BLOCK.bit_length() - 1
    qpos = qb * TQ + lax.broadcasted_iota(jnp.int32, (TQ, NB), 0)
    mblk = lax.broadcasted_iota(jnp.int32, (TQ, NB), 1)
    qblk = jnp.right_shift(qpos, shift)
    forced = (mblk == 0) | (mblk == qblk) | (mblk == qblk - 1)
    score = jnp.where(forced, 3e38, jnp.where(jnp.left_shift(mblk, shift) <= qpos, imp, NEG))
    score_t = score.T
    blk_t = lax.broadcasted_iota(jnp.int32, (NB, TQ), 0).astype(F32)
    sel_t = jnp.zeros((NB, TQ), F32)
    for _ in range(top_n):
        mx = jnp.max(score_t, axis=0, keepdims=True)
        idx = jnp.min(jnp.where(score_t == mx, blk_t, float(NB)), axis=0, keepdims=True)
        pick = blk_t == idx
        sel_t = jnp.where(pick, 1.0, sel_t)
        score_t = jnp.where(pick, -3e38, score_t)
    sel = sel_t.T

    reset()
    NT = 4
    TK = NT * TQ
    bpk = TK // SLC_BLOCK
    erow = lax.broadcasted_iota(jnp.int32, (NB, TK), 0)
    ecol = jnp.right_shift(lax.broadcasted_iota(jnp.int32, (NB, TK), 1), shift)
    expand0 = jnp.where(erow == ecol, 1.0, 0.0).astype(BF16)
    ones_k = jnp.ones((TK, C_DIM), BF16)
    n_steps = qb // NT + 1

    def sel_logits(kq, s_ref, near):
        kc = jnp.minimum(kq, n_steps - 1)
        r0 = pl.multiple_of(kc * TK, TK)
        s = _dot_nt(q_all, ks_ref[pl.ds(r0, TK), :])
        sel_k = pltpu.roll(sel, (NB - kc * bpk) % NB, 1).astype(BF16)
        addmask = _dot(sel_k, expand0) * (-NEG) + NEG
        if near:
            addmask = jnp.where(kq < n_steps, addmask, NEG)
            idx = [jnp.clip(qb - (NT * kc + c) + 1, 0, n_sel_tab - 1) for c in range(NT)]
        for h in range(H):
            bias = addmask
            if near:
                bias = bias + jnp.concatenate([tsel_ref[i, h] for i in idx], axis=1)
            s_ref[h * TQ:(h + 1) * TQ, :] = s[h * TQ:(h + 1) * TQ] + bias

    def sel_consume(kq, s_ref):
        r0 = pl.multiple_of(jnp.minimum(kq, n_steps - 1) * TK, TK)
        vext = jnp.concatenate([vs_ref[pl.ds(r0, TK), :], ones_k], axis=1)
        _flash_update([s_ref[h * TQ:(h + 1) * TQ, :] for h in range(H)], vext, m_ref, acc_ref, p_ref)

    def sel_run(k_lo, n2, near):
        @pl.when(n2 > 0)
        def _():
            sel_logits(k_lo, sa_ref, near)

        def body(j, carry):
            k = k_lo + 2 * j
            sel_logits(k + 1, sb_ref, near)
            sel_consume(k, sa_ref)
            sel_logits(k + 2, sa_ref, near)
            sel_consume(k + 1, sb_ref)
            return carry

        lax.fori_loop(0, n2, body, 0)

    n_far = jnp.maximum((qb + 1 - (n_sel_tab - 2)) // NT, 0)
    far2 = n_far // 2
    sel_run(0, far2, near=False)
    sel_run(2 * far2, (n_steps - 2 * far2 + 1) // 2, near=True)
    g = g_ref[...]
    for h in range(H):
        rs = slice(h * TQ, (h + 1) * TQ)
        out_s = acc_ref[rs, 0:C_DIM] / acc_ref[rs, C_DIM:2 * C_DIM]
        o = (g[:, h:h + 1] * oc_ref[rs, :] + g[:, H + h:H + h + 1] * out_s
             + g[:, 2 * H + h:2 * H + h + 1] * ow_ref[rs, :])
        o_ref[:, h * C_DIM:(h + 1) * C_DIM] = o.astype(BF16)


def _nsa_attention(q, kv, kvc, gates, tsel, twin, tcmp, B, S):
    N = B * S
    QT = S // TQ
    n_slc = S // SLC_BLOCK
    n_cmp_pad = kvc.shape[2]
    NB = 128
    assert n_slc <= NB and n_cmp_pad % TQ == 0 and QT % 4 == 0
    ov = jnp.asarray(_overlap_np(n_cmp_pad, NB), BF16)
    n_sel_delta = tsel.shape[0]
    kvspec = lambda c: pl.BlockSpec((S, C_DIM), lambda b, g, t: (b, c + g))
    cspec = lambda kvi: pl.BlockSpec((1, 1, n_cmp_pad, C_DIM), lambda b, g, t: (kvi * 2 + g, b, 0, 0))
    return pl.pallas_call(
        functools.partial(_nsa_kernel, n_sel_tab=n_sel_delta, top_n=min(SLC_TOP_N, n_slc)),
        grid=(B, C_GROUPS, QT),
        in_specs=[pl.BlockSpec((TQ, C_HPG * C_DIM), lambda b, g, t: (b * QT + t, g)),
                  kvspec(0), kvspec(2), kvspec(4), kvspec(6),
                  cspec(0), cspec(1),
                  pl.BlockSpec((n_cmp_pad, NB), lambda b, g, t: (0, 0)),
                  pl.BlockSpec((n_sel_delta, C_HPG, TQ, TQ), lambda b, g, t: (0, g, 0, 0)),
                  pl.BlockSpec((twin.shape[0], C_HPG, TQ, TQ), lambda b, g, t: (0, g, 0, 0)),
                  pl.BlockSpec((1, 4, C_HPG, TQ, TQ), lambda b, g, t: (t % CMP_CLASSES, 0, g, 0, 0)),
                  pl.BlockSpec((TQ, 128), lambda b, g, t: (b * QT + t, g))],
        out_specs=pl.BlockSpec((TQ, C_HPG * C_DIM), lambda b, g, t: (b * QT + t, g)),
        out_shape=jax.ShapeDtypeStruct((N, C_W), BF16),
        scratch_shapes=[pltpu.VMEM((C_HPG * TQ, 128), F32),
                        pltpu.VMEM((C_HPG * TQ, 2 * C_DIM), F32),
                        pltpu.VMEM((C_HPG * TQ, max(n_cmp_pad, 4 * TQ)), BF16),
                        pltpu.VMEM((C_HPG * TQ, WIN + TQ), BF16),
                        pltpu.VMEM((C_HPG * TQ, 4 * TQ), F32), pltpu.VMEM((C_HPG * TQ, 4 * TQ), F32),
                        pltpu.VMEM((C_HPG * TQ, C_DIM), F32), pltpu.VMEM((C_HPG * TQ, C_DIM), F32)],
        compiler_params=_params(("parallel", "parallel", "arbitrary")),
        name="nsa",
    )(q, kv, kv, kv, kv, kvc, kvc, ov, tsel, twin, tcmp, gates)


def _layer_ab(h2, B, S, w_in, gate_b, conv_w, head_g, w_out, dil_tab):
    w_pad = jnp.pad(w_in, ((0, 0), (0, AB_PAD - AB_PROJ))).astype(BF16)
    gb_pad = jnp.pad(gate_b, (0, 128 - gate_b.shape[0])).reshape(1, 128).astype(F32)
    aq, ak, av, bq, bk, bv, bo, gates = _ab_proj(h2, w_pad, conv_w.astype(F32), gb_pad, S)
    ya = _dilated_attention(aq, ak, av, dil_tab, B, S)
    yb = _mlstm(bq, bk, bv, gates, bo, head_g, B, S)
    wo = w_out.astype(BF16)
    return [ya, yb], [wo[:A_W], wo[A_W:]]


def _layer_c(h2, B, S, w_in, gate_b, cmp_pos, cmp_w1, cmp_w2, w_out, tsel, twin, tcmp):
    w_pad, gb_pad = _c_weights(w_in, gate_b)
    q, kc, kv, gates = _c_proj(h2, w_pad, gb_pad)
    kvc = _compress(kc, cmp_pos, cmp_w1, cmp_w2, B, S)
    out = _nsa_attention(q, kv, kvc, gates, tsel, twin, tcmp, B, S)
    return [out], [w_out.astype(BF16)]


def kernel(x, rel_bias, ln_g, ln_b, ab_w_in, ab_gate_b, ab_conv, ab_head_norm, ab_w_out,
           c_w_in, c_gate_b, c_cmp_pos, c_cmp_w1, c_cmp_w2, c_w_out,
           moe_wr_g, moe_br_g, moe_wr_e, moe_br_e, moe_w_gate, moe_w_up, moe_w_down):
    B, S, D = x.shape
    assert D == D_MODEL and S % (TM) == 0 and S % (16 * A_BLOCK) == 0
    _check_cmp_windows(S)
    dil_tab = _bias_tables(rel_bias, _dilated_idx(), shift=False)
    tsel, twin, tcmp = _nsa_tables(rel_bias)
    h = x.reshape(B * S, D)
    for layer in range(DEPTH):
        j = layer // 2
        if layer % 2 == 0:
            ys, ws = _layer_ab(h, B, S, ab_w_in[j], ab_gate_b[j], ab_conv[j], ab_head_norm[j],
                               ab_w_out[j], dil_tab)
        else:
            ys, ws = _layer_c(h, B, S, c_w_in[j], c_gate_b[j], c_cmp_pos[j], c_cmp_w1[j], c_cmp_w2[j],
                              c_w_out[j], tsel, twin, tcmp)
        wr, br = _router_weights(moe_wr_g[layer], moe_br_g[layer], moe_wr_e[layer], moe_br_e[layer])
        parts = []
        rows = (B * S) // MOE_CHUNKS
        for c in range(MOE_CHUNKS):
            hx, cnt = _out_ln_route(ys, ws, h, ln_g[layer, 0], ln_b[layer, 0], wr, br, c * rows, rows)
            parts.append(_moe(hx, cnt, layer, moe_w_gate, moe_w_up, moe_w_down,
                              ln_g[layer, 1], ln_b[layer, 1]))
        h = jnp.concatenate(parts, axis=0)
    return h.reshape(B, S, D)
```

```python
import functools
import math

import numpy as np
import jax
import jax.numpy as jnp
from jax import lax
from jax.experimental import pallas as pl
from jax.experimental.pallas import tpu as pltpu

F32 = jnp.float32
BF16 = jnp.bfloat16
NEG = -1e30
LOG2E = math.log2(math.e)
VMEM_LIMIT = 48 * 1024 * 1024

D_MODEL = 1024
DEPTH = 2
ALPHA = (2.0 * DEPTH) ** 0.25
LN_EPS = 1e-5
REL_BUCKETS = 32
REL_MAX_DIST = 2048

A_HEADS, A_DIM, A_W = 8, 64, 512
A_PATTERNS = ((128, 1), (512, 4), (2048, 16))
A_BLOCK = 128
B_HEADS, B_DIM, B_W = 4, 128, 512
B_CHUNK = 128
B_CONV = 4
AB_PROJ = 3592
AB_PAD = 3712

C_HEADS, C_GROUPS, C_HPG, C_DIM, C_W = 8, 2, 4, 128, 1024
CMP_BLOCK, CMP_STRIDE, CMP_HIDDEN = 32, 16, 256
SLC_BLOCK, SLC_TOP_N, WIN = 64, 16, 512
C_PROJ = 2584
TQ = 128
CMP_PAD = 128

N_GROUPS, EPG, N_EXPERTS, E_HID = 4, 4, 16, 512
N_BUCKETS = N_GROUPS * 6
TM = 512
TM_MOE = 256
MOE_CHUNKS = 1


def _dot(a, b):
    return jnp.dot(a, b, preferred_element_type=F32)


def _dot_nt(a, b):
    return lax.dot_general(a, b, (((1,), (1,)), ((), ())), preferred_element_type=F32)


def _params(sem):
    return pltpu.CompilerParams(dimension_semantics=sem, vmem_limit_bytes=VMEM_LIMIT)


def _bucket_np(n):
    n = np.maximum(n, 0)
    exact = REL_BUCKETS // 2
    nf = np.maximum(n, 1).astype(np.float64)
    large = exact + (np.log(nf / exact) / math.log(REL_MAX_DIST / exact)
                     * (REL_BUCKETS - exact)).astype(np.int64)
    return np.where(n < exact, n, np.minimum(large, REL_BUCKETS - 1)).astype(np.int32)


def _bias_tab_kernel(tab_ref, idx_ref, out_ref, *, shift, scale):
    R = idx_ref.shape[1]
    RC = 32

    def body(i, carry):
        r0 = pl.multiple_of(i * RC, RC)
        idx = idx_ref[0, pl.ds(r0, RC), :]
        for h in range(8):
            base = tab_ref[REL_BUCKETS - 1, h] if shift else 0.0
            val = jnp.full(idx.shape, (tab_ref[0, h] - base) * scale, F32)
            for b in range(1, REL_BUCKETS):
                val = jnp.where(idx == b, (tab_ref[b, h] - base) * scale, val)
            out_ref[0, h, pl.ds(r0, RC), :] = jnp.where(idx < 0, NEG, val)
        return carry

    lax.fori_loop(0, R // RC, body, 0)


def _bias_tables(rel_bias, idx_np, shift, scale=1.0):
    T, R, C = idx_np.shape
    return pl.pallas_call(
        functools.partial(_bias_tab_kernel, shift=shift, scale=scale),
        grid=(T,),
        in_specs=[pl.BlockSpec(memory_space=pltpu.SMEM),
                  pl.BlockSpec((1, R, C), lambda t: (t, 0, 0))],
        out_specs=pl.BlockSpec((1, 8, R, C), lambda t: (t, 0, 0, 0)),
        out_shape=jax.ShapeDtypeStruct((T, 8, R, C), F32),
        compiler_params=_params(("parallel",)),
        name="bias_tables",
    )(rel_bias.astype(F32), jnp.asarray(idx_np))


def _dilated_idx():
    qi = np.arange(A_BLOCK)[:, None]
    ki = np.arange(2 * A_BLOCK)[None, :]
    j = qi + A_BLOCK - ki
    out = []
    for window, dilation in A_PATTERNS:
        nk = window // dilation
        valid = (j >= 0) & (j <= nk)
        out.append(np.where(valid, _bucket_np(np.maximum(j, 0) * dilation), -1))
    return np.stack(out).astype(np.int32)


def _sel_idx():
    a = np.arange(TQ)[:, None]
    c = np.arange(TQ)[None, :]
    n_delta = -(-(_far_dist() + TQ) // TQ)
    out = []
    for delta in range(-1, n_delta + 1):
        dist = delta * TQ + a - c
        out.append(np.where(dist >= 0, _bucket_np(dist), -1))
    return np.stack(out).astype(np.int32)


def _far_dist():
    n = np.arange(0, 4 * REL_MAX_DIST)
    b = _bucket_np(n)
    return int(np.max(n[b < REL_BUCKETS - 1])) + 1


def _win_idx():
    a = np.arange(TQ)[:, None]
    c = np.arange(TQ)[None, :]
    out = []
    for delta in range(-1, WIN // TQ + 1):
        dist = delta * TQ + a - c
        out.append(np.where((dist >= 0) & (dist < WIN), _bucket_np(dist), -1))
    return np.stack(out).astype(np.int32)


CMP_PER_TILE = TQ // CMP_STRIDE
CMP_CLASSES = TQ // CMP_PER_TILE
CMP_SPLIT = 13


def _cmp_window_start(qb):
    return qb // CMP_CLASSES + (1 if qb % CMP_CLASSES >= CMP_SPLIT else 0)


def _cmp_idx():
    a = np.arange(TQ)[:, None]
    c = np.arange(TQ)[None, :]
    out = []
    for r in range(CMP_CLASSES):
        qb = CMP_CLASSES + r
        i0 = _cmp_window_start(qb) * TQ - CMP_PAD
        for half in range(2):
            dist = qb * TQ + a - ((i0 + half * TQ + c) * CMP_STRIDE + CMP_BLOCK - 1)
            out.append(np.where(dist >= 0, _bucket_np(dist), -1))
        out.append(np.full((TQ, TQ), REL_BUCKETS - 1))
        out.append(np.full((TQ, TQ), -1))
    return np.stack(out).astype(np.int32)


def _nsa_tables(rel_bias):
    tsel = _bias_tables(rel_bias, _sel_idx(), shift=True, scale=LOG2E)
    twin = _bias_tables(rel_bias, _win_idx(), shift=False, scale=LOG2E)
    tcmp = _bias_tables(rel_bias, _cmp_idx(), shift=True, scale=LOG2E)
    return tsel, twin, tcmp.reshape(CMP_CLASSES, 4, 8, TQ, TQ)


def _check_cmp_windows(S):
    far = _far_dist()
    for qb in range(S // TQ):
        i0 = _cmp_window_start(qb) * TQ - CMP_PAD
        s0 = qb * TQ
        assert s0 - ((i0 - 1) * CMP_STRIDE + CMP_BLOCK - 1) >= far
        assert s0 + TQ - 1 - ((i0 + 2 * TQ) * CMP_STRIDE + CMP_BLOCK - 1) < 0


def _residue_col(d, r):
    return (r % 4) * 4 + r // 4 if d == 16 else r


def _lane_chunks_store(ref3, val):
    for c in range(ref3.shape[0]):
        ref3[c] = val[:, c * 128:(c + 1) * 128]


def _to_residue_layout(src3_ref, dst, d):
    nc, rows, _ = src3_ref.shape
    for r in range(d):
        cb = _residue_col(d, r)
        for c in range(nc):
            col = (cb * nc + c) * 128
            dst(slice(col, col + 128), src3_ref[c, pl.ds(r, rows // d, stride=d), :])


def _ab_proj_kernel(x_ref, xh_ref, w_ref, cw_ref, gb_ref,
                    aq_ref, ak_ref, av_ref, aq4_ref, ak4_ref, av4_ref, aq16_ref, ak16_ref, av16_ref,
                    bq_ref, bk_ref, bv_ref, bo_ref, g_ref,
                    pre_ref, tmp_ref, *, tiles_per_seq):
    i = pl.program_id(0)
    tm = x_ref.shape[0]
    xb = x_ref[...].astype(BF16)
    for c, scale, outs in ((0, A_DIM ** -0.5, (aq_ref, aq4_ref, aq16_ref)),
                           (1, 1.0, (ak_ref, ak4_ref, ak16_ref)),
                           (2, 1.0, (av_ref, av4_ref, av16_ref))):
        val = _dot(xb, w_ref[:, c * A_W:(c + 1) * A_W]) * scale
        _lane_chunks_store(tmp_ref, val)
        outs[0][...] = val.astype(BF16)
        for d, o_ref in ((4, outs[1]), (16, outs[2])):
            def put(cols, piece, o_ref=o_ref):
                o_ref[:, cols] = piece.astype(BF16)
            _to_residue_layout(tmp_ref, put, d)
    bv_ref[...] = _dot(xb, w_ref[:, 2560:3072]).astype(BF16)
    bo_ref[...] = _dot(xb, w_ref[:, 3072:3584])
    g_ref[...] = _dot(xb, w_ref[:, 3584:AB_PAD]) + gb_ref[...]
    halo = _dot(xh_ref[...].astype(BF16), w_ref[:, 1536:2560])
    halo = jnp.where(i % tiles_per_seq == 0, 0.0, halo)
    pre_ref[0:8, :] = halo
    pre_ref[8:8 + tm, :] = _dot(xb, w_ref[:, 1536:2560])
    y = pre_ref[8:8 + tm, :] * cw_ref[B_CONV - 1:B_CONV, :]
    for k in range(B_CONV - 1):
        s = B_CONV - 1 - k
        y = y + pre_ref[8 - s:8 - s + tm, :] * cw_ref[k:k + 1, :]
    y = y / (1.0 + jnp.exp(-y))
    bq_ref[...] = (y[:, :B_W] * (B_DIM ** -0.5)).astype(BF16)
    bk_ref[...] = y[:, B_W:].astype(BF16)


def _ab_proj(x2, w_pad, conv_w, gate_b_pad, S):
    N = x2.shape[0]
    tm = TM
    tps = S // tm
    row = lambda i: (i, 0)
    fix = lambda i: (0, 0)
    lay = lambda d: [jax.ShapeDtypeStruct((N // d, d * A_W), BF16)] * 3
    lay_spec = lambda d: [pl.BlockSpec((tm // d, d * A_W), row)] * 3
    outs = lay(1) + lay(4) + lay(16) + [jax.ShapeDtypeStruct((N, 512), BF16)] * 3 + [
        jax.ShapeDtypeStruct((N, 512), F32), jax.ShapeDtypeStruct((N, 128), F32)]
    o_specs = (lay_spec(1) + lay_spec(4) + lay_spec(16) + [pl.BlockSpec((tm, 512), row)] * 4
               + [pl.BlockSpec((tm, 128), row)])
    return pl.pallas_call(
        functools.partial(_ab_proj_kernel, tiles_per_seq=tps),
        grid=(N // tm,),
        in_specs=[pl.BlockSpec((tm, D_MODEL), row),
                  pl.BlockSpec((8, D_MODEL), lambda i: (jnp.maximum(i * (tm // 8) - 1, 0), 0)),
                  pl.BlockSpec((D_MODEL, AB_PAD), fix),
                  pl.BlockSpec((B_CONV, 2 * B_W), fix),
                  pl.BlockSpec((1, 128), fix)],
        out_specs=o_specs,
        out_shape=outs,
        scratch_shapes=[pltpu.VMEM((tm + 8, 2 * B_W), F32), pltpu.VMEM((A_W // 128, tm, 128), F32)],
        compiler_params=_params(("parallel",)),
        name="ab_proj",
    )(x2, x2, w_pad, conv_w, gate_b_pad)


def _dilated_kernel(*refs, has_prev, is_last):
    if has_prev:
        q_ref, kp_ref, kc_ref, vp_ref, vc_ref, tab_ref, op_ref, lp_ref = refs[:8]
        rest = refs[8:]
    else:
        q_ref, kp_ref, kc_ref, vp_ref, vc_ref, tab_ref = refs[:6]
        rest = refs[6:]
    outs, scratch = (rest, ()) if is_last else (rest[:2], rest[2:])
    o_ref = outs[0]
    n = pl.program_id(2)
    first = jnp.where(n == 0, NEG, 0.0)
    lane = lax.broadcasted_iota(jnp.int32, (A_BLOCK, 128), 1)
    keep_side = [jnp.where(lane < A_DIM, 1.0, 0.0).astype(BF16), jnp.where(lane < A_DIM, 0.0, 1.0).astype(BF16)]
    ones = jnp.ones((A_BLOCK, 128), BF16)
    m_tile = jnp.zeros((A_BLOCK, 128), F32)
    l_tile = jnp.ones((A_BLOCK, 128), F32)
    unnorm = []
    for j in range(A_HEADS // 2):
        cs = slice(j * 128, (j + 1) * 128)
        q2 = q_ref[0, :, cs]
        kp, kc, vp, vc = kp_ref[0, :, cs], kc_ref[0, :, cs], vp_ref[0, :, cs], vc_ref[0, :, cs]
        acc = None
        for side in range(2):
            h = 2 * j + side
            keep = keep_side[side]
            q = q2 * keep
            sp = _dot_nt(q, kp) + tab_ref[0, h, :, 0:A_BLOCK] + first
            sc = _dot_nt(q, kc) + tab_ref[0, h, :, A_BLOCK:2 * A_BLOCK]
            m = jnp.max(jnp.maximum(sp, sc), axis=1, keepdims=True)
            pp = jnp.exp(sp - m).astype(BF16)
            pc = jnp.exp(sc - m).astype(BF16)
            vpe = jnp.concatenate([vp * keep, ones], axis=1)
            vce = jnp.concatenate([vc * keep, ones], axis=1)
            r = _dot(pp, vpe) + _dot(pc, vce)
            acc = r[:, 0:128] if acc is None else acc + r[:, 0:128]
            m_tile = jnp.where(lane == A_DIM + h, m, m_tile)
            l_tile = jnp.where(lane == A_DIM + h, r[:, 128:256], l_tile)
        unnorm.append(acc)
    stat = (lane >= A_DIM) & (lane < A_DIM + A_HEADS)
    lse = m_tile + jnp.log(l_tile)
    if has_prev:
        lp = lp_ref[0]
        mm = jnp.maximum(lp, lse)
        wp = jnp.exp(lp - mm)
        wc = jnp.exp(lse - mm)
        tot = wp + wc
        scale_prev = jnp.where(stat, wp / tot, 0.0)
        scale_cur = jnp.where(stat, wc / (tot * l_tile), 0.0)
        lse = mm + jnp.log(tot)
    else:
        scale_cur = jnp.where(stat, 1.0 / l_tile, 0.0)
    erow = lax.broadcasted_iota(jnp.int32, (128, A_W), 0)
    ecol = lax.broadcasted_iota(jnp.int32, (128, A_W), 1)
    expand = jnp.where(erow - A_DIM == jnp.right_shift(ecol, A_DIM.bit_length() - 1), 1.0, 0.0).astype(BF16)

    def spread(t):
        hi, mid, lo = _split3(t)
        return _dot(hi, expand) + _dot(mid, expand) + _dot(lo, expand)

    o = jnp.concatenate(unnorm, axis=1) * spread(scale_cur)
    if has_prev:
        o = o + op_ref[0] * spread(scale_prev)
    if is_last:
        o_ref[0] = o.astype(o_ref.dtype)
    else:
        o_scr, l_scr = scratch
        _lane_chunks_store(o_scr, o)
        l_scr[0] = jnp.where(stat, lse, 0.0)

        def put_o(cols, piece):
            o_ref[0, :, cols] = piece

        def put_l(cols, piece):
            outs[1][0, :, cols] = piece

        _to_residue_layout(o_scr, put_o, 4)
        _to_residue_layout(l_scr, put_l, 4)


def _dilated_call(q, k, v, tab, prev, pattern_idx, dilation, B, S, is_last):
    d = dilation
    L = S // d
    nb = L // A_BLOCK
    r3 = lambda a: a.reshape(B, L, a.shape[-1])
    cur = lambda b, r, n: (b, n, r)
    prv = lambda b, r, n: (b, jnp.maximum(n - 1, 0), r)
    blk = pl.BlockSpec((1, A_BLOCK, A_W), cur)
    in_specs = [blk, pl.BlockSpec((1, A_BLOCK, A_W), prv), blk,
                pl.BlockSpec((1, A_BLOCK, A_W), prv), blk,
                pl.BlockSpec((1, 8, A_BLOCK, 2 * A_BLOCK), lambda b, r, n: (pattern_idx, 0, 0, 0))]
    args = [r3(q), r3(k), r3(k), r3(v), r3(v), tab]
    has_prev = prev is not None
    if has_prev:
        in_specs += [blk, pl.BlockSpec((1, A_BLOCK, 128), cur)]
        args += [r3(prev[0]), r3(prev[1])]
    scratch = []
    if is_last:
        out_shape = [jax.ShapeDtypeStruct((B, L, d * A_W), BF16)]
        out_specs = [blk]
    else:
        d2, rows = 4 * d, A_BLOCK // 4
        nxt = lambda b, r, n: (b, n, r)
        out_shape = [jax.ShapeDtypeStruct((B, S // d2, d2 * A_W), F32),
                     jax.ShapeDtypeStruct((B, S // d2, d2 * 128), F32)]
        out_specs = [pl.BlockSpec((1, rows, 4 * A_W), nxt), pl.BlockSpec((1, rows, 4 * 128), nxt)]
        scratch = [pltpu.VMEM((A_W // 128, A_BLOCK, 128), F32), pltpu.VMEM((1, A_BLOCK, 128), F32)]
    res = pl.pallas_call(
        functools.partial(_dilated_kernel, has_prev=has_prev, is_last=is_last),
        grid=(B, d, nb),
        in_specs=in_specs, out_specs=out_specs, out_shape=out_shape, scratch_shapes=scratch,
        compiler_params=_params(("parallel", "parallel", "arbitrary")),
        name="dilated_d%d" % d,
    )(*args)
    return [r.reshape(-1, r.shape[-1]) for r in res]


def _dilated_attention(qkv_by_dilation, tab, B, S):
    prev = None
    for p, (window, d) in enumerate(A_PATTERNS):
        assert window // d == A_BLOCK and S % (d * A_BLOCK) == 0
        assert p == 0 or d == 4 * A_PATTERNS[p - 1][1]
        last = p == len(A_PATTERNS) - 1
        q, k, v = qkv_by_dilation[d]
        prev = _dilated_call(q, k, v, tab, prev, p, d, B, S, last)
    return prev[0]


def _split3(x):
    hi = x.astype(BF16)
    r = x - hi.astype(F32)
    mid = r.astype(BF16)
    lo = (r - mid.astype(F32)).astype(BF16)
    return hi, mid, lo


def _mlstm_kernel(q_ref, k_ref, v_ref, g_ref, bo_ref, hg_ref, y_ref, c_ref, m_ref):
    L = B_CHUNK
    c = pl.program_id(1)

    @pl.when(c == 0)
    def _():
        c_ref[...] = jnp.zeros_like(c_ref)
        m_ref[...] = jnp.zeros_like(m_ref)

    lane = lax.broadcasted_iota(jnp.int32, (L, 128), 1)
    row = lax.broadcasted_iota(jnp.int32, (L, L), 0)
    col = lax.broadcasted_iota(jnp.int32, (L, L), 1)
    tri = row >= col
    g = g_ref[0]
    is_f = (lane >= B_HEADS) & (lane < 2 * B_HEADS)
    logf = jnp.minimum(g, 0.0) - jnp.log(1.0 + jnp.exp(-jnp.abs(g)))
    gl = jnp.where(is_f, logf, jnp.where(lane < B_HEADS, g, 0.0))
    tril = jnp.where(tri, 1.0, 0.0).astype(BF16)
    hi, mid, lo = _split3(gl)
    cum = _dot(tril, hi) + _dot(tril, mid) + _dot(tril, lo)
    cum_t = cum.T
    gl_t = gl.T
    ones = jnp.ones((L, B_DIM), BF16)
    for h in range(B_HEADS):
        cs = slice(h * B_DIM, (h + 1) * B_DIM)
        q = q_ref[0, :, cs]
        k = k_ref[0, :, cs]
        v_ext = jnp.concatenate([v_ref[0, :, cs], ones], axis=1)
        b_col = cum[:, B_HEADS + h:B_HEADS + h + 1]
        b_row = cum_t[B_HEADS + h:B_HEADS + h + 1, :]
        i_col = gl[:, h:h + 1]
        i_row = gl_t[h:h + 1, :]
        m_prev = m_ref[h, 0:1, 0:1]
        dm = jnp.where(tri, b_col - b_row + i_row, NEG)
        inter = b_col + m_prev
        m_t = jnp.maximum(inter, jnp.max(dm, axis=1, keepdims=True))
        p = jnp.exp(dm - m_t)
        sqk = _dot_nt(q, k) * p
        sc = jnp.exp(inter - m_t)
        lhs = jnp.concatenate([(sc * q.astype(F32)).astype(BF16), sqk.astype(BF16)], axis=1)
        c_ext = c_ref[h]
        rhs = jnp.concatenate([c_ext.astype(BF16), v_ext], axis=0)
        res = _dot(lhs, rhs)
        num = res[:, :B_DIM]
        den = res[:, B_DIM:]
        hh = num / jnp.maximum(jnp.abs(den), jnp.exp(-m_t))
        b_last = b_col[L - 1:L, :]
        gk = b_last - b_col + i_col
        m_new = jnp.maximum(b_last + m_prev, jnp.max(gk, axis=0, keepdims=True))
        wk = jnp.exp(gk - m_new)
        decay = jnp.exp(b_last + m_prev - m_new)
        kw_t = (wk * k.astype(F32)).T.astype(BF16)
        c_ref[h] = decay * c_ext + _dot(kw_t, v_ext)
        m_ref[h] = jnp.broadcast_to(m_new, (8, 128))
        mu = jnp.mean(hh, axis=1, keepdims=True)
        xc = hh - mu
        var = jnp.mean(xc * xc, axis=1, keepdims=True)
        hn = xc * lax.rsqrt(var + LN_EPS) * hg_ref[:, cs]
        bo = bo_ref[0, :, cs]
        y_ref[0, :, cs] = (hn / (1.0 + jnp.exp(-bo))).astype(BF16)


def _mlstm(bq, bk, bv, gates, bo, head_g, B, S):
    nc = S // B_CHUNK
    r3 = lambda a: a.reshape(B, S, a.shape[-1])
    blk = pl.BlockSpec((1, B_CHUNK, B_W), lambda b, c: (b, c, 0))
    y = pl.pallas_call(
        _mlstm_kernel,
        grid=(B, nc),
        in_specs=[blk, blk, blk,
                  pl.BlockSpec((1, B_CHUNK, 128), lambda b, c: (b, c, 0)),
                  blk,
                  pl.BlockSpec((1, B_W), lambda b, c: (0, 0))],
        out_specs=blk,
        out_shape=jax.ShapeDtypeStruct((B, S, B_W), BF16),
        scratch_shapes=[pltpu.VMEM((B_HEADS, B_DIM, 2 * B_DIM), F32),
                        pltpu.VMEM((B_HEADS, 8, 128), F32)],
        compiler_params=_params(("parallel", "arbitrary")),
        name="mlstm",
    )(r3(bq), r3(bk), r3(bv), r3(gates), r3(bo), head_g.reshape(1, B_W).astype(F32))
    return y.reshape(B * S, B_W)


def _layer_norm(z, g, b):
    mu = jnp.mean(z, axis=1, keepdims=True)
    zc = z - mu
    var = jnp.mean(zc * zc, axis=1, keepdims=True)
    return zc * lax.rsqrt(var + LN_EPS) * g + b


def _route(logits, cnt_ref):
    tm = logits.shape[0]
    lt = logits.T
    col = lambda c: lt[c:c + 1, :]
    gl = [col(c) for c in range(N_GROUPS)]
    gmax = functools.reduce(jnp.maximum, gl)
    gsum = sum(jnp.exp(x - gmax) for x in gl)
    g_idx = jnp.full(gmax.shape, N_GROUPS - 1, jnp.int32)
    for c in range(N_GROUPS - 2, -1, -1):
        g_idx = jnp.where(gl[c] == gmax, c, g_idx)
    g_w = 1.0 / gsum
    el = []
    for k in range(EPG):
        x = col(N_GROUPS + (N_GROUPS - 1) * EPG + k)
        for g in range(N_GROUPS - 2, -1, -1):
            x = jnp.where(g_idx == g, col(N_GROUPS + g * EPG + k), x)
        el.append(x)
    v1 = functools.reduce(jnp.maximum, el)
    i1 = jnp.full(v1.shape, EPG - 1, jnp.int32)
    for k in range(EPG - 2, -1, -1):
        i1 = jnp.where(el[k] == v1, k, i1)
    el2 = [jnp.where(i1 == k, -jnp.inf, el[k]) for k in range(EPG)]
    v2 = functools.reduce(jnp.maximum, el2)
    i2 = jnp.full(v2.shape, EPG - 1, jnp.int32)
    for k in range(EPG - 2, -1, -1):
        i2 = jnp.where((el2[k] == v2) & (i1 != k), k, i2)
    t = jnp.exp(v2 - v1)
    w1 = g_w / (1.0 + t)
    w2 = w1 * t
    a = jnp.minimum(i1, i2)
    b = jnp.maximum(i1, i2)
    pair = jnp.where(a == 0, b - 1, jnp.where(a == 1, b + 1, 5))
    bucket = (g_idx * 6 + pair).astype(F32)
    w_lo = jnp.where(i1 < i2, w1, w2)
    w_hi = jnp.where(i1 < i2, w2, w1)
    sub = lax.broadcasted_iota(jnp.int32, (128, tm), 0)
    onehot_t = jnp.where(sub.astype(F32) == bucket, 1.0, 0.0)
    srow = lax.broadcasted_iota(jnp.int32, (tm, tm), 0)
    scol = lax.broadcasted_iota(jnp.int32, (tm, tm), 1)
    before = jnp.where(srow < scol, 1.0, 0.0).astype(BF16)
    oh = onehot_t.astype(BF16)
    carry = cnt_ref[...]
    prior = _dot(oh, before) + jnp.concatenate([carry] * (tm // 128), axis=1)
    rank = jnp.sum(onehot_t * prior, axis=0, keepdims=True)
    cnt_ref[...] = carry + _dot(oh, jnp.ones((tm, 128), BF16))
    out_t = jnp.where(sub == 0, bucket, jnp.where(sub == 1, w_lo, jnp.where(sub == 2, w_hi,
                      jnp.where(sub == 3, rank, 0.0))))
    return out_t.T


def _out_ln_route_kernel(*refs, n_in, layouts):
    y_refs = refs[:n_in]
    w_refs = refs[n_in:2 * n_in]
    x_ref, g_ref, b_ref, wrh_ref, wrl_ref, br_ref, h_ref, c_ref, cnt_ref = refs[2 * n_in:2 * n_in + 9]
    pos_refs = list(refs[2 * n_in + 9:])
    tm = x_ref.shape[0]

    @pl.when(pl.program_id(0) == 0)
    def _():
        cnt_ref[...] = jnp.zeros_like(cnt_ref)

    y = None
    for i in range(n_in):
        d = layouts[i]
        if d == 1:
            lhs = y_refs[i][...]
        else:
            s_ref = pos_refs.pop(0)
            nc = s_ref.shape[0]
            for r in range(d):
                cb = _residue_col(d, r)
                for c in range(nc):
                    col = (cb * nc + c) * 128
                    s_ref[c, pl.ds(r, tm // d, stride=d), :] = y_refs[i][:, col:col + 128].astype(F32)
            lhs = jnp.concatenate([s_ref[c] for c in range(nc)], axis=1).astype(BF16)
        t = _dot(lhs, w_refs[i][...])
        y = t if y is None else y + t
    hn = _layer_norm(ALPHA * x_ref[...] + y, g_ref[...], b_ref[...])
    h_ref[:, 0:D_MODEL] = hn
    hi = hn.astype(BF16)
    lo = (hn - hi.astype(F32)).astype(BF16)
    logits = (_dot_nt(hi, wrh_ref[...]) + _dot_nt(lo, wrh_ref[...]) + _dot_nt(hi, wrl_ref[...])
              + br_ref[...])
    h_ref[:, D_MODEL:D_MODEL + 128] = _route(logits, cnt_ref)
    c_ref[...] = cnt_ref[...]


def _out_ln_route(ys, layouts, ws, x2, ln_g, ln_b, wr, br, row_start, N):
    tm = TM
    off = row_start // tm
    row = lambda i: (i + off, 0)
    fix = lambda i: (0, 0)
    wr_hi = wr.astype(BF16)
    wr_lo = (wr - wr_hi.astype(F32)).astype(BF16)
    in_specs = ([pl.BlockSpec((tm // d, y.shape[1]), row) for y, d in zip(ys, layouts)]
                + [pl.BlockSpec(w.shape, fix) for w in ws]
                + [pl.BlockSpec((tm, D_MODEL), row),
                   pl.BlockSpec((1, D_MODEL), fix), pl.BlockSpec((1, D_MODEL), fix),
                   pl.BlockSpec((128, D_MODEL), fix), pl.BlockSpec((128, D_MODEL), fix),
                   pl.BlockSpec((1, 128), fix)])
    return pl.pallas_call(
        functools.partial(_out_ln_route_kernel, n_in=len(ys), layouts=tuple(layouts)),
        grid=(N // tm,),
        in_specs=in_specs,
        out_specs=[pl.BlockSpec((tm, D_MODEL + 128), lambda i: (i, 0)), pl.BlockSpec((128, 128), fix)],
        out_shape=[jax.ShapeDtypeStruct((N, D_MODEL + 128), F32), jax.ShapeDtypeStruct((128, 128), F32)],
        scratch_shapes=[pltpu.VMEM((128, 128), F32)] + [
            pltpu.VMEM((y.shape[1] // d // 128, tm, 128), F32) for y, d in zip(ys, layouts) if d != 1],
        compiler_params=_params(("arbitrary",)),
        name="out_ln_route",
    )(*ys, *ws, x2, ln_g.reshape(1, -1), ln_b.reshape(1, -1), wr_hi, wr_lo, br)


def _router_weights(wr_g, br_g, wr_e, br_e):
    we = wr_e.transpose(0, 2, 1).reshape(N_GROUPS * EPG, D_MODEL)
    w = jnp.concatenate([wr_g.T, we], axis=0)
    w = jnp.pad(w, ((0, 128 - w.shape[0]), (0, 0)))
    b = jnp.concatenate([br_g, br_e.reshape(-1)])
    b = jnp.pad(b, (0, 128 - b.shape[0])).reshape(1, 128)
    return w.astype(F32), b.astype(F32)


_PAIRS = ((0, 1), (0, 2), (0, 3), (1, 2), (1, 3), (2, 3))


def _moe_kernel(elo_ref, ehi_ref, chg_ref, nt_ref,
                x_ref, wgl_ref, wul_ref, wdl_ref, wgh_ref, wuh_ref, wdh_ref,
                g_ref, b_ref, o_ref, wg_s, wu_s, wd_s):
    t = pl.program_id(0)

    @pl.when(chg_ref[t] == 1)
    def _():
        wg_s[0] = wgl_ref[0, 0].astype(BF16)
        wu_s[0] = wul_ref[0, 0].astype(BF16)
        wd_s[0] = wdl_ref[0, 0].astype(BF16)
        wg_s[1] = wgh_ref[0, 0].astype(BF16)
        wu_s[1] = wuh_ref[0, 0].astype(BF16)
        wd_s[1] = wdh_ref[0, 0].astype(BF16)

    @pl.when(t < nt_ref[0])
    def _():
        x = x_ref[:, 0:D_MODEL]
        xb = x.astype(BF16)
        r = x_ref[:, D_MODEL:D_MODEL + 128]
        acc = None
        for e in range(2):
            a = _dot(xb, wg_s[e])
            u = _dot(xb, wu_s[e])
            hcur = (a / (1.0 + jnp.exp(-a))) * u * r[:, 1 + e:2 + e]
            y = _dot(hcur.astype(BF16), wd_s[e])
            acc = y if acc is None else acc + y
        o_ref[...] = _layer_norm(ALPHA * x + acc, g_ref[...], b_ref[...])

    @pl.when(t >= nt_ref[0])
    def _():
        o_ref[...] = jnp.zeros_like(o_ref)


def _moe(hx, cnt, layer, w_gate, w_up, w_down, ln_g, ln_b):
    N = hx.shape[0]
    tm = TM_MOE
    n_tiles = N // tm + N_BUCKETS
    n_pad = n_tiles * tm
    bucket = hx[:, D_MODEL].astype(jnp.int32)
    rank = hx[:, D_MODEL + 3].astype(jnp.int32)
    counts = cnt[:N_BUCKETS, 0].astype(jnp.int32)
    padded = ((counts + tm - 1) // tm) * tm
    ends = jnp.cumsum(padded)
    offs = ends - padded
    dest = offs[bucket] + rank
    src = jnp.zeros((n_pad,), jnp.int32).at[dest].set(
        jnp.arange(N, dtype=jnp.int32), mode="promise_in_bounds", unique_indices=True)
    tile_start = jnp.arange(n_tiles, dtype=jnp.int32) * tm
    n_used = (ends[-1] // tm).astype(jnp.int32)
    tb = jnp.sum((tile_start[:, None] >= ends[None, :]).astype(jnp.int32), axis=1)
    tb_last = jnp.take(tb, jnp.maximum(n_used - 1, 0))
    tb = jnp.where(tile_start < ends[-1], tb, tb_last)
    pairs = jnp.asarray(_PAIRS, jnp.int32)
    elo = (tb // 6) * EPG + pairs[tb % 6, 0]
    ehi = (tb // 6) * EPG + pairs[tb % 6, 1]
    chg = jnp.concatenate([jnp.ones((1,), jnp.int32), (tb[1:] != tb[:-1]).astype(jnp.int32)])
    xs = hx.at[src].get(mode="promise_in_bounds")

    row = lambda t, *_: (t, 0)
    fix = lambda t, *_: (0, 0)
    wlo = lambda t, elo, ehi, chg, nt: (layer, elo[t], 0, 0)
    whi = lambda t, elo, ehi, chg, nt: (layer, ehi[t], 0, 0)
    up_spec = lambda im: pl.BlockSpec((1, 1, D_MODEL, E_HID), im)
    dn_spec = lambda im: pl.BlockSpec((1, 1, E_HID, D_MODEL), im)
    grid_spec = pltpu.PrefetchScalarGridSpec(
        num_scalar_prefetch=4,
        grid=(n_tiles,),
        in_specs=[pl.BlockSpec((tm, D_MODEL + 128), row),
                  up_spec(wlo), up_spec(wlo), dn_spec(wlo),
                  up_spec(whi), up_spec(whi), dn_spec(whi),
                  pl.BlockSpec((1, D_MODEL), fix), pl.BlockSpec((1, D_MODEL), fix)],
        out_specs=pl.BlockSpec((tm, D_MODEL), row),
        scratch_shapes=[pltpu.VMEM((2, D_MODEL, E_HID), BF16),
                        pltpu.VMEM((2, D_MODEL, E_HID), BF16),
                        pltpu.VMEM((2, E_HID, D_MODEL), BF16)])
    out_sorted = pl.pallas_call(
        _moe_kernel,
        grid_spec=grid_spec,
        out_shape=jax.ShapeDtypeStruct((n_pad, D_MODEL), F32),
        compiler_params=_params(("arbitrary",)),
        name="moe",
    )(elo, ehi, chg, n_used.reshape(1), xs, w_gate, w_up, w_down, w_gate, w_up, w_down,
      ln_g.reshape(1, -1), ln_b.reshape(1, -1))
    return out_sorted.at[dest].get(mode="promise_in_bounds", unique_indices=True)


def _c_proj_kernel(x_ref, w_ref, gb_ref, q_ref, kc_ref, kv_ref, g_ref):
    xb = x_ref[...].astype(BF16)
    q_ref[...] = (_dot(xb, w_ref[:, 0:C_W]) * (C_DIM ** -0.5 * LOG2E)).astype(BF16)
    for i in range(4):
        kc_ref[i] = _dot(xb, w_ref[:, C_W + i * 128:C_W + (i + 1) * 128]).astype(BF16)
    kv_ref[...] = _dot(xb, w_ref[:, C_W + 512:C_W + 1536]).astype(BF16)
    z = _dot(xb, w_ref[:, C_W + 1536:C_W + 1792]) + gb_ref[...]
    g_ref[...] = 1.0 / (1.0 + jnp.exp(-z))


def _c_proj(x2, w_pad, gb_pad):
    N = x2.shape[0]
    tm = TM
    row = lambda i: (i, 0)
    fix = lambda i: (0, 0)
    wcols = w_pad.shape[1]
    return pl.pallas_call(
        _c_proj_kernel,
        grid=(N // tm,),
        in_specs=[pl.BlockSpec((tm, D_MODEL), row), pl.BlockSpec((D_MODEL, wcols), fix),
                  pl.BlockSpec((1, 256), fix)],
        out_specs=[pl.BlockSpec((tm, C_W), row), pl.BlockSpec((4, tm, 128), lambda i: (0, i, 0)),
                   pl.BlockSpec((tm, 1024), row), pl.BlockSpec((tm, 256), row)],
        out_shape=[jax.ShapeDtypeStruct((N, C_W), BF16), jax.ShapeDtypeStruct((4, N, 128), BF16),
                   jax.ShapeDtypeStruct((N, 1024), BF16), jax.ShapeDtypeStruct((N, 256), F32)],
        compiler_params=_params(("parallel",)),
        name="c_proj",
    )(x2, w_pad, gb_pad)


def _c_weights(w_in, gate_b):
    gcols = []
    gb = []
    for g in range(C_GROUPS):
        idx = [C_PROJ - 3 * C_HEADS + br * C_HEADS + g * C_HPG + j for br in range(3) for j in range(C_HPG)]
        gcols.append(jnp.pad(w_in[:, np.asarray(idx)], ((0, 0), (0, 128 - len(idx)))))
        gb.append(jnp.pad(gate_b[np.asarray(idx) - (C_PROJ - 3 * C_HEADS)], (0, 128 - len(idx))))
    w = jnp.concatenate([w_in[:, :C_PROJ - 3 * C_HEADS]] + gcols, axis=1).astype(BF16)
    return w, jnp.concatenate(gb).reshape(1, 256).astype(F32)


def _compress_kernel(seg_ref, w1_ref, pos_ref, w1f_ref, w2_ref, o_ref):
    n_seg = seg_ref.shape[1]
    ul = _dot(seg_ref[0], w1_ref[0])
    u = ul[:, :CMP_HIDDEN]
    lnext = pltpu.roll(ul[:, CMP_HIDDEN:], n_seg - 1, 0)
    cpos = _dot(pos_ref[0], w1f_ref[0])[0:1, :]
    pre = u + lnext + cpos
    act = 0.5 * pre * (1.0 + jnp.tanh(math.sqrt(2.0 / math.pi) * (pre + 0.044715 * pre * pre * pre)))
    o_ref[0, 0, 0:CMP_PAD, :] = jnp.zeros((CMP_PAD, C_DIM), BF16)
    o_ref[0, 0, CMP_PAD:CMP_PAD + n_seg, :] = _dot(act.astype(BF16), w2_ref[0]).astype(BF16)


def _compress(kc, cmp_pos, cmp_w1, cmp_w2, B, S):
    n_seg = S // CMP_STRIDE
    half = CMP_STRIDE * C_DIM
    seg = kc.reshape(4 * B, n_seg, half)
    w1 = cmp_w1.astype(BF16)
    w1_ul = jnp.concatenate([w1[:, :half], w1[:, half:]], axis=2)
    pos = jnp.broadcast_to(cmp_pos.reshape(2, 1, CMP_BLOCK * C_DIM), (2, 8, CMP_BLOCK * C_DIM)).astype(BF16)
    out = pl.pallas_call(
        _compress_kernel,
        grid=(4, B),
        in_specs=[pl.BlockSpec((1, n_seg, half), lambda i, b: (i * B + b, 0, 0)),
                  pl.BlockSpec((1, half, 2 * CMP_HIDDEN), lambda i, b: (i // 2, 0, 0)),
                  pl.BlockSpec((1, 8, CMP_BLOCK * C_DIM), lambda i, b: (i // 2, 0, 0)),
                  pl.BlockSpec((1, CMP_BLOCK * C_DIM, CMP_HIDDEN), lambda i, b: (i // 2, 0, 0)),
                  pl.BlockSpec((1, CMP_HIDDEN, C_DIM), lambda i, b: (i // 2, 0, 0))],
        out_specs=pl.BlockSpec((1, 1, CMP_PAD + n_seg, C_DIM), lambda i, b: (i, b, 0, 0)),
        out_shape=jax.ShapeDtypeStruct((4, B, CMP_PAD + n_seg, C_DIM), BF16),
        compiler_params=_params(("parallel", "parallel")),
        name="compress",
    )(seg, w1_ul, pos, w1, cmp_w2.astype(BF16))
    return out


def _overlap_np(n_cmp_pad, n_slc):
    i = np.arange(n_cmp_pad)[:, None] - CMP_PAD
    m = np.arange(n_slc)[None, :]
    start = i * CMP_STRIDE
    ov = (start < (m + 1) * SLC_BLOCK) & (start + CMP_BLOCK - 1 >= m * SLC_BLOCK) & (i >= 0)
    return ov.astype(np.float32)


def _flash_update(s_heads, vext, m_ref, acc_ref, p_ref):
    W = s_heads[0].shape[1]
    nw = W // 128
    alphas = []
    for h, s in enumerate(s_heads):
        rs = slice(h * TQ, (h + 1) * TQ)
        m_prev = m_ref[rs, :]
        smax = functools.reduce(jnp.maximum, [s[:, i * 128:(i + 1) * 128] for i in range(nw)])
        m_new = jnp.maximum(m_prev, jnp.max(smax, axis=1, keepdims=True))
        alpha = jnp.exp2(m_prev - m_new)
        p = jnp.exp2(s - (m_new if nw == 1 else jnp.concatenate([m_new] * nw, axis=1)))
        m_ref[rs, :] = m_new
        p_ref[rs, 0:W] = p.astype(BF16)
        alphas.append(alpha)
    a = jnp.concatenate(alphas, axis=0)
    acc_ref[...] = jnp.concatenate([a, a], axis=1) * acc_ref[...] + _dot(p_ref[:, 0:W], vext)


def _nsa_kernel(q_ref, ks_ref, vs_ref, kw_ref, vw_ref, kc_ref, vc_ref, ov_ref,
                tsel_ref, twin_ref, tcmp_ref, g_ref, o_ref,
                m_ref, acc_ref, p_ref, pw_ref, sa_ref, sb_ref, oc_ref, ow_ref, *, n_sel_tab, top_n):
    qb = pl.program_id(2)
    H = C_HPG
    NB = ov_ref.shape[1]
    q_all = jnp.concatenate([q_ref[:, h * C_DIM:(h + 1) * C_DIM] for h in range(H)], axis=0)
    heads = lambda x: [x[h * TQ:(h + 1) * TQ] for h in range(H)]

    def reset():
        m_ref[...] = jnp.full(m_ref.shape, NEG, F32)
        acc_ref[...] = jnp.zeros(acc_ref.shape, F32)

    def softmax_rows(s):
        nw = s.shape[1] // 128
        smax = functools.reduce(jnp.maximum, [s[:, i * 128:(i + 1) * 128] for i in range(nw)])
        m = jnp.broadcast_to(jnp.max(smax, axis=1, keepdims=True), (TQ, 128))
        return jnp.exp2(s - jnp.concatenate([m] * nw, axis=1)), m

    t0 = qb // CMP_CLASSES + jnp.where(qb % CMP_CLASSES >= CMP_SPLIT, 1, 0)
    n_ct = kc_ref.shape[2] // TQ
    wc = n_ct * TQ
    tile_kind = [3] + [jnp.where(t == t0, 0, jnp.where(t == t0 + 1, 1, jnp.where(t < t0, 2, 3)))
                       for t in range(1, n_ct)]
    s_c = _dot_nt(q_all, kc_ref[0, 0])
    vext_c = jnp.concatenate([vc_ref[0, 0], ov_ref[...]], axis=1)
    inv_c = []
    for h in range(H):
        bias = jnp.concatenate([tcmp_ref[0, tile_kind[t], h] for t in range(n_ct)], axis=1)
        p, m = softmax_rows(s_c[h * TQ:(h + 1) * TQ] + bias)
        l = jnp.sum(p, axis=1, keepdims=True)
        inv_c.append(jnp.where(m > 0.5 * NEG, 1.0 / l, 0.0))
        p_ref[h * TQ:(h + 1) * TQ, 0:wc] = p.astype(BF16)
    res_c = _dot(p_ref[:, 0:wc], vext_c)
    imp = None
    for h in range(H):
        r = res_c[h * TQ:(h + 1) * TQ] * jnp.concatenate([inv_c[h], inv_c[h]], axis=1)
        oc_ref[h * TQ:(h + 1) * TQ, :] = r[:, 0:C_DIM]
        imp = r[:, C_DIM:] if imp is None else imp + r[:, C_DIM:]

    n_wt = WIN // TQ + 1
    ww = n_wt * TQ
    st = jnp.maximum(qb - (n_wt - 1), 0)
    r0w = pl.multiple_of(st * TQ, TQ)
    s_w = _dot_nt(q_all, kw_ref[pl.ds(r0w, ww), :])
    vext_w = jnp.concatenate([vw_ref[pl.ds(r0w, ww), :], jnp.ones((ww, C_DIM), BF16)], axis=1)
    widx = [jnp.maximum(qb - (st + c) + 1, 0) for c in range(n_wt)]
    for h in range(H):
        bias = jnp.concatenate([twin_ref[i, h] for i in widx], axis=1)
        p, _ = softmax_rows(s_w[h * TQ:(h + 1) * TQ] + bias)
        pw_ref[h * TQ:(h + 1) * TQ, :] = p.astype(BF16)
    res_w = _dot(pw_ref[...], vext_w)
    ow_ref[...] = res_w[:, 0:C_DIM] / res_w[:, C_DIM:]

    shift = SLC_BLOCK.bit_length() - 1
    qpos = qb * TQ + lax.broadcasted_iota(jnp.int32, (TQ, NB), 0)
    mblk = lax.broadcasted_iota(jnp.int32, (TQ, NB), 1)
    qblk = jnp.right_shift(qpos, shift)
    forced = (mblk == 0) | (mblk == qblk) | (mblk == qblk - 1)
    score = jnp.where(forced, 3e38, jnp.where(jnp.left_shift(mblk, shift) <= qpos, imp, NEG))
    score_t = score.T
    blk_t = lax.broadcasted_iota(jnp.int32, (NB, TQ), 0).astype(F32)
    sel_t = jnp.zeros((NB, TQ), F32)
    for _ in range(top_n):
        mx = jnp.max(score_t, axis=0, keepdims=True)
        idx = jnp.min(jnp.where(score_t == mx, blk_t, float(NB)), axis=0, keepdims=True)
        pick = blk_t == idx
        sel_t = jnp.where(pick, 1.0, sel_t)
        score_t = jnp.where(pick, -3e38, score_t)
    sel = sel_t.T

    reset()
    NT = 4
    TK = NT * TQ
    bpk = TK // SLC_BLOCK
    erow = lax.broadcasted_iota(jnp.int32, (NB, TK), 0)
    ecol = jnp.right_shift(lax.broadcasted_iota(jnp.int32, (NB, TK), 1), shift)
    expand0 = jnp.where(erow == ecol, 1.0, 0.0).astype(BF16)
    ones_k = jnp.ones((TK, C_DIM), BF16)
    n_steps = qb // NT + 1

    def sel_logits(kq, s_ref, near):
        kc = jnp.minimum(kq, n_steps - 1)
        r0 = pl.multiple_of(kc * TK, TK)
        s = _dot_nt(q_all, ks_ref[pl.ds(r0, TK), :])
        sel_k = pltpu.roll(sel, (NB - kc * bpk) % NB, 1).astype(BF16)
        addmask = _dot(sel_k, expand0) * (-NEG) + NEG
        if near:
            addmask = jnp.where(kq < n_steps, addmask, NEG)
            idx = [jnp.clip(qb - (NT * kc + c) + 1, 0, n_sel_tab - 1) for c in range(NT)]
        for h in range(H):
            bias = addmask
            if near:
                bias = bias + jnp.concatenate([tsel_ref[i, h] for i in idx], axis=1)
            s_ref[h * TQ:(h + 1) * TQ, :] = s[h * TQ:(h + 1) * TQ] + bias

    def sel_consume(kq, s_ref):
        r0 = pl.multiple_of(jnp.minimum(kq, n_steps - 1) * TK, TK)
        vext = jnp.concatenate([vs_ref[pl.ds(r0, TK), :], ones_k], axis=1)
        _flash_update([s_ref[h * TQ:(h + 1) * TQ, :] for h in range(H)], vext, m_ref, acc_ref, p_ref)

    def sel_run(k_lo, n2, near):
        @pl.when(n2 > 0)
        def _():
            sel_logits(k_lo, sa_ref, near)

        def body(j, carry):
            k = k_lo + 2 * j
            sel_logits(k + 1, sb_ref, near)
            sel_consume(k, sa_ref)
            sel_logits(k + 2, sa_ref, near)
            sel_consume(k + 1, sb_ref)
            return carry

        lax.fori_loop(0, n2, body, 0)

    n_far = jnp.maximum((qb + 1 - (n_sel_tab - 2)) // NT, 0)
    far2 = n_far // 2
    sel_run(0, far2, near=False)
    sel_run(2 * far2, (n_steps - 2 * far2 + 1) // 2, near=True)
    g = g_ref[...]
    for h in range(H):
        rs = slice(h * TQ, (h + 1) * TQ)
        out_s = acc_ref[rs, 0:C_DIM] / acc_ref[rs, C_DIM:2 * C_DIM]
        o = (g[:, h:h + 1] * oc_ref[rs, :] + g[:, H + h:H + h + 1] * out_s
             + g[:, 2 * H + h:2 * H + h + 1] * ow_ref[rs, :])
        o_ref[:, h * C_DIM:(h + 1) * C_DIM] = o.astype(BF16)


def _nsa_attention(q, kv, kvc, gates, tsel, twin, tcmp, B, S):
    N = B * S
    QT = S // TQ
    n_slc = S // SLC_BLOCK
    n_cmp_pad = kvc.shape[2]
    NB = 128
    assert n_slc <= NB and n_cmp_pad % TQ == 0 and QT % 4 == 0
    ov = jnp.asarray(_overlap_np(n_cmp_pad, NB), BF16)
    n_sel_delta = tsel.shape[0]
    kvspec = lambda c: pl.BlockSpec((S, C_DIM), lambda b, g, t: (b, c + g))
    cspec = lambda kvi: pl.BlockSpec((1, 1, n_cmp_pad, C_DIM), lambda b, g, t: (kvi * 2 + g, b, 0, 0))
    return pl.pallas_call(
        functools.partial(_nsa_kernel, n_sel_tab=n_sel_delta, top_n=min(SLC_TOP_N, n_slc)),
        grid=(B, C_GROUPS, QT),
        in_specs=[pl.BlockSpec((TQ, C_HPG * C_DIM), lambda b, g, t: (b * QT + t, g)),
                  kvspec(0), kvspec(2), kvspec(4), kvspec(6),
                  cspec(0), cspec(1),
                  pl.BlockSpec((n_cmp_pad, NB), lambda b, g, t: (0, 0)),
                  pl.BlockSpec((n_sel_delta, C_HPG, TQ, TQ), lambda b, g, t: (0, g, 0, 0)),
                  pl.BlockSpec((twin.shape[0], C_HPG, TQ, TQ), lambda b, g, t: (0, g, 0, 0)),
                  pl.BlockSpec((1, 4, C_HPG, TQ, TQ), lambda b, g, t: (t % CMP_CLASSES, 0, g, 0, 0)),
                  pl.BlockSpec((TQ, 128), lambda b, g, t: (b * QT + t, g))],
        out_specs=pl.BlockSpec((TQ, C_HPG * C_DIM), lambda b, g, t: (b * QT + t, g)),
        out_shape=jax.ShapeDtypeStruct((N, C_W), BF16),
        scratch_shapes=[pltpu.VMEM((C_HPG * TQ, 128), F32),
                        pltpu.VMEM((C_HPG * TQ, 2 * C_DIM), F32),
                        pltpu.VMEM((C_HPG * TQ, max(n_cmp_pad, 4 * TQ)), BF16),
                        pltpu.VMEM((C_HPG * TQ, WIN + TQ), BF16),
                        pltpu.VMEM((C_HPG * TQ, 4 * TQ), F32), pltpu.VMEM((C_HPG * TQ, 4 * TQ), F32),
                        pltpu.VMEM((C_HPG * TQ, C_DIM), F32), pltpu.VMEM((C_HPG * TQ, C_DIM), F32)],
        compiler_params=_params(("parallel", "parallel", "arbitrary")),
        name="nsa",
    )(q, kv, kv, kv, kv, kvc, kvc, ov, tsel, twin, tcmp, gates)


def _layer_ab(h2, B, S, w_in, gate_b, conv_w, head_g, w_out, dil_tab):
    w_pad = jnp.pad(w_in, ((0, 0), (0, AB_PAD - AB_PROJ))).astype(BF16)
    gb_pad = jnp.pad(gate_b, (0, 128 - gate_b.shape[0])).reshape(1, 128).astype(F32)
    (aq, ak, av, aq4, ak4, av4, aq16, ak16, av16,
     bq, bk, bv, bo, gates) = _ab_proj(h2, w_pad, conv_w.astype(F32), gb_pad, S)
    ya = _dilated_attention({1: (aq, ak, av), 4: (aq4, ak4, av4), 16: (aq16, ak16, av16)}, dil_tab, B, S)
    yb = _mlstm(bq, bk, bv, gates, bo, head_g, B, S)
    wo = w_out.astype(BF16)
    return [ya, yb], [A_PATTERNS[-1][1], 1], [wo[:A_W], wo[A_W:]]


def _layer_c(h2, B, S, w_in, gate_b, cmp_pos, cmp_w1, cmp_w2, w_out, tsel, twin, tcmp):
    w_pad, gb_pad = _c_weights(w_in, gate_b)
    q, kc, kv, gates = _c_proj(h2, w_pad, gb_pad)
    kvc = _compress(kc, cmp_pos, cmp_w1, cmp_w2, B, S)
    out = _nsa_attention(q, kv, kvc, gates, tsel, twin, tcmp, B, S)
    return [out], [1], [w_out.astype(BF16)]


def kernel(x, rel_bias, ln_g, ln_b, ab_w_in, ab_gate_b, ab_conv, ab_head_norm, ab_w_out,
           c_w_in, c_gate_b, c_cmp_pos, c_cmp_w1, c_cmp_w2, c_w_out,
           moe_wr_g, moe_br_g, moe_wr_e, moe_br_e, moe_w_gate, moe_w_up, moe_w_down):
    B, S, D = x.shape
    assert D == D_MODEL and S % (TM) == 0 and S % (16 * A_BLOCK) == 0
    _check_cmp_windows(S)
    dil_tab = _bias_tables(rel_bias, _dilated_idx(), shift=False)
    tsel, twin, tcmp = _nsa_tables(rel_bias)
    h = x.reshape(B * S, D)
    for layer in range(DEPTH):
        j = layer // 2
        if layer % 2 == 0:
            ys, lays, ws = _layer_ab(h, B, S, ab_w_in[j], ab_gate_b[j], ab_conv[j], ab_head_norm[j],
                                     ab_w_out[j], dil_tab)
        else:
            ys, lays, ws = _layer_c(h, B, S, c_w_in[j], c_gate_b[j], c_cmp_pos[j], c_cmp_w1[j],
                                    c_cmp_w2[j], c_w_out[j], tsel, twin, tcmp)
        wr, br = _router_weights(moe_wr_g[layer], moe_br_g[layer], moe_wr_e[layer], moe_br_e[layer])
        parts = []
        rows = (B * S) // MOE_CHUNKS
        for c in range(MOE_CHUNKS):
            hx, cnt = _out_ln_route(ys, lays, ws, h, ln_g[layer, 0], ln_b[layer, 0], wr, br, c * rows, rows)
            parts.append(_moe(hx, cnt, layer, moe_w_gate, moe_w_up, moe_w_down,
                              ln_g[layer, 1], ln_b[layer, 1]))
        h = jnp.concatenate(parts, axis=0)
    return h.reshape(B, S, D)
```

```python
import functools
import math

import numpy as np
import jax
import jax.numpy as jnp
from jax import lax
from jax.experimental import pallas as pl
from jax.experimental.pallas import tpu as pltpu

F32 = jnp.float32
BF16 = jnp.bfloat16
NEG = -1e30
LOG2E = math.log2(math.e)
VMEM_LIMIT = 48 * 1024 * 1024

D_MODEL = 1024
DEPTH = 2
ALPHA = (2.0 * DEPTH) ** 0.25
LN_EPS = 1e-5
REL_BUCKETS = 32
REL_MAX_DIST = 2048

A_HEADS, A_DIM, A_W = 8, 64, 512
A_PATTERNS = ((128, 1), (512, 4), (2048, 16))
A_BLOCK = 128
B_HEADS, B_DIM, B_W = 4, 128, 512
B_CHUNK = 128
B_CONV = 4
AB_PROJ = 3592
AB_PAD = 3712

C_HEADS, C_GROUPS, C_HPG, C_DIM, C_W = 8, 2, 4, 128, 1024
CMP_BLOCK, CMP_STRIDE, CMP_HIDDEN = 32, 16, 256
SLC_BLOCK, SLC_TOP_N, WIN = 64, 16, 512
C_PROJ = 2584
TQ = 128
CMP_PAD = 128

N_GROUPS, EPG, N_EXPERTS, E_HID = 4, 4, 16, 512
N_BUCKETS = N_GROUPS * 6
TM = 512
TM_MOE = 256
MOE_CHUNKS = 1


def _dot(a, b):
    return jnp.dot(a, b, preferred_element_type=F32)


def _dot_nt(a, b):
    return lax.dot_general(a, b, (((1,), (1,)), ((), ())), preferred_element_type=F32)


def _params(sem):
    return pltpu.CompilerParams(dimension_semantics=sem, vmem_limit_bytes=VMEM_LIMIT)


def _bucket_np(n):
    n = np.maximum(n, 0)
    exact = REL_BUCKETS // 2
    nf = np.maximum(n, 1).astype(np.float64)
    large = exact + (np.log(nf / exact) / math.log(REL_MAX_DIST / exact)
                     * (REL_BUCKETS - exact)).astype(np.int64)
    return np.where(n < exact, n, np.minimum(large, REL_BUCKETS - 1)).astype(np.int32)


def _bias_tab_kernel(tab_ref, idx_ref, out_ref, *, shift, scale):
    R = idx_ref.shape[1]
    RC = 32

    def body(i, carry):
        r0 = pl.multiple_of(i * RC, RC)
        idx = idx_ref[0, pl.ds(r0, RC), :]
        for h in range(8):
            base = tab_ref[REL_BUCKETS - 1, h] if shift else 0.0
            val = jnp.full(idx.shape, (tab_ref[0, h] - base) * scale, F32)
            for b in range(1, REL_BUCKETS):
                val = jnp.where(idx == b, (tab_ref[b, h] - base) * scale, val)
            out_ref[0, h, pl.ds(r0, RC), :] = jnp.where(idx < 0, NEG, val)
        return carry

    lax.fori_loop(0, R // RC, body, 0)


def _bias_tables(rel_bias, idx_np, shift, scale=1.0):
    T, R, C = idx_np.shape
    return pl.pallas_call(
        functools.partial(_bias_tab_kernel, shift=shift, scale=scale),
        grid=(T,),
        in_specs=[pl.BlockSpec(memory_space=pltpu.SMEM),
                  pl.BlockSpec((1, R, C), lambda t: (t, 0, 0))],
        out_specs=pl.BlockSpec((1, 8, R, C), lambda t: (t, 0, 0, 0)),
        out_shape=jax.ShapeDtypeStruct((T, 8, R, C), F32),
        compiler_params=_params(("parallel",)),
        name="bias_tables",
    )(rel_bias.astype(F32), jnp.asarray(idx_np))


def _dilated_idx():
    qi = np.arange(A_BLOCK)[:, None]
    ki = np.arange(2 * A_BLOCK)[None, :]
    j = qi + A_BLOCK - ki
    out = []
    for window, dilation in A_PATTERNS:
        nk = window // dilation
        valid = (j >= 0) & (j <= nk)
        out.append(np.where(valid, _bucket_np(np.maximum(j, 0) * dilation), -1))
    return np.stack(out).astype(np.int32)


def _sel_idx():
    a = np.arange(TQ)[:, None]
    c = np.arange(TQ)[None, :]
    n_delta = -(-(_far_dist() + TQ) // TQ)
    out = []
    for delta in range(-1, n_delta + 1):
        dist = delta * TQ + a - c
        out.append(np.where(dist >= 0, _bucket_np(dist), -1))
    return np.stack(out).astype(np.int32)


def _far_dist():
    n = np.arange(0, 4 * REL_MAX_DIST)
    b = _bucket_np(n)
    return int(np.max(n[b < REL_BUCKETS - 1])) + 1


def _win_idx():
    a = np.arange(TQ)[:, None]
    c = np.arange(TQ)[None, :]
    out = []
    for delta in range(-1, WIN // TQ + 1):
        dist = delta * TQ + a - c
        out.append(np.where((dist >= 0) & (dist < WIN), _bucket_np(dist), -1))
    return np.stack(out).astype(np.int32)


CMP_PER_TILE = TQ // CMP_STRIDE
CMP_CLASSES = TQ // CMP_PER_TILE
CMP_SPLIT = 13


def _cmp_window_start(qb):
    return qb // CMP_CLASSES + (1 if qb % CMP_CLASSES >= CMP_SPLIT else 0)


def _cmp_idx():
    a = np.arange(TQ)[:, None]
    c = np.arange(TQ)[None, :]
    out = []
    for r in range(CMP_CLASSES):
        qb = CMP_CLASSES + r
        i0 = _cmp_window_start(qb) * TQ - CMP_PAD
        for half in range(2):
            dist = qb * TQ + a - ((i0 + half * TQ + c) * CMP_STRIDE + CMP_BLOCK - 1)
            out.append(np.where(dist >= 0, _bucket_np(dist), -1))
        out.append(np.full((TQ, TQ), REL_BUCKETS - 1))
        out.append(np.full((TQ, TQ), -1))
    return np.stack(out).astype(np.int32)


def _nsa_tables(rel_bias):
    tsel = _bias_tables(rel_bias, _sel_idx(), shift=True, scale=LOG2E)
    twin = _bias_tables(rel_bias, _win_idx(), shift=False, scale=LOG2E)
    tcmp = _bias_tables(rel_bias, _cmp_idx(), shift=True, scale=LOG2E)
    return tsel, twin, tcmp.reshape(CMP_CLASSES, 4, 8, TQ, TQ)


def _check_cmp_windows(S):
    far = _far_dist()
    for qb in range(S // TQ):
        i0 = _cmp_window_start(qb) * TQ - CMP_PAD
        s0 = qb * TQ
        assert s0 - ((i0 - 1) * CMP_STRIDE + CMP_BLOCK - 1) >= far
        assert s0 + TQ - 1 - ((i0 + 2 * TQ) * CMP_STRIDE + CMP_BLOCK - 1) < 0


def _residue_col(d, r):
    return (r % 4) * 4 + r // 4 if d == 16 else r


def _lane_chunks_store(ref3, val):
    for c in range(ref3.shape[0]):
        ref3[c] = val[:, c * 128:(c + 1) * 128]


def _to_residue_layout(src3_ref, dst, d, col_of=_residue_col):
    nc, rows, _ = src3_ref.shape
    for r in range(d):
        cb = col_of(d, r)
        for c in range(nc):
            col = (cb * nc + c) * 128
            dst(slice(col, col + 128), src3_ref[c, pl.ds(r, rows // d, stride=d), :])


def _ab_proj_kernel(x_ref, xh_ref, w_ref, cw_ref, gb_ref,
                    aq_ref, ak_ref, av_ref, aq4_ref, ak4_ref, av4_ref, aq16_ref, ak16_ref, av16_ref,
                    bq_ref, bk_ref, bv_ref, bo_ref, g_ref,
                    pre_ref, tmp_ref, *, tiles_per_seq):
    i = pl.program_id(0)
    tm = x_ref.shape[0]
    xb = x_ref[...].astype(BF16)
    for c, scale, outs in ((0, A_DIM ** -0.5, (aq_ref, aq4_ref, aq16_ref)),
                           (1, 1.0, (ak_ref, ak4_ref, ak16_ref)),
                           (2, 1.0, (av_ref, av4_ref, av16_ref))):
        val = _dot(xb, w_ref[:, c * A_W:(c + 1) * A_W]) * scale
        _lane_chunks_store(tmp_ref, val)
        outs[0][...] = val.astype(BF16)
        for d, o_ref in ((4, outs[1]), (16, outs[2])):
            def put(cols, piece, o_ref=o_ref):
                o_ref[:, cols] = piece.astype(BF16)
            _to_residue_layout(tmp_ref, put, d)
    bv_ref[...] = _dot(xb, w_ref[:, 2560:3072]).astype(BF16)
    bo_ref[...] = _dot(xb, w_ref[:, 3072:3584])
    g_ref[...] = _dot(xb, w_ref[:, 3584:AB_PAD]) + gb_ref[...]
    halo = _dot(xh_ref[...].astype(BF16), w_ref[:, 1536:2560])
    halo = jnp.where(i % tiles_per_seq == 0, 0.0, halo)
    pre_ref[0:8, :] = halo
    pre_ref[8:8 + tm, :] = _dot(xb, w_ref[:, 1536:2560])
    y = pre_ref[8:8 + tm, :] * cw_ref[B_CONV - 1:B_CONV, :]
    for k in range(B_CONV - 1):
        s = B_CONV - 1 - k
        y = y + pre_ref[8 - s:8 - s + tm, :] * cw_ref[k:k + 1, :]
    y = y / (1.0 + jnp.exp(-y))
    bq_ref[...] = (y[:, :B_W] * (B_DIM ** -0.5)).astype(BF16)
    bk_ref[...] = y[:, B_W:].astype(BF16)


def _ab_proj(x2, w_pad, conv_w, gate_b_pad, S):
    N = x2.shape[0]
    tm = TM
    tps = S // tm
    row = lambda i: (i, 0)
    fix = lambda i: (0, 0)
    lay = lambda d: [jax.ShapeDtypeStruct((N // d, d * A_W), BF16)] * 3
    lay_spec = lambda d: [pl.BlockSpec((tm // d, d * A_W), row)] * 3
    outs = lay(1) + lay(4) + lay(16) + [jax.ShapeDtypeStruct((N, 512), BF16)] * 3 + [
        jax.ShapeDtypeStruct((N, 512), F32), jax.ShapeDtypeStruct((N, 128), F32)]
    o_specs = (lay_spec(1) + lay_spec(4) + lay_spec(16) + [pl.BlockSpec((tm, 512), row)] * 4
               + [pl.BlockSpec((tm, 128), row)])
    return pl.pallas_call(
        functools.partial(_ab_proj_kernel, tiles_per_seq=tps),
        grid=(N // tm,),
        in_specs=[pl.BlockSpec((tm, D_MODEL), row),
                  pl.BlockSpec((8, D_MODEL), lambda i: (jnp.maximum(i * (tm // 8) - 1, 0), 0)),
                  pl.BlockSpec((D_MODEL, AB_PAD), fix),
                  pl.BlockSpec((B_CONV, 2 * B_W), fix),
                  pl.BlockSpec((1, 128), fix)],
        out_specs=o_specs,
        out_shape=outs,
        scratch_shapes=[pltpu.VMEM((tm + 8, 2 * B_W), F32), pltpu.VMEM((A_W // 128, tm, 128), F32)],
        compiler_params=_params(("parallel",)),
        name="ab_proj",
    )(x2, x2, w_pad, conv_w, gate_b_pad)


def _dilated_kernel(*refs, has_prev, is_last):
    if has_prev:
        q_ref, kp_ref, kc_ref, vp_ref, vc_ref, tab_ref, op_ref, lp_ref = refs[:8]
        rest = refs[8:]
    else:
        q_ref, kp_ref, kc_ref, vp_ref, vc_ref, tab_ref = refs[:6]
        rest = refs[6:]
    outs, scratch = (rest, ()) if is_last else (rest[:2], rest[2:])
    o_ref = outs[0]
    n = pl.program_id(2)
    first = jnp.where(n == 0, NEG, 0.0)
    lane = lax.broadcasted_iota(jnp.int32, (A_BLOCK, 128), 1)
    keep_side = [jnp.where(lane < A_DIM, 1.0, 0.0).astype(BF16), jnp.where(lane < A_DIM, 0.0, 1.0).astype(BF16)]
    ones = jnp.ones((A_BLOCK, 128), BF16)
    m_tile = jnp.zeros((A_BLOCK, 128), F32)
    l_tile = jnp.ones((A_BLOCK, 128), F32)
    unnorm = []
    for j in range(A_HEADS // 2):
        cs = slice(j * 128, (j + 1) * 128)
        q2 = q_ref[0, :, cs]
        kp, kc, vp, vc = kp_ref[0, :, cs], kc_ref[0, :, cs], vp_ref[0, :, cs], vc_ref[0, :, cs]
        acc = None
        for side in range(2):
            h = 2 * j + side
            keep = keep_side[side]
            q = q2 * keep
            sp = _dot_nt(q, kp) + tab_ref[0, h, :, 0:A_BLOCK] + first
            sc = _dot_nt(q, kc) + tab_ref[0, h, :, A_BLOCK:2 * A_BLOCK]
            m = jnp.max(jnp.maximum(sp, sc), axis=1, keepdims=True)
            pp = jnp.exp(sp - m).astype(BF16)
            pc = jnp.exp(sc - m).astype(BF16)
            vpe = jnp.concatenate([vp * keep, ones], axis=1)
            vce = jnp.concatenate([vc * keep, ones], axis=1)
            r = _dot(pp, vpe) + _dot(pc, vce)
            acc = r[:, 0:128] if acc is None else acc + r[:, 0:128]
            m_tile = jnp.where(lane == A_DIM + h, m, m_tile)
            l_tile = jnp.where(lane == A_DIM + h, r[:, 128:256], l_tile)
        unnorm.append(acc)
    stat = (lane >= A_DIM) & (lane < A_DIM + A_HEADS)
    lse = m_tile + jnp.log(l_tile)
    if has_prev:
        lp = lp_ref[0]
        mm = jnp.maximum(lp, lse)
        wp = jnp.exp(lp - mm)
        wc = jnp.exp(lse - mm)
        tot = wp + wc
        scale_prev = jnp.where(stat, wp / tot, 0.0)
        scale_cur = jnp.where(stat, wc / (tot * l_tile), 0.0)
        lse = mm + jnp.log(tot)
    else:
        scale_cur = jnp.where(stat, 1.0 / l_tile, 0.0)
    erow = lax.broadcasted_iota(jnp.int32, (128, A_W), 0)
    ecol = lax.broadcasted_iota(jnp.int32, (128, A_W), 1)
    expand = jnp.where(erow - A_DIM == jnp.right_shift(ecol, A_DIM.bit_length() - 1), 1.0, 0.0).astype(BF16)

    def spread(t):
        hi, mid, lo = _split3(t)
        return _dot(hi, expand) + _dot(mid, expand) + _dot(lo, expand)

    o = jnp.concatenate(unnorm, axis=1) * spread(scale_cur)
    if has_prev:
        o = o + op_ref[0] * spread(scale_prev)
    if is_last:
        o_ref[0] = o.astype(o_ref.dtype)
    else:
        o_scr, l_scr = scratch
        _lane_chunks_store(o_scr, o)
        l_scr[0] = jnp.where(stat, lse, 0.0)

        def put_o(cols, piece):
            o_ref[0, :, cols] = piece

        def put_l(cols, piece):
            outs[1][0, :, cols] = piece

        _to_residue_layout(o_scr, put_o, 4)
        _to_residue_layout(l_scr, put_l, 4)


def _dilated_call(q, k, v, tab, prev, pattern_idx, dilation, B, S, is_last):
    d = dilation
    L = S // d
    nb = L // A_BLOCK
    r3 = lambda a: a.reshape(B, L, a.shape[-1])
    cur = lambda b, r, n: (b, n, r)
    prv = lambda b, r, n: (b, jnp.maximum(n - 1, 0), r)
    blk = pl.BlockSpec((1, A_BLOCK, A_W), cur)
    in_specs = [blk, pl.BlockSpec((1, A_BLOCK, A_W), prv), blk,
                pl.BlockSpec((1, A_BLOCK, A_W), prv), blk,
                pl.BlockSpec((1, 8, A_BLOCK, 2 * A_BLOCK), lambda b, r, n: (pattern_idx, 0, 0, 0))]
    args = [r3(q), r3(k), r3(k), r3(v), r3(v), tab]
    has_prev = prev is not None
    if has_prev:
        in_specs += [blk, pl.BlockSpec((1, A_BLOCK, 128), cur)]
        args += [r3(prev[0]), r3(prev[1])]
    scratch = []
    if is_last:
        out_shape = [jax.ShapeDtypeStruct((B, L, d * A_W), BF16)]
        out_specs = [blk]
    else:
        d2, rows = 4 * d, A_BLOCK // 4
        nxt = lambda b, r, n: (b, n, r)
        out_shape = [jax.ShapeDtypeStruct((B, S // d2, d2 * A_W), F32),
                     jax.ShapeDtypeStruct((B, S // d2, d2 * 128), F32)]
        out_specs = [pl.BlockSpec((1, rows, 4 * A_W), nxt), pl.BlockSpec((1, rows, 4 * 128), nxt)]
        scratch = [pltpu.VMEM((A_W // 128, A_BLOCK, 128), F32), pltpu.VMEM((1, A_BLOCK, 128), F32)]
    res = pl.pallas_call(
        functools.partial(_dilated_kernel, has_prev=has_prev, is_last=is_last),
        grid=(B, d, nb),
        in_specs=in_specs, out_specs=out_specs, out_shape=out_shape, scratch_shapes=scratch,
        compiler_params=_params(("parallel", "parallel", "arbitrary")),
        name="dilated_d%d" % d,
    )(*args)
    return [r.reshape(-1, r.shape[-1]) for r in res]


def _dilated_attention(qkv_by_dilation, tab, B, S):
    prev = None
    for p, (window, d) in enumerate(A_PATTERNS):
        assert window // d == A_BLOCK and S % (d * A_BLOCK) == 0
        assert p == 0 or d == 4 * A_PATTERNS[p - 1][1]
        last = p == len(A_PATTERNS) - 1
        q, k, v = qkv_by_dilation[d]
        prev = _dilated_call(q, k, v, tab, prev, p, d, B, S, last)
    return prev[0]


def _split3(x):
    hi = x.astype(BF16)
    r = x - hi.astype(F32)
    mid = r.astype(BF16)
    lo = (r - mid.astype(F32)).astype(BF16)
    return hi, mid, lo


def _mlstm_kernel(q_ref, k_ref, v_ref, g_ref, bo_ref, hg_ref, y_ref, c_ref, m_ref):
    L = B_CHUNK
    c = pl.program_id(1)

    @pl.when(c == 0)
    def _():
        c_ref[...] = jnp.zeros_like(c_ref)
        m_ref[...] = jnp.zeros_like(m_ref)

    lane = lax.broadcasted_iota(jnp.int32, (L, 128), 1)
    row = lax.broadcasted_iota(jnp.int32, (L, L), 0)
    col = lax.broadcasted_iota(jnp.int32, (L, L), 1)
    tri = row >= col
    is_f = (lane >= B_HEADS) & (lane < 2 * B_HEADS)
    tril = jnp.where(tri, 1.0, 0.0).astype(BF16)
    ones = jnp.ones((L, B_DIM), BF16)
    for bi, h in [(bi, h) for bi in range(q_ref.shape[0]) for h in range(B_HEADS)]:
        if h == 0:
            g = g_ref[bi]
            logf = jnp.minimum(g, 0.0) - jnp.log(1.0 + jnp.exp(-jnp.abs(g)))
            gl = jnp.where(is_f, logf, jnp.where(lane < B_HEADS, g, 0.0))
            hi, mid, lo = _split3(gl)
            cum = _dot(tril, hi) + _dot(tril, mid) + _dot(tril, lo)
            cum_t = cum.T
            gl_t = gl.T
        st = bi * B_HEADS + h
        cs = slice(h * B_DIM, (h + 1) * B_DIM)
        q = q_ref[bi, :, cs]
        k = k_ref[bi, :, cs]
        v_ext = jnp.concatenate([v_ref[bi, :, cs], ones], axis=1)
        b_col = cum[:, B_HEADS + h:B_HEADS + h + 1]
        b_row = cum_t[B_HEADS + h:B_HEADS + h + 1, :]
        i_col = gl[:, h:h + 1]
        i_row = gl_t[h:h + 1, :]
        m_prev = m_ref[st, 0:1, 0:1]
        dm = jnp.where(tri, b_col - b_row + i_row, NEG)
        inter = b_col + m_prev
        m_t = jnp.maximum(inter, jnp.max(dm, axis=1, keepdims=True))
        p = jnp.exp(dm - m_t)
        sqk = _dot_nt(q, k) * p
        sc = jnp.exp(inter - m_t)
        lhs = jnp.concatenate([(sc * q.astype(F32)).astype(BF16), sqk.astype(BF16)], axis=1)
        c_ext = c_ref[st]
        rhs = jnp.concatenate([c_ext.astype(BF16), v_ext], axis=0)
        res = _dot(lhs, rhs)
        num = res[:, :B_DIM]
        den = res[:, B_DIM:]
        hh = num / jnp.maximum(jnp.abs(den), jnp.exp(-m_t))
        b_last = b_col[L - 1:L, :]
        gk = b_last - b_col + i_col
        m_new = jnp.maximum(b_last + m_prev, jnp.max(gk, axis=0, keepdims=True))
        wk = jnp.exp(gk - m_new)
        decay = jnp.exp(b_last + m_prev - m_new)
        kw_t = (wk * k.astype(F32)).T.astype(BF16)
        c_ref[st] = decay * c_ext + _dot(kw_t, v_ext)
        m_ref[st] = jnp.broadcast_to(m_new, (8, 128))
        mu = jnp.mean(hh, axis=1, keepdims=True)
        xc = hh - mu
        var = jnp.mean(xc * xc, axis=1, keepdims=True)
        hn = xc * lax.rsqrt(var + LN_EPS) * hg_ref[:, cs]
        bo = bo_ref[bi, :, cs]
        y_ref[bi, :, cs] = (hn / (1.0 + jnp.exp(-bo))).astype(BF16)


def _mlstm(bq, bk, bv, gates, bo, head_g, B, S):
    nc = S // B_CHUNK
    bb = 1
    r3 = lambda a: a.reshape(B, S, a.shape[-1])
    blk = pl.BlockSpec((bb, B_CHUNK, B_W), lambda b, c: (b, c, 0))
    y = pl.pallas_call(
        _mlstm_kernel,
        grid=(B // bb, nc),
        in_specs=[blk, blk, blk,
                  pl.BlockSpec((bb, B_CHUNK, 128), lambda b, c: (b, c, 0)),
                  blk,
                  pl.BlockSpec((1, B_W), lambda b, c: (0, 0))],
        out_specs=blk,
        out_shape=jax.ShapeDtypeStruct((B, S, B_W), BF16),
        scratch_shapes=[pltpu.VMEM((bb * B_HEADS, B_DIM, 2 * B_DIM), F32),
                        pltpu.VMEM((bb * B_HEADS, 8, 128), F32)],
        compiler_params=_params(("parallel", "arbitrary")),
        name="mlstm",
    )(r3(bq), r3(bk), r3(bv), r3(gates), r3(bo), head_g.reshape(1, B_W).astype(F32))
    return y.reshape(B * S, B_W)


def _layer_norm(z, g, b):
    mu = jnp.mean(z, axis=1, keepdims=True)
    zc = z - mu
    var = jnp.mean(zc * zc, axis=1, keepdims=True)
    return zc * lax.rsqrt(var + LN_EPS) * g + b


def _route(logits, cnt_ref):
    tm = logits.shape[0]
    lt = logits.T
    col = lambda c: lt[c:c + 1, :]
    gl = [col(c) for c in range(N_GROUPS)]
    gmax = functools.reduce(jnp.maximum, gl)
    gsum = sum(jnp.exp(x - gmax) for x in gl)
    g_idx = jnp.full(gmax.shape, N_GROUPS - 1, jnp.int32)
    for c in range(N_GROUPS - 2, -1, -1):
        g_idx = jnp.where(gl[c] == gmax, c, g_idx)
    g_w = 1.0 / gsum
    el = []
    for k in range(EPG):
        x = col(N_GROUPS + (N_GROUPS - 1) * EPG + k)
        for g in range(N_GROUPS - 2, -1, -1):
            x = jnp.where(g_idx == g, col(N_GROUPS + g * EPG + k), x)
        el.append(x)
    v1 = functools.reduce(jnp.maximum, el)
    i1 = jnp.full(v1.shape, EPG - 1, jnp.int32)
    for k in range(EPG - 2, -1, -1):
        i1 = jnp.where(el[k] == v1, k, i1)
    el2 = [jnp.where(i1 == k, -jnp.inf, el[k]) for k in range(EPG)]
    v2 = functools.reduce(jnp.maximum, el2)
    i2 = jnp.full(v2.shape, EPG - 1, jnp.int32)
    for k in range(EPG - 2, -1, -1):
        i2 = jnp.where((el2[k] == v2) & (i1 != k), k, i2)
    t = jnp.exp(v2 - v1)
    w1 = g_w / (1.0 + t)
    w2 = w1 * t
    a = jnp.minimum(i1, i2)
    b = jnp.maximum(i1, i2)
    pair = jnp.where(a == 0, b - 1, jnp.where(a == 1, b + 1, 5))
    bucket = (g_idx * 6 + pair).astype(F32)
    w_lo = jnp.where(i1 < i2, w1, w2)
    w_hi = jnp.where(i1 < i2, w2, w1)
    sub = lax.broadcasted_iota(jnp.int32, (128, tm), 0)
    onehot_t = jnp.where(sub.astype(F32) == bucket, 1.0, 0.0)
    srow = lax.broadcasted_iota(jnp.int32, (tm, tm), 0)
    scol = lax.broadcasted_iota(jnp.int32, (tm, tm), 1)
    before = jnp.where(srow < scol, 1.0, 0.0).astype(BF16)
    oh = onehot_t.astype(BF16)
    carry = cnt_ref[...]
    prior = _dot(oh, before) + jnp.concatenate([carry] * (tm // 128), axis=1)
    rank = jnp.sum(onehot_t * prior, axis=0, keepdims=True)
    cnt_ref[...] = carry + _dot(oh, jnp.ones((tm, 128), BF16))
    out_t = jnp.where(sub == 0, bucket, jnp.where(sub == 1, w_lo, jnp.where(sub == 2, w_hi,
                      jnp.where(sub == 3, rank, 0.0))))
    return out_t.T


def _out_ln_route_kernel(*refs, n_in, layouts):
    y_refs = refs[:n_in]
    w_refs = refs[n_in:2 * n_in]
    x_ref, g_ref, b_ref, wrh_ref, wrl_ref, br_ref, h_ref, c_ref, cnt_ref = refs[2 * n_in:2 * n_in + 9]
    pos_refs = list(refs[2 * n_in + 9:])
    tm = x_ref.shape[0]

    @pl.when(pl.program_id(0) == 0)
    def _():
        cnt_ref[...] = jnp.zeros_like(cnt_ref)

    y = None
    for i in range(n_in):
        d = layouts[i]
        if d == 1:
            lhs = y_refs[i][...]
        else:
            s_ref = pos_refs.pop(0)
            nc = s_ref.shape[0]
            for r in range(d):
                cb = _residue_col(d, r)
                for c in range(nc):
                    col = (cb * nc + c) * 128
                    s_ref[c, pl.ds(r, tm // d, stride=d), :] = y_refs[i][:, col:col + 128].astype(F32)
            lhs = jnp.concatenate([s_ref[c] for c in range(nc)], axis=1).astype(BF16)
        t = _dot(lhs, w_refs[i][...])
        y = t if y is None else y + t
    hn = _layer_norm(ALPHA * x_ref[...] + y, g_ref[...], b_ref[...])
    h_ref[:, 0:D_MODEL] = hn
    hi = hn.astype(BF16)
    lo = (hn - hi.astype(F32)).astype(BF16)
    logits = (_dot_nt(hi, wrh_ref[...]) + _dot_nt(lo, wrh_ref[...]) + _dot_nt(hi, wrl_ref[...])
              + br_ref[...])
    h_ref[:, D_MODEL:D_MODEL + 128] = _route(logits, cnt_ref)
    c_ref[...] = cnt_ref[...]


def _out_ln_route(ys, layouts, ws, x2, ln_g, ln_b, wr, br, row_start, N):
    tm = TM
    off = row_start // tm
    row = lambda i: (i + off, 0)
    fix = lambda i: (0, 0)
    wr_hi = wr.astype(BF16)
    wr_lo = (wr - wr_hi.astype(F32)).astype(BF16)
    in_specs = ([pl.BlockSpec((tm // d, y.shape[1]), row) for y, d in zip(ys, layouts)]
                + [pl.BlockSpec(w.shape, fix) for w in ws]
                + [pl.BlockSpec((tm, D_MODEL), row),
                   pl.BlockSpec((1, D_MODEL), fix), pl.BlockSpec((1, D_MODEL), fix),
                   pl.BlockSpec((128, D_MODEL), fix), pl.BlockSpec((128, D_MODEL), fix),
                   pl.BlockSpec((1, 128), fix)])
    return pl.pallas_call(
        functools.partial(_out_ln_route_kernel, n_in=len(ys), layouts=tuple(layouts)),
        grid=(N // tm,),
        in_specs=in_specs,
        out_specs=[pl.BlockSpec((tm, D_MODEL + 128), lambda i: (i, 0)), pl.BlockSpec((128, 128), fix)],
        out_shape=[jax.ShapeDtypeStruct((N, D_MODEL + 128), F32), jax.ShapeDtypeStruct((128, 128), F32)],
        scratch_shapes=[pltpu.VMEM((128, 128), F32)] + [
            pltpu.VMEM((y.shape[1] // d // 128, tm, 128), F32) for y, d in zip(ys, layouts) if d != 1],
        compiler_params=_params(("arbitrary",)),
        name="out_ln_route",
    )(*ys, *ws, x2, ln_g.reshape(1, -1), ln_b.reshape(1, -1), wr_hi, wr_lo, br)


def _router_weights(wr_g, br_g, wr_e, br_e):
    we = wr_e.transpose(0, 2, 1).reshape(N_GROUPS * EPG, D_MODEL)
    w = jnp.concatenate([wr_g.T, we], axis=0)
    w = jnp.pad(w, ((0, 128 - w.shape[0]), (0, 0)))
    b = jnp.concatenate([br_g, br_e.reshape(-1)])
    b = jnp.pad(b, (0, 128 - b.shape[0])).reshape(1, 128)
    return w.astype(F32), b.astype(F32)


_PAIRS = ((0, 1), (0, 2), (0, 3), (1, 2), (1, 3), (2, 3))


def _moe_kernel(elo_ref, ehi_ref, chg_ref, nt_ref,
                x_ref, wgl_ref, wul_ref, wdl_ref, wgh_ref, wuh_ref, wdh_ref,
                g_ref, b_ref, o_ref, wg_s, wu_s, wd_s):
    t = pl.program_id(0)

    @pl.when(chg_ref[t] == 1)
    def _():
        wg_s[0] = wgl_ref[0, 0].astype(BF16)
        wu_s[0] = wul_ref[0, 0].astype(BF16)
        wd_s[0] = wdl_ref[0, 0].astype(BF16)
        wg_s[1] = wgh_ref[0, 0].astype(BF16)
        wu_s[1] = wuh_ref[0, 0].astype(BF16)
        wd_s[1] = wdh_ref[0, 0].astype(BF16)

    @pl.when(t < nt_ref[0])
    def _():
        x = x_ref[:, 0:D_MODEL]
        xb = x.astype(BF16)
        r = x_ref[:, D_MODEL:D_MODEL + 128]
        acc = None
        for e in range(2):
            a = _dot(xb, wg_s[e])
            u = _dot(xb, wu_s[e])
            hcur = (a / (1.0 + jnp.exp(-a))) * u * r[:, 1 + e:2 + e]
            y = _dot(hcur.astype(BF16), wd_s[e])
            acc = y if acc is None else acc + y
        o_ref[...] = _layer_norm(ALPHA * x + acc, g_ref[...], b_ref[...])

    @pl.when(t >= nt_ref[0])
    def _():
        o_ref[...] = jnp.zeros_like(o_ref)


def _moe(hx, cnt, layer, w_gate, w_up, w_down, ln_g, ln_b):
    N = hx.shape[0]
    tm = TM_MOE
    n_tiles = N // tm + N_BUCKETS
    n_pad = n_tiles * tm
    bucket = hx[:, D_MODEL].astype(jnp.int32)
    rank = hx[:, D_MODEL + 3].astype(jnp.int32)
    counts = cnt[:N_BUCKETS, 0].astype(jnp.int32)
    padded = ((counts + tm - 1) // tm) * tm
    ends = jnp.cumsum(padded)
    offs = ends - padded
    dest = offs[bucket] + rank
    src = (jnp.arange(n_pad, dtype=jnp.int32) % N).at[dest].set(
        jnp.arange(N, dtype=jnp.int32), mode="promise_in_bounds", unique_indices=True)
    tile_start = jnp.arange(n_tiles, dtype=jnp.int32) * tm
    n_used = (ends[-1] // tm).astype(jnp.int32)
    tb = jnp.sum((tile_start[:, None] >= ends[None, :]).astype(jnp.int32), axis=1)
    tb_last = jnp.take(tb, jnp.maximum(n_used - 1, 0))
    tb = jnp.where(tile_start < ends[-1], tb, tb_last)
    pairs = jnp.asarray(_PAIRS, jnp.int32)
    elo = (tb // 6) * EPG + pairs[tb % 6, 0]
    ehi = (tb // 6) * EPG + pairs[tb % 6, 1]
    chg = jnp.concatenate([jnp.ones((1,), jnp.int32), (tb[1:] != tb[:-1]).astype(jnp.int32)])
    xs = hx.at[src].get(mode="promise_in_bounds")

    row = lambda t, *_: (t, 0)
    fix = lambda t, *_: (0, 0)
    wlo = lambda t, elo, ehi, chg, nt: (layer, elo[t], 0, 0)
    whi = lambda t, elo, ehi, chg, nt: (layer, ehi[t], 0, 0)
    up_spec = lambda im: pl.BlockSpec((1, 1, D_MODEL, E_HID), im)
    dn_spec = lambda im: pl.BlockSpec((1, 1, E_HID, D_MODEL), im)
    grid_spec = pltpu.PrefetchScalarGridSpec(
        num_scalar_prefetch=4,
        grid=(n_tiles,),
        in_specs=[pl.BlockSpec((tm, D_MODEL + 128), row),
                  up_spec(wlo), up_spec(wlo), dn_spec(wlo),
                  up_spec(whi), up_spec(whi), dn_spec(whi),
                  pl.BlockSpec((1, D_MODEL), fix), pl.BlockSpec((1, D_MODEL), fix)],
        out_specs=pl.BlockSpec((tm, D_MODEL), row),
        scratch_shapes=[pltpu.VMEM((2, D_MODEL, E_HID), BF16),
                        pltpu.VMEM((2, D_MODEL, E_HID), BF16),
                        pltpu.VMEM((2, E_HID, D_MODEL), BF16)])
    out_sorted = pl.pallas_call(
        _moe_kernel,
        grid_spec=grid_spec,
        out_shape=jax.ShapeDtypeStruct((n_pad, D_MODEL), F32),
        compiler_params=_params(("arbitrary",)),
        name="moe",
    )(elo, ehi, chg, n_used.reshape(1), xs, w_gate, w_up, w_down, w_gate, w_up, w_down,
      ln_g.reshape(1, -1), ln_b.reshape(1, -1))
    return out_sorted.at[dest].get(mode="promise_in_bounds", unique_indices=True)


def _c_proj_kernel(x_ref, w_ref, gb_ref, q_ref, kc_ref, kv_ref, g_ref, tmp_ref):
    xb = x_ref[...].astype(BF16)
    q_ref[...] = (_dot(xb, w_ref[:, 0:C_W]) * (C_DIM ** -0.5 * LOG2E)).astype(BF16)
    for i in range(4):
        tmp_ref[0] = _dot(xb, w_ref[:, C_W + i * 128:C_W + (i + 1) * 128])

        def put(cols, piece, i=i):
            kc_ref[i, :, cols] = piece.astype(BF16)

        _to_residue_layout(tmp_ref, put, CMP_STRIDE, col_of=lambda d, r: r)
    kv_ref[...] = _dot(xb, w_ref[:, C_W + 512:C_W + 1536]).astype(BF16)
    z = _dot(xb, w_ref[:, C_W + 1536:C_W + 1792]) + gb_ref[...]
    g_ref[...] = 1.0 / (1.0 + jnp.exp(-z))


def _c_proj(x2, w_pad, gb_pad):
    N = x2.shape[0]
    tm = TM
    row = lambda i: (i, 0)
    fix = lambda i: (0, 0)
    wcols = w_pad.shape[1]
    return pl.pallas_call(
        _c_proj_kernel,
        grid=(N // tm,),
        in_specs=[pl.BlockSpec((tm, D_MODEL), row), pl.BlockSpec((D_MODEL, wcols), fix),
                  pl.BlockSpec((1, 256), fix)],
        out_specs=[pl.BlockSpec((tm, C_W), row),
                   pl.BlockSpec((4, tm // CMP_STRIDE, CMP_STRIDE * C_DIM), lambda i: (0, i, 0)),
                   pl.BlockSpec((tm, 1024), row), pl.BlockSpec((tm, 256), row)],
        out_shape=[jax.ShapeDtypeStruct((N, C_W), BF16),
                   jax.ShapeDtypeStruct((4, N // CMP_STRIDE, CMP_STRIDE * C_DIM), BF16),
                   jax.ShapeDtypeStruct((N, 1024), BF16), jax.ShapeDtypeStruct((N, 256), F32)],
        scratch_shapes=[pltpu.VMEM((1, tm, 128), F32)],
        compiler_params=_params(("parallel",)),
        name="c_proj",
    )(x2, w_pad, gb_pad)


def _c_weights(w_in, gate_b):
    gcols = []
    gb = []
    for g in range(C_GROUPS):
        idx = [C_PROJ - 3 * C_HEADS + br * C_HEADS + g * C_HPG + j for br in range(3) for j in range(C_HPG)]
        gcols.append(jnp.pad(w_in[:, np.asarray(idx)], ((0, 0), (0, 128 - len(idx)))))
        gb.append(jnp.pad(gate_b[np.asarray(idx) - (C_PROJ - 3 * C_HEADS)], (0, 128 - len(idx))))
    w = jnp.concatenate([w_in[:, :C_PROJ - 3 * C_HEADS]] + gcols, axis=1).astype(BF16)
    return w, jnp.concatenate(gb).reshape(1, 256).astype(F32)


def _compress_kernel(seg_ref, w1_ref, pos_ref, w1f_ref, w2_ref, o_ref):
    n_seg = seg_ref.shape[1]
    ul = _dot(seg_ref[0], w1_ref[0])
    u = ul[:, :CMP_HIDDEN]
    lnext = pltpu.roll(ul[:, CMP_HIDDEN:], n_seg - 1, 0)
    cpos = _dot(pos_ref[0], w1f_ref[0])[0:1, :]
    pre = u + lnext + cpos
    act = 0.5 * pre * (1.0 + jnp.tanh(math.sqrt(2.0 / math.pi) * (pre + 0.044715 * pre * pre * pre)))
    o_ref[0, 0, 0:CMP_PAD, :] = jnp.zeros((CMP_PAD, C_DIM), BF16)
    o_ref[0, 0, CMP_PAD:CMP_PAD + n_seg, :] = _dot(act.astype(BF16), w2_ref[0]).astype(BF16)


def _compress(kc, cmp_pos, cmp_w1, cmp_w2, B, S):
    n_seg = S // CMP_STRIDE
    half = CMP_STRIDE * C_DIM
    seg = kc.reshape(4 * B, n_seg, half)
    w1 = cmp_w1.astype(BF16)
    w1_ul = jnp.concatenate([w1[:, :half], w1[:, half:]], axis=2)
    pos = jnp.broadcast_to(cmp_pos.reshape(2, 1, CMP_BLOCK * C_DIM), (2, 8, CMP_BLOCK * C_DIM)).astype(BF16)
    out = pl.pallas_call(
        _compress_kernel,
        grid=(4, B),
        in_specs=[pl.BlockSpec((1, n_seg, half), lambda i, b: (i * B + b, 0, 0)),
                  pl.BlockSpec((1, half, 2 * CMP_HIDDEN), lambda i, b: (i // 2, 0, 0)),
                  pl.BlockSpec((1, 8, CMP_BLOCK * C_DIM), lambda i, b: (i // 2, 0, 0)),
                  pl.BlockSpec((1, CMP_BLOCK * C_DIM, CMP_HIDDEN), lambda i, b: (i // 2, 0, 0)),
                  pl.BlockSpec((1, CMP_HIDDEN, C_DIM), lambda i, b: (i // 2, 0, 0))],
        out_specs=pl.BlockSpec((1, 1, CMP_PAD + n_seg, C_DIM), lambda i, b: (i, b, 0, 0)),
        out_shape=jax.ShapeDtypeStruct((4, B, CMP_PAD + n_seg, C_DIM), BF16),
        compiler_params=_params(("parallel", "parallel")),
        name="compress",
    )(seg, w1_ul, pos, w1, cmp_w2.astype(BF16))
    return out


def _overlap_np(n_cmp_pad, n_slc):
    i = np.arange(n_cmp_pad)[:, None] - CMP_PAD
    m = np.arange(n_slc)[None, :]
    start = i * CMP_STRIDE
    ov = (start < (m + 1) * SLC_BLOCK) & (start + CMP_BLOCK - 1 >= m * SLC_BLOCK) & (i >= 0)
    return ov.astype(np.float32)


def _flash_update(s_heads, vext, m_ref, acc_ref, p_ref):
    W = s_heads[0].shape[1]
    nw = W // 128
    alphas = []
    for h, s in enumerate(s_heads):
        rs = slice(h * TQ, (h + 1) * TQ)
        m_prev = m_ref[rs, :]
        smax = functools.reduce(jnp.maximum, [s[:, i * 128:(i + 1) * 128] for i in range(nw)])
        m_new = jnp.maximum(m_prev, jnp.max(smax, axis=1, keepdims=True))
        alpha = jnp.exp2(m_prev - m_new)
        p = jnp.exp2(s - (m_new if nw == 1 else jnp.concatenate([m_new] * nw, axis=1)))
        m_ref[rs, :] = m_new
        p_ref[rs, 0:W] = p.astype(BF16)
        alphas.append(alpha)
    a = jnp.concatenate(alphas, axis=0)
    acc_ref[...] = jnp.concatenate([a, a], axis=1) * acc_ref[...] + _dot(p_ref[:, 0:W], vext)


def _nsa_kernel(q_ref, ks_ref, vs_ref, kw_ref, vw_ref, kc_ref, vc_ref, ov_ref,
                tsel_ref, twin_ref, tcmp_ref, g_ref, o_ref,
                m_ref, acc_ref, p_ref, pw_ref, sa_ref, sb_ref, oc_ref, ow_ref, *, n_sel_tab, top_n):
    qb = pl.program_id(2)
    H = C_HPG
    NB = ov_ref.shape[1]
    q_all = jnp.concatenate([q_ref[:, h * C_DIM:(h + 1) * C_DIM] for h in range(H)], axis=0)
    heads = lambda x: [x[h * TQ:(h + 1) * TQ] for h in range(H)]

    def reset():
        m_ref[...] = jnp.full(m_ref.shape, NEG, F32)
        acc_ref[...] = jnp.zeros(acc_ref.shape, F32)

    def softmax_rows(s):
        nw = s.shape[1] // 128
        smax = functools.reduce(jnp.maximum, [s[:, i * 128:(i + 1) * 128] for i in range(nw)])
        m = jnp.broadcast_to(jnp.max(smax, axis=1, keepdims=True), (TQ, 128))
        return jnp.exp2(s - jnp.concatenate([m] * nw, axis=1)), m

    t0 = qb // CMP_CLASSES + jnp.where(qb % CMP_CLASSES >= CMP_SPLIT, 1, 0)
    n_ct = kc_ref.shape[2] // TQ
    wc = n_ct * TQ
    tile_kind = [3] + [jnp.where(t == t0, 0, jnp.where(t == t0 + 1, 1, jnp.where(t < t0, 2, 3)))
                       for t in range(1, n_ct)]
    s_c = _dot_nt(q_all, kc_ref[0, 0])
    vext_c = jnp.concatenate([vc_ref[0, 0], ov_ref[...]], axis=1)
    inv_c = []
    for h in range(H):
        bias = jnp.concatenate([tcmp_ref[0, tile_kind[t], h] for t in range(n_ct)], axis=1)
        p, m = softmax_rows(s_c[h * TQ:(h + 1) * TQ] + bias)
        l = jnp.sum(p, axis=1, keepdims=True)
        inv_c.append(jnp.where(m > 0.5 * NEG, 1.0 / l, 0.0))
        p_ref[h * TQ:(h + 1) * TQ, 0:wc] = p.astype(BF16)
    res_c = _dot(p_ref[:, 0:wc], vext_c)
    imp = None
    for h in range(H):
        r = res_c[h * TQ:(h + 1) * TQ] * jnp.concatenate([inv_c[h], inv_c[h]], axis=1)
        oc_ref[h * TQ:(h + 1) * TQ, :] = r[:, 0:C_DIM]
        imp = r[:, C_DIM:] if imp is None else imp + r[:, C_DIM:]

    n_wt = WIN // TQ + 1
    ww = n_wt * TQ
    st = jnp.maximum(qb - (n_wt - 1), 0)
    r0w = pl.multiple_of(st * TQ, TQ)
    s_w = _dot_nt(q_all, kw_ref[pl.ds(r0w, ww), :])
    vext_w = jnp.concatenate([vw_ref[pl.ds(r0w, ww), :], jnp.ones((ww, C_DIM), BF16)], axis=1)
    widx = [jnp.maximum(qb - (st + c) + 1, 0) for c in range(n_wt)]
    for h in range(H):
        bias = jnp.concatenate([twin_ref[i, h] for i in widx], axis=1)
        p, _ = softmax_rows(s_w[h * TQ:(h + 1) * TQ] + bias)
        pw_ref[h * TQ:(h + 1) * TQ, :] = p.astype(BF16)
    res_w = _dot(pw_ref[...], vext_w)
    ow_ref[...] = res_w[:, 0:C_DIM] / res_w[:, C_DIM:]

    shift = SLC_BLOCK.bit_length() - 1
    qpos = qb * TQ + lax.broadcasted_iota(jnp.int32, (TQ, NB), 0)
    mblk = lax.broadcasted_iota(jnp.int32, (TQ, NB), 1)
    qblk = jnp.right_shift(qpos, shift)
    forced = (mblk == 0) | (mblk == qblk) | (mblk == qblk - 1)
    score = jnp.where(forced, 3e38, jnp.where(jnp.left_shift(mblk, shift) <= qpos, imp, NEG))
    score_t = score.T
    blk_t = lax.broadcasted_iota(jnp.int32, (NB, TQ), 0).astype(F32)
    sel_t = jnp.zeros((NB, TQ), F32)
    for _ in range(top_n):
        mx = jnp.max(score_t, axis=0, keepdims=True)
        idx = jnp.min(jnp.where(score_t == mx, blk_t, float(NB)), axis=0, keepdims=True)
        pick = blk_t == idx
        sel_t = jnp.where(pick, 1.0, sel_t)
        score_t = jnp.where(pick, -3e38, score_t)
    sel = sel_t.T

    reset()
    NT = 4
    TK = NT * TQ
    bpk = TK // SLC_BLOCK
    erow = lax.broadcasted_iota(jnp.int32, (NB, TK), 0)
    ecol = jnp.right_shift(lax.broadcasted_iota(jnp.int32, (NB, TK), 1), shift)
    expand0 = jnp.where(erow == ecol, 1.0, 0.0).astype(BF16)
    ones_k = jnp.ones((TK, C_DIM), BF16)
    n_steps = qb // NT + 1

    def sel_logits(kq, s_ref, near):
        kc = jnp.minimum(kq, n_steps - 1)
        r0 = pl.multiple_of(kc * TK, TK)
        s = _dot_nt(q_all, ks_ref[pl.ds(r0, TK), :])
        sel_k = pltpu.roll(sel, (NB - kc * bpk) % NB, 1).astype(BF16)
        addmask = _dot(sel_k, expand0) * (-NEG) + NEG
        if near:
            addmask = jnp.where(kq < n_steps, addmask, NEG)
            idx = [jnp.clip(qb - (NT * kc + c) + 1, 0, n_sel_tab - 1) for c in range(NT)]
        for h in range(H):
            bias = addmask
            if near:
                bias = bias + jnp.concatenate([tsel_ref[i, h] for i in idx], axis=1)
            s_ref[h * TQ:(h + 1) * TQ, :] = s[h * TQ:(h + 1) * TQ] + bias

    def sel_consume(kq, s_ref):
        r0 = pl.multiple_of(jnp.minimum(kq, n_steps - 1) * TK, TK)
        vext = jnp.concatenate([vs_ref[pl.ds(r0, TK), :], ones_k], axis=1)
        _flash_update([s_ref[h * TQ:(h + 1) * TQ, :] for h in range(H)], vext, m_ref, acc_ref, p_ref)

    def sel_run(k_lo, n2, near):
        @pl.when(n2 > 0)
        def _():
            sel_logits(k_lo, sa_ref, near)

        def body(j, carry):
            k = k_lo + 2 * j
            sel_logits(k + 1, sb_ref, near)
            sel_consume(k, sa_ref)
            sel_logits(k + 2, sa_ref, near)
            sel_consume(k + 1, sb_ref)
            return carry

        lax.fori_loop(0, n2, body, 0)

    n_far = jnp.maximum((qb + 1 - (n_sel_tab - 2)) // NT, 0)
    far2 = n_far // 2
    sel_run(0, far2, near=False)
    sel_run(2 * far2, (n_steps - 2 * far2 + 1) // 2, near=True)
    g = g_ref[...]
    for h in range(H):
        rs = slice(h * TQ, (h + 1) * TQ)
        out_s = acc_ref[rs, 0:C_DIM] / acc_ref[rs, C_DIM:2 * C_DIM]
        o = (g[:, h:h + 1] * oc_ref[rs, :] + g[:, H + h:H + h + 1] * out_s
             + g[:, 2 * H + h:2 * H + h + 1] * ow_ref[rs, :])
        o_ref[:, h * C_DIM:(h + 1) * C_DIM] = o.astype(BF16)


def _nsa_attention(q, kv, kvc, gates, tsel, twin, tcmp, B, S):
    N = B * S
    QT = S // TQ
    n_slc = S // SLC_BLOCK
    n_cmp_pad = kvc.shape[2]
    NB = 128
    assert n_slc <= NB and n_cmp_pad % TQ == 0 and QT % 4 == 0
    ov = jnp.asarray(_overlap_np(n_cmp_pad, NB), BF16)
    n_sel_delta = tsel.shape[0]
    kvspec = lambda c: pl.BlockSpec((S, C_DIM), lambda b, g, t: (b, c + g))
    cspec = lambda kvi: pl.BlockSpec((1, 1, n_cmp_pad, C_DIM), lambda b, g, t: (kvi * 2 + g, b, 0, 0))
    return pl.pallas_call(
        functools.partial(_nsa_kernel, n_sel_tab=n_sel_delta, top_n=min(SLC_TOP_N, n_slc)),
        grid=(B, C_GROUPS, QT),
        in_specs=[pl.BlockSpec((TQ, C_HPG * C_DIM), lambda b, g, t: (b * QT + t, g)),
                  kvspec(0), kvspec(2), kvspec(4), kvspec(6),
                  cspec(0), cspec(1),
                  pl.BlockSpec((n_cmp_pad, NB), lambda b, g, t: (0, 0)),
                  pl.BlockSpec((n_sel_delta, C_HPG, TQ, TQ), lambda b, g, t: (0, g, 0, 0)),
                  pl.BlockSpec((twin.shape[0], C_HPG, TQ, TQ), lambda b, g, t: (0, g, 0, 0)),
                  pl.BlockSpec((1, 4, C_HPG, TQ, TQ), lambda b, g, t: (t % CMP_CLASSES, 0, g, 0, 0)),
                  pl.BlockSpec((TQ, 128), lambda b, g, t: (b * QT + t, g))],
        out_specs=pl.BlockSpec((TQ, C_HPG * C_DIM), lambda b, g, t: (b * QT + t, g)),
        out_shape=jax.ShapeDtypeStruct((N, C_W), BF16),
        scratch_shapes=[pltpu.VMEM((C_HPG * TQ, 128), F32),
                        pltpu.VMEM((C_HPG * TQ, 2 * C_DIM), F32),
                        pltpu.VMEM((C_HPG * TQ, max(n_cmp_pad, 4 * TQ)), BF16),
                        pltpu.VMEM((C_HPG * TQ, WIN + TQ), BF16),
                        pltpu.VMEM((C_HPG * TQ, 4 * TQ), F32), pltpu.VMEM((C_HPG * TQ, 4 * TQ), F32),
                        pltpu.VMEM((C_HPG * TQ, C_DIM), F32), pltpu.VMEM((C_HPG * TQ, C_DIM), F32)],
        compiler_params=_params(("parallel", "parallel", "arbitrary")),
        name="nsa",
    )(q, kv, kv, kv, kv, kvc, kvc, ov, tsel, twin, tcmp, gates)


def _layer_ab(h2, B, S, w_in, gate_b, conv_w, head_g, w_out, dil_tab):
    w_pad = jnp.pad(w_in, ((0, 0), (0, AB_PAD - AB_PROJ))).astype(BF16)
    gb_pad = jnp.pad(gate_b, (0, 128 - gate_b.shape[0])).reshape(1, 128).astype(F32)
    (aq, ak, av, aq4, ak4, av4, aq16, ak16, av16,
     bq, bk, bv, bo, gates) = _ab_proj(h2, w_pad, conv_w.astype(F32), gb_pad, S)
    ya = _dilated_attention({1: (aq, ak, av), 4: (aq4, ak4, av4), 16: (aq16, ak16, av16)}, dil_tab, B, S)
    yb = _mlstm(bq, bk, bv, gates, bo, head_g, B, S)
    wo = w_out.astype(BF16)
    return [ya, yb], [A_PATTERNS[-1][1], 1], [wo[:A_W], wo[A_W:]]


def _layer_c(h2, B, S, w_in, gate_b, cmp_pos, cmp_w1, cmp_w2, w_out, tsel, twin, tcmp):
    w_pad, gb_pad = _c_weights(w_in, gate_b)
    q, kc, kv, gates = _c_proj(h2, w_pad, gb_pad)
    kvc = _compress(kc, cmp_pos, cmp_w1, cmp_w2, B, S)
    out = _nsa_attention(q, kv, kvc, gates, tsel, twin, tcmp, B, S)
    return [out], [1], [w_out.astype(BF16)]


def kernel(x, rel_bias, ln_g, ln_b, ab_w_in, ab_gate_b, ab_conv, ab_head_norm, ab_w_out,
           c_w_in, c_gate_b, c_cmp_pos, c_cmp_w1, c_cmp_w2, c_w_out,
           moe_wr_g, moe_br_g, moe_wr_e, moe_br_e, moe_w_gate, moe_w_up, moe_w_down):
    B, S, D = x.shape
    assert D == D_MODEL and S % (TM) == 0 and S % (16 * A_BLOCK) == 0
    _check_cmp_windows(S)
    dil_tab = _bias_tables(rel_bias, _dilated_idx(), shift=False)
    tsel, twin, tcmp = _nsa_tables(rel_bias)
    h = x.reshape(B * S, D)
    for layer in range(DEPTH):
        j = layer // 2
        if layer % 2 == 0:
            ys, lays, ws = _layer_ab(h, B, S, ab_w_in[j], ab_gate_b[j], ab_conv[j], ab_head_norm[j],
                                     ab_w_out[j], dil_tab)
        else:
            ys, lays, ws = _layer_c(h, B, S, c_w_in[j], c_gate_b[j], c_cmp_pos[j], c_cmp_w1[j],
                                    c_cmp_w2[j], c_w_out[j], tsel, twin, tcmp)
        wr, br = _router_weights(moe_wr_g[layer], moe_br_g[layer], moe_wr_e[layer], moe_br_e[layer])
        parts = []
        rows = (B * S) // MOE_CHUNKS
        for c in range(MOE_CHUNKS):
            hx, cnt = _out_ln_route(ys, lays, ws, h, ln_g[layer, 0], ln_b[layer, 0], wr, br, c * rows, rows)
            parts.append(_moe(hx, cnt, layer, moe_w_gate, moe_w_up, moe_w_down,
                              ln_g[layer, 1], ln_b[layer, 1]))
        h = jnp.concatenate(parts, axis=0)
    return h.reshape(B, S, D)
```

```python
import functools
import math

import numpy as np
import jax
import jax.numpy as jnp
from jax import lax
from jax.experimental import pallas as pl
from jax.experimental.pallas import tpu as pltpu

F32 = jnp.float32
BF16 = jnp.bfloat16
NEG = -1e30
LOG2E = math.log2(math.e)
VMEM_LIMIT = 48 * 1024 * 1024

D_MODEL = 1024
DEPTH = 2
ALPHA = (2.0 * DEPTH) ** 0.25
LN_EPS = 1e-5
REL_BUCKETS = 32
REL_MAX_DIST = 2048

A_HEADS, A_DIM, A_W = 8, 64, 512
A_PATTERNS = ((128, 1), (512, 4), (2048, 16))
A_BLOCK = 128
B_HEADS, B_DIM, B_W = 4, 128, 512
B_CHUNK = 128
B_CONV = 4
AB_PROJ = 3592
AB_PAD = 3712

C_HEADS, C_GROUPS, C_HPG, C_DIM, C_W = 8, 2, 4, 128, 1024
CMP_BLOCK, CMP_STRIDE, CMP_HIDDEN = 32, 16, 256
SLC_BLOCK, SLC_TOP_N, WIN = 64, 16, 512
C_PROJ = 2584
TQ = 128
CMP_PAD = 128

N_GROUPS, EPG, N_EXPERTS, E_HID = 4, 4, 16, 512
N_BUCKETS = N_GROUPS * 6
TM = 512
TM_MOE = 256
MOE_CHUNKS = 1


def _dot(a, b):
    return jnp.dot(a, b, preferred_element_type=F32)


def _dot_nt(a, b):
    return lax.dot_general(a, b, (((1,), (1,)), ((), ())), preferred_element_type=F32)


def _params(sem):
    return pltpu.CompilerParams(dimension_semantics=sem, vmem_limit_bytes=VMEM_LIMIT)


def _bucket_np(n):
    n = np.maximum(n, 0)
    exact = REL_BUCKETS // 2
    nf = np.maximum(n, 1).astype(np.float64)
    large = exact + (np.log(nf / exact) / math.log(REL_MAX_DIST / exact)
                     * (REL_BUCKETS - exact)).astype(np.int64)
    return np.where(n < exact, n, np.minimum(large, REL_BUCKETS - 1)).astype(np.int32)


def _bias_tab_kernel(tab_ref, idx_ref, out_ref, *, shift, scale):
    R = idx_ref.shape[1]
    RC = 32

    def body(i, carry):
        r0 = pl.multiple_of(i * RC, RC)
        idx = idx_ref[0, pl.ds(r0, RC), :]
        for h in range(8):
            base = tab_ref[REL_BUCKETS - 1, h] if shift else 0.0
            val = jnp.full(idx.shape, (tab_ref[0, h] - base) * scale, F32)
            for b in range(1, REL_BUCKETS):
                val = jnp.where(idx == b, (tab_ref[b, h] - base) * scale, val)
            out_ref[0, h, pl.ds(r0, RC), :] = jnp.where(idx < 0, NEG, val)
        return carry

    lax.fori_loop(0, R // RC, body, 0)


def _bias_tables(rel_bias, idx_np, shift, scale=1.0):
    T, R, C = idx_np.shape
    return pl.pallas_call(
        functools.partial(_bias_tab_kernel, shift=shift, scale=scale),
        grid=(T,),
        in_specs=[pl.BlockSpec(memory_space=pltpu.SMEM),
                  pl.BlockSpec((1, R, C), lambda t: (t, 0, 0))],
        out_specs=pl.BlockSpec((1, 8, R, C), lambda t: (t, 0, 0, 0)),
        out_shape=jax.ShapeDtypeStruct((T, 8, R, C), F32),
        compiler_params=_params(("parallel",)),
        name="bias_tables",
    )(rel_bias.astype(F32), jnp.asarray(idx_np))


def _dilated_idx():
    qi = np.arange(A_BLOCK)[:, None]
    ki = np.arange(2 * A_BLOCK)[None, :]
    j = qi + A_BLOCK - ki
    out = []
    for window, dilation in A_PATTERNS:
        nk = window // dilation
        valid = (j >= 0) & (j <= nk)
        out.append(np.where(valid, _bucket_np(np.maximum(j, 0) * dilation), -1))
    return np.stack(out).astype(np.int32)


def _sel_idx():
    a = np.arange(TQ)[:, None]
    c = np.arange(TQ)[None, :]
    n_delta = -(-(_far_dist() + TQ) // TQ)
    out = []
    for delta in range(-1, n_delta + 1):
        dist = delta * TQ + a - c
        out.append(np.where(dist >= 0, _bucket_np(dist), -1))
    return np.stack(out).astype(np.int32)


def _far_dist():
    n = np.arange(0, 4 * REL_MAX_DIST)
    b = _bucket_np(n)
    return int(np.max(n[b < REL_BUCKETS - 1])) + 1


def _win_idx():
    a = np.arange(TQ)[:, None]
    c = np.arange(TQ)[None, :]
    out = []
    for delta in range(-1, WIN // TQ + 1):
        dist = delta * TQ + a - c
        out.append(np.where((dist >= 0) & (dist < WIN), _bucket_np(dist), -1))
    return np.stack(out).astype(np.int32)


CMP_PER_TILE = TQ // CMP_STRIDE
CMP_CLASSES = TQ // CMP_PER_TILE
CMP_SPLIT = 13


def _cmp_window_start(qb):
    return qb // CMP_CLASSES + (1 if qb % CMP_CLASSES >= CMP_SPLIT else 0)


def _cmp_idx():
    a = np.arange(TQ)[:, None]
    c = np.arange(TQ)[None, :]
    out = []
    for r in range(CMP_CLASSES):
        qb = CMP_CLASSES + r
        i0 = _cmp_window_start(qb) * TQ - CMP_PAD
        for half in range(2):
            dist = qb * TQ + a - ((i0 + half * TQ + c) * CMP_STRIDE + CMP_BLOCK - 1)
            out.append(np.where(dist >= 0, _bucket_np(dist), -1))
        out.append(np.full((TQ, TQ), REL_BUCKETS - 1))
        out.append(np.full((TQ, TQ), -1))
    return np.stack(out).astype(np.int32)


def _nsa_tables(rel_bias):
    tsel = _bias_tables(rel_bias, _sel_idx(), shift=True, scale=LOG2E)
    twin = _bias_tables(rel_bias, _win_idx(), shift=False, scale=LOG2E)
    tcmp = _bias_tables(rel_bias, _cmp_idx(), shift=True, scale=LOG2E)
    return tsel, twin, tcmp.reshape(CMP_CLASSES, 4, 8, TQ, TQ)


def _check_cmp_windows(S):
    far = _far_dist()
    for qb in range(S // TQ):
        i0 = _cmp_window_start(qb) * TQ - CMP_PAD
        s0 = qb * TQ
        assert s0 - ((i0 - 1) * CMP_STRIDE + CMP_BLOCK - 1) >= far
        assert s0 + TQ - 1 - ((i0 + 2 * TQ) * CMP_STRIDE + CMP_BLOCK - 1) < 0


def _residue_col(d, r):
    return (r % 4) * 4 + r // 4 if d == 16 else r


def _lane_chunks_store(ref3, val):
    for c in range(ref3.shape[0]):
        ref3[c] = val[:, c * 128:(c + 1) * 128]


def _to_residue_layout(src3_ref, dst, d, col_of=_residue_col):
    nc, rows, _ = src3_ref.shape
    for r in range(d):
        cb = col_of(d, r)
        for c in range(nc):
            col = (cb * nc + c) * 128
            dst(slice(col, col + 128), src3_ref[c, pl.ds(r, rows // d, stride=d), :])


def _ab_proj_kernel(x_ref, xh_ref, w_ref, cw_ref, gb_ref,
                    aq_ref, ak_ref, av_ref, aq4_ref, ak4_ref, av4_ref, aq16_ref, ak16_ref, av16_ref,
                    bq_ref, bk_ref, bv_ref, bo_ref, g_ref,
                    pre_ref, tmp_ref, *, tiles_per_seq):
    i = pl.program_id(0)
    tm = x_ref.shape[0]
    xb = x_ref[...].astype(BF16)
    for c, scale, outs in ((0, A_DIM ** -0.5, (aq_ref, aq4_ref, aq16_ref)),
                           (1, 1.0, (ak_ref, ak4_ref, ak16_ref)),
                           (2, 1.0, (av_ref, av4_ref, av16_ref))):
        val = _dot(xb, w_ref[:, c * A_W:(c + 1) * A_W]) * scale
        _lane_chunks_store(tmp_ref, val)
        outs[0][...] = val.astype(BF16)
        for d, o_ref in ((4, outs[1]), (16, outs[2])):
            def put(cols, piece, o_ref=o_ref):
                o_ref[:, cols] = piece.astype(BF16)
            _to_residue_layout(tmp_ref, put, d)
    bv_ref[...] = _dot(xb, w_ref[:, 2560:3072]).astype(BF16)
    bo_ref[...] = _dot(xb, w_ref[:, 3072:3584])
    g_ref[...] = _dot(xb, w_ref[:, 3584:AB_PAD]) + gb_ref[...]
    halo = _dot(xh_ref[...].astype(BF16), w_ref[:, 1536:2560])
    halo = jnp.where(i % tiles_per_seq == 0, 0.0, halo)
    pre_ref[0:8, :] = halo
    pre_ref[8:8 + tm, :] = _dot(xb, w_ref[:, 1536:2560])
    y = pre_ref[8:8 + tm, :] * cw_ref[B_CONV - 1:B_CONV, :]
    for k in range(B_CONV - 1):
        s = B_CONV - 1 - k
        y = y + pre_ref[8 - s:8 - s + tm, :] * cw_ref[k:k + 1, :]
    y = y / (1.0 + jnp.exp(-y))
    bq_ref[...] = (y[:, :B_W] * (B_DIM ** -0.5)).astype(BF16)
    bk_ref[...] = y[:, B_W:].astype(BF16)


def _ab_proj(x2, w_pad, conv_w, gate_b_pad, S):
    N = x2.shape[0]
    tm = TM
    tps = S // tm
    row = lambda i: (i, 0)
    fix = lambda i: (0, 0)
    lay = lambda d: [jax.ShapeDtypeStruct((N // d, d * A_W), BF16)] * 3
    lay_spec = lambda d: [pl.BlockSpec((tm // d, d * A_W), row)] * 3
    outs = lay(1) + lay(4) + lay(16) + [jax.ShapeDtypeStruct((N, 512), BF16)] * 3 + [
        jax.ShapeDtypeStruct((N, 512), F32), jax.ShapeDtypeStruct((N, 128), F32)]
    o_specs = (lay_spec(1) + lay_spec(4) + lay_spec(16) + [pl.BlockSpec((tm, 512), row)] * 4
               + [pl.BlockSpec((tm, 128), row)])
    return pl.pallas_call(
        functools.partial(_ab_proj_kernel, tiles_per_seq=tps),
        grid=(N // tm,),
        in_specs=[pl.BlockSpec((tm, D_MODEL), row),
                  pl.BlockSpec((8, D_MODEL), lambda i: (jnp.maximum(i * (tm // 8) - 1, 0), 0)),
                  pl.BlockSpec((D_MODEL, AB_PAD), fix),
                  pl.BlockSpec((B_CONV, 2 * B_W), fix),
                  pl.BlockSpec((1, 128), fix)],
        out_specs=o_specs,
        out_shape=outs,
        scratch_shapes=[pltpu.VMEM((tm + 8, 2 * B_W), F32), pltpu.VMEM((A_W // 128, tm, 128), F32)],
        compiler_params=_params(("parallel",)),
        name="ab_proj",
    )(x2, x2, w_pad, conv_w, gate_b_pad)


def _dilated_kernel(*refs, has_prev, is_last):
    if has_prev:
        q_ref, kp_ref, kc_ref, vp_ref, vc_ref, tab_ref, op_ref, lp_ref = refs[:8]
        rest = refs[8:]
    else:
        q_ref, kp_ref, kc_ref, vp_ref, vc_ref, tab_ref = refs[:6]
        rest = refs[6:]
    outs, scratch = (rest, ()) if is_last else (rest[:2], rest[2:])
    o_ref = outs[0]
    n = pl.program_id(2)
    first = jnp.where(n == 0, NEG, 0.0)
    lane = lax.broadcasted_iota(jnp.int32, (A_BLOCK, 128), 1)
    keep_side = [jnp.where(lane < A_DIM, 1.0, 0.0).astype(BF16), jnp.where(lane < A_DIM, 0.0, 1.0).astype(BF16)]
    ones = jnp.ones((A_BLOCK, 128), BF16)
    m_tile = jnp.zeros((A_BLOCK, 128), F32)
    l_tile = jnp.ones((A_BLOCK, 128), F32)
    unnorm = []
    for j in range(A_HEADS // 2):
        cs = slice(j * 128, (j + 1) * 128)
        q2 = q_ref[0, :, cs]
        kp, kc, vp, vc = kp_ref[0, :, cs], kc_ref[0, :, cs], vp_ref[0, :, cs], vc_ref[0, :, cs]
        acc = None
        for side in range(2):
            h = 2 * j + side
            keep = keep_side[side]
            q = q2 * keep
            sp = _dot_nt(q, kp) + tab_ref[0, h, :, 0:A_BLOCK] + first
            sc = _dot_nt(q, kc) + tab_ref[0, h, :, A_BLOCK:2 * A_BLOCK]
            m = jnp.max(jnp.maximum(sp, sc), axis=1, keepdims=True)
            pp = jnp.exp(sp - m).astype(BF16)
            pc = jnp.exp(sc - m).astype(BF16)
            vpe = jnp.concatenate([vp * keep, ones], axis=1)
            vce = jnp.concatenate([vc * keep, ones], axis=1)
            r = _dot(pp, vpe) + _dot(pc, vce)
            acc = r[:, 0:128] if acc is None else acc + r[:, 0:128]
            m_tile = jnp.where(lane == A_DIM + h, m, m_tile)
            l_tile = jnp.where(lane == A_DIM + h, r[:, 128:256], l_tile)
        unnorm.append(acc)
    stat = (lane >= A_DIM) & (lane < A_DIM + A_HEADS)
    lse = m_tile + jnp.log(l_tile)
    if has_prev:
        lp = lp_ref[0]
        mm = jnp.maximum(lp, lse)
        wp = jnp.exp(lp - mm)
        wc = jnp.exp(lse - mm)
        tot = wp + wc
        scale_prev = jnp.where(stat, wp / tot, 0.0)
        scale_cur = jnp.where(stat, wc / (tot * l_tile), 0.0)
        lse = mm + jnp.log(tot)
    else:
        scale_cur = jnp.where(stat, 1.0 / l_tile, 0.0)
    erow = lax.broadcasted_iota(jnp.int32, (128, A_W), 0)
    ecol = lax.broadcasted_iota(jnp.int32, (128, A_W), 1)
    expand = jnp.where(erow - A_DIM == jnp.right_shift(ecol, A_DIM.bit_length() - 1), 1.0, 0.0).astype(BF16)

    def spread(t):
        hi, mid, lo = _split3(t)
        return _dot(hi, expand) + _dot(mid, expand) + _dot(lo, expand)

    o = jnp.concatenate(unnorm, axis=1) * spread(scale_cur)
    if has_prev:
        o = o + op_ref[0] * spread(scale_prev)
    if is_last:
        o_ref[0] = o.astype(o_ref.dtype)
    else:
        o_scr, l_scr = scratch
        _lane_chunks_store(o_scr, o)
        l_scr[0] = jnp.where(stat, lse, 0.0)

        def put_o(cols, piece):
            o_ref[0, :, cols] = piece

        def put_l(cols, piece):
            outs[1][0, :, cols] = piece

        _to_residue_layout(o_scr, put_o, 4)
        _to_residue_layout(l_scr, put_l, 4)


def _dilated_call(q, k, v, tab, prev, pattern_idx, dilation, B, S, is_last):
    d = dilation
    L = S // d
    nb = L // A_BLOCK
    r3 = lambda a: a.reshape(B, L, a.shape[-1])
    cur = lambda b, r, n: (b, n, r)
    prv = lambda b, r, n: (b, jnp.maximum(n - 1, 0), r)
    blk = pl.BlockSpec((1, A_BLOCK, A_W), cur)
    in_specs = [blk, pl.BlockSpec((1, A_BLOCK, A_W), prv), blk,
                pl.BlockSpec((1, A_BLOCK, A_W), prv), blk,
                pl.BlockSpec((1, 8, A_BLOCK, 2 * A_BLOCK), lambda b, r, n: (pattern_idx, 0, 0, 0))]
    args = [r3(q), r3(k), r3(k), r3(v), r3(v), tab]
    has_prev = prev is not None
    if has_prev:
        in_specs += [blk, pl.BlockSpec((1, A_BLOCK, 128), cur)]
        args += [r3(prev[0]), r3(prev[1])]
    scratch = []
    if is_last:
        out_shape = [jax.ShapeDtypeStruct((B, L, d * A_W), BF16)]
        out_specs = [blk]
    else:
        d2, rows = 4 * d, A_BLOCK // 4
        nxt = lambda b, r, n: (b, n, r)
        out_shape = [jax.ShapeDtypeStruct((B, S // d2, d2 * A_W), F32),
                     jax.ShapeDtypeStruct((B, S // d2, d2 * 128), F32)]
        out_specs = [pl.BlockSpec((1, rows, 4 * A_W), nxt), pl.BlockSpec((1, rows, 4 * 128), nxt)]
        scratch = [pltpu.VMEM((A_W // 128, A_BLOCK, 128), F32), pltpu.VMEM((1, A_BLOCK, 128), F32)]
    res = pl.pallas_call(
        functools.partial(_dilated_kernel, has_prev=has_prev, is_last=is_last),
        grid=(B, d, nb),
        in_specs=in_specs, out_specs=out_specs, out_shape=out_shape, scratch_shapes=scratch,
        compiler_params=_params(("parallel", "parallel", "arbitrary")),
        name="dilated_d%d" % d,
    )(*args)
    return [r.reshape(-1, r.shape[-1]) for r in res]


def _dilated_attention(qkv_by_dilation, tab, B, S):
    prev = None
    for p, (window, d) in enumerate(A_PATTERNS):
        assert window // d == A_BLOCK and S % (d * A_BLOCK) == 0
        assert p == 0 or d == 4 * A_PATTERNS[p - 1][1]
        last = p == len(A_PATTERNS) - 1
        q, k, v = qkv_by_dilation[d]
        prev = _dilated_call(q, k, v, tab, prev, p, d, B, S, last)
    return prev[0]


def _split3(x):
    hi = x.astype(BF16)
    r = x - hi.astype(F32)
    mid = r.astype(BF16)
    lo = (r - mid.astype(F32)).astype(BF16)
    return hi, mid, lo


def _mlstm_kernel(q_ref, k_ref, v_ref, g_ref, bo_ref, hg_ref, y_ref, c_ref, m_ref):
    L = B_CHUNK
    c = pl.program_id(1)

    @pl.when(c == 0)
    def _():
        c_ref[...] = jnp.zeros_like(c_ref)
        m_ref[...] = jnp.zeros_like(m_ref)

    lane = lax.broadcasted_iota(jnp.int32, (L, 128), 1)
    row = lax.broadcasted_iota(jnp.int32, (L, L), 0)
    col = lax.broadcasted_iota(jnp.int32, (L, L), 1)
    tri = row >= col
    is_f = (lane >= B_HEADS) & (lane < 2 * B_HEADS)
    tril = jnp.where(tri, 1.0, 0.0).astype(BF16)
    ones = jnp.ones((L, B_DIM), BF16)
    for bi, h in [(bi, h) for bi in range(q_ref.shape[0]) for h in range(B_HEADS)]:
        if h == 0:
            g = g_ref[bi]
            logf = jnp.minimum(g, 0.0) - jnp.log(1.0 + jnp.exp(-jnp.abs(g)))
            gl = jnp.where(is_f, logf, jnp.where(lane < B_HEADS, g, 0.0))
            hi, mid, lo = _split3(gl)
            cum = _dot(tril, hi) + _dot(tril, mid) + _dot(tril, lo)
            cum_t = cum.T
            gl_t = gl.T
        st = bi * B_HEADS + h
        cs = slice(h * B_DIM, (h + 1) * B_DIM)
        q = q_ref[bi, :, cs]
        k = k_ref[bi, :, cs]
        v_ext = jnp.concatenate([v_ref[bi, :, cs], ones], axis=1)
        b_col = cum[:, B_HEADS + h:B_HEADS + h + 1]
        b_row = cum_t[B_HEADS + h:B_HEADS + h + 1, :]
        i_col = gl[:, h:h + 1]
        i_row = gl_t[h:h + 1, :]
        m_prev = m_ref[st, 0:1, 0:1]
        dm = jnp.where(tri, b_col - b_row + i_row, NEG)
        inter = b_col + m_prev
        m_t = jnp.maximum(inter, jnp.max(dm, axis=1, keepdims=True))
        p = jnp.exp(dm - m_t)
        sqk = _dot_nt(q, k) * p
        sc = jnp.exp(inter - m_t)
        lhs = jnp.concatenate([(sc * q.astype(F32)).astype(BF16), sqk.astype(BF16)], axis=1)
        c_ext = c_ref[st]
        rhs = jnp.concatenate([c_ext.astype(BF16), v_ext], axis=0)
        res = _dot(lhs, rhs)
        num = res[:, :B_DIM]
        den = res[:, B_DIM:]
        hh = num / jnp.maximum(jnp.abs(den), jnp.exp(-m_t))
        b_last = b_col[L - 1:L, :]
        gk = b_last - b_col + i_col
        m_new = jnp.maximum(b_last + m_prev, jnp.max(gk, axis=0, keepdims=True))
        wk = jnp.exp(gk - m_new)
        decay = jnp.exp(b_last + m_prev - m_new)
        kw_t = (wk * k.astype(F32)).T.astype(BF16)
        c_ref[st] = decay * c_ext + _dot(kw_t, v_ext)
        m_ref[st] = jnp.broadcast_to(m_new, (8, 128))
        mu = jnp.mean(hh, axis=1, keepdims=True)
        xc = hh - mu
        var = jnp.mean(xc * xc, axis=1, keepdims=True)
        hn = xc * lax.rsqrt(var + LN_EPS) * hg_ref[:, cs]
        bo = bo_ref[bi, :, cs]
        y_ref[bi, :, cs] = (hn / (1.0 + jnp.exp(-bo))).astype(BF16)


def _mlstm(bq, bk, bv, gates, bo, head_g, B, S):
    nc = S // B_CHUNK
    bb = 1
    r3 = lambda a: a.reshape(B, S, a.shape[-1])
    blk = pl.BlockSpec((bb, B_CHUNK, B_W), lambda b, c: (b, c, 0))
    y = pl.pallas_call(
        _mlstm_kernel,
        grid=(B // bb, nc),
        in_specs=[blk, blk, blk,
                  pl.BlockSpec((bb, B_CHUNK, 128), lambda b, c: (b, c, 0)),
                  blk,
                  pl.BlockSpec((1, B_W), lambda b, c: (0, 0))],
        out_specs=blk,
        out_shape=jax.ShapeDtypeStruct((B, S, B_W), BF16),
        scratch_shapes=[pltpu.VMEM((bb * B_HEADS, B_DIM, 2 * B_DIM), F32),
                        pltpu.VMEM((bb * B_HEADS, 8, 128), F32)],
        compiler_params=_params(("parallel", "arbitrary")),
        name="mlstm",
    )(r3(bq), r3(bk), r3(bv), r3(gates), r3(bo), head_g.reshape(1, B_W).astype(F32))
    return y.reshape(B * S, B_W)


def _layer_norm(z, g, b):
    mu = jnp.mean(z, axis=1, keepdims=True)
    zc = z - mu
    var = jnp.mean(zc * zc, axis=1, keepdims=True)
    return zc * lax.rsqrt(var + LN_EPS) * g + b


def _route(logits, cnt_ref):
    tm = logits.shape[0]
    lt = logits.T
    col = lambda c: lt[c:c + 1, :]
    gl = [col(c) for c in range(N_GROUPS)]
    gmax = functools.reduce(jnp.maximum, gl)
    gsum = sum(jnp.exp(x - gmax) for x in gl)
    g_idx = jnp.full(gmax.shape, N_GROUPS - 1, jnp.int32)
    for c in range(N_GROUPS - 2, -1, -1):
        g_idx = jnp.where(gl[c] == gmax, c, g_idx)
    g_w = 1.0 / gsum
    el = []
    for k in range(EPG):
        x = col(N_GROUPS + (N_GROUPS - 1) * EPG + k)
        for g in range(N_GROUPS - 2, -1, -1):
            x = jnp.where(g_idx == g, col(N_GROUPS + g * EPG + k), x)
        el.append(x)
    v1 = functools.reduce(jnp.maximum, el)
    i1 = jnp.full(v1.shape, EPG - 1, jnp.int32)
    for k in range(EPG - 2, -1, -1):
        i1 = jnp.where(el[k] == v1, k, i1)
    el2 = [jnp.where(i1 == k, -jnp.inf, el[k]) for k in range(EPG)]
    v2 = functools.reduce(jnp.maximum, el2)
    i2 = jnp.full(v2.shape, EPG - 1, jnp.int32)
    for k in range(EPG - 2, -1, -1):
        i2 = jnp.where((el2[k] == v2) & (i1 != k), k, i2)
    t = jnp.exp(v2 - v1)
    w1 = g_w / (1.0 + t)
    w2 = w1 * t
    a = jnp.minimum(i1, i2)
    b = jnp.maximum(i1, i2)
    pair = jnp.where(a == 0, b - 1, jnp.where(a == 1, b + 1, 5))
    bucket = (g_idx * 6 + pair).astype(F32)
    w_lo = jnp.where(i1 < i2, w1, w2)
    w_hi = jnp.where(i1 < i2, w2, w1)
    sub = lax.broadcasted_iota(jnp.int32, (128, tm), 0)
    onehot_t = jnp.where(sub.astype(F32) == bucket, 1.0, 0.0)
    srow = lax.broadcasted_iota(jnp.int32, (tm, tm), 0)
    scol = lax.broadcasted_iota(jnp.int32, (tm, tm), 1)
    before = jnp.where(srow < scol, 1.0, 0.0).astype(BF16)
    oh = onehot_t.astype(BF16)
    carry = cnt_ref[...]
    prior = _dot(oh, before) + jnp.concatenate([carry] * (tm // 128), axis=1)
    rank = jnp.sum(onehot_t * prior, axis=0, keepdims=True)
    cnt_ref[...] = carry + _dot(oh, jnp.ones((tm, 128), BF16))
    out_t = jnp.where(sub == 0, bucket, jnp.where(sub == 1, w_lo, jnp.where(sub == 2, w_hi,
                      jnp.where(sub == 3, rank, 0.0))))
    return out_t.T


def _out_ln_route_kernel(*refs, n_in, layouts):
    y_refs = refs[:n_in]
    w_refs = refs[n_in:2 * n_in]
    x_ref, g_ref, b_ref, wrh_ref, wrl_ref, br_ref, h_ref, c_ref, cnt_ref = refs[2 * n_in:2 * n_in + 9]
    pos_refs = list(refs[2 * n_in + 9:])
    tm = x_ref.shape[0]

    @pl.when(pl.program_id(0) == 0)
    def _():
        cnt_ref[...] = jnp.zeros_like(cnt_ref)

    y = None
    for i in range(n_in):
        d = layouts[i]
        if d == 1:
            lhs = y_refs[i][...]
        else:
            s_ref = pos_refs.pop(0)
            nc = s_ref.shape[0]
            for r in range(d):
                cb = _residue_col(d, r)
                for c in range(nc):
                    col = (cb * nc + c) * 128
                    s_ref[c, pl.ds(r, tm // d, stride=d), :] = y_refs[i][:, col:col + 128].astype(F32)
            lhs = jnp.concatenate([s_ref[c] for c in range(nc)], axis=1).astype(BF16)
        t = _dot(lhs, w_refs[i][...])
        y = t if y is None else y + t
    hn = _layer_norm(ALPHA * x_ref[...] + y, g_ref[...], b_ref[...])
    h_ref[:, 0:D_MODEL] = hn
    hi = hn.astype(BF16)
    lo = (hn - hi.astype(F32)).astype(BF16)
    logits = (_dot_nt(hi, wrh_ref[...]) + _dot_nt(lo, wrh_ref[...]) + _dot_nt(hi, wrl_ref[...])
              + br_ref[...])
    h_ref[:, D_MODEL:D_MODEL + 128] = _route(logits, cnt_ref)
    c_ref[...] = cnt_ref[...]


def _out_ln_route(ys, layouts, ws, x2, ln_g, ln_b, wr, br, row_start, N):
    tm = TM
    off = row_start // tm
    row = lambda i: (i + off, 0)
    fix = lambda i: (0, 0)
    wr_hi = wr.astype(BF16)
    wr_lo = (wr - wr_hi.astype(F32)).astype(BF16)
    in_specs = ([pl.BlockSpec((tm // d, y.shape[1]), row) for y, d in zip(ys, layouts)]
                + [pl.BlockSpec(w.shape, fix) for w in ws]
                + [pl.BlockSpec((tm, D_MODEL), row),
                   pl.BlockSpec((1, D_MODEL), fix), pl.BlockSpec((1, D_MODEL), fix),
                   pl.BlockSpec((128, D_MODEL), fix), pl.BlockSpec((128, D_MODEL), fix),
                   pl.BlockSpec((1, 128), fix)])
    return pl.pallas_call(
        functools.partial(_out_ln_route_kernel, n_in=len(ys), layouts=tuple(layouts)),
        grid=(N // tm,),
        in_specs=in_specs,
        out_specs=[pl.BlockSpec((tm, D_MODEL + 128), lambda i: (i, 0)), pl.BlockSpec((128, 128), fix)],
        out_shape=[jax.ShapeDtypeStruct((N, D_MODEL + 128), F32), jax.ShapeDtypeStruct((128, 128), F32)],
        scratch_shapes=[pltpu.VMEM((128, 128), F32)] + [
            pltpu.VMEM((y.shape[1] // d // 128, tm, 128), F32) for y, d in zip(ys, layouts) if d != 1],
        compiler_params=_params(("arbitrary",)),
        name="out_ln_route",
    )(*ys, *ws, x2, ln_g.reshape(1, -1), ln_b.reshape(1, -1), wr_hi, wr_lo, br)


def _router_weights(wr_g, br_g, wr_e, br_e):
    we = wr_e.transpose(0, 2, 1).reshape(N_GROUPS * EPG, D_MODEL)
    w = jnp.concatenate([wr_g.T, we], axis=0)
    w = jnp.pad(w, ((0, 128 - w.shape[0]), (0, 0)))
    b = jnp.concatenate([br_g, br_e.reshape(-1)])
    b = jnp.pad(b, (0, 128 - b.shape[0])).reshape(1, 128)
    return w.astype(F32), b.astype(F32)


_PAIRS = ((0, 1), (0, 2), (0, 3), (1, 2), (1, 3), (2, 3))


def _moe_kernel(elo_ref, ehi_ref, chg_ref, nt_ref,
                x_ref, wgl_ref, wul_ref, wdl_ref, wgh_ref, wuh_ref, wdh_ref,
                g_ref, b_ref, o_ref, wg_s, wu_s, wd_s):
    t = pl.program_id(0)

    @pl.when(chg_ref[t] == 1)
    def _():
        wg_s[0] = wgl_ref[0, 0].astype(BF16)
        wu_s[0] = wul_ref[0, 0].astype(BF16)
        wd_s[0] = wdl_ref[0, 0].astype(BF16)
        wg_s[1] = wgh_ref[0, 0].astype(BF16)
        wu_s[1] = wuh_ref[0, 0].astype(BF16)
        wd_s[1] = wdh_ref[0, 0].astype(BF16)

    @pl.when(t < nt_ref[0])
    def _():
        x = x_ref[:, 0:D_MODEL]
        xb = x.astype(BF16)
        r = x_ref[:, D_MODEL:D_MODEL + 128]
        acc = None
        for e in range(2):
            a = _dot(xb, wg_s[e])
            u = _dot(xb, wu_s[e])
            hcur = (a / (1.0 + jnp.exp(-a))) * u * r[:, 1 + e:2 + e]
            y = _dot(hcur.astype(BF16), wd_s[e])
            acc = y if acc is None else acc + y
        o_ref[...] = _layer_norm(ALPHA * x + acc, g_ref[...], b_ref[...])

    @pl.when(t >= nt_ref[0])
    def _():
        o_ref[...] = jnp.zeros_like(o_ref)


def _moe(hx, cnt, layer, w_gate, w_up, w_down, ln_g, ln_b):
    N = hx.shape[0]
    tm = TM_MOE
    n_tiles = N // tm + N_BUCKETS
    n_pad = n_tiles * tm
    bucket = hx[:, D_MODEL].astype(jnp.int32)
    rank = hx[:, D_MODEL + 3].astype(jnp.int32)
    counts = cnt[:N_BUCKETS, 0].astype(jnp.int32)
    padded = ((counts + tm - 1) // tm) * tm
    ends = jnp.cumsum(padded)
    offs = ends - padded
    dest = offs[bucket] + rank
    src = (jnp.arange(n_pad, dtype=jnp.int32) % N).at[dest].set(
        jnp.arange(N, dtype=jnp.int32), mode="promise_in_bounds", unique_indices=True)
    tile_start = jnp.arange(n_tiles, dtype=jnp.int32) * tm
    n_used = (ends[-1] // tm).astype(jnp.int32)
    tb = jnp.sum((tile_start[:, None] >= ends[None, :]).astype(jnp.int32), axis=1)
    tb_last = jnp.take(tb, jnp.maximum(n_used - 1, 0))
    tb = jnp.where(tile_start < ends[-1], tb, tb_last)
    pairs = jnp.asarray(_PAIRS, jnp.int32)
    elo = (tb // 6) * EPG + pairs[tb % 6, 0]
    ehi = (tb // 6) * EPG + pairs[tb % 6, 1]
    chg = jnp.concatenate([jnp.ones((1,), jnp.int32), (tb[1:] != tb[:-1]).astype(jnp.int32)])
    xs = hx.at[src].get(mode="promise_in_bounds")

    row = lambda t, *_: (t, 0)
    fix = lambda t, *_: (0, 0)
    wlo = lambda t, elo, ehi, chg, nt: (layer, elo[t], 0, 0)
    whi = lambda t, elo, ehi, chg, nt: (layer, ehi[t], 0, 0)
    up_spec = lambda im: pl.BlockSpec((1, 1, D_MODEL, E_HID), im)
    dn_spec = lambda im: pl.BlockSpec((1, 1, E_HID, D_MODEL), im)
    grid_spec = pltpu.PrefetchScalarGridSpec(
        num_scalar_prefetch=4,
        grid=(n_tiles,),
        in_specs=[pl.BlockSpec((tm, D_MODEL + 128), row),
                  up_spec(wlo), up_spec(wlo), dn_spec(wlo),
                  up_spec(whi), up_spec(whi), dn_spec(whi),
                  pl.BlockSpec((1, D_MODEL), fix), pl.BlockSpec((1, D_MODEL), fix)],
        out_specs=pl.BlockSpec((tm, D_MODEL), row),
        scratch_shapes=[pltpu.VMEM((2, D_MODEL, E_HID), BF16),
                        pltpu.VMEM((2, D_MODEL, E_HID), BF16),
                        pltpu.VMEM((2, E_HID, D_MODEL), BF16)])
    out_sorted = pl.pallas_call(
        _moe_kernel,
        grid_spec=grid_spec,
        out_shape=jax.ShapeDtypeStruct((n_pad, D_MODEL), F32),
        compiler_params=_params(("arbitrary",)),
        name="moe",
    )(elo, ehi, chg, n_used.reshape(1), xs, w_gate, w_up, w_down, w_gate, w_up, w_down,
      ln_g.reshape(1, -1), ln_b.reshape(1, -1))
    return out_sorted.at[dest].get(mode="promise_in_bounds", unique_indices=True)


def _c_proj_kernel(x_ref, w_ref, gb_ref, q_ref, kc_ref, kv_ref, g_ref, tmp_ref):
    xb = x_ref[...].astype(BF16)
    q_ref[...] = (_dot(xb, w_ref[:, 0:C_W]) * (C_DIM ** -0.5 * LOG2E)).astype(BF16)
    for i in range(4):
        tmp_ref[0] = _dot(xb, w_ref[:, C_W + i * 128:C_W + (i + 1) * 128])

        def put(cols, piece, i=i):
            kc_ref[i, :, cols] = piece.astype(BF16)

        _to_residue_layout(tmp_ref, put, CMP_STRIDE, col_of=lambda d, r: r)
    kv_ref[...] = _dot(xb, w_ref[:, C_W + 512:C_W + 1536]).astype(BF16)
    z = _dot(xb, w_ref[:, C_W + 1536:C_W + 1792]) + gb_ref[...]
    g_ref[...] = 1.0 / (1.0 + jnp.exp(-z))


def _c_proj(x2, w_pad, gb_pad):
    N = x2.shape[0]
    tm = TM
    row = lambda i: (i, 0)
    fix = lambda i: (0, 0)
    wcols = w_pad.shape[1]
    return pl.pallas_call(
        _c_proj_kernel,
        grid=(N // tm,),
        in_specs=[pl.BlockSpec((tm, D_MODEL), row), pl.BlockSpec((D_MODEL, wcols), fix),
                  pl.BlockSpec((1, 256), fix)],
        out_specs=[pl.BlockSpec((tm, C_W), row),
                   pl.BlockSpec((4, tm // CMP_STRIDE, CMP_STRIDE * C_DIM), lambda i: (0, i, 0)),
                   pl.BlockSpec((tm, 1024), row), pl.BlockSpec((tm, 256), row)],
        out_shape=[jax.ShapeDtypeStruct((N, C_W), BF16),
                   jax.ShapeDtypeStruct((4, N // CMP_STRIDE, CMP_STRIDE * C_DIM), BF16),
                   jax.ShapeDtypeStruct((N, 1024), BF16), jax.ShapeDtypeStruct((N, 256), F32)],
        scratch_shapes=[pltpu.VMEM((1, tm, 128), F32)],
        compiler_params=_params(("parallel",)),
        name="c_proj",
    )(x2, w_pad, gb_pad)


def _c_weights(w_in, gate_b):
    gcols = []
    gb = []
    for g in range(C_GROUPS):
        idx = [C_PROJ - 3 * C_HEADS + br * C_HEADS + g * C_HPG + j for br in range(3) for j in range(C_HPG)]
        gcols.append(jnp.pad(w_in[:, np.asarray(idx)], ((0, 0), (0, 128 - len(idx)))))
        gb.append(jnp.pad(gate_b[np.asarray(idx) - (C_PROJ - 3 * C_HEADS)], (0, 128 - len(idx))))
    w = jnp.concatenate([w_in[:, :C_PROJ - 3 * C_HEADS]] + gcols, axis=1).astype(BF16)
    return w, jnp.concatenate(gb).reshape(1, 256).astype(F32)


def _compress_kernel(seg_ref, w1_ref, pos_ref, w1f_ref, w2_ref, o_ref):
    n_seg = seg_ref.shape[1]
    ul = _dot(seg_ref[0], w1_ref[0])
    u = ul[:, :CMP_HIDDEN]
    lnext = pltpu.roll(ul[:, CMP_HIDDEN:], n_seg - 1, 0)
    cpos = _dot(pos_ref[0], w1f_ref[0])[0:1, :]
    pre = u + lnext + cpos
    act = 0.5 * pre * (1.0 + jnp.tanh(math.sqrt(2.0 / math.pi) * (pre + 0.044715 * pre * pre * pre)))
    o_ref[0, 0, 0:CMP_PAD, :] = jnp.zeros((CMP_PAD, C_DIM), BF16)
    o_ref[0, 0, CMP_PAD:CMP_PAD + n_seg, :] = _dot(act.astype(BF16), w2_ref[0]).astype(BF16)


def _compress(kc, cmp_pos, cmp_w1, cmp_w2, B, S):
    n_seg = S // CMP_STRIDE
    half = CMP_STRIDE * C_DIM
    seg = kc.reshape(4 * B, n_seg, half)
    w1 = cmp_w1.astype(BF16)
    w1_ul = jnp.concatenate([w1[:, :half], w1[:, half:]], axis=2)
    pos = jnp.broadcast_to(cmp_pos.reshape(2, 1, CMP_BLOCK * C_DIM), (2, 8, CMP_BLOCK * C_DIM)).astype(BF16)
    out = pl.pallas_call(
        _compress_kernel,
        grid=(4, B),
        in_specs=[pl.BlockSpec((1, n_seg, half), lambda i, b: (i * B + b, 0, 0)),
                  pl.BlockSpec((1, half, 2 * CMP_HIDDEN), lambda i, b: (i // 2, 0, 0)),
                  pl.BlockSpec((1, 8, CMP_BLOCK * C_DIM), lambda i, b: (i // 2, 0, 0)),
                  pl.BlockSpec((1, CMP_BLOCK * C_DIM, CMP_HIDDEN), lambda i, b: (i // 2, 0, 0)),
                  pl.BlockSpec((1, CMP_HIDDEN, C_DIM), lambda i, b: (i // 2, 0, 0))],
        out_specs=pl.BlockSpec((1, 1, CMP_PAD + n_seg, C_DIM), lambda i, b: (i, b, 0, 0)),
        out_shape=jax.ShapeDtypeStruct((4, B, CMP_PAD + n_seg, C_DIM), BF16),
        compiler_params=_params(("parallel", "parallel")),
        name="compress",
    )(seg, w1_ul, pos, w1, cmp_w2.astype(BF16))
    return out


def _overlap_np(n_cmp_pad, n_slc):
    i = np.arange(n_cmp_pad)[:, None] - CMP_PAD
    m = np.arange(n_slc)[None, :]
    start = i * CMP_STRIDE
    ov = (start < (m + 1) * SLC_BLOCK) & (start + CMP_BLOCK - 1 >= m * SLC_BLOCK) & (i >= 0)
    return ov.astype(np.float32)


def _flash_update(s_heads, vext, m_ref, acc_ref, p_ref):
    W = s_heads[0].shape[1]
    nw = W // 128
    alphas = []
    for h, s in enumerate(s_heads):
        rs = slice(h * TQ, (h + 1) * TQ)
        m_prev = m_ref[rs, :]
        smax = functools.reduce(jnp.maximum, [s[:, i * 128:(i + 1) * 128] for i in range(nw)])
        m_new = jnp.maximum(m_prev, jnp.max(smax, axis=1, keepdims=True))
        alpha = jnp.exp2(m_prev - m_new)
        p = jnp.exp2(s - (m_new if nw == 1 else jnp.concatenate([m_new] * nw, axis=1)))
        m_ref[rs, :] = m_new
        p_ref[rs, 0:W] = p.astype(BF16)
        alphas.append(alpha)
    a = jnp.concatenate(alphas, axis=0)
    acc_ref[...] = jnp.concatenate([a, a], axis=1) * acc_ref[...] + _dot(p_ref[:, 0:W], vext)


def _nsa_kernel(q_ref, ks_ref, vs_ref, kw_ref, vw_ref, kc_ref, vc_ref, ov_ref, mk_ref,
                tsel_ref, twin_ref, tcmp_ref, g_ref, o_ref,
                m_ref, acc_ref, p_ref, pw_ref, sa_ref, sb_ref, oc_ref, ow_ref, *, n_sel_tab, top_n):
    qb = pl.program_id(2)
    H = C_HPG
    NB = ov_ref.shape[1]
    q_all = jnp.concatenate([q_ref[:, h * C_DIM:(h + 1) * C_DIM] for h in range(H)], axis=0)
    heads = lambda x: [x[h * TQ:(h + 1) * TQ] for h in range(H)]

    def reset():
        m_ref[...] = jnp.full(m_ref.shape, NEG, F32)
        acc_ref[...] = jnp.zeros(acc_ref.shape, F32)

    def softmax_rows(s):
        nw = s.shape[1] // 128
        smax = functools.reduce(jnp.maximum, [s[:, i * 128:(i + 1) * 128] for i in range(nw)])
        m = jnp.broadcast_to(jnp.max(smax, axis=1, keepdims=True), (TQ, 128))
        return jnp.exp2(s - jnp.concatenate([m] * nw, axis=1)), m

    t0 = qb // CMP_CLASSES + jnp.where(qb % CMP_CLASSES >= CMP_SPLIT, 1, 0)
    n_ct = kc_ref.shape[2] // TQ
    wc = n_ct * TQ
    tile_kind = [3] + [jnp.where(t == t0, 0, jnp.where(t == t0 + 1, 1, jnp.where(t < t0, 2, 3)))
                       for t in range(1, n_ct)]
    s_c = _dot_nt(q_all, kc_ref[0, 0])
    vext_c = jnp.concatenate([vc_ref[0, 0], ov_ref[...]], axis=1)
    inv_c = []
    for h in range(H):
        bias = jnp.concatenate([tcmp_ref[0, tile_kind[t], h] for t in range(n_ct)], axis=1)
        p, m = softmax_rows(s_c[h * TQ:(h + 1) * TQ] + bias)
        l = jnp.sum(p, axis=1, keepdims=True)
        inv_c.append(jnp.where(m > 0.5 * NEG, 1.0 / l, 0.0))
        p_ref[h * TQ:(h + 1) * TQ, 0:wc] = p.astype(BF16)
    res_c = _dot(p_ref[:, 0:wc], vext_c)
    imp = None
    for h in range(H):
        r = res_c[h * TQ:(h + 1) * TQ] * jnp.concatenate([inv_c[h], inv_c[h]], axis=1)
        oc_ref[h * TQ:(h + 1) * TQ, :] = r[:, 0:C_DIM]
        imp = r[:, C_DIM:] if imp is None else imp + r[:, C_DIM:]

    n_wt = WIN // TQ + 1
    ww = n_wt * TQ
    st = jnp.maximum(qb - (n_wt - 1), 0)
    r0w = pl.multiple_of(st * TQ, TQ)
    s_w = _dot_nt(q_all, kw_ref[pl.ds(r0w, ww), :])
    vext_w = jnp.concatenate([vw_ref[pl.ds(r0w, ww), :], jnp.ones((ww, C_DIM), BF16)], axis=1)
    widx = [jnp.maximum(qb - (st + c) + 1, 0) for c in range(n_wt)]
    for h in range(H):
        bias = jnp.concatenate([twin_ref[i, h] for i in widx], axis=1)
        p, _ = softmax_rows(s_w[h * TQ:(h + 1) * TQ] + bias)
        pw_ref[h * TQ:(h + 1) * TQ, :] = p.astype(BF16)
    res_w = _dot(pw_ref[...], vext_w)
    ow_ref[...] = res_w[:, 0:C_DIM] / res_w[:, C_DIM:]

    shift = SLC_BLOCK.bit_length() - 1
    qpos = qb * TQ + lax.broadcasted_iota(jnp.int32, (TQ, NB), 0)
    mblk = lax.broadcasted_iota(jnp.int32, (TQ, NB), 1)
    qblk = jnp.right_shift(qpos, shift)
    forced = (mblk == 0) | (mblk == qblk) | (mblk == qblk - 1)
    score = jnp.where(forced, 3e38, jnp.where(jnp.left_shift(mblk, shift) <= qpos, imp, NEG))
    score_t = score.T
    blk_t = lax.broadcasted_iota(jnp.int32, (NB, TQ), 0).astype(F32)
    sel_t = jnp.zeros((NB, TQ), F32)
    for _ in range(top_n):
        mx = jnp.max(score_t, axis=0, keepdims=True)
        idx = jnp.min(jnp.where(score_t == mx, blk_t, float(NB)), axis=0, keepdims=True)
        pick = blk_t == idx
        sel_t = jnp.where(pick, 1.0, sel_t)
        score_t = jnp.where(pick, -3e38, score_t)
    unsel = (1.0 - sel_t.T).astype(BF16)

    reset()
    NT = 4
    TK = NT * TQ
    q_aug = jnp.concatenate([q_all, jnp.concatenate([unsel] * H, axis=0)], axis=1)
    ones_k = jnp.ones((TK, C_DIM), BF16)
    n_steps = qb // NT + 1

    def sel_logits(kq, s_ref, near):
        kc = jnp.minimum(kq, n_steps - 1)
        r0 = pl.multiple_of(kc * TK, TK)
        k_aug = jnp.concatenate([ks_ref[pl.ds(r0, TK), :], mk_ref[pl.ds(r0, TK), :]], axis=1)
        s = _dot_nt(q_aug, k_aug)
        if not near:
            s_ref[...] = s
            return
        idx = [jnp.where(kq < n_steps, jnp.clip(qb - (NT * kc + c) + 1, 0, n_sel_tab - 1), 0)
               for c in range(NT)]
        for h in range(H):
            bias = jnp.concatenate([tsel_ref[i, h] for i in idx], axis=1)
            s_ref[h * TQ:(h + 1) * TQ, :] = s[h * TQ:(h + 1) * TQ] + bias

    def sel_consume(kq, s_ref):
        r0 = pl.multiple_of(jnp.minimum(kq, n_steps - 1) * TK, TK)
        vext = jnp.concatenate([vs_ref[pl.ds(r0, TK), :], ones_k], axis=1)
        _flash_update([s_ref[h * TQ:(h + 1) * TQ, :] for h in range(H)], vext, m_ref, acc_ref, p_ref)

    def sel_run(k_lo, n2, near):
        @pl.when(n2 > 0)
        def _():
            sel_logits(k_lo, sa_ref, near)

        def body(j, carry):
            k = k_lo + 2 * j
            sel_logits(k + 1, sb_ref, near)
            sel_consume(k, sa_ref)
            sel_logits(k + 2, sa_ref, near)
            sel_consume(k + 1, sb_ref)
            return carry

        lax.fori_loop(0, n2, body, 0)

    n_far = jnp.maximum((qb + 1 - (n_sel_tab - 2)) // NT, 0)
    far2 = n_far // 2
    sel_run(0, far2, near=False)
    sel_run(2 * far2, (n_steps - 2 * far2 + 1) // 2, near=True)
    g = g_ref[...]
    for h in range(H):
        rs = slice(h * TQ, (h + 1) * TQ)
        out_s = acc_ref[rs, 0:C_DIM] / acc_ref[rs, C_DIM:2 * C_DIM]
        o = (g[:, h:h + 1] * oc_ref[rs, :] + g[:, H + h:H + h + 1] * out_s
             + g[:, 2 * H + h:2 * H + h + 1] * ow_ref[rs, :])
        o_ref[:, h * C_DIM:(h + 1) * C_DIM] = o.astype(BF16)


def _nsa_attention(q, kv, kvc, gates, tsel, twin, tcmp, B, S):
    N = B * S
    QT = S // TQ
    n_slc = S // SLC_BLOCK
    n_cmp_pad = kvc.shape[2]
    NB = 128
    assert n_slc <= NB and n_cmp_pad % TQ == 0 and QT % 4 == 0
    ov = jnp.asarray(_overlap_np(n_cmp_pad, NB), BF16)
    mk = jnp.asarray(np.where(np.arange(S)[:, None] // SLC_BLOCK == np.arange(NB)[None, :], NEG, 0.0), BF16)
    n_sel_delta = tsel.shape[0]
    kvspec = lambda c: pl.BlockSpec((S, C_DIM), lambda b, g, t: (b, c + g))
    cspec = lambda kvi: pl.BlockSpec((1, 1, n_cmp_pad, C_DIM), lambda b, g, t: (kvi * 2 + g, b, 0, 0))
    return pl.pallas_call(
        functools.partial(_nsa_kernel, n_sel_tab=n_sel_delta, top_n=min(SLC_TOP_N, n_slc)),
        grid=(B, C_GROUPS, QT),
        in_specs=[pl.BlockSpec((TQ, C_HPG * C_DIM), lambda b, g, t: (b * QT + t, g)),
                  kvspec(0), kvspec(2), kvspec(4), kvspec(6),
                  cspec(0), cspec(1),
                  pl.BlockSpec((n_cmp_pad, NB), lambda b, g, t: (0, 0)),
                  pl.BlockSpec((S, NB), lambda b, g, t: (0, 0)),
                  pl.BlockSpec((n_sel_delta, C_HPG, TQ, TQ), lambda b, g, t: (0, g, 0, 0)),
                  pl.BlockSpec((twin.shape[0], C_HPG, TQ, TQ), lambda b, g, t: (0, g, 0, 0)),
                  pl.BlockSpec((1, 4, C_HPG, TQ, TQ), lambda b, g, t: (t % CMP_CLASSES, 0, g, 0, 0)),
                  pl.BlockSpec((TQ, 128), lambda b, g, t: (b * QT + t, g))],
        out_specs=pl.BlockSpec((TQ, C_HPG * C_DIM), lambda b, g, t: (b * QT + t, g)),
        out_shape=jax.ShapeDtypeStruct((N, C_W), BF16),
        scratch_shapes=[pltpu.VMEM((C_HPG * TQ, 128), F32),
                        pltpu.VMEM((C_HPG * TQ, 2 * C_DIM), F32),
                        pltpu.VMEM((C_HPG * TQ, max(n_cmp_pad, 4 * TQ)), BF16),
                        pltpu.VMEM((C_HPG * TQ, WIN + TQ), BF16),
                        pltpu.VMEM((C_HPG * TQ, 4 * TQ), F32), pltpu.VMEM((C_HPG * TQ, 4 * TQ), F32),
                        pltpu.VMEM((C_HPG * TQ, C_DIM), F32), pltpu.VMEM((C_HPG * TQ, C_DIM), F32)],
        compiler_params=_params(("parallel", "parallel", "arbitrary")),
        name="nsa",
    )(q, kv, kv, kv, kv, kvc, kvc, ov, mk, tsel, twin, tcmp, gates)


def _layer_ab(h2, B, S, w_in, gate_b, conv_w, head_g, w_out, dil_tab):
    w_pad = jnp.pad(w_in, ((0, 0), (0, AB_PAD - AB_PROJ))).astype(BF16)
    gb_pad = jnp.pad(gate_b, (0, 128 - gate_b.shape[0])).reshape(1, 128).astype(F32)
    (aq, ak, av, aq4, ak4, av4, aq16, ak16, av16,
     bq, bk, bv, bo, gates) = _ab_proj(h2, w_pad, conv_w.astype(F32), gb_pad, S)
    ya = _dilated_attention({1: (aq, ak, av), 4: (aq4, ak4, av4), 16: (aq16, ak16, av16)}, dil_tab, B, S)
    yb = _mlstm(bq, bk, bv, gates, bo, head_g, B, S)
    wo = w_out.astype(BF16)
    return [ya, yb], [A_PATTERNS[-1][1], 1], [wo[:A_W], wo[A_W:]]


def _layer_c(h2, B, S, w_in, gate_b, cmp_pos, cmp_w1, cmp_w2, w_out, tsel, twin, tcmp):
    w_pad, gb_pad = _c_weights(w_in, gate_b)
    q, kc, kv, gates = _c_proj(h2, w_pad, gb_pad)
    kvc = _compress(kc, cmp_pos, cmp_w1, cmp_w2, B, S)
    out = _nsa_attention(q, kv, kvc, gates, tsel, twin, tcmp, B, S)
    return [out], [1], [w_out.astype(BF16)]


def kernel(x, rel_bias, ln_g, ln_b, ab_w_in, ab_gate_b, ab_conv, ab_head_norm, ab_w_out,
           c_w_in, c_gate_b, c_cmp_pos, c_cmp_w1, c_cmp_w2, c_w_out,
           moe_wr_g, moe_br_g, moe_wr_e, moe_br_e, moe_w_gate, moe_w_up, moe_w_down):
    B, S, D = x.shape
    assert D == D_MODEL and S % (TM) == 0 and S % (16 * A_BLOCK) == 0
    _check_cmp_windows(S)
    dil_tab = _bias_tables(rel_bias, _dilated_idx(), shift=False)
    tsel, twin, tcmp = _nsa_tables(rel_bias)
    h = x.reshape(B * S, D)
    for layer in range(DEPTH):
        j = layer // 2
        if layer % 2 == 0:
            ys, lays, ws = _layer_ab(h, B, S, ab_w_in[j], ab_gate_b[j], ab_conv[j], ab_head_norm[j],
                                     ab_w_out[j], dil_tab)
        else:
            ys, lays, ws = _layer_c(h, B, S, c_w_in[j], c_gate_b[j], c_cmp_pos[j], c_cmp_w1[j],
                                    c_cmp_w2[j], c_w_out[j], tsel, twin, tcmp)
        wr, br = _router_weights(moe_wr_g[layer], moe_br_g[layer], moe_wr_e[layer], moe_br_e[layer])
        parts = []
        rows = (B * S) // MOE_CHUNKS
        for c in range(MOE_CHUNKS):
            hx, cnt = _out_ln_route(ys, lays, ws, h, ln_g[layer, 0], ln_b[layer, 0], wr, br, c * rows, rows)
            parts.append(_moe(hx, cnt, layer, moe_w_gate, moe_w_up, moe_w_down,
                              ln_g[layer, 1], ln_b[layer, 1]))
        h = jnp.concatenate(parts, axis=0)
    return h.reshape(B, S, D)
```

```python
import functools
import math

import numpy as np
import jax
import jax.numpy as jnp
from jax import lax
from jax.experimental import pallas as pl
from jax.experimental.pallas import tpu as pltpu

F32 = jnp.float32
BF16 = jnp.bfloat16
NEG = -1e30
LOG2E = math.log2(math.e)
VMEM_LIMIT = 48 * 1024 * 1024

D_MODEL = 1024
DEPTH = 2
ALPHA = (2.0 * DEPTH) ** 0.25
LN_EPS = 1e-5
REL_BUCKETS = 32
REL_MAX_DIST = 2048

A_HEADS, A_DIM, A_W = 8, 64, 512
A_PATTERNS = ((128, 1), (512, 4), (2048, 16))
A_BLOCK = 128
B_HEADS, B_DIM, B_W = 4, 128, 512
B_CHUNK = 128
B_CONV = 4
AB_PROJ = 3592
AB_PAD = 3712

C_HEADS, C_GROUPS, C_HPG, C_DIM, C_W = 8, 2, 4, 128, 1024
CMP_BLOCK, CMP_STRIDE, CMP_HIDDEN = 32, 16, 256
SLC_BLOCK, SLC_TOP_N, WIN = 64, 16, 512
C_PROJ = 2584
TQ = 128
CMP_PAD = 128

N_GROUPS, EPG, N_EXPERTS, E_HID = 4, 4, 16, 512
N_BUCKETS = N_GROUPS * 6
TM = 512
TM_MOE = 256
MOE_CHUNKS = 1


def _dot(a, b):
    return jnp.dot(a, b, preferred_element_type=F32)


def _dot_nt(a, b):
    return lax.dot_general(a, b, (((1,), (1,)), ((), ())), preferred_element_type=F32)


def _params(sem):
    return pltpu.CompilerParams(dimension_semantics=sem, vmem_limit_bytes=VMEM_LIMIT)


def _bucket_np(n):
    n = np.maximum(n, 0)
    exact = REL_BUCKETS // 2
    nf = np.maximum(n, 1).astype(np.float64)
    large = exact + (np.log(nf / exact) / math.log(REL_MAX_DIST / exact)
                     * (REL_BUCKETS - exact)).astype(np.int64)
    return np.where(n < exact, n, np.minimum(large, REL_BUCKETS - 1)).astype(np.int32)


def _bias_tab_kernel(tab_ref, idx_ref, out_ref, *, shift, scale):
    R = idx_ref.shape[1]
    RC = 32

    def body(i, carry):
        r0 = pl.multiple_of(i * RC, RC)
        idx = idx_ref[0, pl.ds(r0, RC), :]
        for h in range(8):
            base = tab_ref[REL_BUCKETS - 1, h] if shift else 0.0
            val = jnp.full(idx.shape, (tab_ref[0, h] - base) * scale, F32)
            for b in range(1, REL_BUCKETS):
                val = jnp.where(idx == b, (tab_ref[b, h] - base) * scale, val)
            out_ref[0, h, pl.ds(r0, RC), :] = jnp.where(idx < 0, NEG, val)
        return carry

    lax.fori_loop(0, R // RC, body, 0)


def _bias_tables(rel_bias, idx_np, shift, scale=1.0):
    T, R, C = idx_np.shape
    return pl.pallas_call(
        functools.partial(_bias_tab_kernel, shift=shift, scale=scale),
        grid=(T,),
        in_specs=[pl.BlockSpec(memory_space=pltpu.SMEM),
                  pl.BlockSpec((1, R, C), lambda t: (t, 0, 0))],
        out_specs=pl.BlockSpec((1, 8, R, C), lambda t: (t, 0, 0, 0)),
        out_shape=jax.ShapeDtypeStruct((T, 8, R, C), F32),
        compiler_params=_params(("parallel",)),
        name="bias_tables",
    )(rel_bias.astype(F32), jnp.asarray(idx_np))


def _dilated_idx():
    qi = np.arange(A_BLOCK)[:, None]
    ki = np.arange(2 * A_BLOCK)[None, :]
    j = qi + A_BLOCK - ki
    out = []
    for window, dilation in A_PATTERNS:
        nk = window // dilation
        valid = (j >= 0) & (j <= nk)
        out.append(np.where(valid, _bucket_np(np.maximum(j, 0) * dilation), -1))
    return np.stack(out).astype(np.int32)


def _sel_idx():
    a = np.arange(TQ)[:, None]
    c = np.arange(TQ)[None, :]
    n_delta = -(-(_far_dist() + TQ) // TQ)
    out = []
    for delta in range(-1, n_delta + 1):
        dist = delta * TQ + a - c
        out.append(np.where(dist >= 0, _bucket_np(dist), -1))
    return np.stack(out).astype(np.int32)


def _far_dist():
    n = np.arange(0, 4 * REL_MAX_DIST)
    b = _bucket_np(n)
    return int(np.max(n[b < REL_BUCKETS - 1])) + 1


def _win_idx():
    a = np.arange(TQ)[:, None]
    c = np.arange(TQ)[None, :]
    out = []
    for delta in range(-1, WIN // TQ + 1):
        dist = delta * TQ + a - c
        out.append(np.where((dist >= 0) & (dist < WIN), _bucket_np(dist), -1))
    return np.stack(out).astype(np.int32)


CMP_PER_TILE = TQ // CMP_STRIDE
CMP_CLASSES = TQ // CMP_PER_TILE
CMP_SPLIT = 13


def _cmp_window_start(qb):
    return qb // CMP_CLASSES + (1 if qb % CMP_CLASSES >= CMP_SPLIT else 0)


def _cmp_idx():
    a = np.arange(TQ)[:, None]
    c = np.arange(TQ)[None, :]
    out = []
    for r in range(CMP_CLASSES):
        qb = CMP_CLASSES + r
        i0 = _cmp_window_start(qb) * TQ - CMP_PAD
        for half in range(2):
            dist = qb * TQ + a - ((i0 + half * TQ + c) * CMP_STRIDE + CMP_BLOCK - 1)
            out.append(np.where(dist >= 0, _bucket_np(dist), -1))
        out.append(np.full((TQ, TQ), REL_BUCKETS - 1))
        out.append(np.full((TQ, TQ), -1))
    return np.stack(out).astype(np.int32)


def _nsa_tables(rel_bias):
    tsel = _bias_tables(rel_bias, _sel_idx(), shift=True, scale=LOG2E)
    twin = _bias_tables(rel_bias, _win_idx(), shift=False, scale=LOG2E)
    tcmp = _bias_tables(rel_bias, _cmp_idx(), shift=True, scale=LOG2E)
    return tsel, twin, tcmp.reshape(CMP_CLASSES, 4, 8, TQ, TQ)


def _check_cmp_windows(S):
    far = _far_dist()
    for qb in range(S // TQ):
        i0 = _cmp_window_start(qb) * TQ - CMP_PAD
        s0 = qb * TQ
        assert s0 - ((i0 - 1) * CMP_STRIDE + CMP_BLOCK - 1) >= far
        assert s0 + TQ - 1 - ((i0 + 2 * TQ) * CMP_STRIDE + CMP_BLOCK - 1) < 0


def _residue_col(d, r):
    return (r % 4) * 4 + r // 4 if d == 16 else r


def _lane_chunks_store(ref3, val):
    for c in range(ref3.shape[0]):
        ref3[c] = val[:, c * 128:(c + 1) * 128]


def _to_residue_layout(src3_ref, dst, d, col_of=_residue_col):
    nc, rows, _ = src3_ref.shape
    for r in range(d):
        cb = col_of(d, r)
        for c in range(nc):
            col = (cb * nc + c) * 128
            dst(slice(col, col + 128), src3_ref[c, pl.ds(r, rows // d, stride=d), :])


def _ab_proj_kernel(x_ref, xh_ref, w_ref, cw_ref, gb_ref,
                    aq_ref, ak_ref, av_ref, aq4_ref, ak4_ref, av4_ref, aq16_ref, ak16_ref, av16_ref,
                    bq_ref, bk_ref, bv_ref, bo_ref, g_ref,
                    pre_ref, tmp_ref, *, tiles_per_seq):
    i = pl.program_id(0)
    tm = x_ref.shape[0]
    xb = x_ref[...].astype(BF16)
    for c, scale, outs in ((0, A_DIM ** -0.5, (aq_ref, aq4_ref, aq16_ref)),
                           (1, 1.0, (ak_ref, ak4_ref, ak16_ref)),
                           (2, 1.0, (av_ref, av4_ref, av16_ref))):
        val = _dot(xb, w_ref[:, c * A_W:(c + 1) * A_W]) * scale
        _lane_chunks_store(tmp_ref, val)
        outs[0][...] = val.astype(BF16)
        for d, o_ref in ((4, outs[1]), (16, outs[2])):
            def put(cols, piece, o_ref=o_ref):
                o_ref[:, cols] = piece.astype(BF16)
            _to_residue_layout(tmp_ref, put, d)
    bv_ref[...] = _dot(xb, w_ref[:, 2560:3072]).astype(BF16)
    bo_ref[...] = _dot(xb, w_ref[:, 3072:3584])
    g_ref[...] = _dot(xb, w_ref[:, 3584:AB_PAD]) + gb_ref[...]
    halo = _dot(xh_ref[...].astype(BF16), w_ref[:, 1536:2560])
    halo = jnp.where(i % tiles_per_seq == 0, 0.0, halo)
    pre_ref[0:8, :] = halo
    pre_ref[8:8 + tm, :] = _dot(xb, w_ref[:, 1536:2560])
    y = pre_ref[8:8 + tm, :] * cw_ref[B_CONV - 1:B_CONV, :]
    for k in range(B_CONV - 1):
        s = B_CONV - 1 - k
        y = y + pre_ref[8 - s:8 - s + tm, :] * cw_ref[k:k + 1, :]
    y = y / (1.0 + jnp.exp(-y))
    bq_ref[...] = (y[:, :B_W] * (B_DIM ** -0.5)).astype(BF16)
    bk_ref[...] = y[:, B_W:].astype(BF16)


def _ab_proj(x2, w_pad, conv_w, gate_b_pad, S):
    N = x2.shape[0]
    tm = TM
    tps = S // tm
    row = lambda i: (i, 0)
    fix = lambda i: (0, 0)
    lay = lambda d: [jax.ShapeDtypeStruct((N // d, d * A_W), BF16)] * 3
    lay_spec = lambda d: [pl.BlockSpec((tm // d, d * A_W), row)] * 3
    outs = lay(1) + lay(4) + lay(16) + [jax.ShapeDtypeStruct((N, 512), BF16)] * 3 + [
        jax.ShapeDtypeStruct((N, 512), F32), jax.ShapeDtypeStruct((N, 128), F32)]
    o_specs = (lay_spec(1) + lay_spec(4) + lay_spec(16) + [pl.BlockSpec((tm, 512), row)] * 4
               + [pl.BlockSpec((tm, 128), row)])
    return pl.pallas_call(
        functools.partial(_ab_proj_kernel, tiles_per_seq=tps),
        grid=(N // tm,),
        in_specs=[pl.BlockSpec((tm, D_MODEL), row),
                  pl.BlockSpec((8, D_MODEL), lambda i: (jnp.maximum(i * (tm // 8) - 1, 0), 0)),
                  pl.BlockSpec((D_MODEL, AB_PAD), fix),
                  pl.BlockSpec((B_CONV, 2 * B_W), fix),
                  pl.BlockSpec((1, 128), fix)],
        out_specs=o_specs,
        out_shape=outs,
        scratch_shapes=[pltpu.VMEM((tm + 8, 2 * B_W), F32), pltpu.VMEM((A_W // 128, tm, 128), F32)],
        compiler_params=_params(("parallel",)),
        name="ab_proj",
    )(x2, x2, w_pad, conv_w, gate_b_pad)


def _dilated_kernel(*refs, has_prev, is_last):
    if has_prev:
        q_ref, kp_ref, kc_ref, vp_ref, vc_ref, tab_ref, op_ref, lp_ref = refs[:8]
        rest = refs[8:]
    else:
        q_ref, kp_ref, kc_ref, vp_ref, vc_ref, tab_ref = refs[:6]
        rest = refs[6:]
    outs, scratch = (rest, ()) if is_last else (rest[:2], rest[2:])
    o_ref = outs[0]
    n = pl.program_id(2)
    first = jnp.where(n == 0, NEG, 0.0)
    lane = lax.broadcasted_iota(jnp.int32, (A_BLOCK, 128), 1)
    keep_side = [jnp.where(lane < A_DIM, 1.0, 0.0).astype(BF16), jnp.where(lane < A_DIM, 0.0, 1.0).astype(BF16)]
    odd = jnp.bitwise_and(lane, 1) == 1
    ones_side = [jnp.where(odd, 0.0, 1.0).astype(BF16), jnp.where(odd, 1.0, 0.0).astype(BF16)]
    m_tile = jnp.zeros((A_BLOCK, 128), F32)
    l_tile = jnp.ones((A_BLOCK, 128), F32)
    unnorm = []
    for j in range(A_HEADS // 2):
        cs = slice(j * 128, (j + 1) * 128)
        q2 = q_ref[0, :, cs]
        kp, kc, vp, vc = kp_ref[0, :, cs], kc_ref[0, :, cs], vp_ref[0, :, cs], vc_ref[0, :, cs]
        k_st = jnp.concatenate([kp * keep_side[0], kc * keep_side[0],
                                kp * keep_side[1], kc * keep_side[1]], axis=0)
        s = _dot_nt(q2, k_st)
        p_parts = []
        for side in range(2):
            h = 2 * j + side
            c0 = 2 * side * A_BLOCK
            sp = s[:, c0:c0 + A_BLOCK] + tab_ref[0, h, :, 0:A_BLOCK] + first
            sc = s[:, c0 + A_BLOCK:c0 + 2 * A_BLOCK] + tab_ref[0, h, :, A_BLOCK:2 * A_BLOCK]
            m = jnp.max(jnp.maximum(sp, sc), axis=1, keepdims=True)
            p_parts += [jnp.exp(sp - m).astype(BF16), jnp.exp(sc - m).astype(BF16)]
            m_tile = jnp.where(lane == A_DIM + h, m, m_tile)
        v_st = jnp.concatenate(
            [jnp.concatenate([v * keep_side[side], ones_side[side]], axis=1)
             for side in range(2) for v in (vp, vc)], axis=0)
        r = _dot(jnp.concatenate(p_parts, axis=1), v_st)
        unnorm.append(r[:, 0:128])
        pair = (lane == A_DIM + 2 * j) | (lane == A_DIM + 2 * j + 1)
        l_tile = jnp.where(pair, r[:, 128:256], l_tile)
    stat = (lane >= A_DIM) & (lane < A_DIM + A_HEADS)
    lse = m_tile + jnp.log(l_tile)
    if has_prev:
        lp = lp_ref[0]
        mm = jnp.maximum(lp, lse)
        wp = jnp.exp(lp - mm)
        wc = jnp.exp(lse - mm)
        tot = wp + wc
        scale_prev = jnp.where(stat, wp / tot, 0.0)
        scale_cur = jnp.where(stat, wc / (tot * l_tile), 0.0)
        lse = mm + jnp.log(tot)
    else:
        scale_cur = jnp.where(stat, 1.0 / l_tile, 0.0)
    erow = lax.broadcasted_iota(jnp.int32, (128, A_W), 0)
    ecol = lax.broadcasted_iota(jnp.int32, (128, A_W), 1)
    expand = jnp.where(erow - A_DIM == jnp.right_shift(ecol, A_DIM.bit_length() - 1), 1.0, 0.0).astype(BF16)

    def spread(t):
        hi, mid, lo = _split3(t)
        return _dot(hi, expand) + _dot(mid, expand) + _dot(lo, expand)

    o = jnp.concatenate(unnorm, axis=1) * spread(scale_cur)
    if has_prev:
        o = o + op_ref[0] * spread(scale_prev)
    if is_last:
        o_ref[0] = o.astype(o_ref.dtype)
    else:
        o_scr, l_scr = scratch
        _lane_chunks_store(o_scr, o)
        l_scr[0] = jnp.where(stat, lse, 0.0)

        def put_o(cols, piece):
            o_ref[0, :, cols] = piece

        def put_l(cols, piece):
            outs[1][0, :, cols] = piece

        _to_residue_layout(o_scr, put_o, 4)
        _to_residue_layout(l_scr, put_l, 4)


def _dilated_call(q, k, v, tab, prev, pattern_idx, dilation, B, S, is_last):
    d = dilation
    L = S // d
    nb = L // A_BLOCK
    r3 = lambda a: a.reshape(B, L, a.shape[-1])
    cur = lambda b, r, n: (b, n, r)
    prv = lambda b, r, n: (b, jnp.maximum(n - 1, 0), r)
    blk = pl.BlockSpec((1, A_BLOCK, A_W), cur)
    in_specs = [blk, pl.BlockSpec((1, A_BLOCK, A_W), prv), blk,
                pl.BlockSpec((1, A_BLOCK, A_W), prv), blk,
                pl.BlockSpec((1, 8, A_BLOCK, 2 * A_BLOCK), lambda b, r, n: (pattern_idx, 0, 0, 0))]
    args = [r3(q), r3(k), r3(k), r3(v), r3(v), tab]
    has_prev = prev is not None
    if has_prev:
        in_specs += [blk, pl.BlockSpec((1, A_BLOCK, 128), cur)]
        args += [r3(prev[0]), r3(prev[1])]
    scratch = []
    if is_last:
        out_shape = [jax.ShapeDtypeStruct((B, L, d * A_W), BF16)]
        out_specs = [blk]
    else:
        d2, rows = 4 * d, A_BLOCK // 4
        nxt = lambda b, r, n: (b, n, r)
        out_shape = [jax.ShapeDtypeStruct((B, S // d2, d2 * A_W), F32),
                     jax.ShapeDtypeStruct((B, S // d2, d2 * 128), F32)]
        out_specs = [pl.BlockSpec((1, rows, 4 * A_W), nxt), pl.BlockSpec((1, rows, 4 * 128), nxt)]
        scratch = [pltpu.VMEM((A_W // 128, A_BLOCK, 128), F32), pltpu.VMEM((1, A_BLOCK, 128), F32)]
    res = pl.pallas_call(
        functools.partial(_dilated_kernel, has_prev=has_prev, is_last=is_last),
        grid=(B, d, nb),
        in_specs=in_specs, out_specs=out_specs, out_shape=out_shape, scratch_shapes=scratch,
        compiler_params=_params(("parallel", "parallel", "arbitrary")),
        name="dilated_d%d" % d,
    )(*args)
    return [r.reshape(-1, r.shape[-1]) for r in res]


def _dilated_attention(qkv_by_dilation, tab, B, S):
    prev = None
    for p, (window, d) in enumerate(A_PATTERNS):
        assert window // d == A_BLOCK and S % (d * A_BLOCK) == 0
        assert p == 0 or d == 4 * A_PATTERNS[p - 1][1]
        last = p == len(A_PATTERNS) - 1
        q, k, v = qkv_by_dilation[d]
        prev = _dilated_call(q, k, v, tab, prev, p, d, B, S, last)
    return prev[0]


def _split3(x):
    hi = x.astype(BF16)
    r = x - hi.astype(F32)
    mid = r.astype(BF16)
    lo = (r - mid.astype(F32)).astype(BF16)
    return hi, mid, lo


def _mlstm_kernel(q_ref, k_ref, v_ref, g_ref, bo_ref, hg_ref, y_ref, c_ref, m_ref):
    L = B_CHUNK
    c = pl.program_id(1)

    @pl.when(c == 0)
    def _():
        c_ref[...] = jnp.zeros_like(c_ref)
        m_ref[...] = jnp.zeros_like(m_ref)

    lane = lax.broadcasted_iota(jnp.int32, (L, 128), 1)
    row = lax.broadcasted_iota(jnp.int32, (L, L), 0)
    col = lax.broadcasted_iota(jnp.int32, (L, L), 1)
    tri = row >= col
    is_f = (lane >= B_HEADS) & (lane < 2 * B_HEADS)
    tril = jnp.where(tri, 1.0, 0.0).astype(BF16)
    ones = jnp.ones((L, B_DIM), BF16)
    for bi, h in [(bi, h) for bi in range(q_ref.shape[0]) for h in range(B_HEADS)]:
        if h == 0:
            g = g_ref[bi]
            logf = jnp.minimum(g, 0.0) - jnp.log(1.0 + jnp.exp(-jnp.abs(g)))
            gl = jnp.where(is_f, logf, jnp.where(lane < B_HEADS, g, 0.0))
            hi, mid, lo = _split3(gl)
            cum = _dot(tril, hi) + _dot(tril, mid) + _dot(tril, lo)
            cum_t = cum.T
            gl_t = gl.T
        st = bi * B_HEADS + h
        cs = slice(h * B_DIM, (h + 1) * B_DIM)
        q = q_ref[bi, :, cs]
        k = k_ref[bi, :, cs]
        v_ext = jnp.concatenate([v_ref[bi, :, cs], ones], axis=1)
        b_col = cum[:, B_HEADS + h:B_HEADS + h + 1]
        b_row = cum_t[B_HEADS + h:B_HEADS + h + 1, :]
        i_col = gl[:, h:h + 1]
        i_row = gl_t[h:h + 1, :]
        m_prev = m_ref[st, 0:1, 0:1]
        dm = jnp.where(tri, b_col - b_row + i_row, NEG)
        inter = b_col + m_prev
        m_t = jnp.maximum(inter, jnp.max(dm, axis=1, keepdims=True))
        p = jnp.exp(dm - m_t)
        sqk = _dot_nt(q, k) * p
        sc = jnp.exp(inter - m_t)
        lhs = jnp.concatenate([(sc * q.astype(F32)).astype(BF16), sqk.astype(BF16)], axis=1)
        c_ext = c_ref[st]
        rhs = jnp.concatenate([c_ext.astype(BF16), v_ext], axis=0)
        res = _dot(lhs, rhs)
        num = res[:, :B_DIM]
        den = res[:, B_DIM:]
        hh = num / jnp.maximum(jnp.abs(den), jnp.exp(-m_t))
        b_last = b_col[L - 1:L, :]
        gk = b_last - b_col + i_col
        m_new = jnp.maximum(b_last + m_prev, jnp.max(gk, axis=0, keepdims=True))
        wk = jnp.exp(gk - m_new)
        decay = jnp.exp(b_last + m_prev - m_new)
        kw_t = (wk * k.astype(F32)).T.astype(BF16)
        c_ref[st] = decay * c_ext + _dot(kw_t, v_ext)
        m_ref[st] = jnp.broadcast_to(m_new, (8, 128))
        mu = jnp.mean(hh, axis=1, keepdims=True)
        xc = hh - mu
        var = jnp.mean(xc * xc, axis=1, keepdims=True)
        hn = xc * lax.rsqrt(var + LN_EPS) * hg_ref[:, cs]
        bo = bo_ref[bi, :, cs]
        y_ref[bi, :, cs] = (hn / (1.0 + jnp.exp(-bo))).astype(BF16)


def _mlstm(bq, bk, bv, gates, bo, head_g, B, S):
    nc = S // B_CHUNK
    bb = 1
    r3 = lambda a: a.reshape(B, S, a.shape[-1])
    blk = pl.BlockSpec((bb, B_CHUNK, B_W), lambda b, c: (b, c, 0))
    y = pl.pallas_call(
        _mlstm_kernel,
        grid=(B // bb, nc),
        in_specs=[blk, blk, blk,
                  pl.BlockSpec((bb, B_CHUNK, 128), lambda b, c: (b, c, 0)),
                  blk,
                  pl.BlockSpec((1, B_W), lambda b, c: (0, 0))],
        out_specs=blk,
        out_shape=jax.ShapeDtypeStruct((B, S, B_W), BF16),
        scratch_shapes=[pltpu.VMEM((bb * B_HEADS, B_DIM, 2 * B_DIM), F32),
                        pltpu.VMEM((bb * B_HEADS, 8, 128), F32)],
        compiler_params=_params(("parallel", "arbitrary")),
        name="mlstm",
    )(r3(bq), r3(bk), r3(bv), r3(gates), r3(bo), head_g.reshape(1, B_W).astype(F32))
    return y.reshape(B * S, B_W)


def _layer_norm(z, g, b):
    mu = jnp.mean(z, axis=1, keepdims=True)
    zc = z - mu
    var = jnp.mean(zc * zc, axis=1, keepdims=True)
    return zc * lax.rsqrt(var + LN_EPS) * g + b


def _route(logits, cnt_ref):
    tm = logits.shape[0]
    lt = logits.T
    col = lambda c: lt[c:c + 1, :]
    gl = [col(c) for c in range(N_GROUPS)]
    gmax = functools.reduce(jnp.maximum, gl)
    gsum = sum(jnp.exp(x - gmax) for x in gl)
    g_idx = jnp.full(gmax.shape, N_GROUPS - 1, jnp.int32)
    for c in range(N_GROUPS - 2, -1, -1):
        g_idx = jnp.where(gl[c] == gmax, c, g_idx)
    g_w = 1.0 / gsum
    el = []
    for k in range(EPG):
        x = col(N_GROUPS + (N_GROUPS - 1) * EPG + k)
        for g in range(N_GROUPS - 2, -1, -1):
            x = jnp.where(g_idx == g, col(N_GROUPS + g * EPG + k), x)
        el.append(x)
    v1 = functools.reduce(jnp.maximum, el)
    i1 = jnp.full(v1.shape, EPG - 1, jnp.int32)
    for k in range(EPG - 2, -1, -1):
        i1 = jnp.where(el[k] == v1, k, i1)
    el2 = [jnp.where(i1 == k, -jnp.inf, el[k]) for k in range(EPG)]
    v2 = functools.reduce(jnp.maximum, el2)
    i2 = jnp.full(v2.shape, EPG - 1, jnp.int32)
    for k in range(EPG - 2, -1, -1):
        i2 = jnp.where((el2[k] == v2) & (i1 != k), k, i2)
    t = jnp.exp(v2 - v1)
    w1 = g_w / (1.0 + t)
    w2 = w1 * t
    a = jnp.minimum(i1, i2)
    b = jnp.maximum(i1, i2)
    pair = jnp.where(a == 0, b - 1, jnp.where(a == 1, b + 1, 5))
    bucket = (g_idx * 6 + pair).astype(F32)
    w_lo = jnp.where(i1 < i2, w1, w2)
    w_hi = jnp.where(i1 < i2, w2, w1)
    sub = lax.broadcasted_iota(jnp.int32, (128, tm), 0)
    onehot_t = jnp.where(sub.astype(F32) == bucket, 1.0, 0.0)
    srow = lax.broadcasted_iota(jnp.int32, (tm, tm), 0)
    scol = lax.broadcasted_iota(jnp.int32, (tm, tm), 1)
    before = jnp.where(srow < scol, 1.0, 0.0).astype(BF16)
    oh = onehot_t.astype(BF16)
    carry = cnt_ref[...]
    prior = _dot(oh, before) + jnp.concatenate([carry] * (tm // 128), axis=1)
    rank = jnp.sum(onehot_t * prior, axis=0, keepdims=True)
    cnt_ref[...] = carry + _dot(oh, jnp.ones((tm, 128), BF16))
    out_t = jnp.where(sub == 0, bucket, jnp.where(sub == 1, w_lo, jnp.where(sub == 2, w_hi,
                      jnp.where(sub == 3, rank, 0.0))))
    return out_t.T


def _out_ln_route_kernel(*refs, n_in, layouts):
    y_refs = refs[:n_in]
    w_refs = refs[n_in:2 * n_in]
    x_ref, g_ref, b_ref, wrh_ref, wrl_ref, br_ref, h_ref, c_ref, cnt_ref = refs[2 * n_in:2 * n_in + 9]
    pos_refs = list(refs[2 * n_in + 9:])
    tm = x_ref.shape[0]

    @pl.when(pl.program_id(0) == 0)
    def _():
        cnt_ref[...] = jnp.zeros_like(cnt_ref)

    y = None
    for i in range(n_in):
        d = layouts[i]
        if d == 1:
            lhs = y_refs[i][...]
        else:
            s_ref = pos_refs.pop(0)
            nc = s_ref.shape[0]
            for r in range(d):
                cb = _residue_col(d, r)
                for c in range(nc):
                    col = (cb * nc + c) * 128
                    s_ref[c, pl.ds(r, tm // d, stride=d), :] = y_refs[i][:, col:col + 128].astype(F32)
            lhs = jnp.concatenate([s_ref[c] for c in range(nc)], axis=1).astype(BF16)
        t = _dot(lhs, w_refs[i][...])
        y = t if y is None else y + t
    hn = _layer_norm(ALPHA * x_ref[...] + y, g_ref[...], b_ref[...])
    h_ref[:, 0:D_MODEL] = hn
    hi = hn.astype(BF16)
    lo = (hn - hi.astype(F32)).astype(BF16)
    logits = (_dot_nt(hi, wrh_ref[...]) + _dot_nt(lo, wrh_ref[...]) + _dot_nt(hi, wrl_ref[...])
              + br_ref[...])
    h_ref[:, D_MODEL:D_MODEL + 128] = _route(logits, cnt_ref)
    c_ref[...] = cnt_ref[...]


def _out_ln_route(ys, layouts, ws, x2, ln_g, ln_b, wr, br, row_start, N):
    tm = TM
    off = row_start // tm
    row = lambda i: (i + off, 0)
    fix = lambda i: (0, 0)
    wr_hi = wr.astype(BF16)
    wr_lo = (wr - wr_hi.astype(F32)).astype(BF16)
    in_specs = ([pl.BlockSpec((tm // d, y.shape[1]), row) for y, d in zip(ys, layouts)]
                + [pl.BlockSpec(w.shape, fix) for w in ws]
                + [pl.BlockSpec((tm, D_MODEL), row),
                   pl.BlockSpec((1, D_MODEL), fix), pl.BlockSpec((1, D_MODEL), fix),
                   pl.BlockSpec((128, D_MODEL), fix), pl.BlockSpec((128, D_MODEL), fix),
                   pl.BlockSpec((1, 128), fix)])
    return pl.pallas_call(
        functools.partial(_out_ln_route_kernel, n_in=len(ys), layouts=tuple(layouts)),
        grid=(N // tm,),
        in_specs=in_specs,
        out_specs=[pl.BlockSpec((tm, D_MODEL + 128), lambda i: (i, 0)), pl.BlockSpec((128, 128), fix)],
        out_shape=[jax.ShapeDtypeStruct((N, D_MODEL + 128), F32), jax.ShapeDtypeStruct((128, 128), F32)],
        scratch_shapes=[pltpu.VMEM((128, 128), F32)] + [
            pltpu.VMEM((y.shape[1] // d // 128, tm, 128), F32) for y, d in zip(ys, layouts) if d != 1],
        compiler_params=_params(("arbitrary",)),
        name="out_ln_route",
    )(*ys, *ws, x2, ln_g.reshape(1, -1), ln_b.reshape(1, -1), wr_hi, wr_lo, br)


def _router_weights(wr_g, br_g, wr_e, br_e):
    we = wr_e.transpose(0, 2, 1).reshape(N_GROUPS * EPG, D_MODEL)
    w = jnp.concatenate([wr_g.T, we], axis=0)
    w = jnp.pad(w, ((0, 128 - w.shape[0]), (0, 0)))
    b = jnp.concatenate([br_g, br_e.reshape(-1)])
    b = jnp.pad(b, (0, 128 - b.shape[0])).reshape(1, 128)
    return w.astype(F32), b.astype(F32)


_PAIRS = ((0, 1), (0, 2), (0, 3), (1, 2), (1, 3), (2, 3))


def _moe_kernel(elo_ref, ehi_ref, chg_ref, nt_ref,
                x_ref, wgl_ref, wul_ref, wdl_ref, wgh_ref, wuh_ref, wdh_ref,
                g_ref, b_ref, o_ref, wg_s, wu_s, wd_s):
    t = pl.program_id(0)

    @pl.when(chg_ref[t] == 1)
    def _():
        wg_s[0] = wgl_ref[0, 0].astype(BF16)
        wu_s[0] = wul_ref[0, 0].astype(BF16)
        wd_s[0] = wdl_ref[0, 0].astype(BF16)
        wg_s[1] = wgh_ref[0, 0].astype(BF16)
        wu_s[1] = wuh_ref[0, 0].astype(BF16)
        wd_s[1] = wdh_ref[0, 0].astype(BF16)

    @pl.when(t < nt_ref[0])
    def _():
        x = x_ref[:, 0:D_MODEL]
        xb = x.astype(BF16)
        r = x_ref[:, D_MODEL:D_MODEL + 128]
        acc = None
        for e in range(2):
            a = _dot(xb, wg_s[e])
            u = _dot(xb, wu_s[e])
            hcur = (a / (1.0 + jnp.exp(-a))) * u * r[:, 1 + e:2 + e]
            y = _dot(hcur.astype(BF16), wd_s[e])
            acc = y if acc is None else acc + y
        o_ref[...] = _layer_norm(ALPHA * x + acc, g_ref[...], b_ref[...])

    @pl.when(t >= nt_ref[0])
    def _():
        o_ref[...] = jnp.zeros_like(o_ref)


def _moe(hx, cnt, layer, w_gate, w_up, w_down, ln_g, ln_b):
    N = hx.shape[0]
    tm = TM_MOE
    n_tiles = N // tm + N_BUCKETS
    n_pad = n_tiles * tm
    bucket = hx[:, D_MODEL].astype(jnp.int32)
    rank = hx[:, D_MODEL + 3].astype(jnp.int32)
    counts = cnt[:N_BUCKETS, 0].astype(jnp.int32)
    padded = ((counts + tm - 1) // tm) * tm
    ends = jnp.cumsum(padded)
    offs = ends - padded
    b2 = bucket.reshape(-1, 128)
    off2 = functools.reduce(lambda acc, b: jnp.where(b2 == b, offs[b], acc), range(N_BUCKETS),
                            jnp.zeros_like(b2))
    dest = off2.reshape(-1) + rank
    src = (jnp.arange(n_pad, dtype=jnp.int32) % N).at[dest].set(
        jnp.arange(N, dtype=jnp.int32), mode="promise_in_bounds", unique_indices=True)
    tile_start = jnp.arange(n_tiles, dtype=jnp.int32) * tm
    n_used = (ends[-1] // tm).astype(jnp.int32)
    tb = jnp.sum((tile_start[:, None] >= ends[None, :]).astype(jnp.int32), axis=1)
    tb_last = jnp.take(tb, jnp.maximum(n_used - 1, 0))
    tb = jnp.where(tile_start < ends[-1], tb, tb_last)
    pairs = jnp.asarray(_PAIRS, jnp.int32)
    elo = (tb // 6) * EPG + pairs[tb % 6, 0]
    ehi = (tb // 6) * EPG + pairs[tb % 6, 1]
    chg = jnp.concatenate([jnp.ones((1,), jnp.int32), (tb[1:] != tb[:-1]).astype(jnp.int32)])
    xs = hx.at[src].get(mode="promise_in_bounds")

    row = lambda t, *_: (t, 0)
    fix = lambda t, *_: (0, 0)
    wlo = lambda t, elo, ehi, chg, nt: (layer, elo[t], 0, 0)
    whi = lambda t, elo, ehi, chg, nt: (layer, ehi[t], 0, 0)
    up_spec = lambda im: pl.BlockSpec((1, 1, D_MODEL, E_HID), im)
    dn_spec = lambda im: pl.BlockSpec((1, 1, E_HID, D_MODEL), im)
    grid_spec = pltpu.PrefetchScalarGridSpec(
        num_scalar_prefetch=4,
        grid=(n_tiles,),
        in_specs=[pl.BlockSpec((tm, D_MODEL + 128), row),
                  up_spec(wlo), up_spec(wlo), dn_spec(wlo),
                  up_spec(whi), up_spec(whi), dn_spec(whi),
                  pl.BlockSpec((1, D_MODEL), fix), pl.BlockSpec((1, D_MODEL), fix)],
        out_specs=pl.BlockSpec((tm, D_MODEL), row),
        scratch_shapes=[pltpu.VMEM((2, D_MODEL, E_HID), BF16),
                        pltpu.VMEM((2, D_MODEL, E_HID), BF16),
                        pltpu.VMEM((2, E_HID, D_MODEL), BF16)])
    out_sorted = pl.pallas_call(
        _moe_kernel,
        grid_spec=grid_spec,
        out_shape=jax.ShapeDtypeStruct((n_pad, D_MODEL), F32),
        compiler_params=_params(("arbitrary",)),
        name="moe",
    )(elo, ehi, chg, n_used.reshape(1), xs, w_gate, w_up, w_down, w_gate, w_up, w_down,
      ln_g.reshape(1, -1), ln_b.reshape(1, -1))
    return out_sorted.at[dest].get(mode="promise_in_bounds", unique_indices=True)


def _c_proj_kernel(x_ref, w_ref, gb_ref, q_ref, kc_ref, kv_ref, g_ref, tmp_ref):
    xb = x_ref[...].astype(BF16)
    q_ref[...] = (_dot(xb, w_ref[:, 0:C_W]) * (C_DIM ** -0.5 * LOG2E)).astype(BF16)
    for i in range(4):
        tmp_ref[0] = _dot(xb, w_ref[:, C_W + i * 128:C_W + (i + 1) * 128])

        def put(cols, piece, i=i):
            kc_ref[i, :, cols] = piece.astype(BF16)

        _to_residue_layout(tmp_ref, put, CMP_STRIDE, col_of=lambda d, r: r)
    kv_ref[...] = _dot(xb, w_ref[:, C_W + 512:C_W + 1536]).astype(BF16)
    z = _dot(xb, w_ref[:, C_W + 1536:C_W + 1792]) + gb_ref[...]
    g_ref[...] = 1.0 / (1.0 + jnp.exp(-z))


def _c_proj(x2, w_pad, gb_pad):
    N = x2.shape[0]
    tm = TM
    row = lambda i: (i, 0)
    fix = lambda i: (0, 0)
    wcols = w_pad.shape[1]
    return pl.pallas_call(
        _c_proj_kernel,
        grid=(N // tm,),
        in_specs=[pl.BlockSpec((tm, D_MODEL), row), pl.BlockSpec((D_MODEL, wcols), fix),
                  pl.BlockSpec((1, 256), fix)],
        out_specs=[pl.BlockSpec((tm, C_W), row),
                   pl.BlockSpec((4, tm // CMP_STRIDE, CMP_STRIDE * C_DIM), lambda i: (0, i, 0)),
                   pl.BlockSpec((tm, 1024), row), pl.BlockSpec((tm, 256), row)],
        out_shape=[jax.ShapeDtypeStruct((N, C_W), BF16),
                   jax.ShapeDtypeStruct((4, N // CMP_STRIDE, CMP_STRIDE * C_DIM), BF16),
                   jax.ShapeDtypeStruct((N, 1024), BF16), jax.ShapeDtypeStruct((N, 256), F32)],
        scratch_shapes=[pltpu.VMEM((1, tm, 128), F32)],
        compiler_params=_params(("parallel",)),
        name="c_proj",
    )(x2, w_pad, gb_pad)


def _c_weights(w_in, gate_b):
    gcols = []
    gb = []
    for g in range(C_GROUPS):
        idx = [C_PROJ - 3 * C_HEADS + br * C_HEADS + g * C_HPG + j for br in range(3) for j in range(C_HPG)]
        gcols.append(jnp.pad(w_in[:, np.asarray(idx)], ((0, 0), (0, 128 - len(idx)))))
        gb.append(jnp.pad(gate_b[np.asarray(idx) - (C_PROJ - 3 * C_HEADS)], (0, 128 - len(idx))))
    w = jnp.concatenate([w_in[:, :C_PROJ - 3 * C_HEADS]] + gcols, axis=1).astype(BF16)
    return w, jnp.concatenate(gb).reshape(1, 256).astype(F32)


def _compress_kernel(seg_ref, w1_ref, pos_ref, w1f_ref, w2_ref, o_ref):
    n_seg = seg_ref.shape[1]
    ul = _dot(seg_ref[0], w1_ref[0])
    u = ul[:, :CMP_HIDDEN]
    lnext = pltpu.roll(ul[:, CMP_HIDDEN:], n_seg - 1, 0)
    cpos = _dot(pos_ref[0], w1f_ref[0])[0:1, :]
    pre = u + lnext + cpos
    act = 0.5 * pre * (1.0 + jnp.tanh(math.sqrt(2.0 / math.pi) * (pre + 0.044715 * pre * pre * pre)))
    o_ref[0, 0, 0:CMP_PAD, :] = jnp.zeros((CMP_PAD, C_DIM), BF16)
    o_ref[0, 0, CMP_PAD:CMP_PAD + n_seg, :] = _dot(act.astype(BF16), w2_ref[0]).astype(BF16)


def _compress(kc, cmp_pos, cmp_w1, cmp_w2, B, S):
    n_seg = S // CMP_STRIDE
    half = CMP_STRIDE * C_DIM
    seg = kc.reshape(4 * B, n_seg, half)
    w1 = cmp_w1.astype(BF16)
    w1_ul = jnp.concatenate([w1[:, :half], w1[:, half:]], axis=2)
    pos = jnp.broadcast_to(cmp_pos.reshape(2, 1, CMP_BLOCK * C_DIM), (2, 8, CMP_BLOCK * C_DIM)).astype(BF16)
    out = pl.pallas_call(
        _compress_kernel,
        grid=(4, B),
        in_specs=[pl.BlockSpec((1, n_seg, half), lambda i, b: (i * B + b, 0, 0)),
                  pl.BlockSpec((1, half, 2 * CMP_HIDDEN), lambda i, b: (i // 2, 0, 0)),
                  pl.BlockSpec((1, 8, CMP_BLOCK * C_DIM), lambda i, b: (i // 2, 0, 0)),
                  pl.BlockSpec((1, CMP_BLOCK * C_DIM, CMP_HIDDEN), lambda i, b: (i // 2, 0, 0)),
                  pl.BlockSpec((1, CMP_HIDDEN, C_DIM), lambda i, b: (i // 2, 0, 0))],
        out_specs=pl.BlockSpec((1, 1, CMP_PAD + n_seg, C_DIM), lambda i, b: (i, b, 0, 0)),
        out_shape=jax.ShapeDtypeStruct((4, B, CMP_PAD + n_seg, C_DIM), BF16),
        compiler_params=_params(("parallel", "parallel")),
        name="compress",
    )(seg, w1_ul, pos, w1, cmp_w2.astype(BF16))
    return out


def _overlap_np(n_cmp_pad, n_slc):
    i = np.arange(n_cmp_pad)[:, None] - CMP_PAD
    m = np.arange(n_slc)[None, :]
    start = i * CMP_STRIDE
    ov = (start < (m + 1) * SLC_BLOCK) & (start + CMP_BLOCK - 1 >= m * SLC_BLOCK) & (i >= 0)
    return ov.astype(np.float32)


def _flash_update(s_heads, vext, m_ref, acc_ref, p_ref):
    W = s_heads[0].shape[1]
    nw = W // 128
    alphas = []
    for h, s in enumerate(s_heads):
        rs = slice(h * TQ, (h + 1) * TQ)
        m_prev = m_ref[rs, :]
        smax = functools.reduce(jnp.maximum, [s[:, i * 128:(i + 1) * 128] for i in range(nw)])
        m_new = jnp.maximum(m_prev, jnp.max(smax, axis=1, keepdims=True))
        alpha = jnp.exp2(m_prev - m_new)
        p = jnp.exp2(s - (m_new if nw == 1 else jnp.concatenate([m_new] * nw, axis=1)))
        m_ref[rs, :] = m_new
        p_ref[rs, 0:W] = p.astype(BF16)
        alphas.append(alpha)
    a = jnp.concatenate(alphas, axis=0)
    acc_ref[...] = jnp.concatenate([a, a], axis=1) * acc_ref[...] + _dot(p_ref[:, 0:W], vext)


def _nsa_kernel(q_ref, ks_ref, vs_ref, kw_ref, vw_ref, kc_ref, vc_ref, ov_ref, mk_ref,
                tsel_ref, twin_ref, tcmp_ref, g_ref, o_ref,
                m_ref, acc_ref, p_ref, pw_ref, sa_ref, sb_ref, oc_ref, ow_ref, *, n_sel_tab, top_n):
    qb = pl.program_id(2)
    H = C_HPG
    NB = ov_ref.shape[1]
    q_all = jnp.concatenate([q_ref[:, h * C_DIM:(h + 1) * C_DIM] for h in range(H)], axis=0)
    heads = lambda x: [x[h * TQ:(h + 1) * TQ] for h in range(H)]

    def reset():
        m_ref[...] = jnp.full(m_ref.shape, NEG, F32)
        acc_ref[...] = jnp.zeros(acc_ref.shape, F32)

    def softmax_rows(s):
        nw = s.shape[1] // 128
        smax = functools.reduce(jnp.maximum, [s[:, i * 128:(i + 1) * 128] for i in range(nw)])
        m = jnp.broadcast_to(jnp.max(smax, axis=1, keepdims=True), (TQ, 128))
        return jnp.exp2(s - jnp.concatenate([m] * nw, axis=1)), m

    t0 = qb // CMP_CLASSES + jnp.where(qb % CMP_CLASSES >= CMP_SPLIT, 1, 0)
    n_ct = kc_ref.shape[2] // TQ
    wc = n_ct * TQ
    tile_kind = [3] + [jnp.where(t == t0, 0, jnp.where(t == t0 + 1, 1, jnp.where(t < t0, 2, 3)))
                       for t in range(1, n_ct)]
    s_c = _dot_nt(q_all, kc_ref[0, 0])
    vext_c = jnp.concatenate([vc_ref[0, 0], ov_ref[...]], axis=1)
    inv_c = []
    for h in range(H):
        bias = jnp.concatenate([tcmp_ref[0, tile_kind[t], h] for t in range(n_ct)], axis=1)
        p, m = softmax_rows(s_c[h * TQ:(h + 1) * TQ] + bias)
        l = jnp.sum(p, axis=1, keepdims=True)
        inv_c.append(jnp.where(m > 0.5 * NEG, 1.0 / l, 0.0))
        p_ref[h * TQ:(h + 1) * TQ, 0:wc] = p.astype(BF16)
    res_c = _dot(p_ref[:, 0:wc], vext_c)
    imp = None
    for h in range(H):
        r = res_c[h * TQ:(h + 1) * TQ] * jnp.concatenate([inv_c[h], inv_c[h]], axis=1)
        oc_ref[h * TQ:(h + 1) * TQ, :] = r[:, 0:C_DIM]
        imp = r[:, C_DIM:] if imp is None else imp + r[:, C_DIM:]

    n_wt = WIN // TQ + 1
    ww = n_wt * TQ
    st = jnp.maximum(qb - (n_wt - 1), 0)
    r0w = pl.multiple_of(st * TQ, TQ)
    s_w = _dot_nt(q_all, kw_ref[pl.ds(r0w, ww), :])
    vext_w = jnp.concatenate([vw_ref[pl.ds(r0w, ww), :], jnp.ones((ww, C_DIM), BF16)], axis=1)
    widx = [jnp.maximum(qb - (st + c) + 1, 0) for c in range(n_wt)]
    for h in range(H):
        bias = jnp.concatenate([twin_ref[i, h] for i in widx], axis=1)
        p, _ = softmax_rows(s_w[h * TQ:(h + 1) * TQ] + bias)
        pw_ref[h * TQ:(h + 1) * TQ, :] = p.astype(BF16)
    res_w = _dot(pw_ref[...], vext_w)
    ow_ref[...] = res_w[:, 0:C_DIM] / res_w[:, C_DIM:]

    shift = SLC_BLOCK.bit_length() - 1
    qpos = qb * TQ + lax.broadcasted_iota(jnp.int32, (TQ, NB), 0)
    mblk = lax.broadcasted_iota(jnp.int32, (TQ, NB), 1)
    qblk = jnp.right_shift(qpos, shift)
    forced = (mblk == 0) | (mblk == qblk) | (mblk == qblk - 1)
    score = jnp.where(forced, 3e38, jnp.where(jnp.left_shift(mblk, shift) <= qpos, imp, NEG))
    score_t = score.T
    blk_t = lax.broadcasted_iota(jnp.int32, (NB, TQ), 0).astype(F32)
    sel_t = jnp.zeros((NB, TQ), F32)
    for _ in range(top_n):
        mx = jnp.max(score_t, axis=0, keepdims=True)
        idx = jnp.min(jnp.where(score_t == mx, blk_t, float(NB)), axis=0, keepdims=True)
        pick = blk_t == idx
        sel_t = jnp.where(pick, 1.0, sel_t)
        score_t = jnp.where(pick, -3e38, score_t)
    unsel = (1.0 - sel_t.T).astype(BF16)

    reset()
    NT = 4
    TK = NT * TQ
    q_aug = jnp.concatenate([q_all, jnp.concatenate([unsel] * H, axis=0)], axis=1)
    ones_k = jnp.ones((TK, C_DIM), BF16)
    n_steps = qb // NT + 1

    def sel_logits(kq, s_ref, near):
        kc = jnp.minimum(kq, n_steps - 1)
        r0 = pl.multiple_of(kc * TK, TK)
        k_aug = jnp.concatenate([ks_ref[pl.ds(r0, TK), :], mk_ref[pl.ds(r0, TK), :]], axis=1)
        s = _dot_nt(q_aug, k_aug)
        if not near:
            s_ref[...] = s
            return
        idx = [jnp.where(kq < n_steps, jnp.clip(qb - (NT * kc + c) + 1, 0, n_sel_tab - 1), 0)
               for c in range(NT)]
        for h in range(H):
            bias = jnp.concatenate([tsel_ref[i, h] for i in idx], axis=1)
            s_ref[h * TQ:(h + 1) * TQ, :] = s[h * TQ:(h + 1) * TQ] + bias

    def sel_consume(kq, s_ref):
        r0 = pl.multiple_of(jnp.minimum(kq, n_steps - 1) * TK, TK)
        vext = jnp.concatenate([vs_ref[pl.ds(r0, TK), :], ones_k], axis=1)
        _flash_update([s_ref[h * TQ:(h + 1) * TQ, :] for h in range(H)], vext, m_ref, acc_ref, p_ref)

    def sel_run(k_lo, n2, near):
        @pl.when(n2 > 0)
        def _():
            sel_logits(k_lo, sa_ref, near)

        def body(j, carry):
            k = k_lo + 2 * j
            sel_logits(k + 1, sb_ref, near)
            sel_consume(k, sa_ref)
            sel_logits(k + 2, sa_ref, near)
            sel_consume(k + 1, sb_ref)
            return carry

        lax.fori_loop(0, n2, body, 0)

    n_far = jnp.maximum((qb + 1 - (n_sel_tab - 2)) // NT, 0)
    far2 = n_far // 2
    sel_run(0, far2, near=False)
    sel_run(2 * far2, (n_steps - 2 * far2 + 1) // 2, near=True)
    g = g_ref[...]
    for h in range(H):
        rs = slice(h * TQ, (h + 1) * TQ)
        out_s = acc_ref[rs, 0:C_DIM] / acc_ref[rs, C_DIM:2 * C_DIM]
        o = (g[:, h:h + 1] * oc_ref[rs, :] + g[:, H + h:H + h + 1] * out_s
             + g[:, 2 * H + h:2 * H + h + 1] * ow_ref[rs, :])
        o_ref[:, h * C_DIM:(h + 1) * C_DIM] = o.astype(BF16)


def _nsa_attention(q, kv, kvc, gates, tsel, twin, tcmp, B, S):
    N = B * S
    QT = S // TQ
    n_slc = S // SLC_BLOCK
    n_cmp_pad = kvc.shape[2]
    NB = 128
    assert n_slc <= NB and n_cmp_pad % TQ == 0 and QT % 4 == 0
    ov = jnp.asarray(_overlap_np(n_cmp_pad, NB), BF16)
    mk = jnp.asarray(np.where(np.arange(S)[:, None] // SLC_BLOCK == np.arange(NB)[None, :], NEG, 0.0), BF16)
    n_sel_delta = tsel.shape[0]
    kvspec = lambda c: pl.BlockSpec((S, C_DIM), lambda b, g, t: (b, c + g))
    cspec = lambda kvi: pl.BlockSpec((1, 1, n_cmp_pad, C_DIM), lambda b, g, t: (kvi * 2 + g, b, 0, 0))
    return pl.pallas_call(
        functools.partial(_nsa_kernel, n_sel_tab=n_sel_delta, top_n=min(SLC_TOP_N, n_slc)),
        grid=(B, C_GROUPS, QT),
        in_specs=[pl.BlockSpec((TQ, C_HPG * C_DIM), lambda b, g, t: (b * QT + t, g)),
                  kvspec(0), kvspec(2), kvspec(4), kvspec(6),
                  cspec(0), cspec(1),
                  pl.BlockSpec((n_cmp_pad, NB), lambda b, g, t: (0, 0)),
                  pl.BlockSpec((S, NB), lambda b, g, t: (0, 0)),
                  pl.BlockSpec((n_sel_delta, C_HPG, TQ, TQ), lambda b, g, t: (0, g, 0, 0)),
                  pl.BlockSpec((twin.shape[0], C_HPG, TQ, TQ), lambda b, g, t: (0, g, 0, 0)),
                  pl.BlockSpec((1, 4, C_HPG, TQ, TQ), lambda b, g, t: (t % CMP_CLASSES, 0, g, 0, 0)),
                  pl.BlockSpec((TQ, 128), lambda b, g, t: (b * QT + t, g))],
        out_specs=pl.BlockSpec((TQ, C_HPG * C_DIM), lambda b, g, t: (b * QT + t, g)),
        out_shape=jax.ShapeDtypeStruct((N, C_W), BF16),
        scratch_shapes=[pltpu.VMEM((C_HPG * TQ, 128), F32),
                        pltpu.VMEM((C_HPG * TQ, 2 * C_DIM), F32),
                        pltpu.VMEM((C_HPG * TQ, max(n_cmp_pad, 4 * TQ)), BF16),
                        pltpu.VMEM((C_HPG * TQ, WIN + TQ), BF16),
                        pltpu.VMEM((C_HPG * TQ, 4 * TQ), F32), pltpu.VMEM((C_HPG * TQ, 4 * TQ), F32),
                        pltpu.VMEM((C_HPG * TQ, C_DIM), F32), pltpu.VMEM((C_HPG * TQ, C_DIM), F32)],
        compiler_params=_params(("parallel", "parallel", "arbitrary")),
        name="nsa",
    )(q, kv, kv, kv, kv, kvc, kvc, ov, mk, tsel, twin, tcmp, gates)


def _layer_ab(h2, B, S, w_in, gate_b, conv_w, head_g, w_out, dil_tab):
    w_pad = jnp.pad(w_in, ((0, 0), (0, AB_PAD - AB_PROJ))).astype(BF16)
    gb_pad = jnp.pad(gate_b, (0, 128 - gate_b.shape[0])).reshape(1, 128).astype(F32)
    (aq, ak, av, aq4, ak4, av4, aq16, ak16, av16,
     bq, bk, bv, bo, gates) = _ab_proj(h2, w_pad, conv_w.astype(F32), gb_pad, S)
    ya = _dilated_attention({1: (aq, ak, av), 4: (aq4, ak4, av4), 16: (aq16, ak16, av16)}, dil_tab, B, S)
    yb = _mlstm(bq, bk, bv, gates, bo, head_g, B, S)
    wo = w_out.astype(BF16)
    return [ya, yb], [A_PATTERNS[-1][1], 1], [wo[:A_W], wo[A_W:]]


def _layer_c(h2, B, S, w_in, gate_b, cmp_pos, cmp_w1, cmp_w2, w_out, tsel, twin, tcmp):
    w_pad, gb_pad = _c_weights(w_in, gate_b)
    q, kc, kv, gates = _c_proj(h2, w_pad, gb_pad)
    kvc = _compress(kc, cmp_pos, cmp_w1, cmp_w2, B, S)
    out = _nsa_attention(q, kv, kvc, gates, tsel, twin, tcmp, B, S)
    return [out], [1], [w_out.astype(BF16)]


def kernel(x, rel_bias, ln_g, ln_b, ab_w_in, ab_gate_b, ab_conv, ab_head_norm, ab_w_out,
           c_w_in, c_gate_b, c_cmp_pos, c_cmp_w1, c_cmp_w2, c_w_out,
           moe_wr_g, moe_br_g, moe_wr_e, moe_br_e, moe_w_gate, moe_w_up, moe_w_down):
    B, S, D = x.shape
    assert D == D_MODEL and S % (TM) == 0 and S % (16 * A_BLOCK) == 0
    _check_cmp_windows(S)
    dil_tab = _bias_tables(rel_bias, _dilated_idx(), shift=False)
    tsel, twin, tcmp = _nsa_tables(rel_bias)
    h = x.reshape(B * S, D)
    for layer in range(DEPTH):
        j = layer // 2
        if layer % 2 == 0:
            ys, lays, ws = _layer_ab(h, B, S, ab_w_in[j], ab_gate_b[j], ab_conv[j], ab_head_norm[j],
                                     ab_w_out[j], dil_tab)
        else:
            ys, lays, ws = _layer_c(h, B, S, c_w_in[j], c_gate_b[j], c_cmp_pos[j], c_cmp_w1[j],
                                    c_cmp_w2[j], c_w_out[j], tsel, twin, tcmp)
        wr, br = _router_weights(moe_wr_g[layer], moe_br_g[layer], moe_wr_e[layer], moe_br_e[layer])
        parts = []
        rows = (B * S) // MOE_CHUNKS
        for c in range(MOE_CHUNKS):
            hx, cnt = _out_ln_route(ys, lays, ws, h, ln_g[layer, 0], ln_b[layer, 0], wr, br, c * rows, rows)
            parts.append(_moe(hx, cnt, layer, moe_w_gate, moe_w_up, moe_w_down,
                              ln_g[layer, 1], ln_b[layer, 1]))
        h = jnp.concatenate(parts, axis=0)
    return h.reshape(B, S, D)
```

```python
import functools
import math

import numpy as np
import jax
import jax.numpy as jnp
from jax import lax
from jax.experimental import pallas as pl
from jax.experimental.pallas import tpu as pltpu

F32 = jnp.float32
BF16 = jnp.bfloat16
NEG = -1e30
LOG2E = math.log2(math.e)
VMEM_LIMIT = 48 * 1024 * 1024

D_MODEL = 1024
DEPTH = 2
ALPHA = (2.0 * DEPTH) ** 0.25
LN_EPS = 1e-5
REL_BUCKETS = 32
REL_MAX_DIST = 2048

A_HEADS, A_DIM, A_W = 8, 64, 512
A_PATTERNS = ((128, 1), (512, 4), (2048, 16))
A_BLOCK = 128
B_HEADS, B_DIM, B_W = 4, 128, 512
B_CHUNK = 128
B_CONV = 4
AB_PROJ = 3592
AB_PAD = 3712

C_HEADS, C_GROUPS, C_HPG, C_DIM, C_W = 8, 2, 4, 128, 1024
CMP_BLOCK, CMP_STRIDE, CMP_HIDDEN = 32, 16, 256
SLC_BLOCK, SLC_TOP_N, WIN = 64, 16, 512
C_PROJ = 2584
TQ = 128
CMP_PAD = 128

N_GROUPS, EPG, N_EXPERTS, E_HID = 4, 4, 16, 512
N_BUCKETS = N_GROUPS * 6
TM = 512
TM_MOE = 256
MOE_CHUNKS = 1


def _dot(a, b):
    return jnp.dot(a, b, preferred_element_type=F32)


def _dot_nt(a, b):
    return lax.dot_general(a, b, (((1,), (1,)), ((), ())), preferred_element_type=F32)


def _params(sem):
    return pltpu.CompilerParams(dimension_semantics=sem, vmem_limit_bytes=VMEM_LIMIT)


def _bucket_np(n):
    n = np.maximum(n, 0)
    exact = REL_BUCKETS // 2
    nf = np.maximum(n, 1).astype(np.float64)
    large = exact + (np.log(nf / exact) / math.log(REL_MAX_DIST / exact)
                     * (REL_BUCKETS - exact)).astype(np.int64)
    return np.where(n < exact, n, np.minimum(large, REL_BUCKETS - 1)).astype(np.int32)


def _bias_tab_kernel(tab_ref, idx_ref, out_ref, *, shift, scale):
    R = idx_ref.shape[1]
    RC = 32

    def body(i, carry):
        r0 = pl.multiple_of(i * RC, RC)
        idx = idx_ref[0, pl.ds(r0, RC), :]
        for h in range(8):
            base = tab_ref[REL_BUCKETS - 1, h] if shift else 0.0
            val = jnp.full(idx.shape, (tab_ref[0, h] - base) * scale, F32)
            for b in range(1, REL_BUCKETS):
                val = jnp.where(idx == b, (tab_ref[b, h] - base) * scale, val)
            out_ref[0, h, pl.ds(r0, RC), :] = jnp.where(idx < 0, NEG, val)
        return carry

    lax.fori_loop(0, R // RC, body, 0)


def _bias_tables(rel_bias, idx_np, shift, scale=1.0):
    T, R, C = idx_np.shape
    return pl.pallas_call(
        functools.partial(_bias_tab_kernel, shift=shift, scale=scale),
        grid=(T,),
        in_specs=[pl.BlockSpec(memory_space=pltpu.SMEM),
                  pl.BlockSpec((1, R, C), lambda t: (t, 0, 0))],
        out_specs=pl.BlockSpec((1, 8, R, C), lambda t: (t, 0, 0, 0)),
        out_shape=jax.ShapeDtypeStruct((T, 8, R, C), F32),
        compiler_params=_params(("parallel",)),
        name="bias_tables",
    )(rel_bias.astype(F32), jnp.asarray(idx_np))


def _dilated_idx():
    qi = np.arange(A_BLOCK)[:, None]
    ki = np.arange(2 * A_BLOCK)[None, :]
    j = qi + A_BLOCK - ki
    out = []
    for window, dilation in A_PATTERNS:
        nk = window // dilation
        valid = (j >= 0) & (j <= nk)
        out.append(np.where(valid, _bucket_np(np.maximum(j, 0) * dilation), -1))
    return np.stack(out).astype(np.int32)


def _sel_idx():
    a = np.arange(TQ)[:, None]
    c = np.arange(TQ)[None, :]
    n_delta = -(-(_far_dist() + TQ) // TQ)
    out = []
    for delta in range(-1, n_delta + 1):
        dist = delta * TQ + a - c
        out.append(np.where(dist >= 0, _bucket_np(dist), -1))
    return np.stack(out).astype(np.int32)


def _far_dist():
    n = np.arange(0, 4 * REL_MAX_DIST)
    b = _bucket_np(n)
    return int(np.max(n[b < REL_BUCKETS - 1])) + 1


def _win_idx():
    a = np.arange(TQ)[:, None]
    c = np.arange(TQ)[None, :]
    out = []
    for delta in range(-1, WIN // TQ + 1):
        dist = delta * TQ + a - c
        out.append(np.where((dist >= 0) & (dist < WIN), _bucket_np(dist), -1))
    return np.stack(out).astype(np.int32)


CMP_PER_TILE = TQ // CMP_STRIDE
CMP_CLASSES = TQ // CMP_PER_TILE
CMP_SPLIT = 13


def _cmp_window_start(qb):
    return qb // CMP_CLASSES + (1 if qb % CMP_CLASSES >= CMP_SPLIT else 0)


def _cmp_idx():
    a = np.arange(TQ)[:, None]
    c = np.arange(TQ)[None, :]
    out = []
    for r in range(CMP_CLASSES):
        qb = CMP_CLASSES + r
        i0 = _cmp_window_start(qb) * TQ - CMP_PAD
        for half in range(2):
            dist = qb * TQ + a - ((i0 + half * TQ + c) * CMP_STRIDE + CMP_BLOCK - 1)
            out.append(np.where(dist >= 0, _bucket_np(dist), -1))
        out.append(np.full((TQ, TQ), REL_BUCKETS - 1))
        out.append(np.full((TQ, TQ), -1))
    return np.stack(out).astype(np.int32)


def _nsa_tables(rel_bias):
    tsel = _bias_tables(rel_bias, _sel_idx(), shift=True, scale=LOG2E)
    twin = _bias_tables(rel_bias, _win_idx(), shift=False, scale=LOG2E)
    tcmp = _bias_tables(rel_bias, _cmp_idx(), shift=True, scale=LOG2E)
    return tsel, twin, tcmp.reshape(CMP_CLASSES, 4, 8, TQ, TQ)


def _check_cmp_windows(S):
    far = _far_dist()
    for qb in range(S // TQ):
        i0 = _cmp_window_start(qb) * TQ - CMP_PAD
        s0 = qb * TQ
        assert s0 - ((i0 - 1) * CMP_STRIDE + CMP_BLOCK - 1) >= far
        assert s0 + TQ - 1 - ((i0 + 2 * TQ) * CMP_STRIDE + CMP_BLOCK - 1) < 0


def _residue_col(d, r):
    return (r % 4) * 4 + r // 4 if d == 16 else r


def _lane_chunks_store(ref3, val):
    for c in range(ref3.shape[0]):
        ref3[c] = val[:, c * 128:(c + 1) * 128]


def _to_residue_layout(src3_ref, dst, d, col_of=_residue_col):
    nc, rows, _ = src3_ref.shape
    for r in range(d):
        cb = col_of(d, r)
        for c in range(nc):
            col = (cb * nc + c) * 128
            dst(slice(col, col + 128), src3_ref[c, pl.ds(r, rows // d, stride=d), :])


def _ab_proj_kernel(x_ref, xh_ref, w_ref, cw_ref, gb_ref,
                    aq_ref, ak_ref, av_ref, aq4_ref, ak4_ref, av4_ref, aq16_ref, ak16_ref, av16_ref,
                    bq_ref, bk_ref, bv_ref, bo_ref, g_ref,
                    pre_ref, tmp_ref, *, tiles_per_seq):
    i = pl.program_id(0)
    tm = x_ref.shape[0]
    xb = x_ref[...].astype(BF16)
    for c, scale, outs in ((0, A_DIM ** -0.5, (aq_ref, aq4_ref, aq16_ref)),
                           (1, 1.0, (ak_ref, ak4_ref, ak16_ref)),
                           (2, 1.0, (av_ref, av4_ref, av16_ref))):
        val = _dot(xb, w_ref[:, c * A_W:(c + 1) * A_W]) * scale
        _lane_chunks_store(tmp_ref, val)
        outs[0][...] = val.astype(BF16)
        for d, o_ref in ((4, outs[1]), (16, outs[2])):
            def put(cols, piece, o_ref=o_ref):
                o_ref[:, cols] = piece.astype(BF16)
            _to_residue_layout(tmp_ref, put, d)
    bv_ref[...] = _dot(xb, w_ref[:, 2560:3072]).astype(BF16)
    bo_ref[...] = _dot(xb, w_ref[:, 3072:3584])
    g_ref[...] = _dot(xb, w_ref[:, 3584:AB_PAD]) + gb_ref[...]
    halo = _dot(xh_ref[...].astype(BF16), w_ref[:, 1536:2560])
    halo = jnp.where(i % tiles_per_seq == 0, 0.0, halo)
    pre_ref[0:8, :] = halo
    pre_ref[8:8 + tm, :] = _dot(xb, w_ref[:, 1536:2560])
    y = pre_ref[8:8 + tm, :] * cw_ref[B_CONV - 1:B_CONV, :]
    for k in range(B_CONV - 1):
        s = B_CONV - 1 - k
        y = y + pre_ref[8 - s:8 - s + tm, :] * cw_ref[k:k + 1, :]
    y = y / (1.0 + jnp.exp(-y))
    bq_ref[...] = (y[:, :B_W] * (B_DIM ** -0.5)).astype(BF16)
    bk_ref[...] = y[:, B_W:].astype(BF16)


def _ab_proj(x2, w_pad, conv_w, gate_b_pad, S):
    N = x2.shape[0]
    tm = TM
    tps = S // tm
    row = lambda i: (i, 0)
    fix = lambda i: (0, 0)
    lay = lambda d: [jax.ShapeDtypeStruct((N // d, d * A_W), BF16)] * 3
    lay_spec = lambda d: [pl.BlockSpec((tm // d, d * A_W), row)] * 3
    outs = lay(1) + lay(4) + lay(16) + [jax.ShapeDtypeStruct((N, 512), BF16)] * 3 + [
        jax.ShapeDtypeStruct((N, 512), F32), jax.ShapeDtypeStruct((N, 128), F32)]
    o_specs = (lay_spec(1) + lay_spec(4) + lay_spec(16) + [pl.BlockSpec((tm, 512), row)] * 4
               + [pl.BlockSpec((tm, 128), row)])
    return pl.pallas_call(
        functools.partial(_ab_proj_kernel, tiles_per_seq=tps),
        grid=(N // tm,),
        in_specs=[pl.BlockSpec((tm, D_MODEL), row),
                  pl.BlockSpec((8, D_MODEL), lambda i: (jnp.maximum(i * (tm // 8) - 1, 0), 0)),
                  pl.BlockSpec((D_MODEL, AB_PAD), fix),
                  pl.BlockSpec((B_CONV, 2 * B_W), fix),
                  pl.BlockSpec((1, 128), fix)],
        out_specs=o_specs,
        out_shape=outs,
        scratch_shapes=[pltpu.VMEM((tm + 8, 2 * B_W), F32), pltpu.VMEM((A_W // 128, tm, 128), F32)],
        compiler_params=_params(("parallel",)),
        name="ab_proj",
    )(x2, x2, w_pad, conv_w, gate_b_pad)


def _dilated_kernel(*refs, has_prev, is_last):
    if has_prev:
        q_ref, kp_ref, kc_ref, vp_ref, vc_ref, tab_ref, op_ref, lp_ref = refs[:8]
        rest = refs[8:]
    else:
        q_ref, kp_ref, kc_ref, vp_ref, vc_ref, tab_ref = refs[:6]
        rest = refs[6:]
    outs, scratch = (rest, ()) if is_last else (rest[:2], rest[2:])
    o_ref = outs[0]
    n = pl.program_id(2)
    first = jnp.where(n == 0, NEG, 0.0)
    lane = lax.broadcasted_iota(jnp.int32, (A_BLOCK, 128), 1)
    keep_side = [jnp.where(lane < A_DIM, 1.0, 0.0).astype(BF16), jnp.where(lane < A_DIM, 0.0, 1.0).astype(BF16)]
    odd = jnp.bitwise_and(lane, 1) == 1
    ones_side = [jnp.where(odd, 0.0, 1.0).astype(BF16), jnp.where(odd, 1.0, 0.0).astype(BF16)]
    m_tile = jnp.zeros((A_BLOCK, 128), F32)
    l_tile = jnp.ones((A_BLOCK, 128), F32)
    unnorm = []
    for j in range(A_HEADS // 2):
        cs = slice(j * 128, (j + 1) * 128)
        q2 = q_ref[0, :, cs]
        kp, kc, vp, vc = kp_ref[0, :, cs], kc_ref[0, :, cs], vp_ref[0, :, cs], vc_ref[0, :, cs]
        k_st = jnp.concatenate([kp * keep_side[0], kc * keep_side[0],
                                kp * keep_side[1], kc * keep_side[1]], axis=0)
        s = _dot_nt(q2, k_st)
        p_parts = []
        for side in range(2):
            h = 2 * j + side
            c0 = 2 * side * A_BLOCK
            sp = s[:, c0:c0 + A_BLOCK] + tab_ref[0, h, :, 0:A_BLOCK] + first
            sc = s[:, c0 + A_BLOCK:c0 + 2 * A_BLOCK] + tab_ref[0, h, :, A_BLOCK:2 * A_BLOCK]
            m = jnp.max(jnp.maximum(sp, sc), axis=1, keepdims=True)
            p_parts += [jnp.exp(sp - m).astype(BF16), jnp.exp(sc - m).astype(BF16)]
            m_tile = jnp.where(lane == A_DIM + h, m, m_tile)
        v_st = jnp.concatenate(
            [jnp.concatenate([v * keep_side[side], ones_side[side]], axis=1)
             for side in range(2) for v in (vp, vc)], axis=0)
        r = _dot(jnp.concatenate(p_parts, axis=1), v_st)
        unnorm.append(r[:, 0:128])
        pair = (lane == A_DIM + 2 * j) | (lane == A_DIM + 2 * j + 1)
        l_tile = jnp.where(pair, r[:, 128:256], l_tile)
    stat = (lane >= A_DIM) & (lane < A_DIM + A_HEADS)
    lse = m_tile + jnp.log(l_tile)
    if has_prev:
        lp = lp_ref[0]
        mm = jnp.maximum(lp, lse)
        wp = jnp.exp(lp - mm)
        wc = jnp.exp(lse - mm)
        tot = wp + wc
        scale_prev = jnp.where(stat, wp / tot, 0.0)
        scale_cur = jnp.where(stat, wc / (tot * l_tile), 0.0)
        lse = mm + jnp.log(tot)
    else:
        scale_cur = jnp.where(stat, 1.0 / l_tile, 0.0)
    erow = lax.broadcasted_iota(jnp.int32, (128, A_W), 0)
    ecol = lax.broadcasted_iota(jnp.int32, (128, A_W), 1)
    expand = jnp.where(erow - A_DIM == jnp.right_shift(ecol, A_DIM.bit_length() - 1), 1.0, 0.0).astype(BF16)

    def spread(t):
        hi = t.astype(BF16)
        return _dot(hi, expand) + _dot((t - hi.astype(F32)).astype(BF16), expand)

    o = jnp.concatenate(unnorm, axis=1) * spread(scale_cur)
    if has_prev:
        o = o + op_ref[0] * spread(scale_prev)
    if is_last:
        o_ref[0] = o.astype(o_ref.dtype)
    else:
        o_scr, l_scr = scratch
        _lane_chunks_store(o_scr, o)
        l_scr[0] = jnp.where(stat, lse, 0.0)

        def put_o(cols, piece):
            o_ref[0, :, cols] = piece

        def put_l(cols, piece):
            outs[1][0, :, cols] = piece

        _to_residue_layout(o_scr, put_o, 4)
        _to_residue_layout(l_scr, put_l, 4)


def _dilated_call(q, k, v, tab, prev, pattern_idx, dilation, B, S, is_last):
    d = dilation
    L = S // d
    nb = L // A_BLOCK
    r3 = lambda a: a.reshape(B, L, a.shape[-1])
    cur = lambda b, r, n: (b, n, r)
    prv = lambda b, r, n: (b, jnp.maximum(n - 1, 0), r)
    blk = pl.BlockSpec((1, A_BLOCK, A_W), cur)
    in_specs = [blk, pl.BlockSpec((1, A_BLOCK, A_W), prv), blk,
                pl.BlockSpec((1, A_BLOCK, A_W), prv), blk,
                pl.BlockSpec((1, 8, A_BLOCK, 2 * A_BLOCK), lambda b, r, n: (pattern_idx, 0, 0, 0))]
    args = [r3(q), r3(k), r3(k), r3(v), r3(v), tab]
    has_prev = prev is not None
    if has_prev:
        in_specs += [blk, pl.BlockSpec((1, A_BLOCK, 128), cur)]
        args += [r3(prev[0]), r3(prev[1])]
    scratch = []
    if is_last:
        out_shape = [jax.ShapeDtypeStruct((B, L, d * A_W), BF16)]
        out_specs = [blk]
    else:
        d2, rows = 4 * d, A_BLOCK // 4
        nxt = lambda b, r, n: (b, n, r)
        out_shape = [jax.ShapeDtypeStruct((B, S // d2, d2 * A_W), F32),
                     jax.ShapeDtypeStruct((B, S // d2, d2 * 128), F32)]
        out_specs = [pl.BlockSpec((1, rows, 4 * A_W), nxt), pl.BlockSpec((1, rows, 4 * 128), nxt)]
        scratch = [pltpu.VMEM((A_W // 128, A_BLOCK, 128), F32), pltpu.VMEM((1, A_BLOCK, 128), F32)]
    res = pl.pallas_call(
        functools.partial(_dilated_kernel, has_prev=has_prev, is_last=is_last),
        grid=(B, d, nb),
        in_specs=in_specs, out_specs=out_specs, out_shape=out_shape, scratch_shapes=scratch,
        compiler_params=_params(("parallel", "parallel", "arbitrary")),
        name="dilated_d%d" % d,
    )(*args)
    return [r.reshape(-1, r.shape[-1]) for r in res]


def _dilated_attention(qkv_by_dilation, tab, B, S):
    prev = None
    for p, (window, d) in enumerate(A_PATTERNS):
        assert window // d == A_BLOCK and S % (d * A_BLOCK) == 0
        assert p == 0 or d == 4 * A_PATTERNS[p - 1][1]
        last = p == len(A_PATTERNS) - 1
        q, k, v = qkv_by_dilation[d]
        prev = _dilated_call(q, k, v, tab, prev, p, d, B, S, last)
    return prev[0]


def _split3(x):
    hi = x.astype(BF16)
    r = x - hi.astype(F32)
    mid = r.astype(BF16)
    lo = (r - mid.astype(F32)).astype(BF16)
    return hi, mid, lo


def _mlstm_kernel(q_ref, k_ref, v_ref, g_ref, bo_ref, hg_ref, y_ref, c_ref, m_ref):
    L = B_CHUNK
    c = pl.program_id(1)

    @pl.when(c == 0)
    def _():
        c_ref[...] = jnp.zeros_like(c_ref)
        m_ref[...] = jnp.zeros_like(m_ref)

    lane = lax.broadcasted_iota(jnp.int32, (L, 128), 1)
    row = lax.broadcasted_iota(jnp.int32, (L, L), 0)
    col = lax.broadcasted_iota(jnp.int32, (L, L), 1)
    tri = row >= col
    is_f = (lane >= B_HEADS) & (lane < 2 * B_HEADS)
    tril = jnp.where(tri, 1.0, 0.0).astype(BF16)
    ones = jnp.ones((L, B_DIM), BF16)
    for bi, h in [(bi, h) for bi in range(q_ref.shape[0]) for h in range(B_HEADS)]:
        if h == 0:
            g = g_ref[bi]
            logf = jnp.minimum(g, 0.0) - jnp.log(1.0 + jnp.exp(-jnp.abs(g)))
            gl = jnp.where(is_f, logf, jnp.where(lane < B_HEADS, g, 0.0))
            hi, mid, lo = _split3(gl)
            cum = _dot(tril, hi) + _dot(tril, mid) + _dot(tril, lo)
            cum_t = cum.T
            gl_t = gl.T
        st = bi * B_HEADS + h
        cs = slice(h * B_DIM, (h + 1) * B_DIM)
        q = q_ref[bi, :, cs]
        k = k_ref[bi, :, cs]
        v_ext = jnp.concatenate([v_ref[bi, :, cs], ones], axis=1)
        b_col = cum[:, B_HEADS + h:B_HEADS + h + 1]
        b_row = cum_t[B_HEADS + h:B_HEADS + h + 1, :]
        i_col = gl[:, h:h + 1]
        i_row = gl_t[h:h + 1, :]
        m_prev = m_ref[st, 0:1, 0:1]
        dm = jnp.where(tri, b_col - b_row + i_row, NEG)
        inter = b_col + m_prev
        m_t = jnp.maximum(inter, jnp.max(dm, axis=1, keepdims=True))
        p = jnp.exp(dm - m_t)
        sqk = _dot_nt(q, k) * p
        sc = jnp.exp(inter - m_t)
        lhs = jnp.concatenate([(sc * q.astype(F32)).astype(BF16), sqk.astype(BF16)], axis=1)
        c_ext = c_ref[st]
        rhs = jnp.concatenate([c_ext.astype(BF16), v_ext], axis=0)
        res = _dot(lhs, rhs)
        num = res[:, :B_DIM]
        den = res[:, B_DIM:]
        hh = num / jnp.maximum(jnp.abs(den), jnp.exp(-m_t))
        b_last = b_col[L - 1:L, :]
        gk = b_last - b_col + i_col
        m_new = jnp.maximum(b_last + m_prev, jnp.max(gk, axis=0, keepdims=True))
        wk = jnp.exp(gk - m_new)
        decay = jnp.exp(b_last + m_prev - m_new)
        kw_t = (wk * k.astype(F32)).T.astype(BF16)
        c_ref[st] = decay * c_ext + _dot(kw_t, v_ext)
        m_ref[st] = jnp.broadcast_to(m_new, (8, 128))
        mu = jnp.mean(hh, axis=1, keepdims=True)
        xc = hh - mu
        var = jnp.mean(xc * xc, axis=1, keepdims=True)
        hn = xc * lax.rsqrt(var + LN_EPS) * hg_ref[:, cs]
        bo = bo_ref[bi, :, cs]
        y_ref[bi, :, cs] = (hn / (1.0 + jnp.exp(-bo))).astype(BF16)


def _mlstm(bq, bk, bv, gates, bo, head_g, B, S):
    nc = S // B_CHUNK
    bb = 1
    r3 = lambda a: a.reshape(B, S, a.shape[-1])
    blk = pl.BlockSpec((bb, B_CHUNK, B_W), lambda b, c: (b, c, 0))
    y = pl.pallas_call(
        _mlstm_kernel,
        grid=(B // bb, nc),
        in_specs=[blk, blk, blk,
                  pl.BlockSpec((bb, B_CHUNK, 128), lambda b, c: (b, c, 0)),
                  blk,
                  pl.BlockSpec((1, B_W), lambda b, c: (0, 0))],
        out_specs=blk,
        out_shape=jax.ShapeDtypeStruct((B, S, B_W), BF16),
        scratch_shapes=[pltpu.VMEM((bb * B_HEADS, B_DIM, 2 * B_DIM), F32),
                        pltpu.VMEM((bb * B_HEADS, 8, 128), F32)],
        compiler_params=_params(("parallel", "arbitrary")),
        name="mlstm",
    )(r3(bq), r3(bk), r3(bv), r3(gates), r3(bo), head_g.reshape(1, B_W).astype(F32))
    return y.reshape(B * S, B_W)


def _layer_norm(z, g, b):
    mu = jnp.mean(z, axis=1, keepdims=True)
    zc = z - mu
    var = jnp.mean(zc * zc, axis=1, keepdims=True)
    return zc * lax.rsqrt(var + LN_EPS) * g + b


def _route(logits, cnt_ref):
    tm = logits.shape[0]
    lt = logits.T
    col = lambda c: lt[c:c + 1, :]
    gl = [col(c) for c in range(N_GROUPS)]
    gmax = functools.reduce(jnp.maximum, gl)
    gsum = sum(jnp.exp(x - gmax) for x in gl)
    g_idx = jnp.full(gmax.shape, N_GROUPS - 1, jnp.int32)
    for c in range(N_GROUPS - 2, -1, -1):
        g_idx = jnp.where(gl[c] == gmax, c, g_idx)
    g_w = 1.0 / gsum
    el = []
    for k in range(EPG):
        x = col(N_GROUPS + (N_GROUPS - 1) * EPG + k)
        for g in range(N_GROUPS - 2, -1, -1):
            x = jnp.where(g_idx == g, col(N_GROUPS + g * EPG + k), x)
        el.append(x)
    v1 = functools.reduce(jnp.maximum, el)
    i1 = jnp.full(v1.shape, EPG - 1, jnp.int32)
    for k in range(EPG - 2, -1, -1):
        i1 = jnp.where(el[k] == v1, k, i1)
    el2 = [jnp.where(i1 == k, -jnp.inf, el[k]) for k in range(EPG)]
    v2 = functools.reduce(jnp.maximum, el2)
    i2 = jnp.full(v2.shape, EPG - 1, jnp.int32)
    for k in range(EPG - 2, -1, -1):
        i2 = jnp.where((el2[k] == v2) & (i1 != k), k, i2)
    t = jnp.exp(v2 - v1)
    w1 = g_w / (1.0 + t)
    w2 = w1 * t
    a = jnp.minimum(i1, i2)
    b = jnp.maximum(i1, i2)
    pair = jnp.where(a == 0, b - 1, jnp.where(a == 1, b + 1, 5))
    bucket = (g_idx * 6 + pair).astype(F32)
    w_lo = jnp.where(i1 < i2, w1, w2)
    w_hi = jnp.where(i1 < i2, w2, w1)
    sub = lax.broadcasted_iota(jnp.int32, (128, tm), 0)
    onehot_t = jnp.where(sub.astype(F32) == bucket, 1.0, 0.0)
    srow = lax.broadcasted_iota(jnp.int32, (tm, tm), 0)
    scol = lax.broadcasted_iota(jnp.int32, (tm, tm), 1)
    before = jnp.where(srow < scol, 1.0, 0.0).astype(BF16)
    oh = onehot_t.astype(BF16)
    carry = cnt_ref[...]
    prior = _dot(oh, before) + jnp.concatenate([carry] * (tm // 128), axis=1)
    rank = jnp.sum(onehot_t * prior, axis=0, keepdims=True)
    cnt_ref[...] = carry + _dot(oh, jnp.ones((tm, 128), BF16))
    out_t = jnp.where(sub == 0, bucket, jnp.where(sub == 1, w_lo, jnp.where(sub == 2, w_hi,
                      jnp.where(sub == 3, rank, 0.0))))
    return out_t.T


def _out_ln_route_kernel(*refs, n_in, layouts):
    y_refs = refs[:n_in]
    w_refs = refs[n_in:2 * n_in]
    x_ref, g_ref, b_ref, wrh_ref, wrl_ref, br_ref, h_ref, c_ref, cnt_ref = refs[2 * n_in:2 * n_in + 9]
    pos_refs = list(refs[2 * n_in + 9:])
    tm = x_ref.shape[0]

    @pl.when(pl.program_id(0) == 0)
    def _():
        cnt_ref[...] = jnp.zeros_like(cnt_ref)

    y = None
    for i in range(n_in):
        d = layouts[i]
        if d == 1:
            lhs = y_refs[i][...]
        else:
            s_ref = pos_refs.pop(0)
            nc = s_ref.shape[0]
            for r in range(d):
                cb = _residue_col(d, r)
                for c in range(nc):
                    col = (cb * nc + c) * 128
                    s_ref[c, pl.ds(r, tm // d, stride=d), :] = y_refs[i][:, col:col + 128].astype(F32)
            lhs = jnp.concatenate([s_ref[c] for c in range(nc)], axis=1).astype(BF16)
        t = _dot(lhs, w_refs[i][...])
        y = t if y is None else y + t
    hn = _layer_norm(ALPHA * x_ref[...] + y, g_ref[...], b_ref[...])
    h_ref[:, 0:D_MODEL] = hn
    hi = hn.astype(BF16)
    lo = (hn - hi.astype(F32)).astype(BF16)
    logits = (_dot_nt(hi, wrh_ref[...]) + _dot_nt(lo, wrh_ref[...]) + _dot_nt(hi, wrl_ref[...])
              + br_ref[...])
    h_ref[:, D_MODEL:D_MODEL + 128] = _route(logits, cnt_ref)
    c_ref[...] = cnt_ref[...]


def _out_ln_route(ys, layouts, ws, x2, ln_g, ln_b, wr, br, row_start, N):
    tm = TM
    off = row_start // tm
    row = lambda i: (i + off, 0)
    fix = lambda i: (0, 0)
    wr_hi = wr.astype(BF16)
    wr_lo = (wr - wr_hi.astype(F32)).astype(BF16)
    in_specs = ([pl.BlockSpec((tm // d, y.shape[1]), row) for y, d in zip(ys, layouts)]
                + [pl.BlockSpec(w.shape, fix) for w in ws]
                + [pl.BlockSpec((tm, D_MODEL), row),
                   pl.BlockSpec((1, D_MODEL), fix), pl.BlockSpec((1, D_MODEL), fix),
                   pl.BlockSpec((128, D_MODEL), fix), pl.BlockSpec((128, D_MODEL), fix),
                   pl.BlockSpec((1, 128), fix)])
    return pl.pallas_call(
        functools.partial(_out_ln_route_kernel, n_in=len(ys), layouts=tuple(layouts)),
        grid=(N // tm,),
        in_specs=in_specs,
        out_specs=[pl.BlockSpec((tm, D_MODEL + 128), lambda i: (i, 0)), pl.BlockSpec((128, 128), fix)],
        out_shape=[jax.ShapeDtypeStruct((N, D_MODEL + 128), F32), jax.ShapeDtypeStruct((128, 128), F32)],
        scratch_shapes=[pltpu.VMEM((128, 128), F32)] + [
            pltpu.VMEM((y.shape[1] // d // 128, tm, 128), F32) for y, d in zip(ys, layouts) if d != 1],
        compiler_params=_params(("arbitrary",)),
        name="out_ln_route",
    )(*ys, *ws, x2, ln_g.reshape(1, -1), ln_b.reshape(1, -1), wr_hi, wr_lo, br)


def _router_weights(wr_g, br_g, wr_e, br_e):
    we = wr_e.transpose(0, 2, 1).reshape(N_GROUPS * EPG, D_MODEL)
    w = jnp.concatenate([wr_g.T, we], axis=0)
    w = jnp.pad(w, ((0, 128 - w.shape[0]), (0, 0)))
    b = jnp.concatenate([br_g, br_e.reshape(-1)])
    b = jnp.pad(b, (0, 128 - b.shape[0])).reshape(1, 128)
    return w.astype(F32), b.astype(F32)


_PAIRS = ((0, 1), (0, 2), (0, 3), (1, 2), (1, 3), (2, 3))


def _moe_kernel(elo_ref, ehi_ref, chg_ref, nt_ref,
                x_ref, wgl_ref, wul_ref, wdl_ref, wgh_ref, wuh_ref, wdh_ref,
                g_ref, b_ref, o_ref, wg_s, wu_s, wd_s):
    t = pl.program_id(0)

    @pl.when(chg_ref[t] == 1)
    def _():
        wg_s[0] = wgl_ref[0, 0].astype(BF16)
        wu_s[0] = wul_ref[0, 0].astype(BF16)
        wd_s[0] = wdl_ref[0, 0].astype(BF16)
        wg_s[1] = wgh_ref[0, 0].astype(BF16)
        wu_s[1] = wuh_ref[0, 0].astype(BF16)
        wd_s[1] = wdh_ref[0, 0].astype(BF16)

    @pl.when(t < nt_ref[0])
    def _():
        x = x_ref[:, 0:D_MODEL]
        xb = x.astype(BF16)
        r = x_ref[:, D_MODEL:D_MODEL + 128]
        acc = None
        for e in range(2):
            a = _dot(xb, wg_s[e])
            u = _dot(xb, wu_s[e])
            hcur = (a / (1.0 + jnp.exp(-a))) * u * r[:, 1 + e:2 + e]
            y = _dot(hcur.astype(BF16), wd_s[e])
            acc = y if acc is None else acc + y
        o_ref[...] = _layer_norm(ALPHA * x + acc, g_ref[...], b_ref[...])

    @pl.when(t >= nt_ref[0])
    def _():
        o_ref[...] = jnp.zeros_like(o_ref)


def _moe(hx, cnt, layer, w_gate, w_up, w_down, ln_g, ln_b):
    N = hx.shape[0]
    tm = TM_MOE
    n_tiles = N // tm + N_BUCKETS
    n_pad = n_tiles * tm
    bucket = hx[:, D_MODEL].astype(jnp.int32)
    rank = hx[:, D_MODEL + 3].astype(jnp.int32)
    counts = cnt[:N_BUCKETS, 0].astype(jnp.int32)
    padded = ((counts + tm - 1) // tm) * tm
    ends = jnp.cumsum(padded)
    offs = ends - padded
    b2 = bucket.reshape(-1, 128)
    off2 = functools.reduce(lambda acc, b: jnp.where(b2 == b, offs[b], acc), range(N_BUCKETS),
                            jnp.zeros_like(b2))
    dest = off2.reshape(-1) + rank
    src = (jnp.arange(n_pad, dtype=jnp.int32) % N).at[dest].set(
        jnp.arange(N, dtype=jnp.int32), mode="promise_in_bounds", unique_indices=True)
    tile_start = jnp.arange(n_tiles, dtype=jnp.int32) * tm
    n_used = (ends[-1] // tm).astype(jnp.int32)
    tb = jnp.sum((tile_start[:, None] >= ends[None, :]).astype(jnp.int32), axis=1)
    tb_last = jnp.take(tb, jnp.maximum(n_used - 1, 0))
    tb = jnp.where(tile_start < ends[-1], tb, tb_last)
    pairs = jnp.asarray(_PAIRS, jnp.int32)
    elo = (tb // 6) * EPG + pairs[tb % 6, 0]
    ehi = (tb // 6) * EPG + pairs[tb % 6, 1]
    chg = jnp.concatenate([jnp.ones((1,), jnp.int32), (tb[1:] != tb[:-1]).astype(jnp.int32)])
    xs = hx.at[src].get(mode="promise_in_bounds")

    row = lambda t, *_: (t, 0)
    fix = lambda t, *_: (0, 0)
    wlo = lambda t, elo, ehi, chg, nt: (layer, elo[t], 0, 0)
    whi = lambda t, elo, ehi, chg, nt: (layer, ehi[t], 0, 0)
    up_spec = lambda im: pl.BlockSpec((1, 1, D_MODEL, E_HID), im)
    dn_spec = lambda im: pl.BlockSpec((1, 1, E_HID, D_MODEL), im)
    grid_spec = pltpu.PrefetchScalarGridSpec(
        num_scalar_prefetch=4,
        grid=(n_tiles,),
        in_specs=[pl.BlockSpec((tm, D_MODEL + 128), row),
                  up_spec(wlo), up_spec(wlo), dn_spec(wlo),
                  up_spec(whi), up_spec(whi), dn_spec(whi),
                  pl.BlockSpec((1, D_MODEL), fix), pl.BlockSpec((1, D_MODEL), fix)],
        out_specs=pl.BlockSpec((tm, D_MODEL), row),
        scratch_shapes=[pltpu.VMEM((2, D_MODEL, E_HID), BF16),
                        pltpu.VMEM((2, D_MODEL, E_HID), BF16),
                        pltpu.VMEM((2, E_HID, D_MODEL), BF16)])
    out_sorted = pl.pallas_call(
        _moe_kernel,
        grid_spec=grid_spec,
        out_shape=jax.ShapeDtypeStruct((n_pad, D_MODEL), F32),
        compiler_params=_params(("arbitrary",)),
        name="moe",
    )(elo, ehi, chg, n_used.reshape(1), xs, w_gate, w_up, w_down, w_gate, w_up, w_down,
      ln_g.reshape(1, -1), ln_b.reshape(1, -1))
    return out_sorted.at[dest].get(mode="promise_in_bounds", unique_indices=True)


def _c_proj_kernel(x_ref, w_ref, gb_ref, q_ref, kc_ref, kv_ref, g_ref, tmp_ref):
    xb = x_ref[...].astype(BF16)
    q_ref[...] = (_dot(xb, w_ref[:, 0:C_W]) * (C_DIM ** -0.5 * LOG2E)).astype(BF16)
    for i in range(4):
        tmp_ref[0] = _dot(xb, w_ref[:, C_W + i * 128:C_W + (i + 1) * 128])

        def put(cols, piece, i=i):
            kc_ref[i, :, cols] = piece.astype(BF16)

        _to_residue_layout(tmp_ref, put, CMP_STRIDE, col_of=lambda d, r: r)
    kv_ref[...] = _dot(xb, w_ref[:, C_W + 512:C_W + 1536]).astype(BF16)
    z = _dot(xb, w_ref[:, C_W + 1536:C_W + 1792]) + gb_ref[...]
    g_ref[...] = 1.0 / (1.0 + jnp.exp(-z))


def _c_proj(x2, w_pad, gb_pad):
    N = x2.shape[0]
    tm = TM
    row = lambda i: (i, 0)
    fix = lambda i: (0, 0)
    wcols = w_pad.shape[1]
    return pl.pallas_call(
        _c_proj_kernel,
        grid=(N // tm,),
        in_specs=[pl.BlockSpec((tm, D_MODEL), row), pl.BlockSpec((D_MODEL, wcols), fix),
                  pl.BlockSpec((1, 256), fix)],
        out_specs=[pl.BlockSpec((tm, C_W), row),
                   pl.BlockSpec((4, tm // CMP_STRIDE, CMP_STRIDE * C_DIM), lambda i: (0, i, 0)),
                   pl.BlockSpec((tm, 1024), row), pl.BlockSpec((tm, 256), row)],
        out_shape=[jax.ShapeDtypeStruct((N, C_W), BF16),
                   jax.ShapeDtypeStruct((4, N // CMP_STRIDE, CMP_STRIDE * C_DIM), BF16),
                   jax.ShapeDtypeStruct((N, 1024), BF16), jax.ShapeDtypeStruct((N, 256), F32)],
        scratch_shapes=[pltpu.VMEM((1, tm, 128), F32)],
        compiler_params=_params(("parallel",)),
        name="c_proj",
    )(x2, w_pad, gb_pad)


def _c_weights(w_in, gate_b):
    gcols = []
    gb = []
    for g in range(C_GROUPS):
        idx = [C_PROJ - 3 * C_HEADS + br * C_HEADS + g * C_HPG + j for br in range(3) for j in range(C_HPG)]
        gcols.append(jnp.pad(w_in[:, np.asarray(idx)], ((0, 0), (0, 128 - len(idx)))))
        gb.append(jnp.pad(gate_b[np.asarray(idx) - (C_PROJ - 3 * C_HEADS)], (0, 128 - len(idx))))
    w = jnp.concatenate([w_in[:, :C_PROJ - 3 * C_HEADS]] + gcols, axis=1).astype(BF16)
    return w, jnp.concatenate(gb).reshape(1, 256).astype(F32)


def _compress_kernel(seg_ref, w1_ref, pos_ref, w1f_ref, w2_ref, o_ref):
    n_seg = seg_ref.shape[1]
    ul = _dot(seg_ref[0], w1_ref[0])
    u = ul[:, :CMP_HIDDEN]
    lnext = pltpu.roll(ul[:, CMP_HIDDEN:], n_seg - 1, 0)
    cpos = _dot(pos_ref[0], w1f_ref[0])[0:1, :]
    pre = u + lnext + cpos
    act = 0.5 * pre * (1.0 + jnp.tanh(math.sqrt(2.0 / math.pi) * (pre + 0.044715 * pre * pre * pre)))
    o_ref[0, 0, 0:CMP_PAD, :] = jnp.zeros((CMP_PAD, C_DIM), BF16)
    o_ref[0, 0, CMP_PAD:CMP_PAD + n_seg, :] = _dot(act.astype(BF16), w2_ref[0]).astype(BF16)


def _compress(kc, cmp_pos, cmp_w1, cmp_w2, B, S):
    n_seg = S // CMP_STRIDE
    half = CMP_STRIDE * C_DIM
    seg = kc.reshape(4 * B, n_seg, half)
    w1 = cmp_w1.astype(BF16)
    w1_ul = jnp.concatenate([w1[:, :half], w1[:, half:]], axis=2)
    pos = jnp.broadcast_to(cmp_pos.reshape(2, 1, CMP_BLOCK * C_DIM), (2, 8, CMP_BLOCK * C_DIM)).astype(BF16)
    out = pl.pallas_call(
        _compress_kernel,
        grid=(4, B),
        in_specs=[pl.BlockSpec((1, n_seg, half), lambda i, b: (i * B + b, 0, 0)),
                  pl.BlockSpec((1, half, 2 * CMP_HIDDEN), lambda i, b: (i // 2, 0, 0)),
                  pl.BlockSpec((1, 8, CMP_BLOCK * C_DIM), lambda i, b: (i // 2, 0, 0)),
                  pl.BlockSpec((1, CMP_BLOCK * C_DIM, CMP_HIDDEN), lambda i, b: (i // 2, 0, 0)),
                  pl.BlockSpec((1, CMP_HIDDEN, C_DIM), lambda i, b: (i // 2, 0, 0))],
        out_specs=pl.BlockSpec((1, 1, CMP_PAD + n_seg, C_DIM), lambda i, b: (i, b, 0, 0)),
        out_shape=jax.ShapeDtypeStruct((4, B, CMP_PAD + n_seg, C_DIM), BF16),
        compiler_params=_params(("parallel", "parallel")),
        name="compress",
    )(seg, w1_ul, pos, w1, cmp_w2.astype(BF16))
    return out


def _overlap_np(n_cmp_pad, n_slc):
    i = np.arange(n_cmp_pad)[:, None] - CMP_PAD
    m = np.arange(n_slc)[None, :]
    start = i * CMP_STRIDE
    ov = (start < (m + 1) * SLC_BLOCK) & (start + CMP_BLOCK - 1 >= m * SLC_BLOCK) & (i >= 0)
    return ov.astype(np.float32)


def _flash_update(s_heads, vext, m_ref, acc_ref, p_ref):
    W = s_heads[0].shape[1]
    nw = W // 128
    alphas = []
    for h, s in enumerate(s_heads):
        rs = slice(h * TQ, (h + 1) * TQ)
        m_prev = m_ref[rs, :]
        smax = functools.reduce(jnp.maximum, [s[:, i * 128:(i + 1) * 128] for i in range(nw)])
        m_new = jnp.maximum(m_prev, jnp.max(smax, axis=1, keepdims=True))
        alpha = jnp.exp2(m_prev - m_new)
        p = jnp.exp2(s - (m_new if nw == 1 else jnp.concatenate([m_new] * nw, axis=1)))
        m_ref[rs, :] = m_new
        p_ref[rs, 0:W] = p.astype(BF16)
        alphas.append(alpha)
    a = jnp.concatenate(alphas, axis=0)
    acc_ref[...] = jnp.concatenate([a, a], axis=1) * acc_ref[...] + _dot(p_ref[:, 0:W], vext)


def _nsa_kernel(q_ref, ks_ref, vs_ref, kw_ref, vw_ref, kc_ref, vc_ref, ov_ref, mk_ref,
                tsel_ref, twin_ref, tcmp_ref, g_ref, o_ref,
                m_ref, acc_ref, p_ref, pw_ref, sa_ref, sb_ref, oc_ref, ow_ref, *, n_sel_tab, top_n):
    qb = pl.program_id(2)
    H = C_HPG
    NB = ov_ref.shape[1]
    q_all = jnp.concatenate([q_ref[:, h * C_DIM:(h + 1) * C_DIM] for h in range(H)], axis=0)
    heads = lambda x: [x[h * TQ:(h + 1) * TQ] for h in range(H)]

    def reset():
        m_ref[...] = jnp.full(m_ref.shape, NEG, F32)
        acc_ref[...] = jnp.zeros(acc_ref.shape, F32)

    def softmax_rows(s):
        nw = s.shape[1] // 128
        smax = functools.reduce(jnp.maximum, [s[:, i * 128:(i + 1) * 128] for i in range(nw)])
        m = jnp.broadcast_to(jnp.max(smax, axis=1, keepdims=True), (TQ, 128))
        return jnp.exp2(s - jnp.concatenate([m] * nw, axis=1)), m

    t0 = qb // CMP_CLASSES + jnp.where(qb % CMP_CLASSES >= CMP_SPLIT, 1, 0)
    n_ct = kc_ref.shape[2] // TQ
    wc = n_ct * TQ
    tile_kind = [3] + [jnp.where(t == t0, 0, jnp.where(t == t0 + 1, 1, jnp.where(t < t0, 2, 3)))
                       for t in range(1, n_ct)]
    s_c = _dot_nt(q_all, kc_ref[0, 0])
    vext_c = jnp.concatenate([vc_ref[0, 0], ov_ref[...]], axis=1)
    inv_c = []
    for h in range(H):
        bias = jnp.concatenate([tcmp_ref[0, tile_kind[t], h] for t in range(n_ct)], axis=1)
        p, m = softmax_rows(s_c[h * TQ:(h + 1) * TQ] + bias)
        l = jnp.sum(p, axis=1, keepdims=True)
        inv_c.append(jnp.where(m > 0.5 * NEG, 1.0 / l, 0.0))
        p_ref[h * TQ:(h + 1) * TQ, 0:wc] = p.astype(BF16)
    res_c = _dot(p_ref[:, 0:wc], vext_c)
    imp = None
    for h in range(H):
        r = res_c[h * TQ:(h + 1) * TQ] * jnp.concatenate([inv_c[h], inv_c[h]], axis=1)
        oc_ref[h * TQ:(h + 1) * TQ, :] = r[:, 0:C_DIM]
        imp = r[:, C_DIM:] if imp is None else imp + r[:, C_DIM:]

    n_wt = WIN // TQ + 1
    ww = n_wt * TQ
    st = jnp.maximum(qb - (n_wt - 1), 0)
    r0w = pl.multiple_of(st * TQ, TQ)
    s_w = _dot_nt(q_all, kw_ref[pl.ds(r0w, ww), :])
    vext_w = jnp.concatenate([vw_ref[pl.ds(r0w, ww), :], jnp.ones((ww, C_DIM), BF16)], axis=1)
    widx = [jnp.maximum(qb - (st + c) + 1, 0) for c in range(n_wt)]
    for h in range(H):
        bias = jnp.concatenate([twin_ref[i, h] for i in widx], axis=1)
        p, _ = softmax_rows(s_w[h * TQ:(h + 1) * TQ] + bias)
        pw_ref[h * TQ:(h + 1) * TQ, :] = p.astype(BF16)
    res_w = _dot(pw_ref[...], vext_w)
    ow_ref[...] = res_w[:, 0:C_DIM] / res_w[:, C_DIM:]

    shift = SLC_BLOCK.bit_length() - 1
    qpos = qb * TQ + lax.broadcasted_iota(jnp.int32, (TQ, NB), 0)
    mblk = lax.broadcasted_iota(jnp.int32, (TQ, NB), 1)
    qblk = jnp.right_shift(qpos, shift)
    forced = (mblk == 0) | (mblk == qblk) | (mblk == qblk - 1)
    score = jnp.where(forced, 3e38, jnp.where(jnp.left_shift(mblk, shift) <= qpos, imp, NEG))
    score_t = score.T
    blk_t = lax.broadcasted_iota(jnp.int32, (NB, TQ), 0).astype(F32)
    sel_t = jnp.zeros((NB, TQ), F32)
    for _ in range(top_n):
        mx = jnp.max(score_t, axis=0, keepdims=True)
        idx = jnp.min(jnp.where(score_t == mx, blk_t, float(NB)), axis=0, keepdims=True)
        pick = blk_t == idx
        sel_t = jnp.where(pick, 1.0, sel_t)
        score_t = jnp.where(pick, -3e38, score_t)
    unsel = (1.0 - sel_t.T).astype(BF16)

    reset()
    NT = 4
    TK = NT * TQ
    q_aug = jnp.concatenate([q_all, jnp.concatenate([unsel] * H, axis=0)], axis=1)
    ones_k = jnp.ones((TK, C_DIM), BF16)
    n_steps = qb // NT + 1

    def sel_logits(kq, s_ref, near):
        kc = jnp.minimum(kq, n_steps - 1)
        r0 = pl.multiple_of(kc * TK, TK)
        k_aug = jnp.concatenate([ks_ref[pl.ds(r0, TK), :], mk_ref[pl.ds(r0, TK), :]], axis=1)
        s = _dot_nt(q_aug, k_aug)
        if not near:
            s_ref[...] = s
            return
        idx = [jnp.where(kq < n_steps, jnp.clip(qb - (NT * kc + c) + 1, 0, n_sel_tab - 1), 0)
               for c in range(NT)]
        for h in range(H):
            bias = jnp.concatenate([tsel_ref[i, h] for i in idx], axis=1)
            s_ref[h * TQ:(h + 1) * TQ, :] = s[h * TQ:(h + 1) * TQ] + bias

    def sel_consume(kq, s_ref):
        r0 = pl.multiple_of(jnp.minimum(kq, n_steps - 1) * TK, TK)
        vext = jnp.concatenate([vs_ref[pl.ds(r0, TK), :], ones_k], axis=1)
        _flash_update([s_ref[h * TQ:(h + 1) * TQ, :] for h in range(H)], vext, m_ref, acc_ref, p_ref)

    def sel_run(k_lo, count, near):
        @pl.when(count > 0)
        def _():
            sel_logits(k_lo, sa_ref, near)

        def body(j, carry):
            k = k_lo + 2 * j
            sel_logits(k + 1, sb_ref, near)
            sel_consume(k, sa_ref)
            sel_logits(k + 2, sa_ref, near)
            sel_consume(k + 1, sb_ref)
            return carry

        lax.fori_loop(0, count // 2, body, 0)

        @pl.when(count % 2 == 1)
        def _():
            sel_consume(k_lo + count - 1, sa_ref)

    n_far = jnp.maximum((qb + 1 - (n_sel_tab - 2)) // NT, 0)
    sel_run(0, n_far, near=False)
    sel_run(n_far, n_steps - n_far, near=True)
    g = g_ref[...]
    for h in range(H):
        rs = slice(h * TQ, (h + 1) * TQ)
        out_s = acc_ref[rs, 0:C_DIM] / acc_ref[rs, C_DIM:2 * C_DIM]
        o = (g[:, h:h + 1] * oc_ref[rs, :] + g[:, H + h:H + h + 1] * out_s
             + g[:, 2 * H + h:2 * H + h + 1] * ow_ref[rs, :])
        o_ref[:, h * C_DIM:(h + 1) * C_DIM] = o.astype(BF16)


def _nsa_attention(q, kv, kvc, gates, tsel, twin, tcmp, B, S):
    N = B * S
    QT = S // TQ
    n_slc = S // SLC_BLOCK
    n_cmp_pad = kvc.shape[2]
    NB = 128
    assert n_slc <= NB and n_cmp_pad % TQ == 0 and QT % 4 == 0
    ov = jnp.asarray(_overlap_np(n_cmp_pad, NB), BF16)
    mk = jnp.asarray(np.where(np.arange(S)[:, None] // SLC_BLOCK == np.arange(NB)[None, :], NEG, 0.0), BF16)
    n_sel_delta = tsel.shape[0]
    kvspec = lambda c: pl.BlockSpec((S, C_DIM), lambda b, g, t: (b, c + g))
    cspec = lambda kvi: pl.BlockSpec((1, 1, n_cmp_pad, C_DIM), lambda b, g, t: (kvi * 2 + g, b, 0, 0))
    return pl.pallas_call(
        functools.partial(_nsa_kernel, n_sel_tab=n_sel_delta, top_n=min(SLC_TOP_N, n_slc)),
        grid=(B, C_GROUPS, QT),
        in_specs=[pl.BlockSpec((TQ, C_HPG * C_DIM), lambda b, g, t: (b * QT + t, g)),
                  kvspec(0), kvspec(2), kvspec(4), kvspec(6),
                  cspec(0), cspec(1),
                  pl.BlockSpec((n_cmp_pad, NB), lambda b, g, t: (0, 0)),
                  pl.BlockSpec((S, NB), lambda b, g, t: (0, 0)),
                  pl.BlockSpec((n_sel_delta, C_HPG, TQ, TQ), lambda b, g, t: (0, g, 0, 0)),
                  pl.BlockSpec((twin.shape[0], C_HPG, TQ, TQ), lambda b, g, t: (0, g, 0, 0)),
                  pl.BlockSpec((1, 4, C_HPG, TQ, TQ), lambda b, g, t: (t % CMP_CLASSES, 0, g, 0, 0)),
                  pl.BlockSpec((TQ, 128), lambda b, g, t: (b * QT + t, g))],
        out_specs=pl.BlockSpec((TQ, C_HPG * C_DIM), lambda b, g, t: (b * QT + t, g)),
        out_shape=jax.ShapeDtypeStruct((N, C_W), BF16),
        scratch_shapes=[pltpu.VMEM((C_HPG * TQ, 128), F32),
                        pltpu.VMEM((C_HPG * TQ, 2 * C_DIM), F32),
                        pltpu.VMEM((C_HPG * TQ, max(n_cmp_pad, 4 * TQ)), BF16),
                        pltpu.VMEM((C_HPG * TQ, WIN + TQ), BF16),
                        pltpu.VMEM((C_HPG * TQ, 4 * TQ), F32), pltpu.VMEM((C_HPG * TQ, 4 * TQ), F32),
                        pltpu.VMEM((C_HPG * TQ, C_DIM), F32), pltpu.VMEM((C_HPG * TQ, C_DIM), F32)],
        compiler_params=_params(("parallel", "parallel", "arbitrary")),
        name="nsa",
    )(q, kv, kv, kv, kv, kvc, kvc, ov, mk, tsel, twin, tcmp, gates)


def _layer_ab(h2, B, S, w_in, gate_b, conv_w, head_g, w_out, dil_tab):
    w_pad = jnp.pad(w_in, ((0, 0), (0, AB_PAD - AB_PROJ))).astype(BF16)
    gb_pad = jnp.pad(gate_b, (0, 128 - gate_b.shape[0])).reshape(1, 128).astype(F32)
    (aq, ak, av, aq4, ak4, av4, aq16, ak16, av16,
     bq, bk, bv, bo, gates) = _ab_proj(h2, w_pad, conv_w.astype(F32), gb_pad, S)
    ya = _dilated_attention({1: (aq, ak, av), 4: (aq4, ak4, av4), 16: (aq16, ak16, av16)}, dil_tab, B, S)
    yb = _mlstm(bq, bk, bv, gates, bo, head_g, B, S)
    wo = w_out.astype(BF16)
    return [ya, yb], [A_PATTERNS[-1][1], 1], [wo[:A_W], wo[A_W:]]


def _layer_c(h2, B, S, w_in, gate_b, cmp_pos, cmp_w1, cmp_w2, w_out, tsel, twin, tcmp):
    w_pad, gb_pad = _c_weights(w_in, gate_b)
    q, kc, kv, gates = _c_proj(h2, w_pad, gb_pad)
    kvc = _compress(kc, cmp_pos, cmp_w1, cmp_w2, B, S)
    out = _nsa_attention(q, kv, kvc, gates, tsel, twin, tcmp, B, S)
    return [out], [1], [w_out.astype(BF16)]


def kernel(x, rel_bias, ln_g, ln_b, ab_w_in, ab_gate_b, ab_conv, ab_head_norm, ab_w_out,
           c_w_in, c_gate_b, c_cmp_pos, c_cmp_w1, c_cmp_w2, c_w_out,
           moe_wr_g, moe_br_g, moe_wr_e, moe_br_e, moe_w_gate, moe_w_up, moe_w_down):
    B, S, D = x.shape
    assert D == D_MODEL and S % (TM) == 0 and S % (16 * A_BLOCK) == 0
    _check_cmp_windows(S)
    dil_tab = _bias_tables(rel_bias, _dilated_idx(), shift=False)
    tsel, twin, tcmp = _nsa_tables(rel_bias)
    h = x.reshape(B * S, D)
    for layer in range(DEPTH):
        j = layer // 2
        if layer % 2 == 0:
            ys, lays, ws = _layer_ab(h, B, S, ab_w_in[j], ab_gate_b[j], ab_conv[j], ab_head_norm[j],
                                     ab_w_out[j], dil_tab)
        else:
            ys, lays, ws = _layer_c(h, B, S, c_w_in[j], c_gate_b[j], c_cmp_pos[j], c_cmp_w1[j],
                                    c_cmp_w2[j], c_w_out[j], tsel, twin, tcmp)
        wr, br = _router_weights(moe_wr_g[layer], moe_br_g[layer], moe_wr_e[layer], moe_br_e[layer])
        parts = []
        rows = (B * S) // MOE_CHUNKS
        for c in range(MOE_CHUNKS):
            hx, cnt = _out_ln_route(ys, lays, ws, h, ln_g[layer, 0], ln_b[layer, 0], wr, br, c * rows, rows)
            parts.append(_moe(hx, cnt, layer, moe_w_gate, moe_w_up, moe_w_down,
                              ln_g[layer, 1], ln_b[layer, 1]))
        h = jnp.concatenate(parts, axis=0)
    return h.reshape(B, S, D)
```

```python
import functools
import math

import numpy as np
import jax
import jax.numpy as jnp
from jax import lax
from jax.experimental import pallas as pl
from jax.experimental.pallas import tpu as pltpu

F32 = jnp.float32
BF16 = jnp.bfloat16
NEG = -1e30
LOG2E = math.log2(math.e)
VMEM_LIMIT = 48 * 1024 * 1024

D_MODEL = 1024
DEPTH = 2
ALPHA = (2.0 * DEPTH) ** 0.25
LN_EPS = 1e-5
REL_BUCKETS = 32
REL_MAX_DIST = 2048

A_HEADS, A_DIM, A_W = 8, 64, 512
A_PATTERNS = ((128, 1), (512, 4), (2048, 16))
A_BLOCK = 128
B_HEADS, B_DIM, B_W = 4, 128, 512
B_CHUNK = 128
B_CONV = 4
AB_PROJ = 3592
AB_PAD = 3712

C_HEADS, C_GROUPS, C_HPG, C_DIM, C_W = 8, 2, 4, 128, 1024
CMP_BLOCK, CMP_STRIDE, CMP_HIDDEN = 32, 16, 256
SLC_BLOCK, SLC_TOP_N, WIN = 64, 16, 512
C_PROJ = 2584
TQ = 128
CMP_PAD = 128

N_GROUPS, EPG, N_EXPERTS, E_HID = 4, 4, 16, 512
N_BUCKETS = N_GROUPS * 6
TM = 512
TM_MOE = 256


def _dot(a, b):
    return jnp.dot(a, b, preferred_element_type=F32)


def _dot_nt(a, b):
    return lax.dot_general(a, b, (((1,), (1,)), ((), ())), preferred_element_type=F32)


def _params(sem):
    return pltpu.CompilerParams(dimension_semantics=sem, vmem_limit_bytes=VMEM_LIMIT)


def _bucket_np(n):
    n = np.maximum(n, 0)
    exact = REL_BUCKETS // 2
    nf = np.maximum(n, 1).astype(np.float64)
    large = exact + (np.log(nf / exact) / math.log(REL_MAX_DIST / exact)
                     * (REL_BUCKETS - exact)).astype(np.int64)
    return np.where(n < exact, n, np.minimum(large, REL_BUCKETS - 1)).astype(np.int32)


def _bias_tab_kernel(tab_ref, idx_ref, out_ref, *, shift, scale):
    R = idx_ref.shape[1]
    RC = 32

    def body(i, carry):
        r0 = pl.multiple_of(i * RC, RC)
        idx = idx_ref[0, pl.ds(r0, RC), :]
        for h in range(8):
            base = tab_ref[REL_BUCKETS - 1, h] if shift else 0.0
            val = jnp.full(idx.shape, (tab_ref[0, h] - base) * scale, F32)
            for b in range(1, REL_BUCKETS):
                val = jnp.where(idx == b, (tab_ref[b, h] - base) * scale, val)
            out_ref[0, h, pl.ds(r0, RC), :] = jnp.where(idx < 0, NEG, val)
        return carry

    lax.fori_loop(0, R // RC, body, 0)


def _bias_tables(rel_bias, idx_np, shift, scale=1.0):
    T, R, C = idx_np.shape
    return pl.pallas_call(
        functools.partial(_bias_tab_kernel, shift=shift, scale=scale),
        grid=(T,),
        in_specs=[pl.BlockSpec(memory_space=pltpu.SMEM),
                  pl.BlockSpec((1, R, C), lambda t: (t, 0, 0))],
        out_specs=pl.BlockSpec((1, 8, R, C), lambda t: (t, 0, 0, 0)),
        out_shape=jax.ShapeDtypeStruct((T, 8, R, C), F32),
        compiler_params=_params(("parallel",)),
        name="bias_tables",
    )(rel_bias.astype(F32), jnp.asarray(idx_np))


def _dilated_idx():
    qi = np.arange(A_BLOCK)[:, None]
    ki = np.arange(2 * A_BLOCK)[None, :]
    j = qi + A_BLOCK - ki
    out = []
    for window, dilation in A_PATTERNS:
        nk = window // dilation
        valid = (j >= 0) & (j <= nk)
        out.append(np.where(valid, _bucket_np(np.maximum(j, 0) * dilation), -1))
    return np.stack(out).astype(np.int32)


def _sel_idx():
    a = np.arange(TQ)[:, None]
    c = np.arange(TQ)[None, :]
    n_delta = -(-(_far_dist() + TQ) // TQ)
    out = []
    for delta in range(-1, n_delta + 1):
        dist = delta * TQ + a - c
        out.append(np.where(dist >= 0, _bucket_np(dist), -1))
    return np.stack(out).astype(np.int32)


def _far_dist():
    n = np.arange(0, 4 * REL_MAX_DIST)
    b = _bucket_np(n)
    return int(np.max(n[b < REL_BUCKETS - 1])) + 1


def _win_idx():
    a = np.arange(TQ)[:, None]
    c = np.arange(TQ)[None, :]
    out = []
    for delta in range(-1, WIN // TQ + 1):
        dist = delta * TQ + a - c
        out.append(np.where((dist >= 0) & (dist < WIN), _bucket_np(dist), -1))
    return np.stack(out).astype(np.int32)


CMP_PER_TILE = TQ // CMP_STRIDE
CMP_CLASSES = TQ // CMP_PER_TILE
CMP_SPLIT = 13


def _cmp_window_start(qb):
    return qb // CMP_CLASSES + (1 if qb % CMP_CLASSES >= CMP_SPLIT else 0)


def _cmp_idx():
    a = np.arange(TQ)[:, None]
    c = np.arange(TQ)[None, :]
    out = []
    for r in range(CMP_CLASSES):
        qb = CMP_CLASSES + r
        i0 = _cmp_window_start(qb) * TQ - CMP_PAD
        for half in range(2):
            dist = qb * TQ + a - ((i0 + half * TQ + c) * CMP_STRIDE + CMP_BLOCK - 1)
            out.append(np.where(dist >= 0, _bucket_np(dist), -1))
        out.append(np.full((TQ, TQ), REL_BUCKETS - 1))
        out.append(np.full((TQ, TQ), -1))
    return np.stack(out).astype(np.int32)


def _nsa_tables(rel_bias):
    tsel = _bias_tables(rel_bias, _sel_idx(), shift=True, scale=LOG2E)
    twin = _bias_tables(rel_bias, _win_idx(), shift=False, scale=LOG2E)
    tcmp = _bias_tables(rel_bias, _cmp_idx(), shift=True, scale=LOG2E)
    return tsel, twin, tcmp.reshape(CMP_CLASSES, 4, 8, TQ, TQ)


def _check_cmp_windows(S):
    far = _far_dist()
    for qb in range(S // TQ):
        i0 = _cmp_window_start(qb) * TQ - CMP_PAD
        s0 = qb * TQ
        assert s0 - ((i0 - 1) * CMP_STRIDE + CMP_BLOCK - 1) >= far
        assert s0 + TQ - 1 - ((i0 + 2 * TQ) * CMP_STRIDE + CMP_BLOCK - 1) < 0


def _residue_col(d, r):
    return (r % 4) * 4 + r // 4 if d == 16 else r


def _lane_chunks_store(ref3, val):
    for c in range(ref3.shape[0]):
        ref3[c] = val[:, c * 128:(c + 1) * 128]


def _to_residue_layout(src3_ref, dst, d, col_of=_residue_col):
    nc, rows, _ = src3_ref.shape
    for r in range(d):
        cb = col_of(d, r)
        for c in range(nc):
            col = (cb * nc + c) * 128
            dst(slice(col, col + 128), src3_ref[c, pl.ds(r, rows // d, stride=d), :])


def _ab_proj_kernel(x_ref, xh_ref, w_ref, cw_ref, gb_ref,
                    aq_ref, ak_ref, av_ref, aq4_ref, ak4_ref, av4_ref, aq16_ref, ak16_ref, av16_ref,
                    bq_ref, bk_ref, bv_ref, bo_ref, g_ref,
                    pre_ref, tmp_ref, *, tiles_per_seq):
    i = pl.program_id(0)
    tm = x_ref.shape[0]
    xb = x_ref[...].astype(BF16)
    for c, scale, outs in ((0, A_DIM ** -0.5, (aq_ref, aq4_ref, aq16_ref)),
                           (1, 1.0, (ak_ref, ak4_ref, ak16_ref)),
                           (2, 1.0, (av_ref, av4_ref, av16_ref))):
        val = _dot(xb, w_ref[:, c * A_W:(c + 1) * A_W]) * scale
        _lane_chunks_store(tmp_ref, val)
        outs[0][...] = val.astype(BF16)
        for d, o_ref in ((4, outs[1]), (16, outs[2])):
            def put(cols, piece, o_ref=o_ref):
                o_ref[:, cols] = piece.astype(BF16)
            _to_residue_layout(tmp_ref, put, d)
    bv_ref[...] = _dot(xb, w_ref[:, 2560:3072]).astype(BF16)
    bo_ref[...] = _dot(xb, w_ref[:, 3072:3584])
    g_ref[...] = _dot(xb, w_ref[:, 3584:AB_PAD]) + gb_ref[...]
    halo = _dot(xh_ref[...].astype(BF16), w_ref[:, 1536:2560])
    halo = jnp.where(i % tiles_per_seq == 0, 0.0, halo)
    pre_ref[0:8, :] = halo
    pre_ref[8:8 + tm, :] = _dot(xb, w_ref[:, 1536:2560])
    y = pre_ref[8:8 + tm, :] * cw_ref[B_CONV - 1:B_CONV, :]
    for k in range(B_CONV - 1):
        s = B_CONV - 1 - k
        y = y + pre_ref[8 - s:8 - s + tm, :] * cw_ref[k:k + 1, :]
    y = y / (1.0 + jnp.exp(-y))
    bq_ref[...] = (y[:, :B_W] * (B_DIM ** -0.5)).astype(BF16)
    bk_ref[...] = y[:, B_W:].astype(BF16)


def _ab_proj(x2, w_pad, conv_w, gate_b_pad, S):
    N = x2.shape[0]
    tm = TM
    tps = S // tm
    row = lambda i: (i, 0)
    fix = lambda i: (0, 0)
    lay = lambda d: [jax.ShapeDtypeStruct((N // d, d * A_W), BF16)] * 3
    lay_spec = lambda d: [pl.BlockSpec((tm // d, d * A_W), row)] * 3
    outs = lay(1) + lay(4) + lay(16) + [jax.ShapeDtypeStruct((N, 512), BF16)] * 3 + [
        jax.ShapeDtypeStruct((N, 512), F32), jax.ShapeDtypeStruct((N, 128), F32)]
    o_specs = (lay_spec(1) + lay_spec(4) + lay_spec(16) + [pl.BlockSpec((tm, 512), row)] * 4
               + [pl.BlockSpec((tm, 128), row)])
    return pl.pallas_call(
        functools.partial(_ab_proj_kernel, tiles_per_seq=tps),
        grid=(N // tm,),
        in_specs=[pl.BlockSpec((tm, D_MODEL), row),
                  pl.BlockSpec((8, D_MODEL), lambda i: (jnp.maximum(i * (tm // 8) - 1, 0), 0)),
                  pl.BlockSpec((D_MODEL, AB_PAD), fix),
                  pl.BlockSpec((B_CONV, 2 * B_W), fix),
                  pl.BlockSpec((1, 128), fix)],
        out_specs=o_specs,
        out_shape=outs,
        scratch_shapes=[pltpu.VMEM((tm + 8, 2 * B_W), F32), pltpu.VMEM((A_W // 128, tm, 128), F32)],
        compiler_params=_params(("parallel",)),
        name="ab_proj",
    )(x2, x2, w_pad, conv_w, gate_b_pad)


def _dilated_kernel(*refs, has_prev, is_last):
    if has_prev:
        q_ref, kp_ref, kc_ref, vp_ref, vc_ref, tab_ref, op_ref, lp_ref = refs[:8]
        rest = refs[8:]
    else:
        q_ref, kp_ref, kc_ref, vp_ref, vc_ref, tab_ref = refs[:6]
        rest = refs[6:]
    outs, scratch = (rest, ()) if is_last else (rest[:2], rest[2:])
    o_ref = outs[0]
    n = pl.program_id(2)
    first = jnp.where(n == 0, NEG, 0.0)
    lane = lax.broadcasted_iota(jnp.int32, (A_BLOCK, 128), 1)
    keep_side = [jnp.where(lane < A_DIM, 1.0, 0.0).astype(BF16), jnp.where(lane < A_DIM, 0.0, 1.0).astype(BF16)]
    odd = jnp.bitwise_and(lane, 1) == 1
    ones_side = [jnp.where(odd, 0.0, 1.0).astype(BF16), jnp.where(odd, 1.0, 0.0).astype(BF16)]
    m_tile = jnp.zeros((A_BLOCK, 128), F32)
    l_tile = jnp.ones((A_BLOCK, 128), F32)
    unnorm = []
    for j in range(A_HEADS // 2):
        cs = slice(j * 128, (j + 1) * 128)
        q2 = q_ref[0, :, cs]
        kp, kc, vp, vc = kp_ref[0, :, cs], kc_ref[0, :, cs], vp_ref[0, :, cs], vc_ref[0, :, cs]
        k_st = jnp.concatenate([kp * keep_side[0], kc * keep_side[0],
                                kp * keep_side[1], kc * keep_side[1]], axis=0)
        s = _dot_nt(q2, k_st)
        p_parts = []
        for side in range(2):
            h = 2 * j + side
            c0 = 2 * side * A_BLOCK
            sp = s[:, c0:c0 + A_BLOCK] + tab_ref[0, h, :, 0:A_BLOCK] + first
            sc = s[:, c0 + A_BLOCK:c0 + 2 * A_BLOCK] + tab_ref[0, h, :, A_BLOCK:2 * A_BLOCK]
            m = jnp.max(jnp.maximum(sp, sc), axis=1, keepdims=True)
            p_parts += [jnp.exp(sp - m).astype(BF16), jnp.exp(sc - m).astype(BF16)]
            m_tile = jnp.where(lane == A_DIM + h, m, m_tile)
        v_st = jnp.concatenate(
            [jnp.concatenate([v * keep_side[side], ones_side[side]], axis=1)
             for side in range(2) for v in (vp, vc)], axis=0)
        r = _dot(jnp.concatenate(p_parts, axis=1), v_st)
        unnorm.append(r[:, 0:128])
        pair = (lane == A_DIM + 2 * j) | (lane == A_DIM + 2 * j + 1)
        l_tile = jnp.where(pair, r[:, 128:256], l_tile)
    stat = (lane >= A_DIM) & (lane < A_DIM + A_HEADS)
    lse = m_tile + jnp.log(l_tile)
    if has_prev:
        lp = lp_ref[0]
        mm = jnp.maximum(lp, lse)
        wp = jnp.exp(lp - mm)
        wc = jnp.exp(lse - mm)
        tot = wp + wc
        scale_prev = jnp.where(stat, wp / tot, 0.0)
        scale_cur = jnp.where(stat, wc / (tot * l_tile), 0.0)
        lse = mm + jnp.log(tot)
    else:
        scale_cur = jnp.where(stat, 1.0 / l_tile, 0.0)
    erow = lax.broadcasted_iota(jnp.int32, (128, A_W), 0)
    ecol = lax.broadcasted_iota(jnp.int32, (128, A_W), 1)
    expand = jnp.where(erow - A_DIM == jnp.right_shift(ecol, A_DIM.bit_length() - 1), 1.0, 0.0).astype(BF16)

    def spread(t):
        hi = t.astype(BF16)
        return _dot(hi, expand) + _dot((t - hi.astype(F32)).astype(BF16), expand)

    o = jnp.concatenate(unnorm, axis=1) * spread(scale_cur)
    if has_prev:
        o = o + op_ref[0] * spread(scale_prev)
    if is_last:
        o_ref[0] = o.astype(o_ref.dtype)
    else:
        o_scr, l_scr = scratch
        _lane_chunks_store(o_scr, o)
        l_scr[0] = jnp.where(stat, lse, 0.0)

        def put_o(cols, piece):
            o_ref[0, :, cols] = piece

        def put_l(cols, piece):
            outs[1][0, :, cols] = piece

        _to_residue_layout(o_scr, put_o, 4)
        _to_residue_layout(l_scr, put_l, 4)


def _dilated_call(q, k, v, tab, prev, pattern_idx, dilation, B, S, is_last):
    d = dilation
    L = S // d
    nb = L // A_BLOCK
    r3 = lambda a: a.reshape(B, L, a.shape[-1])
    cur = lambda b, r, n: (b, n, r)
    prv = lambda b, r, n: (b, jnp.maximum(n - 1, 0), r)
    blk = pl.BlockSpec((1, A_BLOCK, A_W), cur)
    in_specs = [blk, pl.BlockSpec((1, A_BLOCK, A_W), prv), blk,
                pl.BlockSpec((1, A_BLOCK, A_W), prv), blk,
                pl.BlockSpec((1, 8, A_BLOCK, 2 * A_BLOCK), lambda b, r, n: (pattern_idx, 0, 0, 0))]
    args = [r3(q), r3(k), r3(k), r3(v), r3(v), tab]
    has_prev = prev is not None
    if has_prev:
        in_specs += [blk, pl.BlockSpec((1, A_BLOCK, 128), cur)]
        args += [r3(prev[0]), r3(prev[1])]
    scratch = []
    if is_last:
        out_shape = [jax.ShapeDtypeStruct((B, L, d * A_W), BF16)]
        out_specs = [blk]
    else:
        d2, rows = 4 * d, A_BLOCK // 4
        nxt = lambda b, r, n: (b, n, r)
        out_shape = [jax.ShapeDtypeStruct((B, S // d2, d2 * A_W), F32),
                     jax.ShapeDtypeStruct((B, S // d2, d2 * 128), F32)]
        out_specs = [pl.BlockSpec((1, rows, 4 * A_W), nxt), pl.BlockSpec((1, rows, 4 * 128), nxt)]
        scratch = [pltpu.VMEM((A_W // 128, A_BLOCK, 128), F32), pltpu.VMEM((1, A_BLOCK, 128), F32)]
    res = pl.pallas_call(
        functools.partial(_dilated_kernel, has_prev=has_prev, is_last=is_last),
        grid=(B, d, nb),
        in_specs=in_specs, out_specs=out_specs, out_shape=out_shape, scratch_shapes=scratch,
        compiler_params=_params(("parallel", "parallel", "arbitrary")),
        name="dilated_d%d" % d,
    )(*args)
    return [r.reshape(-1, r.shape[-1]) for r in res]


def _dilated_attention(qkv_by_dilation, tab, B, S):
    prev = None
    for p, (window, d) in enumerate(A_PATTERNS):
        assert window // d == A_BLOCK and S % (d * A_BLOCK) == 0
        assert p == 0 or d == 4 * A_PATTERNS[p - 1][1]
        last = p == len(A_PATTERNS) - 1
        q, k, v = qkv_by_dilation[d]
        prev = _dilated_call(q, k, v, tab, prev, p, d, B, S, last)
    return prev[0]


def _split3(x):
    hi = x.astype(BF16)
    r = x - hi.astype(F32)
    mid = r.astype(BF16)
    lo = (r - mid.astype(F32)).astype(BF16)
    return hi, mid, lo


def _mlstm_kernel(q_ref, k_ref, v_ref, g_ref, bo_ref, hg_ref, y_ref, c_ref, m_ref):
    L = B_CHUNK
    c = pl.program_id(1)

    @pl.when(c == 0)
    def _():
        c_ref[...] = jnp.zeros_like(c_ref)
        m_ref[...] = jnp.zeros_like(m_ref)

    lane = lax.broadcasted_iota(jnp.int32, (L, 128), 1)
    row = lax.broadcasted_iota(jnp.int32, (L, L), 0)
    col = lax.broadcasted_iota(jnp.int32, (L, L), 1)
    tri = row >= col
    is_f = (lane >= B_HEADS) & (lane < 2 * B_HEADS)
    tril = jnp.where(tri, 1.0, 0.0).astype(BF16)
    ones = jnp.ones((L, B_DIM), BF16)
    bi = 0
    g = g_ref[bi]
    logf = jnp.minimum(g, 0.0) - jnp.log(1.0 + jnp.exp(-jnp.abs(g)))
    gl = jnp.where(is_f, logf, jnp.where(lane < B_HEADS, g, 0.0))
    hi, mid, lo = _split3(gl)
    cum = _dot(tril, hi) + _dot(tril, mid) + _dot(tril, lo)
    cum_t = cum.T
    gl_t = gl.T
    for h in range(B_HEADS):
        st = h
        cs = slice(h * B_DIM, (h + 1) * B_DIM)
        q = q_ref[bi, :, cs]
        k = k_ref[bi, :, cs]
        v_ext = jnp.concatenate([v_ref[bi, :, cs], ones], axis=1)
        b_col = cum[:, B_HEADS + h:B_HEADS + h + 1]
        b_row = cum_t[B_HEADS + h:B_HEADS + h + 1, :]
        i_col = gl[:, h:h + 1]
        i_row = gl_t[h:h + 1, :]
        m_prev = m_ref[st, 0:1, 0:1]
        dm = jnp.where(tri, b_col - b_row + i_row, NEG)
        inter = b_col + m_prev
        m_t = jnp.maximum(inter, jnp.max(dm, axis=1, keepdims=True))
        p = jnp.exp(dm - m_t)
        sqk = _dot_nt(q, k) * p
        sc = jnp.exp(inter - m_t)
        lhs = jnp.concatenate([(sc * q.astype(F32)).astype(BF16), sqk.astype(BF16)], axis=1)
        c_ext = c_ref[st]
        rhs = jnp.concatenate([c_ext.astype(BF16), v_ext], axis=0)
        res = _dot(lhs, rhs)
        num = res[:, :B_DIM]
        den = res[:, B_DIM:]
        hh = num / jnp.maximum(jnp.abs(den), jnp.exp(-m_t))
        b_last = b_col[L - 1:L, :]
        gk = b_last - b_col + i_col
        m_new = jnp.maximum(b_last + m_prev, jnp.max(gk, axis=0, keepdims=True))
        wk = jnp.exp(gk - m_new)
        decay = jnp.exp(b_last + m_prev - m_new)
        kw_t = (wk * k.astype(F32)).T.astype(BF16)
        c_ref[st] = decay * c_ext + _dot(kw_t, v_ext)
        m_ref[st] = jnp.broadcast_to(m_new, (8, 128))
        mu = jnp.mean(hh, axis=1, keepdims=True)
        xc = hh - mu
        var = jnp.mean(xc * xc, axis=1, keepdims=True)
        hn = xc * lax.rsqrt(var + LN_EPS) * hg_ref[:, cs]
        bo = bo_ref[bi, :, cs]
        y_ref[bi, :, cs] = (hn / (1.0 + jnp.exp(-bo))).astype(BF16)


def _mlstm(bq, bk, bv, gates, bo, head_g, B, S):
    nc = S // B_CHUNK
    r3 = lambda a: a.reshape(B, S, a.shape[-1])
    blk = pl.BlockSpec((1, B_CHUNK, B_W), lambda b, c: (b, c, 0))
    y = pl.pallas_call(
        _mlstm_kernel,
        grid=(B, nc),
        in_specs=[blk, blk, blk,
                  pl.BlockSpec((1, B_CHUNK, 128), lambda b, c: (b, c, 0)),
                  blk,
                  pl.BlockSpec((1, B_W), lambda b, c: (0, 0))],
        out_specs=blk,
        out_shape=jax.ShapeDtypeStruct((B, S, B_W), BF16),
        scratch_shapes=[pltpu.VMEM((B_HEADS, B_DIM, 2 * B_DIM), F32),
                        pltpu.VMEM((B_HEADS, 8, 128), F32)],
        compiler_params=_params(("parallel", "arbitrary")),
        name="mlstm",
    )(r3(bq), r3(bk), r3(bv), r3(gates), r3(bo), head_g.reshape(1, B_W).astype(F32))
    return y.reshape(B * S, B_W)


def _layer_norm(z, g, b):
    mu = jnp.mean(z, axis=1, keepdims=True)
    zc = z - mu
    var = jnp.mean(zc * zc, axis=1, keepdims=True)
    return zc * lax.rsqrt(var + LN_EPS) * g + b


def _route(logits, cnt_ref):
    tm = logits.shape[0]
    lt = logits.T
    col = lambda c: lt[c:c + 1, :]
    gl = [col(c) for c in range(N_GROUPS)]
    gmax = functools.reduce(jnp.maximum, gl)
    gsum = sum(jnp.exp(x - gmax) for x in gl)
    g_idx = jnp.full(gmax.shape, N_GROUPS - 1, jnp.int32)
    for c in range(N_GROUPS - 2, -1, -1):
        g_idx = jnp.where(gl[c] == gmax, c, g_idx)
    g_w = 1.0 / gsum
    el = []
    for k in range(EPG):
        x = col(N_GROUPS + (N_GROUPS - 1) * EPG + k)
        for g in range(N_GROUPS - 2, -1, -1):
            x = jnp.where(g_idx == g, col(N_GROUPS + g * EPG + k), x)
        el.append(x)
    v1 = functools.reduce(jnp.maximum, el)
    i1 = jnp.full(v1.shape, EPG - 1, jnp.int32)
    for k in range(EPG - 2, -1, -1):
        i1 = jnp.where(el[k] == v1, k, i1)
    el2 = [jnp.where(i1 == k, -jnp.inf, el[k]) for k in range(EPG)]
    v2 = functools.reduce(jnp.maximum, el2)
    i2 = jnp.full(v2.shape, EPG - 1, jnp.int32)
    for k in range(EPG - 2, -1, -1):
        i2 = jnp.where((el2[k] == v2) & (i1 != k), k, i2)
    t = jnp.exp(v2 - v1)
    w1 = g_w / (1.0 + t)
    w2 = w1 * t
    a = jnp.minimum(i1, i2)
    b = jnp.maximum(i1, i2)
    pair = jnp.where(a == 0, b - 1, jnp.where(a == 1, b + 1, 5))
    bucket = (g_idx * 6 + pair).astype(F32)
    w_lo = jnp.where(i1 < i2, w1, w2)
    w_hi = jnp.where(i1 < i2, w2, w1)
    sub = lax.broadcasted_iota(jnp.int32, (128, tm), 0)
    onehot_t = jnp.where(sub.astype(F32) == bucket, 1.0, 0.0)
    srow = lax.broadcasted_iota(jnp.int32, (tm, tm), 0)
    scol = lax.broadcasted_iota(jnp.int32, (tm, tm), 1)
    before = jnp.where(srow < scol, 1.0, 0.0).astype(BF16)
    oh = onehot_t.astype(BF16)
    carry = cnt_ref[...]
    prior = _dot(oh, before) + jnp.concatenate([carry] * (tm // 128), axis=1)
    rank = jnp.sum(onehot_t * prior, axis=0, keepdims=True)
    cnt_ref[...] = carry + _dot(oh, jnp.ones((tm, 128), BF16))
    out_t = jnp.where(sub == 0, bucket, jnp.where(sub == 1, w_lo, jnp.where(sub == 2, w_hi,
                      jnp.where(sub == 3, rank, 0.0))))
    return out_t.T


def _out_ln_route_kernel(*refs, n_in, layouts):
    y_refs = refs[:n_in]
    w_refs = refs[n_in:2 * n_in]
    x_ref, g_ref, b_ref, wrh_ref, wrl_ref, br_ref, h_ref, c_ref, cnt_ref = refs[2 * n_in:2 * n_in + 9]
    pos_refs = list(refs[2 * n_in + 9:])
    tm = x_ref.shape[0]

    @pl.when(pl.program_id(0) == 0)
    def _():
        cnt_ref[...] = jnp.zeros_like(cnt_ref)

    y = None
    for i in range(n_in):
        d = layouts[i]
        if d == 1:
            lhs = y_refs[i][...]
        else:
            s_ref = pos_refs.pop(0)
            nc = s_ref.shape[0]
            for r in range(d):
                cb = _residue_col(d, r)
                for c in range(nc):
                    col = (cb * nc + c) * 128
                    s_ref[c, pl.ds(r, tm // d, stride=d), :] = y_refs[i][:, col:col + 128].astype(F32)
            lhs = jnp.concatenate([s_ref[c] for c in range(nc)], axis=1).astype(BF16)
        t = _dot(lhs, w_refs[i][...])
        y = t if y is None else y + t
    hn = _layer_norm(ALPHA * x_ref[...] + y, g_ref[...], b_ref[...])
    h_ref[:, 0:D_MODEL] = hn
    hi = hn.astype(BF16)
    lo = (hn - hi.astype(F32)).astype(BF16)
    logits = (_dot_nt(hi, wrh_ref[...]) + _dot_nt(lo, wrh_ref[...]) + _dot_nt(hi, wrl_ref[...])
              + br_ref[...])
    h_ref[:, D_MODEL:D_MODEL + 128] = _route(logits, cnt_ref)
    c_ref[...] = cnt_ref[...]


def _out_ln_route(ys, layouts, ws, x2, ln_g, ln_b, wr, br):
    N = x2.shape[0]
    tm = TM
    row = lambda i: (i, 0)
    fix = lambda i: (0, 0)
    wr_hi = wr.astype(BF16)
    wr_lo = (wr - wr_hi.astype(F32)).astype(BF16)
    in_specs = ([pl.BlockSpec((tm // d, y.shape[1]), row) for y, d in zip(ys, layouts)]
                + [pl.BlockSpec(w.shape, fix) for w in ws]
                + [pl.BlockSpec((tm, D_MODEL), row),
                   pl.BlockSpec((1, D_MODEL), fix), pl.BlockSpec((1, D_MODEL), fix),
                   pl.BlockSpec((128, D_MODEL), fix), pl.BlockSpec((128, D_MODEL), fix),
                   pl.BlockSpec((1, 128), fix)])
    return pl.pallas_call(
        functools.partial(_out_ln_route_kernel, n_in=len(ys), layouts=tuple(layouts)),
        grid=(N // tm,),
        in_specs=in_specs,
        out_specs=[pl.BlockSpec((tm, D_MODEL + 128), row), pl.BlockSpec((128, 128), fix)],
        out_shape=[jax.ShapeDtypeStruct((N, D_MODEL + 128), F32), jax.ShapeDtypeStruct((128, 128), F32)],
        scratch_shapes=[pltpu.VMEM((128, 128), F32)] + [
            pltpu.VMEM((y.shape[1] // d // 128, tm, 128), F32) for y, d in zip(ys, layouts) if d != 1],
        compiler_params=_params(("arbitrary",)),
        name="out_ln_route",
    )(*ys, *ws, x2, ln_g.reshape(1, -1), ln_b.reshape(1, -1), wr_hi, wr_lo, br)


def _router_weights(wr_g, br_g, wr_e, br_e):
    we = wr_e.transpose(0, 2, 1).reshape(N_GROUPS * EPG, D_MODEL)
    w = jnp.concatenate([wr_g.T, we], axis=0)
    w = jnp.pad(w, ((0, 128 - w.shape[0]), (0, 0)))
    b = jnp.concatenate([br_g, br_e.reshape(-1)])
    b = jnp.pad(b, (0, 128 - b.shape[0])).reshape(1, 128)
    return w.astype(F32), b.astype(F32)


_PAIRS = ((0, 1), (0, 2), (0, 3), (1, 2), (1, 3), (2, 3))


def _moe_kernel(elo_ref, ehi_ref, chg_ref, nt_ref,
                x_ref, wgl_ref, wul_ref, wdl_ref, wgh_ref, wuh_ref, wdh_ref,
                g_ref, b_ref, o_ref, wg_s, wu_s, wd_s):
    t = pl.program_id(0)

    @pl.when(chg_ref[t] == 1)
    def _():
        wg_s[0] = wgl_ref[0, 0].astype(BF16)
        wu_s[0] = wul_ref[0, 0].astype(BF16)
        wd_s[0] = wdl_ref[0, 0].astype(BF16)
        wg_s[1] = wgh_ref[0, 0].astype(BF16)
        wu_s[1] = wuh_ref[0, 0].astype(BF16)
        wd_s[1] = wdh_ref[0, 0].astype(BF16)

    @pl.when(t < nt_ref[0])
    def _():
        x = x_ref[:, 0:D_MODEL]
        xb = x.astype(BF16)
        r = x_ref[:, D_MODEL:D_MODEL + 128]
        acc = None
        for e in range(2):
            a = _dot(xb, wg_s[e])
            u = _dot(xb, wu_s[e])
            hcur = (a / (1.0 + jnp.exp(-a))) * u * r[:, 1 + e:2 + e]
            y = _dot(hcur.astype(BF16), wd_s[e])
            acc = y if acc is None else acc + y
        o_ref[...] = _layer_norm(ALPHA * x + acc, g_ref[...], b_ref[...])

    @pl.when(t >= nt_ref[0])
    def _():
        o_ref[...] = jnp.zeros_like(o_ref)


def _moe(hx, cnt, layer, w_gate, w_up, w_down, ln_g, ln_b):
    N = hx.shape[0]
    tm = TM_MOE
    n_tiles = N // tm + N_BUCKETS
    n_pad = n_tiles * tm
    bucket = hx[:, D_MODEL].astype(jnp.int32)
    rank = hx[:, D_MODEL + 3].astype(jnp.int32)
    counts = cnt[:N_BUCKETS, 0].astype(jnp.int32)
    padded = ((counts + tm - 1) // tm) * tm
    ends = jnp.cumsum(padded)
    offs = ends - padded
    b2 = bucket.reshape(-1, 128)
    off2 = functools.reduce(lambda acc, b: jnp.where(b2 == b, offs[b], acc), range(N_BUCKETS),
                            jnp.zeros_like(b2))
    dest = off2.reshape(-1) + rank
    src = (jnp.arange(n_pad, dtype=jnp.int32) % N).at[dest].set(
        jnp.arange(N, dtype=jnp.int32), mode="promise_in_bounds", unique_indices=True)
    tile_start = jnp.arange(n_tiles, dtype=jnp.int32) * tm
    n_used = (ends[-1] // tm).astype(jnp.int32)
    tb = jnp.sum((tile_start[:, None] >= ends[None, :]).astype(jnp.int32), axis=1)
    tb_last = jnp.take(tb, jnp.maximum(n_used - 1, 0))
    tb = jnp.where(tile_start < ends[-1], tb, tb_last)
    pairs = jnp.asarray(_PAIRS, jnp.int32)
    elo = (tb // 6) * EPG + pairs[tb % 6, 0]
    ehi = (tb // 6) * EPG + pairs[tb % 6, 1]
    chg = jnp.concatenate([jnp.ones((1,), jnp.int32), (tb[1:] != tb[:-1]).astype(jnp.int32)])
    xs = hx.at[src].get(mode="promise_in_bounds")

    row = lambda t, *_: (t, 0)
    fix = lambda t, *_: (0, 0)
    wlo = lambda t, elo, ehi, chg, nt: (layer, elo[t], 0, 0)
    whi = lambda t, elo, ehi, chg, nt: (layer, ehi[t], 0, 0)
    up_spec = lambda im: pl.BlockSpec((1, 1, D_MODEL, E_HID), im)
    dn_spec = lambda im: pl.BlockSpec((1, 1, E_HID, D_MODEL), im)
    grid_spec = pltpu.PrefetchScalarGridSpec(
        num_scalar_prefetch=4,
        grid=(n_tiles,),
        in_specs=[pl.BlockSpec((tm, D_MODEL + 128), row),
                  up_spec(wlo), up_spec(wlo), dn_spec(wlo),
                  up_spec(whi), up_spec(whi), dn_spec(whi),
                  pl.BlockSpec((1, D_MODEL), fix), pl.BlockSpec((1, D_MODEL), fix)],
        out_specs=pl.BlockSpec((tm, D_MODEL), row),
        scratch_shapes=[pltpu.VMEM((2, D_MODEL, E_HID), BF16),
                        pltpu.VMEM((2, D_MODEL, E_HID), BF16),
                        pltpu.VMEM((2, E_HID, D_MODEL), BF16)])
    out_sorted = pl.pallas_call(
        _moe_kernel,
        grid_spec=grid_spec,
        out_shape=jax.ShapeDtypeStruct((n_pad, D_MODEL), F32),
        compiler_params=_params(("arbitrary",)),
        name="moe",
    )(elo, ehi, chg, n_used.reshape(1), xs, w_gate, w_up, w_down, w_gate, w_up, w_down,
      ln_g.reshape(1, -1), ln_b.reshape(1, -1))
    return out_sorted.at[dest].get(mode="promise_in_bounds", unique_indices=True)


def _c_proj_kernel(x_ref, w_ref, gb_ref, q_ref, kc_ref, kv_ref, g_ref, tmp_ref):
    xb = x_ref[...].astype(BF16)
    q_ref[...] = (_dot(xb, w_ref[:, 0:C_W]) * (C_DIM ** -0.5 * LOG2E)).astype(BF16)
    for i in range(4):
        tmp_ref[0] = _dot(xb, w_ref[:, C_W + i * 128:C_W + (i + 1) * 128])

        def put(cols, piece, i=i):
            kc_ref[i, :, cols] = piece.astype(BF16)

        _to_residue_layout(tmp_ref, put, CMP_STRIDE, col_of=lambda d, r: r)
    kv_ref[...] = _dot(xb, w_ref[:, C_W + 512:C_W + 1536]).astype(BF16)
    z = _dot(xb, w_ref[:, C_W + 1536:C_W + 1792]) + gb_ref[...]
    g_ref[...] = 1.0 / (1.0 + jnp.exp(-z))


def _c_proj(x2, w_pad, gb_pad):
    N = x2.shape[0]
    tm = TM
    row = lambda i: (i, 0)
    fix = lambda i: (0, 0)
    wcols = w_pad.shape[1]
    return pl.pallas_call(
        _c_proj_kernel,
        grid=(N // tm,),
        in_specs=[pl.BlockSpec((tm, D_MODEL), row), pl.BlockSpec((D_MODEL, wcols), fix),
                  pl.BlockSpec((1, 256), fix)],
        out_specs=[pl.BlockSpec((tm, C_W), row),
                   pl.BlockSpec((4, tm // CMP_STRIDE, CMP_STRIDE * C_DIM), lambda i: (0, i, 0)),
                   pl.BlockSpec((tm, 1024), row), pl.BlockSpec((tm, 256), row)],
        out_shape=[jax.ShapeDtypeStruct((N, C_W), BF16),
                   jax.ShapeDtypeStruct((4, N // CMP_STRIDE, CMP_STRIDE * C_DIM), BF16),
                   jax.ShapeDtypeStruct((N, 1024), BF16), jax.ShapeDtypeStruct((N, 256), F32)],
        scratch_shapes=[pltpu.VMEM((1, tm, 128), F32)],
        compiler_params=_params(("parallel",)),
        name="c_proj",
    )(x2, w_pad, gb_pad)


def _c_weights(w_in, gate_b):
    gcols = []
    gb = []
    for g in range(C_GROUPS):
        idx = [C_PROJ - 3 * C_HEADS + br * C_HEADS + g * C_HPG + j for br in range(3) for j in range(C_HPG)]
        gcols.append(jnp.pad(w_in[:, np.asarray(idx)], ((0, 0), (0, 128 - len(idx)))))
        gb.append(jnp.pad(gate_b[np.asarray(idx) - (C_PROJ - 3 * C_HEADS)], (0, 128 - len(idx))))
    w = jnp.concatenate([w_in[:, :C_PROJ - 3 * C_HEADS]] + gcols, axis=1).astype(BF16)
    return w, jnp.concatenate(gb).reshape(1, 256).astype(F32)


def _compress_kernel(seg_ref, w1_ref, pos_ref, w1f_ref, w2_ref, o_ref):
    n_seg = seg_ref.shape[1]
    ul = _dot(seg_ref[0], w1_ref[0])
    u = ul[:, :CMP_HIDDEN]
    lnext = pltpu.roll(ul[:, CMP_HIDDEN:], n_seg - 1, 0)
    cpos = _dot(pos_ref[0], w1f_ref[0])[0:1, :]
    pre = u + lnext + cpos
    act = 0.5 * pre * (1.0 + jnp.tanh(math.sqrt(2.0 / math.pi) * (pre + 0.044715 * pre * pre * pre)))
    o_ref[0, 0, 0:CMP_PAD, :] = jnp.zeros((CMP_PAD, C_DIM), BF16)
    o_ref[0, 0, CMP_PAD:CMP_PAD + n_seg, :] = _dot(act.astype(BF16), w2_ref[0]).astype(BF16)


def _compress(kc, cmp_pos, cmp_w1, cmp_w2, B, S):
    n_seg = S // CMP_STRIDE
    half = CMP_STRIDE * C_DIM
    seg = kc.reshape(4 * B, n_seg, half)
    w1 = cmp_w1.astype(BF16)
    w1_ul = jnp.concatenate([w1[:, :half], w1[:, half:]], axis=2)
    pos = jnp.broadcast_to(cmp_pos.reshape(2, 1, CMP_BLOCK * C_DIM), (2, 8, CMP_BLOCK * C_DIM)).astype(BF16)
    out = pl.pallas_call(
        _compress_kernel,
        grid=(4, B),
        in_specs=[pl.BlockSpec((1, n_seg, half), lambda i, b: (i * B + b, 0, 0)),
                  pl.BlockSpec((1, half, 2 * CMP_HIDDEN), lambda i, b: (i // 2, 0, 0)),
                  pl.BlockSpec((1, 8, CMP_BLOCK * C_DIM), lambda i, b: (i // 2, 0, 0)),
                  pl.BlockSpec((1, CMP_BLOCK * C_DIM, CMP_HIDDEN), lambda i, b: (i // 2, 0, 0)),
                  pl.BlockSpec((1, CMP_HIDDEN, C_DIM), lambda i, b: (i // 2, 0, 0))],
        out_specs=pl.BlockSpec((1, 1, CMP_PAD + n_seg, C_DIM), lambda i, b: (i, b, 0, 0)),
        out_shape=jax.ShapeDtypeStruct((4, B, CMP_PAD + n_seg, C_DIM), BF16),
        compiler_params=_params(("parallel", "parallel")),
        name="compress",
    )(seg, w1_ul, pos, w1, cmp_w2.astype(BF16))
    return out


def _overlap_np(n_cmp_pad, n_slc):
    i = np.arange(n_cmp_pad)[:, None] - CMP_PAD
    m = np.arange(n_slc)[None, :]
    start = i * CMP_STRIDE
    ov = (start < (m + 1) * SLC_BLOCK) & (start + CMP_BLOCK - 1 >= m * SLC_BLOCK) & (i >= 0)
    return ov.astype(np.float32)


def _nsa_kernel(q_ref, ks_ref, vs_ref, kw_ref, vw_ref, kc_ref, vc_ref, ov_ref, mk_ref,
                tsel_ref, twin_ref, tcmp_ref, g_ref, o_ref,
                m_ref, acc_ref, p_ref, a_ref, pc_ref, pw_ref, sa_ref, sb_ref, oc_ref, ow_ref,
                *, n_sel_tab, top_n):
    qb = pl.program_id(2)
    H = C_HPG
    NB = ov_ref.shape[1]
    q_all = jnp.concatenate([q_ref[:, h * C_DIM:(h + 1) * C_DIM] for h in range(H)], axis=0)
    heads = lambda x: [x[h * TQ:(h + 1) * TQ] for h in range(H)]

    def reset():
        m_ref[...] = jnp.full(m_ref.shape, NEG, F32)
        acc_ref[...] = jnp.zeros(acc_ref.shape, F32)

    def softmax_rows(s):
        nw = s.shape[1] // 128
        smax = functools.reduce(jnp.maximum, [s[:, i * 128:(i + 1) * 128] for i in range(nw)])
        m = jnp.broadcast_to(jnp.max(smax, axis=1, keepdims=True), (TQ, 128))
        return jnp.exp2(s - jnp.concatenate([m] * nw, axis=1)), m

    t0 = qb // CMP_CLASSES + jnp.where(qb % CMP_CLASSES >= CMP_SPLIT, 1, 0)
    n_ct = kc_ref.shape[2] // TQ
    wc = n_ct * TQ
    tile_kind = [3] + [jnp.where(t == t0, 0, jnp.where(t == t0 + 1, 1, jnp.where(t < t0, 2, 3)))
                       for t in range(1, n_ct)]
    s_c = _dot_nt(q_all, kc_ref[0, 0])
    vext_c = jnp.concatenate([vc_ref[0, 0], ov_ref[...]], axis=1)
    inv_c = []
    for h in range(H):
        bias = jnp.concatenate([tcmp_ref[0, tile_kind[t], h] for t in range(n_ct)], axis=1)
        p, m = softmax_rows(s_c[h * TQ:(h + 1) * TQ] + bias)
        l = jnp.sum(p, axis=1, keepdims=True)
        inv_c.append(jnp.where(m > 0.5 * NEG, 1.0 / l, 0.0))
        pc_ref[h * TQ:(h + 1) * TQ, :] = p.astype(BF16)
    res_c = _dot(pc_ref[...], vext_c)
    imp = None
    for h in range(H):
        r = res_c[h * TQ:(h + 1) * TQ] * jnp.concatenate([inv_c[h], inv_c[h]], axis=1)
        oc_ref[h * TQ:(h + 1) * TQ, :] = r[:, 0:C_DIM]
        imp = r[:, C_DIM:] if imp is None else imp + r[:, C_DIM:]

    def window_branch():
        n_wt = WIN // TQ + 1
        ww = n_wt * TQ
        st = jnp.maximum(qb - (n_wt - 1), 0)
        r0w = pl.multiple_of(st * TQ, TQ)
        s_w = _dot_nt(q_all, kw_ref[pl.ds(r0w, ww), :])
        vext_w = jnp.concatenate([vw_ref[pl.ds(r0w, ww), :], jnp.ones((ww, C_DIM), BF16)], axis=1)
        widx = [jnp.maximum(qb - (st + c) + 1, 0) for c in range(n_wt)]
        for h in range(H):
            bias = jnp.concatenate([twin_ref[i, h] for i in widx], axis=1)
            p, _ = softmax_rows(s_w[h * TQ:(h + 1) * TQ] + bias)
            pw_ref[h * TQ:(h + 1) * TQ, :] = p.astype(BF16)
        res_w = _dot(pw_ref[...], vext_w)
        ow_ref[...] = res_w[:, 0:C_DIM] / res_w[:, C_DIM:]

    window_branch()

    shift = SLC_BLOCK.bit_length() - 1
    qpos = qb * TQ + lax.broadcasted_iota(jnp.int32, (TQ, NB), 0)
    mblk = lax.broadcasted_iota(jnp.int32, (TQ, NB), 1)
    qblk = jnp.right_shift(qpos, shift)
    forced = (mblk == 0) | (mblk == qblk) | (mblk == qblk - 1)
    score = jnp.where(forced, 3e38, jnp.where(jnp.left_shift(mblk, shift) <= qpos, imp, NEG))
    score_t = score.T
    blk_t = lax.broadcasted_iota(jnp.int32, (NB, TQ), 0).astype(F32)
    sel_t = jnp.zeros((NB, TQ), F32)
    for _ in range(top_n):
        mx = jnp.max(score_t, axis=0, keepdims=True)
        idx = jnp.min(jnp.where(score_t == mx, blk_t, float(NB)), axis=0, keepdims=True)
        pick = blk_t == idx
        sel_t = jnp.where(pick, 1.0, sel_t)
        score_t = jnp.where(pick, -3e38, score_t)
    unsel = (1.0 - sel_t.T).astype(BF16)

    reset()
    NT = 4
    TK = NT * TQ
    q_aug = jnp.concatenate([q_all, jnp.concatenate([unsel] * H, axis=0)], axis=1)
    ones_k = jnp.ones((TK, C_DIM), BF16)
    n_steps = qb // NT + 1

    def sel_logits(kq, s_ref, near):
        kc = jnp.minimum(kq, n_steps - 1)
        r0 = pl.multiple_of(kc * TK, TK)
        k_aug = jnp.concatenate([ks_ref[pl.ds(r0, TK), :], mk_ref[pl.ds(r0, TK), :]], axis=1)
        s = _dot_nt(q_aug, k_aug)
        if not near:
            s_ref[...] = s
            return
        idx = [jnp.where(kq < n_steps, jnp.clip(qb - (NT * kc + c) + 1, 0, n_sel_tab - 1), 0)
               for c in range(NT)]
        for h in range(H):
            bias = jnp.concatenate([tsel_ref[i, h] for i in idx], axis=1)
            s_ref[h * TQ:(h + 1) * TQ, :] = s[h * TQ:(h + 1) * TQ] + bias

    def sel_softmax(s_ref, slot):
        for h in range(H):
            rs = slice(h * TQ, (h + 1) * TQ)
            s = s_ref[rs, :]
            m_prev = m_ref[rs, :]
            smax = functools.reduce(jnp.maximum, [s[:, i * 128:(i + 1) * 128] for i in range(NT)])
            m_new = jnp.maximum(m_prev, jnp.max(smax, axis=1, keepdims=True))
            a_ref[slot, rs, :] = jnp.exp2(m_prev - m_new)
            p_ref[slot, rs, :] = jnp.exp2(s - jnp.concatenate([m_new] * NT, axis=1)).astype(BF16)
            m_ref[rs, :] = m_new

    def sel_pv(kq, slot):
        r0 = pl.multiple_of(jnp.clip(kq, 0, n_steps - 1) * TK, TK)
        vext = jnp.concatenate([vs_ref[pl.ds(r0, TK), :], ones_k], axis=1)
        a = a_ref[slot]
        acc_ref[...] = jnp.concatenate([a, a], axis=1) * acc_ref[...] + _dot(p_ref[slot], vext)

    def sel_run(k_lo, count, near):
        @pl.when(count > 0)
        def _():
            sel_logits(k_lo, sa_ref, near)

        p_ref[1] = jnp.zeros(p_ref.shape[1:], BF16)
        a_ref[1] = jnp.ones(a_ref.shape[1:], F32)

        def body(j, carry):
            k = k_lo + 2 * j
            sel_logits(k + 1, sb_ref, near)
            sel_softmax(sa_ref, 0)
            sel_pv(k - 1, 1)
            sel_logits(k + 2, sa_ref, near)
            sel_softmax(sb_ref, 1)
            sel_pv(k, 0)
            return carry

        lax.fori_loop(0, count // 2, body, 0)
        last = k_lo + count - 1

        @pl.when(count % 2 == 1)
        def _():
            sel_softmax(sa_ref, 0)
            sel_pv(last - 1, 1)
            sel_pv(last, 0)

        @pl.when((count % 2 == 0) & (count > 0))
        def _():
            sel_pv(last, 1)

    n_far = jnp.maximum((qb + 1 - (n_sel_tab - 2)) // NT, 0)
    sel_run(0, n_far, False)
    sel_run(n_far, n_steps - n_far, True)
    g = g_ref[...]
    for h in range(H):
        rs = slice(h * TQ, (h + 1) * TQ)
        out_s = acc_ref[rs, 0:C_DIM] / acc_ref[rs, C_DIM:2 * C_DIM]
        o = (g[:, h:h + 1] * oc_ref[rs, :] + g[:, H + h:H + h + 1] * out_s
             + g[:, 2 * H + h:2 * H + h + 1] * ow_ref[rs, :])
        o_ref[:, h * C_DIM:(h + 1) * C_DIM] = o.astype(BF16)


def _nsa_attention(q, kv, kvc, gates, tsel, twin, tcmp, B, S):
    N = B * S
    QT = S // TQ
    n_slc = S // SLC_BLOCK
    n_cmp_pad = kvc.shape[2]
    NB = 128
    assert n_slc <= NB and n_cmp_pad % TQ == 0 and QT % 4 == 0
    ov = jnp.asarray(_overlap_np(n_cmp_pad, NB), BF16)
    mk = jnp.asarray(np.where(np.arange(S)[:, None] // SLC_BLOCK == np.arange(NB)[None, :], NEG, 0.0), BF16)
    n_sel_delta = tsel.shape[0]
    kvspec = lambda c: pl.BlockSpec((S, C_DIM), lambda b, g, t: (b, c + g))
    cspec = lambda kvi: pl.BlockSpec((1, 1, n_cmp_pad, C_DIM), lambda b, g, t: (kvi * 2 + g, b, 0, 0))
    return pl.pallas_call(
        functools.partial(_nsa_kernel, n_sel_tab=n_sel_delta, top_n=min(SLC_TOP_N, n_slc)),
        grid=(B, C_GROUPS, QT),
        in_specs=[pl.BlockSpec((TQ, C_HPG * C_DIM), lambda b, g, t: (b * QT + t, g)),
                  kvspec(0), kvspec(2), kvspec(4), kvspec(6),
                  cspec(0), cspec(1),
                  pl.BlockSpec((n_cmp_pad, NB), lambda b, g, t: (0, 0)),
                  pl.BlockSpec((S, NB), lambda b, g, t: (0, 0)),
                  pl.BlockSpec((n_sel_delta, C_HPG, TQ, TQ), lambda b, g, t: (0, g, 0, 0)),
                  pl.BlockSpec((twin.shape[0], C_HPG, TQ, TQ), lambda b, g, t: (0, g, 0, 0)),
                  pl.BlockSpec((1, 4, C_HPG, TQ, TQ), lambda b, g, t: (t % CMP_CLASSES, 0, g, 0, 0)),
                  pl.BlockSpec((TQ, 128), lambda b, g, t: (b * QT + t, g))],
        out_specs=pl.BlockSpec((TQ, C_HPG * C_DIM), lambda b, g, t: (b * QT + t, g)),
        out_shape=jax.ShapeDtypeStruct((N, C_W), BF16),
        scratch_shapes=[pltpu.VMEM((C_HPG * TQ, 128), F32),
                        pltpu.VMEM((C_HPG * TQ, 2 * C_DIM), F32),
                        pltpu.VMEM((2, C_HPG * TQ, 4 * TQ), BF16),
                        pltpu.VMEM((2, C_HPG * TQ, 128), F32),
                        pltpu.VMEM((C_HPG * TQ, n_cmp_pad), BF16),
                        pltpu.VMEM((C_HPG * TQ, WIN + TQ), BF16),
                        pltpu.VMEM((C_HPG * TQ, 4 * TQ), F32), pltpu.VMEM((C_HPG * TQ, 4 * TQ), F32),
                        pltpu.VMEM((C_HPG * TQ, C_DIM), F32), pltpu.VMEM((C_HPG * TQ, C_DIM), F32)],
        compiler_params=_params(("parallel", "parallel", "arbitrary")),
        name="nsa",
    )(q, kv, kv, kv, kv, kvc, kvc, ov, mk, tsel, twin, tcmp, gates)


def _layer_ab(h2, B, S, w_in, gate_b, conv_w, head_g, w_out, dil_tab):
    w_pad = jnp.pad(w_in, ((0, 0), (0, AB_PAD - AB_PROJ))).astype(BF16)
    gb_pad = jnp.pad(gate_b, (0, 128 - gate_b.shape[0])).reshape(1, 128).astype(F32)
    (aq, ak, av, aq4, ak4, av4, aq16, ak16, av16,
     bq, bk, bv, bo, gates) = _ab_proj(h2, w_pad, conv_w.astype(F32), gb_pad, S)
    ya = _dilated_attention({1: (aq, ak, av), 4: (aq4, ak4, av4), 16: (aq16, ak16, av16)}, dil_tab, B, S)
    yb = _mlstm(bq, bk, bv, gates, bo, head_g, B, S)
    wo = w_out.astype(BF16)
    return [ya, yb], [A_PATTERNS[-1][1], 1], [wo[:A_W], wo[A_W:]]


def _layer_c(h2, B, S, w_in, gate_b, cmp_pos, cmp_w1, cmp_w2, w_out, tsel, twin, tcmp):
    w_pad, gb_pad = _c_weights(w_in, gate_b)
    q, kc, kv, gates = _c_proj(h2, w_pad, gb_pad)
    kvc = _compress(kc, cmp_pos, cmp_w1, cmp_w2, B, S)
    out = _nsa_attention(q, kv, kvc, gates, tsel, twin, tcmp, B, S)
    return [out], [1], [w_out.astype(BF16)]


def kernel(x, rel_bias, ln_g, ln_b, ab_w_in, ab_gate_b, ab_conv, ab_head_norm, ab_w_out,
           c_w_in, c_gate_b, c_cmp_pos, c_cmp_w1, c_cmp_w2, c_w_out,
           moe_wr_g, moe_br_g, moe_wr_e, moe_br_e, moe_w_gate, moe_w_up, moe_w_down):
    B, S, D = x.shape
    assert D == D_MODEL and S % (TM) == 0 and S % (16 * A_BLOCK) == 0
    _check_cmp_windows(S)
    dil_tab = _bias_tables(rel_bias, _dilated_idx(), shift=False)
    tsel, twin, tcmp = _nsa_tables(rel_bias)
    h = x.reshape(B * S, D)
    for layer in range(DEPTH):
        j = layer // 2
        if layer % 2 == 0:
            ys, lays, ws = _layer_ab(h, B, S, ab_w_in[j], ab_gate_b[j], ab_conv[j], ab_head_norm[j],
                                     ab_w_out[j], dil_tab)
        else:
            ys, lays, ws = _layer_c(h, B, S, c_w_in[j], c_gate_b[j], c_cmp_pos[j], c_cmp_w1[j],
                                    c_cmp_w2[j], c_w_out[j], tsel, twin, tcmp)
        wr, br = _router_weights(moe_wr_g[layer], moe_br_g[layer], moe_wr_e[layer], moe_br_e[layer])
        hx, cnt = _out_ln_route(ys, lays, ws, h, ln_g[layer, 0], ln_b[layer, 0], wr, br)
        h = _moe(hx, cnt, layer, moe_w_gate, moe_w_up, moe_w_down, ln_g[layer, 1], ln_b[layer, 1])
    return h.reshape(B, S, D)
```

```python
import functools
import math

import numpy as np
import jax
import jax.numpy as jnp
from jax import lax
from jax.experimental import pallas as pl
from jax.experimental.pallas import tpu as pltpu

F32 = jnp.float32
BF16 = jnp.bfloat16
NEG = -1e30
LOG2E = math.log2(math.e)
VMEM_LIMIT = 48 * 1024 * 1024

D_MODEL = 1024
DEPTH = 2
ALPHA = (2.0 * DEPTH) ** 0.25
LN_EPS = 1e-5
REL_BUCKETS = 32
REL_MAX_DIST = 2048

A_HEADS, A_DIM, A_W = 8, 64, 512
A_PATTERNS = ((128, 1), (512, 4), (2048, 16))
A_BLOCK = 128
B_HEADS, B_DIM, B_W = 4, 128, 512
B_CHUNK = 128
B_CONV = 4
AB_PROJ = 3592
AB_PAD = 3712

C_HEADS, C_GROUPS, C_HPG, C_DIM, C_W = 8, 2, 4, 128, 1024
CMP_BLOCK, CMP_STRIDE, CMP_HIDDEN = 32, 16, 256
SLC_BLOCK, SLC_TOP_N, WIN = 64, 16, 512
C_PROJ = 2584
TQ = 128
CMP_PAD = 128

N_GROUPS, EPG, N_EXPERTS, E_HID = 4, 4, 16, 512
N_BUCKETS = N_GROUPS * 6
TM = 512
TM_MOE = 256


def _dot(a, b):
    return jnp.dot(a, b, preferred_element_type=F32)


def _dot_nt(a, b):
    return lax.dot_general(a, b, (((1,), (1,)), ((), ())), preferred_element_type=F32)


def _params(sem):
    return pltpu.CompilerParams(dimension_semantics=sem, vmem_limit_bytes=VMEM_LIMIT)


def _bucket_np(n):
    n = np.maximum(n, 0)
    exact = REL_BUCKETS // 2
    nf = np.maximum(n, 1).astype(np.float64)
    large = exact + (np.log(nf / exact) / math.log(REL_MAX_DIST / exact)
                     * (REL_BUCKETS - exact)).astype(np.int64)
    return np.where(n < exact, n, np.minimum(large, REL_BUCKETS - 1)).astype(np.int32)


def _bias_tab_kernel(tab_ref, idx_ref, out_ref, *, shift, scale):
    R = idx_ref.shape[1]
    RC = 32

    def body(i, carry):
        r0 = pl.multiple_of(i * RC, RC)
        idx = idx_ref[0, pl.ds(r0, RC), :]
        for h in range(8):
            base = tab_ref[REL_BUCKETS - 1, h] if shift else 0.0
            val = jnp.full(idx.shape, (tab_ref[0, h] - base) * scale, F32)
            for b in range(1, REL_BUCKETS):
                val = jnp.where(idx == b, (tab_ref[b, h] - base) * scale, val)
            out_ref[0, h, pl.ds(r0, RC), :] = jnp.where(idx < 0, NEG, val)
        return carry

    lax.fori_loop(0, R // RC, body, 0)


def _bias_tables(rel_bias, idx_np, shift, scale=1.0):
    T, R, C = idx_np.shape
    return pl.pallas_call(
        functools.partial(_bias_tab_kernel, shift=shift, scale=scale),
        grid=(T,),
        in_specs=[pl.BlockSpec(memory_space=pltpu.SMEM),
                  pl.BlockSpec((1, R, C), lambda t: (t, 0, 0))],
        out_specs=pl.BlockSpec((1, 8, R, C), lambda t: (t, 0, 0, 0)),
        out_shape=jax.ShapeDtypeStruct((T, 8, R, C), F32),
        compiler_params=_params(("parallel",)),
        name="bias_tables",
    )(rel_bias.astype(F32), jnp.asarray(idx_np))


def _dilated_idx():
    qi = np.arange(A_BLOCK)[:, None]
    ki = np.arange(2 * A_BLOCK)[None, :]
    j = qi + A_BLOCK - ki
    out = []
    for window, dilation in A_PATTERNS:
        nk = window // dilation
        valid = (j >= 0) & (j <= nk)
        out.append(np.where(valid, _bucket_np(np.maximum(j, 0) * dilation), -1))
    return np.stack(out).astype(np.int32)


def _sel_idx():
    a = np.arange(TQ)[:, None]
    c = np.arange(TQ)[None, :]
    n_delta = -(-(_far_dist() + TQ) // TQ)
    out = []
    for delta in range(-1, n_delta + 1):
        dist = delta * TQ + a - c
        out.append(np.where(dist >= 0, _bucket_np(dist), -1))
    return np.stack(out).astype(np.int32)


def _far_dist():
    n = np.arange(0, 4 * REL_MAX_DIST)
    b = _bucket_np(n)
    return int(np.max(n[b < REL_BUCKETS - 1])) + 1


def _win_idx():
    a = np.arange(TQ)[:, None]
    c = np.arange(TQ)[None, :]
    out = []
    for delta in range(-1, WIN // TQ + 1):
        dist = delta * TQ + a - c
        out.append(np.where((dist >= 0) & (dist < WIN), _bucket_np(dist), -1))
    return np.stack(out).astype(np.int32)


CMP_PER_TILE = TQ // CMP_STRIDE
CMP_CLASSES = TQ // CMP_PER_TILE
CMP_SPLIT = 13


def _cmp_window_start(qb):
    return qb // CMP_CLASSES + (1 if qb % CMP_CLASSES >= CMP_SPLIT else 0)


def _cmp_idx():
    a = np.arange(TQ)[:, None]
    c = np.arange(TQ)[None, :]
    out = []
    for r in range(CMP_CLASSES):
        qb = CMP_CLASSES + r
        i0 = _cmp_window_start(qb) * TQ - CMP_PAD
        for half in range(2):
            dist = qb * TQ + a - ((i0 + half * TQ + c) * CMP_STRIDE + CMP_BLOCK - 1)
            out.append(np.where(dist >= 0, _bucket_np(dist), -1))
        out.append(np.full((TQ, TQ), REL_BUCKETS - 1))
        out.append(np.full((TQ, TQ), -1))
    return np.stack(out).astype(np.int32)


def _nsa_tables(rel_bias):
    tsel = _bias_tables(rel_bias, _sel_idx(), shift=True, scale=LOG2E)
    twin = _bias_tables(rel_bias, _win_idx(), shift=False, scale=LOG2E)
    tcmp = _bias_tables(rel_bias, _cmp_idx(), shift=True, scale=LOG2E)
    return tsel, twin, tcmp.reshape(CMP_CLASSES, 4, 8, TQ, TQ)


def _check_cmp_windows(S):
    far = _far_dist()
    for qb in range(S // TQ):
        i0 = _cmp_window_start(qb) * TQ - CMP_PAD
        s0 = qb * TQ
        assert s0 - ((i0 - 1) * CMP_STRIDE + CMP_BLOCK - 1) >= far
        assert s0 + TQ - 1 - ((i0 + 2 * TQ) * CMP_STRIDE + CMP_BLOCK - 1) < 0


def _residue_col(d, r):
    return (r % 4) * 4 + r // 4 if d == 16 else r


def _lane_chunks_store(ref3, val):
    for c in range(ref3.shape[0]):
        ref3[c] = val[:, c * 128:(c + 1) * 128]


def _to_residue_layout(src3_ref, dst, d, col_of=_residue_col):
    nc, rows, _ = src3_ref.shape
    for r in range(d):
        cb = col_of(d, r)
        for c in range(nc):
            col = (cb * nc + c) * 128
            dst(slice(col, col + 128), src3_ref[c, pl.ds(r, rows // d, stride=d), :])


def _ab_proj_kernel(x_ref, xh_ref, w_ref, cw_ref, gb_ref,
                    aq_ref, ak_ref, av_ref, aq4_ref, ak4_ref, av4_ref, aq16_ref, ak16_ref, av16_ref,
                    bq_ref, bk_ref, bv_ref, bo_ref, g_ref,
                    pre_ref, tmp_ref, *, tiles_per_seq):
    i = pl.program_id(0)
    tm = x_ref.shape[0]
    xb = x_ref[...].astype(BF16)
    for c, scale, outs in ((0, A_DIM ** -0.5, (aq_ref, aq4_ref, aq16_ref)),
                           (1, 1.0, (ak_ref, ak4_ref, ak16_ref)),
                           (2, 1.0, (av_ref, av4_ref, av16_ref))):
        val = _dot(xb, w_ref[:, c * A_W:(c + 1) * A_W]) * scale
        _lane_chunks_store(tmp_ref, val)
        outs[0][...] = val.astype(BF16)
        for d, o_ref in ((4, outs[1]), (16, outs[2])):
            def put(cols, piece, o_ref=o_ref):
                o_ref[:, cols] = piece.astype(BF16)
            _to_residue_layout(tmp_ref, put, d)
    bv_ref[...] = _dot(xb, w_ref[:, 2560:3072]).astype(BF16)
    bo_ref[...] = _dot(xb, w_ref[:, 3072:3584])
    g_ref[...] = _dot(xb, w_ref[:, 3584:AB_PAD]) + gb_ref[...]
    halo = _dot(xh_ref[...].astype(BF16), w_ref[:, 1536:2560])
    halo = jnp.where(i % tiles_per_seq == 0, 0.0, halo)
    pre_ref[0:8, :] = halo
    pre_ref[8:8 + tm, :] = _dot(xb, w_ref[:, 1536:2560])
    y = pre_ref[8:8 + tm, :] * cw_ref[B_CONV - 1:B_CONV, :]
    for k in range(B_CONV - 1):
        s = B_CONV - 1 - k
        y = y + pre_ref[8 - s:8 - s + tm, :] * cw_ref[k:k + 1, :]
    y = y / (1.0 + jnp.exp(-y))
    bq_ref[...] = (y[:, :B_W] * (B_DIM ** -0.5)).astype(BF16)
    bk_ref[...] = y[:, B_W:].astype(BF16)


def _ab_proj(x2, w_pad, conv_w, gate_b_pad, S):
    N = x2.shape[0]
    tm = TM
    tps = S // tm
    row = lambda i: (i, 0)
    fix = lambda i: (0, 0)
    lay = lambda d: [jax.ShapeDtypeStruct((N // d, d * A_W), BF16)] * 3
    lay_spec = lambda d: [pl.BlockSpec((tm // d, d * A_W), row)] * 3
    outs = lay(1) + lay(4) + lay(16) + [jax.ShapeDtypeStruct((N, 512), BF16)] * 3 + [
        jax.ShapeDtypeStruct((N, 512), F32), jax.ShapeDtypeStruct((N, 128), F32)]
    o_specs = (lay_spec(1) + lay_spec(4) + lay_spec(16) + [pl.BlockSpec((tm, 512), row)] * 4
               + [pl.BlockSpec((tm, 128), row)])
    return pl.pallas_call(
        functools.partial(_ab_proj_kernel, tiles_per_seq=tps),
        grid=(N // tm,),
        in_specs=[pl.BlockSpec((tm, D_MODEL), row),
                  pl.BlockSpec((8, D_MODEL), lambda i: (jnp.maximum(i * (tm // 8) - 1, 0), 0)),
                  pl.BlockSpec((D_MODEL, AB_PAD), fix),
                  pl.BlockSpec((B_CONV, 2 * B_W), fix),
                  pl.BlockSpec((1, 128), fix)],
        out_specs=o_specs,
        out_shape=outs,
        scratch_shapes=[pltpu.VMEM((tm + 8, 2 * B_W), F32), pltpu.VMEM((A_W // 128, tm, 128), F32)],
        compiler_params=_params(("parallel",)),
        name="ab_proj",
    )(x2, x2, w_pad, conv_w, gate_b_pad)


def _dilated_kernel(*refs, has_prev, is_last):
    if has_prev:
        q_ref, kp_ref, kc_ref, vp_ref, vc_ref, tab_ref, op_ref, lp_ref = refs[:8]
        rest = refs[8:]
    else:
        q_ref, kp_ref, kc_ref, vp_ref, vc_ref, tab_ref = refs[:6]
        rest = refs[6:]
    outs, scratch = (rest, ()) if is_last else (rest[:2], rest[2:])
    o_ref = outs[0]
    n = pl.program_id(2)
    first = jnp.where(n == 0, NEG, 0.0)
    lane = lax.broadcasted_iota(jnp.int32, (A_BLOCK, 128), 1)
    keep_side = [jnp.where(lane < A_DIM, 1.0, 0.0).astype(BF16), jnp.where(lane < A_DIM, 0.0, 1.0).astype(BF16)]
    odd = jnp.bitwise_and(lane, 1) == 1
    ones_side = [jnp.where(odd, 0.0, 1.0).astype(BF16), jnp.where(odd, 1.0, 0.0).astype(BF16)]
    m_tile = jnp.zeros((A_BLOCK, 128), F32)
    l_tile = jnp.ones((A_BLOCK, 128), F32)
    unnorm = []
    for j in range(A_HEADS // 2):
        cs = slice(j * 128, (j + 1) * 128)
        q2 = q_ref[0, :, cs]
        kp, kc, vp, vc = kp_ref[0, :, cs], kc_ref[0, :, cs], vp_ref[0, :, cs], vc_ref[0, :, cs]
        k_st = jnp.concatenate([kp * keep_side[0], kc * keep_side[0],
                                kp * keep_side[1], kc * keep_side[1]], axis=0)
        s = _dot_nt(q2, k_st)
        p_parts = []
        for side in range(2):
            h = 2 * j + side
            c0 = 2 * side * A_BLOCK
            sp = s[:, c0:c0 + A_BLOCK] + tab_ref[0, h, :, 0:A_BLOCK] + first
            sc = s[:, c0 + A_BLOCK:c0 + 2 * A_BLOCK] + tab_ref[0, h, :, A_BLOCK:2 * A_BLOCK]
            m = jnp.max(jnp.maximum(sp, sc), axis=1, keepdims=True)
            p_parts += [jnp.exp(sp - m).astype(BF16), jnp.exp(sc - m).astype(BF16)]
            m_tile = jnp.where(lane == A_DIM + h, m, m_tile)
        v_st = jnp.concatenate(
            [jnp.concatenate([v * keep_side[side], ones_side[side]], axis=1)
             for side in range(2) for v in (vp, vc)], axis=0)
        r = _dot(jnp.concatenate(p_parts, axis=1), v_st)
        unnorm.append(r[:, 0:128])
        pair = (lane == A_DIM + 2 * j) | (lane == A_DIM + 2 * j + 1)
        l_tile = jnp.where(pair, r[:, 128:256], l_tile)
    stat = (lane >= A_DIM) & (lane < A_DIM + A_HEADS)
    lse = m_tile + jnp.log(l_tile)
    if has_prev:
        lp = lp_ref[0]
        mm = jnp.maximum(lp, lse)
        wp = jnp.exp(lp - mm)
        wc = jnp.exp(lse - mm)
        tot = wp + wc
        scale_prev = jnp.where(stat, wp / tot, 0.0)
        scale_cur = jnp.where(stat, wc / (tot * l_tile), 0.0)
        lse = mm + jnp.log(tot)
    else:
        scale_cur = jnp.where(stat, 1.0 / l_tile, 0.0)
    erow = lax.broadcasted_iota(jnp.int32, (128, A_W), 0)
    ecol = lax.broadcasted_iota(jnp.int32, (128, A_W), 1)
    expand = jnp.where(erow - A_DIM == jnp.right_shift(ecol, A_DIM.bit_length() - 1), 1.0, 0.0).astype(BF16)

    def spread(t):
        hi = t.astype(BF16)
        return _dot(hi, expand) + _dot((t - hi.astype(F32)).astype(BF16), expand)

    o = jnp.concatenate(unnorm, axis=1) * spread(scale_cur)
    if has_prev:
        o = o + op_ref[0] * spread(scale_prev)
    if is_last:
        o_ref[0] = o.astype(o_ref.dtype)
    else:
        o_scr, l_scr = scratch
        _lane_chunks_store(o_scr, o)
        l_scr[0] = jnp.where(stat, lse, 0.0)

        def put_o(cols, piece):
            o_ref[0, :, cols] = piece

        def put_l(cols, piece):
            outs[1][0, :, cols] = piece

        _to_residue_layout(o_scr, put_o, 4)
        _to_residue_layout(l_scr, put_l, 4)


def _dilated_call(q, k, v, tab, prev, pattern_idx, dilation, B, S, is_last):
    d = dilation
    L = S // d
    nb = L // A_BLOCK
    r3 = lambda a: a.reshape(B, L, a.shape[-1])
    cur = lambda b, r, n: (b, n, r)
    prv = lambda b, r, n: (b, jnp.maximum(n - 1, 0), r)
    blk = pl.BlockSpec((1, A_BLOCK, A_W), cur)
    in_specs = [blk, pl.BlockSpec((1, A_BLOCK, A_W), prv), blk,
                pl.BlockSpec((1, A_BLOCK, A_W), prv), blk,
                pl.BlockSpec((1, 8, A_BLOCK, 2 * A_BLOCK), lambda b, r, n: (pattern_idx, 0, 0, 0))]
    args = [r3(q), r3(k), r3(k), r3(v), r3(v), tab]
    has_prev = prev is not None
    if has_prev:
        in_specs += [blk, pl.BlockSpec((1, A_BLOCK, 128), cur)]
        args += [r3(prev[0]), r3(prev[1])]
    scratch = []
    if is_last:
        out_shape = [jax.ShapeDtypeStruct((B, L, d * A_W), BF16)]
        out_specs = [blk]
    else:
        d2, rows = 4 * d, A_BLOCK // 4
        nxt = lambda b, r, n: (b, n, r)
        out_shape = [jax.ShapeDtypeStruct((B, S // d2, d2 * A_W), F32),
                     jax.ShapeDtypeStruct((B, S // d2, d2 * 128), F32)]
        out_specs = [pl.BlockSpec((1, rows, 4 * A_W), nxt), pl.BlockSpec((1, rows, 4 * 128), nxt)]
        scratch = [pltpu.VMEM((A_W // 128, A_BLOCK, 128), F32), pltpu.VMEM((1, A_BLOCK, 128), F32)]
    res = pl.pallas_call(
        functools.partial(_dilated_kernel, has_prev=has_prev, is_last=is_last),
        grid=(B, d, nb),
        in_specs=in_specs, out_specs=out_specs, out_shape=out_shape, scratch_shapes=scratch,
        compiler_params=_params(("parallel", "parallel", "arbitrary")),
        name="dilated_d%d" % d,
    )(*args)
    return [r.reshape(-1, r.shape[-1]) for r in res]


def _dilated_attention(qkv_by_dilation, tab, B, S):
    prev = None
    for p, (window, d) in enumerate(A_PATTERNS):
        assert window // d == A_BLOCK and S % (d * A_BLOCK) == 0
        assert p == 0 or d == 4 * A_PATTERNS[p - 1][1]
        last = p == len(A_PATTERNS) - 1
        q, k, v = qkv_by_dilation[d]
        prev = _dilated_call(q, k, v, tab, prev, p, d, B, S, last)
    return prev[0]


def _split3(x):
    hi = x.astype(BF16)
    r = x - hi.astype(F32)
    mid = r.astype(BF16)
    lo = (r - mid.astype(F32)).astype(BF16)
    return hi, mid, lo


def _mlstm_kernel(q_ref, k_ref, v_ref, g_ref, bo_ref, hg_ref, y_ref, c_ref, m_ref):
    L = B_CHUNK
    c = pl.program_id(1)

    @pl.when(c == 0)
    def _():
        c_ref[...] = jnp.zeros_like(c_ref)
        m_ref[...] = jnp.zeros_like(m_ref)

    lane = lax.broadcasted_iota(jnp.int32, (L, 128), 1)
    row = lax.broadcasted_iota(jnp.int32, (L, L), 0)
    col = lax.broadcasted_iota(jnp.int32, (L, L), 1)
    tri = row >= col
    is_f = (lane >= B_HEADS) & (lane < 2 * B_HEADS)
    tril = jnp.where(tri, 1.0, 0.0).astype(BF16)
    ones = jnp.ones((L, B_DIM), BF16)
    bi = 0
    g = g_ref[bi]
    logf = jnp.minimum(g, 0.0) - jnp.log(1.0 + jnp.exp(-jnp.abs(g)))
    gl = jnp.where(is_f, logf, jnp.where(lane < B_HEADS, g, 0.0))
    hi, mid, lo = _split3(gl)
    cum = _dot(tril, hi) + _dot(tril, mid) + _dot(tril, lo)
    cum_t = cum.T
    gl_t = gl.T
    for h in range(B_HEADS):
        st = h
        cs = slice(h * B_DIM, (h + 1) * B_DIM)
        q = q_ref[bi, :, cs]
        k = k_ref[bi, :, cs]
        v_ext = jnp.concatenate([v_ref[bi, :, cs], ones], axis=1)
        b_col = cum[:, B_HEADS + h:B_HEADS + h + 1]
        b_row = cum_t[B_HEADS + h:B_HEADS + h + 1, :]
        i_col = gl[:, h:h + 1]
        i_row = gl_t[h:h + 1, :]
        m_prev = m_ref[st, 0:1, 0:1]
        dm = jnp.where(tri, b_col - b_row + i_row, NEG)
        inter = b_col + m_prev
        m_t = jnp.maximum(inter, jnp.max(dm, axis=1, keepdims=True))
        p = jnp.exp(dm - m_t)
        sqk = _dot_nt(q, k) * p
        sc = jnp.exp(inter - m_t)
        lhs = jnp.concatenate([(sc * q.astype(F32)).astype(BF16), sqk.astype(BF16)], axis=1)
        c_ext = c_ref[st]
        rhs = jnp.concatenate([c_ext.astype(BF16), v_ext], axis=0)
        res = _dot(lhs, rhs)
        num = res[:, :B_DIM]
        den = res[:, B_DIM:]
        hh = num / jnp.maximum(jnp.abs(den), jnp.exp(-m_t))
        b_last = b_col[L - 1:L, :]
        gk = b_last - b_col + i_col
        m_new = jnp.maximum(b_last + m_prev, jnp.max(gk, axis=0, keepdims=True))
        wk = jnp.exp(gk - m_new)
        decay = jnp.exp(b_last + m_prev - m_new)
        kw_t = (wk * k.astype(F32)).T.astype(BF16)
        c_ref[st] = decay * c_ext + _dot(kw_t, v_ext)
        m_ref[st] = jnp.broadcast_to(m_new, (8, 128))
        mu = jnp.mean(hh, axis=1, keepdims=True)
        xc = hh - mu
        var = jnp.mean(xc * xc, axis=1, keepdims=True)
        hn = xc * lax.rsqrt(var + LN_EPS) * hg_ref[:, cs]
        bo = bo_ref[bi, :, cs]
        y_ref[bi, :, cs] = (hn / (1.0 + jnp.exp(-bo))).astype(BF16)


def _mlstm(bq, bk, bv, gates, bo, head_g, B, S):
    nc = S // B_CHUNK
    r3 = lambda a: a.reshape(B, S, a.shape[-1])
    blk = pl.BlockSpec((1, B_CHUNK, B_W), lambda b, c: (b, c, 0))
    y = pl.pallas_call(
        _mlstm_kernel,
        grid=(B, nc),
        in_specs=[blk, blk, blk,
                  pl.BlockSpec((1, B_CHUNK, 128), lambda b, c: (b, c, 0)),
                  blk,
                  pl.BlockSpec((1, B_W), lambda b, c: (0, 0))],
        out_specs=blk,
        out_shape=jax.ShapeDtypeStruct((B, S, B_W), BF16),
        scratch_shapes=[pltpu.VMEM((B_HEADS, B_DIM, 2 * B_DIM), F32),
                        pltpu.VMEM((B_HEADS, 8, 128), F32)],
        compiler_params=_params(("parallel", "arbitrary")),
        name="mlstm",
    )(r3(bq), r3(bk), r3(bv), r3(gates), r3(bo), head_g.reshape(1, B_W).astype(F32))
    return y.reshape(B * S, B_W)


def _layer_norm(z, g, b):
    mu = jnp.mean(z, axis=1, keepdims=True)
    zc = z - mu
    var = jnp.mean(zc * zc, axis=1, keepdims=True)
    return zc * lax.rsqrt(var + LN_EPS) * g + b


def _route(logits, cnt_ref):
    tm = logits.shape[0]
    lt = logits.T
    col = lambda c: lt[c:c + 1, :]
    gl = [col(c) for c in range(N_GROUPS)]
    gmax = functools.reduce(jnp.maximum, gl)
    gsum = sum(jnp.exp(x - gmax) for x in gl)
    g_idx = jnp.full(gmax.shape, N_GROUPS - 1, jnp.int32)
    for c in range(N_GROUPS - 2, -1, -1):
        g_idx = jnp.where(gl[c] == gmax, c, g_idx)
    g_w = 1.0 / gsum
    el = []
    for k in range(EPG):
        x = col(N_GROUPS + (N_GROUPS - 1) * EPG + k)
        for g in range(N_GROUPS - 2, -1, -1):
            x = jnp.where(g_idx == g, col(N_GROUPS + g * EPG + k), x)
        el.append(x)
    v1 = functools.reduce(jnp.maximum, el)
    i1 = jnp.full(v1.shape, EPG - 1, jnp.int32)
    for k in range(EPG - 2, -1, -1):
        i1 = jnp.where(el[k] == v1, k, i1)
    el2 = [jnp.where(i1 == k, -jnp.inf, el[k]) for k in range(EPG)]
    v2 = functools.reduce(jnp.maximum, el2)
    i2 = jnp.full(v2.shape, EPG - 1, jnp.int32)
    for k in range(EPG - 2, -1, -1):
        i2 = jnp.where((el2[k] == v2) & (i1 != k), k, i2)
    t = jnp.exp(v2 - v1)
    w1 = g_w / (1.0 + t)
    w2 = w1 * t
    a = jnp.minimum(i1, i2)
    b = jnp.maximum(i1, i2)
    pair = jnp.where(a == 0, b - 1, jnp.where(a == 1, b + 1, 5))
    bucket = (g_idx * 6 + pair).astype(F32)
    w_lo = jnp.where(i1 < i2, w1, w2)
    w_hi = jnp.where(i1 < i2, w2, w1)
    sub = lax.broadcasted_iota(jnp.int32, (128, tm), 0)
    onehot_t = jnp.where(sub.astype(F32) == bucket, 1.0, 0.0)
    srow = lax.broadcasted_iota(jnp.int32, (tm, tm), 0)
    scol = lax.broadcasted_iota(jnp.int32, (tm, tm), 1)
    before = jnp.where(srow < scol, 1.0, 0.0).astype(BF16)
    oh = onehot_t.astype(BF16)
    carry = cnt_ref[...]
    prior = _dot(oh, before) + jnp.concatenate([carry] * (tm // 128), axis=1)
    rank = jnp.sum(onehot_t * prior, axis=0, keepdims=True)
    cnt_ref[...] = carry + _dot(oh, jnp.ones((tm, 128), BF16))
    out_t = jnp.where(sub == 0, bucket, jnp.where(sub == 1, w_lo, jnp.where(sub == 2, w_hi,
                      jnp.where(sub == 3, rank, 0.0))))
    return out_t.T


def _out_ln_route_kernel(*refs, n_in, layouts):
    y_refs = refs[:n_in]
    w_refs = refs[n_in:2 * n_in]
    x_ref, g_ref, b_ref, wrh_ref, wrl_ref, br_ref, h_ref, c_ref, cnt_ref = refs[2 * n_in:2 * n_in + 9]
    pos_refs = list(refs[2 * n_in + 9:])
    tm = x_ref.shape[0]

    @pl.when(pl.program_id(0) == 0)
    def _():
        cnt_ref[...] = jnp.zeros_like(cnt_ref)

    y = None
    for i in range(n_in):
        d = layouts[i]
        if d == 1:
            lhs = y_refs[i][...]
        else:
            s_ref = pos_refs.pop(0)
            nc = s_ref.shape[0]
            for r in range(d):
                cb = _residue_col(d, r)
                for c in range(nc):
                    col = (cb * nc + c) * 128
                    s_ref[c, pl.ds(r, tm // d, stride=d), :] = y_refs[i][:, col:col + 128].astype(F32)
            lhs = jnp.concatenate([s_ref[c] for c in range(nc)], axis=1).astype(BF16)
        t = _dot(lhs, w_refs[i][...])
        y = t if y is None else y + t
    hn = _layer_norm(ALPHA * x_ref[...] + y, g_ref[...], b_ref[...])
    h_ref[:, 0:D_MODEL] = hn
    hi = hn.astype(BF16)
    lo = (hn - hi.astype(F32)).astype(BF16)
    logits = (_dot_nt(hi, wrh_ref[...]) + _dot_nt(lo, wrh_ref[...]) + _dot_nt(hi, wrl_ref[...])
              + br_ref[...])
    h_ref[:, D_MODEL:D_MODEL + 128] = _route(logits, cnt_ref)
    c_ref[...] = cnt_ref[...]


def _out_ln_route(ys, layouts, ws, x2, ln_g, ln_b, wr, br):
    N = x2.shape[0]
    tm = TM
    row = lambda i: (i, 0)
    fix = lambda i: (0, 0)
    wr_hi = wr.astype(BF16)
    wr_lo = (wr - wr_hi.astype(F32)).astype(BF16)
    in_specs = ([pl.BlockSpec((tm // d, y.shape[1]), row) for y, d in zip(ys, layouts)]
                + [pl.BlockSpec(w.shape, fix) for w in ws]
                + [pl.BlockSpec((tm, D_MODEL), row),
                   pl.BlockSpec((1, D_MODEL), fix), pl.BlockSpec((1, D_MODEL), fix),
                   pl.BlockSpec((128, D_MODEL), fix), pl.BlockSpec((128, D_MODEL), fix),
                   pl.BlockSpec((1, 128), fix)])
    return pl.pallas_call(
        functools.partial(_out_ln_route_kernel, n_in=len(ys), layouts=tuple(layouts)),
        grid=(N // tm,),
        in_specs=in_specs,
        out_specs=[pl.BlockSpec((tm, D_MODEL + 128), row), pl.BlockSpec((128, 128), fix)],
        out_shape=[jax.ShapeDtypeStruct((N, D_MODEL + 128), F32), jax.ShapeDtypeStruct((128, 128), F32)],
        scratch_shapes=[pltpu.VMEM((128, 128), F32)] + [
            pltpu.VMEM((y.shape[1] // d // 128, tm, 128), F32) for y, d in zip(ys, layouts) if d != 1],
        compiler_params=_params(("arbitrary",)),
        name="out_ln_route",
    )(*ys, *ws, x2, ln_g.reshape(1, -1), ln_b.reshape(1, -1), wr_hi, wr_lo, br)


def _router_weights(wr_g, br_g, wr_e, br_e):
    we = wr_e.transpose(0, 2, 1).reshape(N_GROUPS * EPG, D_MODEL)
    w = jnp.concatenate([wr_g.T, we], axis=0)
    w = jnp.pad(w, ((0, 128 - w.shape[0]), (0, 0)))
    b = jnp.concatenate([br_g, br_e.reshape(-1)])
    b = jnp.pad(b, (0, 128 - b.shape[0])).reshape(1, 128)
    return w.astype(F32), b.astype(F32)


_PAIRS = ((0, 1), (0, 2), (0, 3), (1, 2), (1, 3), (2, 3))


def _moe_kernel(elo_ref, ehi_ref, chg_ref, nt_ref,
                x_ref, wgl_ref, wul_ref, wdl_ref, wgh_ref, wuh_ref, wdh_ref,
                g_ref, b_ref, o_ref, wg_s, wu_s, wd_s):
    t = pl.program_id(0)

    @pl.when(chg_ref[t] == 1)
    def _():
        wg_s[0] = wgl_ref[0, 0].astype(BF16)
        wu_s[0] = wul_ref[0, 0].astype(BF16)
        wd_s[0] = wdl_ref[0, 0].astype(BF16)
        wg_s[1] = wgh_ref[0, 0].astype(BF16)
        wu_s[1] = wuh_ref[0, 0].astype(BF16)
        wd_s[1] = wdh_ref[0, 0].astype(BF16)

    @pl.when(t < nt_ref[0])
    def _():
        x = x_ref[:, 0:D_MODEL]
        xb = x.astype(BF16)
        r = x_ref[:, D_MODEL:D_MODEL + 128]
        acc = None
        for e in range(2):
            a = _dot(xb, wg_s[e])
            u = _dot(xb, wu_s[e])
            hcur = (a / (1.0 + jnp.exp(-a))) * u * r[:, 1 + e:2 + e]
            y = _dot(hcur.astype(BF16), wd_s[e])
            acc = y if acc is None else acc + y
        o_ref[...] = _layer_norm(ALPHA * x + acc, g_ref[...], b_ref[...])

    @pl.when(t >= nt_ref[0])
    def _():
        o_ref[...] = jnp.zeros_like(o_ref)


def _moe(hx, cnt, layer, w_gate, w_up, w_down, ln_g, ln_b):
    N = hx.shape[0]
    tm = TM_MOE
    n_tiles = N // tm + N_BUCKETS
    n_pad = n_tiles * tm
    bucket = hx[:, D_MODEL].astype(jnp.int32)
    rank = hx[:, D_MODEL + 3].astype(jnp.int32)
    counts = cnt[:N_BUCKETS, 0].astype(jnp.int32)
    padded = ((counts + tm - 1) // tm) * tm
    ends = jnp.cumsum(padded)
    offs = ends - padded
    b2 = bucket.reshape(-1, 128)
    off2 = functools.reduce(lambda acc, b: jnp.where(b2 == b, offs[b], acc), range(N_BUCKETS),
                            jnp.zeros_like(b2))
    dest = off2.reshape(-1) + rank
    src = (jnp.arange(n_pad, dtype=jnp.int32) % N).at[dest].set(
        jnp.arange(N, dtype=jnp.int32), mode="promise_in_bounds", unique_indices=True)
    tile_start = jnp.arange(n_tiles, dtype=jnp.int32) * tm
    n_used = (ends[-1] // tm).astype(jnp.int32)
    tb = jnp.sum((tile_start[:, None] >= ends[None, :]).astype(jnp.int32), axis=1)
    tb_last = jnp.take(tb, jnp.maximum(n_used - 1, 0))
    tb = jnp.where(tile_start < ends[-1], tb, tb_last)
    pairs = jnp.asarray(_PAIRS, jnp.int32)
    elo = (tb // 6) * EPG + pairs[tb % 6, 0]
    ehi = (tb // 6) * EPG + pairs[tb % 6, 1]
    chg = jnp.concatenate([jnp.ones((1,), jnp.int32), (tb[1:] != tb[:-1]).astype(jnp.int32)])
    xs = hx.at[src].get(mode="promise_in_bounds")

    row = lambda t, *_: (t, 0)
    fix = lambda t, *_: (0, 0)
    wlo = lambda t, elo, ehi, chg, nt: (layer, elo[t], 0, 0)
    whi = lambda t, elo, ehi, chg, nt: (layer, ehi[t], 0, 0)
    up_spec = lambda im: pl.BlockSpec((1, 1, D_MODEL, E_HID), im)
    dn_spec = lambda im: pl.BlockSpec((1, 1, E_HID, D_MODEL), im)
    grid_spec = pltpu.PrefetchScalarGridSpec(
        num_scalar_prefetch=4,
        grid=(n_tiles,),
        in_specs=[pl.BlockSpec((tm, D_MODEL + 128), row),
                  up_spec(wlo), up_spec(wlo), dn_spec(wlo),
                  up_spec(whi), up_spec(whi), dn_spec(whi),
                  pl.BlockSpec((1, D_MODEL), fix), pl.BlockSpec((1, D_MODEL), fix)],
        out_specs=pl.BlockSpec((tm, D_MODEL), row),
        scratch_shapes=[pltpu.VMEM((2, D_MODEL, E_HID), BF16),
                        pltpu.VMEM((2, D_MODEL, E_HID), BF16),
                        pltpu.VMEM((2, E_HID, D_MODEL), BF16)])
    out_sorted = pl.pallas_call(
        _moe_kernel,
        grid_spec=grid_spec,
        out_shape=jax.ShapeDtypeStruct((n_pad, D_MODEL), F32),
        compiler_params=_params(("arbitrary",)),
        name="moe",
    )(elo, ehi, chg, n_used.reshape(1), xs, w_gate, w_up, w_down, w_gate, w_up, w_down,
      ln_g.reshape(1, -1), ln_b.reshape(1, -1))
    return out_sorted.at[dest].get(mode="promise_in_bounds", unique_indices=True)


def _c_proj_kernel(x_ref, w_ref, gb_ref, q_ref, kc_ref, kv_ref, g_ref, tmp_ref):
    xb = x_ref[...].astype(BF16)
    q_ref[...] = (_dot(xb, w_ref[:, 0:C_W]) * (C_DIM ** -0.5 * LOG2E)).astype(BF16)
    for i in range(4):
        tmp_ref[0] = _dot(xb, w_ref[:, C_W + i * 128:C_W + (i + 1) * 128])

        def put(cols, piece, i=i):
            kc_ref[i, :, cols] = piece.astype(BF16)

        _to_residue_layout(tmp_ref, put, CMP_STRIDE, col_of=lambda d, r: r)
    kv_ref[...] = _dot(xb, w_ref[:, C_W + 512:C_W + 1536]).astype(BF16)
    z = _dot(xb, w_ref[:, C_W + 1536:C_W + 1792]) + gb_ref[...]
    g_ref[...] = 1.0 / (1.0 + jnp.exp(-z))


def _c_proj(x2, w_pad, gb_pad):
    N = x2.shape[0]
    tm = TM
    row = lambda i: (i, 0)
    fix = lambda i: (0, 0)
    wcols = w_pad.shape[1]
    return pl.pallas_call(
        _c_proj_kernel,
        grid=(N // tm,),
        in_specs=[pl.BlockSpec((tm, D_MODEL), row), pl.BlockSpec((D_MODEL, wcols), fix),
                  pl.BlockSpec((1, 256), fix)],
        out_specs=[pl.BlockSpec((tm, C_W), row),
                   pl.BlockSpec((4, tm // CMP_STRIDE, CMP_STRIDE * C_DIM), lambda i: (0, i, 0)),
                   pl.BlockSpec((tm, 1024), row), pl.BlockSpec((tm, 256), row)],
        out_shape=[jax.ShapeDtypeStruct((N, C_W), BF16),
                   jax.ShapeDtypeStruct((4, N // CMP_STRIDE, CMP_STRIDE * C_DIM), BF16),
                   jax.ShapeDtypeStruct((N, 1024), BF16), jax.ShapeDtypeStruct((N, 256), F32)],
        scratch_shapes=[pltpu.VMEM((1, tm, 128), F32)],
        compiler_params=_params(("parallel",)),
        name="c_proj",
    )(x2, w_pad, gb_pad)


def _c_weights(w_in, gate_b):
    gcols = []
    gb = []
    for g in range(C_GROUPS):
        idx = [C_PROJ - 3 * C_HEADS + br * C_HEADS + g * C_HPG + j for br in range(3) for j in range(C_HPG)]
        gcols.append(jnp.pad(w_in[:, np.asarray(idx)], ((0, 0), (0, 128 - len(idx)))))
        gb.append(jnp.pad(gate_b[np.asarray(idx) - (C_PROJ - 3 * C_HEADS)], (0, 128 - len(idx))))
    w = jnp.concatenate([w_in[:, :C_PROJ - 3 * C_HEADS]] + gcols, axis=1).astype(BF16)
    return w, jnp.concatenate(gb).reshape(1, 256).astype(F32)


def _compress_kernel(seg_ref, w1_ref, pos_ref, w1f_ref, w2_ref, o_ref):
    n_seg = seg_ref.shape[1]
    ul = _dot(seg_ref[0], w1_ref[0])
    u = ul[:, :CMP_HIDDEN]
    lnext = pltpu.roll(ul[:, CMP_HIDDEN:], n_seg - 1, 0)
    cpos = _dot(pos_ref[0], w1f_ref[0])[0:1, :]
    pre = u + lnext + cpos
    act = 0.5 * pre * (1.0 + jnp.tanh(math.sqrt(2.0 / math.pi) * (pre + 0.044715 * pre * pre * pre)))
    o_ref[0, 0, 0:CMP_PAD, :] = jnp.zeros((CMP_PAD, C_DIM), BF16)
    o_ref[0, 0, CMP_PAD:CMP_PAD + n_seg, :] = _dot(act.astype(BF16), w2_ref[0]).astype(BF16)


def _compress(kc, cmp_pos, cmp_w1, cmp_w2, B, S):
    n_seg = S // CMP_STRIDE
    half = CMP_STRIDE * C_DIM
    seg = kc.reshape(4 * B, n_seg, half)
    w1 = cmp_w1.astype(BF16)
    w1_ul = jnp.concatenate([w1[:, :half], w1[:, half:]], axis=2)
    pos = jnp.broadcast_to(cmp_pos.reshape(2, 1, CMP_BLOCK * C_DIM), (2, 8, CMP_BLOCK * C_DIM)).astype(BF16)
    out = pl.pallas_call(
        _compress_kernel,
        grid=(4, B),
        in_specs=[pl.BlockSpec((1, n_seg, half), lambda i, b: (i * B + b, 0, 0)),
                  pl.BlockSpec((1, half, 2 * CMP_HIDDEN), lambda i, b: (i // 2, 0, 0)),
                  pl.BlockSpec((1, 8, CMP_BLOCK * C_DIM), lambda i, b: (i // 2, 0, 0)),
                  pl.BlockSpec((1, CMP_BLOCK * C_DIM, CMP_HIDDEN), lambda i, b: (i // 2, 0, 0)),
                  pl.BlockSpec((1, CMP_HIDDEN, C_DIM), lambda i, b: (i // 2, 0, 0))],
        out_specs=pl.BlockSpec((1, 1, CMP_PAD + n_seg, C_DIM), lambda i, b: (i, b, 0, 0)),
        out_shape=jax.ShapeDtypeStruct((4, B, CMP_PAD + n_seg, C_DIM), BF16),
        compiler_params=_params(("parallel", "parallel")),
        name="compress",
    )(seg, w1_ul, pos, w1, cmp_w2.astype(BF16))
    return out


def _overlap_np(n_cmp_pad, n_slc):
    i = np.arange(n_cmp_pad)[:, None] - CMP_PAD
    m = np.arange(n_slc)[None, :]
    start = i * CMP_STRIDE
    ov = (start < (m + 1) * SLC_BLOCK) & (start + CMP_BLOCK - 1 >= m * SLC_BLOCK) & (i >= 0)
    return ov.astype(np.float32)


def _nsa_kernel(q_ref, ks_ref, vs_ref, kw_ref, vw_ref, kc_ref, vc_ref, ov_ref, mk_ref,
                tsel_ref, twin_ref, tcmp_ref, g_ref, o_ref,
                m_ref, acc_ref, p_ref, a_ref, pc_ref, pw_ref, sa_ref, sb_ref, oc_ref, ow_ref,
                *, n_sel_tab, top_n):
    qb = pl.program_id(2)
    H = C_HPG
    NB = ov_ref.shape[1]
    q_all = jnp.concatenate([q_ref[:, h * C_DIM:(h + 1) * C_DIM] for h in range(H)], axis=0)

    def reset():
        m_ref[...] = jnp.full(m_ref.shape, NEG, F32)
        acc_ref[...] = jnp.zeros(acc_ref.shape, F32)

    def softmax_rows(s):
        nw = s.shape[1] // 128
        smax = functools.reduce(jnp.maximum, [s[:, i * 128:(i + 1) * 128] for i in range(nw)])
        m = jnp.broadcast_to(jnp.max(smax, axis=1, keepdims=True), (TQ, 128))
        return jnp.exp2(s - jnp.concatenate([m] * nw, axis=1)), m

    t0 = qb // CMP_CLASSES + jnp.where(qb % CMP_CLASSES >= CMP_SPLIT, 1, 0)
    n_ct = kc_ref.shape[2] // TQ
    wc = n_ct * TQ
    tile_kind = [3] + [jnp.where(t == t0, 0, jnp.where(t == t0 + 1, 1, jnp.where(t < t0, 2, 3)))
                       for t in range(1, n_ct)]
    s_c = _dot_nt(q_all, kc_ref[0, 0])
    vext_c = jnp.concatenate([vc_ref[0, 0], ov_ref[...]], axis=1)
    inv_c = []
    for h in range(H):
        bias = jnp.concatenate([tcmp_ref[0, tile_kind[t], h] for t in range(n_ct)], axis=1)
        p, m = softmax_rows(s_c[h * TQ:(h + 1) * TQ] + bias)
        l = jnp.sum(p, axis=1, keepdims=True)
        inv_c.append(jnp.where(m > 0.5 * NEG, 1.0 / l, 0.0))
        pc_ref[h * TQ:(h + 1) * TQ, :] = p.astype(BF16)
    res_c = _dot(pc_ref[...], vext_c)
    imp = None
    for h in range(H):
        r = res_c[h * TQ:(h + 1) * TQ] * jnp.concatenate([inv_c[h], inv_c[h]], axis=1)
        oc_ref[h * TQ:(h + 1) * TQ, :] = r[:, 0:C_DIM]
        imp = r[:, C_DIM:] if imp is None else imp + r[:, C_DIM:]

    def window_branch():
        n_wt = WIN // TQ + 1
        ww = n_wt * TQ
        st = jnp.maximum(qb - (n_wt - 1), 0)
        r0w = pl.multiple_of(st * TQ, TQ)
        s_w = _dot_nt(q_all, kw_ref[pl.ds(r0w, ww), :])
        vext_w = jnp.concatenate([vw_ref[pl.ds(r0w, ww), :], jnp.ones((ww, C_DIM), BF16)], axis=1)
        widx = [jnp.maximum(qb - (st + c) + 1, 0) for c in range(n_wt)]
        for h in range(H):
            bias = jnp.concatenate([twin_ref[i, h] for i in widx], axis=1)
            p, _ = softmax_rows(s_w[h * TQ:(h + 1) * TQ] + bias)
            pw_ref[h * TQ:(h + 1) * TQ, :] = p.astype(BF16)
        res_w = _dot(pw_ref[...], vext_w)
        ow_ref[...] = res_w[:, 0:C_DIM] / res_w[:, C_DIM:]

    window_branch()

    shift = SLC_BLOCK.bit_length() - 1
    qpos = qb * TQ + lax.broadcasted_iota(jnp.int32, (TQ, NB), 0)
    mblk = lax.broadcasted_iota(jnp.int32, (TQ, NB), 1)
    qblk = jnp.right_shift(qpos, shift)
    forced = (mblk == 0) | (mblk == qblk) | (mblk == qblk - 1)
    score = jnp.where(forced, 3e38, jnp.where(jnp.left_shift(mblk, shift) <= qpos, imp, NEG))
    score_t = score.T
    blk_t = lax.broadcasted_iota(jnp.int32, (NB, TQ), 0).astype(F32)
    sel_t = jnp.zeros((NB, TQ), F32)
    for _ in range(top_n):
        mx = jnp.max(score_t, axis=0, keepdims=True)
        idx = jnp.min(jnp.where(score_t == mx, blk_t, float(NB)), axis=0, keepdims=True)
        pick = blk_t == idx
        sel_t = jnp.where(pick, 1.0, sel_t)
        score_t = jnp.where(pick, -3e38, score_t)
    unsel = (1.0 - sel_t.T).astype(BF16)

    reset()
    NT = 4
    TK = NT * TQ
    q_aug = jnp.concatenate([q_all, jnp.concatenate([unsel] * H, axis=0)], axis=1)
    ones_k = jnp.ones((TK, C_DIM), BF16)
    n_steps = qb // NT + 1

    def sel_logits(kq, s_ref, near):
        kc = jnp.minimum(kq, n_steps - 1)
        r0 = pl.multiple_of(kc * TK, TK)
        k_aug = jnp.concatenate([ks_ref[pl.ds(r0, TK), :], mk_ref[pl.ds(r0, TK), :]], axis=1)
        s = _dot_nt(q_aug, k_aug)
        if not near:
            s_ref[...] = s
            return
        idx = [jnp.where(kq < n_steps, jnp.clip(qb - (NT * kc + c) + 1, 0, n_sel_tab - 1), 0)
               for c in range(NT)]
        for h in range(H):
            bias = jnp.concatenate([tsel_ref[i, h] for i in idx], axis=1)
            s_ref[h * TQ:(h + 1) * TQ, :] = s[h * TQ:(h + 1) * TQ] + bias

    def sel_softmax(s_ref, slot):
        for h in range(H):
            rs = slice(h * TQ, (h + 1) * TQ)
            s = s_ref[rs, :]
            m_prev = m_ref[rs, :]
            smax = functools.reduce(jnp.maximum, [s[:, i * 128:(i + 1) * 128] for i in range(NT)])
            m_new = jnp.maximum(m_prev, jnp.max(smax, axis=1, keepdims=True))
            a_ref[slot, rs, :] = jnp.exp2(m_prev - m_new)
            p_ref[slot, rs, :] = jnp.exp2(s - jnp.concatenate([m_new] * NT, axis=1)).astype(BF16)
            m_ref[rs, :] = m_new

    def sel_pv(kq, slot):
        r0 = pl.multiple_of(jnp.clip(kq, 0, n_steps - 1) * TK, TK)
        vext = jnp.concatenate([vs_ref[pl.ds(r0, TK), :], ones_k], axis=1)
        a = a_ref[slot]
        acc_ref[...] = jnp.concatenate([a, a], axis=1) * acc_ref[...] + _dot(p_ref[slot], vext)

    def sel_run(k_lo, count, near):
        @pl.when(count > 0)
        def _():
            sel_logits(k_lo, sa_ref, near)

        def body(j, carry):
            k = k_lo + 2 * j
            sel_logits(k + 1, sb_ref, near)
            sel_softmax(sa_ref, 0)
            sel_pv(k, 0)
            sel_logits(k + 2, sa_ref, near)
            sel_softmax(sb_ref, 1)
            sel_pv(k + 1, 1)
            return carry

        lax.fori_loop(0, count // 2, body, 0)

        @pl.when(count % 2 == 1)
        def _():
            sel_softmax(sa_ref, 0)
            sel_pv(k_lo + count - 1, 0)

    n_far = jnp.maximum((qb + 1 - (n_sel_tab - 2)) // NT, 0)
    sel_run(0, n_far, False)
    sel_run(n_far, n_steps - n_far, True)
    g = g_ref[...]
    for h in range(H):
        rs = slice(h * TQ, (h + 1) * TQ)
        out_s = acc_ref[rs, 0:C_DIM] / acc_ref[rs, C_DIM:2 * C_DIM]
        o = (g[:, h:h + 1] * oc_ref[rs, :] + g[:, H + h:H + h + 1] * out_s
             + g[:, 2 * H + h:2 * H + h + 1] * ow_ref[rs, :])
        o_ref[:, h * C_DIM:(h + 1) * C_DIM] = o.astype(BF16)


def _nsa_attention(q, kv, kvc, gates, tsel, twin, tcmp, B, S):
    N = B * S
    QT = S // TQ
    n_slc = S // SLC_BLOCK
    n_cmp_pad = kvc.shape[2]
    NB = 128
    assert n_slc <= NB and n_cmp_pad % TQ == 0 and QT % 4 == 0
    ov = jnp.asarray(_overlap_np(n_cmp_pad, NB), BF16)
    mk = jnp.asarray(np.where(np.arange(S)[:, None] // SLC_BLOCK == np.arange(NB)[None, :], NEG, 0.0), BF16)
    n_sel_delta = tsel.shape[0]
    kvspec = lambda c: pl.BlockSpec((S, C_DIM), lambda b, g, t: (b, c + g))
    cspec = lambda kvi: pl.BlockSpec((1, 1, n_cmp_pad, C_DIM), lambda b, g, t: (kvi * 2 + g, b, 0, 0))
    return pl.pallas_call(
        functools.partial(_nsa_kernel, n_sel_tab=n_sel_delta, top_n=min(SLC_TOP_N, n_slc)),
        grid=(B, C_GROUPS, QT),
        in_specs=[pl.BlockSpec((TQ, C_HPG * C_DIM), lambda b, g, t: (b * QT + t, g)),
                  kvspec(0), kvspec(2), kvspec(4), kvspec(6),
                  cspec(0), cspec(1),
                  pl.BlockSpec((n_cmp_pad, NB), lambda b, g, t: (0, 0)),
                  pl.BlockSpec((S, NB), lambda b, g, t: (0, 0)),
                  pl.BlockSpec((n_sel_delta, C_HPG, TQ, TQ), lambda b, g, t: (0, g, 0, 0)),
                  pl.BlockSpec((twin.shape[0], C_HPG, TQ, TQ), lambda b, g, t: (0, g, 0, 0)),
                  pl.BlockSpec((1, 4, C_HPG, TQ, TQ), lambda b, g, t: (t % CMP_CLASSES, 0, g, 0, 0)),
                  pl.BlockSpec((TQ, 128), lambda b, g, t: (b * QT + t, g))],
        out_specs=pl.BlockSpec((TQ, C_HPG * C_DIM), lambda b, g, t: (b * QT + t, g)),
        out_shape=jax.ShapeDtypeStruct((N, C_W), BF16),
        scratch_shapes=[pltpu.VMEM((C_HPG * TQ, 128), F32),
                        pltpu.VMEM((C_HPG * TQ, 2 * C_DIM), F32),
                        pltpu.VMEM((2, C_HPG * TQ, 4 * TQ), BF16),
                        pltpu.VMEM((2, C_HPG * TQ, 128), F32),
                        pltpu.VMEM((C_HPG * TQ, n_cmp_pad), BF16),
                        pltpu.VMEM((C_HPG * TQ, WIN + TQ), BF16),
                        pltpu.VMEM((C_HPG * TQ, 4 * TQ), F32), pltpu.VMEM((C_HPG * TQ, 4 * TQ), F32),
                        pltpu.VMEM((C_HPG * TQ, C_DIM), F32), pltpu.VMEM((C_HPG * TQ, C_DIM), F32)],
        compiler_params=_params(("parallel", "parallel", "arbitrary")),
        name="nsa",
    )(q, kv, kv, kv, kv, kvc, kvc, ov, mk, tsel, twin, tcmp, gates)


def _layer_ab(h2, B, S, w_in, gate_b, conv_w, head_g, w_out, dil_tab):
    w_pad = jnp.pad(w_in, ((0, 0), (0, AB_PAD - AB_PROJ))).astype(BF16)
    gb_pad = jnp.pad(gate_b, (0, 128 - gate_b.shape[0])).reshape(1, 128).astype(F32)
    (aq, ak, av, aq4, ak4, av4, aq16, ak16, av16,
     bq, bk, bv, bo, gates) = _ab_proj(h2, w_pad, conv_w.astype(F32), gb_pad, S)
    ya = _dilated_attention({1: (aq, ak, av), 4: (aq4, ak4, av4), 16: (aq16, ak16, av16)}, dil_tab, B, S)
    yb = _mlstm(bq, bk, bv, gates, bo, head_g, B, S)
    wo = w_out.astype(BF16)
    return [ya, yb], [A_PATTERNS[-1][1], 1], [wo[:A_W], wo[A_W:]]


def _layer_c(h2, B, S, w_in, gate_b, cmp_pos, cmp_w1, cmp_w2, w_out, tsel, twin, tcmp):
    w_pad, gb_pad = _c_weights(w_in, gate_b)
    q, kc, kv, gates = _c_proj(h2, w_pad, gb_pad)
    kvc = _compress(kc, cmp_pos, cmp_w1, cmp_w2, B, S)
    out = _nsa_attention(q, kv, kvc, gates, tsel, twin, tcmp, B, S)
    return [out], [1], [w_out.astype(BF16)]


def kernel(x, rel_bias, ln_g, ln_b, ab_w_in, ab_gate_b, ab_conv, ab_head_norm, ab_w_out,
           c_w_in, c_gate_b, c_cmp_pos, c_cmp_w1, c_cmp_w2, c_w_out,
           moe_wr_g, moe_br_g, moe_wr_e, moe_br_e, moe_w_gate, moe_w_up, moe_w_down):
    B, S, D = x.shape
    assert D == D_MODEL and S % (TM) == 0 and S % (16 * A_BLOCK) == 0
    _check_cmp_windows(S)
    dil_tab = _bias_tables(rel_bias, _dilated_idx(), shift=False)
    tsel, twin, tcmp = _nsa_tables(rel_bias)
    h = x.reshape(B * S, D)
    for layer in range(DEPTH):
        j = layer // 2
        if layer % 2 == 0:
            ys, lays, ws = _layer_ab(h, B, S, ab_w_in[j], ab_gate_b[j], ab_conv[j], ab_head_norm[j],
                                     ab_w_out[j], dil_tab)
        else:
            ys, lays, ws = _layer_c(h, B, S, c_w_in[j], c_gate_b[j], c_cmp_pos[j], c_cmp_w1[j],
                                    c_cmp_w2[j], c_w_out[j], tsel, twin, tcmp)
        wr, br = _router_weights(moe_wr_g[layer], moe_br_g[layer], moe_wr_e[layer], moe_br_e[layer])
        hx, cnt = _out_ln_route(ys, lays, ws, h, ln_g[layer, 0], ln_b[layer, 0], wr, br)
        h = _moe(hx, cnt, layer, moe_w_gate, moe_w_up, moe_w_down, ln_g[layer, 1], ln_b[layer, 1])
    return h.reshape(B, S, D)
```

```python
import functools
import math

import numpy as np
import jax
import jax.numpy as jnp
from jax import lax
from jax.experimental import pallas as pl
from jax.experimental.pallas import tpu as pltpu

F32 = jnp.float32
BF16 = jnp.bfloat16
NEG = -1e30
LOG2E = math.log2(math.e)
VMEM_LIMIT = 48 * 1024 * 1024

D_MODEL = 1024
DEPTH = 2
ALPHA = (2.0 * DEPTH) ** 0.25
LN_EPS = 1e-5
REL_BUCKETS = 32
REL_MAX_DIST = 2048

A_HEADS, A_DIM, A_W = 8, 64, 512
A_PATTERNS = ((128, 1), (512, 4), (2048, 16))
A_BLOCK = 128
B_HEADS, B_DIM, B_W = 4, 128, 512
B_CHUNK = 128
B_CONV = 4
AB_PROJ = 3592
AB_PAD = 3712

C_HEADS, C_GROUPS, C_HPG, C_DIM, C_W = 8, 2, 4, 128, 1024
CMP_BLOCK, CMP_STRIDE, CMP_HIDDEN = 32, 16, 256
SLC_BLOCK, SLC_TOP_N, WIN = 64, 16, 512
C_PROJ = 2584
TQ = 128
CMP_PAD = 128

N_GROUPS, EPG, N_EXPERTS, E_HID = 4, 4, 16, 512
N_BUCKETS = N_GROUPS * 6
TM = 512
TM_MOE = 256


def _dot(a, b):
    return jnp.dot(a, b, preferred_element_type=F32)


def _dot_nt(a, b):
    return lax.dot_general(a, b, (((1,), (1,)), ((), ())), preferred_element_type=F32)


def _params(sem):
    return pltpu.CompilerParams(dimension_semantics=sem, vmem_limit_bytes=VMEM_LIMIT)


def _bucket_np(n):
    n = np.maximum(n, 0)
    exact = REL_BUCKETS // 2
    nf = np.maximum(n, 1).astype(np.float64)
    large = exact + (np.log(nf / exact) / math.log(REL_MAX_DIST / exact)
                     * (REL_BUCKETS - exact)).astype(np.int64)
    return np.where(n < exact, n, np.minimum(large, REL_BUCKETS - 1)).astype(np.int32)


def _bias_tab_kernel(tab_ref, idx_ref, out_ref, *, shift, scale):
    R = idx_ref.shape[1]
    RC = 32

    def body(i, carry):
        r0 = pl.multiple_of(i * RC, RC)
        idx = idx_ref[0, pl.ds(r0, RC), :]
        for h in range(8):
            base = tab_ref[REL_BUCKETS - 1, h] if shift else 0.0
            val = jnp.full(idx.shape, (tab_ref[0, h] - base) * scale, F32)
            for b in range(1, REL_BUCKETS):
                val = jnp.where(idx == b, (tab_ref[b, h] - base) * scale, val)
            out_ref[0, h, pl.ds(r0, RC), :] = jnp.where(idx < 0, NEG, val)
        return carry

    lax.fori_loop(0, R // RC, body, 0)


def _bias_tables(rel_bias, idx_np, shift, scale=1.0):
    T, R, C = idx_np.shape
    return pl.pallas_call(
        functools.partial(_bias_tab_kernel, shift=shift, scale=scale),
        grid=(T,),
        in_specs=[pl.BlockSpec(memory_space=pltpu.SMEM),
                  pl.BlockSpec((1, R, C), lambda t: (t, 0, 0))],
        out_specs=pl.BlockSpec((1, 8, R, C), lambda t: (t, 0, 0, 0)),
        out_shape=jax.ShapeDtypeStruct((T, 8, R, C), F32),
        compiler_params=_params(("parallel",)),
        name="bias_tables",
    )(rel_bias.astype(F32), jnp.asarray(idx_np))


def _dilated_idx():
    qi = np.arange(A_BLOCK)[:, None]
    ki = np.arange(2 * A_BLOCK)[None, :]
    j = qi + A_BLOCK - ki
    out = []
    for window, dilation in A_PATTERNS:
        nk = window // dilation
        valid = (j >= 0) & (j <= nk)
        out.append(np.where(valid, _bucket_np(np.maximum(j, 0) * dilation), -1))
    return np.stack(out).astype(np.int32)


def _sel_idx():
    a = np.arange(TQ)[:, None]
    c = np.arange(TQ)[None, :]
    n_delta = -(-(_far_dist() + TQ) // TQ)
    out = []
    for delta in range(-1, n_delta + 1):
        dist = delta * TQ + a - c
        out.append(np.where(dist >= 0, _bucket_np(dist), -1))
    return np.stack(out).astype(np.int32)


def _far_dist():
    n = np.arange(0, 4 * REL_MAX_DIST)
    b = _bucket_np(n)
    return int(np.max(n[b < REL_BUCKETS - 1])) + 1


def _win_idx():
    a = np.arange(TQ)[:, None]
    c = np.arange(TQ)[None, :]
    out = []
    for delta in range(-1, WIN // TQ + 1):
        dist = delta * TQ + a - c
        out.append(np.where((dist >= 0) & (dist < WIN), _bucket_np(dist), -1))
    return np.stack(out).astype(np.int32)


CMP_PER_TILE = TQ // CMP_STRIDE
CMP_CLASSES = TQ // CMP_PER_TILE
CMP_SPLIT = 13


def _cmp_window_start(qb):
    return qb // CMP_CLASSES + (1 if qb % CMP_CLASSES >= CMP_SPLIT else 0)


def _cmp_idx():
    a = np.arange(TQ)[:, None]
    c = np.arange(TQ)[None, :]
    out = []
    for r in range(CMP_CLASSES):
        qb = CMP_CLASSES + r
        i0 = _cmp_window_start(qb) * TQ - CMP_PAD
        for half in range(2):
            dist = qb * TQ + a - ((i0 + half * TQ + c) * CMP_STRIDE + CMP_BLOCK - 1)
            out.append(np.where(dist >= 0, _bucket_np(dist), -1))
        out.append(np.full((TQ, TQ), REL_BUCKETS - 1))
        out.append(np.full((TQ, TQ), -1))
    return np.stack(out).astype(np.int32)


def _nsa_tables(rel_bias):
    tsel = _bias_tables(rel_bias, _sel_idx(), shift=True, scale=LOG2E)
    twin = _bias_tables(rel_bias, _win_idx(), shift=False, scale=LOG2E)
    tcmp = _bias_tables(rel_bias, _cmp_idx(), shift=True, scale=LOG2E)
    return tsel, twin, tcmp.reshape(CMP_CLASSES, 4, 8, TQ, TQ)


def _check_cmp_windows(S):
    far = _far_dist()
    for qb in range(S // TQ):
        i0 = _cmp_window_start(qb) * TQ - CMP_PAD
        s0 = qb * TQ
        assert s0 - ((i0 - 1) * CMP_STRIDE + CMP_BLOCK - 1) >= far
        assert s0 + TQ - 1 - ((i0 + 2 * TQ) * CMP_STRIDE + CMP_BLOCK - 1) < 0


def _residue_col(d, r):
    return (r % 4) * 4 + r // 4 if d == 16 else r


def _lane_chunks_store(ref3, val):
    for c in range(ref3.shape[0]):
        ref3[c] = val[:, c * 128:(c + 1) * 128]


def _to_residue_layout(src3_ref, dst, d, col_of=_residue_col):
    nc, rows, _ = src3_ref.shape
    for r in range(d):
        cb = col_of(d, r)
        for c in range(nc):
            col = (cb * nc + c) * 128
            dst(slice(col, col + 128), src3_ref[c, pl.ds(r, rows // d, stride=d), :])


def _ab_proj_kernel(x_ref, xh_ref, w_ref, cw_ref, gb_ref,
                    aq_ref, ak_ref, av_ref, aq4_ref, ak4_ref, av4_ref, aq16_ref, ak16_ref, av16_ref,
                    bq_ref, bk_ref, bv_ref, bo_ref, g_ref,
                    pre_ref, tmp_ref, *, tiles_per_seq):
    i = pl.program_id(0)
    tm = x_ref.shape[0]
    xb = x_ref[...].astype(BF16)
    for c, scale, outs in ((0, A_DIM ** -0.5, (aq_ref, aq4_ref, aq16_ref)),
                           (1, 1.0, (ak_ref, ak4_ref, ak16_ref)),
                           (2, 1.0, (av_ref, av4_ref, av16_ref))):
        val = _dot(xb, w_ref[:, c * A_W:(c + 1) * A_W]) * scale
        _lane_chunks_store(tmp_ref, val)
        outs[0][...] = val.astype(BF16)
        for d, o_ref in ((4, outs[1]), (16, outs[2])):
            def put(cols, piece, o_ref=o_ref):
                o_ref[:, cols] = piece.astype(BF16)
            _to_residue_layout(tmp_ref, put, d)
    bv_ref[...] = _dot(xb, w_ref[:, 2560:3072]).astype(BF16)
    bo_ref[...] = _dot(xb, w_ref[:, 3072:3584])
    g_ref[...] = _dot(xb, w_ref[:, 3584:AB_PAD]) + gb_ref[...]
    halo = _dot(xh_ref[...].astype(BF16), w_ref[:, 1536:2560])
    halo = jnp.where(i % tiles_per_seq == 0, 0.0, halo)
    pre_ref[0:8, :] = halo
    pre_ref[8:8 + tm, :] = _dot(xb, w_ref[:, 1536:2560])
    y = pre_ref[8:8 + tm, :] * cw_ref[B_CONV - 1:B_CONV, :]
    for k in range(B_CONV - 1):
        s = B_CONV - 1 - k
        y = y + pre_ref[8 - s:8 - s + tm, :] * cw_ref[k:k + 1, :]
    y = y / (1.0 + jnp.exp(-y))
    bq_ref[...] = (y[:, :B_W] * (B_DIM ** -0.5)).astype(BF16)
    bk_ref[...] = y[:, B_W:].astype(BF16)


def _ab_proj(x2, w_pad, conv_w, gate_b_pad, S):
    N = x2.shape[0]
    tm = TM
    tps = S // tm
    row = lambda i: (i, 0)
    fix = lambda i: (0, 0)
    lay = lambda d: [jax.ShapeDtypeStruct((N // d, d * A_W), BF16)] * 3
    lay_spec = lambda d: [pl.BlockSpec((tm // d, d * A_W), row)] * 3
    outs = lay(1) + lay(4) + lay(16) + [jax.ShapeDtypeStruct((N, 512), BF16)] * 3 + [
        jax.ShapeDtypeStruct((N, 512), F32), jax.ShapeDtypeStruct((N, 128), F32)]
    o_specs = (lay_spec(1) + lay_spec(4) + lay_spec(16) + [pl.BlockSpec((tm, 512), row)] * 4
               + [pl.BlockSpec((tm, 128), row)])
    return pl.pallas_call(
        functools.partial(_ab_proj_kernel, tiles_per_seq=tps),
        grid=(N // tm,),
        in_specs=[pl.BlockSpec((tm, D_MODEL), row),
                  pl.BlockSpec((8, D_MODEL), lambda i: (jnp.maximum(i * (tm // 8) - 1, 0), 0)),
                  pl.BlockSpec((D_MODEL, AB_PAD), fix),
                  pl.BlockSpec((B_CONV, 2 * B_W), fix),
                  pl.BlockSpec((1, 128), fix)],
        out_specs=o_specs,
        out_shape=outs,
        scratch_shapes=[pltpu.VMEM((tm + 8, 2 * B_W), F32), pltpu.VMEM((A_W // 128, tm, 128), F32)],
        compiler_params=_params(("parallel",)),
        name="ab_proj",
    )(x2, x2, w_pad, conv_w, gate_b_pad)


def _dilated_kernel(*refs, has_prev, is_last):
    if has_prev:
        q_ref, kp_ref, kc_ref, vp_ref, vc_ref, tab_ref, op_ref, lp_ref = refs[:8]
        rest = refs[8:]
    else:
        q_ref, kp_ref, kc_ref, vp_ref, vc_ref, tab_ref = refs[:6]
        rest = refs[6:]
    outs, scratch = (rest, ()) if is_last else (rest[:2], rest[2:])
    o_ref = outs[0]
    n = pl.program_id(2)
    first = jnp.where(n == 0, NEG, 0.0)
    lane = lax.broadcasted_iota(jnp.int32, (A_BLOCK, 128), 1)
    keep_side = [jnp.where(lane < A_DIM, 1.0, 0.0).astype(BF16), jnp.where(lane < A_DIM, 0.0, 1.0).astype(BF16)]
    odd = jnp.bitwise_and(lane, 1) == 1
    ones_side = [jnp.where(odd, 0.0, 1.0).astype(BF16), jnp.where(odd, 1.0, 0.0).astype(BF16)]
    m_tile = jnp.zeros((A_BLOCK, 128), F32)
    l_tile = jnp.ones((A_BLOCK, 128), F32)
    unnorm = []
    for j in range(A_HEADS // 2):
        cs = slice(j * 128, (j + 1) * 128)
        q2 = q_ref[0, :, cs]
        kp, kc, vp, vc = kp_ref[0, :, cs], kc_ref[0, :, cs], vp_ref[0, :, cs], vc_ref[0, :, cs]
        k_st = jnp.concatenate([kp * keep_side[0], kc * keep_side[0],
                                kp * keep_side[1], kc * keep_side[1]], axis=0)
        s = _dot_nt(q2, k_st)
        p_parts = []
        for side in range(2):
            h = 2 * j + side
            c0 = 2 * side * A_BLOCK
            sp = s[:, c0:c0 + A_BLOCK] + tab_ref[0, h, :, 0:A_BLOCK] + first
            sc = s[:, c0 + A_BLOCK:c0 + 2 * A_BLOCK] + tab_ref[0, h, :, A_BLOCK:2 * A_BLOCK]
            m = jnp.max(jnp.maximum(sp, sc), axis=1, keepdims=True)
            p_parts += [jnp.exp(sp - m).astype(BF16), jnp.exp(sc - m).astype(BF16)]
            m_tile = jnp.where(lane == A_DIM + h, m, m_tile)
        v_st = jnp.concatenate(
            [jnp.concatenate([v * keep_side[side], ones_side[side]], axis=1)
             for side in range(2) for v in (vp, vc)], axis=0)
        r = _dot(jnp.concatenate(p_parts, axis=1), v_st)
        unnorm.append(r[:, 0:128])
        pair = (lane == A_DIM + 2 * j) | (lane == A_DIM + 2 * j + 1)
        l_tile = jnp.where(pair, r[:, 128:256], l_tile)
    stat = (lane >= A_DIM) & (lane < A_DIM + A_HEADS)
    lse = m_tile + jnp.log(l_tile)
    if has_prev:
        lp = lp_ref[0]
        mm = jnp.maximum(lp, lse)
        wp = jnp.exp(lp - mm)
        wc = jnp.exp(lse - mm)
        tot = wp + wc
        scale_prev = jnp.where(stat, wp / tot, 0.0)
        scale_cur = jnp.where(stat, wc / (tot * l_tile), 0.0)
        lse = mm + jnp.log(tot)
    else:
        scale_cur = jnp.where(stat, 1.0 / l_tile, 0.0)
    erow = lax.broadcasted_iota(jnp.int32, (128, A_W), 0)
    ecol = lax.broadcasted_iota(jnp.int32, (128, A_W), 1)
    expand = jnp.where(erow - A_DIM == jnp.right_shift(ecol, A_DIM.bit_length() - 1), 1.0, 0.0).astype(BF16)

    def spread(t):
        hi = t.astype(BF16)
        return _dot(hi, expand) + _dot((t - hi.astype(F32)).astype(BF16), expand)

    o = jnp.concatenate(unnorm, axis=1) * spread(scale_cur)
    if has_prev:
        o = o + op_ref[0] * spread(scale_prev)
    if is_last:
        o_ref[0] = o.astype(o_ref.dtype)
    else:
        o_scr, l_scr = scratch
        _lane_chunks_store(o_scr, o)
        l_scr[0] = jnp.where(stat, lse, 0.0)

        def put_o(cols, piece):
            o_ref[0, :, cols] = piece

        def put_l(cols, piece):
            outs[1][0, :, cols] = piece

        _to_residue_layout(o_scr, put_o, 4)
        _to_residue_layout(l_scr, put_l, 4)


def _dilated_call(q, k, v, tab, prev, pattern_idx, dilation, B, S, is_last):
    d = dilation
    L = S // d
    nb = L // A_BLOCK
    r3 = lambda a: a.reshape(B, L, a.shape[-1])
    cur = lambda b, r, n: (b, n, r)
    prv = lambda b, r, n: (b, jnp.maximum(n - 1, 0), r)
    blk = pl.BlockSpec((1, A_BLOCK, A_W), cur)
    in_specs = [blk, pl.BlockSpec((1, A_BLOCK, A_W), prv), blk,
                pl.BlockSpec((1, A_BLOCK, A_W), prv), blk,
                pl.BlockSpec((1, 8, A_BLOCK, 2 * A_BLOCK), lambda b, r, n: (pattern_idx, 0, 0, 0))]
    args = [r3(q), r3(k), r3(k), r3(v), r3(v), tab]
    has_prev = prev is not None
    if has_prev:
        in_specs += [blk, pl.BlockSpec((1, A_BLOCK, 128), cur)]
        args += [r3(prev[0]), r3(prev[1])]
    scratch = []
    if is_last:
        out_shape = [jax.ShapeDtypeStruct((B, L, d * A_W), BF16)]
        out_specs = [blk]
    else:
        d2, rows = 4 * d, A_BLOCK // 4
        nxt = lambda b, r, n: (b, n, r)
        out_shape = [jax.ShapeDtypeStruct((B, S // d2, d2 * A_W), F32),
                     jax.ShapeDtypeStruct((B, S // d2, d2 * 128), F32)]
        out_specs = [pl.BlockSpec((1, rows, 4 * A_W), nxt), pl.BlockSpec((1, rows, 4 * 128), nxt)]
        scratch = [pltpu.VMEM((A_W // 128, A_BLOCK, 128), F32), pltpu.VMEM((1, A_BLOCK, 128), F32)]
    res = pl.pallas_call(
        functools.partial(_dilated_kernel, has_prev=has_prev, is_last=is_last),
        grid=(B, d, nb),
        in_specs=in_specs, out_specs=out_specs, out_shape=out_shape, scratch_shapes=scratch,
        compiler_params=_params(("parallel", "parallel", "arbitrary")),
        name="dilated_d%d" % d,
    )(*args)
    return [r.reshape(-1, r.shape[-1]) for r in res]


def _dilated_attention(qkv_by_dilation, tab, B, S):
    prev = None
    for p, (window, d) in enumerate(A_PATTERNS):
        assert window // d == A_BLOCK and S % (d * A_BLOCK) == 0
        assert p == 0 or d == 4 * A_PATTERNS[p - 1][1]
        last = p == len(A_PATTERNS) - 1
        q, k, v = qkv_by_dilation[d]
        prev = _dilated_call(q, k, v, tab, prev, p, d, B, S, last)
    return prev[0]


def _split3(x):
    hi = x.astype(BF16)
    r = x - hi.astype(F32)
    mid = r.astype(BF16)
    lo = (r - mid.astype(F32)).astype(BF16)
    return hi, mid, lo


def _mlstm_kernel(q_ref, k_ref, v_ref, g_ref, bo_ref, hg_ref, y_ref, c_ref, m_ref):
    L = B_CHUNK
    c = pl.program_id(1)

    @pl.when(c == 0)
    def _():
        c_ref[...] = jnp.zeros_like(c_ref)
        m_ref[...] = jnp.zeros_like(m_ref)

    lane = lax.broadcasted_iota(jnp.int32, (L, 128), 1)
    row = lax.broadcasted_iota(jnp.int32, (L, L), 0)
    col = lax.broadcasted_iota(jnp.int32, (L, L), 1)
    tri = row >= col
    is_f = (lane >= B_HEADS) & (lane < 2 * B_HEADS)
    tril = jnp.where(tri, 1.0, 0.0).astype(BF16)
    ones = jnp.ones((L, B_DIM), BF16)
    bi = 0
    g = g_ref[bi]
    logf = jnp.minimum(g, 0.0) - jnp.log(1.0 + jnp.exp(-jnp.abs(g)))
    gl = jnp.where(is_f, logf, jnp.where(lane < B_HEADS, g, 0.0))
    hi, mid, lo = _split3(gl)
    cum = _dot(tril, hi) + _dot(tril, mid) + _dot(tril, lo)
    cum_t = cum.T
    gl_t = gl.T
    for h in range(B_HEADS):
        st = h
        cs = slice(h * B_DIM, (h + 1) * B_DIM)
        q = q_ref[bi, :, cs]
        k = k_ref[bi, :, cs]
        v_ext = jnp.concatenate([v_ref[bi, :, cs], ones], axis=1)
        b_col = cum[:, B_HEADS + h:B_HEADS + h + 1]
        b_row = cum_t[B_HEADS + h:B_HEADS + h + 1, :]
        i_col = gl[:, h:h + 1]
        i_row = gl_t[h:h + 1, :]
        m_prev = m_ref[st, 0:1, 0:1]
        dm = jnp.where(tri, b_col - b_row + i_row, NEG)
        inter = b_col + m_prev
        m_t = jnp.maximum(inter, jnp.max(dm, axis=1, keepdims=True))
        p = jnp.exp(dm - m_t)
        sqk = _dot_nt(q, k) * p
        sc = jnp.exp(inter - m_t)
        lhs = jnp.concatenate([(sc * q.astype(F32)).astype(BF16), sqk.astype(BF16)], axis=1)
        c_ext = c_ref[st]
        rhs = jnp.concatenate([c_ext.astype(BF16), v_ext], axis=0)
        res = _dot(lhs, rhs)
        num = res[:, :B_DIM]
        den = res[:, B_DIM:]
        hh = num / jnp.maximum(jnp.abs(den), jnp.exp(-m_t))
        b_last = b_col[L - 1:L, :]
        gk = b_last - b_col + i_col
        m_new = jnp.maximum(b_last + m_prev, jnp.max(gk, axis=0, keepdims=True))
        wk = jnp.exp(gk - m_new)
        decay = jnp.exp(b_last + m_prev - m_new)
        kw_t = (wk * k.astype(F32)).T.astype(BF16)
        c_ref[st] = decay * c_ext + _dot(kw_t, v_ext)
        m_ref[st] = jnp.broadcast_to(m_new, (8, 128))
        mu = jnp.mean(hh, axis=1, keepdims=True)
        xc = hh - mu
        var = jnp.mean(xc * xc, axis=1, keepdims=True)
        hn = xc * lax.rsqrt(var + LN_EPS) * hg_ref[:, cs]
        bo = bo_ref[bi, :, cs]
        y_ref[bi, :, cs] = (hn / (1.0 + jnp.exp(-bo))).astype(BF16)


def _mlstm(bq, bk, bv, gates, bo, head_g, B, S):
    nc = S // B_CHUNK
    r3 = lambda a: a.reshape(B, S, a.shape[-1])
    blk = pl.BlockSpec((1, B_CHUNK, B_W), lambda b, c: (b, c, 0))
    y = pl.pallas_call(
        _mlstm_kernel,
        grid=(B, nc),
        in_specs=[blk, blk, blk,
                  pl.BlockSpec((1, B_CHUNK, 128), lambda b, c: (b, c, 0)),
                  blk,
                  pl.BlockSpec((1, B_W), lambda b, c: (0, 0))],
        out_specs=blk,
        out_shape=jax.ShapeDtypeStruct((B, S, B_W), BF16),
        scratch_shapes=[pltpu.VMEM((B_HEADS, B_DIM, 2 * B_DIM), F32),
                        pltpu.VMEM((B_HEADS, 8, 128), F32)],
        compiler_params=_params(("parallel", "arbitrary")),
        name="mlstm",
    )(r3(bq), r3(bk), r3(bv), r3(gates), r3(bo), head_g.reshape(1, B_W).astype(F32))
    return y.reshape(B * S, B_W)


def _layer_norm(z, g, b):
    mu = jnp.mean(z, axis=1, keepdims=True)
    zc = z - mu
    var = jnp.mean(zc * zc, axis=1, keepdims=True)
    return zc * lax.rsqrt(var + LN_EPS) * g + b


def _route(logits, cnt_ref):
    tm = logits.shape[0]
    lt = logits.T
    col = lambda c: lt[c:c + 1, :]
    gl = [col(c) for c in range(N_GROUPS)]
    gmax = functools.reduce(jnp.maximum, gl)
    gsum = sum(jnp.exp(x - gmax) for x in gl)
    g_idx = jnp.full(gmax.shape, N_GROUPS - 1, jnp.int32)
    for c in range(N_GROUPS - 2, -1, -1):
        g_idx = jnp.where(gl[c] == gmax, c, g_idx)
    g_w = 1.0 / gsum
    el = []
    for k in range(EPG):
        x = col(N_GROUPS + (N_GROUPS - 1) * EPG + k)
        for g in range(N_GROUPS - 2, -1, -1):
            x = jnp.where(g_idx == g, col(N_GROUPS + g * EPG + k), x)
        el.append(x)
    v1 = functools.reduce(jnp.maximum, el)
    i1 = jnp.full(v1.shape, EPG - 1, jnp.int32)
    for k in range(EPG - 2, -1, -1):
        i1 = jnp.where(el[k] == v1, k, i1)
    el2 = [jnp.where(i1 == k, -jnp.inf, el[k]) for k in range(EPG)]
    v2 = functools.reduce(jnp.maximum, el2)
    i2 = jnp.full(v2.shape, EPG - 1, jnp.int32)
    for k in range(EPG - 2, -1, -1):
        i2 = jnp.where((el2[k] == v2) & (i1 != k), k, i2)
    t = jnp.exp(v2 - v1)
    w1 = g_w / (1.0 + t)
    w2 = w1 * t
    a = jnp.minimum(i1, i2)
    b = jnp.maximum(i1, i2)
    pair = jnp.where(a == 0, b - 1, jnp.where(a == 1, b + 1, 5))
    bucket = (g_idx * 6 + pair).astype(F32)
    w_lo = jnp.where(i1 < i2, w1, w2)
    w_hi = jnp.where(i1 < i2, w2, w1)
    sub = lax.broadcasted_iota(jnp.int32, (128, tm), 0)
    onehot_t = jnp.where(sub.astype(F32) == bucket, 1.0, 0.0)
    srow = lax.broadcasted_iota(jnp.int32, (tm, tm), 0)
    scol = lax.broadcasted_iota(jnp.int32, (tm, tm), 1)
    before = jnp.where(srow < scol, 1.0, 0.0).astype(BF16)
    oh = onehot_t.astype(BF16)
    carry = cnt_ref[...]
    prior = _dot(oh, before) + jnp.concatenate([carry] * (tm // 128), axis=1)
    rank = jnp.sum(onehot_t * prior, axis=0, keepdims=True)
    cnt_ref[...] = carry + _dot(oh, jnp.ones((tm, 128), BF16))
    out_t = jnp.where(sub == 0, bucket, jnp.where(sub == 1, w_lo, jnp.where(sub == 2, w_hi,
                      jnp.where(sub == 3, rank, 0.0))))
    return out_t.T


def _out_ln_route_kernel(*refs, n_in, layouts):
    y_refs = refs[:n_in]
    w_refs = refs[n_in:2 * n_in]
    x_ref, g_ref, b_ref, wrh_ref, wrl_ref, br_ref, h_ref, c_ref, cnt_ref = refs[2 * n_in:2 * n_in + 9]
    pos_refs = list(refs[2 * n_in + 9:])
    tm = x_ref.shape[0]

    @pl.when(pl.program_id(0) == 0)
    def _():
        cnt_ref[...] = jnp.zeros_like(cnt_ref)

    y = None
    for i in range(n_in):
        d = layouts[i]
        if d == 1:
            lhs = y_refs[i][...]
        else:
            s_ref = pos_refs.pop(0)
            nc = s_ref.shape[0]
            for r in range(d):
                cb = _residue_col(d, r)
                for c in range(nc):
                    col = (cb * nc + c) * 128
                    s_ref[c, pl.ds(r, tm // d, stride=d), :] = y_refs[i][:, col:col + 128].astype(F32)
            lhs = jnp.concatenate([s_ref[c] for c in range(nc)], axis=1).astype(BF16)
        t = _dot(lhs, w_refs[i][...])
        y = t if y is None else y + t
    hn = _layer_norm(ALPHA * x_ref[...] + y, g_ref[...], b_ref[...])
    h_ref[:, 0:D_MODEL] = hn
    hi = hn.astype(BF16)
    lo = (hn - hi.astype(F32)).astype(BF16)
    logits = (_dot_nt(hi, wrh_ref[...]) + _dot_nt(lo, wrh_ref[...]) + _dot_nt(hi, wrl_ref[...])
              + br_ref[...])
    h_ref[:, D_MODEL:D_MODEL + 128] = _route(logits, cnt_ref)
    c_ref[...] = cnt_ref[...]


def _out_ln_route(ys, layouts, ws, x2, ln_g, ln_b, wr, br):
    N = x2.shape[0]
    tm = TM
    row = lambda i: (i, 0)
    fix = lambda i: (0, 0)
    wr_hi = wr.astype(BF16)
    wr_lo = (wr - wr_hi.astype(F32)).astype(BF16)
    in_specs = ([pl.BlockSpec((tm // d, y.shape[1]), row) for y, d in zip(ys, layouts)]
                + [pl.BlockSpec(w.shape, fix) for w in ws]
                + [pl.BlockSpec((tm, D_MODEL), row),
                   pl.BlockSpec((1, D_MODEL), fix), pl.BlockSpec((1, D_MODEL), fix),
                   pl.BlockSpec((128, D_MODEL), fix), pl.BlockSpec((128, D_MODEL), fix),
                   pl.BlockSpec((1, 128), fix)])
    return pl.pallas_call(
        functools.partial(_out_ln_route_kernel, n_in=len(ys), layouts=tuple(layouts)),
        grid=(N // tm,),
        in_specs=in_specs,
        out_specs=[pl.BlockSpec((tm, D_MODEL + 128), row), pl.BlockSpec((128, 128), fix)],
        out_shape=[jax.ShapeDtypeStruct((N, D_MODEL + 128), F32), jax.ShapeDtypeStruct((128, 128), F32)],
        scratch_shapes=[pltpu.VMEM((128, 128), F32)] + [
            pltpu.VMEM((y.shape[1] // d // 128, tm, 128), F32) for y, d in zip(ys, layouts) if d != 1],
        compiler_params=_params(("arbitrary",)),
        name="out_ln_route",
    )(*ys, *ws, x2, ln_g.reshape(1, -1), ln_b.reshape(1, -1), wr_hi, wr_lo, br)


def _router_weights(wr_g, br_g, wr_e, br_e):
    we = wr_e.transpose(0, 2, 1).reshape(N_GROUPS * EPG, D_MODEL)
    w = jnp.concatenate([wr_g.T, we], axis=0)
    w = jnp.pad(w, ((0, 128 - w.shape[0]), (0, 0)))
    b = jnp.concatenate([br_g, br_e.reshape(-1)])
    b = jnp.pad(b, (0, 128 - b.shape[0])).reshape(1, 128)
    return w.astype(F32), b.astype(F32)


_PAIRS = ((0, 1), (0, 2), (0, 3), (1, 2), (1, 3), (2, 3))


def _moe_kernel(elo_ref, ehi_ref, chg_ref, nt_ref,
                x_ref, wgl_ref, wul_ref, wdl_ref, wgh_ref, wuh_ref, wdh_ref,
                g_ref, b_ref, o_ref, wg_s, wu_s, wd_s):
    t = pl.program_id(0)

    @pl.when(chg_ref[t] == 1)
    def _():
        wg_s[0] = wgl_ref[0, 0].astype(BF16)
        wu_s[0] = wul_ref[0, 0].astype(BF16)
        wd_s[0] = wdl_ref[0, 0].astype(BF16)
        wg_s[1] = wgh_ref[0, 0].astype(BF16)
        wu_s[1] = wuh_ref[0, 0].astype(BF16)
        wd_s[1] = wdh_ref[0, 0].astype(BF16)

    @pl.when(t < nt_ref[0])
    def _():
        x = x_ref[:, 0:D_MODEL]
        xb = x.astype(BF16)
        r = x_ref[:, D_MODEL:D_MODEL + 128]
        acc = None
        for e in range(2):
            a = _dot(xb, wg_s[e])
            u = _dot(xb, wu_s[e])
            hcur = (a / (1.0 + jnp.exp(-a))) * u * r[:, 1 + e:2 + e]
            y = _dot(hcur.astype(BF16), wd_s[e])
            acc = y if acc is None else acc + y
        o_ref[...] = _layer_norm(ALPHA * x + acc, g_ref[...], b_ref[...])

    @pl.when(t >= nt_ref[0])
    def _():
        o_ref[...] = jnp.zeros_like(o_ref)


def _moe(hx, cnt, layer, w_gate, w_up, w_down, ln_g, ln_b):
    N = hx.shape[0]
    tm = TM_MOE
    n_tiles = N // tm + N_BUCKETS
    n_pad = n_tiles * tm
    bucket = hx[:, D_MODEL].astype(jnp.int32)
    rank = hx[:, D_MODEL + 3].astype(jnp.int32)
    counts = cnt[:N_BUCKETS, 0].astype(jnp.int32)
    padded = ((counts + tm - 1) // tm) * tm
    ends = jnp.cumsum(padded)
    offs = ends - padded
    b2 = bucket.reshape(-1, 128)
    off2 = functools.reduce(lambda acc, b: jnp.where(b2 == b, offs[b], acc), range(N_BUCKETS),
                            jnp.zeros_like(b2))
    dest = off2.reshape(-1) + rank
    src = (jnp.arange(n_pad, dtype=jnp.int32) % N).at[dest].set(
        jnp.arange(N, dtype=jnp.int32), mode="promise_in_bounds", unique_indices=True)
    tile_start = jnp.arange(n_tiles, dtype=jnp.int32) * tm
    n_used = (ends[-1] // tm).astype(jnp.int32)
    tb = jnp.sum((tile_start[:, None] >= ends[None, :]).astype(jnp.int32), axis=1)
    tb_last = jnp.take(tb, jnp.maximum(n_used - 1, 0))
    tb = jnp.where(tile_start < ends[-1], tb, tb_last)
    pairs = jnp.asarray(_PAIRS, jnp.int32)
    elo = (tb // 6) * EPG + pairs[tb % 6, 0]
    ehi = (tb // 6) * EPG + pairs[tb % 6, 1]
    chg = jnp.concatenate([jnp.ones((1,), jnp.int32), (tb[1:] != tb[:-1]).astype(jnp.int32)])
    xs = hx.at[src].get(mode="promise_in_bounds")

    row = lambda t, *_: (t, 0)
    fix = lambda t, *_: (0, 0)
    wlo = lambda t, elo, ehi, chg, nt: (layer, elo[t], 0, 0)
    whi = lambda t, elo, ehi, chg, nt: (layer, ehi[t], 0, 0)
    up_spec = lambda im: pl.BlockSpec((1, 1, D_MODEL, E_HID), im)
    dn_spec = lambda im: pl.BlockSpec((1, 1, E_HID, D_MODEL), im)
    grid_spec = pltpu.PrefetchScalarGridSpec(
        num_scalar_prefetch=4,
        grid=(n_tiles,),
        in_specs=[pl.BlockSpec((tm, D_MODEL + 128), row),
                  up_spec(wlo), up_spec(wlo), dn_spec(wlo),
                  up_spec(whi), up_spec(whi), dn_spec(whi),
                  pl.BlockSpec((1, D_MODEL), fix), pl.BlockSpec((1, D_MODEL), fix)],
        out_specs=pl.BlockSpec((tm, D_MODEL), row),
        scratch_shapes=[pltpu.VMEM((2, D_MODEL, E_HID), BF16),
                        pltpu.VMEM((2, D_MODEL, E_HID), BF16),
                        pltpu.VMEM((2, E_HID, D_MODEL), BF16)])
    out_sorted = pl.pallas_call(
        _moe_kernel,
        grid_spec=grid_spec,
        out_shape=jax.ShapeDtypeStruct((n_pad, D_MODEL), F32),
        compiler_params=_params(("arbitrary",)),
        name="moe",
    )(elo, ehi, chg, n_used.reshape(1), xs, w_gate, w_up, w_down, w_gate, w_up, w_down,
      ln_g.reshape(1, -1), ln_b.reshape(1, -1))
    return out_sorted.at[dest].get(mode="promise_in_bounds", unique_indices=True)


def _c_proj_kernel(x_ref, w_ref, gb_ref, q_ref, kc_ref, kv_ref, g_ref, tmp_ref):
    xb = x_ref[...].astype(BF16)
    q_ref[...] = (_dot(xb, w_ref[:, 0:C_W]) * (C_DIM ** -0.5 * LOG2E)).astype(BF16)
    for i in range(4):
        tmp_ref[0] = _dot(xb, w_ref[:, C_W + i * 128:C_W + (i + 1) * 128])

        def put(cols, piece, i=i):
            kc_ref[i, :, cols] = piece.astype(BF16)

        _to_residue_layout(tmp_ref, put, CMP_STRIDE, col_of=lambda d, r: r)
    kv_ref[...] = _dot(xb, w_ref[:, C_W + 512:C_W + 1536]).astype(BF16)
    z = _dot(xb, w_ref[:, C_W + 1536:C_W + 1792]) + gb_ref[...]
    g_ref[...] = 1.0 / (1.0 + jnp.exp(-z))


def _c_proj(x2, w_pad, gb_pad):
    N = x2.shape[0]
    tm = TM
    row = lambda i: (i, 0)
    fix = lambda i: (0, 0)
    wcols = w_pad.shape[1]
    return pl.pallas_call(
        _c_proj_kernel,
        grid=(N // tm,),
        in_specs=[pl.BlockSpec((tm, D_MODEL), row), pl.BlockSpec((D_MODEL, wcols), fix),
                  pl.BlockSpec((1, 256), fix)],
        out_specs=[pl.BlockSpec((tm, C_W), row),
                   pl.BlockSpec((4, tm // CMP_STRIDE, CMP_STRIDE * C_DIM), lambda i: (0, i, 0)),
                   pl.BlockSpec((tm, 1024), row), pl.BlockSpec((tm, 256), row)],
        out_shape=[jax.ShapeDtypeStruct((N, C_W), BF16),
                   jax.ShapeDtypeStruct((4, N // CMP_STRIDE, CMP_STRIDE * C_DIM), BF16),
                   jax.ShapeDtypeStruct((N, 1024), BF16), jax.ShapeDtypeStruct((N, 256), F32)],
        scratch_shapes=[pltpu.VMEM((1, tm, 128), F32)],
        compiler_params=_params(("parallel",)),
        name="c_proj",
    )(x2, w_pad, gb_pad)


def _c_weights(w_in, gate_b):
    gcols = []
    gb = []
    for g in range(C_GROUPS):
        idx = [C_PROJ - 3 * C_HEADS + br * C_HEADS + g * C_HPG + j for br in range(3) for j in range(C_HPG)]
        gcols.append(jnp.pad(w_in[:, np.asarray(idx)], ((0, 0), (0, 128 - len(idx)))))
        gb.append(jnp.pad(gate_b[np.asarray(idx) - (C_PROJ - 3 * C_HEADS)], (0, 128 - len(idx))))
    w = jnp.concatenate([w_in[:, :C_PROJ - 3 * C_HEADS]] + gcols, axis=1).astype(BF16)
    return w, jnp.concatenate(gb).reshape(1, 256).astype(F32)


def _compress_kernel(seg_ref, w1_ref, pos_ref, w1f_ref, w2_ref, o_ref):
    n_seg = seg_ref.shape[1]
    ul = _dot(seg_ref[0], w1_ref[0])
    u = ul[:, :CMP_HIDDEN]
    lnext = pltpu.roll(ul[:, CMP_HIDDEN:], n_seg - 1, 0)
    cpos = _dot(pos_ref[0], w1f_ref[0])[0:1, :]
    pre = u + lnext + cpos
    act = 0.5 * pre * (1.0 + jnp.tanh(math.sqrt(2.0 / math.pi) * (pre + 0.044715 * pre * pre * pre)))
    o_ref[0, 0, 0:CMP_PAD, :] = jnp.zeros((CMP_PAD, C_DIM), BF16)
    o_ref[0, 0, CMP_PAD:CMP_PAD + n_seg, :] = _dot(act.astype(BF16), w2_ref[0]).astype(BF16)


def _compress(kc, cmp_pos, cmp_w1, cmp_w2, B, S):
    n_seg = S // CMP_STRIDE
    half = CMP_STRIDE * C_DIM
    seg = kc.reshape(4 * B, n_seg, half)
    w1 = cmp_w1.astype(BF16)
    w1_ul = jnp.concatenate([w1[:, :half], w1[:, half:]], axis=2)
    pos = jnp.broadcast_to(cmp_pos.reshape(2, 1, CMP_BLOCK * C_DIM), (2, 8, CMP_BLOCK * C_DIM)).astype(BF16)
    out = pl.pallas_call(
        _compress_kernel,
        grid=(4, B),
        in_specs=[pl.BlockSpec((1, n_seg, half), lambda i, b: (i * B + b, 0, 0)),
                  pl.BlockSpec((1, half, 2 * CMP_HIDDEN), lambda i, b: (i // 2, 0, 0)),
                  pl.BlockSpec((1, 8, CMP_BLOCK * C_DIM), lambda i, b: (i // 2, 0, 0)),
                  pl.BlockSpec((1, CMP_BLOCK * C_DIM, CMP_HIDDEN), lambda i, b: (i // 2, 0, 0)),
                  pl.BlockSpec((1, CMP_HIDDEN, C_DIM), lambda i, b: (i // 2, 0, 0))],
        out_specs=pl.BlockSpec((1, 1, CMP_PAD + n_seg, C_DIM), lambda i, b: (i, b, 0, 0)),
        out_shape=jax.ShapeDtypeStruct((4, B, CMP_PAD + n_seg, C_DIM), BF16),
        compiler_params=_params(("parallel", "parallel")),
        name="compress",
    )(seg, w1_ul, pos, w1, cmp_w2.astype(BF16))
    return out


def _overlap_np(n_cmp_pad, n_slc):
    i = np.arange(n_cmp_pad)[:, None] - CMP_PAD
    m = np.arange(n_slc)[None, :]
    start = i * CMP_STRIDE
    ov = (start < (m + 1) * SLC_BLOCK) & (start + CMP_BLOCK - 1 >= m * SLC_BLOCK) & (i >= 0)
    return ov.astype(np.float32)


def _nsa_kernel(q_ref, ks_ref, vs_ref, kw_ref, vw_ref, kc_ref, vc_ref, ov_ref, mk_ref,
                tsel_ref, twin_ref, tcmp_ref, g_ref, o_ref,
                m_ref, acc_ref, p_ref, a_ref, pc_ref, pw_ref, sa_ref, sb_ref, oc_ref, ow_ref,
                *, n_sel_tab, top_n):
    qb = pl.program_id(2)
    H = C_HPG
    NB = ov_ref.shape[1]
    q_all = jnp.concatenate([q_ref[:, h * C_DIM:(h + 1) * C_DIM] for h in range(H)], axis=0)

    def reset():
        m_ref[...] = jnp.full(m_ref.shape, NEG, F32)
        acc_ref[...] = jnp.zeros(acc_ref.shape, F32)

    def softmax_rows(s):
        nw = s.shape[1] // 128
        smax = functools.reduce(jnp.maximum, [s[:, i * 128:(i + 1) * 128] for i in range(nw)])
        m = jnp.broadcast_to(jnp.max(smax, axis=1, keepdims=True), (TQ, 128))
        return jnp.exp2(s - jnp.concatenate([m] * nw, axis=1)), m

    t0 = qb // CMP_CLASSES + jnp.where(qb % CMP_CLASSES >= CMP_SPLIT, 1, 0)
    n_ct = kc_ref.shape[2] // TQ
    wc = n_ct * TQ
    tile_kind = [3] + [jnp.where(t == t0, 0, jnp.where(t == t0 + 1, 1, jnp.where(t < t0, 2, 3)))
                       for t in range(1, n_ct)]
    s_c = _dot_nt(q_all, kc_ref[0, 0])
    vext_c = jnp.concatenate([vc_ref[0, 0], ov_ref[...]], axis=1)
    inv_c = []
    for h in range(H):
        bias = jnp.concatenate([tcmp_ref[0, tile_kind[t], h] for t in range(n_ct)], axis=1)
        p, m = softmax_rows(s_c[h * TQ:(h + 1) * TQ] + bias)
        l = jnp.sum(p, axis=1, keepdims=True)
        inv_c.append(jnp.where(m > 0.5 * NEG, 1.0 / l, 0.0))
        pc_ref[h * TQ:(h + 1) * TQ, :] = p.astype(BF16)
    res_c = _dot(pc_ref[...], vext_c)
    imp = None
    for h in range(H):
        r = res_c[h * TQ:(h + 1) * TQ] * jnp.concatenate([inv_c[h], inv_c[h]], axis=1)
        oc_ref[h * TQ:(h + 1) * TQ, :] = r[:, 0:C_DIM]
        imp = r[:, C_DIM:] if imp is None else imp + r[:, C_DIM:]

    def window_branch():
        n_wt = WIN // TQ + 1
        ww = n_wt * TQ
        st = jnp.maximum(qb - (n_wt - 1), 0)
        r0w = pl.multiple_of(st * TQ, TQ)
        s_w = _dot_nt(q_all, kw_ref[pl.ds(r0w, ww), :])
        vext_w = jnp.concatenate([vw_ref[pl.ds(r0w, ww), :], jnp.ones((ww, C_DIM), BF16)], axis=1)
        widx = [jnp.maximum(qb - (st + c) + 1, 0) for c in range(n_wt)]
        for h in range(H):
            bias = jnp.concatenate([twin_ref[i, h] for i in widx], axis=1)
            p, _ = softmax_rows(s_w[h * TQ:(h + 1) * TQ] + bias)
            pw_ref[h * TQ:(h + 1) * TQ, :] = p.astype(BF16)
        res_w = _dot(pw_ref[...], vext_w)
        ow_ref[...] = res_w[:, 0:C_DIM] / res_w[:, C_DIM:]

    window_branch()

    shift = SLC_BLOCK.bit_length() - 1
    qpos = qb * TQ + lax.broadcasted_iota(jnp.int32, (NB, TQ), 1)
    mblk = lax.broadcasted_iota(jnp.int32, (NB, TQ), 0)
    qblk = jnp.right_shift(qpos, shift)
    forced = (mblk == 0) | (mblk == qblk) | (mblk == qblk - 1)
    score_t = jnp.where(forced, -3e38, jnp.where(jnp.left_shift(mblk, shift) <= qpos, imp.T, NEG))
    blk_t = mblk.astype(F32)
    sel_t = jnp.where(forced, 1.0, 0.0)
    for _ in range(top_n - 3):
        mx = jnp.max(score_t, axis=0, keepdims=True)
        idx = jnp.min(jnp.where(score_t == mx, blk_t, float(NB)), axis=0, keepdims=True)
        pick = blk_t == idx
        sel_t = jnp.where(pick, 1.0, sel_t)
        score_t = jnp.where(pick, -3e38, score_t)
    unsel = (1.0 - sel_t.T).astype(BF16)

    reset()
    NT = 4
    TK = NT * TQ
    q_aug = jnp.concatenate([q_all, jnp.concatenate([unsel] * H, axis=0)], axis=1)
    ones_k = jnp.ones((TK, C_DIM), BF16)
    n_steps = qb // NT + 1

    def sel_logits(kq, s_ref, near):
        kc = jnp.minimum(kq, n_steps - 1)
        r0 = pl.multiple_of(kc * TK, TK)
        k_aug = jnp.concatenate([ks_ref[pl.ds(r0, TK), :], mk_ref[pl.ds(r0, TK), :]], axis=1)
        s = _dot_nt(q_aug, k_aug)
        if not near:
            s_ref[...] = s
            return
        idx = [jnp.where(kq < n_steps, jnp.clip(qb - (NT * kc + c) + 1, 0, n_sel_tab - 1), 0)
               for c in range(NT)]
        for h in range(H):
            bias = jnp.concatenate([tsel_ref[i, h] for i in idx], axis=1)
            s_ref[h * TQ:(h + 1) * TQ, :] = s[h * TQ:(h + 1) * TQ] + bias

    def sel_softmax(s_ref, slot):
        for h in range(H):
            rs = slice(h * TQ, (h + 1) * TQ)
            s = s_ref[rs, :]
            m_prev = m_ref[rs, :]
            smax = functools.reduce(jnp.maximum, [s[:, i * 128:(i + 1) * 128] for i in range(NT)])
            m_new = jnp.maximum(m_prev, jnp.max(smax, axis=1, keepdims=True))
            a_ref[slot, rs, :] = jnp.exp2(m_prev - m_new)
            p_ref[slot, rs, :] = jnp.exp2(s - jnp.concatenate([m_new] * NT, axis=1)).astype(BF16)
            m_ref[rs, :] = m_new

    def sel_pv(kq, slot):
        r0 = pl.multiple_of(jnp.clip(kq, 0, n_steps - 1) * TK, TK)
        vext = jnp.concatenate([vs_ref[pl.ds(r0, TK), :], ones_k], axis=1)
        a = a_ref[slot]
        acc_ref[...] = jnp.concatenate([a, a], axis=1) * acc_ref[...] + _dot(p_ref[slot], vext)

    def sel_run(k_lo, count, near):
        @pl.when(count > 0)
        def _():
            sel_logits(k_lo, sa_ref, near)

        def body(j, carry):
            k = k_lo + 2 * j
            sel_logits(k + 1, sb_ref, near)
            sel_softmax(sa_ref, 0)
            sel_pv(k, 0)
            sel_logits(k + 2, sa_ref, near)
            sel_softmax(sb_ref, 1)
            sel_pv(k + 1, 1)
            return carry

        lax.fori_loop(0, count // 2, body, 0)

        @pl.when(count % 2 == 1)
        def _():
            sel_softmax(sa_ref, 0)
            sel_pv(k_lo + count - 1, 0)

    n_far = jnp.maximum((qb + 1 - (n_sel_tab - 2)) // NT, 0)
    sel_run(0, n_far, False)
    sel_run(n_far, n_steps - n_far, True)
    g = g_ref[...]
    for h in range(H):
        rs = slice(h * TQ, (h + 1) * TQ)
        out_s = acc_ref[rs, 0:C_DIM] / acc_ref[rs, C_DIM:2 * C_DIM]
        o = (g[:, h:h + 1] * oc_ref[rs, :] + g[:, H + h:H + h + 1] * out_s
             + g[:, 2 * H + h:2 * H + h + 1] * ow_ref[rs, :])
        o_ref[:, h * C_DIM:(h + 1) * C_DIM] = o.astype(BF16)


def _nsa_attention(q, kv, kvc, gates, tsel, twin, tcmp, B, S):
    N = B * S
    QT = S // TQ
    n_slc = S // SLC_BLOCK
    n_cmp_pad = kvc.shape[2]
    NB = 128
    assert 3 <= n_slc <= NB and n_cmp_pad % TQ == 0 and QT % 4 == 0
    ov = jnp.asarray(_overlap_np(n_cmp_pad, NB), BF16)
    mk = jnp.asarray(np.where(np.arange(S)[:, None] // SLC_BLOCK == np.arange(NB)[None, :], NEG, 0.0), BF16)
    n_sel_delta = tsel.shape[0]
    kvspec = lambda c: pl.BlockSpec((S, C_DIM), lambda b, g, t: (b, c + g))
    cspec = lambda kvi: pl.BlockSpec((1, 1, n_cmp_pad, C_DIM), lambda b, g, t: (kvi * 2 + g, b, 0, 0))
    return pl.pallas_call(
        functools.partial(_nsa_kernel, n_sel_tab=n_sel_delta, top_n=min(SLC_TOP_N, n_slc)),
        grid=(B, C_GROUPS, QT),
        in_specs=[pl.BlockSpec((TQ, C_HPG * C_DIM), lambda b, g, t: (b * QT + t, g)),
                  kvspec(0), kvspec(2), kvspec(4), kvspec(6),
                  cspec(0), cspec(1),
                  pl.BlockSpec((n_cmp_pad, NB), lambda b, g, t: (0, 0)),
                  pl.BlockSpec((S, NB), lambda b, g, t: (0, 0)),
                  pl.BlockSpec((n_sel_delta, C_HPG, TQ, TQ), lambda b, g, t: (0, g, 0, 0)),
                  pl.BlockSpec((twin.shape[0], C_HPG, TQ, TQ), lambda b, g, t: (0, g, 0, 0)),
                  pl.BlockSpec((1, 4, C_HPG, TQ, TQ), lambda b, g, t: (t % CMP_CLASSES, 0, g, 0, 0)),
                  pl.BlockSpec((TQ, 128), lambda b, g, t: (b * QT + t, g))],
        out_specs=pl.BlockSpec((TQ, C_HPG * C_DIM), lambda b, g, t: (b * QT + t, g)),
        out_shape=jax.ShapeDtypeStruct((N, C_W), BF16),
        scratch_shapes=[pltpu.VMEM((C_HPG * TQ, 128), F32),
                        pltpu.VMEM((C_HPG * TQ, 2 * C_DIM), F32),
                        pltpu.VMEM((2, C_HPG * TQ, 4 * TQ), BF16),
                        pltpu.VMEM((2, C_HPG * TQ, 128), F32),
                        pltpu.VMEM((C_HPG * TQ, n_cmp_pad), BF16),
                        pltpu.VMEM((C_HPG * TQ, WIN + TQ), BF16),
                        pltpu.VMEM((C_HPG * TQ, 4 * TQ), F32), pltpu.VMEM((C_HPG * TQ, 4 * TQ), F32),
                        pltpu.VMEM((C_HPG * TQ, C_DIM), F32), pltpu.VMEM((C_HPG * TQ, C_DIM), F32)],
        compiler_params=_params(("parallel", "parallel", "arbitrary")),
        name="nsa",
    )(q, kv, kv, kv, kv, kvc, kvc, ov, mk, tsel, twin, tcmp, gates)


def _layer_ab(h2, B, S, w_in, gate_b, conv_w, head_g, w_out, dil_tab):
    w_pad = jnp.pad(w_in, ((0, 0), (0, AB_PAD - AB_PROJ))).astype(BF16)
    gb_pad = jnp.pad(gate_b, (0, 128 - gate_b.shape[0])).reshape(1, 128).astype(F32)
    (aq, ak, av, aq4, ak4, av4, aq16, ak16, av16,
     bq, bk, bv, bo, gates) = _ab_proj(h2, w_pad, conv_w.astype(F32), gb_pad, S)
    ya = _dilated_attention({1: (aq, ak, av), 4: (aq4, ak4, av4), 16: (aq16, ak16, av16)}, dil_tab, B, S)
    yb = _mlstm(bq, bk, bv, gates, bo, head_g, B, S)
    wo = w_out.astype(BF16)
    return [ya, yb], [A_PATTERNS[-1][1], 1], [wo[:A_W], wo[A_W:]]


def _layer_c(h2, B, S, w_in, gate_b, cmp_pos, cmp_w1, cmp_w2, w_out, tsel, twin, tcmp):
    w_pad, gb_pad = _c_weights(w_in, gate_b)
    q, kc, kv, gates = _c_proj(h2, w_pad, gb_pad)
    kvc = _compress(kc, cmp_pos, cmp_w1, cmp_w2, B, S)
    out = _nsa_attention(q, kv, kvc, gates, tsel, twin, tcmp, B, S)
    return [out], [1], [w_out.astype(BF16)]


def kernel(x, rel_bias, ln_g, ln_b, ab_w_in, ab_gate_b, ab_conv, ab_head_norm, ab_w_out,
           c_w_in, c_gate_b, c_cmp_pos, c_cmp_w1, c_cmp_w2, c_w_out,
           moe_wr_g, moe_br_g, moe_wr_e, moe_br_e, moe_w_gate, moe_w_up, moe_w_down):
    B, S, D = x.shape
    assert D == D_MODEL and S % (TM) == 0 and S % (16 * A_BLOCK) == 0
    _check_cmp_windows(S)
    dil_tab = _bias_tables(rel_bias, _dilated_idx(), shift=False)
    tsel, twin, tcmp = _nsa_tables(rel_bias)
    h = x.reshape(B * S, D)
    for layer in range(DEPTH):
        j = layer // 2
        if layer % 2 == 0:
            ys, lays, ws = _layer_ab(h, B, S, ab_w_in[j], ab_gate_b[j], ab_conv[j], ab_head_norm[j],
                                     ab_w_out[j], dil_tab)
        else:
            ys, lays, ws = _layer_c(h, B, S, c_w_in[j], c_gate_b[j], c_cmp_pos[j], c_cmp_w1[j],
                                    c_cmp_w2[j], c_w_out[j], tsel, twin, tcmp)
        wr, br = _router_weights(moe_wr_g[layer], moe_br_g[layer], moe_wr_e[layer], moe_br_e[layer])
        hx, cnt = _out_ln_route(ys, lays, ws, h, ln_g[layer, 0], ln_b[layer, 0], wr, br)
        h = _moe(hx, cnt, layer, moe_w_gate, moe_w_up, moe_w_down, ln_g[layer, 1], ln_b[layer, 1])
    return h.reshape(B, S, D)
```

```python
import functools
import math

import numpy as np
import jax
import jax.numpy as jnp
from jax import lax
from jax.experimental import pallas as pl
from jax.experimental.pallas import tpu as pltpu

F32 = jnp.float32
BF16 = jnp.bfloat16
NEG = -1e30
LOG2E = math.log2(math.e)
VMEM_LIMIT = 48 * 1024 * 1024

D_MODEL = 1024
DEPTH = 2
ALPHA = (2.0 * DEPTH) ** 0.25
LN_EPS = 1e-5
REL_BUCKETS = 32
REL_MAX_DIST = 2048

A_HEADS, A_DIM, A_W = 8, 64, 512
A_PATTERNS = ((128, 1), (512, 4), (2048, 16))
A_BLOCK = 128
B_HEADS, B_DIM, B_W = 4, 128, 512
B_CHUNK = 128
B_CONV = 4
AB_PROJ = 3592
AB_PAD = 3712

C_HEADS, C_GROUPS, C_HPG, C_DIM, C_W = 8, 2, 4, 128, 1024
CMP_BLOCK, CMP_STRIDE, CMP_HIDDEN = 32, 16, 256
SLC_BLOCK, SLC_TOP_N, WIN = 64, 16, 512
C_PROJ = 2584
TQ = 128
CMP_PAD = 128

N_GROUPS, EPG, N_EXPERTS, E_HID = 4, 4, 16, 512
N_BUCKETS = N_GROUPS * 6
TM = 512
TM_MOE = 256


def _dot(a, b):
    return jnp.dot(a, b, preferred_element_type=F32)


def _dot_nt(a, b):
    return lax.dot_general(a, b, (((1,), (1,)), ((), ())), preferred_element_type=F32)


def _params(sem):
    return pltpu.CompilerParams(dimension_semantics=sem, vmem_limit_bytes=VMEM_LIMIT)


def _bucket_np(n):
    n = np.maximum(n, 0)
    exact = REL_BUCKETS // 2
    nf = np.maximum(n, 1).astype(np.float64)
    large = exact + (np.log(nf / exact) / math.log(REL_MAX_DIST / exact)
                     * (REL_BUCKETS - exact)).astype(np.int64)
    return np.where(n < exact, n, np.minimum(large, REL_BUCKETS - 1)).astype(np.int32)


def _bias_tab_kernel(tab_ref, idx_ref, out_ref, *, shift, scale):
    R = idx_ref.shape[1]
    RC = 32

    def body(i, carry):
        r0 = pl.multiple_of(i * RC, RC)
        idx = idx_ref[0, pl.ds(r0, RC), :]
        for h in range(8):
            base = tab_ref[REL_BUCKETS - 1, h] if shift else 0.0
            val = jnp.full(idx.shape, (tab_ref[0, h] - base) * scale, F32)
            for b in range(1, REL_BUCKETS):
                val = jnp.where(idx == b, (tab_ref[b, h] - base) * scale, val)
            out_ref[0, h, pl.ds(r0, RC), :] = jnp.where(idx < 0, NEG, val)
        return carry

    lax.fori_loop(0, R // RC, body, 0)


def _bias_tables(rel_bias, idx_np, shift, scale=1.0):
    T, R, C = idx_np.shape
    return pl.pallas_call(
        functools.partial(_bias_tab_kernel, shift=shift, scale=scale),
        grid=(T,),
        in_specs=[pl.BlockSpec(memory_space=pltpu.SMEM),
                  pl.BlockSpec((1, R, C), lambda t: (t, 0, 0))],
        out_specs=pl.BlockSpec((1, 8, R, C), lambda t: (t, 0, 0, 0)),
        out_shape=jax.ShapeDtypeStruct((T, 8, R, C), F32),
        compiler_params=_params(("parallel",)),
        name="bias_tables",
    )(rel_bias.astype(F32), jnp.asarray(idx_np))


def _dilated_idx():
    qi = np.arange(A_BLOCK)[:, None]
    ki = np.arange(2 * A_BLOCK)[None, :]
    j = qi + A_BLOCK - ki
    out = []
    for window, dilation in A_PATTERNS:
        nk = window // dilation
        valid = (j >= 0) & (j <= nk)
        out.append(np.where(valid, _bucket_np(np.maximum(j, 0) * dilation), -1))
    return np.stack(out).astype(np.int32)


def _sel_idx():
    a = np.arange(TQ)[:, None]
    c = np.arange(TQ)[None, :]
    n_delta = -(-(_far_dist() + TQ) // TQ)
    out = []
    for delta in range(-1, n_delta + 1):
        dist = delta * TQ + a - c
        out.append(np.where(dist >= 0, _bucket_np(dist), -1))
    return np.stack(out).astype(np.int32)


def _far_dist():
    n = np.arange(0, 4 * REL_MAX_DIST)
    b = _bucket_np(n)
    return int(np.max(n[b < REL_BUCKETS - 1])) + 1


def _win_idx():
    a = np.arange(TQ)[:, None]
    c = np.arange(TQ)[None, :]
    out = []
    for delta in range(-1, WIN // TQ + 1):
        dist = delta * TQ + a - c
        out.append(np.where((dist >= 0) & (dist < WIN), _bucket_np(dist), -1))
    return np.stack(out).astype(np.int32)


CMP_PER_TILE = TQ // CMP_STRIDE
CMP_CLASSES = TQ // CMP_PER_TILE
CMP_SPLIT = 13


def _cmp_window_start(qb):
    return qb // CMP_CLASSES + (1 if qb % CMP_CLASSES >= CMP_SPLIT else 0)


def _cmp_idx():
    a = np.arange(TQ)[:, None]
    c = np.arange(TQ)[None, :]
    out = []
    for r in range(CMP_CLASSES):
        qb = CMP_CLASSES + r
        i0 = _cmp_window_start(qb) * TQ - CMP_PAD
        for half in range(2):
            dist = qb * TQ + a - ((i0 + half * TQ + c) * CMP_STRIDE + CMP_BLOCK - 1)
            out.append(np.where(dist >= 0, _bucket_np(dist), -1))
        out.append(np.full((TQ, TQ), REL_BUCKETS - 1))
        out.append(np.full((TQ, TQ), -1))
    return np.stack(out).astype(np.int32)


def _nsa_tables(rel_bias):
    tsel = _bias_tables(rel_bias, _sel_idx(), shift=True, scale=LOG2E)
    twin = _bias_tables(rel_bias, _win_idx(), shift=False, scale=LOG2E)
    tcmp = _bias_tables(rel_bias, _cmp_idx(), shift=True, scale=LOG2E)
    return tsel, twin, tcmp.reshape(CMP_CLASSES, 4, 8, TQ, TQ)


def _check_cmp_windows(S):
    far = _far_dist()
    for qb in range(S // TQ):
        i0 = _cmp_window_start(qb) * TQ - CMP_PAD
        s0 = qb * TQ
        assert s0 - ((i0 - 1) * CMP_STRIDE + CMP_BLOCK - 1) >= far
        assert s0 + TQ - 1 - ((i0 + 2 * TQ) * CMP_STRIDE + CMP_BLOCK - 1) < 0


def _residue_col(d, r):
    return (r % 4) * 4 + r // 4 if d == 16 else r


def _lane_chunks_store(ref3, val):
    for c in range(ref3.shape[0]):
        ref3[c] = val[:, c * 128:(c + 1) * 128]


def _to_residue_layout(src3_ref, dst, d, col_of=_residue_col):
    nc, rows, _ = src3_ref.shape
    for r in range(d):
        cb = col_of(d, r)
        for c in range(nc):
            col = (cb * nc + c) * 128
            dst(slice(col, col + 128), src3_ref[c, pl.ds(r, rows // d, stride=d), :])


def _ab_proj_kernel(x_ref, xh_ref, w_ref, cw_ref, gb_ref,
                    aq_ref, ak_ref, av_ref, aq4_ref, ak4_ref, av4_ref, aq16_ref, ak16_ref, av16_ref,
                    bq_ref, bk_ref, bv_ref, bo_ref, g_ref,
                    pre_ref, tmp_ref, *, tiles_per_seq):
    i = pl.program_id(0)
    tm = x_ref.shape[0]
    xb = x_ref[...].astype(BF16)
    for c, scale, outs in ((0, A_DIM ** -0.5, (aq_ref, aq4_ref, aq16_ref)),
                           (1, 1.0, (ak_ref, ak4_ref, ak16_ref)),
                           (2, 1.0, (av_ref, av4_ref, av16_ref))):
        val = _dot(xb, w_ref[:, c * A_W:(c + 1) * A_W]) * scale
        _lane_chunks_store(tmp_ref, val)
        outs[0][...] = val.astype(BF16)
        for d, o_ref in ((4, outs[1]), (16, outs[2])):
            def put(cols, piece, o_ref=o_ref):
                o_ref[:, cols] = piece.astype(BF16)
            _to_residue_layout(tmp_ref, put, d)
    bv_ref[...] = _dot(xb, w_ref[:, 2560:3072]).astype(BF16)
    bo_ref[...] = _dot(xb, w_ref[:, 3072:3584])
    g_ref[...] = _dot(xb, w_ref[:, 3584:AB_PAD]) + gb_ref[...]
    halo = _dot(xh_ref[...].astype(BF16), w_ref[:, 1536:2560])
    halo = jnp.where(i % tiles_per_seq == 0, 0.0, halo)
    pre_ref[0:8, :] = halo
    pre_ref[8:8 + tm, :] = _dot(xb, w_ref[:, 1536:2560])
    y = pre_ref[8:8 + tm, :] * cw_ref[B_CONV - 1:B_CONV, :]
    for k in range(B_CONV - 1):
        s = B_CONV - 1 - k
        y = y + pre_ref[8 - s:8 - s + tm, :] * cw_ref[k:k + 1, :]
    y = y / (1.0 + jnp.exp(-y))
    bq_ref[...] = (y[:, :B_W] * (B_DIM ** -0.5)).astype(BF16)
    bk_ref[...] = y[:, B_W:].astype(BF16)


def _ab_proj(x2, w_pad, conv_w, gate_b_pad, S):
    N = x2.shape[0]
    tm = TM
    tps = S // tm
    row = lambda i: (i, 0)
    fix = lambda i: (0, 0)
    lay = lambda d: [jax.ShapeDtypeStruct((N // d, d * A_W), BF16)] * 3
    lay_spec = lambda d: [pl.BlockSpec((tm // d, d * A_W), row)] * 3
    outs = lay(1) + lay(4) + lay(16) + [jax.ShapeDtypeStruct((N, 512), BF16)] * 3 + [
        jax.ShapeDtypeStruct((N, 512), F32), jax.ShapeDtypeStruct((N, 128), F32)]
    o_specs = (lay_spec(1) + lay_spec(4) + lay_spec(16) + [pl.BlockSpec((tm, 512), row)] * 4
               + [pl.BlockSpec((tm, 128), row)])
    return pl.pallas_call(
        functools.partial(_ab_proj_kernel, tiles_per_seq=tps),
        grid=(N // tm,),
        in_specs=[pl.BlockSpec((tm, D_MODEL), row),
                  pl.BlockSpec((8, D_MODEL), lambda i: (jnp.maximum(i * (tm // 8) - 1, 0), 0)),
                  pl.BlockSpec((D_MODEL, AB_PAD), fix),
                  pl.BlockSpec((B_CONV, 2 * B_W), fix),
                  pl.BlockSpec((1, 128), fix)],
        out_specs=o_specs,
        out_shape=outs,
        scratch_shapes=[pltpu.VMEM((tm + 8, 2 * B_W), F32), pltpu.VMEM((A_W // 128, tm, 128), F32)],
        compiler_params=_params(("parallel",)),
        name="ab_proj",
    )(x2, x2, w_pad, conv_w, gate_b_pad)


def _dilated_kernel(*refs, has_prev, is_last):
    if has_prev:
        q_ref, kp_ref, kc_ref, vp_ref, vc_ref, tab_ref, op_ref, lp_ref = refs[:8]
        rest = refs[8:]
    else:
        q_ref, kp_ref, kc_ref, vp_ref, vc_ref, tab_ref = refs[:6]
        rest = refs[6:]
    outs, scratch = (rest, ()) if is_last else (rest[:2], rest[2:])
    o_ref = outs[0]
    n = pl.program_id(2)
    n_sub = q_ref.shape[1] // A_BLOCK
    lane = lax.broadcasted_iota(jnp.int32, (A_BLOCK, 128), 1)
    erow = lax.broadcasted_iota(jnp.int32, (128, A_W), 0)
    ecol = lax.broadcasted_iota(jnp.int32, (128, A_W), 1)
    expand = jnp.where(erow - A_DIM == jnp.right_shift(ecol, A_DIM.bit_length() - 1), 1.0, 0.0).astype(BF16)

    def spread(t):
        hi = t.astype(BF16)
        return _dot(hi, expand) + _dot((t - hi.astype(F32)).astype(BF16), expand)

    keep_side = [jnp.where(lane < A_DIM, 1.0, 0.0).astype(BF16), jnp.where(lane < A_DIM, 0.0, 1.0).astype(BF16)]
    odd = jnp.bitwise_and(lane, 1) == 1
    ones_side = [jnp.where(odd, 0.0, 1.0).astype(BF16), jnp.where(odd, 1.0, 0.0).astype(BF16)]
    stat = (lane >= A_DIM) & (lane < A_DIM + A_HEADS)
    for sb in range(n_sub):
        rs = slice(sb * A_BLOCK, (sb + 1) * A_BLOCK)
        first = jnp.where(n == 0, NEG, 0.0) if sb == 0 else 0.0
        m_tile = jnp.zeros((A_BLOCK, 128), F32)
        l_tile = jnp.ones((A_BLOCK, 128), F32)
        unnorm = []
        for j in range(A_HEADS // 2):
            cs = slice(j * 128, (j + 1) * 128)
            q2 = q_ref[0, rs, cs]
            kc, vc = kc_ref[0, rs, cs], vc_ref[0, rs, cs]
            if sb == 0:
                kp, vp = kp_ref[0, :, cs], vp_ref[0, :, cs]
            else:
                ps = slice((sb - 1) * A_BLOCK, sb * A_BLOCK)
                kp, vp = kc_ref[0, ps, cs], vc_ref[0, ps, cs]
            k_st = jnp.concatenate([kp * keep_side[0], kc * keep_side[0],
                                    kp * keep_side[1], kc * keep_side[1]], axis=0)
            s = _dot_nt(q2, k_st)
            p_parts = []
            for side in range(2):
                h = 2 * j + side
                c0 = 2 * side * A_BLOCK
                sp = s[:, c0:c0 + A_BLOCK] + tab_ref[0, h, :, 0:A_BLOCK] + first
                sc = s[:, c0 + A_BLOCK:c0 + 2 * A_BLOCK] + tab_ref[0, h, :, A_BLOCK:2 * A_BLOCK]
                m = jnp.max(jnp.maximum(sp, sc), axis=1, keepdims=True)
                p_parts += [jnp.exp(sp - m).astype(BF16), jnp.exp(sc - m).astype(BF16)]
                m_tile = jnp.where(lane == A_DIM + h, m, m_tile)
            v_st = jnp.concatenate(
                [jnp.concatenate([v * keep_side[side], ones_side[side]], axis=1)
                 for side in range(2) for v in (vp, vc)], axis=0)
            r = _dot(jnp.concatenate(p_parts, axis=1), v_st)
            unnorm.append(r[:, 0:128])
            pair = (lane == A_DIM + 2 * j) | (lane == A_DIM + 2 * j + 1)
            l_tile = jnp.where(pair, r[:, 128:256], l_tile)
        lse = m_tile + jnp.log(l_tile)
        if has_prev:
            lp = lp_ref[0, rs, :]
            mm = jnp.maximum(lp, lse)
            wp = jnp.exp(lp - mm)
            wc = jnp.exp(lse - mm)
            tot = wp + wc
            scale_prev = jnp.where(stat, wp / tot, 0.0)
            scale_cur = jnp.where(stat, wc / (tot * l_tile), 0.0)
            lse = mm + jnp.log(tot)
        else:
            scale_cur = jnp.where(stat, 1.0 / l_tile, 0.0)
        o = jnp.concatenate(unnorm, axis=1) * spread(scale_cur)
        if has_prev:
            o = o + op_ref[0, rs, :] * spread(scale_prev)
        if is_last:
            o_ref[0, rs, :] = o.astype(o_ref.dtype)
        else:
            o_scr, l_scr = scratch[0].at[sb], scratch[1].at[sb]
            _lane_chunks_store(o_scr, o)
            l_scr[0] = jnp.where(stat, lse, 0.0)
            orows = slice(sb * (A_BLOCK // 4), (sb + 1) * (A_BLOCK // 4))

            def put_o(cols, piece, orows=orows):
                o_ref[0, orows, cols] = piece

            def put_l(cols, piece, orows=orows):
                outs[1][0, orows, cols] = piece

            _to_residue_layout(o_scr, put_o, 4)
            _to_residue_layout(l_scr, put_l, 4)


def _dilated_call(q, k, v, tab, prev, pattern_idx, dilation, B, S, is_last):
    d = dilation
    L = S // d
    nb = L // A_BLOCK
    sub = max(s for s in (8, 4, 2, 1) if nb % s == 0)
    r3 = lambda a: a.reshape(B, L, a.shape[-1])
    cur = lambda b, r, n: (b, n, r)
    prv = lambda b, r, n: (b, jnp.maximum(sub * n - 1, 0), r)
    blk = pl.BlockSpec((1, sub * A_BLOCK, A_W), cur)
    in_specs = [blk, pl.BlockSpec((1, A_BLOCK, A_W), prv), blk,
                pl.BlockSpec((1, A_BLOCK, A_W), prv), blk,
                pl.BlockSpec((1, 8, A_BLOCK, 2 * A_BLOCK), lambda b, r, n: (pattern_idx, 0, 0, 0))]
    args = [r3(q), r3(k), r3(k), r3(v), r3(v), tab]
    has_prev = prev is not None
    if has_prev:
        in_specs += [blk, pl.BlockSpec((1, sub * A_BLOCK, 128), cur)]
        args += [r3(prev[0]), r3(prev[1])]
    scratch = []
    if is_last:
        out_shape = [jax.ShapeDtypeStruct((B, L, d * A_W), BF16)]
        out_specs = [blk]
    else:
        d2, rows = 4 * d, sub * A_BLOCK // 4
        nxt = lambda b, r, n: (b, n, r)
        out_shape = [jax.ShapeDtypeStruct((B, S // d2, d2 * A_W), F32),
                     jax.ShapeDtypeStruct((B, S // d2, d2 * 128), F32)]
        out_specs = [pl.BlockSpec((1, rows, 4 * A_W), nxt), pl.BlockSpec((1, rows, 4 * 128), nxt)]
        scratch = [pltpu.VMEM((sub, A_W // 128, A_BLOCK, 128), F32),
                   pltpu.VMEM((sub, 1, A_BLOCK, 128), F32)]
    res = pl.pallas_call(
        functools.partial(_dilated_kernel, has_prev=has_prev, is_last=is_last),
        grid=(B, d, nb // sub),
        in_specs=in_specs, out_specs=out_specs, out_shape=out_shape, scratch_shapes=scratch,
        compiler_params=_params(("parallel", "parallel", "arbitrary")),
        name="dilated_d%d" % d,
    )(*args)
    return [r.reshape(-1, r.shape[-1]) for r in res]


def _dilated_attention(qkv_by_dilation, tab, B, S):
    prev = None
    for p, (window, d) in enumerate(A_PATTERNS):
        assert window // d == A_BLOCK and S % (d * A_BLOCK) == 0
        assert p == 0 or d == 4 * A_PATTERNS[p - 1][1]
        last = p == len(A_PATTERNS) - 1
        q, k, v = qkv_by_dilation[d]
        prev = _dilated_call(q, k, v, tab, prev, p, d, B, S, last)
    return prev[0]


def _split3(x):
    hi = x.astype(BF16)
    r = x - hi.astype(F32)
    mid = r.astype(BF16)
    lo = (r - mid.astype(F32)).astype(BF16)
    return hi, mid, lo


def _mlstm_kernel(q_ref, k_ref, v_ref, g_ref, bo_ref, hg_ref, y_ref, c_ref, m_ref):
    L = B_CHUNK
    c = pl.program_id(1)

    @pl.when(c == 0)
    def _():
        c_ref[...] = jnp.zeros_like(c_ref)
        m_ref[...] = jnp.zeros_like(m_ref)

    lane = lax.broadcasted_iota(jnp.int32, (L, 128), 1)
    row = lax.broadcasted_iota(jnp.int32, (L, L), 0)
    col = lax.broadcasted_iota(jnp.int32, (L, L), 1)
    tri = row >= col
    is_f = (lane >= B_HEADS) & (lane < 2 * B_HEADS)
    tril = jnp.where(tri, 1.0, 0.0).astype(BF16)
    ones = jnp.ones((L, B_DIM), BF16)
    bi = 0
    g = g_ref[bi]
    logf = jnp.minimum(g, 0.0) - jnp.log(1.0 + jnp.exp(-jnp.abs(g)))
    gl = jnp.where(is_f, logf, jnp.where(lane < B_HEADS, g, 0.0))
    hi, mid, lo = _split3(gl)
    cum = _dot(tril, hi) + _dot(tril, mid) + _dot(tril, lo)
    cum_t = cum.T
    gl_t = gl.T
    for h in range(B_HEADS):
        st = h
        cs = slice(h * B_DIM, (h + 1) * B_DIM)
        q = q_ref[bi, :, cs]
        k = k_ref[bi, :, cs]
        v_ext = jnp.concatenate([v_ref[bi, :, cs], ones], axis=1)
        b_col = cum[:, B_HEADS + h:B_HEADS + h + 1]
        b_row = cum_t[B_HEADS + h:B_HEADS + h + 1, :]
        i_col = gl[:, h:h + 1]
        i_row = gl_t[h:h + 1, :]
        m_prev = m_ref[st, 0:1, 0:1]
        dm = jnp.where(tri, b_col - b_row + i_row, NEG)
        inter = b_col + m_prev
        m_t = jnp.maximum(inter, jnp.max(dm, axis=1, keepdims=True))
        p = jnp.exp(dm - m_t)
        sqk = _dot_nt(q, k) * p
        sc = jnp.exp(inter - m_t)
        lhs = jnp.concatenate([(sc * q.astype(F32)).astype(BF16), sqk.astype(BF16)], axis=1)
        c_ext = c_ref[st]
        rhs = jnp.concatenate([c_ext.astype(BF16), v_ext], axis=0)
        res = _dot(lhs, rhs)
        num = res[:, :B_DIM]
        den = res[:, B_DIM:]
        hh = num / jnp.maximum(jnp.abs(den), jnp.exp(-m_t))
        b_last = b_col[L - 1:L, :]
        gk = b_last - b_col + i_col
        m_new = jnp.maximum(b_last + m_prev, jnp.max(gk, axis=0, keepdims=True))
        wk = jnp.exp(gk - m_new)
        decay = jnp.exp(b_last + m_prev - m_new)
        kw_t = (wk * k.astype(F32)).T.astype(BF16)
        c_ref[st] = decay * c_ext + _dot(kw_t, v_ext)
        m_ref[st] = jnp.broadcast_to(m_new, (8, 128))
        mu = jnp.mean(hh, axis=1, keepdims=True)
        xc = hh - mu
        var = jnp.mean(xc * xc, axis=1, keepdims=True)
        hn = xc * lax.rsqrt(var + LN_EPS) * hg_ref[:, cs]
        bo = bo_ref[bi, :, cs]
        y_ref[bi, :, cs] = (hn / (1.0 + jnp.exp(-bo))).astype(BF16)


def _mlstm(bq, bk, bv, gates, bo, head_g, B, S):
    nc = S // B_CHUNK
    r3 = lambda a: a.reshape(B, S, a.shape[-1])
    blk = pl.BlockSpec((1, B_CHUNK, B_W), lambda b, c: (b, c, 0))
    y = pl.pallas_call(
        _mlstm_kernel,
        grid=(B, nc),
        in_specs=[blk, blk, blk,
                  pl.BlockSpec((1, B_CHUNK, 128), lambda b, c: (b, c, 0)),
                  blk,
                  pl.BlockSpec((1, B_W), lambda b, c: (0, 0))],
        out_specs=blk,
        out_shape=jax.ShapeDtypeStruct((B, S, B_W), BF16),
        scratch_shapes=[pltpu.VMEM((B_HEADS, B_DIM, 2 * B_DIM), F32),
                        pltpu.VMEM((B_HEADS, 8, 128), F32)],
        compiler_params=_params(("parallel", "arbitrary")),
        name="mlstm",
    )(r3(bq), r3(bk), r3(bv), r3(gates), r3(bo), head_g.reshape(1, B_W).astype(F32))
    return y.reshape(B * S, B_W)


def _layer_norm(z, g, b):
    mu = jnp.mean(z, axis=1, keepdims=True)
    zc = z - mu
    var = jnp.mean(zc * zc, axis=1, keepdims=True)
    return zc * lax.rsqrt(var + LN_EPS) * g + b


def _route(logits, cnt_ref):
    tm = logits.shape[0]
    lt = logits.T
    col = lambda c: lt[c:c + 1, :]
    gl = [col(c) for c in range(N_GROUPS)]
    gmax = functools.reduce(jnp.maximum, gl)
    gsum = sum(jnp.exp(x - gmax) for x in gl)
    g_idx = jnp.full(gmax.shape, N_GROUPS - 1, jnp.int32)
    for c in range(N_GROUPS - 2, -1, -1):
        g_idx = jnp.where(gl[c] == gmax, c, g_idx)
    g_w = 1.0 / gsum
    el = []
    for k in range(EPG):
        x = col(N_GROUPS + (N_GROUPS - 1) * EPG + k)
        for g in range(N_GROUPS - 2, -1, -1):
            x = jnp.where(g_idx == g, col(N_GROUPS + g * EPG + k), x)
        el.append(x)
    v1 = functools.reduce(jnp.maximum, el)
    i1 = jnp.full(v1.shape, EPG - 1, jnp.int32)
    for k in range(EPG - 2, -1, -1):
        i1 = jnp.where(el[k] == v1, k, i1)
    el2 = [jnp.where(i1 == k, -jnp.inf, el[k]) for k in range(EPG)]
    v2 = functools.reduce(jnp.maximum, el2)
    i2 = jnp.full(v2.shape, EPG - 1, jnp.int32)
    for k in range(EPG - 2, -1, -1):
        i2 = jnp.where((el2[k] == v2) & (i1 != k), k, i2)
    t = jnp.exp(v2 - v1)
    w1 = g_w / (1.0 + t)
    w2 = w1 * t
    a = jnp.minimum(i1, i2)
    b = jnp.maximum(i1, i2)
    pair = jnp.where(a == 0, b - 1, jnp.where(a == 1, b + 1, 5))
    bucket = (g_idx * 6 + pair).astype(F32)
    w_lo = jnp.where(i1 < i2, w1, w2)
    w_hi = jnp.where(i1 < i2, w2, w1)
    sub = lax.broadcasted_iota(jnp.int32, (128, tm), 0)
    onehot_t = jnp.where(sub.astype(F32) == bucket, 1.0, 0.0)
    srow = lax.broadcasted_iota(jnp.int32, (tm, tm), 0)
    scol = lax.broadcasted_iota(jnp.int32, (tm, tm), 1)
    before = jnp.where(srow < scol, 1.0, 0.0).astype(BF16)
    oh = onehot_t.astype(BF16)
    carry = cnt_ref[...]
    prior = _dot(oh, before) + jnp.concatenate([carry] * (tm // 128), axis=1)
    rank = jnp.sum(onehot_t * prior, axis=0, keepdims=True)
    cnt_ref[...] = carry + _dot(oh, jnp.ones((tm, 128), BF16))
    out_t = jnp.where(sub == 0, bucket, jnp.where(sub == 1, w_lo, jnp.where(sub == 2, w_hi,
                      jnp.where(sub == 3, rank, 0.0))))
    return out_t.T


def _out_ln_route_kernel(*refs, n_in, layouts):
    y_refs = refs[:n_in]
    w_refs = refs[n_in:2 * n_in]
    x_ref, g_ref, b_ref, wrh_ref, wrl_ref, br_ref, h_ref, c_ref, cnt_ref = refs[2 * n_in:2 * n_in + 9]
    pos_refs = list(refs[2 * n_in + 9:])
    tm = x_ref.shape[0]

    @pl.when(pl.program_id(0) == 0)
    def _():
        cnt_ref[...] = jnp.zeros_like(cnt_ref)

    y = None
    for i in range(n_in):
        d = layouts[i]
        if d == 1:
            lhs = y_refs[i][...]
        else:
            s_ref = pos_refs.pop(0)
            nc = s_ref.shape[0]
            for r in range(d):
                cb = _residue_col(d, r)
                for c in range(nc):
                    col = (cb * nc + c) * 128
                    s_ref[c, pl.ds(r, tm // d, stride=d), :] = y_refs[i][:, col:col + 128].astype(F32)
            lhs = jnp.concatenate([s_ref[c] for c in range(nc)], axis=1).astype(BF16)
        t = _dot(lhs, w_refs[i][...])
        y = t if y is None else y + t
    hn = _layer_norm(ALPHA * x_ref[...] + y, g_ref[...], b_ref[...])
    h_ref[:, 0:D_MODEL] = hn
    hi = hn.astype(BF16)
    lo = (hn - hi.astype(F32)).astype(BF16)
    logits = (_dot_nt(hi, wrh_ref[...]) + _dot_nt(lo, wrh_ref[...]) + _dot_nt(hi, wrl_ref[...])
              + br_ref[...])
    h_ref[:, D_MODEL:D_MODEL + 128] = _route(logits, cnt_ref)
    c_ref[...] = cnt_ref[...]


def _out_ln_route(ys, layouts, ws, x2, ln_g, ln_b, wr, br):
    N = x2.shape[0]
    tm = TM
    row = lambda i: (i, 0)
    fix = lambda i: (0, 0)
    wr_hi = wr.astype(BF16)
    wr_lo = (wr - wr_hi.astype(F32)).astype(BF16)
    in_specs = ([pl.BlockSpec((tm // d, y.shape[1]), row) for y, d in zip(ys, layouts)]
                + [pl.BlockSpec(w.shape, fix) for w in ws]
                + [pl.BlockSpec((tm, D_MODEL), row),
                   pl.BlockSpec((1, D_MODEL), fix), pl.BlockSpec((1, D_MODEL), fix),
                   pl.BlockSpec((128, D_MODEL), fix), pl.BlockSpec((128, D_MODEL), fix),
                   pl.BlockSpec((1, 128), fix)])
    return pl.pallas_call(
        functools.partial(_out_ln_route_kernel, n_in=len(ys), layouts=tuple(layouts)),
        grid=(N // tm,),
        in_specs=in_specs,
        out_specs=[pl.BlockSpec((tm, D_MODEL + 128), row), pl.BlockSpec((128, 128), fix)],
        out_shape=[jax.ShapeDtypeStruct((N, D_MODEL + 128), F32), jax.ShapeDtypeStruct((128, 128), F32)],
        scratch_shapes=[pltpu.VMEM((128, 128), F32)] + [
            pltpu.VMEM((y.shape[1] // d // 128, tm, 128), F32) for y, d in zip(ys, layouts) if d != 1],
        compiler_params=_params(("arbitrary",)),
        name="out_ln_route",
    )(*ys, *ws, x2, ln_g.reshape(1, -1), ln_b.reshape(1, -1), wr_hi, wr_lo, br)


def _router_weights(wr_g, br_g, wr_e, br_e):
    we = wr_e.transpose(0, 2, 1).reshape(N_GROUPS * EPG, D_MODEL)
    w = jnp.concatenate([wr_g.T, we], axis=0)
    w = jnp.pad(w, ((0, 128 - w.shape[0]), (0, 0)))
    b = jnp.concatenate([br_g, br_e.reshape(-1)])
    b = jnp.pad(b, (0, 128 - b.shape[0])).reshape(1, 128)
    return w.astype(F32), b.astype(F32)


_PAIRS = ((0, 1), (0, 2), (0, 3), (1, 2), (1, 3), (2, 3))


def _moe_kernel(elo_ref, ehi_ref, chg_ref, nt_ref,
                x_ref, wgl_ref, wul_ref, wdl_ref, wgh_ref, wuh_ref, wdh_ref,
                g_ref, b_ref, o_ref, wg_s, wu_s, wd_s):
    t = pl.program_id(0)

    @pl.when(chg_ref[t] == 1)
    def _():
        wg_s[0] = wgl_ref[0, 0].astype(BF16)
        wu_s[0] = wul_ref[0, 0].astype(BF16)
        wd_s[0] = wdl_ref[0, 0].astype(BF16)
        wg_s[1] = wgh_ref[0, 0].astype(BF16)
        wu_s[1] = wuh_ref[0, 0].astype(BF16)
        wd_s[1] = wdh_ref[0, 0].astype(BF16)

    @pl.when(t < nt_ref[0])
    def _():
        x = x_ref[:, 0:D_MODEL]
        xb = x.astype(BF16)
        r = x_ref[:, D_MODEL:D_MODEL + 128]
        acc = None
        for e in range(2):
            a = _dot(xb, wg_s[e])
            u = _dot(xb, wu_s[e])
            hcur = (a / (1.0 + jnp.exp(-a))) * u * r[:, 1 + e:2 + e]
            y = _dot(hcur.astype(BF16), wd_s[e])
            acc = y if acc is None else acc + y
        o_ref[...] = _layer_norm(ALPHA * x + acc, g_ref[...], b_ref[...])

    @pl.when(t >= nt_ref[0])
    def _():
        o_ref[...] = jnp.zeros_like(o_ref)


def _moe(hx, cnt, layer, w_gate, w_up, w_down, ln_g, ln_b):
    N = hx.shape[0]
    tm = TM_MOE
    n_tiles = N // tm + N_BUCKETS
    n_pad = n_tiles * tm
    bucket = hx[:, D_MODEL].astype(jnp.int32)
    rank = hx[:, D_MODEL + 3].astype(jnp.int32)
    counts = cnt[:N_BUCKETS, 0].astype(jnp.int32)
    padded = ((counts + tm - 1) // tm) * tm
    ends = jnp.cumsum(padded)
    offs = ends - padded
    b2 = bucket.reshape(-1, 128)
    off2 = functools.reduce(lambda acc, b: jnp.where(b2 == b, offs[b], acc), range(N_BUCKETS),
                            jnp.zeros_like(b2))
    dest = off2.reshape(-1) + rank
    src = (jnp.arange(n_pad, dtype=jnp.int32) % N).at[dest].set(
        jnp.arange(N, dtype=jnp.int32), mode="promise_in_bounds", unique_indices=True)
    tile_start = jnp.arange(n_tiles, dtype=jnp.int32) * tm
    n_used = (ends[-1] // tm).astype(jnp.int32)
    tb = jnp.sum((tile_start[:, None] >= ends[None, :]).astype(jnp.int32), axis=1)
    tb_last = jnp.take(tb, jnp.maximum(n_used - 1, 0))
    tb = jnp.where(tile_start < ends[-1], tb, tb_last)
    pairs = jnp.asarray(_PAIRS, jnp.int32)
    elo = (tb // 6) * EPG + pairs[tb % 6, 0]
    ehi = (tb // 6) * EPG + pairs[tb % 6, 1]
    chg = jnp.concatenate([jnp.ones((1,), jnp.int32), (tb[1:] != tb[:-1]).astype(jnp.int32)])
    xs = hx.at[src].get(mode="promise_in_bounds")

    row = lambda t, *_: (t, 0)
    fix = lambda t, *_: (0, 0)
    wlo = lambda t, elo, ehi, chg, nt: (layer, elo[t], 0, 0)
    whi = lambda t, elo, ehi, chg, nt: (layer, ehi[t], 0, 0)
    up_spec = lambda im: pl.BlockSpec((1, 1, D_MODEL, E_HID), im)
    dn_spec = lambda im: pl.BlockSpec((1, 1, E_HID, D_MODEL), im)
    grid_spec = pltpu.PrefetchScalarGridSpec(
        num_scalar_prefetch=4,
        grid=(n_tiles,),
        in_specs=[pl.BlockSpec((tm, D_MODEL + 128), row),
                  up_spec(wlo), up_spec(wlo), dn_spec(wlo),
                  up_spec(whi), up_spec(whi), dn_spec(whi),
                  pl.BlockSpec((1, D_MODEL), fix), pl.BlockSpec((1, D_MODEL), fix)],
        out_specs=pl.BlockSpec((tm, D_MODEL), row),
        scratch_shapes=[pltpu.VMEM((2, D_MODEL, E_HID), BF16),
                        pltpu.VMEM((2, D_MODEL, E_HID), BF16),
                        pltpu.VMEM((2, E_HID, D_MODEL), BF16)])
    out_sorted = pl.pallas_call(
        _moe_kernel,
        grid_spec=grid_spec,
        out_shape=jax.ShapeDtypeStruct((n_pad, D_MODEL), F32),
        compiler_params=_params(("arbitrary",)),
        name="moe",
    )(elo, ehi, chg, n_used.reshape(1), xs, w_gate, w_up, w_down, w_gate, w_up, w_down,
      ln_g.reshape(1, -1), ln_b.reshape(1, -1))
    return out_sorted.at[dest].get(mode="promise_in_bounds", unique_indices=True)


def _c_proj_kernel(x_ref, w_ref, gb_ref, q_ref, kc_ref, kv_ref, g_ref, tmp_ref):
    xb = x_ref[...].astype(BF16)
    q_ref[...] = (_dot(xb, w_ref[:, 0:C_W]) * (C_DIM ** -0.5 * LOG2E)).astype(BF16)
    for i in range(4):
        tmp_ref[0] = _dot(xb, w_ref[:, C_W + i * 128:C_W + (i + 1) * 128])

        def put(cols, piece, i=i):
            kc_ref[i, :, cols] = piece.astype(BF16)

        _to_residue_layout(tmp_ref, put, CMP_STRIDE, col_of=lambda d, r: r)
    kv_ref[...] = _dot(xb, w_ref[:, C_W + 512:C_W + 1536]).astype(BF16)
    z = _dot(xb, w_ref[:, C_W + 1536:C_W + 1792]) + gb_ref[...]
    g_ref[...] = 1.0 / (1.0 + jnp.exp(-z))


def _c_proj(x2, w_pad, gb_pad):
    N = x2.shape[0]
    tm = TM
    row = lambda i: (i, 0)
    fix = lambda i: (0, 0)
    wcols = w_pad.shape[1]
    return pl.pallas_call(
        _c_proj_kernel,
        grid=(N // tm,),
        in_specs=[pl.BlockSpec((tm, D_MODEL), row), pl.BlockSpec((D_MODEL, wcols), fix),
                  pl.BlockSpec((1, 256), fix)],
        out_specs=[pl.BlockSpec((tm, C_W), row),
                   pl.BlockSpec((4, tm // CMP_STRIDE, CMP_STRIDE * C_DIM), lambda i: (0, i, 0)),
                   pl.BlockSpec((tm, 1024), row), pl.BlockSpec((tm, 256), row)],
        out_shape=[jax.ShapeDtypeStruct((N, C_W), BF16),
                   jax.ShapeDtypeStruct((4, N // CMP_STRIDE, CMP_STRIDE * C_DIM), BF16),
                   jax.ShapeDtypeStruct((N, 1024), BF16), jax.ShapeDtypeStruct((N, 256), F32)],
        scratch_shapes=[pltpu.VMEM((1, tm, 128), F32)],
        compiler_params=_params(("parallel",)),
        name="c_proj",
    )(x2, w_pad, gb_pad)


def _c_weights(w_in, gate_b):
    gcols = []
    gb = []
    for g in range(C_GROUPS):
        idx = [C_PROJ - 3 * C_HEADS + br * C_HEADS + g * C_HPG + j for br in range(3) for j in range(C_HPG)]
        gcols.append(jnp.pad(w_in[:, np.asarray(idx)], ((0, 0), (0, 128 - len(idx)))))
        gb.append(jnp.pad(gate_b[np.asarray(idx) - (C_PROJ - 3 * C_HEADS)], (0, 128 - len(idx))))
    w = jnp.concatenate([w_in[:, :C_PROJ - 3 * C_HEADS]] + gcols, axis=1).astype(BF16)
    return w, jnp.concatenate(gb).reshape(1, 256).astype(F32)


def _compress_kernel(seg_ref, w1_ref, pos_ref, w1f_ref, w2_ref, o_ref):
    n_seg = seg_ref.shape[1]
    ul = _dot(seg_ref[0], w1_ref[0])
    u = ul[:, :CMP_HIDDEN]
    lnext = pltpu.roll(ul[:, CMP_HIDDEN:], n_seg - 1, 0)
    cpos = _dot(pos_ref[0], w1f_ref[0])[0:1, :]
    pre = u + lnext + cpos
    act = 0.5 * pre * (1.0 + jnp.tanh(math.sqrt(2.0 / math.pi) * (pre + 0.044715 * pre * pre * pre)))
    o_ref[0, 0, 0:CMP_PAD, :] = jnp.zeros((CMP_PAD, C_DIM), BF16)
    o_ref[0, 0, CMP_PAD:CMP_PAD + n_seg, :] = _dot(act.astype(BF16), w2_ref[0]).astype(BF16)


def _compress(kc, cmp_pos, cmp_w1, cmp_w2, B, S):
    n_seg = S // CMP_STRIDE
    half = CMP_STRIDE * C_DIM
    seg = kc.reshape(4 * B, n_seg, half)
    w1 = cmp_w1.astype(BF16)
    w1_ul = jnp.concatenate([w1[:, :half], w1[:, half:]], axis=2)
    pos = jnp.broadcast_to(cmp_pos.reshape(2, 1, CMP_BLOCK * C_DIM), (2, 8, CMP_BLOCK * C_DIM)).astype(BF16)
    out = pl.pallas_call(
        _compress_kernel,
        grid=(4, B),
        in_specs=[pl.BlockSpec((1, n_seg, half), lambda i, b: (i * B + b, 0, 0)),
                  pl.BlockSpec((1, half, 2 * CMP_HIDDEN), lambda i, b: (i // 2, 0, 0)),
                  pl.BlockSpec((1, 8, CMP_BLOCK * C_DIM), lambda i, b: (i // 2, 0, 0)),
                  pl.BlockSpec((1, CMP_BLOCK * C_DIM, CMP_HIDDEN), lambda i, b: (i // 2, 0, 0)),
                  pl.BlockSpec((1, CMP_HIDDEN, C_DIM), lambda i, b: (i // 2, 0, 0))],
        out_specs=pl.BlockSpec((1, 1, CMP_PAD + n_seg, C_DIM), lambda i, b: (i, b, 0, 0)),
        out_shape=jax.ShapeDtypeStruct((4, B, CMP_PAD + n_seg, C_DIM), BF16),
        compiler_params=_params(("parallel", "parallel")),
        name="compress",
    )(seg, w1_ul, pos, w1, cmp_w2.astype(BF16))
    return out


def _overlap_np(n_cmp_pad, n_slc):
    i = np.arange(n_cmp_pad)[:, None] - CMP_PAD
    m = np.arange(n_slc)[None, :]
    start = i * CMP_STRIDE
    ov = (start < (m + 1) * SLC_BLOCK) & (start + CMP_BLOCK - 1 >= m * SLC_BLOCK) & (i >= 0)
    return ov.astype(np.float32)


def _nsa_kernel(q_ref, ks_ref, vs_ref, kw_ref, vw_ref, kc_ref, vc_ref, ov_ref, mk_ref,
                tsel_ref, twin_ref, tcmp_ref, g_ref, o_ref,
                m_ref, acc_ref, p_ref, a_ref, pc_ref, pw_ref, sa_ref, sb_ref, oc_ref, ow_ref,
                *, n_sel_tab, top_n):
    qb = pl.program_id(2)
    H = C_HPG
    NB = ov_ref.shape[1]
    q_all = jnp.concatenate([q_ref[:, h * C_DIM:(h + 1) * C_DIM] for h in range(H)], axis=0)

    def reset():
        m_ref[...] = jnp.full(m_ref.shape, NEG, F32)
        acc_ref[...] = jnp.zeros(acc_ref.shape, F32)

    def softmax_rows(s):
        nw = s.shape[1] // 128
        smax = functools.reduce(jnp.maximum, [s[:, i * 128:(i + 1) * 128] for i in range(nw)])
        m = jnp.broadcast_to(jnp.max(smax, axis=1, keepdims=True), (TQ, 128))
        return jnp.exp2(s - jnp.concatenate([m] * nw, axis=1)), m

    t0 = qb // CMP_CLASSES + jnp.where(qb % CMP_CLASSES >= CMP_SPLIT, 1, 0)
    n_ct = kc_ref.shape[2] // TQ
    wc = n_ct * TQ
    tile_kind = [3] + [jnp.where(t == t0, 0, jnp.where(t == t0 + 1, 1, jnp.where(t < t0, 2, 3)))
                       for t in range(1, n_ct)]
    s_c = _dot_nt(q_all, kc_ref[0, 0])
    vext_c = jnp.concatenate([vc_ref[0, 0], ov_ref[...]], axis=1)
    inv_c = []
    for h in range(H):
        bias = jnp.concatenate([tcmp_ref[0, tile_kind[t], h] for t in range(n_ct)], axis=1)
        p, m = softmax_rows(s_c[h * TQ:(h + 1) * TQ] + bias)
        l = jnp.sum(p, axis=1, keepdims=True)
        inv_c.append(jnp.where(m > 0.5 * NEG, 1.0 / l, 0.0))
        pc_ref[h * TQ:(h + 1) * TQ, :] = p.astype(BF16)
    res_c = _dot(pc_ref[...], vext_c)
    imp = None
    for h in range(H):
        r = res_c[h * TQ:(h + 1) * TQ] * jnp.concatenate([inv_c[h], inv_c[h]], axis=1)
        oc_ref[h * TQ:(h + 1) * TQ, :] = r[:, 0:C_DIM]
        imp = r[:, C_DIM:] if imp is None else imp + r[:, C_DIM:]

    def window_branch():
        n_wt = WIN // TQ + 1
        ww = n_wt * TQ
        st = jnp.maximum(qb - (n_wt - 1), 0)
        r0w = pl.multiple_of(st * TQ, TQ)
        s_w = _dot_nt(q_all, kw_ref[pl.ds(r0w, ww), :])
        vext_w = jnp.concatenate([vw_ref[pl.ds(r0w, ww), :], jnp.ones((ww, C_DIM), BF16)], axis=1)
        widx = [jnp.maximum(qb - (st + c) + 1, 0) for c in range(n_wt)]
        for h in range(H):
            bias = jnp.concatenate([twin_ref[i, h] for i in widx], axis=1)
            p, _ = softmax_rows(s_w[h * TQ:(h + 1) * TQ] + bias)
            pw_ref[h * TQ:(h + 1) * TQ, :] = p.astype(BF16)
        res_w = _dot(pw_ref[...], vext_w)
        ow_ref[...] = res_w[:, 0:C_DIM] / res_w[:, C_DIM:]

    window_branch()

    shift = SLC_BLOCK.bit_length() - 1
    qpos = qb * TQ + lax.broadcasted_iota(jnp.int32, (TQ, NB), 0)
    mblk = lax.broadcasted_iota(jnp.int32, (TQ, NB), 1)
    qblk = jnp.right_shift(qpos, shift)
    forced = (mblk == 0) | (mblk == qblk) | (mblk == qblk - 1)
    score = jnp.where(forced, 3e38, jnp.where(jnp.left_shift(mblk, shift) <= qpos, imp, NEG))
    score_t = score.T
    blk_t = lax.broadcasted_iota(jnp.int32, (NB, TQ), 0).astype(F32)
    sel_t = jnp.zeros((NB, TQ), F32)
    for _ in range(top_n):
        mx = jnp.max(score_t, axis=0, keepdims=True)
        idx = jnp.min(jnp.where(score_t == mx, blk_t, float(NB)), axis=0, keepdims=True)
        pick = blk_t == idx
        sel_t = jnp.where(pick, 1.0, sel_t)
        score_t = jnp.where(pick, -3e38, score_t)
    unsel = (1.0 - sel_t.T).astype(BF16)

    reset()
    NT = 4
    TK = NT * TQ
    q_aug = jnp.concatenate([q_all, jnp.concatenate([unsel] * H, axis=0)], axis=1)
    ones_k = jnp.ones((TK, C_DIM), BF16)
    n_steps = qb // NT + 1

    def sel_logits(kq, s_ref, near):
        kc = jnp.minimum(kq, n_steps - 1)
        r0 = pl.multiple_of(kc * TK, TK)
        k_aug = jnp.concatenate([ks_ref[pl.ds(r0, TK), :], mk_ref[pl.ds(r0, TK), :]], axis=1)
        s = _dot_nt(q_aug, k_aug)
        if not near:
            s_ref[...] = s
            return
        idx = [jnp.where(kq < n_steps, jnp.clip(qb - (NT * kc + c) + 1, 0, n_sel_tab - 1), 0)
               for c in range(NT)]
        for h in range(H):
            bias = jnp.concatenate([tsel_ref[i, h] for i in idx], axis=1)
            s_ref[h * TQ:(h + 1) * TQ, :] = s[h * TQ:(h + 1) * TQ] + bias

    def sel_softmax(s_ref, slot):
        for h in range(H):
            rs = slice(h * TQ, (h + 1) * TQ)
            s = s_ref[rs, :]
            m_prev = m_ref[rs, :]
            smax = functools.reduce(jnp.maximum, [s[:, i * 128:(i + 1) * 128] for i in range(NT)])
            m_new = jnp.maximum(m_prev, jnp.max(smax, axis=1, keepdims=True))
            a_ref[slot, rs, :] = jnp.exp2(m_prev - m_new)
            p_ref[slot, rs, :] = jnp.exp2(s - jnp.concatenate([m_new] * NT, axis=1)).astype(BF16)
            m_ref[rs, :] = m_new

    def sel_pv(kq, slot):
        r0 = pl.multiple_of(jnp.clip(kq, 0, n_steps - 1) * TK, TK)
        vext = jnp.concatenate([vs_ref[pl.ds(r0, TK), :], ones_k], axis=1)
        a = a_ref[slot]
        acc_ref[...] = jnp.concatenate([a, a], axis=1) * acc_ref[...] + _dot(p_ref[slot], vext)

    def sel_run(k_lo, count, near):
        @pl.when(count > 0)
        def _():
            sel_logits(k_lo, sa_ref, near)

        def body(j, carry):
            k = k_lo + 2 * j
            sel_logits(k + 1, sb_ref, near)
            sel_softmax(sa_ref, 0)
            sel_pv(k, 0)
            sel_logits(k + 2, sa_ref, near)
            sel_softmax(sb_ref, 1)
            sel_pv(k + 1, 1)
            return carry

        lax.fori_loop(0, count // 2, body, 0)

        @pl.when(count % 2 == 1)
        def _():
            sel_softmax(sa_ref, 0)
            sel_pv(k_lo + count - 1, 0)

    n_far = jnp.maximum((qb + 1 - (n_sel_tab - 2)) // NT, 0)
    sel_run(0, n_far, False)
    sel_run(n_far, n_steps - n_far, True)
    g = g_ref[...]
    for h in range(H):
        rs = slice(h * TQ, (h + 1) * TQ)
        out_s = acc_ref[rs, 0:C_DIM] / acc_ref[rs, C_DIM:2 * C_DIM]
        o = (g[:, h:h + 1] * oc_ref[rs, :] + g[:, H + h:H + h + 1] * out_s
             + g[:, 2 * H + h:2 * H + h + 1] * ow_ref[rs, :])
        o_ref[:, h * C_DIM:(h + 1) * C_DIM] = o.astype(BF16)


def _nsa_attention(q, kv, kvc, gates, tsel, twin, tcmp, B, S):
    N = B * S
    QT = S // TQ
    n_slc = S // SLC_BLOCK
    n_cmp_pad = kvc.shape[2]
    NB = 128
    assert n_slc <= NB and n_cmp_pad % TQ == 0 and QT % 4 == 0
    ov = jnp.asarray(_overlap_np(n_cmp_pad, NB), BF16)
    mk = jnp.asarray(np.where(np.arange(S)[:, None] // SLC_BLOCK == np.arange(NB)[None, :], NEG, 0.0), BF16)
    n_sel_delta = tsel.shape[0]
    kvspec = lambda c: pl.BlockSpec((S, C_DIM), lambda b, g, t: (b, c + g))
    cspec = lambda kvi: pl.BlockSpec((1, 1, n_cmp_pad, C_DIM), lambda b, g, t: (kvi * 2 + g, b, 0, 0))
    return pl.pallas_call(
        functools.partial(_nsa_kernel, n_sel_tab=n_sel_delta, top_n=min(SLC_TOP_N, n_slc)),
        grid=(B, C_GROUPS, QT),
        in_specs=[pl.BlockSpec((TQ, C_HPG * C_DIM), lambda b, g, t: (b * QT + t, g)),
                  kvspec(0), kvspec(2), kvspec(4), kvspec(6),
                  cspec(0), cspec(1),
                  pl.BlockSpec((n_cmp_pad, NB), lambda b, g, t: (0, 0)),
                  pl.BlockSpec((S, NB), lambda b, g, t: (0, 0)),
                  pl.BlockSpec((n_sel_delta, C_HPG, TQ, TQ), lambda b, g, t: (0, g, 0, 0)),
                  pl.BlockSpec((twin.shape[0], C_HPG, TQ, TQ), lambda b, g, t: (0, g, 0, 0)),
                  pl.BlockSpec((1, 4, C_HPG, TQ, TQ), lambda b, g, t: (t % CMP_CLASSES, 0, g, 0, 0)),
                  pl.BlockSpec((TQ, 128), lambda b, g, t: (b * QT + t, g))],
        out_specs=pl.BlockSpec((TQ, C_HPG * C_DIM), lambda b, g, t: (b * QT + t, g)),
        out_shape=jax.ShapeDtypeStruct((N, C_W), BF16),
        scratch_shapes=[pltpu.VMEM((C_HPG * TQ, 128), F32),
                        pltpu.VMEM((C_HPG * TQ, 2 * C_DIM), F32),
                        pltpu.VMEM((2, C_HPG * TQ, 4 * TQ), BF16),
                        pltpu.VMEM((2, C_HPG * TQ, 128), F32),
                        pltpu.VMEM((C_HPG * TQ, n_cmp_pad), BF16),
                        pltpu.VMEM((C_HPG * TQ, WIN + TQ), BF16),
                        pltpu.VMEM((C_HPG * TQ, 4 * TQ), F32), pltpu.VMEM((C_HPG * TQ, 4 * TQ), F32),
                        pltpu.VMEM((C_HPG * TQ, C_DIM), F32), pltpu.VMEM((C_HPG * TQ, C_DIM), F32)],
        compiler_params=_params(("parallel", "parallel", "arbitrary")),
        name="nsa",
    )(q, kv, kv, kv, kv, kvc, kvc, ov, mk, tsel, twin, tcmp, gates)


def _layer_ab(h2, B, S, w_in, gate_b, conv_w, head_g, w_out, dil_tab):
    w_pad = jnp.pad(w_in, ((0, 0), (0, AB_PAD - AB_PROJ))).astype(BF16)
    gb_pad = jnp.pad(gate_b, (0, 128 - gate_b.shape[0])).reshape(1, 128).astype(F32)
    (aq, ak, av, aq4, ak4, av4, aq16, ak16, av16,
     bq, bk, bv, bo, gates) = _ab_proj(h2, w_pad, conv_w.astype(F32), gb_pad, S)
    ya = _dilated_attention({1: (aq, ak, av), 4: (aq4, ak4, av4), 16: (aq16, ak16, av16)}, dil_tab, B, S)
    yb = _mlstm(bq, bk, bv, gates, bo, head_g, B, S)
    wo = w_out.astype(BF16)
    return [ya, yb], [A_PATTERNS[-1][1], 1], [wo[:A_W], wo[A_W:]]


def _layer_c(h2, B, S, w_in, gate_b, cmp_pos, cmp_w1, cmp_w2, w_out, tsel, twin, tcmp):
    w_pad, gb_pad = _c_weights(w_in, gate_b)
    q, kc, kv, gates = _c_proj(h2, w_pad, gb_pad)
    kvc = _compress(kc, cmp_pos, cmp_w1, cmp_w2, B, S)
    out = _nsa_attention(q, kv, kvc, gates, tsel, twin, tcmp, B, S)
    return [out], [1], [w_out.astype(BF16)]


def kernel(x, rel_bias, ln_g, ln_b, ab_w_in, ab_gate_b, ab_conv, ab_head_norm, ab_w_out,
           c_w_in, c_gate_b, c_cmp_pos, c_cmp_w1, c_cmp_w2, c_w_out,
           moe_wr_g, moe_br_g, moe_wr_e, moe_br_e, moe_w_gate, moe_w_up, moe_w_down):
    B, S, D = x.shape
    assert D == D_MODEL and S % (TM) == 0 and S % (16 * A_BLOCK) == 0
    _check_cmp_windows(S)
    dil_tab = _bias_tables(rel_bias, _dilated_idx(), shift=False)
    tsel, twin, tcmp = _nsa_tables(rel_bias)
    h = x.reshape(B * S, D)
    for layer in range(DEPTH):
        j = layer // 2
        if layer % 2 == 0:
            ys, lays, ws = _layer_ab(h, B, S, ab_w_in[j], ab_gate_b[j], ab_conv[j], ab_head_norm[j],
                                     ab_w_out[j], dil_tab)
        else:
            ys, lays, ws = _layer_c(h, B, S, c_w_in[j], c_gate_b[j], c_cmp_pos[j], c_cmp_w1[j],
                                    c_cmp_w2[j], c_w_out[j], tsel, twin, tcmp)
        wr, br = _router_weights(moe_wr_g[layer], moe_br_g[layer], moe_wr_e[layer], moe_br_e[layer])
        hx, cnt = _out_ln_route(ys, lays, ws, h, ln_g[layer, 0], ln_b[layer, 0], wr, br)
        h = _moe(hx, cnt, layer, moe_w_gate, moe_w_up, moe_w_down, ln_g[layer, 1], ln_b[layer, 1])
    return h.reshape(B, S, D)
```

```python
import functools
import math

import numpy as np
import jax
import jax.numpy as jnp
from jax import lax
from jax.experimental import pallas as pl
from jax.experimental.pallas import tpu as pltpu

F32 = jnp.float32
BF16 = jnp.bfloat16
NEG = -1e30
LOG2E = math.log2(math.e)
VMEM_LIMIT = 48 * 1024 * 1024

D_MODEL = 1024
DEPTH = 2
ALPHA = (2.0 * DEPTH) ** 0.25
LN_EPS = 1e-5
REL_BUCKETS = 32
REL_MAX_DIST = 2048

A_HEADS, A_DIM, A_W = 8, 64, 512
A_PATTERNS = ((128, 1), (512, 4), (2048, 16))
A_BLOCK = 128
B_HEADS, B_DIM, B_W = 4, 128, 512
B_CHUNK = 128
B_CONV = 4
AB_PROJ = 3592
AB_PAD = 3712

C_HEADS, C_GROUPS, C_HPG, C_DIM, C_W = 8, 2, 4, 128, 1024
CMP_BLOCK, CMP_STRIDE, CMP_HIDDEN = 32, 16, 256
SLC_BLOCK, SLC_TOP_N, WIN = 64, 16, 512
C_PROJ = 2584
TQ = 128
NSA_TILES = 2
CMP_PAD = 128

N_GROUPS, EPG, N_EXPERTS, E_HID = 4, 4, 16, 512
N_BUCKETS = N_GROUPS * 6
TM = 512
TM_MOE = 256


def _dot(a, b):
    return jnp.dot(a, b, preferred_element_type=F32)


def _dot_nt(a, b):
    return lax.dot_general(a, b, (((1,), (1,)), ((), ())), preferred_element_type=F32)


def _params(sem):
    return pltpu.CompilerParams(dimension_semantics=sem, vmem_limit_bytes=VMEM_LIMIT)


def _bucket_np(n):
    n = np.maximum(n, 0)
    exact = REL_BUCKETS // 2
    nf = np.maximum(n, 1).astype(np.float64)
    large = exact + (np.log(nf / exact) / math.log(REL_MAX_DIST / exact)
                     * (REL_BUCKETS - exact)).astype(np.int64)
    return np.where(n < exact, n, np.minimum(large, REL_BUCKETS - 1)).astype(np.int32)


def _bias_tab_kernel(tab_ref, idx_ref, out_ref, *, shift, scale):
    R = idx_ref.shape[1]
    RC = 32

    def body(i, carry):
        r0 = pl.multiple_of(i * RC, RC)
        idx = idx_ref[0, pl.ds(r0, RC), :]
        for h in range(8):
            base = tab_ref[REL_BUCKETS - 1, h] if shift else 0.0
            val = jnp.full(idx.shape, (tab_ref[0, h] - base) * scale, F32)
            for b in range(1, REL_BUCKETS):
                val = jnp.where(idx == b, (tab_ref[b, h] - base) * scale, val)
            out_ref[0, h, pl.ds(r0, RC), :] = jnp.where(idx < 0, NEG, val)
        return carry

    lax.fori_loop(0, R // RC, body, 0)


def _bias_tables(rel_bias, idx_np, shift, scale=1.0):
    T, R, C = idx_np.shape
    return pl.pallas_call(
        functools.partial(_bias_tab_kernel, shift=shift, scale=scale),
        grid=(T,),
        in_specs=[pl.BlockSpec(memory_space=pltpu.SMEM),
                  pl.BlockSpec((1, R, C), lambda t: (t, 0, 0))],
        out_specs=pl.BlockSpec((1, 8, R, C), lambda t: (t, 0, 0, 0)),
        out_shape=jax.ShapeDtypeStruct((T, 8, R, C), F32),
        compiler_params=_params(("parallel",)),
        name="bias_tables",
    )(rel_bias.astype(F32), jnp.asarray(idx_np))


def _dilated_idx():
    qi = np.arange(A_BLOCK)[:, None]
    ki = np.arange(2 * A_BLOCK)[None, :]
    j = qi + A_BLOCK - ki
    out = []
    for window, dilation in A_PATTERNS:
        nk = window // dilation
        valid = (j >= 0) & (j <= nk)
        out.append(np.where(valid, _bucket_np(np.maximum(j, 0) * dilation), -1))
    return np.stack(out).astype(np.int32)


def _sel_idx():
    a = np.arange(TQ)[:, None]
    c = np.arange(TQ)[None, :]
    n_delta = -(-(_far_dist() + TQ) // TQ)
    out = []
    for delta in range(-1, n_delta + 1):
        dist = delta * TQ + a - c
        out.append(np.where(dist >= 0, _bucket_np(dist), -1))
    return np.stack(out).astype(np.int32)


def _far_dist():
    n = np.arange(0, 4 * REL_MAX_DIST)
    b = _bucket_np(n)
    return int(np.max(n[b < REL_BUCKETS - 1])) + 1


def _win_idx():
    a = np.arange(TQ)[:, None]
    c = np.arange(TQ)[None, :]
    out = []
    for delta in range(-1, WIN // TQ + 1):
        dist = delta * TQ + a - c
        out.append(np.where((dist >= 0) & (dist < WIN), _bucket_np(dist), -1))
    return np.stack(out).astype(np.int32)


CMP_PER_TILE = TQ // CMP_STRIDE
CMP_CLASSES = TQ // CMP_PER_TILE
CMP_SPLIT = 13


def _cmp_window_start(qb):
    return qb // CMP_CLASSES + (1 if qb % CMP_CLASSES >= CMP_SPLIT else 0)


def _cmp_idx():
    a = np.arange(TQ)[:, None]
    c = np.arange(TQ)[None, :]
    out = []
    for r in range(CMP_CLASSES):
        qb = CMP_CLASSES + r
        i0 = _cmp_window_start(qb) * TQ - CMP_PAD
        for half in range(2):
            dist = qb * TQ + a - ((i0 + half * TQ + c) * CMP_STRIDE + CMP_BLOCK - 1)
            out.append(np.where(dist >= 0, _bucket_np(dist), -1))
        out.append(np.full((TQ, TQ), REL_BUCKETS - 1))
        out.append(np.full((TQ, TQ), -1))
    return np.stack(out).astype(np.int32)


def _nsa_tables(rel_bias):
    tsel = _bias_tables(rel_bias, _sel_idx(), shift=True, scale=LOG2E)
    twin = _bias_tables(rel_bias, _win_idx(), shift=False, scale=LOG2E)
    tcmp = _bias_tables(rel_bias, _cmp_idx(), shift=True, scale=LOG2E)
    return tsel, twin, tcmp.reshape(CMP_CLASSES, 4, 8, TQ, TQ)


def _check_cmp_windows(S):
    far = _far_dist()
    for qb in range(S // TQ):
        i0 = _cmp_window_start(qb) * TQ - CMP_PAD
        s0 = qb * TQ
        assert s0 - ((i0 - 1) * CMP_STRIDE + CMP_BLOCK - 1) >= far
        assert s0 + TQ - 1 - ((i0 + 2 * TQ) * CMP_STRIDE + CMP_BLOCK - 1) < 0


def _residue_col(d, r):
    return (r % 4) * 4 + r // 4 if d == 16 else r


def _lane_chunks_store(ref3, val):
    for c in range(ref3.shape[0]):
        ref3[c] = val[:, c * 128:(c + 1) * 128]


def _to_residue_layout(src3_ref, dst, d, col_of=_residue_col):
    nc, rows, _ = src3_ref.shape
    for r in range(d):
        cb = col_of(d, r)
        for c in range(nc):
            col = (cb * nc + c) * 128
            dst(slice(col, col + 128), src3_ref[c, pl.ds(r, rows // d, stride=d), :])


def _ab_proj_kernel(x_ref, xh_ref, w_ref, cw_ref, gb_ref,
                    aq_ref, ak_ref, av_ref, aq4_ref, ak4_ref, av4_ref, aq16_ref, ak16_ref, av16_ref,
                    bq_ref, bk_ref, bv_ref, bo_ref, g_ref,
                    pre_ref, tmp_ref, *, tiles_per_seq):
    i = pl.program_id(0)
    tm = x_ref.shape[0]
    xb = x_ref[...].astype(BF16)
    for c, scale, outs in ((0, A_DIM ** -0.5, (aq_ref, aq4_ref, aq16_ref)),
                           (1, 1.0, (ak_ref, ak4_ref, ak16_ref)),
                           (2, 1.0, (av_ref, av4_ref, av16_ref))):
        val = _dot(xb, w_ref[:, c * A_W:(c + 1) * A_W]) * scale
        _lane_chunks_store(tmp_ref, val)
        outs[0][...] = val.astype(BF16)
        for d, o_ref in ((4, outs[1]), (16, outs[2])):
            def put(cols, piece, o_ref=o_ref):
                o_ref[:, cols] = piece.astype(BF16)
            _to_residue_layout(tmp_ref, put, d)
    bv_ref[...] = _dot(xb, w_ref[:, 2560:3072]).astype(BF16)
    bo_ref[...] = _dot(xb, w_ref[:, 3072:3584])
    g_ref[...] = _dot(xb, w_ref[:, 3584:AB_PAD]) + gb_ref[...]
    halo = _dot(xh_ref[...].astype(BF16), w_ref[:, 1536:2560])
    halo = jnp.where(i % tiles_per_seq == 0, 0.0, halo)
    pre_ref[0:8, :] = halo
    pre_ref[8:8 + tm, :] = _dot(xb, w_ref[:, 1536:2560])
    y = pre_ref[8:8 + tm, :] * cw_ref[B_CONV - 1:B_CONV, :]
    for k in range(B_CONV - 1):
        s = B_CONV - 1 - k
        y = y + pre_ref[8 - s:8 - s + tm, :] * cw_ref[k:k + 1, :]
    y = y / (1.0 + jnp.exp(-y))
    bq_ref[...] = (y[:, :B_W] * (B_DIM ** -0.5)).astype(BF16)
    bk_ref[...] = y[:, B_W:].astype(BF16)


def _ab_proj(x2, w_pad, conv_w, gate_b_pad, S):
    N = x2.shape[0]
    tm = TM
    tps = S // tm
    row = lambda i: (i, 0)
    fix = lambda i: (0, 0)
    lay = lambda d: [jax.ShapeDtypeStruct((N // d, d * A_W), BF16)] * 3
    lay_spec = lambda d: [pl.BlockSpec((tm // d, d * A_W), row)] * 3
    outs = lay(1) + lay(4) + lay(16) + [jax.ShapeDtypeStruct((N, 512), BF16)] * 3 + [
        jax.ShapeDtypeStruct((N, 512), F32), jax.ShapeDtypeStruct((N, 128), F32)]
    o_specs = (lay_spec(1) + lay_spec(4) + lay_spec(16) + [pl.BlockSpec((tm, 512), row)] * 4
               + [pl.BlockSpec((tm, 128), row)])
    return pl.pallas_call(
        functools.partial(_ab_proj_kernel, tiles_per_seq=tps),
        grid=(N // tm,),
        in_specs=[pl.BlockSpec((tm, D_MODEL), row),
                  pl.BlockSpec((8, D_MODEL), lambda i: (jnp.maximum(i * (tm // 8) - 1, 0), 0)),
                  pl.BlockSpec((D_MODEL, AB_PAD), fix),
                  pl.BlockSpec((B_CONV, 2 * B_W), fix),
                  pl.BlockSpec((1, 128), fix)],
        out_specs=o_specs,
        out_shape=outs,
        scratch_shapes=[pltpu.VMEM((tm + 8, 2 * B_W), F32), pltpu.VMEM((A_W // 128, tm, 128), F32)],
        compiler_params=_params(("parallel",)),
        name="ab_proj",
    )(x2, x2, w_pad, conv_w, gate_b_pad)


def _dilated_kernel(*refs, has_prev, is_last):
    if has_prev:
        q_ref, kp_ref, kc_ref, vp_ref, vc_ref, tab_ref, op_ref, lp_ref = refs[:8]
        rest = refs[8:]
    else:
        q_ref, kp_ref, kc_ref, vp_ref, vc_ref, tab_ref = refs[:6]
        rest = refs[6:]
    outs, scratch = (rest, ()) if is_last else (rest[:2], rest[2:])
    o_ref = outs[0]
    n = pl.program_id(2)
    n_sub = q_ref.shape[1] // A_BLOCK
    lane = lax.broadcasted_iota(jnp.int32, (A_BLOCK, 128), 1)
    erow = lax.broadcasted_iota(jnp.int32, (128, A_W), 0)
    ecol = lax.broadcasted_iota(jnp.int32, (128, A_W), 1)
    expand = jnp.where(erow - A_DIM == jnp.right_shift(ecol, A_DIM.bit_length() - 1), 1.0, 0.0).astype(BF16)

    def spread(t):
        hi = t.astype(BF16)
        return _dot(hi, expand) + _dot((t - hi.astype(F32)).astype(BF16), expand)

    keep_side = [jnp.where(lane < A_DIM, 1.0, 0.0).astype(BF16), jnp.where(lane < A_DIM, 0.0, 1.0).astype(BF16)]
    odd = jnp.bitwise_and(lane, 1) == 1
    ones_side = [jnp.where(odd, 0.0, 1.0).astype(BF16), jnp.where(odd, 1.0, 0.0).astype(BF16)]
    stat = (lane >= A_DIM) & (lane < A_DIM + A_HEADS)
    for sb in range(n_sub):
        rs = slice(sb * A_BLOCK, (sb + 1) * A_BLOCK)
        first = jnp.where(n == 0, NEG, 0.0) if sb == 0 else 0.0
        m_tile = jnp.zeros((A_BLOCK, 128), F32)
        l_tile = jnp.ones((A_BLOCK, 128), F32)
        unnorm = []
        for j in range(A_HEADS // 2):
            cs = slice(j * 128, (j + 1) * 128)
            q2 = q_ref[0, rs, cs]
            kc, vc = kc_ref[0, rs, cs], vc_ref[0, rs, cs]
            if sb == 0:
                kp, vp = kp_ref[0, :, cs], vp_ref[0, :, cs]
            else:
                ps = slice((sb - 1) * A_BLOCK, sb * A_BLOCK)
                kp, vp = kc_ref[0, ps, cs], vc_ref[0, ps, cs]
            k_st = jnp.concatenate([kp * keep_side[0], kc * keep_side[0],
                                    kp * keep_side[1], kc * keep_side[1]], axis=0)
            s = _dot_nt(q2, k_st)
            p_parts = []
            for side in range(2):
                h = 2 * j + side
                c0 = 2 * side * A_BLOCK
                sp = s[:, c0:c0 + A_BLOCK] + tab_ref[0, h, :, 0:A_BLOCK] + first
                sc = s[:, c0 + A_BLOCK:c0 + 2 * A_BLOCK] + tab_ref[0, h, :, A_BLOCK:2 * A_BLOCK]
                m = jnp.max(jnp.maximum(sp, sc), axis=1, keepdims=True)
                p_parts += [jnp.exp(sp - m).astype(BF16), jnp.exp(sc - m).astype(BF16)]
                m_tile = jnp.where(lane == A_DIM + h, m, m_tile)
            v_st = jnp.concatenate(
                [jnp.concatenate([v * keep_side[side], ones_side[side]], axis=1)
                 for side in range(2) for v in (vp, vc)], axis=0)
            r = _dot(jnp.concatenate(p_parts, axis=1), v_st)
            unnorm.append(r[:, 0:128])
            pair = (lane == A_DIM + 2 * j) | (lane == A_DIM + 2 * j + 1)
            l_tile = jnp.where(pair, r[:, 128:256], l_tile)
        lse = m_tile + jnp.log(l_tile)
        if has_prev:
            lp = lp_ref[0, rs, :]
            mm = jnp.maximum(lp, lse)
            wp = jnp.exp(lp - mm)
            wc = jnp.exp(lse - mm)
            tot = wp + wc
            scale_prev = jnp.where(stat, wp / tot, 0.0)
            scale_cur = jnp.where(stat, wc / (tot * l_tile), 0.0)
            lse = mm + jnp.log(tot)
        else:
            scale_cur = jnp.where(stat, 1.0 / l_tile, 0.0)
        o = jnp.concatenate(unnorm, axis=1) * spread(scale_cur)
        if has_prev:
            o = o + op_ref[0, rs, :] * spread(scale_prev)
        if is_last:
            o_ref[0, rs, :] = o.astype(o_ref.dtype)
        else:
            o_scr, l_scr = scratch[0].at[sb], scratch[1].at[sb]
            _lane_chunks_store(o_scr, o)
            l_scr[0] = jnp.where(stat, lse, 0.0)
            orows = slice(sb * (A_BLOCK // 4), (sb + 1) * (A_BLOCK // 4))

            def put_o(cols, piece, orows=orows):
                o_ref[0, orows, cols] = piece

            def put_l(cols, piece, orows=orows):
                outs[1][0, orows, cols] = piece

            _to_residue_layout(o_scr, put_o, 4)
            _to_residue_layout(l_scr, put_l, 4)


def _dilated_call(q, k, v, tab, prev, pattern_idx, dilation, B, S, is_last):
    d = dilation
    L = S // d
    nb = L // A_BLOCK
    sub = max(s for s in (8, 4, 2, 1) if nb % s == 0)
    r3 = lambda a: a.reshape(B, L, a.shape[-1])
    cur = lambda b, r, n: (b, n, r)
    prv = lambda b, r, n: (b, jnp.maximum(sub * n - 1, 0), r)
    blk = pl.BlockSpec((1, sub * A_BLOCK, A_W), cur)
    in_specs = [blk, pl.BlockSpec((1, A_BLOCK, A_W), prv), blk,
                pl.BlockSpec((1, A_BLOCK, A_W), prv), blk,
                pl.BlockSpec((1, 8, A_BLOCK, 2 * A_BLOCK), lambda b, r, n: (pattern_idx, 0, 0, 0))]
    args = [r3(q), r3(k), r3(k), r3(v), r3(v), tab]
    has_prev = prev is not None
    if has_prev:
        in_specs += [blk, pl.BlockSpec((1, sub * A_BLOCK, 128), cur)]
        args += [r3(prev[0]), r3(prev[1])]
    scratch = []
    if is_last:
        out_shape = [jax.ShapeDtypeStruct((B, L, d * A_W), BF16)]
        out_specs = [blk]
    else:
        d2, rows = 4 * d, sub * A_BLOCK // 4
        nxt = lambda b, r, n: (b, n, r)
        out_shape = [jax.ShapeDtypeStruct((B, S // d2, d2 * A_W), F32),
                     jax.ShapeDtypeStruct((B, S // d2, d2 * 128), F32)]
        out_specs = [pl.BlockSpec((1, rows, 4 * A_W), nxt), pl.BlockSpec((1, rows, 4 * 128), nxt)]
        scratch = [pltpu.VMEM((sub, A_W // 128, A_BLOCK, 128), F32),
                   pltpu.VMEM((sub, 1, A_BLOCK, 128), F32)]
    res = pl.pallas_call(
        functools.partial(_dilated_kernel, has_prev=has_prev, is_last=is_last),
        grid=(B, d, nb // sub),
        in_specs=in_specs, out_specs=out_specs, out_shape=out_shape, scratch_shapes=scratch,
        compiler_params=_params(("parallel", "parallel", "arbitrary")),
        name="dilated_d%d" % d,
    )(*args)
    return [r.reshape(-1, r.shape[-1]) for r in res]


def _dilated_attention(qkv_by_dilation, tab, B, S):
    prev = None
    for p, (window, d) in enumerate(A_PATTERNS):
        assert window // d == A_BLOCK and S % (d * A_BLOCK) == 0
        assert p == 0 or d == 4 * A_PATTERNS[p - 1][1]
        last = p == len(A_PATTERNS) - 1
        q, k, v = qkv_by_dilation[d]
        prev = _dilated_call(q, k, v, tab, prev, p, d, B, S, last)
    return prev[0]


def _split3(x):
    hi = x.astype(BF16)
    r = x - hi.astype(F32)
    mid = r.astype(BF16)
    lo = (r - mid.astype(F32)).astype(BF16)
    return hi, mid, lo


def _mlstm_kernel(q_ref, k_ref, v_ref, g_ref, bo_ref, hg_ref, y_ref, c_ref, m_ref):
    L = B_CHUNK
    c = pl.program_id(1)

    @pl.when(c == 0)
    def _():
        c_ref[...] = jnp.zeros_like(c_ref)
        m_ref[...] = jnp.zeros_like(m_ref)

    lane = lax.broadcasted_iota(jnp.int32, (L, 128), 1)
    row = lax.broadcasted_iota(jnp.int32, (L, L), 0)
    col = lax.broadcasted_iota(jnp.int32, (L, L), 1)
    tri = row >= col
    is_f = (lane >= B_HEADS) & (lane < 2 * B_HEADS)
    tril = jnp.where(tri, 1.0, 0.0).astype(BF16)
    ones = jnp.ones((L, B_DIM), BF16)
    bi = 0
    g = g_ref[bi]
    logf = jnp.minimum(g, 0.0) - jnp.log(1.0 + jnp.exp(-jnp.abs(g)))
    gl = jnp.where(is_f, logf, jnp.where(lane < B_HEADS, g, 0.0))
    hi, mid, lo = _split3(gl)
    cum = _dot(tril, hi) + _dot(tril, mid) + _dot(tril, lo)
    cum_t = cum.T
    gl_t = gl.T
    for h in range(B_HEADS):
        st = h
        cs = slice(h * B_DIM, (h + 1) * B_DIM)
        q = q_ref[bi, :, cs]
        k = k_ref[bi, :, cs]
        v_ext = jnp.concatenate([v_ref[bi, :, cs], ones], axis=1)
        b_col = cum[:, B_HEADS + h:B_HEADS + h + 1]
        b_row = cum_t[B_HEADS + h:B_HEADS + h + 1, :]
        i_col = gl[:, h:h + 1]
        i_row = gl_t[h:h + 1, :]
        m_prev = m_ref[st, 0:1, 0:1]
        dm = jnp.where(tri, b_col - b_row + i_row, NEG)
        inter = b_col + m_prev
        m_t = jnp.maximum(inter, jnp.max(dm, axis=1, keepdims=True))
        p = jnp.exp(dm - m_t)
        sqk = _dot_nt(q, k) * p
        sc = jnp.exp(inter - m_t)
        lhs = jnp.concatenate([(sc * q.astype(F32)).astype(BF16), sqk.astype(BF16)], axis=1)
        c_ext = c_ref[st]
        rhs = jnp.concatenate([c_ext.astype(BF16), v_ext], axis=0)
        res = _dot(lhs, rhs)
        num = res[:, :B_DIM]
        den = res[:, B_DIM:]
        hh = num / jnp.maximum(jnp.abs(den), jnp.exp(-m_t))
        b_last = b_col[L - 1:L, :]
        gk = b_last - b_col + i_col
        m_new = jnp.maximum(b_last + m_prev, jnp.max(gk, axis=0, keepdims=True))
        wk = jnp.exp(gk - m_new)
        decay = jnp.exp(b_last + m_prev - m_new)
        kw_t = (wk * k.astype(F32)).T.astype(BF16)
        c_ref[st] = decay * c_ext + _dot(kw_t, v_ext)
        m_ref[st] = jnp.broadcast_to(m_new, (8, 128))
        mu = jnp.mean(hh, axis=1, keepdims=True)
        xc = hh - mu
        var = jnp.mean(xc * xc, axis=1, keepdims=True)
        hn = xc * lax.rsqrt(var + LN_EPS) * hg_ref[:, cs]
        bo = bo_ref[bi, :, cs]
        y_ref[bi, :, cs] = (hn / (1.0 + jnp.exp(-bo))).astype(BF16)


def _mlstm(bq, bk, bv, gates, bo, head_g, B, S):
    nc = S // B_CHUNK
    r3 = lambda a: a.reshape(B, S, a.shape[-1])
    blk = pl.BlockSpec((1, B_CHUNK, B_W), lambda b, c: (b, c, 0))
    y = pl.pallas_call(
        _mlstm_kernel,
        grid=(B, nc),
        in_specs=[blk, blk, blk,
                  pl.BlockSpec((1, B_CHUNK, 128), lambda b, c: (b, c, 0)),
                  blk,
                  pl.BlockSpec((1, B_W), lambda b, c: (0, 0))],
        out_specs=blk,
        out_shape=jax.ShapeDtypeStruct((B, S, B_W), BF16),
        scratch_shapes=[pltpu.VMEM((B_HEADS, B_DIM, 2 * B_DIM), F32),
                        pltpu.VMEM((B_HEADS, 8, 128), F32)],
        compiler_params=_params(("parallel", "arbitrary")),
        name="mlstm",
    )(r3(bq), r3(bk), r3(bv), r3(gates), r3(bo), head_g.reshape(1, B_W).astype(F32))
    return y.reshape(B * S, B_W)


def _layer_norm(z, g, b):
    mu = jnp.mean(z, axis=1, keepdims=True)
    zc = z - mu
    var = jnp.mean(zc * zc, axis=1, keepdims=True)
    return zc * lax.rsqrt(var + LN_EPS) * g + b


def _route(logits, cnt_ref):
    tm = logits.shape[0]
    lt = logits.T
    col = lambda c: lt[c:c + 1, :]
    gl = [col(c) for c in range(N_GROUPS)]
    gmax = functools.reduce(jnp.maximum, gl)
    gsum = sum(jnp.exp(x - gmax) for x in gl)
    g_idx = jnp.full(gmax.shape, N_GROUPS - 1, jnp.int32)
    for c in range(N_GROUPS - 2, -1, -1):
        g_idx = jnp.where(gl[c] == gmax, c, g_idx)
    g_w = 1.0 / gsum
    el = []
    for k in range(EPG):
        x = col(N_GROUPS + (N_GROUPS - 1) * EPG + k)
        for g in range(N_GROUPS - 2, -1, -1):
            x = jnp.where(g_idx == g, col(N_GROUPS + g * EPG + k), x)
        el.append(x)
    v1 = functools.reduce(jnp.maximum, el)
    i1 = jnp.full(v1.shape, EPG - 1, jnp.int32)
    for k in range(EPG - 2, -1, -1):
        i1 = jnp.where(el[k] == v1, k, i1)
    el2 = [jnp.where(i1 == k, -jnp.inf, el[k]) for k in range(EPG)]
    v2 = functools.reduce(jnp.maximum, el2)
    i2 = jnp.full(v2.shape, EPG - 1, jnp.int32)
    for k in range(EPG - 2, -1, -1):
        i2 = jnp.where((el2[k] == v2) & (i1 != k), k, i2)
    t = jnp.exp(v2 - v1)
    w1 = g_w / (1.0 + t)
    w2 = w1 * t
    a = jnp.minimum(i1, i2)
    b = jnp.maximum(i1, i2)
    pair = jnp.where(a == 0, b - 1, jnp.where(a == 1, b + 1, 5))
    bucket = (g_idx * 6 + pair).astype(F32)
    w_lo = jnp.where(i1 < i2, w1, w2)
    w_hi = jnp.where(i1 < i2, w2, w1)
    sub = lax.broadcasted_iota(jnp.int32, (128, tm), 0)
    onehot_t = jnp.where(sub.astype(F32) == bucket, 1.0, 0.0)
    srow = lax.broadcasted_iota(jnp.int32, (tm, tm), 0)
    scol = lax.broadcasted_iota(jnp.int32, (tm, tm), 1)
    before = jnp.where(srow < scol, 1.0, 0.0).astype(BF16)
    oh = onehot_t.astype(BF16)
    carry = cnt_ref[...]
    prior = _dot(oh, before) + jnp.concatenate([carry] * (tm // 128), axis=1)
    rank = jnp.sum(onehot_t * prior, axis=0, keepdims=True)
    cnt_ref[...] = carry + _dot(oh, jnp.ones((tm, 128), BF16))
    out_t = jnp.where(sub == 0, bucket, jnp.where(sub == 1, w_lo, jnp.where(sub == 2, w_hi,
                      jnp.where(sub == 3, rank, 0.0))))
    return out_t.T


def _out_ln_route_kernel(*refs, n_in, layouts):
    y_refs = refs[:n_in]
    w_refs = refs[n_in:2 * n_in]
    x_ref, g_ref, b_ref, wrh_ref, wrl_ref, br_ref, h_ref, c_ref, cnt_ref = refs[2 * n_in:2 * n_in + 9]
    pos_refs = list(refs[2 * n_in + 9:])
    tm = x_ref.shape[0]

    @pl.when(pl.program_id(0) == 0)
    def _():
        cnt_ref[...] = jnp.zeros_like(cnt_ref)

    y = None
    for i in range(n_in):
        d = layouts[i]
        if d == 1:
            lhs = y_refs[i][...]
        else:
            s_ref = pos_refs.pop(0)
            nc = s_ref.shape[0]
            for r in range(d):
                cb = _residue_col(d, r)
                for c in range(nc):
                    col = (cb * nc + c) * 128
                    s_ref[c, pl.ds(r, tm // d, stride=d), :] = y_refs[i][:, col:col + 128].astype(F32)
            lhs = jnp.concatenate([s_ref[c] for c in range(nc)], axis=1).astype(BF16)
        t = _dot(lhs, w_refs[i][...])
        y = t if y is None else y + t
    hn = _layer_norm(ALPHA * x_ref[...] + y, g_ref[...], b_ref[...])
    h_ref[:, 0:D_MODEL] = hn
    hi = hn.astype(BF16)
    lo = (hn - hi.astype(F32)).astype(BF16)
    logits = (_dot_nt(hi, wrh_ref[...]) + _dot_nt(lo, wrh_ref[...]) + _dot_nt(hi, wrl_ref[...])
              + br_ref[...])
    h_ref[:, D_MODEL:D_MODEL + 128] = _route(logits, cnt_ref)
    c_ref[...] = cnt_ref[...]


def _out_ln_route(ys, layouts, ws, x2, ln_g, ln_b, wr, br):
    N = x2.shape[0]
    tm = TM
    row = lambda i: (i, 0)
    fix = lambda i: (0, 0)
    wr_hi = wr.astype(BF16)
    wr_lo = (wr - wr_hi.astype(F32)).astype(BF16)
    in_specs = ([pl.BlockSpec((tm // d, y.shape[1]), row) for y, d in zip(ys, layouts)]
                + [pl.BlockSpec(w.shape, fix) for w in ws]
                + [pl.BlockSpec((tm, D_MODEL), row),
                   pl.BlockSpec((1, D_MODEL), fix), pl.BlockSpec((1, D_MODEL), fix),
                   pl.BlockSpec((128, D_MODEL), fix), pl.BlockSpec((128, D_MODEL), fix),
                   pl.BlockSpec((1, 128), fix)])
    return pl.pallas_call(
        functools.partial(_out_ln_route_kernel, n_in=len(ys), layouts=tuple(layouts)),
        grid=(N // tm,),
        in_specs=in_specs,
        out_specs=[pl.BlockSpec((tm, D_MODEL + 128), row), pl.BlockSpec((128, 128), fix)],
        out_shape=[jax.ShapeDtypeStruct((N, D_MODEL + 128), F32), jax.ShapeDtypeStruct((128, 128), F32)],
        scratch_shapes=[pltpu.VMEM((128, 128), F32)] + [
            pltpu.VMEM((y.shape[1] // d // 128, tm, 128), F32) for y, d in zip(ys, layouts) if d != 1],
        compiler_params=_params(("arbitrary",)),
        name="out_ln_route",
    )(*ys, *ws, x2, ln_g.reshape(1, -1), ln_b.reshape(1, -1), wr_hi, wr_lo, br)


def _router_weights(wr_g, br_g, wr_e, br_e):
    we = wr_e.transpose(0, 2, 1).reshape(N_GROUPS * EPG, D_MODEL)
    w = jnp.concatenate([wr_g.T, we], axis=0)
    w = jnp.pad(w, ((0, 128 - w.shape[0]), (0, 0)))
    b = jnp.concatenate([br_g, br_e.reshape(-1)])
    b = jnp.pad(b, (0, 128 - b.shape[0])).reshape(1, 128)
    return w.astype(F32), b.astype(F32)


_PAIRS = ((0, 1), (0, 2), (0, 3), (1, 2), (1, 3), (2, 3))


def _moe_kernel(elo_ref, ehi_ref, chg_ref, nt_ref,
                x_ref, wgl_ref, wul_ref, wdl_ref, wgh_ref, wuh_ref, wdh_ref,
                g_ref, b_ref, o_ref, wg_s, wu_s, wd_s):
    t = pl.program_id(0)

    @pl.when(chg_ref[t] == 1)
    def _():
        wg_s[0] = wgl_ref[0, 0].astype(BF16)
        wu_s[0] = wul_ref[0, 0].astype(BF16)
        wd_s[0] = wdl_ref[0, 0].astype(BF16)
        wg_s[1] = wgh_ref[0, 0].astype(BF16)
        wu_s[1] = wuh_ref[0, 0].astype(BF16)
        wd_s[1] = wdh_ref[0, 0].astype(BF16)

    @pl.when(t < nt_ref[0])
    def _():
        x = x_ref[:, 0:D_MODEL]
        xb = x.astype(BF16)
        r = x_ref[:, D_MODEL:D_MODEL + 128]
        acc = None
        for e in range(2):
            a = _dot(xb, wg_s[e])
            u = _dot(xb, wu_s[e])
            hcur = (a / (1.0 + jnp.exp(-a))) * u * r[:, 1 + e:2 + e]
            y = _dot(hcur.astype(BF16), wd_s[e])
            acc = y if acc is None else acc + y
        o_ref[...] = _layer_norm(ALPHA * x + acc, g_ref[...], b_ref[...])

    @pl.when(t >= nt_ref[0])
    def _():
        o_ref[...] = jnp.zeros_like(o_ref)


def _moe(hx, cnt, layer, w_gate, w_up, w_down, ln_g, ln_b):
    N = hx.shape[0]
    tm = TM_MOE
    n_tiles = N // tm + N_BUCKETS
    n_pad = n_tiles * tm
    bucket = hx[:, D_MODEL].astype(jnp.int32)
    rank = hx[:, D_MODEL + 3].astype(jnp.int32)
    counts = cnt[:N_BUCKETS, 0].astype(jnp.int32)
    padded = ((counts + tm - 1) // tm) * tm
    ends = jnp.cumsum(padded)
    offs = ends - padded
    b2 = bucket.reshape(-1, 128)
    off2 = functools.reduce(lambda acc, b: jnp.where(b2 == b, offs[b], acc), range(N_BUCKETS),
                            jnp.zeros_like(b2))
    dest = off2.reshape(-1) + rank
    src = (jnp.arange(n_pad, dtype=jnp.int32) % N).at[dest].set(
        jnp.arange(N, dtype=jnp.int32), mode="promise_in_bounds", unique_indices=True)
    tile_start = jnp.arange(n_tiles, dtype=jnp.int32) * tm
    n_used = (ends[-1] // tm).astype(jnp.int32)
    tb = jnp.sum((tile_start[:, None] >= ends[None, :]).astype(jnp.int32), axis=1)
    tb_last = jnp.take(tb, jnp.maximum(n_used - 1, 0))
    tb = jnp.where(tile_start < ends[-1], tb, tb_last)
    pairs = jnp.asarray(_PAIRS, jnp.int32)
    elo = (tb // 6) * EPG + pairs[tb % 6, 0]
    ehi = (tb // 6) * EPG + pairs[tb % 6, 1]
    chg = jnp.concatenate([jnp.ones((1,), jnp.int32), (tb[1:] != tb[:-1]).astype(jnp.int32)])
    xs = hx.at[src].get(mode="promise_in_bounds")

    row = lambda t, *_: (t, 0)
    fix = lambda t, *_: (0, 0)
    wlo = lambda t, elo, ehi, chg, nt: (layer, elo[t], 0, 0)
    whi = lambda t, elo, ehi, chg, nt: (layer, ehi[t], 0, 0)
    up_spec = lambda im: pl.BlockSpec((1, 1, D_MODEL, E_HID), im)
    dn_spec = lambda im: pl.BlockSpec((1, 1, E_HID, D_MODEL), im)
    grid_spec = pltpu.PrefetchScalarGridSpec(
        num_scalar_prefetch=4,
        grid=(n_tiles,),
        in_specs=[pl.BlockSpec((tm, D_MODEL + 128), row),
                  up_spec(wlo), up_spec(wlo), dn_spec(wlo),
                  up_spec(whi), up_spec(whi), dn_spec(whi),
                  pl.BlockSpec((1, D_MODEL), fix), pl.BlockSpec((1, D_MODEL), fix)],
        out_specs=pl.BlockSpec((tm, D_MODEL), row),
        scratch_shapes=[pltpu.VMEM((2, D_MODEL, E_HID), BF16),
                        pltpu.VMEM((2, D_MODEL, E_HID), BF16),
                        pltpu.VMEM((2, E_HID, D_MODEL), BF16)])
    out_sorted = pl.pallas_call(
        _moe_kernel,
        grid_spec=grid_spec,
        out_shape=jax.ShapeDtypeStruct((n_pad, D_MODEL), F32),
        compiler_params=_params(("arbitrary",)),
        name="moe",
    )(elo, ehi, chg, n_used.reshape(1), xs, w_gate, w_up, w_down, w_gate, w_up, w_down,
      ln_g.reshape(1, -1), ln_b.reshape(1, -1))
    return out_sorted.at[dest].get(mode="promise_in_bounds", unique_indices=True)


def _c_proj_kernel(x_ref, w_ref, gb_ref, q_ref, kc_ref, kv_ref, g_ref, tmp_ref):
    xb = x_ref[...].astype(BF16)
    q_ref[...] = (_dot(xb, w_ref[:, 0:C_W]) * (C_DIM ** -0.5 * LOG2E)).astype(BF16)
    for i in range(4):
        tmp_ref[0] = _dot(xb, w_ref[:, C_W + i * 128:C_W + (i + 1) * 128])

        def put(cols, piece, i=i):
            kc_ref[i, :, cols] = piece.astype(BF16)

        _to_residue_layout(tmp_ref, put, CMP_STRIDE, col_of=lambda d, r: r)
    kv_ref[...] = _dot(xb, w_ref[:, C_W + 512:C_W + 1536]).astype(BF16)
    z = _dot(xb, w_ref[:, C_W + 1536:C_W + 1792]) + gb_ref[...]
    g_ref[...] = 1.0 / (1.0 + jnp.exp(-z))


def _c_proj(x2, w_pad, gb_pad):
    N = x2.shape[0]
    tm = TM
    row = lambda i: (i, 0)
    fix = lambda i: (0, 0)
    wcols = w_pad.shape[1]
    return pl.pallas_call(
        _c_proj_kernel,
        grid=(N // tm,),
        in_specs=[pl.BlockSpec((tm, D_MODEL), row), pl.BlockSpec((D_MODEL, wcols), fix),
                  pl.BlockSpec((1, 256), fix)],
        out_specs=[pl.BlockSpec((tm, C_W), row),
                   pl.BlockSpec((4, tm // CMP_STRIDE, CMP_STRIDE * C_DIM), lambda i: (0, i, 0)),
                   pl.BlockSpec((tm, 1024), row), pl.BlockSpec((tm, 256), row)],
        out_shape=[jax.ShapeDtypeStruct((N, C_W), BF16),
                   jax.ShapeDtypeStruct((4, N // CMP_STRIDE, CMP_STRIDE * C_DIM), BF16),
                   jax.ShapeDtypeStruct((N, 1024), BF16), jax.ShapeDtypeStruct((N, 256), F32)],
        scratch_shapes=[pltpu.VMEM((1, tm, 128), F32)],
        compiler_params=_params(("parallel",)),
        name="c_proj",
    )(x2, w_pad, gb_pad)


def _c_weights(w_in, gate_b):
    gcols = []
    gb = []
    for g in range(C_GROUPS):
        idx = [C_PROJ - 3 * C_HEADS + br * C_HEADS + g * C_HPG + j for br in range(3) for j in range(C_HPG)]
        gcols.append(jnp.pad(w_in[:, np.asarray(idx)], ((0, 0), (0, 128 - len(idx)))))
        gb.append(jnp.pad(gate_b[np.asarray(idx) - (C_PROJ - 3 * C_HEADS)], (0, 128 - len(idx))))
    w = jnp.concatenate([w_in[:, :C_PROJ - 3 * C_HEADS]] + gcols, axis=1).astype(BF16)
    return w, jnp.concatenate(gb).reshape(1, 256).astype(F32)


def _compress_kernel(seg_ref, w1_ref, pos_ref, w1f_ref, w2_ref, o_ref):
    n_seg = seg_ref.shape[1]
    ul = _dot(seg_ref[0], w1_ref[0])
    u = ul[:, :CMP_HIDDEN]
    lnext = pltpu.roll(ul[:, CMP_HIDDEN:], n_seg - 1, 0)
    cpos = _dot(pos_ref[0], w1f_ref[0])[0:1, :]
    pre = u + lnext + cpos
    act = 0.5 * pre * (1.0 + jnp.tanh(math.sqrt(2.0 / math.pi) * (pre + 0.044715 * pre * pre * pre)))
    o_ref[0, 0, 0:CMP_PAD, :] = jnp.zeros((CMP_PAD, C_DIM), BF16)
    o_ref[0, 0, CMP_PAD:CMP_PAD + n_seg, :] = _dot(act.astype(BF16), w2_ref[0]).astype(BF16)


def _compress(kc, cmp_pos, cmp_w1, cmp_w2, B, S):
    n_seg = S // CMP_STRIDE
    half = CMP_STRIDE * C_DIM
    seg = kc.reshape(4 * B, n_seg, half)
    w1 = cmp_w1.astype(BF16)
    w1_ul = jnp.concatenate([w1[:, :half], w1[:, half:]], axis=2)
    pos = jnp.broadcast_to(cmp_pos.reshape(2, 1, CMP_BLOCK * C_DIM), (2, 8, CMP_BLOCK * C_DIM)).astype(BF16)
    out = pl.pallas_call(
        _compress_kernel,
        grid=(4, B),
        in_specs=[pl.BlockSpec((1, n_seg, half), lambda i, b: (i * B + b, 0, 0)),
                  pl.BlockSpec((1, half, 2 * CMP_HIDDEN), lambda i, b: (i // 2, 0, 0)),
                  pl.BlockSpec((1, 8, CMP_BLOCK * C_DIM), lambda i, b: (i // 2, 0, 0)),
                  pl.BlockSpec((1, CMP_BLOCK * C_DIM, CMP_HIDDEN), lambda i, b: (i // 2, 0, 0)),
                  pl.BlockSpec((1, CMP_HIDDEN, C_DIM), lambda i, b: (i // 2, 0, 0))],
        out_specs=pl.BlockSpec((1, 1, CMP_PAD + n_seg, C_DIM), lambda i, b: (i, b, 0, 0)),
        out_shape=jax.ShapeDtypeStruct((4, B, CMP_PAD + n_seg, C_DIM), BF16),
        compiler_params=_params(("parallel", "parallel")),
        name="compress",
    )(seg, w1_ul, pos, w1, cmp_w2.astype(BF16))
    return out


def _overlap_np(n_cmp_pad, n_slc):
    i = np.arange(n_cmp_pad)[:, None] - CMP_PAD
    m = np.arange(n_slc)[None, :]
    start = i * CMP_STRIDE
    ov = (start < (m + 1) * SLC_BLOCK) & (start + CMP_BLOCK - 1 >= m * SLC_BLOCK) & (i >= 0)
    return ov.astype(np.float32)


def _nsa_kernel(q_ref, ks_ref, vs_ref, kw_ref, vw_ref, kc_ref, vc_ref, ov_ref, mk_ref,
                tsel_ref, twin_ref, tcmp_ref, g_ref, o_ref,
                m_ref, acc_ref, p_ref, a_ref, pc_ref, pw_ref, sa_ref, sb_ref, oc_ref, ow_ref, qa_ref,
                *, n_sel_tab, top_n):
    H = C_HPG
    NB = ov_ref.shape[1]
    n_tiles = q_ref.shape[0] // TQ
    NT = 4
    TK = NT * TQ

    def reset():
        m_ref[...] = jnp.full(m_ref.shape, NEG, F32)
        acc_ref[...] = jnp.zeros(acc_ref.shape, F32)

    def softmax_rows(s):
        nw = s.shape[1] // 128
        smax = functools.reduce(jnp.maximum, [s[:, i * 128:(i + 1) * 128] for i in range(nw)])
        m = jnp.broadcast_to(jnp.max(smax, axis=1, keepdims=True), (TQ, 128))
        return jnp.exp2(s - jnp.concatenate([m] * nw, axis=1)), m

    def single_step_branches(a):
        qb = pl.program_id(2) * n_tiles + a
        rows = slice(a * TQ, (a + 1) * TQ)
        q_all = jnp.concatenate([q_ref[rows, h * C_DIM:(h + 1) * C_DIM] for h in range(H)], axis=0)

        t0 = qb // CMP_CLASSES + jnp.where(qb % CMP_CLASSES >= CMP_SPLIT, 1, 0)
        n_ct = kc_ref.shape[2] // TQ
        tile_kind = [3] + [jnp.where(t == t0, 0, jnp.where(t == t0 + 1, 1, jnp.where(t < t0, 2, 3)))
                           for t in range(1, n_ct)]
        s_c = _dot_nt(q_all, kc_ref[0, 0])
        vext_c = jnp.concatenate([vc_ref[0, 0], ov_ref[...]], axis=1)
        inv_c = []
        for h in range(H):
            bias = jnp.concatenate([tcmp_ref[a, tile_kind[t], h] for t in range(n_ct)], axis=1)
            p, m = softmax_rows(s_c[h * TQ:(h + 1) * TQ] + bias)
            l = jnp.sum(p, axis=1, keepdims=True)
            inv_c.append(jnp.where(m > 0.5 * NEG, 1.0 / l, 0.0))
            pc_ref[a, h * TQ:(h + 1) * TQ, :] = p.astype(BF16)
        res_c = _dot(pc_ref[a], vext_c)
        imp = None
        for h in range(H):
            r = res_c[h * TQ:(h + 1) * TQ] * jnp.concatenate([inv_c[h], inv_c[h]], axis=1)
            oc_ref[a, h * TQ:(h + 1) * TQ, :] = r[:, 0:C_DIM]
            imp = r[:, C_DIM:] if imp is None else imp + r[:, C_DIM:]

        n_wt = WIN // TQ + 1
        ww = n_wt * TQ
        st = jnp.maximum(qb - (n_wt - 1), 0)
        r0w = pl.multiple_of(st * TQ, TQ)
        s_w = _dot_nt(q_all, kw_ref[pl.ds(r0w, ww), :])
        vext_w = jnp.concatenate([vw_ref[pl.ds(r0w, ww), :], jnp.ones((ww, C_DIM), BF16)], axis=1)
        widx = [jnp.maximum(qb - (st + c) + 1, 0) for c in range(n_wt)]
        for h in range(H):
            bias = jnp.concatenate([twin_ref[i, h] for i in widx], axis=1)
            p, _ = softmax_rows(s_w[h * TQ:(h + 1) * TQ] + bias)
            pw_ref[a, h * TQ:(h + 1) * TQ, :] = p.astype(BF16)
        res_w = _dot(pw_ref[a], vext_w)
        ow_ref[a] = res_w[:, 0:C_DIM] / res_w[:, C_DIM:]

        shift = SLC_BLOCK.bit_length() - 1
        qpos = qb * TQ + lax.broadcasted_iota(jnp.int32, (TQ, NB), 0)
        mblk = lax.broadcasted_iota(jnp.int32, (TQ, NB), 1)
        qblk = jnp.right_shift(qpos, shift)
        forced = (mblk == 0) | (mblk == qblk) | (mblk == qblk - 1)
        score = jnp.where(forced, 3e38, jnp.where(jnp.left_shift(mblk, shift) <= qpos, imp, NEG))
        score_t = score.T
        blk_t = lax.broadcasted_iota(jnp.int32, (NB, TQ), 0).astype(F32)
        sel_t = jnp.zeros((NB, TQ), F32)
        for _ in range(top_n):
            mx = jnp.max(score_t, axis=0, keepdims=True)
            idx = jnp.min(jnp.where(score_t == mx, blk_t, float(NB)), axis=0, keepdims=True)
            pick = blk_t == idx
            sel_t = jnp.where(pick, 1.0, sel_t)
            score_t = jnp.where(pick, -3e38, score_t)
        unsel = (1.0 - sel_t.T).astype(BF16)
        qa_ref[a] = jnp.concatenate([q_all, jnp.concatenate([unsel] * H, axis=0)], axis=1)

    for a in range(n_tiles):
        single_step_branches(a)

    ones_k = jnp.ones((TK, C_DIM), BF16)
    for a in range(n_tiles):
        _nsa_selected(a, pl.program_id(2) * n_tiles + a, ks_ref, vs_ref, mk_ref, tsel_ref, g_ref, o_ref,
                      m_ref, acc_ref, p_ref, a_ref, sa_ref, sb_ref, oc_ref, ow_ref, qa_ref, ones_k,
                      reset, n_sel_tab)


def _nsa_selected(a, qb, ks_ref, vs_ref, mk_ref, tsel_ref, g_ref, o_ref,
                  m_ref, acc_ref, p_ref, a_ref, sa_ref, sb_ref, oc_ref, ow_ref, qa_ref, ones_k,
                  reset, n_sel_tab):
    H = C_HPG
    NT = 4
    TK = NT * TQ
    reset()
    n_steps = qb // NT + 1

    def sel_logits(kq, s_ref, near):
        kc = jnp.minimum(kq, n_steps - 1)
        r0 = pl.multiple_of(kc * TK, TK)
        k_aug = jnp.concatenate([ks_ref[pl.ds(r0, TK), :], mk_ref[pl.ds(r0, TK), :]], axis=1)
        s = _dot_nt(qa_ref[a], k_aug)
        if not near:
            s_ref[...] = s
            return
        idx = [jnp.where(kq < n_steps, jnp.clip(qb - (NT * kc + c) + 1, 0, n_sel_tab - 1), 0)
               for c in range(NT)]
        for h in range(H):
            bias = jnp.concatenate([tsel_ref[i, h] for i in idx], axis=1)
            s_ref[h * TQ:(h + 1) * TQ, :] = s[h * TQ:(h + 1) * TQ] + bias

    def sel_softmax(s_ref, slot):
        for h in range(H):
            rs = slice(h * TQ, (h + 1) * TQ)
            s = s_ref[rs, :]
            m_prev = m_ref[rs, :]
            smax = functools.reduce(jnp.maximum, [s[:, i * 128:(i + 1) * 128] for i in range(NT)])
            m_new = jnp.maximum(m_prev, jnp.max(smax, axis=1, keepdims=True))
            a_ref[slot, rs, :] = jnp.exp2(m_prev - m_new)
            p_ref[slot, rs, :] = jnp.exp2(s - jnp.concatenate([m_new] * NT, axis=1)).astype(BF16)
            m_ref[rs, :] = m_new

    def sel_pv(kq, slot):
        r0 = pl.multiple_of(jnp.clip(kq, 0, n_steps - 1) * TK, TK)
        vext = jnp.concatenate([vs_ref[pl.ds(r0, TK), :], ones_k], axis=1)
        a = a_ref[slot]
        acc_ref[...] = jnp.concatenate([a, a], axis=1) * acc_ref[...] + _dot(p_ref[slot], vext)

    def sel_run(k_lo, count, near):
        @pl.when(count > 0)
        def _():
            sel_logits(k_lo, sa_ref, near)

        def body(j, carry):
            k = k_lo + 2 * j
            sel_logits(k + 1, sb_ref, near)
            sel_softmax(sa_ref, 0)
            sel_pv(k, 0)
            sel_logits(k + 2, sa_ref, near)
            sel_softmax(sb_ref, 1)
            sel_pv(k + 1, 1)
            return carry

        lax.fori_loop(0, count // 2, body, 0)

        @pl.when(count % 2 == 1)
        def _():
            sel_softmax(sa_ref, 0)
            sel_pv(k_lo + count - 1, 0)

    n_far = jnp.maximum((qb + 1 - (n_sel_tab - 2)) // NT, 0)
    sel_run(0, n_far, False)
    sel_run(n_far, n_steps - n_far, True)
    rows = slice(a * TQ, (a + 1) * TQ)
    g = g_ref[rows, :]
    for h in range(H):
        rs = slice(h * TQ, (h + 1) * TQ)
        out_s = acc_ref[rs, 0:C_DIM] / acc_ref[rs, C_DIM:2 * C_DIM]
        o = (g[:, h:h + 1] * oc_ref[a, rs, :] + g[:, H + h:H + h + 1] * out_s
             + g[:, 2 * H + h:2 * H + h + 1] * ow_ref[a, rs, :])
        o_ref[rows, h * C_DIM:(h + 1) * C_DIM] = o.astype(BF16)


def _nsa_attention(q, kv, kvc, gates, tsel, twin, tcmp, B, S):
    N = B * S
    QT = S // TQ
    n_slc = S // SLC_BLOCK
    n_cmp_pad = kvc.shape[2]
    NB = 128
    assert n_slc <= NB and n_cmp_pad % TQ == 0 and QT % 4 == 0
    ov = jnp.asarray(_overlap_np(n_cmp_pad, NB), BF16)
    mk = jnp.asarray(np.where(np.arange(S)[:, None] // SLC_BLOCK == np.arange(NB)[None, :], NEG, 0.0), BF16)
    n_sel_delta = tsel.shape[0]
    kvspec = lambda c: pl.BlockSpec((S, C_DIM), lambda b, g, t: (b, c + g))
    cspec = lambda kvi: pl.BlockSpec((1, 1, n_cmp_pad, C_DIM), lambda b, g, t: (kvi * 2 + g, b, 0, 0))
    npt = NSA_TILES
    assert QT % npt == 0 and CMP_CLASSES % npt == 0
    rowblk = lambda b, g, t: (b * (QT // npt) + t, g)
    hq = C_HPG * TQ
    return pl.pallas_call(
        functools.partial(_nsa_kernel, n_sel_tab=n_sel_delta, top_n=min(SLC_TOP_N, n_slc)),
        grid=(B, C_GROUPS, QT // npt),
        in_specs=[pl.BlockSpec((npt * TQ, C_HPG * C_DIM), rowblk),
                  kvspec(0), kvspec(2), kvspec(4), kvspec(6),
                  cspec(0), cspec(1),
                  pl.BlockSpec((n_cmp_pad, NB), lambda b, g, t: (0, 0)),
                  pl.BlockSpec((S, NB), lambda b, g, t: (0, 0)),
                  pl.BlockSpec((n_sel_delta, C_HPG, TQ, TQ), lambda b, g, t: (0, g, 0, 0)),
                  pl.BlockSpec((twin.shape[0], C_HPG, TQ, TQ), lambda b, g, t: (0, g, 0, 0)),
                  pl.BlockSpec((npt, 4, C_HPG, TQ, TQ),
                               lambda b, g, t: (t % (CMP_CLASSES // npt), 0, g, 0, 0)),
                  pl.BlockSpec((npt * TQ, 128), rowblk)],
        out_specs=pl.BlockSpec((npt * TQ, C_HPG * C_DIM), rowblk),
        out_shape=jax.ShapeDtypeStruct((N, C_W), BF16),
        scratch_shapes=[pltpu.VMEM((hq, 128), F32),
                        pltpu.VMEM((hq, 2 * C_DIM), F32),
                        pltpu.VMEM((2, hq, 4 * TQ), BF16),
                        pltpu.VMEM((2, hq, 128), F32),
                        pltpu.VMEM((npt, hq, n_cmp_pad), BF16),
                        pltpu.VMEM((npt, hq, WIN + TQ), BF16),
                        pltpu.VMEM((hq, 4 * TQ), F32), pltpu.VMEM((hq, 4 * TQ), F32),
                        pltpu.VMEM((npt, hq, C_DIM), F32), pltpu.VMEM((npt, hq, C_DIM), F32),
                        pltpu.VMEM((npt, hq, 2 * C_DIM), BF16)],
        compiler_params=_params(("parallel", "parallel", "arbitrary")),
        name="nsa",
    )(q, kv, kv, kv, kv, kvc, kvc, ov, mk, tsel, twin, tcmp, gates)


def _layer_ab(h2, B, S, w_in, gate_b, conv_w, head_g, w_out, dil_tab):
    w_pad = jnp.pad(w_in, ((0, 0), (0, AB_PAD - AB_PROJ))).astype(BF16)
    gb_pad = jnp.pad(gate_b, (0, 128 - gate_b.shape[0])).reshape(1, 128).astype(F32)
    (aq, ak, av, aq4, ak4, av4, aq16, ak16, av16,
     bq, bk, bv, bo, gates) = _ab_proj(h2, w_pad, conv_w.astype(F32), gb_pad, S)
    ya = _dilated_attention({1: (aq, ak, av), 4: (aq4, ak4, av4), 16: (aq16, ak16, av16)}, dil_tab, B, S)
    yb = _mlstm(bq, bk, bv, gates, bo, head_g, B, S)
    wo = w_out.astype(BF16)
    return [ya, yb], [A_PATTERNS[-1][1], 1], [wo[:A_W], wo[A_W:]]


def _layer_c(h2, B, S, w_in, gate_b, cmp_pos, cmp_w1, cmp_w2, w_out, tsel, twin, tcmp):
    w_pad, gb_pad = _c_weights(w_in, gate_b)
    q, kc, kv, gates = _c_proj(h2, w_pad, gb_pad)
    kvc = _compress(kc, cmp_pos, cmp_w1, cmp_w2, B, S)
    out = _nsa_attention(q, kv, kvc, gates, tsel, twin, tcmp, B, S)
    return [out], [1], [w_out.astype(BF16)]


def kernel(x, rel_bias, ln_g, ln_b, ab_w_in, ab_gate_b, ab_conv, ab_head_norm, ab_w_out,
           c_w_in, c_gate_b, c_cmp_pos, c_cmp_w1, c_cmp_w2, c_w_out,
           moe_wr_g, moe_br_g, moe_wr_e, moe_br_e, moe_w_gate, moe_w_up, moe_w_down):
    B, S, D = x.shape
    assert D == D_MODEL and S % (TM) == 0 and S % (16 * A_BLOCK) == 0
    _check_cmp_windows(S)
    dil_tab = _bias_tables(rel_bias, _dilated_idx(), shift=False)
    tsel, twin, tcmp = _nsa_tables(rel_bias)
    h = x.reshape(B * S, D)
    for layer in range(DEPTH):
        j = layer // 2
        if layer % 2 == 0:
            ys, lays, ws = _layer_ab(h, B, S, ab_w_in[j], ab_gate_b[j], ab_conv[j], ab_head_norm[j],
                                     ab_w_out[j], dil_tab)
        else:
            ys, lays, ws = _layer_c(h, B, S, c_w_in[j], c_gate_b[j], c_cmp_pos[j], c_cmp_w1[j],
                                    c_cmp_w2[j], c_w_out[j], tsel, twin, tcmp)
        wr, br = _router_weights(moe_wr_g[layer], moe_br_g[layer], moe_wr_e[layer], moe_br_e[layer])
        hx, cnt = _out_ln_route(ys, lays, ws, h, ln_g[layer, 0], ln_b[layer, 0], wr, br)
        h = _moe(hx, cnt, layer, moe_w_gate, moe_w_up, moe_w_down, ln_g[layer, 1], ln_b[layer, 1])
    return h.reshape(B, S, D)
```

```python
import functools
import math

import numpy as np
import jax
import jax.numpy as jnp
from jax import lax
from jax.experimental import pallas as pl
from jax.experimental.pallas import tpu as pltpu

F32 = jnp.float32
BF16 = jnp.bfloat16
NEG = -1e30
LOG2E = math.log2(math.e)
VMEM_LIMIT = 48 * 1024 * 1024

D_MODEL = 1024
DEPTH = 2
ALPHA = (2.0 * DEPTH) ** 0.25
LN_EPS = 1e-5
REL_BUCKETS = 32
REL_MAX_DIST = 2048

A_HEADS, A_DIM, A_W = 8, 64, 512
A_PATTERNS = ((128, 1), (512, 4), (2048, 16))
A_BLOCK = 128
B_HEADS, B_DIM, B_W = 4, 128, 512
B_CHUNK = 128
B_CONV = 4
AB_PROJ = 3592
AB_PAD = 3712

C_HEADS, C_GROUPS, C_HPG, C_DIM, C_W = 8, 2, 4, 128, 1024
CMP_BLOCK, CMP_STRIDE, CMP_HIDDEN = 32, 16, 256
SLC_BLOCK, SLC_TOP_N, WIN = 64, 16, 512
C_PROJ = 2584
TQ = 128
NSA_TILES = 4
CMP_PAD = 128

N_GROUPS, EPG, N_EXPERTS, E_HID = 4, 4, 16, 512
N_BUCKETS = N_GROUPS * 6
TM = 512
TM_MOE = 256


def _dot(a, b):
    return jnp.dot(a, b, preferred_element_type=F32)


def _dot_nt(a, b):
    return lax.dot_general(a, b, (((1,), (1,)), ((), ())), preferred_element_type=F32)


def _params(sem):
    return pltpu.CompilerParams(dimension_semantics=sem, vmem_limit_bytes=VMEM_LIMIT)


def _bucket_np(n):
    n = np.maximum(n, 0)
    exact = REL_BUCKETS // 2
    nf = np.maximum(n, 1).astype(np.float64)
    large = exact + (np.log(nf / exact) / math.log(REL_MAX_DIST / exact)
                     * (REL_BUCKETS - exact)).astype(np.int64)
    return np.where(n < exact, n, np.minimum(large, REL_BUCKETS - 1)).astype(np.int32)


def _bias_tab_kernel(tab_ref, idx_ref, out_ref, *, shift, scale):
    R = idx_ref.shape[1]
    RC = 32

    def body(i, carry):
        r0 = pl.multiple_of(i * RC, RC)
        idx = idx_ref[0, pl.ds(r0, RC), :]
        for h in range(8):
            base = tab_ref[REL_BUCKETS - 1, h] if shift else 0.0
            val = jnp.full(idx.shape, (tab_ref[0, h] - base) * scale, F32)
            for b in range(1, REL_BUCKETS):
                val = jnp.where(idx == b, (tab_ref[b, h] - base) * scale, val)
            out_ref[0, h, pl.ds(r0, RC), :] = jnp.where(idx < 0, NEG, val)
        return carry

    lax.fori_loop(0, R // RC, body, 0)


def _bias_tables(rel_bias, idx_np, shift, scale=1.0):
    T, R, C = idx_np.shape
    return pl.pallas_call(
        functools.partial(_bias_tab_kernel, shift=shift, scale=scale),
        grid=(T,),
        in_specs=[pl.BlockSpec(memory_space=pltpu.SMEM),
                  pl.BlockSpec((1, R, C), lambda t: (t, 0, 0))],
        out_specs=pl.BlockSpec((1, 8, R, C), lambda t: (t, 0, 0, 0)),
        out_shape=jax.ShapeDtypeStruct((T, 8, R, C), F32),
        compiler_params=_params(("parallel",)),
        name="bias_tables",
    )(rel_bias.astype(F32), jnp.asarray(idx_np))


def _dilated_idx():
    qi = np.arange(A_BLOCK)[:, None]
    ki = np.arange(2 * A_BLOCK)[None, :]
    j = qi + A_BLOCK - ki
    out = []
    for window, dilation in A_PATTERNS:
        nk = window // dilation
        valid = (j >= 0) & (j <= nk)
        out.append(np.where(valid, _bucket_np(np.maximum(j, 0) * dilation), -1))
    return np.stack(out).astype(np.int32)


def _sel_idx():
    a = np.arange(TQ)[:, None]
    c = np.arange(TQ)[None, :]
    n_delta = -(-(_far_dist() + TQ) // TQ)
    out = []
    for delta in range(-1, n_delta + 1):
        dist = delta * TQ + a - c
        out.append(np.where(dist >= 0, _bucket_np(dist), -1))
    return np.stack(out).astype(np.int32)


def _far_dist():
    n = np.arange(0, 4 * REL_MAX_DIST)
    b = _bucket_np(n)
    return int(np.max(n[b < REL_BUCKETS - 1])) + 1


def _win_idx():
    a = np.arange(TQ)[:, None]
    c = np.arange(TQ)[None, :]
    out = []
    for delta in range(-1, WIN // TQ + 1):
        dist = delta * TQ + a - c
        out.append(np.where((dist >= 0) & (dist < WIN), _bucket_np(dist), -1))
    return np.stack(out).astype(np.int32)


CMP_PER_TILE = TQ // CMP_STRIDE
CMP_CLASSES = TQ // CMP_PER_TILE
CMP_SPLIT = 13


def _cmp_window_start(qb):
    return qb // CMP_CLASSES + (1 if qb % CMP_CLASSES >= CMP_SPLIT else 0)


def _cmp_idx():
    a = np.arange(TQ)[:, None]
    c = np.arange(TQ)[None, :]
    out = []
    for r in range(CMP_CLASSES):
        qb = CMP_CLASSES + r
        i0 = _cmp_window_start(qb) * TQ - CMP_PAD
        for half in range(2):
            dist = qb * TQ + a - ((i0 + half * TQ + c) * CMP_STRIDE + CMP_BLOCK - 1)
            out.append(np.where(dist >= 0, _bucket_np(dist), -1))
        out.append(np.full((TQ, TQ), REL_BUCKETS - 1))
        out.append(np.full((TQ, TQ), -1))
    return np.stack(out).astype(np.int32)


def _nsa_tables(rel_bias):
    tsel = _bias_tables(rel_bias, _sel_idx(), shift=True, scale=LOG2E)
    twin = _bias_tables(rel_bias, _win_idx(), shift=False, scale=LOG2E)
    tcmp = _bias_tables(rel_bias, _cmp_idx(), shift=True, scale=LOG2E)
    return tsel, twin, tcmp.reshape(CMP_CLASSES, 4, 8, TQ, TQ)


def _check_cmp_windows(S):
    far = _far_dist()
    for qb in range(S // TQ):
        i0 = _cmp_window_start(qb) * TQ - CMP_PAD
        s0 = qb * TQ
        assert s0 - ((i0 - 1) * CMP_STRIDE + CMP_BLOCK - 1) >= far
        assert s0 + TQ - 1 - ((i0 + 2 * TQ) * CMP_STRIDE + CMP_BLOCK - 1) < 0


def _residue_col(d, r):
    return (r % 4) * 4 + r // 4 if d == 16 else r


def _lane_chunks_store(ref3, val):
    for c in range(ref3.shape[0]):
        ref3[c] = val[:, c * 128:(c + 1) * 128]


def _to_residue_layout(src3_ref, dst, d, col_of=_residue_col):
    nc, rows, _ = src3_ref.shape
    for r in range(d):
        cb = col_of(d, r)
        for c in range(nc):
            col = (cb * nc + c) * 128
            dst(slice(col, col + 128), src3_ref[c, pl.ds(r, rows // d, stride=d), :])


def _ab_proj_kernel(x_ref, xh_ref, w_ref, cw_ref, gb_ref,
                    aq_ref, ak_ref, av_ref, aq4_ref, ak4_ref, av4_ref, aq16_ref, ak16_ref, av16_ref,
                    bq_ref, bk_ref, bv_ref, bo_ref, g_ref,
                    pre_ref, tmp_ref, *, tiles_per_seq):
    i = pl.program_id(0)
    tm = x_ref.shape[0]
    xb = x_ref[...].astype(BF16)
    for c, scale, outs in ((0, A_DIM ** -0.5, (aq_ref, aq4_ref, aq16_ref)),
                           (1, 1.0, (ak_ref, ak4_ref, ak16_ref)),
                           (2, 1.0, (av_ref, av4_ref, av16_ref))):
        val = _dot(xb, w_ref[:, c * A_W:(c + 1) * A_W]) * scale
        _lane_chunks_store(tmp_ref, val)
        outs[0][...] = val.astype(BF16)
        for d, o_ref in ((4, outs[1]), (16, outs[2])):
            def put(cols, piece, o_ref=o_ref):
                o_ref[:, cols] = piece.astype(BF16)
            _to_residue_layout(tmp_ref, put, d)
    bv_ref[...] = _dot(xb, w_ref[:, 2560:3072]).astype(BF16)
    bo_ref[...] = _dot(xb, w_ref[:, 3072:3584])
    g_ref[...] = _dot(xb, w_ref[:, 3584:AB_PAD]) + gb_ref[...]
    halo = _dot(xh_ref[...].astype(BF16), w_ref[:, 1536:2560])
    halo = jnp.where(i % tiles_per_seq == 0, 0.0, halo)
    pre_ref[0:8, :] = halo
    pre_ref[8:8 + tm, :] = _dot(xb, w_ref[:, 1536:2560])
    y = pre_ref[8:8 + tm, :] * cw_ref[B_CONV - 1:B_CONV, :]
    for k in range(B_CONV - 1):
        s = B_CONV - 1 - k
        y = y + pre_ref[8 - s:8 - s + tm, :] * cw_ref[k:k + 1, :]
    y = y / (1.0 + jnp.exp(-y))
    bq_ref[...] = (y[:, :B_W] * (B_DIM ** -0.5)).astype(BF16)
    bk_ref[...] = y[:, B_W:].astype(BF16)


def _ab_proj(x2, w_pad, conv_w, gate_b_pad, S):
    N = x2.shape[0]
    tm = TM
    tps = S // tm
    row = lambda i: (i, 0)
    fix = lambda i: (0, 0)
    lay = lambda d: [jax.ShapeDtypeStruct((N // d, d * A_W), BF16)] * 3
    lay_spec = lambda d: [pl.BlockSpec((tm // d, d * A_W), row)] * 3
    outs = lay(1) + lay(4) + lay(16) + [jax.ShapeDtypeStruct((N, 512), BF16)] * 3 + [
        jax.ShapeDtypeStruct((N, 512), F32), jax.ShapeDtypeStruct((N, 128), F32)]
    o_specs = (lay_spec(1) + lay_spec(4) + lay_spec(16) + [pl.BlockSpec((tm, 512), row)] * 4
               + [pl.BlockSpec((tm, 128), row)])
    return pl.pallas_call(
        functools.partial(_ab_proj_kernel, tiles_per_seq=tps),
        grid=(N // tm,),
        in_specs=[pl.BlockSpec((tm, D_MODEL), row),
                  pl.BlockSpec((8, D_MODEL), lambda i: (jnp.maximum(i * (tm // 8) - 1, 0), 0)),
                  pl.BlockSpec((D_MODEL, AB_PAD), fix),
                  pl.BlockSpec((B_CONV, 2 * B_W), fix),
                  pl.BlockSpec((1, 128), fix)],
        out_specs=o_specs,
        out_shape=outs,
        scratch_shapes=[pltpu.VMEM((tm + 8, 2 * B_W), F32), pltpu.VMEM((A_W // 128, tm, 128), F32)],
        compiler_params=_params(("parallel",)),
        name="ab_proj",
    )(x2, x2, w_pad, conv_w, gate_b_pad)


def _dilated_kernel(*refs, has_prev, is_last):
    if has_prev:
        q_ref, kp_ref, kc_ref, vp_ref, vc_ref, tab_ref, op_ref, lp_ref = refs[:8]
        rest = refs[8:]
    else:
        q_ref, kp_ref, kc_ref, vp_ref, vc_ref, tab_ref = refs[:6]
        rest = refs[6:]
    outs, scratch = (rest, ()) if is_last else (rest[:2], rest[2:])
    o_ref = outs[0]
    n = pl.program_id(2)
    n_sub = q_ref.shape[1] // A_BLOCK
    lane = lax.broadcasted_iota(jnp.int32, (A_BLOCK, 128), 1)
    erow = lax.broadcasted_iota(jnp.int32, (128, A_W), 0)
    ecol = lax.broadcasted_iota(jnp.int32, (128, A_W), 1)
    expand = jnp.where(erow - A_DIM == jnp.right_shift(ecol, A_DIM.bit_length() - 1), 1.0, 0.0).astype(BF16)

    def spread(t):
        hi = t.astype(BF16)
        return _dot(hi, expand) + _dot((t - hi.astype(F32)).astype(BF16), expand)

    keep_side = [jnp.where(lane < A_DIM, 1.0, 0.0).astype(BF16), jnp.where(lane < A_DIM, 0.0, 1.0).astype(BF16)]
    odd = jnp.bitwise_and(lane, 1) == 1
    ones_side = [jnp.where(odd, 0.0, 1.0).astype(BF16), jnp.where(odd, 1.0, 0.0).astype(BF16)]
    stat = (lane >= A_DIM) & (lane < A_DIM + A_HEADS)
    for sb in range(n_sub):
        rs = slice(sb * A_BLOCK, (sb + 1) * A_BLOCK)
        first = jnp.where(n == 0, NEG, 0.0) if sb == 0 else 0.0
        m_tile = jnp.zeros((A_BLOCK, 128), F32)
        l_tile = jnp.ones((A_BLOCK, 128), F32)
        unnorm = []
        for j in range(A_HEADS // 2):
            cs = slice(j * 128, (j + 1) * 128)
            q2 = q_ref[0, rs, cs]
            kc, vc = kc_ref[0, rs, cs], vc_ref[0, rs, cs]
            if sb == 0:
                kp, vp = kp_ref[0, :, cs], vp_ref[0, :, cs]
            else:
                ps = slice((sb - 1) * A_BLOCK, sb * A_BLOCK)
                kp, vp = kc_ref[0, ps, cs], vc_ref[0, ps, cs]
            k_st = jnp.concatenate([kp * keep_side[0], kc * keep_side[0],
                                    kp * keep_side[1], kc * keep_side[1]], axis=0)
            s = _dot_nt(q2, k_st)
            p_parts = []
            for side in range(2):
                h = 2 * j + side
                c0 = 2 * side * A_BLOCK
                sp = s[:, c0:c0 + A_BLOCK] + tab_ref[0, h, :, 0:A_BLOCK] + first
                sc = s[:, c0 + A_BLOCK:c0 + 2 * A_BLOCK] + tab_ref[0, h, :, A_BLOCK:2 * A_BLOCK]
                m = jnp.max(jnp.maximum(sp, sc), axis=1, keepdims=True)
                p_parts += [jnp.exp(sp - m).astype(BF16), jnp.exp(sc - m).astype(BF16)]
                m_tile = jnp.where(lane == A_DIM + h, m, m_tile)
            v_st = jnp.concatenate(
                [jnp.concatenate([v * keep_side[side], ones_side[side]], axis=1)
                 for side in range(2) for v in (vp, vc)], axis=0)
            r = _dot(jnp.concatenate(p_parts, axis=1), v_st)
            unnorm.append(r[:, 0:128])
            pair = (lane == A_DIM + 2 * j) | (lane == A_DIM + 2 * j + 1)
            l_tile = jnp.where(pair, r[:, 128:256], l_tile)
        lse = m_tile + jnp.log(l_tile)
        if has_prev:
            lp = lp_ref[0, rs, :]
            mm = jnp.maximum(lp, lse)
            wp = jnp.exp(lp - mm)
            wc = jnp.exp(lse - mm)
            tot = wp + wc
            scale_prev = jnp.where(stat, wp / tot, 0.0)
            scale_cur = jnp.where(stat, wc / (tot * l_tile), 0.0)
            lse = mm + jnp.log(tot)
        else:
            scale_cur = jnp.where(stat, 1.0 / l_tile, 0.0)
        o = jnp.concatenate(unnorm, axis=1) * spread(scale_cur)
        if has_prev:
            o = o + op_ref[0, rs, :] * spread(scale_prev)
        if is_last:
            o_ref[0, rs, :] = o.astype(o_ref.dtype)
        else:
            o_scr, l_scr = scratch[0].at[sb], scratch[1].at[sb]
            _lane_chunks_store(o_scr, o)
            l_scr[0] = jnp.where(stat, lse, 0.0)
            orows = slice(sb * (A_BLOCK // 4), (sb + 1) * (A_BLOCK // 4))

            def put_o(cols, piece, orows=orows):
                o_ref[0, orows, cols] = piece

            def put_l(cols, piece, orows=orows):
                outs[1][0, orows, cols] = piece

            _to_residue_layout(o_scr, put_o, 4)
            _to_residue_layout(l_scr, put_l, 4)


def _dilated_call(q, k, v, tab, prev, pattern_idx, dilation, B, S, is_last):
    d = dilation
    L = S // d
    nb = L // A_BLOCK
    sub = max(s for s in (8, 4, 2, 1) if nb % s == 0)
    r3 = lambda a: a.reshape(B, L, a.shape[-1])
    cur = lambda b, r, n: (b, n, r)
    prv = lambda b, r, n: (b, jnp.maximum(sub * n - 1, 0), r)
    blk = pl.BlockSpec((1, sub * A_BLOCK, A_W), cur)
    in_specs = [blk, pl.BlockSpec((1, A_BLOCK, A_W), prv), blk,
                pl.BlockSpec((1, A_BLOCK, A_W), prv), blk,
                pl.BlockSpec((1, 8, A_BLOCK, 2 * A_BLOCK), lambda b, r, n: (pattern_idx, 0, 0, 0))]
    args = [r3(q), r3(k), r3(k), r3(v), r3(v), tab]
    has_prev = prev is not None
    if has_prev:
        in_specs += [blk, pl.BlockSpec((1, sub * A_BLOCK, 128), cur)]
        args += [r3(prev[0]), r3(prev[1])]
    scratch = []
    if is_last:
        out_shape = [jax.ShapeDtypeStruct((B, L, d * A_W), BF16)]
        out_specs = [blk]
    else:
        d2, rows = 4 * d, sub * A_BLOCK // 4
        nxt = lambda b, r, n: (b, n, r)
        out_shape = [jax.ShapeDtypeStruct((B, S // d2, d2 * A_W), F32),
                     jax.ShapeDtypeStruct((B, S // d2, d2 * 128), F32)]
        out_specs = [pl.BlockSpec((1, rows, 4 * A_W), nxt), pl.BlockSpec((1, rows, 4 * 128), nxt)]
        scratch = [pltpu.VMEM((sub, A_W // 128, A_BLOCK, 128), F32),
                   pltpu.VMEM((sub, 1, A_BLOCK, 128), F32)]
    res = pl.pallas_call(
        functools.partial(_dilated_kernel, has_prev=has_prev, is_last=is_last),
        grid=(B, d, nb // sub),
        in_specs=in_specs, out_specs=out_specs, out_shape=out_shape, scratch_shapes=scratch,
        compiler_params=_params(("parallel", "parallel", "arbitrary")),
        name="dilated_d%d" % d,
    )(*args)
    return [r.reshape(-1, r.shape[-1]) for r in res]


def _dilated_attention(qkv_by_dilation, tab, B, S):
    prev = None
    for p, (window, d) in enumerate(A_PATTERNS):
        assert window // d == A_BLOCK and S % (d * A_BLOCK) == 0
        assert p == 0 or d == 4 * A_PATTERNS[p - 1][1]
        last = p == len(A_PATTERNS) - 1
        q, k, v = qkv_by_dilation[d]
        prev = _dilated_call(q, k, v, tab, prev, p, d, B, S, last)
    return prev[0]


def _split3(x):
    hi = x.astype(BF16)
    r = x - hi.astype(F32)
    mid = r.astype(BF16)
    lo = (r - mid.astype(F32)).astype(BF16)
    return hi, mid, lo


def _mlstm_kernel(q_ref, k_ref, v_ref, g_ref, bo_ref, hg_ref, y_ref, c_ref, m_ref):
    L = B_CHUNK
    c = pl.program_id(1)

    @pl.when(c == 0)
    def _():
        c_ref[...] = jnp.zeros_like(c_ref)
        m_ref[...] = jnp.zeros_like(m_ref)

    lane = lax.broadcasted_iota(jnp.int32, (L, 128), 1)
    row = lax.broadcasted_iota(jnp.int32, (L, L), 0)
    col = lax.broadcasted_iota(jnp.int32, (L, L), 1)
    tri = row >= col
    is_f = (lane >= B_HEADS) & (lane < 2 * B_HEADS)
    tril = jnp.where(tri, 1.0, 0.0).astype(BF16)
    ones = jnp.ones((L, B_DIM), BF16)
    bi = 0
    g = g_ref[bi]
    logf = jnp.minimum(g, 0.0) - jnp.log(1.0 + jnp.exp(-jnp.abs(g)))
    gl = jnp.where(is_f, logf, jnp.where(lane < B_HEADS, g, 0.0))
    hi, mid, lo = _split3(gl)
    cum = _dot(tril, hi) + _dot(tril, mid) + _dot(tril, lo)
    cum_t = cum.T
    gl_t = gl.T
    for h in range(B_HEADS):
        st = h
        cs = slice(h * B_DIM, (h + 1) * B_DIM)
        q = q_ref[bi, :, cs]
        k = k_ref[bi, :, cs]
        v_ext = jnp.concatenate([v_ref[bi, :, cs], ones], axis=1)
        b_col = cum[:, B_HEADS + h:B_HEADS + h + 1]
        b_row = cum_t[B_HEADS + h:B_HEADS + h + 1, :]
        i_col = gl[:, h:h + 1]
        i_row = gl_t[h:h + 1, :]
        m_prev = m_ref[st, 0:1, 0:1]
        dm = jnp.where(tri, b_col - b_row + i_row, NEG)
        inter = b_col + m_prev
        m_t = jnp.maximum(inter, jnp.max(dm, axis=1, keepdims=True))
        p = jnp.exp(dm - m_t)
        sqk = _dot_nt(q, k) * p
        sc = jnp.exp(inter - m_t)
        lhs = jnp.concatenate([(sc * q.astype(F32)).astype(BF16), sqk.astype(BF16)], axis=1)
        c_ext = c_ref[st]
        rhs = jnp.concatenate([c_ext.astype(BF16), v_ext], axis=0)
        res = _dot(lhs, rhs)
        num = res[:, :B_DIM]
        den = res[:, B_DIM:]
        hh = num / jnp.maximum(jnp.abs(den), jnp.exp(-m_t))
        b_last = b_col[L - 1:L, :]
        gk = b_last - b_col + i_col
        m_new = jnp.maximum(b_last + m_prev, jnp.max(gk, axis=0, keepdims=True))
        wk = jnp.exp(gk - m_new)
        decay = jnp.exp(b_last + m_prev - m_new)
        kw_t = (wk * k.astype(F32)).T.astype(BF16)
        c_ref[st] = decay * c_ext + _dot(kw_t, v_ext)
        m_ref[st] = jnp.broadcast_to(m_new, (8, 128))
        mu = jnp.mean(hh, axis=1, keepdims=True)
        xc = hh - mu
        var = jnp.mean(xc * xc, axis=1, keepdims=True)
        hn = xc * lax.rsqrt(var + LN_EPS) * hg_ref[:, cs]
        bo = bo_ref[bi, :, cs]
        y_ref[bi, :, cs] = (hn / (1.0 + jnp.exp(-bo))).astype(BF16)


def _mlstm(bq, bk, bv, gates, bo, head_g, B, S):
    nc = S // B_CHUNK
    r3 = lambda a: a.reshape(B, S, a.shape[-1])
    blk = pl.BlockSpec((1, B_CHUNK, B_W), lambda b, c: (b, c, 0))
    y = pl.pallas_call(
        _mlstm_kernel,
        grid=(B, nc),
        in_specs=[blk, blk, blk,
                  pl.BlockSpec((1, B_CHUNK, 128), lambda b, c: (b, c, 0)),
                  blk,
                  pl.BlockSpec((1, B_W), lambda b, c: (0, 0))],
        out_specs=blk,
        out_shape=jax.ShapeDtypeStruct((B, S, B_W), BF16),
        scratch_shapes=[pltpu.VMEM((B_HEADS, B_DIM, 2 * B_DIM), F32),
                        pltpu.VMEM((B_HEADS, 8, 128), F32)],
        compiler_params=_params(("parallel", "arbitrary")),
        name="mlstm",
    )(r3(bq), r3(bk), r3(bv), r3(gates), r3(bo), head_g.reshape(1, B_W).astype(F32))
    return y.reshape(B * S, B_W)


def _layer_norm(z, g, b):
    mu = jnp.mean(z, axis=1, keepdims=True)
    zc = z - mu
    var = jnp.mean(zc * zc, axis=1, keepdims=True)
    return zc * lax.rsqrt(var + LN_EPS) * g + b


def _route(logits, cnt_ref):
    tm = logits.shape[0]
    lt = logits.T
    col = lambda c: lt[c:c + 1, :]
    gl = [col(c) for c in range(N_GROUPS)]
    gmax = functools.reduce(jnp.maximum, gl)
    gsum = sum(jnp.exp(x - gmax) for x in gl)
    g_idx = jnp.full(gmax.shape, N_GROUPS - 1, jnp.int32)
    for c in range(N_GROUPS - 2, -1, -1):
        g_idx = jnp.where(gl[c] == gmax, c, g_idx)
    g_w = 1.0 / gsum
    el = []
    for k in range(EPG):
        x = col(N_GROUPS + (N_GROUPS - 1) * EPG + k)
        for g in range(N_GROUPS - 2, -1, -1):
            x = jnp.where(g_idx == g, col(N_GROUPS + g * EPG + k), x)
        el.append(x)
    v1 = functools.reduce(jnp.maximum, el)
    i1 = jnp.full(v1.shape, EPG - 1, jnp.int32)
    for k in range(EPG - 2, -1, -1):
        i1 = jnp.where(el[k] == v1, k, i1)
    el2 = [jnp.where(i1 == k, -jnp.inf, el[k]) for k in range(EPG)]
    v2 = functools.reduce(jnp.maximum, el2)
    i2 = jnp.full(v2.shape, EPG - 1, jnp.int32)
    for k in range(EPG - 2, -1, -1):
        i2 = jnp.where((el2[k] == v2) & (i1 != k), k, i2)
    t = jnp.exp(v2 - v1)
    w1 = g_w / (1.0 + t)
    w2 = w1 * t
    a = jnp.minimum(i1, i2)
    b = jnp.maximum(i1, i2)
    pair = jnp.where(a == 0, b - 1, jnp.where(a == 1, b + 1, 5))
    bucket = (g_idx * 6 + pair).astype(F32)
    w_lo = jnp.where(i1 < i2, w1, w2)
    w_hi = jnp.where(i1 < i2, w2, w1)
    sub = lax.broadcasted_iota(jnp.int32, (128, tm), 0)
    onehot_t = jnp.where(sub.astype(F32) == bucket, 1.0, 0.0)
    srow = lax.broadcasted_iota(jnp.int32, (tm, tm), 0)
    scol = lax.broadcasted_iota(jnp.int32, (tm, tm), 1)
    before = jnp.where(srow < scol, 1.0, 0.0).astype(BF16)
    oh = onehot_t.astype(BF16)
    carry = cnt_ref[...]
    prior = _dot(oh, before) + jnp.concatenate([carry] * (tm // 128), axis=1)
    rank = jnp.sum(onehot_t * prior, axis=0, keepdims=True)
    cnt_ref[...] = carry + _dot(oh, jnp.ones((tm, 128), BF16))
    out_t = jnp.where(sub == 0, bucket, jnp.where(sub == 1, w_lo, jnp.where(sub == 2, w_hi,
                      jnp.where(sub == 3, rank, 0.0))))
    return out_t.T


def _out_ln_route_kernel(*refs, n_in, layouts):
    y_refs = refs[:n_in]
    w_refs = refs[n_in:2 * n_in]
    x_ref, g_ref, b_ref, wrh_ref, wrl_ref, br_ref, h_ref, c_ref, cnt_ref = refs[2 * n_in:2 * n_in + 9]
    pos_refs = list(refs[2 * n_in + 9:])
    tm = x_ref.shape[0]

    @pl.when(pl.program_id(0) == 0)
    def _():
        cnt_ref[...] = jnp.zeros_like(cnt_ref)

    y = None
    for i in range(n_in):
        d = layouts[i]
        if d == 1:
            lhs = y_refs[i][...]
        else:
            s_ref = pos_refs.pop(0)
            nc = s_ref.shape[0]
            for r in range(d):
                cb = _residue_col(d, r)
                for c in range(nc):
                    col = (cb * nc + c) * 128
                    s_ref[c, pl.ds(r, tm // d, stride=d), :] = y_refs[i][:, col:col + 128].astype(F32)
            lhs = jnp.concatenate([s_ref[c] for c in range(nc)], axis=1).astype(BF16)
        t = _dot(lhs, w_refs[i][...])
        y = t if y is None else y + t
    hn = _layer_norm(ALPHA * x_ref[...] + y, g_ref[...], b_ref[...])
    h_ref[:, 0:D_MODEL] = hn
    hi = hn.astype(BF16)
    lo = (hn - hi.astype(F32)).astype(BF16)
    logits = (_dot_nt(hi, wrh_ref[...]) + _dot_nt(lo, wrh_ref[...]) + _dot_nt(hi, wrl_ref[...])
              + br_ref[...])
    h_ref[:, D_MODEL:D_MODEL + 128] = _route(logits, cnt_ref)
    c_ref[...] = cnt_ref[...]


def _out_ln_route(ys, layouts, ws, x2, ln_g, ln_b, wr, br):
    N = x2.shape[0]
    tm = TM
    row = lambda i: (i, 0)
    fix = lambda i: (0, 0)
    wr_hi = wr.astype(BF16)
    wr_lo = (wr - wr_hi.astype(F32)).astype(BF16)
    in_specs = ([pl.BlockSpec((tm // d, y.shape[1]), row) for y, d in zip(ys, layouts)]
                + [pl.BlockSpec(w.shape, fix) for w in ws]
                + [pl.BlockSpec((tm, D_MODEL), row),
                   pl.BlockSpec((1, D_MODEL), fix), pl.BlockSpec((1, D_MODEL), fix),
                   pl.BlockSpec((128, D_MODEL), fix), pl.BlockSpec((128, D_MODEL), fix),
                   pl.BlockSpec((1, 128), fix)])
    return pl.pallas_call(
        functools.partial(_out_ln_route_kernel, n_in=len(ys), layouts=tuple(layouts)),
        grid=(N // tm,),
        in_specs=in_specs,
        out_specs=[pl.BlockSpec((tm, D_MODEL + 128), row), pl.BlockSpec((128, 128), fix)],
        out_shape=[jax.ShapeDtypeStruct((N, D_MODEL + 128), F32), jax.ShapeDtypeStruct((128, 128), F32)],
        scratch_shapes=[pltpu.VMEM((128, 128), F32)] + [
            pltpu.VMEM((y.shape[1] // d // 128, tm, 128), F32) for y, d in zip(ys, layouts) if d != 1],
        compiler_params=_params(("arbitrary",)),
        name="out_ln_route",
    )(*ys, *ws, x2, ln_g.reshape(1, -1), ln_b.reshape(1, -1), wr_hi, wr_lo, br)


def _router_weights(wr_g, br_g, wr_e, br_e):
    we = wr_e.transpose(0, 2, 1).reshape(N_GROUPS * EPG, D_MODEL)
    w = jnp.concatenate([wr_g.T, we], axis=0)
    w = jnp.pad(w, ((0, 128 - w.shape[0]), (0, 0)))
    b = jnp.concatenate([br_g, br_e.reshape(-1)])
    b = jnp.pad(b, (0, 128 - b.shape[0])).reshape(1, 128)
    return w.astype(F32), b.astype(F32)


_PAIRS = ((0, 1), (0, 2), (0, 3), (1, 2), (1, 3), (2, 3))


def _moe_kernel(elo_ref, ehi_ref, chg_ref, nt_ref,
                x_ref, wgl_ref, wul_ref, wdl_ref, wgh_ref, wuh_ref, wdh_ref,
                g_ref, b_ref, o_ref, wg_s, wu_s, wd_s):
    t = pl.program_id(0)

    @pl.when(chg_ref[t] == 1)
    def _():
        wg_s[0] = wgl_ref[0, 0].astype(BF16)
        wu_s[0] = wul_ref[0, 0].astype(BF16)
        wd_s[0] = wdl_ref[0, 0].astype(BF16)
        wg_s[1] = wgh_ref[0, 0].astype(BF16)
        wu_s[1] = wuh_ref[0, 0].astype(BF16)
        wd_s[1] = wdh_ref[0, 0].astype(BF16)

    @pl.when(t < nt_ref[0])
    def _():
        x = x_ref[:, 0:D_MODEL]
        xb = x.astype(BF16)
        r = x_ref[:, D_MODEL:D_MODEL + 128]
        acc = None
        for e in range(2):
            a = _dot(xb, wg_s[e])
            u = _dot(xb, wu_s[e])
            hcur = (a / (1.0 + jnp.exp(-a))) * u * r[:, 1 + e:2 + e]
            y = _dot(hcur.astype(BF16), wd_s[e])
            acc = y if acc is None else acc + y
        o_ref[...] = _layer_norm(ALPHA * x + acc, g_ref[...], b_ref[...])

    @pl.when(t >= nt_ref[0])
    def _():
        o_ref[...] = jnp.zeros_like(o_ref)


def _moe(hx, cnt, layer, w_gate, w_up, w_down, ln_g, ln_b):
    N = hx.shape[0]
    tm = TM_MOE
    n_tiles = N // tm + N_BUCKETS
    n_pad = n_tiles * tm
    bucket = hx[:, D_MODEL].astype(jnp.int32)
    rank = hx[:, D_MODEL + 3].astype(jnp.int32)
    counts = cnt[:N_BUCKETS, 0].astype(jnp.int32)
    padded = ((counts + tm - 1) // tm) * tm
    ends = jnp.cumsum(padded)
    offs = ends - padded
    b2 = bucket.reshape(-1, 128)
    off2 = functools.reduce(lambda acc, b: jnp.where(b2 == b, offs[b], acc), range(N_BUCKETS),
                            jnp.zeros_like(b2))
    dest = off2.reshape(-1) + rank
    src = (jnp.arange(n_pad, dtype=jnp.int32) % N).at[dest].set(
        jnp.arange(N, dtype=jnp.int32), mode="promise_in_bounds", unique_indices=True)
    tile_start = jnp.arange(n_tiles, dtype=jnp.int32) * tm
    n_used = (ends[-1] // tm).astype(jnp.int32)
    tb = jnp.sum((tile_start[:, None] >= ends[None, :]).astype(jnp.int32), axis=1)
    tb_last = jnp.take(tb, jnp.maximum(n_used - 1, 0))
    tb = jnp.where(tile_start < ends[-1], tb, tb_last)
    pairs = jnp.asarray(_PAIRS, jnp.int32)
    elo = (tb // 6) * EPG + pairs[tb % 6, 0]
    ehi = (tb // 6) * EPG + pairs[tb % 6, 1]
    chg = jnp.concatenate([jnp.ones((1,), jnp.int32), (tb[1:] != tb[:-1]).astype(jnp.int32)])
    xs = hx.at[src].get(mode="promise_in_bounds")

    row = lambda t, *_: (t, 0)
    fix = lambda t, *_: (0, 0)
    wlo = lambda t, elo, ehi, chg, nt: (layer, elo[t], 0, 0)
    whi = lambda t, elo, ehi, chg, nt: (layer, ehi[t], 0, 0)
    up_spec = lambda im: pl.BlockSpec((1, 1, D_MODEL, E_HID), im)
    dn_spec = lambda im: pl.BlockSpec((1, 1, E_HID, D_MODEL), im)
    grid_spec = pltpu.PrefetchScalarGridSpec(
        num_scalar_prefetch=4,
        grid=(n_tiles,),
        in_specs=[pl.BlockSpec((tm, D_MODEL + 128), row),
                  up_spec(wlo), up_spec(wlo), dn_spec(wlo),
                  up_spec(whi), up_spec(whi), dn_spec(whi),
                  pl.BlockSpec((1, D_MODEL), fix), pl.BlockSpec((1, D_MODEL), fix)],
        out_specs=pl.BlockSpec((tm, D_MODEL), row),
        scratch_shapes=[pltpu.VMEM((2, D_MODEL, E_HID), BF16),
                        pltpu.VMEM((2, D_MODEL, E_HID), BF16),
                        pltpu.VMEM((2, E_HID, D_MODEL), BF16)])
    out_sorted = pl.pallas_call(
        _moe_kernel,
        grid_spec=grid_spec,
        out_shape=jax.ShapeDtypeStruct((n_pad, D_MODEL), F32),
        compiler_params=_params(("arbitrary",)),
        name="moe",
    )(elo, ehi, chg, n_used.reshape(1), xs, w_gate, w_up, w_down, w_gate, w_up, w_down,
      ln_g.reshape(1, -1), ln_b.reshape(1, -1))
    return out_sorted.at[dest].get(mode="promise_in_bounds", unique_indices=True)


def _c_proj_kernel(x_ref, w_ref, gb_ref, q_ref, kc_ref, kv_ref, g_ref, tmp_ref):
    xb = x_ref[...].astype(BF16)
    q_ref[...] = (_dot(xb, w_ref[:, 0:C_W]) * (C_DIM ** -0.5 * LOG2E)).astype(BF16)
    for i in range(4):
        tmp_ref[0] = _dot(xb, w_ref[:, C_W + i * 128:C_W + (i + 1) * 128])

        def put(cols, piece, i=i):
            kc_ref[i, :, cols] = piece.astype(BF16)

        _to_residue_layout(tmp_ref, put, CMP_STRIDE, col_of=lambda d, r: r)
    kv_ref[...] = _dot(xb, w_ref[:, C_W + 512:C_W + 1536]).astype(BF16)
    z = _dot(xb, w_ref[:, C_W + 1536:C_W + 1792]) + gb_ref[...]
    g_ref[...] = 1.0 / (1.0 + jnp.exp(-z))


def _c_proj(x2, w_pad, gb_pad):
    N = x2.shape[0]
    tm = TM
    row = lambda i: (i, 0)
    fix = lambda i: (0, 0)
    wcols = w_pad.shape[1]
    return pl.pallas_call(
        _c_proj_kernel,
        grid=(N // tm,),
        in_specs=[pl.BlockSpec((tm, D_MODEL), row), pl.BlockSpec((D_MODEL, wcols), fix),
                  pl.BlockSpec((1, 256), fix)],
        out_specs=[pl.BlockSpec((tm, C_W), row),
                   pl.BlockSpec((4, tm // CMP_STRIDE, CMP_STRIDE * C_DIM), lambda i: (0, i, 0)),
                   pl.BlockSpec((tm, 1024), row), pl.BlockSpec((tm, 256), row)],
        out_shape=[jax.ShapeDtypeStruct((N, C_W), BF16),
                   jax.ShapeDtypeStruct((4, N // CMP_STRIDE, CMP_STRIDE * C_DIM), BF16),
                   jax.ShapeDtypeStruct((N, 1024), BF16), jax.ShapeDtypeStruct((N, 256), F32)],
        scratch_shapes=[pltpu.VMEM((1, tm, 128), F32)],
        compiler_params=_params(("parallel",)),
        name="c_proj",
    )(x2, w_pad, gb_pad)


def _c_weights(w_in, gate_b):
    gcols = []
    gb = []
    for g in range(C_GROUPS):
        idx = [C_PROJ - 3 * C_HEADS + br * C_HEADS + g * C_HPG + j for br in range(3) for j in range(C_HPG)]
        gcols.append(jnp.pad(w_in[:, np.asarray(idx)], ((0, 0), (0, 128 - len(idx)))))
        gb.append(jnp.pad(gate_b[np.asarray(idx) - (C_PROJ - 3 * C_HEADS)], (0, 128 - len(idx))))
    w = jnp.concatenate([w_in[:, :C_PROJ - 3 * C_HEADS]] + gcols, axis=1).astype(BF16)
    return w, jnp.concatenate(gb).reshape(1, 256).astype(F32)


def _compress_kernel(seg_ref, w1_ref, pos_ref, w1f_ref, w2_ref, o_ref):
    n_seg = seg_ref.shape[1]
    ul = _dot(seg_ref[0], w1_ref[0])
    u = ul[:, :CMP_HIDDEN]
    lnext = pltpu.roll(ul[:, CMP_HIDDEN:], n_seg - 1, 0)
    cpos = _dot(pos_ref[0], w1f_ref[0])[0:1, :]
    pre = u + lnext + cpos
    act = 0.5 * pre * (1.0 + jnp.tanh(math.sqrt(2.0 / math.pi) * (pre + 0.044715 * pre * pre * pre)))
    o_ref[0, 0, 0:CMP_PAD, :] = jnp.zeros((CMP_PAD, C_DIM), BF16)
    o_ref[0, 0, CMP_PAD:CMP_PAD + n_seg, :] = _dot(act.astype(BF16), w2_ref[0]).astype(BF16)


def _compress(kc, cmp_pos, cmp_w1, cmp_w2, B, S):
    n_seg = S // CMP_STRIDE
    half = CMP_STRIDE * C_DIM
    seg = kc.reshape(4 * B, n_seg, half)
    w1 = cmp_w1.astype(BF16)
    w1_ul = jnp.concatenate([w1[:, :half], w1[:, half:]], axis=2)
    pos = jnp.broadcast_to(cmp_pos.reshape(2, 1, CMP_BLOCK * C_DIM), (2, 8, CMP_BLOCK * C_DIM)).astype(BF16)
    out = pl.pallas_call(
        _compress_kernel,
        grid=(4, B),
        in_specs=[pl.BlockSpec((1, n_seg, half), lambda i, b: (i * B + b, 0, 0)),
                  pl.BlockSpec((1, half, 2 * CMP_HIDDEN), lambda i, b: (i // 2, 0, 0)),
                  pl.BlockSpec((1, 8, CMP_BLOCK * C_DIM), lambda i, b: (i // 2, 0, 0)),
                  pl.BlockSpec((1, CMP_BLOCK * C_DIM, CMP_HIDDEN), lambda i, b: (i // 2, 0, 0)),
                  pl.BlockSpec((1, CMP_HIDDEN, C_DIM), lambda i, b: (i // 2, 0, 0))],
        out_specs=pl.BlockSpec((1, 1, CMP_PAD + n_seg, C_DIM), lambda i, b: (i, b, 0, 0)),
        out_shape=jax.ShapeDtypeStruct((4, B, CMP_PAD + n_seg, C_DIM), BF16),
        compiler_params=_params(("parallel", "parallel")),
        name="compress",
    )(seg, w1_ul, pos, w1, cmp_w2.astype(BF16))
    return out


def _overlap_np(n_cmp_pad, n_slc):
    i = np.arange(n_cmp_pad)[:, None] - CMP_PAD
    m = np.arange(n_slc)[None, :]
    start = i * CMP_STRIDE
    ov = (start < (m + 1) * SLC_BLOCK) & (start + CMP_BLOCK - 1 >= m * SLC_BLOCK) & (i >= 0)
    return ov.astype(np.float32)


def _nsa_kernel(q_ref, ks_ref, vs_ref, kw_ref, vw_ref, kc_ref, vc_ref, ov_ref, mk_ref,
                tsel_ref, twin_ref, tcmp_ref, g_ref, o_ref,
                m_ref, acc_ref, p_ref, a_ref, pc_ref, pw_ref, sa_ref, sb_ref, oc_ref, ow_ref, qa_ref,
                *, n_sel_tab, top_n):
    H = C_HPG
    NB = ov_ref.shape[1]
    n_tiles = q_ref.shape[0] // TQ
    NT = 4
    TK = NT * TQ

    def reset():
        m_ref[...] = jnp.full(m_ref.shape, NEG, F32)
        acc_ref[...] = jnp.zeros(acc_ref.shape, F32)

    def softmax_rows(s):
        nw = s.shape[1] // 128
        smax = functools.reduce(jnp.maximum, [s[:, i * 128:(i + 1) * 128] for i in range(nw)])
        m = jnp.broadcast_to(jnp.max(smax, axis=1, keepdims=True), (TQ, 128))
        return jnp.exp2(s - jnp.concatenate([m] * nw, axis=1)), m

    def single_step_branches(a):
        qb = pl.program_id(2) * n_tiles + a
        rows = slice(a * TQ, (a + 1) * TQ)
        q_all = jnp.concatenate([q_ref[rows, h * C_DIM:(h + 1) * C_DIM] for h in range(H)], axis=0)

        t0 = qb // CMP_CLASSES + jnp.where(qb % CMP_CLASSES >= CMP_SPLIT, 1, 0)
        n_ct = kc_ref.shape[2] // TQ
        tile_kind = [3] + [jnp.where(t == t0, 0, jnp.where(t == t0 + 1, 1, jnp.where(t < t0, 2, 3)))
                           for t in range(1, n_ct)]
        s_c = _dot_nt(q_all, kc_ref[0, 0])
        vext_c = jnp.concatenate([vc_ref[0, 0], ov_ref[...]], axis=1)
        inv_c = []
        for h in range(H):
            bias = jnp.concatenate([tcmp_ref[a, tile_kind[t], h] for t in range(n_ct)], axis=1)
            p, m = softmax_rows(s_c[h * TQ:(h + 1) * TQ] + bias)
            l = jnp.sum(p, axis=1, keepdims=True)
            inv_c.append(jnp.where(m > 0.5 * NEG, 1.0 / l, 0.0))
            pc_ref[a, h * TQ:(h + 1) * TQ, :] = p.astype(BF16)
        res_c = _dot(pc_ref[a], vext_c)
        imp = None
        for h in range(H):
            r = res_c[h * TQ:(h + 1) * TQ] * jnp.concatenate([inv_c[h], inv_c[h]], axis=1)
            oc_ref[a, h * TQ:(h + 1) * TQ, :] = r[:, 0:C_DIM]
            imp = r[:, C_DIM:] if imp is None else imp + r[:, C_DIM:]

        n_wt = WIN // TQ + 1
        ww = n_wt * TQ
        st = jnp.maximum(qb - (n_wt - 1), 0)
        r0w = pl.multiple_of(st * TQ, TQ)
        s_w = _dot_nt(q_all, kw_ref[pl.ds(r0w, ww), :])
        vext_w = jnp.concatenate([vw_ref[pl.ds(r0w, ww), :], jnp.ones((ww, C_DIM), BF16)], axis=1)
        widx = [jnp.maximum(qb - (st + c) + 1, 0) for c in range(n_wt)]
        for h in range(H):
            bias = jnp.concatenate([twin_ref[i, h] for i in widx], axis=1)
            p, _ = softmax_rows(s_w[h * TQ:(h + 1) * TQ] + bias)
            pw_ref[a, h * TQ:(h + 1) * TQ, :] = p.astype(BF16)
        res_w = _dot(pw_ref[a], vext_w)
        ow_ref[a] = res_w[:, 0:C_DIM] / res_w[:, C_DIM:]

        shift = SLC_BLOCK.bit_length() - 1
        qpos = qb * TQ + lax.broadcasted_iota(jnp.int32, (TQ, NB), 0)
        mblk = lax.broadcasted_iota(jnp.int32, (TQ, NB), 1)
        qblk = jnp.right_shift(qpos, shift)
        forced = (mblk == 0) | (mblk == qblk) | (mblk == qblk - 1)
        score = jnp.where(forced, 3e38, jnp.where(jnp.left_shift(mblk, shift) <= qpos, imp, NEG))
        score_t = score.T
        blk_t = lax.broadcasted_iota(jnp.int32, (NB, TQ), 0).astype(F32)
        sel_t = jnp.zeros((NB, TQ), F32)
        for _ in range(top_n):
            mx = jnp.max(score_t, axis=0, keepdims=True)
            idx = jnp.min(jnp.where(score_t == mx, blk_t, float(NB)), axis=0, keepdims=True)
            pick = blk_t == idx
            sel_t = jnp.where(pick, 1.0, sel_t)
            score_t = jnp.where(pick, -3e38, score_t)
        unsel = (1.0 - sel_t.T).astype(BF16)
        qa_ref[a] = jnp.concatenate([q_all, jnp.concatenate([unsel] * H, axis=0)], axis=1)

    for a in range(n_tiles):
        single_step_branches(a)

    ones_k = jnp.ones((TK, C_DIM), BF16)
    for a in range(n_tiles):
        _nsa_selected(a, pl.program_id(2) * n_tiles + a, ks_ref, vs_ref, mk_ref, tsel_ref, g_ref, o_ref,
                      m_ref, acc_ref, p_ref, a_ref, sa_ref, sb_ref, oc_ref, ow_ref, qa_ref, ones_k,
                      reset, n_sel_tab)


def _nsa_selected(a, qb, ks_ref, vs_ref, mk_ref, tsel_ref, g_ref, o_ref,
                  m_ref, acc_ref, p_ref, a_ref, sa_ref, sb_ref, oc_ref, ow_ref, qa_ref, ones_k,
                  reset, n_sel_tab):
    H = C_HPG
    NT = 4
    TK = NT * TQ
    reset()
    n_steps = qb // NT + 1

    def sel_logits(kq, s_ref, near):
        kc = jnp.minimum(kq, n_steps - 1)
        r0 = pl.multiple_of(kc * TK, TK)
        k_aug = jnp.concatenate([ks_ref[pl.ds(r0, TK), :], mk_ref[pl.ds(r0, TK), :]], axis=1)
        s = _dot_nt(qa_ref[a], k_aug)
        if not near:
            s_ref[...] = s
            return
        idx = [jnp.where(kq < n_steps, jnp.clip(qb - (NT * kc + c) + 1, 0, n_sel_tab - 1), 0)
               for c in range(NT)]
        for h in range(H):
            bias = jnp.concatenate([tsel_ref[i, h] for i in idx], axis=1)
            s_ref[h * TQ:(h + 1) * TQ, :] = s[h * TQ:(h + 1) * TQ] + bias

    def sel_softmax(s_ref, slot):
        for h in range(H):
            rs = slice(h * TQ, (h + 1) * TQ)
            s = s_ref[rs, :]
            m_prev = m_ref[rs, :]
            smax = functools.reduce(jnp.maximum, [s[:, i * 128:(i + 1) * 128] for i in range(NT)])
            m_new = jnp.maximum(m_prev, jnp.max(smax, axis=1, keepdims=True))
            a_ref[slot, rs, :] = jnp.exp2(m_prev - m_new)
            p_ref[slot, rs, :] = jnp.exp2(s - jnp.concatenate([m_new] * NT, axis=1)).astype(BF16)
            m_ref[rs, :] = m_new

    def sel_pv(kq, slot):
        r0 = pl.multiple_of(jnp.clip(kq, 0, n_steps - 1) * TK, TK)
        vext = jnp.concatenate([vs_ref[pl.ds(r0, TK), :], ones_k], axis=1)
        a = a_ref[slot]
        acc_ref[...] = jnp.concatenate([a, a], axis=1) * acc_ref[...] + _dot(p_ref[slot], vext)

    def sel_run(k_lo, count, near):
        @pl.when(count > 0)
        def _():
            sel_logits(k_lo, sa_ref, near)

        def body(j, carry):
            k = k_lo + 2 * j
            sel_logits(k + 1, sb_ref, near)
            sel_softmax(sa_ref, 0)
            sel_pv(k, 0)
            sel_logits(k + 2, sa_ref, near)
            sel_softmax(sb_ref, 1)
            sel_pv(k + 1, 1)
            return carry

        lax.fori_loop(0, count // 2, body, 0)

        @pl.when(count % 2 == 1)
        def _():
            sel_softmax(sa_ref, 0)
            sel_pv(k_lo + count - 1, 0)

    n_far = jnp.maximum((qb + 1 - (n_sel_tab - 2)) // NT, 0)
    sel_run(0, n_far, False)
    sel_run(n_far, n_steps - n_far, True)
    rows = slice(a * TQ, (a + 1) * TQ)
    g = g_ref[rows, :]
    for h in range(H):
        rs = slice(h * TQ, (h + 1) * TQ)
        out_s = acc_ref[rs, 0:C_DIM] / acc_ref[rs, C_DIM:2 * C_DIM]
        o = (g[:, h:h + 1] * oc_ref[a, rs, :] + g[:, H + h:H + h + 1] * out_s
             + g[:, 2 * H + h:2 * H + h + 1] * ow_ref[a, rs, :])
        o_ref[rows, h * C_DIM:(h + 1) * C_DIM] = o.astype(BF16)


def _nsa_attention(q, kv, kvc, gates, tsel, twin, tcmp, B, S):
    N = B * S
    QT = S // TQ
    n_slc = S // SLC_BLOCK
    n_cmp_pad = kvc.shape[2]
    NB = 128
    assert n_slc <= NB and n_cmp_pad % TQ == 0 and QT % 4 == 0
    ov = jnp.asarray(_overlap_np(n_cmp_pad, NB), BF16)
    mk = jnp.asarray(np.where(np.arange(S)[:, None] // SLC_BLOCK == np.arange(NB)[None, :], NEG, 0.0), BF16)
    n_sel_delta = tsel.shape[0]
    once = pl.Buffered(1)
    kvspec = lambda c: pl.BlockSpec((S, C_DIM), lambda b, g, t: (b, c + g), pipeline_mode=once)
    cspec = lambda kvi: pl.BlockSpec((1, 1, n_cmp_pad, C_DIM), lambda b, g, t: (kvi * 2 + g, b, 0, 0),
                                     pipeline_mode=once)
    npt = NSA_TILES
    assert QT % npt == 0 and CMP_CLASSES % npt == 0
    rowblk = lambda b, g, t: (b * (QT // npt) + t, g)
    hq = C_HPG * TQ
    return pl.pallas_call(
        functools.partial(_nsa_kernel, n_sel_tab=n_sel_delta, top_n=min(SLC_TOP_N, n_slc)),
        grid=(B, C_GROUPS, QT // npt),
        in_specs=[pl.BlockSpec((npt * TQ, C_HPG * C_DIM), rowblk),
                  kvspec(0), kvspec(2), kvspec(4), kvspec(6),
                  cspec(0), cspec(1),
                  pl.BlockSpec((n_cmp_pad, NB), lambda b, g, t: (0, 0), pipeline_mode=once),
                  pl.BlockSpec((S, NB), lambda b, g, t: (0, 0), pipeline_mode=once),
                  pl.BlockSpec((n_sel_delta, C_HPG, TQ, TQ), lambda b, g, t: (0, g, 0, 0),
                               pipeline_mode=once),
                  pl.BlockSpec((twin.shape[0], C_HPG, TQ, TQ), lambda b, g, t: (0, g, 0, 0),
                               pipeline_mode=once),
                  pl.BlockSpec((npt, 4, C_HPG, TQ, TQ),
                               lambda b, g, t: (t % (CMP_CLASSES // npt), 0, g, 0, 0)),
                  pl.BlockSpec((npt * TQ, 128), rowblk)],
        out_specs=pl.BlockSpec((npt * TQ, C_HPG * C_DIM), rowblk),
        out_shape=jax.ShapeDtypeStruct((N, C_W), BF16),
        scratch_shapes=[pltpu.VMEM((hq, 128), F32),
                        pltpu.VMEM((hq, 2 * C_DIM), F32),
                        pltpu.VMEM((2, hq, 4 * TQ), BF16),
                        pltpu.VMEM((2, hq, 128), F32),
                        pltpu.VMEM((npt, hq, n_cmp_pad), BF16),
                        pltpu.VMEM((npt, hq, WIN + TQ), BF16),
                        pltpu.VMEM((hq, 4 * TQ), F32), pltpu.VMEM((hq, 4 * TQ), F32),
                        pltpu.VMEM((npt, hq, C_DIM), F32), pltpu.VMEM((npt, hq, C_DIM), F32),
                        pltpu.VMEM((npt, hq, 2 * C_DIM), BF16)],
        compiler_params=_params(("parallel", "parallel", "arbitrary")),
        name="nsa",
    )(q, kv, kv, kv, kv, kvc, kvc, ov, mk, tsel, twin, tcmp, gates)


def _layer_ab(h2, B, S, w_in, gate_b, conv_w, head_g, w_out, dil_tab):
    w_pad = jnp.pad(w_in, ((0, 0), (0, AB_PAD - AB_PROJ))).astype(BF16)
    gb_pad = jnp.pad(gate_b, (0, 128 - gate_b.shape[0])).reshape(1, 128).astype(F32)
    (aq, ak, av, aq4, ak4, av4, aq16, ak16, av16,
     bq, bk, bv, bo, gates) = _ab_proj(h2, w_pad, conv_w.astype(F32), gb_pad, S)
    ya = _dilated_attention({1: (aq, ak, av), 4: (aq4, ak4, av4), 16: (aq16, ak16, av16)}, dil_tab, B, S)
    yb = _mlstm(bq, bk, bv, gates, bo, head_g, B, S)
    wo = w_out.astype(BF16)
    return [ya, yb], [A_PATTERNS[-1][1], 1], [wo[:A_W], wo[A_W:]]


def _layer_c(h2, B, S, w_in, gate_b, cmp_pos, cmp_w1, cmp_w2, w_out, tsel, twin, tcmp):
    w_pad, gb_pad = _c_weights(w_in, gate_b)
    q, kc, kv, gates = _c_proj(h2, w_pad, gb_pad)
    kvc = _compress(kc, cmp_pos, cmp_w1, cmp_w2, B, S)
    out = _nsa_attention(q, kv, kvc, gates, tsel, twin, tcmp, B, S)
    return [out], [1], [w_out.astype(BF16)]


def kernel(x, rel_bias, ln_g, ln_b, ab_w_in, ab_gate_b, ab_conv, ab_head_norm, ab_w_out,
           c_w_in, c_gate_b, c_cmp_pos, c_cmp_w1, c_cmp_w2, c_w_out,
           moe_wr_g, moe_br_g, moe_wr_e, moe_br_e, moe_w_gate, moe_w_up, moe_w_down):
    B, S, D = x.shape
    assert D == D_MODEL and S % (TM) == 0 and S % (16 * A_BLOCK) == 0
    _check_cmp_windows(S)
    dil_tab = _bias_tables(rel_bias, _dilated_idx(), shift=False)
    tsel, twin, tcmp = _nsa_tables(rel_bias)
    h = x.reshape(B * S, D)
    for layer in range(DEPTH):
        j = layer // 2
        if layer % 2 == 0:
            ys, lays, ws = _layer_ab(h, B, S, ab_w_in[j], ab_gate_b[j], ab_conv[j], ab_head_norm[j],
                                     ab_w_out[j], dil_tab)
        else:
            ys, lays, ws = _layer_c(h, B, S, c_w_in[j], c_gate_b[j], c_cmp_pos[j], c_cmp_w1[j],
                                    c_cmp_w2[j], c_w_out[j], tsel, twin, tcmp)
        wr, br = _router_weights(moe_wr_g[layer], moe_br_g[layer], moe_wr_e[layer], moe_br_e[layer])
        hx, cnt = _out_ln_route(ys, lays, ws, h, ln_g[layer, 0], ln_b[layer, 0], wr, br)
        h = _moe(hx, cnt, layer, moe_w_gate, moe_w_up, moe_w_down, ln_g[layer, 1], ln_b[layer, 1])
    return h.reshape(B, S, D)
```

```python
import functools
import math

import numpy as np
import jax
import jax.numpy as jnp
from jax import lax
from jax.experimental import pallas as pl
from jax.experimental.pallas import tpu as pltpu

F32 = jnp.float32
BF16 = jnp.bfloat16
NEG = -1e30
LOG2E = math.log2(math.e)
VMEM_LIMIT = 48 * 1024 * 1024

D_MODEL = 1024
DEPTH = 2
ALPHA = (2.0 * DEPTH) ** 0.25
LN_EPS = 1e-5
REL_BUCKETS = 32
REL_MAX_DIST = 2048

A_HEADS, A_DIM, A_W = 8, 64, 512
A_PATTERNS = ((128, 1), (512, 4), (2048, 16))
A_BLOCK = 128
B_HEADS, B_DIM, B_W = 4, 128, 512
B_CHUNK = 128
B_CONV = 4
AB_PROJ = 3592
AB_PAD = 3712

C_HEADS, C_GROUPS, C_HPG, C_DIM, C_W = 8, 2, 4, 128, 1024
CMP_BLOCK, CMP_STRIDE, CMP_HIDDEN = 32, 16, 256
SLC_BLOCK, SLC_TOP_N, WIN = 64, 16, 512
C_PROJ = 2584
TQ = 128
NSA_TILES = 2
NSA_FUSE = 2
CMP_PAD = 128

N_GROUPS, EPG, N_EXPERTS, E_HID = 4, 4, 16, 512
N_BUCKETS = N_GROUPS * 6
TM = 512
TM_MOE = 256


def _dot(a, b):
    return jnp.dot(a, b, preferred_element_type=F32)


def _dot_nt(a, b):
    return lax.dot_general(a, b, (((1,), (1,)), ((), ())), preferred_element_type=F32)


def _params(sem):
    return pltpu.CompilerParams(dimension_semantics=sem, vmem_limit_bytes=VMEM_LIMIT)


def _bucket_np(n):
    n = np.maximum(n, 0)
    exact = REL_BUCKETS // 2
    nf = np.maximum(n, 1).astype(np.float64)
    large = exact + (np.log(nf / exact) / math.log(REL_MAX_DIST / exact)
                     * (REL_BUCKETS - exact)).astype(np.int64)
    return np.where(n < exact, n, np.minimum(large, REL_BUCKETS - 1)).astype(np.int32)


def _bias_tab_kernel(tab_ref, idx_ref, out_ref, *, shift, scale):
    R = idx_ref.shape[1]
    RC = 32

    def body(i, carry):
        r0 = pl.multiple_of(i * RC, RC)
        idx = idx_ref[0, pl.ds(r0, RC), :]
        for h in range(8):
            base = tab_ref[REL_BUCKETS - 1, h] if shift else 0.0
            val = jnp.full(idx.shape, (tab_ref[0, h] - base) * scale, F32)
            for b in range(1, REL_BUCKETS):
                val = jnp.where(idx == b, (tab_ref[b, h] - base) * scale, val)
            out_ref[0, h, pl.ds(r0, RC), :] = jnp.where(idx < 0, NEG, val)
        return carry

    lax.fori_loop(0, R // RC, body, 0)


def _bias_tables(rel_bias, idx_np, shift, scale=1.0):
    T, R, C = idx_np.shape
    return pl.pallas_call(
        functools.partial(_bias_tab_kernel, shift=shift, scale=scale),
        grid=(T,),
        in_specs=[pl.BlockSpec(memory_space=pltpu.SMEM),
                  pl.BlockSpec((1, R, C), lambda t: (t, 0, 0))],
        out_specs=pl.BlockSpec((1, 8, R, C), lambda t: (t, 0, 0, 0)),
        out_shape=jax.ShapeDtypeStruct((T, 8, R, C), F32),
        compiler_params=_params(("parallel",)),
        name="bias_tables",
    )(rel_bias.astype(F32), jnp.asarray(idx_np))


def _dilated_idx():
    qi = np.arange(A_BLOCK)[:, None]
    ki = np.arange(2 * A_BLOCK)[None, :]
    j = qi + A_BLOCK - ki
    out = []
    for window, dilation in A_PATTERNS:
        nk = window // dilation
        valid = (j >= 0) & (j <= nk)
        out.append(np.where(valid, _bucket_np(np.maximum(j, 0) * dilation), -1))
    return np.stack(out).astype(np.int32)


def _sel_idx():
    a = np.arange(TQ)[:, None]
    c = np.arange(TQ)[None, :]
    n_delta = -(-(_far_dist() + TQ) // TQ)
    out = []
    for delta in range(-1, n_delta + 1):
        dist = delta * TQ + a - c
        out.append(np.where(dist >= 0, _bucket_np(dist), -1))
    return np.stack(out).astype(np.int32)


def _far_dist():
    n = np.arange(0, 4 * REL_MAX_DIST)
    b = _bucket_np(n)
    return int(np.max(n[b < REL_BUCKETS - 1])) + 1


def _win_idx():
    a = np.arange(TQ)[:, None]
    c = np.arange(TQ)[None, :]
    out = []
    for delta in range(-1, WIN // TQ + 1):
        dist = delta * TQ + a - c
        out.append(np.where((dist >= 0) & (dist < WIN), _bucket_np(dist), -1))
    return np.stack(out).astype(np.int32)


CMP_PER_TILE = TQ // CMP_STRIDE
CMP_CLASSES = TQ // CMP_PER_TILE
CMP_SPLIT = 13


def _cmp_window_start(qb):
    return qb // CMP_CLASSES + (1 if qb % CMP_CLASSES >= CMP_SPLIT else 0)


def _cmp_idx():
    a = np.arange(TQ)[:, None]
    c = np.arange(TQ)[None, :]
    out = []
    for r in range(CMP_CLASSES):
        qb = CMP_CLASSES + r
        i0 = _cmp_window_start(qb) * TQ - CMP_PAD
        for half in range(2):
            dist = qb * TQ + a - ((i0 + half * TQ + c) * CMP_STRIDE + CMP_BLOCK - 1)
            out.append(np.where(dist >= 0, _bucket_np(dist), -1))
        out.append(np.full((TQ, TQ), REL_BUCKETS - 1))
        out.append(np.full((TQ, TQ), -1))
    return np.stack(out).astype(np.int32)


def _nsa_tables(rel_bias):
    tsel = _bias_tables(rel_bias, _sel_idx(), shift=True, scale=LOG2E)
    twin = _bias_tables(rel_bias, _win_idx(), shift=False, scale=LOG2E)
    tcmp = _bias_tables(rel_bias, _cmp_idx(), shift=True, scale=LOG2E)
    return tsel, twin, tcmp.reshape(CMP_CLASSES, 4, 8, TQ, TQ)


def _check_cmp_windows(S):
    far = _far_dist()
    for qb in range(S // TQ):
        i0 = _cmp_window_start(qb) * TQ - CMP_PAD
        s0 = qb * TQ
        assert s0 - ((i0 - 1) * CMP_STRIDE + CMP_BLOCK - 1) >= far
        assert s0 + TQ - 1 - ((i0 + 2 * TQ) * CMP_STRIDE + CMP_BLOCK - 1) < 0


def _residue_col(d, r):
    return (r % 4) * 4 + r // 4 if d == 16 else r


def _lane_chunks_store(ref3, val):
    for c in range(ref3.shape[0]):
        ref3[c] = val[:, c * 128:(c + 1) * 128]


def _to_residue_layout(src3_ref, dst, d, col_of=_residue_col):
    nc, rows, _ = src3_ref.shape
    for r in range(d):
        cb = col_of(d, r)
        for c in range(nc):
            col = (cb * nc + c) * 128
            dst(slice(col, col + 128), src3_ref[c, pl.ds(r, rows // d, stride=d), :])


def _ab_proj_kernel(x_ref, xh_ref, w_ref, cw_ref, gb_ref,
                    aq_ref, ak_ref, av_ref, aq4_ref, ak4_ref, av4_ref, aq16_ref, ak16_ref, av16_ref,
                    bq_ref, bk_ref, bv_ref, bo_ref, g_ref,
                    pre_ref, tmp_ref, *, tiles_per_seq):
    i = pl.program_id(0)
    tm = x_ref.shape[0]
    xb = x_ref[...].astype(BF16)
    for c, scale, outs in ((0, A_DIM ** -0.5, (aq_ref, aq4_ref, aq16_ref)),
                           (1, 1.0, (ak_ref, ak4_ref, ak16_ref)),
                           (2, 1.0, (av_ref, av4_ref, av16_ref))):
        val = _dot(xb, w_ref[:, c * A_W:(c + 1) * A_W]) * scale
        _lane_chunks_store(tmp_ref, val)
        outs[0][...] = val.astype(BF16)
        for d, o_ref in ((4, outs[1]), (16, outs[2])):
            def put(cols, piece, o_ref=o_ref):
                o_ref[:, cols] = piece.astype(BF16)
            _to_residue_layout(tmp_ref, put, d)
    bv_ref[...] = _dot(xb, w_ref[:, 2560:3072]).astype(BF16)
    bo_ref[...] = _dot(xb, w_ref[:, 3072:3584])
    g_ref[...] = _dot(xb, w_ref[:, 3584:AB_PAD]) + gb_ref[...]
    halo = _dot(xh_ref[...].astype(BF16), w_ref[:, 1536:2560])
    halo = jnp.where(i % tiles_per_seq == 0, 0.0, halo)
    pre_ref[0:8, :] = halo
    pre_ref[8:8 + tm, :] = _dot(xb, w_ref[:, 1536:2560])
    y = pre_ref[8:8 + tm, :] * cw_ref[B_CONV - 1:B_CONV, :]
    for k in range(B_CONV - 1):
        s = B_CONV - 1 - k
        y = y + pre_ref[8 - s:8 - s + tm, :] * cw_ref[k:k + 1, :]
    y = y / (1.0 + jnp.exp(-y))
    bq_ref[...] = (y[:, :B_W] * (B_DIM ** -0.5)).astype(BF16)
    bk_ref[...] = y[:, B_W:].astype(BF16)


def _ab_proj(x2, w_pad, conv_w, gate_b_pad, S):
    N = x2.shape[0]
    tm = TM
    tps = S // tm
    row = lambda i: (i, 0)
    fix = lambda i: (0, 0)
    lay = lambda d: [jax.ShapeDtypeStruct((N // d, d * A_W), BF16)] * 3
    lay_spec = lambda d: [pl.BlockSpec((tm // d, d * A_W), row)] * 3
    outs = lay(1) + lay(4) + lay(16) + [jax.ShapeDtypeStruct((N, 512), BF16)] * 3 + [
        jax.ShapeDtypeStruct((N, 512), F32), jax.ShapeDtypeStruct((N, 128), F32)]
    o_specs = (lay_spec(1) + lay_spec(4) + lay_spec(16) + [pl.BlockSpec((tm, 512), row)] * 4
               + [pl.BlockSpec((tm, 128), row)])
    return pl.pallas_call(
        functools.partial(_ab_proj_kernel, tiles_per_seq=tps),
        grid=(N // tm,),
        in_specs=[pl.BlockSpec((tm, D_MODEL), row),
                  pl.BlockSpec((8, D_MODEL), lambda i: (jnp.maximum(i * (tm // 8) - 1, 0), 0)),
                  pl.BlockSpec((D_MODEL, AB_PAD), fix),
                  pl.BlockSpec((B_CONV, 2 * B_W), fix),
                  pl.BlockSpec((1, 128), fix)],
        out_specs=o_specs,
        out_shape=outs,
        scratch_shapes=[pltpu.VMEM((tm + 8, 2 * B_W), F32), pltpu.VMEM((A_W // 128, tm, 128), F32)],
        compiler_params=_params(("parallel",)),
        name="ab_proj",
    )(x2, x2, w_pad, conv_w, gate_b_pad)


def _dilated_kernel(*refs, has_prev, is_last):
    if has_prev:
        q_ref, kp_ref, kc_ref, vp_ref, vc_ref, tab_ref, op_ref, lp_ref = refs[:8]
        rest = refs[8:]
    else:
        q_ref, kp_ref, kc_ref, vp_ref, vc_ref, tab_ref = refs[:6]
        rest = refs[6:]
    outs, scratch = (rest, ()) if is_last else (rest[:2], rest[2:])
    o_ref = outs[0]
    n = pl.program_id(2)
    n_sub = q_ref.shape[1] // A_BLOCK
    lane = lax.broadcasted_iota(jnp.int32, (A_BLOCK, 128), 1)
    erow = lax.broadcasted_iota(jnp.int32, (128, A_W), 0)
    ecol = lax.broadcasted_iota(jnp.int32, (128, A_W), 1)
    expand = jnp.where(erow - A_DIM == jnp.right_shift(ecol, A_DIM.bit_length() - 1), 1.0, 0.0).astype(BF16)

    def spread(t):
        hi = t.astype(BF16)
        return _dot(hi, expand) + _dot((t - hi.astype(F32)).astype(BF16), expand)

    keep_side = [jnp.where(lane < A_DIM, 1.0, 0.0).astype(BF16), jnp.where(lane < A_DIM, 0.0, 1.0).astype(BF16)]
    odd = jnp.bitwise_and(lane, 1) == 1
    ones_side = [jnp.where(odd, 0.0, 1.0).astype(BF16), jnp.where(odd, 1.0, 0.0).astype(BF16)]
    stat = (lane >= A_DIM) & (lane < A_DIM + A_HEADS)
    for sb in range(n_sub):
        rs = slice(sb * A_BLOCK, (sb + 1) * A_BLOCK)
        first = jnp.where(n == 0, NEG, 0.0) if sb == 0 else 0.0
        m_tile = jnp.zeros((A_BLOCK, 128), F32)
        l_tile = jnp.ones((A_BLOCK, 128), F32)
        unnorm = []
        for j in range(A_HEADS // 2):
            cs = slice(j * 128, (j + 1) * 128)
            q2 = q_ref[0, rs, cs]
            kc, vc = kc_ref[0, rs, cs], vc_ref[0, rs, cs]
            if sb == 0:
                kp, vp = kp_ref[0, :, cs], vp_ref[0, :, cs]
            else:
                ps = slice((sb - 1) * A_BLOCK, sb * A_BLOCK)
                kp, vp = kc_ref[0, ps, cs], vc_ref[0, ps, cs]
            k_st = jnp.concatenate([kp * keep_side[0], kc * keep_side[0],
                                    kp * keep_side[1], kc * keep_side[1]], axis=0)
            s = _dot_nt(q2, k_st)
            p_parts = []
            for side in range(2):
                h = 2 * j + side
                c0 = 2 * side * A_BLOCK
                sp = s[:, c0:c0 + A_BLOCK] + tab_ref[0, h, :, 0:A_BLOCK] + first
                sc = s[:, c0 + A_BLOCK:c0 + 2 * A_BLOCK] + tab_ref[0, h, :, A_BLOCK:2 * A_BLOCK]
                m = jnp.max(jnp.maximum(sp, sc), axis=1, keepdims=True)
                p_parts += [jnp.exp(sp - m).astype(BF16), jnp.exp(sc - m).astype(BF16)]
                m_tile = jnp.where(lane == A_DIM + h, m, m_tile)
            v_st = jnp.concatenate(
                [jnp.concatenate([v * keep_side[side], ones_side[side]], axis=1)
                 for side in range(2) for v in (vp, vc)], axis=0)
            r = _dot(jnp.concatenate(p_parts, axis=1), v_st)
            unnorm.append(r[:, 0:128])
            pair = (lane == A_DIM + 2 * j) | (lane == A_DIM + 2 * j + 1)
            l_tile = jnp.where(pair, r[:, 128:256], l_tile)
        lse = m_tile + jnp.log(l_tile)
        if has_prev:
            lp = lp_ref[0, rs, :]
            mm = jnp.maximum(lp, lse)
            wp = jnp.exp(lp - mm)
            wc = jnp.exp(lse - mm)
            tot = wp + wc
            scale_prev = jnp.where(stat, wp / tot, 0.0)
            scale_cur = jnp.where(stat, wc / (tot * l_tile), 0.0)
            lse = mm + jnp.log(tot)
        else:
            scale_cur = jnp.where(stat, 1.0 / l_tile, 0.0)
        o = jnp.concatenate(unnorm, axis=1) * spread(scale_cur)
        if has_prev:
            o = o + op_ref[0, rs, :] * spread(scale_prev)
        if is_last:
            o_ref[0, rs, :] = o.astype(o_ref.dtype)
        else:
            o_scr, l_scr = scratch[0].at[sb], scratch[1].at[sb]
            _lane_chunks_store(o_scr, o)
            l_scr[0] = jnp.where(stat, lse, 0.0)
            orows = slice(sb * (A_BLOCK // 4), (sb + 1) * (A_BLOCK // 4))

            def put_o(cols, piece, orows=orows):
                o_ref[0, orows, cols] = piece

            def put_l(cols, piece, orows=orows):
                outs[1][0, orows, cols] = piece

            _to_residue_layout(o_scr, put_o, 4)
            _to_residue_layout(l_scr, put_l, 4)


def _dilated_call(q, k, v, tab, prev, pattern_idx, dilation, B, S, is_last):
    d = dilation
    L = S // d
    nb = L // A_BLOCK
    sub = max(s for s in (8, 4, 2, 1) if nb % s == 0)
    r3 = lambda a: a.reshape(B, L, a.shape[-1])
    cur = lambda b, r, n: (b, n, r)
    prv = lambda b, r, n: (b, jnp.maximum(sub * n - 1, 0), r)
    blk = pl.BlockSpec((1, sub * A_BLOCK, A_W), cur)
    in_specs = [blk, pl.BlockSpec((1, A_BLOCK, A_W), prv), blk,
                pl.BlockSpec((1, A_BLOCK, A_W), prv), blk,
                pl.BlockSpec((1, 8, A_BLOCK, 2 * A_BLOCK), lambda b, r, n: (pattern_idx, 0, 0, 0))]
    args = [r3(q), r3(k), r3(k), r3(v), r3(v), tab]
    has_prev = prev is not None
    if has_prev:
        in_specs += [blk, pl.BlockSpec((1, sub * A_BLOCK, 128), cur)]
        args += [r3(prev[0]), r3(prev[1])]
    scratch = []
    if is_last:
        out_shape = [jax.ShapeDtypeStruct((B, L, d * A_W), BF16)]
        out_specs = [blk]
    else:
        d2, rows = 4 * d, sub * A_BLOCK // 4
        nxt = lambda b, r, n: (b, n, r)
        out_shape = [jax.ShapeDtypeStruct((B, S // d2, d2 * A_W), F32),
                     jax.ShapeDtypeStruct((B, S // d2, d2 * 128), F32)]
        out_specs = [pl.BlockSpec((1, rows, 4 * A_W), nxt), pl.BlockSpec((1, rows, 4 * 128), nxt)]
        scratch = [pltpu.VMEM((sub, A_W // 128, A_BLOCK, 128), F32),
                   pltpu.VMEM((sub, 1, A_BLOCK, 128), F32)]
    res = pl.pallas_call(
        functools.partial(_dilated_kernel, has_prev=has_prev, is_last=is_last),
        grid=(B, d, nb // sub),
        in_specs=in_specs, out_specs=out_specs, out_shape=out_shape, scratch_shapes=scratch,
        compiler_params=_params(("parallel", "parallel", "arbitrary")),
        name="dilated_d%d" % d,
    )(*args)
    return [r.reshape(-1, r.shape[-1]) for r in res]


def _dilated_attention(qkv_by_dilation, tab, B, S):
    prev = None
    for p, (window, d) in enumerate(A_PATTERNS):
        assert window // d == A_BLOCK and S % (d * A_BLOCK) == 0
        assert p == 0 or d == 4 * A_PATTERNS[p - 1][1]
        last = p == len(A_PATTERNS) - 1
        q, k, v = qkv_by_dilation[d]
        prev = _dilated_call(q, k, v, tab, prev, p, d, B, S, last)
    return prev[0]


def _split3(x):
    hi = x.astype(BF16)
    r = x - hi.astype(F32)
    mid = r.astype(BF16)
    lo = (r - mid.astype(F32)).astype(BF16)
    return hi, mid, lo


def _mlstm_kernel(q_ref, k_ref, v_ref, g_ref, bo_ref, hg_ref, y_ref, c_ref, m_ref):
    L = B_CHUNK
    c = pl.program_id(1)

    @pl.when(c == 0)
    def _():
        c_ref[...] = jnp.zeros_like(c_ref)
        m_ref[...] = jnp.zeros_like(m_ref)

    lane = lax.broadcasted_iota(jnp.int32, (L, 128), 1)
    row = lax.broadcasted_iota(jnp.int32, (L, L), 0)
    col = lax.broadcasted_iota(jnp.int32, (L, L), 1)
    tri = row >= col
    is_f = (lane >= B_HEADS) & (lane < 2 * B_HEADS)
    tril = jnp.where(tri, 1.0, 0.0).astype(BF16)
    ones = jnp.ones((L, B_DIM), BF16)
    bi = 0
    g = g_ref[bi]
    logf = jnp.minimum(g, 0.0) - jnp.log(1.0 + jnp.exp(-jnp.abs(g)))
    gl = jnp.where(is_f, logf, jnp.where(lane < B_HEADS, g, 0.0))
    hi, mid, lo = _split3(gl)
    cum = _dot(tril, hi) + _dot(tril, mid) + _dot(tril, lo)
    cum_t = cum.T
    gl_t = gl.T
    for h in range(B_HEADS):
        st = h
        cs = slice(h * B_DIM, (h + 1) * B_DIM)
        q = q_ref[bi, :, cs]
        k = k_ref[bi, :, cs]
        v_ext = jnp.concatenate([v_ref[bi, :, cs], ones], axis=1)
        b_col = cum[:, B_HEADS + h:B_HEADS + h + 1]
        b_row = cum_t[B_HEADS + h:B_HEADS + h + 1, :]
        i_col = gl[:, h:h + 1]
        i_row = gl_t[h:h + 1, :]
        m_prev = m_ref[st, 0:1, 0:1]
        dm = jnp.where(tri, b_col - b_row + i_row, NEG)
        inter = b_col + m_prev
        m_t = jnp.maximum(inter, jnp.max(dm, axis=1, keepdims=True))
        p = jnp.exp(dm - m_t)
        sqk = _dot_nt(q, k) * p
        sc = jnp.exp(inter - m_t)
        lhs = jnp.concatenate([(sc * q.astype(F32)).astype(BF16), sqk.astype(BF16)], axis=1)
        c_ext = c_ref[st]
        rhs = jnp.concatenate([c_ext.astype(BF16), v_ext], axis=0)
        res = _dot(lhs, rhs)
        num = res[:, :B_DIM]
        den = res[:, B_DIM:]
        hh = num / jnp.maximum(jnp.abs(den), jnp.exp(-m_t))
        b_last = b_col[L - 1:L, :]
        gk = b_last - b_col + i_col
        m_new = jnp.maximum(b_last + m_prev, jnp.max(gk, axis=0, keepdims=True))
        wk = jnp.exp(gk - m_new)
        decay = jnp.exp(b_last + m_prev - m_new)
        kw_t = (wk * k.astype(F32)).T.astype(BF16)
        c_ref[st] = decay * c_ext + _dot(kw_t, v_ext)
        m_ref[st] = jnp.broadcast_to(m_new, (8, 128))
        mu = jnp.mean(hh, axis=1, keepdims=True)
        xc = hh - mu
        var = jnp.mean(xc * xc, axis=1, keepdims=True)
        hn = xc * lax.rsqrt(var + LN_EPS) * hg_ref[:, cs]
        bo = bo_ref[bi, :, cs]
        y_ref[bi, :, cs] = (hn / (1.0 + jnp.exp(-bo))).astype(BF16)


def _mlstm(bq, bk, bv, gates, bo, head_g, B, S):
    nc = S // B_CHUNK
    r3 = lambda a: a.reshape(B, S, a.shape[-1])
    blk = pl.BlockSpec((1, B_CHUNK, B_W), lambda b, c: (b, c, 0))
    y = pl.pallas_call(
        _mlstm_kernel,
        grid=(B, nc),
        in_specs=[blk, blk, blk,
                  pl.BlockSpec((1, B_CHUNK, 128), lambda b, c: (b, c, 0)),
                  blk,
                  pl.BlockSpec((1, B_W), lambda b, c: (0, 0))],
        out_specs=blk,
        out_shape=jax.ShapeDtypeStruct((B, S, B_W), BF16),
        scratch_shapes=[pltpu.VMEM((B_HEADS, B_DIM, 2 * B_DIM), F32),
                        pltpu.VMEM((B_HEADS, 8, 128), F32)],
        compiler_params=_params(("parallel", "arbitrary")),
        name="mlstm",
    )(r3(bq), r3(bk), r3(bv), r3(gates), r3(bo), head_g.reshape(1, B_W).astype(F32))
    return y.reshape(B * S, B_W)


def _layer_norm(z, g, b):
    mu = jnp.mean(z, axis=1, keepdims=True)
    zc = z - mu
    var = jnp.mean(zc * zc, axis=1, keepdims=True)
    return zc * lax.rsqrt(var + LN_EPS) * g + b


def _route(logits, cnt_ref):
    tm = logits.shape[0]
    lt = logits.T
    col = lambda c: lt[c:c + 1, :]
    gl = [col(c) for c in range(N_GROUPS)]
    gmax = functools.reduce(jnp.maximum, gl)
    gsum = sum(jnp.exp(x - gmax) for x in gl)
    g_idx = jnp.full(gmax.shape, N_GROUPS - 1, jnp.int32)
    for c in range(N_GROUPS - 2, -1, -1):
        g_idx = jnp.where(gl[c] == gmax, c, g_idx)
    g_w = 1.0 / gsum
    el = []
    for k in range(EPG):
        x = col(N_GROUPS + (N_GROUPS - 1) * EPG + k)
        for g in range(N_GROUPS - 2, -1, -1):
            x = jnp.where(g_idx == g, col(N_GROUPS + g * EPG + k), x)
        el.append(x)
    v1 = functools.reduce(jnp.maximum, el)
    i1 = jnp.full(v1.shape, EPG - 1, jnp.int32)
    for k in range(EPG - 2, -1, -1):
        i1 = jnp.where(el[k] == v1, k, i1)
    el2 = [jnp.where(i1 == k, -jnp.inf, el[k]) for k in range(EPG)]
    v2 = functools.reduce(jnp.maximum, el2)
    i2 = jnp.full(v2.shape, EPG - 1, jnp.int32)
    for k in range(EPG - 2, -1, -1):
        i2 = jnp.where((el2[k] == v2) & (i1 != k), k, i2)
    t = jnp.exp(v2 - v1)
    w1 = g_w / (1.0 + t)
    w2 = w1 * t
    a = jnp.minimum(i1, i2)
    b = jnp.maximum(i1, i2)
    pair = jnp.where(a == 0, b - 1, jnp.where(a == 1, b + 1, 5))
    bucket = (g_idx * 6 + pair).astype(F32)
    w_lo = jnp.where(i1 < i2, w1, w2)
    w_hi = jnp.where(i1 < i2, w2, w1)
    sub = lax.broadcasted_iota(jnp.int32, (128, tm), 0)
    onehot_t = jnp.where(sub.astype(F32) == bucket, 1.0, 0.0)
    srow = lax.broadcasted_iota(jnp.int32, (tm, tm), 0)
    scol = lax.broadcasted_iota(jnp.int32, (tm, tm), 1)
    before = jnp.where(srow < scol, 1.0, 0.0).astype(BF16)
    oh = onehot_t.astype(BF16)
    carry = cnt_ref[...]
    prior = _dot(oh, before) + jnp.concatenate([carry] * (tm // 128), axis=1)
    rank = jnp.sum(onehot_t * prior, axis=0, keepdims=True)
    cnt_ref[...] = carry + _dot(oh, jnp.ones((tm, 128), BF16))
    out_t = jnp.where(sub == 0, bucket, jnp.where(sub == 1, w_lo, jnp.where(sub == 2, w_hi,
                      jnp.where(sub == 3, rank, 0.0))))
    return out_t.T


def _out_ln_route_kernel(*refs, n_in, layouts):
    y_refs = refs[:n_in]
    w_refs = refs[n_in:2 * n_in]
    x_ref, g_ref, b_ref, wrh_ref, wrl_ref, br_ref, h_ref, c_ref, cnt_ref = refs[2 * n_in:2 * n_in + 9]
    pos_refs = list(refs[2 * n_in + 9:])
    tm = x_ref.shape[0]

    @pl.when(pl.program_id(0) == 0)
    def _():
        cnt_ref[...] = jnp.zeros_like(cnt_ref)

    y = None
    for i in range(n_in):
        d = layouts[i]
        if d == 1:
            lhs = y_refs[i][...]
        else:
            s_ref = pos_refs.pop(0)
            nc = s_ref.shape[0]
            for r in range(d):
                cb = _residue_col(d, r)
                for c in range(nc):
                    col = (cb * nc + c) * 128
                    s_ref[c, pl.ds(r, tm // d, stride=d), :] = y_refs[i][:, col:col + 128].astype(F32)
            lhs = jnp.concatenate([s_ref[c] for c in range(nc)], axis=1).astype(BF16)
        t = _dot(lhs, w_refs[i][...])
        y = t if y is None else y + t
    hn = _layer_norm(ALPHA * x_ref[...] + y, g_ref[...], b_ref[...])
    h_ref[:, 0:D_MODEL] = hn
    hi = hn.astype(BF16)
    lo = (hn - hi.astype(F32)).astype(BF16)
    logits = (_dot_nt(hi, wrh_ref[...]) + _dot_nt(lo, wrh_ref[...]) + _dot_nt(hi, wrl_ref[...])
              + br_ref[...])
    h_ref[:, D_MODEL:D_MODEL + 128] = _route(logits, cnt_ref)
    c_ref[...] = cnt_ref[...]


def _out_ln_route(ys, layouts, ws, x2, ln_g, ln_b, wr, br):
    N = x2.shape[0]
    tm = TM
    row = lambda i: (i, 0)
    fix = lambda i: (0, 0)
    wr_hi = wr.astype(BF16)
    wr_lo = (wr - wr_hi.astype(F32)).astype(BF16)
    in_specs = ([pl.BlockSpec((tm // d, y.shape[1]), row) for y, d in zip(ys, layouts)]
                + [pl.BlockSpec(w.shape, fix) for w in ws]
                + [pl.BlockSpec((tm, D_MODEL), row),
                   pl.BlockSpec((1, D_MODEL), fix), pl.BlockSpec((1, D_MODEL), fix),
                   pl.BlockSpec((128, D_MODEL), fix), pl.BlockSpec((128, D_MODEL), fix),
                   pl.BlockSpec((1, 128), fix)])
    return pl.pallas_call(
        functools.partial(_out_ln_route_kernel, n_in=len(ys), layouts=tuple(layouts)),
        grid=(N // tm,),
        in_specs=in_specs,
        out_specs=[pl.BlockSpec((tm, D_MODEL + 128), row), pl.BlockSpec((128, 128), fix)],
        out_shape=[jax.ShapeDtypeStruct((N, D_MODEL + 128), F32), jax.ShapeDtypeStruct((128, 128), F32)],
        scratch_shapes=[pltpu.VMEM((128, 128), F32)] + [
            pltpu.VMEM((y.shape[1] // d // 128, tm, 128), F32) for y, d in zip(ys, layouts) if d != 1],
        compiler_params=_params(("arbitrary",)),
        name="out_ln_route",
    )(*ys, *ws, x2, ln_g.reshape(1, -1), ln_b.reshape(1, -1), wr_hi, wr_lo, br)


def _router_weights(wr_g, br_g, wr_e, br_e):
    we = wr_e.transpose(0, 2, 1).reshape(N_GROUPS * EPG, D_MODEL)
    w = jnp.concatenate([wr_g.T, we], axis=0)
    w = jnp.pad(w, ((0, 128 - w.shape[0]), (0, 0)))
    b = jnp.concatenate([br_g, br_e.reshape(-1)])
    b = jnp.pad(b, (0, 128 - b.shape[0])).reshape(1, 128)
    return w.astype(F32), b.astype(F32)


_PAIRS = ((0, 1), (0, 2), (0, 3), (1, 2), (1, 3), (2, 3))


def _moe_kernel(elo_ref, ehi_ref, chg_ref, nt_ref,
                x_ref, wgl_ref, wul_ref, wdl_ref, wgh_ref, wuh_ref, wdh_ref,
                g_ref, b_ref, o_ref, wg_s, wu_s, wd_s):
    t = pl.program_id(0)

    @pl.when(chg_ref[t] == 1)
    def _():
        wg_s[0] = wgl_ref[0, 0].astype(BF16)
        wu_s[0] = wul_ref[0, 0].astype(BF16)
        wd_s[0] = wdl_ref[0, 0].astype(BF16)
        wg_s[1] = wgh_ref[0, 0].astype(BF16)
        wu_s[1] = wuh_ref[0, 0].astype(BF16)
        wd_s[1] = wdh_ref[0, 0].astype(BF16)

    @pl.when(t < nt_ref[0])
    def _():
        x = x_ref[:, 0:D_MODEL]
        xb = x.astype(BF16)
        r = x_ref[:, D_MODEL:D_MODEL + 128]
        acc = None
        for e in range(2):
            a = _dot(xb, wg_s[e])
            u = _dot(xb, wu_s[e])
            hcur = (a / (1.0 + jnp.exp(-a))) * u * r[:, 1 + e:2 + e]
            y = _dot(hcur.astype(BF16), wd_s[e])
            acc = y if acc is None else acc + y
        o_ref[...] = _layer_norm(ALPHA * x + acc, g_ref[...], b_ref[...])

    @pl.when(t >= nt_ref[0])
    def _():
        o_ref[...] = jnp.zeros_like(o_ref)


def _moe(hx, cnt, layer, w_gate, w_up, w_down, ln_g, ln_b):
    N = hx.shape[0]
    tm = TM_MOE
    n_tiles = N // tm + N_BUCKETS
    n_pad = n_tiles * tm
    bucket = hx[:, D_MODEL].astype(jnp.int32)
    rank = hx[:, D_MODEL + 3].astype(jnp.int32)
    counts = cnt[:N_BUCKETS, 0].astype(jnp.int32)
    padded = ((counts + tm - 1) // tm) * tm
    ends = jnp.cumsum(padded)
    offs = ends - padded
    b2 = bucket.reshape(-1, 128)
    off2 = functools.reduce(lambda acc, b: jnp.where(b2 == b, offs[b], acc), range(N_BUCKETS),
                            jnp.zeros_like(b2))
    dest = off2.reshape(-1) + rank
    src = (jnp.arange(n_pad, dtype=jnp.int32) % N).at[dest].set(
        jnp.arange(N, dtype=jnp.int32), mode="promise_in_bounds", unique_indices=True)
    tile_start = jnp.arange(n_tiles, dtype=jnp.int32) * tm
    n_used = (ends[-1] // tm).astype(jnp.int32)
    tb = jnp.sum((tile_start[:, None] >= ends[None, :]).astype(jnp.int32), axis=1)
    tb_last = jnp.take(tb, jnp.maximum(n_used - 1, 0))
    tb = jnp.where(tile_start < ends[-1], tb, tb_last)
    pairs = jnp.asarray(_PAIRS, jnp.int32)
    elo = (tb // 6) * EPG + pairs[tb % 6, 0]
    ehi = (tb // 6) * EPG + pairs[tb % 6, 1]
    chg = jnp.concatenate([jnp.ones((1,), jnp.int32), (tb[1:] != tb[:-1]).astype(jnp.int32)])
    xs = hx.at[src].get(mode="promise_in_bounds")

    row = lambda t, *_: (t, 0)
    fix = lambda t, *_: (0, 0)
    wlo = lambda t, elo, ehi, chg, nt: (layer, elo[t], 0, 0)
    whi = lambda t, elo, ehi, chg, nt: (layer, ehi[t], 0, 0)
    up_spec = lambda im: pl.BlockSpec((1, 1, D_MODEL, E_HID), im)
    dn_spec = lambda im: pl.BlockSpec((1, 1, E_HID, D_MODEL), im)
    grid_spec = pltpu.PrefetchScalarGridSpec(
        num_scalar_prefetch=4,
        grid=(n_tiles,),
        in_specs=[pl.BlockSpec((tm, D_MODEL + 128), row),
                  up_spec(wlo), up_spec(wlo), dn_spec(wlo),
                  up_spec(whi), up_spec(whi), dn_spec(whi),
                  pl.BlockSpec((1, D_MODEL), fix), pl.BlockSpec((1, D_MODEL), fix)],
        out_specs=pl.BlockSpec((tm, D_MODEL), row),
        scratch_shapes=[pltpu.VMEM((2, D_MODEL, E_HID), BF16),
                        pltpu.VMEM((2, D_MODEL, E_HID), BF16),
                        pltpu.VMEM((2, E_HID, D_MODEL), BF16)])
    out_sorted = pl.pallas_call(
        _moe_kernel,
        grid_spec=grid_spec,
        out_shape=jax.ShapeDtypeStruct((n_pad, D_MODEL), F32),
        compiler_params=_params(("arbitrary",)),
        name="moe",
    )(elo, ehi, chg, n_used.reshape(1), xs, w_gate, w_up, w_down, w_gate, w_up, w_down,
      ln_g.reshape(1, -1), ln_b.reshape(1, -1))
    return out_sorted.at[dest].get(mode="promise_in_bounds", unique_indices=True)


def _c_proj_kernel(x_ref, w_ref, gb_ref, q_ref, kc_ref, kv_ref, g_ref, tmp_ref):
    xb = x_ref[...].astype(BF16)
    q_ref[...] = (_dot(xb, w_ref[:, 0:C_W]) * (C_DIM ** -0.5 * LOG2E)).astype(BF16)
    for i in range(4):
        tmp_ref[0] = _dot(xb, w_ref[:, C_W + i * 128:C_W + (i + 1) * 128])

        def put(cols, piece, i=i):
            kc_ref[i, :, cols] = piece.astype(BF16)

        _to_residue_layout(tmp_ref, put, CMP_STRIDE, col_of=lambda d, r: r)
    kv_ref[...] = _dot(xb, w_ref[:, C_W + 512:C_W + 1536]).astype(BF16)
    z = _dot(xb, w_ref[:, C_W + 1536:C_W + 1792]) + gb_ref[...]
    g_ref[...] = 1.0 / (1.0 + jnp.exp(-z))


def _c_proj(x2, w_pad, gb_pad):
    N = x2.shape[0]
    tm = TM
    row = lambda i: (i, 0)
    fix = lambda i: (0, 0)
    wcols = w_pad.shape[1]
    return pl.pallas_call(
        _c_proj_kernel,
        grid=(N // tm,),
        in_specs=[pl.BlockSpec((tm, D_MODEL), row), pl.BlockSpec((D_MODEL, wcols), fix),
                  pl.BlockSpec((1, 256), fix)],
        out_specs=[pl.BlockSpec((tm, C_W), row),
                   pl.BlockSpec((4, tm // CMP_STRIDE, CMP_STRIDE * C_DIM), lambda i: (0, i, 0)),
                   pl.BlockSpec((tm, 1024), row), pl.BlockSpec((tm, 256), row)],
        out_shape=[jax.ShapeDtypeStruct((N, C_W), BF16),
                   jax.ShapeDtypeStruct((4, N // CMP_STRIDE, CMP_STRIDE * C_DIM), BF16),
                   jax.ShapeDtypeStruct((N, 1024), BF16), jax.ShapeDtypeStruct((N, 256), F32)],
        scratch_shapes=[pltpu.VMEM((1, tm, 128), F32)],
        compiler_params=_params(("parallel",)),
        name="c_proj",
    )(x2, w_pad, gb_pad)


def _c_weights(w_in, gate_b):
    gcols = []
    gb = []
    for g in range(C_GROUPS):
        idx = [C_PROJ - 3 * C_HEADS + br * C_HEADS + g * C_HPG + j for br in range(3) for j in range(C_HPG)]
        gcols.append(jnp.pad(w_in[:, np.asarray(idx)], ((0, 0), (0, 128 - len(idx)))))
        gb.append(jnp.pad(gate_b[np.asarray(idx) - (C_PROJ - 3 * C_HEADS)], (0, 128 - len(idx))))
    w = jnp.concatenate([w_in[:, :C_PROJ - 3 * C_HEADS]] + gcols, axis=1).astype(BF16)
    return w, jnp.concatenate(gb).reshape(1, 256).astype(F32)


def _compress_kernel(seg_ref, w1_ref, pos_ref, w1f_ref, w2_ref, o_ref):
    n_seg = seg_ref.shape[1]
    ul = _dot(seg_ref[0], w1_ref[0])
    u = ul[:, :CMP_HIDDEN]
    lnext = pltpu.roll(ul[:, CMP_HIDDEN:], n_seg - 1, 0)
    cpos = _dot(pos_ref[0], w1f_ref[0])[0:1, :]
    pre = u + lnext + cpos
    act = 0.5 * pre * (1.0 + jnp.tanh(math.sqrt(2.0 / math.pi) * (pre + 0.044715 * pre * pre * pre)))
    o_ref[0, 0, 0:CMP_PAD, :] = jnp.zeros((CMP_PAD, C_DIM), BF16)
    o_ref[0, 0, CMP_PAD:CMP_PAD + n_seg, :] = _dot(act.astype(BF16), w2_ref[0]).astype(BF16)


def _compress(kc, cmp_pos, cmp_w1, cmp_w2, B, S):
    n_seg = S // CMP_STRIDE
    half = CMP_STRIDE * C_DIM
    seg = kc.reshape(4 * B, n_seg, half)
    w1 = cmp_w1.astype(BF16)
    w1_ul = jnp.concatenate([w1[:, :half], w1[:, half:]], axis=2)
    pos = jnp.broadcast_to(cmp_pos.reshape(2, 1, CMP_BLOCK * C_DIM), (2, 8, CMP_BLOCK * C_DIM)).astype(BF16)
    out = pl.pallas_call(
        _compress_kernel,
        grid=(4, B),
        in_specs=[pl.BlockSpec((1, n_seg, half), lambda i, b: (i * B + b, 0, 0)),
                  pl.BlockSpec((1, half, 2 * CMP_HIDDEN), lambda i, b: (i // 2, 0, 0)),
                  pl.BlockSpec((1, 8, CMP_BLOCK * C_DIM), lambda i, b: (i // 2, 0, 0)),
                  pl.BlockSpec((1, CMP_BLOCK * C_DIM, CMP_HIDDEN), lambda i, b: (i // 2, 0, 0)),
                  pl.BlockSpec((1, CMP_HIDDEN, C_DIM), lambda i, b: (i // 2, 0, 0))],
        out_specs=pl.BlockSpec((1, 1, CMP_PAD + n_seg, C_DIM), lambda i, b: (i, b, 0, 0)),
        out_shape=jax.ShapeDtypeStruct((4, B, CMP_PAD + n_seg, C_DIM), BF16),
        compiler_params=_params(("parallel", "parallel")),
        name="compress",
    )(seg, w1_ul, pos, w1, cmp_w2.astype(BF16))
    return out


def _overlap_np(n_cmp_pad, n_slc):
    i = np.arange(n_cmp_pad)[:, None] - CMP_PAD
    m = np.arange(n_slc)[None, :]
    start = i * CMP_STRIDE
    ov = (start < (m + 1) * SLC_BLOCK) & (start + CMP_BLOCK - 1 >= m * SLC_BLOCK) & (i >= 0)
    return ov.astype(np.float32)


def _nsa_kernel(q_ref, ks_ref, vs_ref, kw_ref, vw_ref, kc_ref, vc_ref, ov_ref, mk_ref,
                tsel_ref, twin_ref, tcmp_ref, g_ref, o_ref,
                m_ref, acc_ref, p_ref, a_ref, pc_ref, pw_ref, sa_ref, sb_ref, oc_ref, ow_ref, qa_ref,
                *, n_sel_tab, top_n):
    H = C_HPG
    NB = ov_ref.shape[1]
    n_tiles = q_ref.shape[0] // TQ
    NT = 4
    TK = NT * TQ

    def reset():
        m_ref[...] = jnp.full(m_ref.shape, NEG, F32)
        acc_ref[...] = jnp.zeros(acc_ref.shape, F32)

    def softmax_rows(s):
        nw = s.shape[1] // 128
        smax = functools.reduce(jnp.maximum, [s[:, i * 128:(i + 1) * 128] for i in range(nw)])
        m = jnp.broadcast_to(jnp.max(smax, axis=1, keepdims=True), (TQ, 128))
        return jnp.exp2(s - jnp.concatenate([m] * nw, axis=1)), m

    def single_step_branches(a):
        qb = pl.program_id(2) * n_tiles + a
        rows = slice(a * TQ, (a + 1) * TQ)
        q_all = jnp.concatenate([q_ref[rows, h * C_DIM:(h + 1) * C_DIM] for h in range(H)], axis=0)

        t0 = qb // CMP_CLASSES + jnp.where(qb % CMP_CLASSES >= CMP_SPLIT, 1, 0)
        n_ct = kc_ref.shape[2] // TQ
        tile_kind = [3] + [jnp.where(t == t0, 0, jnp.where(t == t0 + 1, 1, jnp.where(t < t0, 2, 3)))
                           for t in range(1, n_ct)]
        s_c = _dot_nt(q_all, kc_ref[0, 0])
        vext_c = jnp.concatenate([vc_ref[0, 0], ov_ref[...]], axis=1)
        inv_c = []
        for h in range(H):
            bias = jnp.concatenate([tcmp_ref[a, tile_kind[t], h] for t in range(n_ct)], axis=1)
            p, m = softmax_rows(s_c[h * TQ:(h + 1) * TQ] + bias)
            l = jnp.sum(p, axis=1, keepdims=True)
            inv_c.append(jnp.where(m > 0.5 * NEG, 1.0 / l, 0.0))
            pc_ref[a, h * TQ:(h + 1) * TQ, :] = p.astype(BF16)
        res_c = _dot(pc_ref[a], vext_c)
        imp = None
        for h in range(H):
            r = res_c[h * TQ:(h + 1) * TQ] * jnp.concatenate([inv_c[h], inv_c[h]], axis=1)
            oc_ref[a, h * TQ:(h + 1) * TQ, :] = r[:, 0:C_DIM]
            imp = r[:, C_DIM:] if imp is None else imp + r[:, C_DIM:]

        n_wt = WIN // TQ + 1
        ww = n_wt * TQ
        st = jnp.maximum(qb - (n_wt - 1), 0)
        r0w = pl.multiple_of(st * TQ, TQ)
        s_w = _dot_nt(q_all, kw_ref[pl.ds(r0w, ww), :])
        vext_w = jnp.concatenate([vw_ref[pl.ds(r0w, ww), :], jnp.ones((ww, C_DIM), BF16)], axis=1)
        widx = [jnp.maximum(qb - (st + c) + 1, 0) for c in range(n_wt)]
        for h in range(H):
            bias = jnp.concatenate([twin_ref[i, h] for i in widx], axis=1)
            p, _ = softmax_rows(s_w[h * TQ:(h + 1) * TQ] + bias)
            pw_ref[a, h * TQ:(h + 1) * TQ, :] = p.astype(BF16)
        res_w = _dot(pw_ref[a], vext_w)
        ow_ref[a] = res_w[:, 0:C_DIM] / res_w[:, C_DIM:]

        shift = SLC_BLOCK.bit_length() - 1
        qpos = qb * TQ + lax.broadcasted_iota(jnp.int32, (TQ, NB), 0)
        mblk = lax.broadcasted_iota(jnp.int32, (TQ, NB), 1)
        qblk = jnp.right_shift(qpos, shift)
        forced = (mblk == 0) | (mblk == qblk) | (mblk == qblk - 1)
        score = jnp.where(forced, 3e38, jnp.where(jnp.left_shift(mblk, shift) <= qpos, imp, NEG))
        score_t = score.T
        blk_t = lax.broadcasted_iota(jnp.int32, (NB, TQ), 0).astype(F32)
        sel_t = jnp.zeros((NB, TQ), F32)
        for _ in range(top_n):
            mx = jnp.max(score_t, axis=0, keepdims=True)
            idx = jnp.min(jnp.where(score_t == mx, blk_t, float(NB)), axis=0, keepdims=True)
            pick = blk_t == idx
            sel_t = jnp.where(pick, 1.0, sel_t)
            score_t = jnp.where(pick, -3e38, score_t)
        unsel = (1.0 - sel_t.T).astype(BF16)
        qa_ref[a * H * TQ:(a + 1) * H * TQ, :] = jnp.concatenate(
            [q_all, jnp.concatenate([unsel] * H, axis=0)], axis=1)

    for a in range(n_tiles):
        single_step_branches(a)

    ones_k = jnp.ones((TK, C_DIM), BF16)
    n_fuse = m_ref.shape[0] // (H * TQ)
    for a in range(0, n_tiles, n_fuse):
        _nsa_selected(a, n_fuse, pl.program_id(2) * n_tiles + a, ks_ref, vs_ref, mk_ref, tsel_ref, g_ref,
                      o_ref, m_ref, acc_ref, p_ref, a_ref, sa_ref, sb_ref, oc_ref, ow_ref, qa_ref, ones_k,
                      reset, n_sel_tab)


def _nsa_selected(a0, n_fuse, qb0, ks_ref, vs_ref, mk_ref, tsel_ref, g_ref, o_ref,
                  m_ref, acc_ref, p_ref, a_ref, sa_ref, sb_ref, oc_ref, ow_ref, qa_ref, ones_k,
                  reset, n_sel_tab):
    H = C_HPG
    NT = 4
    TK = NT * TQ
    hq = H * TQ
    reset()
    n_steps = (qb0 + n_fuse - 1) // NT + 1

    def sel_logits(kq, s_ref, near):
        kc = jnp.minimum(kq, n_steps - 1)
        r0 = pl.multiple_of(kc * TK, TK)
        k_aug = jnp.concatenate([ks_ref[pl.ds(r0, TK), :], mk_ref[pl.ds(r0, TK), :]], axis=1)
        s = _dot_nt(qa_ref[a0 * hq:(a0 + n_fuse) * hq, :], k_aug)
        if not near:
            s_ref[...] = s
            return
        for i in range(n_fuse):
            idx = [jnp.where(kq < n_steps, jnp.clip(qb0 + i - (NT * kc + c) + 1, 0, n_sel_tab - 1), 0)
                   for c in range(NT)]
            for h in range(H):
                rs = slice((i * H + h) * TQ, (i * H + h + 1) * TQ)
                s_ref[rs, :] = s[rs] + jnp.concatenate([tsel_ref[j, h] for j in idx], axis=1)

    def sel_softmax(s_ref, slot):
        for h in range(n_fuse * H):
            rs = slice(h * TQ, (h + 1) * TQ)
            s = s_ref[rs, :]
            m_prev = m_ref[rs, :]
            smax = functools.reduce(jnp.maximum, [s[:, i * 128:(i + 1) * 128] for i in range(NT)])
            m_new = jnp.maximum(m_prev, jnp.max(smax, axis=1, keepdims=True))
            a_ref[slot, rs, :] = jnp.exp2(m_prev - m_new)
            p_ref[slot, rs, :] = jnp.exp2(s - jnp.concatenate([m_new] * NT, axis=1)).astype(BF16)
            m_ref[rs, :] = m_new

    def sel_pv(kq, slot):
        r0 = pl.multiple_of(jnp.clip(kq, 0, n_steps - 1) * TK, TK)
        vext = jnp.concatenate([vs_ref[pl.ds(r0, TK), :], ones_k], axis=1)
        a = a_ref[slot]
        acc_ref[...] = jnp.concatenate([a, a], axis=1) * acc_ref[...] + _dot(p_ref[slot], vext)

    def sel_run(k_lo, count, near):
        @pl.when(count > 0)
        def _():
            sel_logits(k_lo, sa_ref, near)

        def body(j, carry):
            k = k_lo + 2 * j
            sel_logits(k + 1, sb_ref, near)
            sel_softmax(sa_ref, 0)
            sel_pv(k, 0)
            sel_logits(k + 2, sa_ref, near)
            sel_softmax(sb_ref, 1)
            sel_pv(k + 1, 1)
            return carry

        lax.fori_loop(0, count // 2, body, 0)

        @pl.when(count % 2 == 1)
        def _():
            sel_softmax(sa_ref, 0)
            sel_pv(k_lo + count - 1, 0)

    n_far = jnp.maximum((qb0 + 1 - (n_sel_tab - 2)) // NT, 0)
    sel_run(0, n_far, False)
    sel_run(n_far, n_steps - n_far, True)
    for i in range(n_fuse):
        a = a0 + i
        rows = slice(a * TQ, (a + 1) * TQ)
        g = g_ref[rows, :]
        for h in range(H):
            rs = slice(h * TQ, (h + 1) * TQ)
            fr = slice((i * H + h) * TQ, (i * H + h + 1) * TQ)
            out_s = acc_ref[fr, 0:C_DIM] / acc_ref[fr, C_DIM:2 * C_DIM]
            o = (g[:, h:h + 1] * oc_ref[a, rs, :] + g[:, H + h:H + h + 1] * out_s
                 + g[:, 2 * H + h:2 * H + h + 1] * ow_ref[a, rs, :])
            o_ref[rows, h * C_DIM:(h + 1) * C_DIM] = o.astype(BF16)


def _nsa_attention(q, kv, kvc, gates, tsel, twin, tcmp, B, S):
    N = B * S
    QT = S // TQ
    n_slc = S // SLC_BLOCK
    n_cmp_pad = kvc.shape[2]
    NB = 128
    assert n_slc <= NB and n_cmp_pad % TQ == 0 and QT % 4 == 0
    ov = jnp.asarray(_overlap_np(n_cmp_pad, NB), BF16)
    mk = jnp.asarray(np.where(np.arange(S)[:, None] // SLC_BLOCK == np.arange(NB)[None, :], NEG, 0.0), BF16)
    n_sel_delta = tsel.shape[0]
    once = pl.Buffered(1)
    kvspec = lambda c: pl.BlockSpec((S, C_DIM), lambda b, g, t: (b, c + g), pipeline_mode=once)
    cspec = lambda kvi: pl.BlockSpec((1, 1, n_cmp_pad, C_DIM), lambda b, g, t: (kvi * 2 + g, b, 0, 0),
                                     pipeline_mode=once)
    npt = NSA_TILES
    nf = NSA_FUSE
    assert QT % npt == 0 and CMP_CLASSES % npt == 0 and npt % nf == 0 and 4 % npt == 0
    rowblk = lambda b, g, t: (b * (QT // npt) + t, g)
    hq = C_HPG * TQ
    return pl.pallas_call(
        functools.partial(_nsa_kernel, n_sel_tab=n_sel_delta, top_n=min(SLC_TOP_N, n_slc)),
        grid=(B, C_GROUPS, QT // npt),
        in_specs=[pl.BlockSpec((npt * TQ, C_HPG * C_DIM), rowblk),
                  kvspec(0), kvspec(2), kvspec(4), kvspec(6),
                  cspec(0), cspec(1),
                  pl.BlockSpec((n_cmp_pad, NB), lambda b, g, t: (0, 0), pipeline_mode=once),
                  pl.BlockSpec((S, NB), lambda b, g, t: (0, 0), pipeline_mode=once),
                  pl.BlockSpec((n_sel_delta, C_HPG, TQ, TQ), lambda b, g, t: (0, g, 0, 0),
                               pipeline_mode=once),
                  pl.BlockSpec((twin.shape[0], C_HPG, TQ, TQ), lambda b, g, t: (0, g, 0, 0),
                               pipeline_mode=once),
                  pl.BlockSpec((npt, 4, C_HPG, TQ, TQ),
                               lambda b, g, t: (t % (CMP_CLASSES // npt), 0, g, 0, 0)),
                  pl.BlockSpec((npt * TQ, 128), rowblk)],
        out_specs=pl.BlockSpec((npt * TQ, C_HPG * C_DIM), rowblk),
        out_shape=jax.ShapeDtypeStruct((N, C_W), BF16),
        scratch_shapes=[pltpu.VMEM((nf * hq, 128), F32),
                        pltpu.VMEM((nf * hq, 2 * C_DIM), F32),
                        pltpu.VMEM((2, nf * hq, 4 * TQ), BF16),
                        pltpu.VMEM((2, nf * hq, 128), F32),
                        pltpu.VMEM((npt, hq, n_cmp_pad), BF16),
                        pltpu.VMEM((npt, hq, WIN + TQ), BF16),
                        pltpu.VMEM((nf * hq, 4 * TQ), F32), pltpu.VMEM((nf * hq, 4 * TQ), F32),
                        pltpu.VMEM((npt, hq, C_DIM), F32), pltpu.VMEM((npt, hq, C_DIM), F32),
                        pltpu.VMEM((npt * hq, 2 * C_DIM), BF16)],
        compiler_params=_params(("parallel", "parallel", "arbitrary")),
        name="nsa",
    )(q, kv, kv, kv, kv, kvc, kvc, ov, mk, tsel, twin, tcmp, gates)


def _layer_ab(h2, B, S, w_in, gate_b, conv_w, head_g, w_out, dil_tab):
    w_pad = jnp.pad(w_in, ((0, 0), (0, AB_PAD - AB_PROJ))).astype(BF16)
    gb_pad = jnp.pad(gate_b, (0, 128 - gate_b.shape[0])).reshape(1, 128).astype(F32)
    (aq, ak, av, aq4, ak4, av4, aq16, ak16, av16,
     bq, bk, bv, bo, gates) = _ab_proj(h2, w_pad, conv_w.astype(F32), gb_pad, S)
    ya = _dilated_attention({1: (aq, ak, av), 4: (aq4, ak4, av4), 16: (aq16, ak16, av16)}, dil_tab, B, S)
    yb = _mlstm(bq, bk, bv, gates, bo, head_g, B, S)
    wo = w_out.astype(BF16)
    return [ya, yb], [A_PATTERNS[-1][1], 1], [wo[:A_W], wo[A_W:]]


def _layer_c(h2, B, S, w_in, gate_b, cmp_pos, cmp_w1, cmp_w2, w_out, tsel, twin, tcmp):
    w_pad, gb_pad = _c_weights(w_in, gate_b)
    q, kc, kv, gates = _c_proj(h2, w_pad, gb_pad)
    kvc = _compress(kc, cmp_pos, cmp_w1, cmp_w2, B, S)
    out = _nsa_attention(q, kv, kvc, gates, tsel, twin, tcmp, B, S)
    return [out], [1], [w_out.astype(BF16)]


def kernel(x, rel_bias, ln_g, ln_b, ab_w_in, ab_gate_b, ab_conv, ab_head_norm, ab_w_out,
           c_w_in, c_gate_b, c_cmp_pos, c_cmp_w1, c_cmp_w2, c_w_out,
           moe_wr_g, moe_br_g, moe_wr_e, moe_br_e, moe_w_gate, moe_w_up, moe_w_down):
    B, S, D = x.shape
    assert D == D_MODEL and S % (TM) == 0 and S % (16 * A_BLOCK) == 0
    _check_cmp_windows(S)
    dil_tab = _bias_tables(rel_bias, _dilated_idx(), shift=False)
    tsel, twin, tcmp = _nsa_tables(rel_bias)
    h = x.reshape(B * S, D)
    for layer in range(DEPTH):
        j = layer // 2
        if layer % 2 == 0:
            ys, lays, ws = _layer_ab(h, B, S, ab_w_in[j], ab_gate_b[j], ab_conv[j], ab_head_norm[j],
                                     ab_w_out[j], dil_tab)
        else:
            ys, lays, ws = _layer_c(h, B, S, c_w_in[j], c_gate_b[j], c_cmp_pos[j], c_cmp_w1[j],
                                    c_cmp_w2[j], c_w_out[j], tsel, twin, tcmp)
        wr, br = _router_weights(moe_wr_g[layer], moe_br_g[layer], moe_wr_e[layer], moe_br_e[layer])
        hx, cnt = _out_ln_route(ys, lays, ws, h, ln_g[layer, 0], ln_b[layer, 0], wr, br)
        h = _moe(hx, cnt, layer, moe_w_gate, moe_w_up, moe_w_down, ln_g[layer, 1], ln_b[layer, 1])
    return h.reshape(B, S, D)
```

```python
import functools
import math

import numpy as np
import jax
import jax.numpy as jnp
from jax import lax
from jax.experimental import pallas as pl
from jax.experimental.pallas import tpu as pltpu

F32 = jnp.float32
BF16 = jnp.bfloat16
NEG = -1e30
LOG2E = math.log2(math.e)
VMEM_LIMIT = 48 * 1024 * 1024

D_MODEL = 1024
DEPTH = 2
ALPHA = (2.0 * DEPTH) ** 0.25
LN_EPS = 1e-5
REL_BUCKETS = 32
REL_MAX_DIST = 2048

A_HEADS, A_DIM, A_W = 8, 64, 512
A_PATTERNS = ((128, 1), (512, 4), (2048, 16))
A_BLOCK = 128
B_HEADS, B_DIM, B_W = 4, 128, 512
B_CHUNK = 128
B_CONV = 4
AB_PROJ = 3592
AB_PAD = 3712

C_HEADS, C_GROUPS, C_HPG, C_DIM, C_W = 8, 2, 4, 128, 1024
CMP_BLOCK, CMP_STRIDE, CMP_HIDDEN = 32, 16, 256
SLC_BLOCK, SLC_TOP_N, WIN = 64, 16, 512
C_PROJ = 2584
TQ = 128
NSA_TILES = 4
NSA_VMEM_LIMIT = 56 * 1024 * 1024
NSA_FUSE = 2
CMP_PAD = 128

N_GROUPS, EPG, N_EXPERTS, E_HID = 4, 4, 16, 512
N_BUCKETS = N_GROUPS * 6
TM = 512
TM_MOE = 256


def _dot(a, b):
    return jnp.dot(a, b, preferred_element_type=F32)


def _dot_nt(a, b):
    return lax.dot_general(a, b, (((1,), (1,)), ((), ())), preferred_element_type=F32)


def _params(sem, vmem_limit=VMEM_LIMIT):
    return pltpu.CompilerParams(dimension_semantics=sem, vmem_limit_bytes=vmem_limit)


def _bucket_np(n):
    n = np.maximum(n, 0)
    exact = REL_BUCKETS // 2
    nf = np.maximum(n, 1).astype(np.float64)
    large = exact + (np.log(nf / exact) / math.log(REL_MAX_DIST / exact)
                     * (REL_BUCKETS - exact)).astype(np.int64)
    return np.where(n < exact, n, np.minimum(large, REL_BUCKETS - 1)).astype(np.int32)


def _bias_tab_kernel(tab_ref, idx_ref, out_ref, *, shift, scale):
    R = idx_ref.shape[1]
    RC = 32

    def body(i, carry):
        r0 = pl.multiple_of(i * RC, RC)
        idx = idx_ref[0, pl.ds(r0, RC), :]
        for h in range(8):
            base = tab_ref[REL_BUCKETS - 1, h] if shift else 0.0
            val = jnp.full(idx.shape, (tab_ref[0, h] - base) * scale, F32)
            for b in range(1, REL_BUCKETS):
                val = jnp.where(idx == b, (tab_ref[b, h] - base) * scale, val)
            out_ref[0, h, pl.ds(r0, RC), :] = jnp.where(idx < 0, NEG, val)
        return carry

    lax.fori_loop(0, R // RC, body, 0)


def _bias_tables(rel_bias, idx_np, shift, scale=1.0):
    T, R, C = idx_np.shape
    return pl.pallas_call(
        functools.partial(_bias_tab_kernel, shift=shift, scale=scale),
        grid=(T,),
        in_specs=[pl.BlockSpec(memory_space=pltpu.SMEM),
                  pl.BlockSpec((1, R, C), lambda t: (t, 0, 0))],
        out_specs=pl.BlockSpec((1, 8, R, C), lambda t: (t, 0, 0, 0)),
        out_shape=jax.ShapeDtypeStruct((T, 8, R, C), F32),
        compiler_params=_params(("parallel",)),
        name="bias_tables",
    )(rel_bias.astype(F32), jnp.asarray(idx_np))


def _dilated_idx():
    qi = np.arange(A_BLOCK)[:, None]
    ki = np.arange(2 * A_BLOCK)[None, :]
    j = qi + A_BLOCK - ki
    out = []
    for window, dilation in A_PATTERNS:
        nk = window // dilation
        valid = (j >= 0) & (j <= nk)
        out.append(np.where(valid, _bucket_np(np.maximum(j, 0) * dilation), -1))
    return np.stack(out).astype(np.int32)


def _sel_idx():
    a = np.arange(TQ)[:, None]
    c = np.arange(TQ)[None, :]
    n_delta = -(-(_far_dist() + TQ) // TQ)
    out = []
    for delta in range(-1, n_delta + 1):
        dist = delta * TQ + a - c
        out.append(np.where(dist >= 0, _bucket_np(dist), -1))
    return np.stack(out).astype(np.int32)


def _far_dist():
    n = np.arange(0, 4 * REL_MAX_DIST)
    b = _bucket_np(n)
    return int(np.max(n[b < REL_BUCKETS - 1])) + 1


def _win_idx():
    a = np.arange(TQ)[:, None]
    c = np.arange(TQ)[None, :]
    out = []
    for delta in range(-1, WIN // TQ + 1):
        dist = delta * TQ + a - c
        out.append(np.where((dist >= 0) & (dist < WIN), _bucket_np(dist), -1))
    return np.stack(out).astype(np.int32)


CMP_PER_TILE = TQ // CMP_STRIDE
CMP_CLASSES = TQ // CMP_PER_TILE
CMP_SPLIT = 13


def _cmp_window_start(qb):
    return qb // CMP_CLASSES + (1 if qb % CMP_CLASSES >= CMP_SPLIT else 0)


def _cmp_idx():
    a = np.arange(TQ)[:, None]
    c = np.arange(TQ)[None, :]
    out = []
    for r in range(CMP_CLASSES):
        qb = CMP_CLASSES + r
        i0 = _cmp_window_start(qb) * TQ - CMP_PAD
        for half in range(2):
            dist = qb * TQ + a - ((i0 + half * TQ + c) * CMP_STRIDE + CMP_BLOCK - 1)
            out.append(np.where(dist >= 0, _bucket_np(dist), -1))
        out.append(np.full((TQ, TQ), REL_BUCKETS - 1))
        out.append(np.full((TQ, TQ), -1))
    return np.stack(out).astype(np.int32)


def _nsa_tables(rel_bias):
    tsel = _bias_tables(rel_bias, _sel_idx(), shift=True, scale=LOG2E)
    twin = _bias_tables(rel_bias, _win_idx(), shift=False, scale=LOG2E)
    tcmp = _bias_tables(rel_bias, _cmp_idx(), shift=True, scale=LOG2E)
    return tsel, twin, tcmp.reshape(CMP_CLASSES, 4, 8, TQ, TQ)


def _check_cmp_windows(S):
    far = _far_dist()
    for qb in range(S // TQ):
        i0 = _cmp_window_start(qb) * TQ - CMP_PAD
        s0 = qb * TQ
        assert s0 - ((i0 - 1) * CMP_STRIDE + CMP_BLOCK - 1) >= far
        assert s0 + TQ - 1 - ((i0 + 2 * TQ) * CMP_STRIDE + CMP_BLOCK - 1) < 0


def _residue_col(d, r):
    return (r % 4) * 4 + r // 4 if d == 16 else r


def _lane_chunks_store(ref3, val):
    for c in range(ref3.shape[0]):
        ref3[c] = val[:, c * 128:(c + 1) * 128]


def _to_residue_layout(src3_ref, dst, d, col_of=_residue_col):
    nc, rows, _ = src3_ref.shape
    for r in range(d):
        cb = col_of(d, r)
        for c in range(nc):
            col = (cb * nc + c) * 128
            dst(slice(col, col + 128), src3_ref[c, pl.ds(r, rows // d, stride=d), :])


def _ab_proj_kernel(x_ref, xh_ref, w_ref, cw_ref, gb_ref,
                    aq_ref, ak_ref, av_ref, aq4_ref, ak4_ref, av4_ref, aq16_ref, ak16_ref, av16_ref,
                    bq_ref, bk_ref, bv_ref, bo_ref, g_ref,
                    pre_ref, tmp_ref, *, tiles_per_seq):
    i = pl.program_id(0)
    tm = x_ref.shape[0]
    xb = x_ref[...].astype(BF16)
    for c, scale, outs in ((0, A_DIM ** -0.5, (aq_ref, aq4_ref, aq16_ref)),
                           (1, 1.0, (ak_ref, ak4_ref, ak16_ref)),
                           (2, 1.0, (av_ref, av4_ref, av16_ref))):
        val = _dot(xb, w_ref[:, c * A_W:(c + 1) * A_W]) * scale
        _lane_chunks_store(tmp_ref, val)
        outs[0][...] = val.astype(BF16)
        for d, o_ref in ((4, outs[1]), (16, outs[2])):
            def put(cols, piece, o_ref=o_ref):
                o_ref[:, cols] = piece.astype(BF16)
            _to_residue_layout(tmp_ref, put, d)
    bv_ref[...] = _dot(xb, w_ref[:, 2560:3072]).astype(BF16)
    bo_ref[...] = _dot(xb, w_ref[:, 3072:3584])
    g_ref[...] = _dot(xb, w_ref[:, 3584:AB_PAD]) + gb_ref[...]
    halo = _dot(xh_ref[...].astype(BF16), w_ref[:, 1536:2560])
    halo = jnp.where(i % tiles_per_seq == 0, 0.0, halo)
    pre_ref[0:8, :] = halo
    pre_ref[8:8 + tm, :] = _dot(xb, w_ref[:, 1536:2560])
    y = pre_ref[8:8 + tm, :] * cw_ref[B_CONV - 1:B_CONV, :]
    for k in range(B_CONV - 1):
        s = B_CONV - 1 - k
        y = y + pre_ref[8 - s:8 - s + tm, :] * cw_ref[k:k + 1, :]
    y = y / (1.0 + jnp.exp(-y))
    bq_ref[...] = (y[:, :B_W] * (B_DIM ** -0.5)).astype(BF16)
    bk_ref[...] = y[:, B_W:].astype(BF16)


def _ab_proj(x2, w_pad, conv_w, gate_b_pad, S):
    N = x2.shape[0]
    tm = TM
    tps = S // tm
    row = lambda i: (i, 0)
    fix = lambda i: (0, 0)
    lay = lambda d: [jax.ShapeDtypeStruct((N // d, d * A_W), BF16)] * 3
    lay_spec = lambda d: [pl.BlockSpec((tm // d, d * A_W), row)] * 3
    outs = lay(1) + lay(4) + lay(16) + [jax.ShapeDtypeStruct((N, 512), BF16)] * 3 + [
        jax.ShapeDtypeStruct((N, 512), F32), jax.ShapeDtypeStruct((N, 128), F32)]
    o_specs = (lay_spec(1) + lay_spec(4) + lay_spec(16) + [pl.BlockSpec((tm, 512), row)] * 4
               + [pl.BlockSpec((tm, 128), row)])
    return pl.pallas_call(
        functools.partial(_ab_proj_kernel, tiles_per_seq=tps),
        grid=(N // tm,),
        in_specs=[pl.BlockSpec((tm, D_MODEL), row),
                  pl.BlockSpec((8, D_MODEL), lambda i: (jnp.maximum(i * (tm // 8) - 1, 0), 0)),
                  pl.BlockSpec((D_MODEL, AB_PAD), fix),
                  pl.BlockSpec((B_CONV, 2 * B_W), fix),
                  pl.BlockSpec((1, 128), fix)],
        out_specs=o_specs,
        out_shape=outs,
        scratch_shapes=[pltpu.VMEM((tm + 8, 2 * B_W), F32), pltpu.VMEM((A_W // 128, tm, 128), F32)],
        compiler_params=_params(("parallel",)),
        name="ab_proj",
    )(x2, x2, w_pad, conv_w, gate_b_pad)


def _dilated_kernel(*refs, has_prev, is_last):
    if has_prev:
        q_ref, kp_ref, kc_ref, vp_ref, vc_ref, tab_ref, op_ref, lp_ref = refs[:8]
        rest = refs[8:]
    else:
        q_ref, kp_ref, kc_ref, vp_ref, vc_ref, tab_ref = refs[:6]
        rest = refs[6:]
    outs, scratch = (rest, ()) if is_last else (rest[:2], rest[2:])
    o_ref = outs[0]
    n = pl.program_id(2)
    n_sub = q_ref.shape[1] // A_BLOCK
    lane = lax.broadcasted_iota(jnp.int32, (A_BLOCK, 128), 1)
    erow = lax.broadcasted_iota(jnp.int32, (128, A_W), 0)
    ecol = lax.broadcasted_iota(jnp.int32, (128, A_W), 1)
    expand = jnp.where(erow - A_DIM == jnp.right_shift(ecol, A_DIM.bit_length() - 1), 1.0, 0.0).astype(BF16)

    def spread(t):
        hi = t.astype(BF16)
        return _dot(hi, expand) + _dot((t - hi.astype(F32)).astype(BF16), expand)

    keep_side = [jnp.where(lane < A_DIM, 1.0, 0.0).astype(BF16), jnp.where(lane < A_DIM, 0.0, 1.0).astype(BF16)]
    odd = jnp.bitwise_and(lane, 1) == 1
    ones_side = [jnp.where(odd, 0.0, 1.0).astype(BF16), jnp.where(odd, 1.0, 0.0).astype(BF16)]
    stat = (lane >= A_DIM) & (lane < A_DIM + A_HEADS)
    for sb in range(n_sub):
        rs = slice(sb * A_BLOCK, (sb + 1) * A_BLOCK)
        first = jnp.where(n == 0, NEG, 0.0) if sb == 0 else 0.0
        m_tile = jnp.zeros((A_BLOCK, 128), F32)
        l_tile = jnp.ones((A_BLOCK, 128), F32)
        unnorm = []
        for j in range(A_HEADS // 2):
            cs = slice(j * 128, (j + 1) * 128)
            q2 = q_ref[0, rs, cs]
            kc, vc = kc_ref[0, rs, cs], vc_ref[0, rs, cs]
            if sb == 0:
                kp, vp = kp_ref[0, :, cs], vp_ref[0, :, cs]
            else:
                ps = slice((sb - 1) * A_BLOCK, sb * A_BLOCK)
                kp, vp = kc_ref[0, ps, cs], vc_ref[0, ps, cs]
            k_st = jnp.concatenate([kp * keep_side[0], kc * keep_side[0],
                                    kp * keep_side[1], kc * keep_side[1]], axis=0)
            s = _dot_nt(q2, k_st)
            p_parts = []
            for side in range(2):
                h = 2 * j + side
                c0 = 2 * side * A_BLOCK
                sp = s[:, c0:c0 + A_BLOCK] + tab_ref[0, h, :, 0:A_BLOCK] + first
                sc = s[:, c0 + A_BLOCK:c0 + 2 * A_BLOCK] + tab_ref[0, h, :, A_BLOCK:2 * A_BLOCK]
                m = jnp.max(jnp.maximum(sp, sc), axis=1, keepdims=True)
                p_parts += [jnp.exp(sp - m).astype(BF16), jnp.exp(sc - m).astype(BF16)]
                m_tile = jnp.where(lane == A_DIM + h, m, m_tile)
            v_st = jnp.concatenate(
                [jnp.concatenate([v * keep_side[side], ones_side[side]], axis=1)
                 for side in range(2) for v in (vp, vc)], axis=0)
            r = _dot(jnp.concatenate(p_parts, axis=1), v_st)
            unnorm.append(r[:, 0:128])
            pair = (lane == A_DIM + 2 * j) | (lane == A_DIM + 2 * j + 1)
            l_tile = jnp.where(pair, r[:, 128:256], l_tile)
        lse = m_tile + jnp.log(l_tile)
        if has_prev:
            lp = lp_ref[0, rs, :]
            mm = jnp.maximum(lp, lse)
            wp = jnp.exp(lp - mm)
            wc = jnp.exp(lse - mm)
            tot = wp + wc
            scale_prev = jnp.where(stat, wp / tot, 0.0)
            scale_cur = jnp.where(stat, wc / (tot * l_tile), 0.0)
            lse = mm + jnp.log(tot)
        else:
            scale_cur = jnp.where(stat, 1.0 / l_tile, 0.0)
        o = jnp.concatenate(unnorm, axis=1) * spread(scale_cur)
        if has_prev:
            o = o + op_ref[0, rs, :] * spread(scale_prev)
        if is_last:
            o_ref[0, rs, :] = o.astype(o_ref.dtype)
        else:
            o_scr, l_scr = scratch[0].at[sb], scratch[1].at[sb]
            _lane_chunks_store(o_scr, o)
            l_scr[0] = jnp.where(stat, lse, 0.0)
            orows = slice(sb * (A_BLOCK // 4), (sb + 1) * (A_BLOCK // 4))

            def put_o(cols, piece, orows=orows):
                o_ref[0, orows, cols] = piece

            def put_l(cols, piece, orows=orows):
                outs[1][0, orows, cols] = piece

            _to_residue_layout(o_scr, put_o, 4)
            _to_residue_layout(l_scr, put_l, 4)


def _dilated_call(q, k, v, tab, prev, pattern_idx, dilation, B, S, is_last):
    d = dilation
    L = S // d
    nb = L // A_BLOCK
    sub = max(s for s in (8, 4, 2, 1) if nb % s == 0)
    r3 = lambda a: a.reshape(B, L, a.shape[-1])
    cur = lambda b, r, n: (b, n, r)
    prv = lambda b, r, n: (b, jnp.maximum(sub * n - 1, 0), r)
    blk = pl.BlockSpec((1, sub * A_BLOCK, A_W), cur)
    in_specs = [blk, pl.BlockSpec((1, A_BLOCK, A_W), prv), blk,
                pl.BlockSpec((1, A_BLOCK, A_W), prv), blk,
                pl.BlockSpec((1, 8, A_BLOCK, 2 * A_BLOCK), lambda b, r, n: (pattern_idx, 0, 0, 0))]
    args = [r3(q), r3(k), r3(k), r3(v), r3(v), tab]
    has_prev = prev is not None
    if has_prev:
        in_specs += [blk, pl.BlockSpec((1, sub * A_BLOCK, 128), cur)]
        args += [r3(prev[0]), r3(prev[1])]
    scratch = []
    if is_last:
        out_shape = [jax.ShapeDtypeStruct((B, L, d * A_W), BF16)]
        out_specs = [blk]
    else:
        d2, rows = 4 * d, sub * A_BLOCK // 4
        nxt = lambda b, r, n: (b, n, r)
        out_shape = [jax.ShapeDtypeStruct((B, S // d2, d2 * A_W), F32),
                     jax.ShapeDtypeStruct((B, S // d2, d2 * 128), F32)]
        out_specs = [pl.BlockSpec((1, rows, 4 * A_W), nxt), pl.BlockSpec((1, rows, 4 * 128), nxt)]
        scratch = [pltpu.VMEM((sub, A_W // 128, A_BLOCK, 128), F32),
                   pltpu.VMEM((sub, 1, A_BLOCK, 128), F32)]
    res = pl.pallas_call(
        functools.partial(_dilated_kernel, has_prev=has_prev, is_last=is_last),
        grid=(B, d, nb // sub),
        in_specs=in_specs, out_specs=out_specs, out_shape=out_shape, scratch_shapes=scratch,
        compiler_params=_params(("parallel", "parallel", "arbitrary")),
        name="dilated_d%d" % d,
    )(*args)
    return [r.reshape(-1, r.shape[-1]) for r in res]


def _dilated_attention(qkv_by_dilation, tab, B, S):
    prev = None
    for p, (window, d) in enumerate(A_PATTERNS):
        assert window // d == A_BLOCK and S % (d * A_BLOCK) == 0
        assert p == 0 or d == 4 * A_PATTERNS[p - 1][1]
        last = p == len(A_PATTERNS) - 1
        q, k, v = qkv_by_dilation[d]
        prev = _dilated_call(q, k, v, tab, prev, p, d, B, S, last)
    return prev[0]


def _split3(x):
    hi = x.astype(BF16)
    r = x - hi.astype(F32)
    mid = r.astype(BF16)
    lo = (r - mid.astype(F32)).astype(BF16)
    return hi, mid, lo


def _mlstm_kernel(q_ref, k_ref, v_ref, g_ref, bo_ref, hg_ref, y_ref, c_ref, m_ref):
    L = B_CHUNK
    c = pl.program_id(1)

    @pl.when(c == 0)
    def _():
        c_ref[...] = jnp.zeros_like(c_ref)
        m_ref[...] = jnp.zeros_like(m_ref)

    lane = lax.broadcasted_iota(jnp.int32, (L, 128), 1)
    row = lax.broadcasted_iota(jnp.int32, (L, L), 0)
    col = lax.broadcasted_iota(jnp.int32, (L, L), 1)
    tri = row >= col
    is_f = (lane >= B_HEADS) & (lane < 2 * B_HEADS)
    tril = jnp.where(tri, 1.0, 0.0).astype(BF16)
    ones = jnp.ones((L, B_DIM), BF16)
    bi = 0
    g = g_ref[bi]
    logf = jnp.minimum(g, 0.0) - jnp.log(1.0 + jnp.exp(-jnp.abs(g)))
    gl = jnp.where(is_f, logf, jnp.where(lane < B_HEADS, g, 0.0))
    hi, mid, lo = _split3(gl)
    cum = _dot(tril, hi) + _dot(tril, mid) + _dot(tril, lo)
    cum_t = cum.T
    gl_t = gl.T
    for h in range(B_HEADS):
        st = h
        cs = slice(h * B_DIM, (h + 1) * B_DIM)
        q = q_ref[bi, :, cs]
        k = k_ref[bi, :, cs]
        v_ext = jnp.concatenate([v_ref[bi, :, cs], ones], axis=1)
        b_col = cum[:, B_HEADS + h:B_HEADS + h + 1]
        b_row = cum_t[B_HEADS + h:B_HEADS + h + 1, :]
        i_col = gl[:, h:h + 1]
        i_row = gl_t[h:h + 1, :]
        m_prev = m_ref[st, 0:1, 0:1]
        dm = jnp.where(tri, b_col - b_row + i_row, NEG)
        inter = b_col + m_prev
        m_t = jnp.maximum(inter, jnp.max(dm, axis=1, keepdims=True))
        p = jnp.exp(dm - m_t)
        sqk = _dot_nt(q, k) * p
        sc = jnp.exp(inter - m_t)
        lhs = jnp.concatenate([(sc * q.astype(F32)).astype(BF16), sqk.astype(BF16)], axis=1)
        c_ext = c_ref[st]
        rhs = jnp.concatenate([c_ext.astype(BF16), v_ext], axis=0)
        res = _dot(lhs, rhs)
        num = res[:, :B_DIM]
        den = res[:, B_DIM:]
        hh = num / jnp.maximum(jnp.abs(den), jnp.exp(-m_t))
        b_last = b_col[L - 1:L, :]
        gk = b_last - b_col + i_col
        m_new = jnp.maximum(b_last + m_prev, jnp.max(gk, axis=0, keepdims=True))
        wk = jnp.exp(gk - m_new)
        decay = jnp.exp(b_last + m_prev - m_new)
        kw_t = (wk * k.astype(F32)).T.astype(BF16)
        c_ref[st] = decay * c_ext + _dot(kw_t, v_ext)
        m_ref[st] = jnp.broadcast_to(m_new, (8, 128))
        mu = jnp.mean(hh, axis=1, keepdims=True)
        xc = hh - mu
        var = jnp.mean(xc * xc, axis=1, keepdims=True)
        hn = xc * lax.rsqrt(var + LN_EPS) * hg_ref[:, cs]
        bo = bo_ref[bi, :, cs]
        y_ref[bi, :, cs] = (hn / (1.0 + jnp.exp(-bo))).astype(BF16)


def _mlstm(bq, bk, bv, gates, bo, head_g, B, S):
    nc = S // B_CHUNK
    r3 = lambda a: a.reshape(B, S, a.shape[-1])
    blk = pl.BlockSpec((1, B_CHUNK, B_W), lambda b, c: (b, c, 0))
    y = pl.pallas_call(
        _mlstm_kernel,
        grid=(B, nc),
        in_specs=[blk, blk, blk,
                  pl.BlockSpec((1, B_CHUNK, 128), lambda b, c: (b, c, 0)),
                  blk,
                  pl.BlockSpec((1, B_W), lambda b, c: (0, 0))],
        out_specs=blk,
        out_shape=jax.ShapeDtypeStruct((B, S, B_W), BF16),
        scratch_shapes=[pltpu.VMEM((B_HEADS, B_DIM, 2 * B_DIM), F32),
                        pltpu.VMEM((B_HEADS, 8, 128), F32)],
        compiler_params=_params(("parallel", "arbitrary")),
        name="mlstm",
    )(r3(bq), r3(bk), r3(bv), r3(gates), r3(bo), head_g.reshape(1, B_W).astype(F32))
    return y.reshape(B * S, B_W)


def _layer_norm(z, g, b):
    mu = jnp.mean(z, axis=1, keepdims=True)
    zc = z - mu
    var = jnp.mean(zc * zc, axis=1, keepdims=True)
    return zc * lax.rsqrt(var + LN_EPS) * g + b


def _route(logits, cnt_ref):
    tm = logits.shape[0]
    lt = logits.T
    col = lambda c: lt[c:c + 1, :]
    gl = [col(c) for c in range(N_GROUPS)]
    gmax = functools.reduce(jnp.maximum, gl)
    gsum = sum(jnp.exp(x - gmax) for x in gl)
    g_idx = jnp.full(gmax.shape, N_GROUPS - 1, jnp.int32)
    for c in range(N_GROUPS - 2, -1, -1):
        g_idx = jnp.where(gl[c] == gmax, c, g_idx)
    g_w = 1.0 / gsum
    el = []
    for k in range(EPG):
        x = col(N_GROUPS + (N_GROUPS - 1) * EPG + k)
        for g in range(N_GROUPS - 2, -1, -1):
            x = jnp.where(g_idx == g, col(N_GROUPS + g * EPG + k), x)
        el.append(x)
    v1 = functools.reduce(jnp.maximum, el)
    i1 = jnp.full(v1.shape, EPG - 1, jnp.int32)
    for k in range(EPG - 2, -1, -1):
        i1 = jnp.where(el[k] == v1, k, i1)
    el2 = [jnp.where(i1 == k, -jnp.inf, el[k]) for k in range(EPG)]
    v2 = functools.reduce(jnp.maximum, el2)
    i2 = jnp.full(v2.shape, EPG - 1, jnp.int32)
    for k in range(EPG - 2, -1, -1):
        i2 = jnp.where((el2[k] == v2) & (i1 != k), k, i2)
    t = jnp.exp(v2 - v1)
    w1 = g_w / (1.0 + t)
    w2 = w1 * t
    a = jnp.minimum(i1, i2)
    b = jnp.maximum(i1, i2)
    pair = jnp.where(a == 0, b - 1, jnp.where(a == 1, b + 1, 5))
    bucket = (g_idx * 6 + pair).astype(F32)
    w_lo = jnp.where(i1 < i2, w1, w2)
    w_hi = jnp.where(i1 < i2, w2, w1)
    sub = lax.broadcasted_iota(jnp.int32, (128, tm), 0)
    onehot_t = jnp.where(sub.astype(F32) == bucket, 1.0, 0.0)
    srow = lax.broadcasted_iota(jnp.int32, (tm, tm), 0)
    scol = lax.broadcasted_iota(jnp.int32, (tm, tm), 1)
    before = jnp.where(srow < scol, 1.0, 0.0).astype(BF16)
    oh = onehot_t.astype(BF16)
    carry = cnt_ref[...]
    prior = _dot(oh, before) + jnp.concatenate([carry] * (tm // 128), axis=1)
    rank = jnp.sum(onehot_t * prior, axis=0, keepdims=True)
    cnt_ref[...] = carry + _dot(oh, jnp.ones((tm, 128), BF16))
    out_t = jnp.where(sub == 0, bucket, jnp.where(sub == 1, w_lo, jnp.where(sub == 2, w_hi,
                      jnp.where(sub == 3, rank, 0.0))))
    return out_t.T


def _out_ln_route_kernel(*refs, n_in, layouts):
    y_refs = refs[:n_in]
    w_refs = refs[n_in:2 * n_in]
    x_ref, g_ref, b_ref, wrh_ref, wrl_ref, br_ref, h_ref, c_ref, cnt_ref = refs[2 * n_in:2 * n_in + 9]
    pos_refs = list(refs[2 * n_in + 9:])
    tm = x_ref.shape[0]

    @pl.when(pl.program_id(0) == 0)
    def _():
        cnt_ref[...] = jnp.zeros_like(cnt_ref)

    y = None
    for i in range(n_in):
        d = layouts[i]
        if d == 1:
            lhs = y_refs[i][...]
        else:
            s_ref = pos_refs.pop(0)
            nc = s_ref.shape[0]
            for r in range(d):
                cb = _residue_col(d, r)
                for c in range(nc):
                    col = (cb * nc + c) * 128
                    s_ref[c, pl.ds(r, tm // d, stride=d), :] = y_refs[i][:, col:col + 128].astype(F32)
            lhs = jnp.concatenate([s_ref[c] for c in range(nc)], axis=1).astype(BF16)
        t = _dot(lhs, w_refs[i][...])
        y = t if y is None else y + t
    hn = _layer_norm(ALPHA * x_ref[...] + y, g_ref[...], b_ref[...])
    h_ref[:, 0:D_MODEL] = hn
    hi = hn.astype(BF16)
    lo = (hn - hi.astype(F32)).astype(BF16)
    logits = (_dot_nt(hi, wrh_ref[...]) + _dot_nt(lo, wrh_ref[...]) + _dot_nt(hi, wrl_ref[...])
              + br_ref[...])
    h_ref[:, D_MODEL:D_MODEL + 128] = _route(logits, cnt_ref)
    c_ref[...] = cnt_ref[...]


def _out_ln_route(ys, layouts, ws, x2, ln_g, ln_b, wr, br):
    N = x2.shape[0]
    tm = TM
    row = lambda i: (i, 0)
    fix = lambda i: (0, 0)
    wr_hi = wr.astype(BF16)
    wr_lo = (wr - wr_hi.astype(F32)).astype(BF16)
    in_specs = ([pl.BlockSpec((tm // d, y.shape[1]), row) for y, d in zip(ys, layouts)]
                + [pl.BlockSpec(w.shape, fix) for w in ws]
                + [pl.BlockSpec((tm, D_MODEL), row),
                   pl.BlockSpec((1, D_MODEL), fix), pl.BlockSpec((1, D_MODEL), fix),
                   pl.BlockSpec((128, D_MODEL), fix), pl.BlockSpec((128, D_MODEL), fix),
                   pl.BlockSpec((1, 128), fix)])
    return pl.pallas_call(
        functools.partial(_out_ln_route_kernel, n_in=len(ys), layouts=tuple(layouts)),
        grid=(N // tm,),
        in_specs=in_specs,
        out_specs=[pl.BlockSpec((tm, D_MODEL + 128), row), pl.BlockSpec((128, 128), fix)],
        out_shape=[jax.ShapeDtypeStruct((N, D_MODEL + 128), F32), jax.ShapeDtypeStruct((128, 128), F32)],
        scratch_shapes=[pltpu.VMEM((128, 128), F32)] + [
            pltpu.VMEM((y.shape[1] // d // 128, tm, 128), F32) for y, d in zip(ys, layouts) if d != 1],
        compiler_params=_params(("arbitrary",)),
        name="out_ln_route",
    )(*ys, *ws, x2, ln_g.reshape(1, -1), ln_b.reshape(1, -1), wr_hi, wr_lo, br)


def _router_weights(wr_g, br_g, wr_e, br_e):
    we = wr_e.transpose(0, 2, 1).reshape(N_GROUPS * EPG, D_MODEL)
    w = jnp.concatenate([wr_g.T, we], axis=0)
    w = jnp.pad(w, ((0, 128 - w.shape[0]), (0, 0)))
    b = jnp.concatenate([br_g, br_e.reshape(-1)])
    b = jnp.pad(b, (0, 128 - b.shape[0])).reshape(1, 128)
    return w.astype(F32), b.astype(F32)


_PAIRS = ((0, 1), (0, 2), (0, 3), (1, 2), (1, 3), (2, 3))


def _moe_kernel(elo_ref, ehi_ref, chg_ref, nt_ref,
                x_ref, wgl_ref, wul_ref, wdl_ref, wgh_ref, wuh_ref, wdh_ref,
                g_ref, b_ref, o_ref, wg_s, wu_s, wd_s):
    t = pl.program_id(0)

    @pl.when(chg_ref[t] == 1)
    def _():
        wg_s[0] = wgl_ref[0, 0].astype(BF16)
        wu_s[0] = wul_ref[0, 0].astype(BF16)
        wd_s[0] = wdl_ref[0, 0].astype(BF16)
        wg_s[1] = wgh_ref[0, 0].astype(BF16)
        wu_s[1] = wuh_ref[0, 0].astype(BF16)
        wd_s[1] = wdh_ref[0, 0].astype(BF16)

    @pl.when(t < nt_ref[0])
    def _():
        x = x_ref[:, 0:D_MODEL]
        xb = x.astype(BF16)
        r = x_ref[:, D_MODEL:D_MODEL + 128]
        acc = None
        for e in range(2):
            a = _dot(xb, wg_s[e])
            u = _dot(xb, wu_s[e])
            hcur = (a / (1.0 + jnp.exp(-a))) * u * r[:, 1 + e:2 + e]
            y = _dot(hcur.astype(BF16), wd_s[e])
            acc = y if acc is None else acc + y
        o_ref[...] = _layer_norm(ALPHA * x + acc, g_ref[...], b_ref[...])

    @pl.when(t >= nt_ref[0])
    def _():
        o_ref[...] = jnp.zeros_like(o_ref)


def _moe(hx, cnt, layer, w_gate, w_up, w_down, ln_g, ln_b):
    N = hx.shape[0]
    tm = TM_MOE
    n_tiles = N // tm + N_BUCKETS
    n_pad = n_tiles * tm
    bucket = hx[:, D_MODEL].astype(jnp.int32)
    rank = hx[:, D_MODEL + 3].astype(jnp.int32)
    counts = cnt[:N_BUCKETS, 0].astype(jnp.int32)
    padded = ((counts + tm - 1) // tm) * tm
    ends = jnp.cumsum(padded)
    offs = ends - padded
    b2 = bucket.reshape(-1, 128)
    off2 = functools.reduce(lambda acc, b: jnp.where(b2 == b, offs[b], acc), range(N_BUCKETS),
                            jnp.zeros_like(b2))
    dest = off2.reshape(-1) + rank
    src = (jnp.arange(n_pad, dtype=jnp.int32) % N).at[dest].set(
        jnp.arange(N, dtype=jnp.int32), mode="promise_in_bounds", unique_indices=True)
    tile_start = jnp.arange(n_tiles, dtype=jnp.int32) * tm
    n_used = (ends[-1] // tm).astype(jnp.int32)
    tb = jnp.sum((tile_start[:, None] >= ends[None, :]).astype(jnp.int32), axis=1)
    tb_last = jnp.take(tb, jnp.maximum(n_used - 1, 0))
    tb = jnp.where(tile_start < ends[-1], tb, tb_last)
    pairs = jnp.asarray(_PAIRS, jnp.int32)
    elo = (tb // 6) * EPG + pairs[tb % 6, 0]
    ehi = (tb // 6) * EPG + pairs[tb % 6, 1]
    chg = jnp.concatenate([jnp.ones((1,), jnp.int32), (tb[1:] != tb[:-1]).astype(jnp.int32)])
    xs = hx.at[src].get(mode="promise_in_bounds")

    row = lambda t, *_: (t, 0)
    fix = lambda t, *_: (0, 0)
    wlo = lambda t, elo, ehi, chg, nt: (layer, elo[t], 0, 0)
    whi = lambda t, elo, ehi, chg, nt: (layer, ehi[t], 0, 0)
    up_spec = lambda im: pl.BlockSpec((1, 1, D_MODEL, E_HID), im)
    dn_spec = lambda im: pl.BlockSpec((1, 1, E_HID, D_MODEL), im)
    grid_spec = pltpu.PrefetchScalarGridSpec(
        num_scalar_prefetch=4,
        grid=(n_tiles,),
        in_specs=[pl.BlockSpec((tm, D_MODEL + 128), row),
                  up_spec(wlo), up_spec(wlo), dn_spec(wlo),
                  up_spec(whi), up_spec(whi), dn_spec(whi),
                  pl.BlockSpec((1, D_MODEL), fix), pl.BlockSpec((1, D_MODEL), fix)],
        out_specs=pl.BlockSpec((tm, D_MODEL), row),
        scratch_shapes=[pltpu.VMEM((2, D_MODEL, E_HID), BF16),
                        pltpu.VMEM((2, D_MODEL, E_HID), BF16),
                        pltpu.VMEM((2, E_HID, D_MODEL), BF16)])
    out_sorted = pl.pallas_call(
        _moe_kernel,
        grid_spec=grid_spec,
        out_shape=jax.ShapeDtypeStruct((n_pad, D_MODEL), F32),
        compiler_params=_params(("arbitrary",)),
        name="moe",
    )(elo, ehi, chg, n_used.reshape(1), xs, w_gate, w_up, w_down, w_gate, w_up, w_down,
      ln_g.reshape(1, -1), ln_b.reshape(1, -1))
    return out_sorted.at[dest].get(mode="promise_in_bounds", unique_indices=True)


def _c_proj_kernel(x_ref, w_ref, gb_ref, q_ref, kc_ref, kv_ref, g_ref, tmp_ref):
    xb = x_ref[...].astype(BF16)
    q_ref[...] = (_dot(xb, w_ref[:, 0:C_W]) * (C_DIM ** -0.5 * LOG2E)).astype(BF16)
    for i in range(4):
        tmp_ref[0] = _dot(xb, w_ref[:, C_W + i * 128:C_W + (i + 1) * 128])

        def put(cols, piece, i=i):
            kc_ref[i, :, cols] = piece.astype(BF16)

        _to_residue_layout(tmp_ref, put, CMP_STRIDE, col_of=lambda d, r: r)
    kv_ref[...] = _dot(xb, w_ref[:, C_W + 512:C_W + 1536]).astype(BF16)
    z = _dot(xb, w_ref[:, C_W + 1536:C_W + 1792]) + gb_ref[...]
    g_ref[...] = 1.0 / (1.0 + jnp.exp(-z))


def _c_proj(x2, w_pad, gb_pad):
    N = x2.shape[0]
    tm = TM
    row = lambda i: (i, 0)
    fix = lambda i: (0, 0)
    wcols = w_pad.shape[1]
    return pl.pallas_call(
        _c_proj_kernel,
        grid=(N // tm,),
        in_specs=[pl.BlockSpec((tm, D_MODEL), row), pl.BlockSpec((D_MODEL, wcols), fix),
                  pl.BlockSpec((1, 256), fix)],
        out_specs=[pl.BlockSpec((tm, C_W), row),
                   pl.BlockSpec((4, tm // CMP_STRIDE, CMP_STRIDE * C_DIM), lambda i: (0, i, 0)),
                   pl.BlockSpec((tm, 1024), row), pl.BlockSpec((tm, 256), row)],
        out_shape=[jax.ShapeDtypeStruct((N, C_W), BF16),
                   jax.ShapeDtypeStruct((4, N // CMP_STRIDE, CMP_STRIDE * C_DIM), BF16),
                   jax.ShapeDtypeStruct((N, 1024), BF16), jax.ShapeDtypeStruct((N, 256), F32)],
        scratch_shapes=[pltpu.VMEM((1, tm, 128), F32)],
        compiler_params=_params(("parallel",)),
        name="c_proj",
    )(x2, w_pad, gb_pad)


def _c_weights(w_in, gate_b):
    gcols = []
    gb = []
    for g in range(C_GROUPS):
        idx = [C_PROJ - 3 * C_HEADS + br * C_HEADS + g * C_HPG + j for br in range(3) for j in range(C_HPG)]
        gcols.append(jnp.pad(w_in[:, np.asarray(idx)], ((0, 0), (0, 128 - len(idx)))))
        gb.append(jnp.pad(gate_b[np.asarray(idx) - (C_PROJ - 3 * C_HEADS)], (0, 128 - len(idx))))
    w = jnp.concatenate([w_in[:, :C_PROJ - 3 * C_HEADS]] + gcols, axis=1).astype(BF16)
    return w, jnp.concatenate(gb).reshape(1, 256).astype(F32)


def _compress_kernel(seg_ref, w1_ref, pos_ref, w1f_ref, w2_ref, o_ref):
    n_seg = seg_ref.shape[1]
    ul = _dot(seg_ref[0], w1_ref[0])
    u = ul[:, :CMP_HIDDEN]
    lnext = pltpu.roll(ul[:, CMP_HIDDEN:], n_seg - 1, 0)
    cpos = _dot(pos_ref[0], w1f_ref[0])[0:1, :]
    pre = u + lnext + cpos
    act = 0.5 * pre * (1.0 + jnp.tanh(math.sqrt(2.0 / math.pi) * (pre + 0.044715 * pre * pre * pre)))
    o_ref[0, 0, 0:CMP_PAD, :] = jnp.zeros((CMP_PAD, C_DIM), BF16)
    o_ref[0, 0, CMP_PAD:CMP_PAD + n_seg, :] = _dot(act.astype(BF16), w2_ref[0]).astype(BF16)


def _compress(kc, cmp_pos, cmp_w1, cmp_w2, B, S):
    n_seg = S // CMP_STRIDE
    half = CMP_STRIDE * C_DIM
    seg = kc.reshape(4 * B, n_seg, half)
    w1 = cmp_w1.astype(BF16)
    w1_ul = jnp.concatenate([w1[:, :half], w1[:, half:]], axis=2)
    pos = jnp.broadcast_to(cmp_pos.reshape(2, 1, CMP_BLOCK * C_DIM), (2, 8, CMP_BLOCK * C_DIM)).astype(BF16)
    out = pl.pallas_call(
        _compress_kernel,
        grid=(4, B),
        in_specs=[pl.BlockSpec((1, n_seg, half), lambda i, b: (i * B + b, 0, 0)),
                  pl.BlockSpec((1, half, 2 * CMP_HIDDEN), lambda i, b: (i // 2, 0, 0)),
                  pl.BlockSpec((1, 8, CMP_BLOCK * C_DIM), lambda i, b: (i // 2, 0, 0)),
                  pl.BlockSpec((1, CMP_BLOCK * C_DIM, CMP_HIDDEN), lambda i, b: (i // 2, 0, 0)),
                  pl.BlockSpec((1, CMP_HIDDEN, C_DIM), lambda i, b: (i // 2, 0, 0))],
        out_specs=pl.BlockSpec((1, 1, CMP_PAD + n_seg, C_DIM), lambda i, b: (i, b, 0, 0)),
        out_shape=jax.ShapeDtypeStruct((4, B, CMP_PAD + n_seg, C_DIM), BF16),
        compiler_params=_params(("parallel", "parallel")),
        name="compress",
    )(seg, w1_ul, pos, w1, cmp_w2.astype(BF16))
    return out


def _overlap_np(n_cmp_pad, n_slc):
    i = np.arange(n_cmp_pad)[:, None] - CMP_PAD
    m = np.arange(n_slc)[None, :]
    start = i * CMP_STRIDE
    ov = (start < (m + 1) * SLC_BLOCK) & (start + CMP_BLOCK - 1 >= m * SLC_BLOCK) & (i >= 0)
    return ov.astype(np.float32)


def _nsa_kernel(q_ref, ks_ref, vs_ref, kw_ref, vw_ref, kc_ref, vc_ref, ov_ref, mk_ref,
                tsel_ref, twin_ref, tcmp_ref, g_ref, o_ref,
                m_ref, acc_ref, p_ref, a_ref, pc_ref, pw_ref, sa_ref, sb_ref, oc_ref, ow_ref, qa_ref,
                *, n_sel_tab, top_n):
    H = C_HPG
    NB = ov_ref.shape[1]
    n_tiles = q_ref.shape[0] // TQ
    NT = 4
    TK = NT * TQ

    def reset():
        m_ref[...] = jnp.full(m_ref.shape, NEG, F32)
        acc_ref[...] = jnp.zeros(acc_ref.shape, F32)

    def softmax_rows(s):
        nw = s.shape[1] // 128
        smax = functools.reduce(jnp.maximum, [s[:, i * 128:(i + 1) * 128] for i in range(nw)])
        m = jnp.broadcast_to(jnp.max(smax, axis=1, keepdims=True), (TQ, 128))
        return jnp.exp2(s - jnp.concatenate([m] * nw, axis=1)), m

    def single_step_branches(a):
        qb = pl.program_id(2) * n_tiles + a
        rows = slice(a * TQ, (a + 1) * TQ)
        q_all = jnp.concatenate([q_ref[rows, h * C_DIM:(h + 1) * C_DIM] for h in range(H)], axis=0)

        t0 = qb // CMP_CLASSES + jnp.where(qb % CMP_CLASSES >= CMP_SPLIT, 1, 0)
        n_ct = kc_ref.shape[2] // TQ
        tile_kind = [3] + [jnp.where(t == t0, 0, jnp.where(t == t0 + 1, 1, jnp.where(t < t0, 2, 3)))
                           for t in range(1, n_ct)]
        s_c = _dot_nt(q_all, kc_ref[0, 0])
        vext_c = jnp.concatenate([vc_ref[0, 0], ov_ref[...]], axis=1)
        inv_c = []
        for h in range(H):
            bias = jnp.concatenate([tcmp_ref[a, tile_kind[t], h] for t in range(n_ct)], axis=1)
            p, m = softmax_rows(s_c[h * TQ:(h + 1) * TQ] + bias)
            l = jnp.sum(p, axis=1, keepdims=True)
            inv_c.append(jnp.where(m > 0.5 * NEG, 1.0 / l, 0.0))
            pc_ref[a, h * TQ:(h + 1) * TQ, :] = p.astype(BF16)
        res_c = _dot(pc_ref[a], vext_c)
        imp = None
        for h in range(H):
            r = res_c[h * TQ:(h + 1) * TQ] * jnp.concatenate([inv_c[h], inv_c[h]], axis=1)
            oc_ref[a, h * TQ:(h + 1) * TQ, :] = r[:, 0:C_DIM]
            imp = r[:, C_DIM:] if imp is None else imp + r[:, C_DIM:]

        n_wt = WIN // TQ + 1
        ww = n_wt * TQ
        st = jnp.maximum(qb - (n_wt - 1), 0)
        r0w = pl.multiple_of(st * TQ, TQ)
        s_w = _dot_nt(q_all, kw_ref[pl.ds(r0w, ww), :])
        vext_w = jnp.concatenate([vw_ref[pl.ds(r0w, ww), :], jnp.ones((ww, C_DIM), BF16)], axis=1)
        widx = [jnp.maximum(qb - (st + c) + 1, 0) for c in range(n_wt)]
        for h in range(H):
            bias = jnp.concatenate([twin_ref[i, h] for i in widx], axis=1)
            p, _ = softmax_rows(s_w[h * TQ:(h + 1) * TQ] + bias)
            pw_ref[a, h * TQ:(h + 1) * TQ, :] = p.astype(BF16)
        res_w = _dot(pw_ref[a], vext_w)
        ow_ref[a] = res_w[:, 0:C_DIM] / res_w[:, C_DIM:]

        shift = SLC_BLOCK.bit_length() - 1
        qpos = qb * TQ + lax.broadcasted_iota(jnp.int32, (TQ, NB), 0)
        mblk = lax.broadcasted_iota(jnp.int32, (TQ, NB), 1)
        qblk = jnp.right_shift(qpos, shift)
        forced = (mblk == 0) | (mblk == qblk) | (mblk == qblk - 1)
        score = jnp.where(forced, 3e38, jnp.where(jnp.left_shift(mblk, shift) <= qpos, imp, NEG))
        score_t = score.T
        blk_t = lax.broadcasted_iota(jnp.int32, (NB, TQ), 0).astype(F32)
        sel_t = jnp.zeros((NB, TQ), F32)
        for _ in range(top_n):
            mx = jnp.max(score_t, axis=0, keepdims=True)
            idx = jnp.min(jnp.where(score_t == mx, blk_t, float(NB)), axis=0, keepdims=True)
            pick = blk_t == idx
            sel_t = jnp.where(pick, 1.0, sel_t)
            score_t = jnp.where(pick, -3e38, score_t)
        unsel = (1.0 - sel_t.T).astype(BF16)
        qa_ref[a * H * TQ:(a + 1) * H * TQ, :] = jnp.concatenate(
            [q_all, jnp.concatenate([unsel] * H, axis=0)], axis=1)

    for a in range(n_tiles):
        single_step_branches(a)

    ones_k = jnp.ones((TK, C_DIM), BF16)
    n_fuse = m_ref.shape[0] // (H * TQ)
    for a in range(0, n_tiles, n_fuse):
        _nsa_selected(a, n_fuse, pl.program_id(2) * n_tiles + a, ks_ref, vs_ref, mk_ref, tsel_ref, g_ref,
                      o_ref, m_ref, acc_ref, p_ref, a_ref, sa_ref, sb_ref, oc_ref, ow_ref, qa_ref, ones_k,
                      reset, n_sel_tab)


def _nsa_selected(a0, n_fuse, qb0, ks_ref, vs_ref, mk_ref, tsel_ref, g_ref, o_ref,
                  m_ref, acc_ref, p_ref, a_ref, sa_ref, sb_ref, oc_ref, ow_ref, qa_ref, ones_k,
                  reset, n_sel_tab):
    H = C_HPG
    NT = 4
    TK = NT * TQ
    hq = H * TQ
    reset()
    n_steps = (qb0 + n_fuse - 1) // NT + 1

    def sel_logits(kq, s_ref, near):
        kc = jnp.minimum(kq, n_steps - 1)
        r0 = pl.multiple_of(kc * TK, TK)
        k_aug = jnp.concatenate([ks_ref[pl.ds(r0, TK), :], mk_ref[pl.ds(r0, TK), :]], axis=1)
        s = _dot_nt(qa_ref[a0 * hq:(a0 + n_fuse) * hq, :], k_aug)
        if not near:
            s_ref[...] = s
            return
        for i in range(n_fuse):
            idx = [jnp.where(kq < n_steps, jnp.clip(qb0 + i - (NT * kc + c) + 1, 0, n_sel_tab - 1), 0)
                   for c in range(NT)]
            for h in range(H):
                rs = slice((i * H + h) * TQ, (i * H + h + 1) * TQ)
                s_ref[rs, :] = s[rs] + jnp.concatenate([tsel_ref[j, h] for j in idx], axis=1)

    def sel_softmax(s_ref, slot):
        for h in range(n_fuse * H):
            rs = slice(h * TQ, (h + 1) * TQ)
            s = s_ref[rs, :]
            m_prev = m_ref[rs, :]
            smax = functools.reduce(jnp.maximum, [s[:, i * 128:(i + 1) * 128] for i in range(NT)])
            m_new = jnp.maximum(m_prev, jnp.max(smax, axis=1, keepdims=True))
            a_ref[slot, rs, :] = jnp.exp2(m_prev - m_new)
            p_ref[slot, rs, :] = jnp.exp2(s - jnp.concatenate([m_new] * NT, axis=1)).astype(BF16)
            m_ref[rs, :] = m_new

    def sel_pv(kq, slot):
        r0 = pl.multiple_of(jnp.clip(kq, 0, n_steps - 1) * TK, TK)
        vext = jnp.concatenate([vs_ref[pl.ds(r0, TK), :], ones_k], axis=1)
        a = a_ref[slot]
        acc_ref[...] = jnp.concatenate([a, a], axis=1) * acc_ref[...] + _dot(p_ref[slot], vext)

    def sel_run(k_lo, count, near):
        @pl.when(count > 0)
        def _():
            sel_logits(k_lo, sa_ref, near)

        def body(j, carry):
            k = k_lo + 2 * j
            sel_logits(k + 1, sb_ref, near)
            sel_softmax(sa_ref, 0)
            sel_pv(k, 0)
            sel_logits(k + 2, sa_ref, near)
            sel_softmax(sb_ref, 1)
            sel_pv(k + 1, 1)
            return carry

        lax.fori_loop(0, count // 2, body, 0)

        @pl.when(count % 2 == 1)
        def _():
            sel_softmax(sa_ref, 0)
            sel_pv(k_lo + count - 1, 0)

    n_far = jnp.maximum((qb0 + 1 - (n_sel_tab - 2)) // NT, 0)
    sel_run(0, n_far, False)
    sel_run(n_far, n_steps - n_far, True)
    for i in range(n_fuse):
        a = a0 + i
        rows = slice(a * TQ, (a + 1) * TQ)
        g = g_ref[rows, :]
        for h in range(H):
            rs = slice(h * TQ, (h + 1) * TQ)
            fr = slice((i * H + h) * TQ, (i * H + h + 1) * TQ)
            out_s = acc_ref[fr, 0:C_DIM] / acc_ref[fr, C_DIM:2 * C_DIM]
            o = (g[:, h:h + 1] * oc_ref[a, rs, :] + g[:, H + h:H + h + 1] * out_s
                 + g[:, 2 * H + h:2 * H + h + 1] * ow_ref[a, rs, :])
            o_ref[rows, h * C_DIM:(h + 1) * C_DIM] = o.astype(BF16)


def _nsa_attention(q, kv, kvc, gates, tsel, twin, tcmp, B, S):
    N = B * S
    QT = S // TQ
    n_slc = S // SLC_BLOCK
    n_cmp_pad = kvc.shape[2]
    NB = 128
    assert n_slc <= NB and n_cmp_pad % TQ == 0 and QT % 4 == 0
    ov = jnp.asarray(_overlap_np(n_cmp_pad, NB), BF16)
    mk = jnp.asarray(np.where(np.arange(S)[:, None] // SLC_BLOCK == np.arange(NB)[None, :], NEG, 0.0), BF16)
    n_sel_delta = tsel.shape[0]
    once = pl.Buffered(1)
    kvspec = lambda c: pl.BlockSpec((S, C_DIM), lambda b, g, t: (b, c + g), pipeline_mode=once)
    cspec = lambda kvi: pl.BlockSpec((1, 1, n_cmp_pad, C_DIM), lambda b, g, t: (kvi * 2 + g, b, 0, 0),
                                     pipeline_mode=once)
    npt = NSA_TILES
    nf = NSA_FUSE
    assert QT % npt == 0 and CMP_CLASSES % npt == 0 and npt % nf == 0 and 4 % npt == 0
    rowblk = lambda b, g, t: (b * (QT // npt) + t, g)
    hq = C_HPG * TQ
    return pl.pallas_call(
        functools.partial(_nsa_kernel, n_sel_tab=n_sel_delta, top_n=min(SLC_TOP_N, n_slc)),
        grid=(B, C_GROUPS, QT // npt),
        in_specs=[pl.BlockSpec((npt * TQ, C_HPG * C_DIM), rowblk),
                  kvspec(0), kvspec(2), kvspec(4), kvspec(6),
                  cspec(0), cspec(1),
                  pl.BlockSpec((n_cmp_pad, NB), lambda b, g, t: (0, 0), pipeline_mode=once),
                  pl.BlockSpec((S, NB), lambda b, g, t: (0, 0), pipeline_mode=once),
                  pl.BlockSpec((n_sel_delta, C_HPG, TQ, TQ), lambda b, g, t: (0, g, 0, 0),
                               pipeline_mode=once),
                  pl.BlockSpec((twin.shape[0], C_HPG, TQ, TQ), lambda b, g, t: (0, g, 0, 0),
                               pipeline_mode=once),
                  pl.BlockSpec((npt, 4, C_HPG, TQ, TQ),
                               lambda b, g, t: (t % (CMP_CLASSES // npt), 0, g, 0, 0)),
                  pl.BlockSpec((npt * TQ, 128), rowblk)],
        out_specs=pl.BlockSpec((npt * TQ, C_HPG * C_DIM), rowblk),
        out_shape=jax.ShapeDtypeStruct((N, C_W), BF16),
        scratch_shapes=[pltpu.VMEM((nf * hq, 128), F32),
                        pltpu.VMEM((nf * hq, 2 * C_DIM), F32),
                        pltpu.VMEM((2, nf * hq, 4 * TQ), BF16),
                        pltpu.VMEM((2, nf * hq, 128), F32),
                        pltpu.VMEM((npt, hq, n_cmp_pad), BF16),
                        pltpu.VMEM((npt, hq, WIN + TQ), BF16),
                        pltpu.VMEM((nf * hq, 4 * TQ), F32), pltpu.VMEM((nf * hq, 4 * TQ), F32),
                        pltpu.VMEM((npt, hq, C_DIM), F32), pltpu.VMEM((npt, hq, C_DIM), F32),
                        pltpu.VMEM((npt * hq, 2 * C_DIM), BF16)],
        compiler_params=_params(("parallel", "parallel", "arbitrary"), NSA_VMEM_LIMIT),
        name="nsa",
    )(q, kv, kv, kv, kv, kvc, kvc, ov, mk, tsel, twin, tcmp, gates)


def _layer_ab(h2, B, S, w_in, gate_b, conv_w, head_g, w_out, dil_tab):
    w_pad = jnp.pad(w_in, ((0, 0), (0, AB_PAD - AB_PROJ))).astype(BF16)
    gb_pad = jnp.pad(gate_b, (0, 128 - gate_b.shape[0])).reshape(1, 128).astype(F32)
    (aq, ak, av, aq4, ak4, av4, aq16, ak16, av16,
     bq, bk, bv, bo, gates) = _ab_proj(h2, w_pad, conv_w.astype(F32), gb_pad, S)
    ya = _dilated_attention({1: (aq, ak, av), 4: (aq4, ak4, av4), 16: (aq16, ak16, av16)}, dil_tab, B, S)
    yb = _mlstm(bq, bk, bv, gates, bo, head_g, B, S)
    wo = w_out.astype(BF16)
    return [ya, yb], [A_PATTERNS[-1][1], 1], [wo[:A_W], wo[A_W:]]


def _layer_c(h2, B, S, w_in, gate_b, cmp_pos, cmp_w1, cmp_w2, w_out, tsel, twin, tcmp):
    w_pad, gb_pad = _c_weights(w_in, gate_b)
    q, kc, kv, gates = _c_proj(h2, w_pad, gb_pad)
    kvc = _compress(kc, cmp_pos, cmp_w1, cmp_w2, B, S)
    out = _nsa_attention(q, kv, kvc, gates, tsel, twin, tcmp, B, S)
    return [out], [1], [w_out.astype(BF16)]


def kernel(x, rel_bias, ln_g, ln_b, ab_w_in, ab_gate_b, ab_conv, ab_head_norm, ab_w_out,
           c_w_in, c_gate_b, c_cmp_pos, c_cmp_w1, c_cmp_w2, c_w_out,
           moe_wr_g, moe_br_g, moe_wr_e, moe_br_e, moe_w_gate, moe_w_up, moe_w_down):
    B, S, D = x.shape
    assert D == D_MODEL and S % (TM) == 0 and S % (16 * A_BLOCK) == 0
    _check_cmp_windows(S)
    dil_tab = _bias_tables(rel_bias, _dilated_idx(), shift=False)
    tsel, twin, tcmp = _nsa_tables(rel_bias)
    h = x.reshape(B * S, D)
    for layer in range(DEPTH):
        j = layer // 2
        if layer % 2 == 0:
            ys, lays, ws = _layer_ab(h, B, S, ab_w_in[j], ab_gate_b[j], ab_conv[j], ab_head_norm[j],
                                     ab_w_out[j], dil_tab)
        else:
            ys, lays, ws = _layer_c(h, B, S, c_w_in[j], c_gate_b[j], c_cmp_pos[j], c_cmp_w1[j],
                                    c_cmp_w2[j], c_w_out[j], tsel, twin, tcmp)
        wr, br = _router_weights(moe_wr_g[layer], moe_br_g[layer], moe_wr_e[layer], moe_br_e[layer])
        hx, cnt = _out_ln_route(ys, lays, ws, h, ln_g[layer, 0], ln_b[layer, 0], wr, br)
        h = _moe(hx, cnt, layer, moe_w_gate, moe_w_up, moe_w_down, ln_g[layer, 1], ln_b[layer, 1])
    return h.reshape(B, S, D)
```

```python
import functools
import math

import numpy as np
import jax
import jax.numpy as jnp
from jax import lax
from jax.experimental import pallas as pl
from jax.experimental.pallas import tpu as pltpu

F32 = jnp.float32
BF16 = jnp.bfloat16
NEG = -1e30
LOG2E = math.log2(math.e)
VMEM_LIMIT = 48 * 1024 * 1024

D_MODEL = 1024
DEPTH = 2
ALPHA = (2.0 * DEPTH) ** 0.25
LN_EPS = 1e-5
REL_BUCKETS = 32
REL_MAX_DIST = 2048

A_HEADS, A_DIM, A_W = 8, 64, 512
A_PATTERNS = ((128, 1), (512, 4), (2048, 16))
A_BLOCK = 128
B_HEADS, B_DIM, B_W = 4, 128, 512
B_CHUNK = 128
B_CONV = 4
AB_PROJ = 3592
AB_PAD = 3712

C_HEADS, C_GROUPS, C_HPG, C_DIM, C_W = 8, 2, 4, 128, 1024
CMP_BLOCK, CMP_STRIDE, CMP_HIDDEN = 32, 16, 256
SLC_BLOCK, SLC_TOP_N, WIN = 64, 16, 512
C_PROJ = 2584
TQ = 128
NSA_TILES = 4
NSA_VMEM_LIMIT = 56 * 1024 * 1024
NSA_FUSE = 2
CMP_PAD = 128

N_GROUPS, EPG, N_EXPERTS, E_HID = 4, 4, 16, 512
N_BUCKETS = N_GROUPS * 6
TM = 512
TM_MOE = 256


def _dot(a, b):
    return jnp.dot(a, b, preferred_element_type=F32)


def _dot_nt(a, b):
    return lax.dot_general(a, b, (((1,), (1,)), ((), ())), preferred_element_type=F32)


def _params(sem, vmem_limit=VMEM_LIMIT):
    return pltpu.CompilerParams(dimension_semantics=sem, vmem_limit_bytes=vmem_limit)


def _bucket_np(n):
    n = np.maximum(n, 0)
    exact = REL_BUCKETS // 2
    nf = np.maximum(n, 1).astype(np.float64)
    large = exact + (np.log(nf / exact) / math.log(REL_MAX_DIST / exact)
                     * (REL_BUCKETS - exact)).astype(np.int64)
    return np.where(n < exact, n, np.minimum(large, REL_BUCKETS - 1)).astype(np.int32)


def _bias_tab_kernel(tab_ref, idx_ref, out_ref, *, shift, scale):
    R = idx_ref.shape[1]
    RC = 32

    def body(i, carry):
        r0 = pl.multiple_of(i * RC, RC)
        idx = idx_ref[0, pl.ds(r0, RC), :]
        for h in range(8):
            base = tab_ref[REL_BUCKETS - 1, h] if shift else 0.0
            val = jnp.full(idx.shape, (tab_ref[0, h] - base) * scale, F32)
            for b in range(1, REL_BUCKETS):
                val = jnp.where(idx == b, (tab_ref[b, h] - base) * scale, val)
            out_ref[0, h, pl.ds(r0, RC), :] = jnp.where(idx < 0, NEG, val)
        return carry

    lax.fori_loop(0, R // RC, body, 0)


def _bias_tables(rel_bias, idx_np, shift, scale=1.0):
    T, R, C = idx_np.shape
    return pl.pallas_call(
        functools.partial(_bias_tab_kernel, shift=shift, scale=scale),
        grid=(T,),
        in_specs=[pl.BlockSpec(memory_space=pltpu.SMEM),
                  pl.BlockSpec((1, R, C), lambda t: (t, 0, 0))],
        out_specs=pl.BlockSpec((1, 8, R, C), lambda t: (t, 0, 0, 0)),
        out_shape=jax.ShapeDtypeStruct((T, 8, R, C), F32),
        compiler_params=_params(("parallel",)),
        name="bias_tables",
    )(rel_bias.astype(F32), jnp.asarray(idx_np))


def _dilated_idx():
    qi = np.arange(A_BLOCK)[:, None]
    ki = np.arange(2 * A_BLOCK)[None, :]
    j = qi + A_BLOCK - ki
    out = []
    for window, dilation in A_PATTERNS:
        nk = window // dilation
        valid = (j >= 0) & (j <= nk)
        out.append(np.where(valid, _bucket_np(np.maximum(j, 0) * dilation), -1))
    return np.stack(out).astype(np.int32)


def _sel_idx():
    a = np.arange(TQ)[:, None]
    c = np.arange(TQ)[None, :]
    n_delta = -(-(_far_dist() + TQ) // TQ)
    out = []
    for delta in range(-1, n_delta + 1):
        dist = delta * TQ + a - c
        out.append(np.where(dist >= 0, _bucket_np(dist), -1))
    return np.stack(out).astype(np.int32)


def _far_dist():
    n = np.arange(0, 4 * REL_MAX_DIST)
    b = _bucket_np(n)
    return int(np.max(n[b < REL_BUCKETS - 1])) + 1


def _win_idx():
    a = np.arange(TQ)[:, None]
    c = np.arange(TQ)[None, :]
    out = []
    for delta in range(-1, WIN // TQ + 1):
        dist = delta * TQ + a - c
        out.append(np.where((dist >= 0) & (dist < WIN), _bucket_np(dist), -1))
    return np.stack(out).astype(np.int32)


CMP_PER_TILE = TQ // CMP_STRIDE
CMP_CLASSES = TQ // CMP_PER_TILE
CMP_SPLIT = 13


def _cmp_window_start(qb):
    return qb // CMP_CLASSES + (1 if qb % CMP_CLASSES >= CMP_SPLIT else 0)


def _cmp_idx():
    a = np.arange(TQ)[:, None]
    c = np.arange(TQ)[None, :]
    out = []
    for r in range(CMP_CLASSES):
        qb = CMP_CLASSES + r
        i0 = _cmp_window_start(qb) * TQ - CMP_PAD
        for half in range(2):
            dist = qb * TQ + a - ((i0 + half * TQ + c) * CMP_STRIDE + CMP_BLOCK - 1)
            out.append(np.where(dist >= 0, _bucket_np(dist), -1))
        out.append(np.full((TQ, TQ), REL_BUCKETS - 1))
        out.append(np.full((TQ, TQ), -1))
    return np.stack(out).astype(np.int32)


def _nsa_tables(rel_bias):
    tsel = _bias_tables(rel_bias, _sel_idx(), shift=True, scale=LOG2E)
    twin = _bias_tables(rel_bias, _win_idx(), shift=False, scale=LOG2E)
    tcmp = _bias_tables(rel_bias, _cmp_idx(), shift=True, scale=LOG2E)
    return tsel, twin, tcmp.reshape(CMP_CLASSES, 4, 8, TQ, TQ)


def _check_cmp_windows(S):
    far = _far_dist()
    for qb in range(S // TQ):
        i0 = _cmp_window_start(qb) * TQ - CMP_PAD
        s0 = qb * TQ
        assert s0 - ((i0 - 1) * CMP_STRIDE + CMP_BLOCK - 1) >= far
        assert s0 + TQ - 1 - ((i0 + 2 * TQ) * CMP_STRIDE + CMP_BLOCK - 1) < 0


def _residue_col(d, r):
    return (r % 4) * 4 + r // 4 if d == 16 else r


def _lane_chunks_store(ref3, val):
    for c in range(ref3.shape[0]):
        ref3[c] = val[:, c * 128:(c + 1) * 128]


def _to_residue_layout(src3_ref, dst, d, col_of=_residue_col):
    nc, rows, _ = src3_ref.shape
    for r in range(d):
        cb = col_of(d, r)
        for c in range(nc):
            col = (cb * nc + c) * 128
            dst(slice(col, col + 128), src3_ref[c, pl.ds(r, rows // d, stride=d), :])


def _ab_proj_kernel(x_ref, xh_ref, w_ref, cw_ref, gb_ref,
                    aq_ref, ak_ref, av_ref, aq4_ref, ak4_ref, av4_ref, aq16_ref, ak16_ref, av16_ref,
                    bq_ref, bk_ref, bv_ref, bo_ref, g_ref,
                    pre_ref, tmp_ref, *, tiles_per_seq):
    i = pl.program_id(0)
    tm = x_ref.shape[0]
    xb = x_ref[...].astype(BF16)
    halo = _dot(xh_ref[...].astype(BF16), w_ref[:, 1536:2560])
    halo = jnp.where(i % tiles_per_seq == 0, 0.0, halo)
    pre_ref[0:8, :] = halo
    pre_ref[8:8 + tm, :] = _dot(xb, w_ref[:, 1536:2560])
    y = pre_ref[8:8 + tm, :] * cw_ref[B_CONV - 1:B_CONV, :]
    for k in range(B_CONV - 1):
        s = B_CONV - 1 - k
        y = y + pre_ref[8 - s:8 - s + tm, :] * cw_ref[k:k + 1, :]
    y = y / (1.0 + jnp.exp(-y))
    bq_ref[...] = (y[:, :B_W] * (B_DIM ** -0.5)).astype(BF16)
    bk_ref[...] = y[:, B_W:].astype(BF16)
    for c, scale, outs in ((0, A_DIM ** -0.5, (aq_ref, aq4_ref, aq16_ref)),
                           (1, 1.0, (ak_ref, ak4_ref, ak16_ref)),
                           (2, 1.0, (av_ref, av4_ref, av16_ref))):
        val = _dot(xb, w_ref[:, c * A_W:(c + 1) * A_W]) * scale
        _lane_chunks_store(tmp_ref, val)
        outs[0][...] = val.astype(BF16)
        for d, o_ref in ((4, outs[1]), (16, outs[2])):
            def put(cols, piece, o_ref=o_ref):
                o_ref[:, cols] = piece.astype(BF16)
            _to_residue_layout(tmp_ref, put, d)
    bv_ref[...] = _dot(xb, w_ref[:, 2560:3072]).astype(BF16)
    bo_ref[...] = _dot(xb, w_ref[:, 3072:3584])
    g_ref[...] = _dot(xb, w_ref[:, 3584:AB_PAD]) + gb_ref[...]


def _ab_proj(x2, w_pad, conv_w, gate_b_pad, S):
    N = x2.shape[0]
    tm = TM
    tps = S // tm
    row = lambda i: (i, 0)
    fix = lambda i: (0, 0)
    lay = lambda d: [jax.ShapeDtypeStruct((N // d, d * A_W), BF16)] * 3
    lay_spec = lambda d: [pl.BlockSpec((tm // d, d * A_W), row)] * 3
    outs = lay(1) + lay(4) + lay(16) + [jax.ShapeDtypeStruct((N, 512), BF16)] * 3 + [
        jax.ShapeDtypeStruct((N, 512), F32), jax.ShapeDtypeStruct((N, 128), F32)]
    o_specs = (lay_spec(1) + lay_spec(4) + lay_spec(16) + [pl.BlockSpec((tm, 512), row)] * 4
               + [pl.BlockSpec((tm, 128), row)])
    return pl.pallas_call(
        functools.partial(_ab_proj_kernel, tiles_per_seq=tps),
        grid=(N // tm,),
        in_specs=[pl.BlockSpec((tm, D_MODEL), row),
                  pl.BlockSpec((8, D_MODEL), lambda i: (jnp.maximum(i * (tm // 8) - 1, 0), 0)),
                  pl.BlockSpec((D_MODEL, AB_PAD), fix),
                  pl.BlockSpec((B_CONV, 2 * B_W), fix),
                  pl.BlockSpec((1, 128), fix)],
        out_specs=o_specs,
        out_shape=outs,
        scratch_shapes=[pltpu.VMEM((tm + 8, 2 * B_W), F32), pltpu.VMEM((A_W // 128, tm, 128), F32)],
        compiler_params=_params(("parallel",)),
        name="ab_proj",
    )(x2, x2, w_pad, conv_w, gate_b_pad)


def _dilated_kernel(*refs, has_prev, is_last):
    if has_prev:
        q_ref, kp_ref, kc_ref, vp_ref, vc_ref, tab_ref, op_ref, lp_ref = refs[:8]
        rest = refs[8:]
    else:
        q_ref, kp_ref, kc_ref, vp_ref, vc_ref, tab_ref = refs[:6]
        rest = refs[6:]
    outs, scratch = (rest, ()) if is_last else (rest[:2], rest[2:])
    o_ref = outs[0]
    n = pl.program_id(2)
    n_sub = q_ref.shape[1] // A_BLOCK
    lane = lax.broadcasted_iota(jnp.int32, (A_BLOCK, 128), 1)
    erow = lax.broadcasted_iota(jnp.int32, (128, A_W), 0)
    ecol = lax.broadcasted_iota(jnp.int32, (128, A_W), 1)
    expand = jnp.where(erow - A_DIM == jnp.right_shift(ecol, A_DIM.bit_length() - 1), 1.0, 0.0).astype(BF16)

    def spread(t):
        hi = t.astype(BF16)
        return _dot(hi, expand) + _dot((t - hi.astype(F32)).astype(BF16), expand)

    keep_side = [jnp.where(lane < A_DIM, 1.0, 0.0).astype(BF16), jnp.where(lane < A_DIM, 0.0, 1.0).astype(BF16)]
    odd = jnp.bitwise_and(lane, 1) == 1
    ones_side = [jnp.where(odd, 0.0, 1.0).astype(BF16), jnp.where(odd, 1.0, 0.0).astype(BF16)]
    stat = (lane >= A_DIM) & (lane < A_DIM + A_HEADS)
    for sb in range(n_sub):
        rs = slice(sb * A_BLOCK, (sb + 1) * A_BLOCK)
        first = jnp.where(n == 0, NEG, 0.0) if sb == 0 else 0.0
        m_tile = jnp.zeros((A_BLOCK, 128), F32)
        l_tile = jnp.ones((A_BLOCK, 128), F32)
        unnorm = []
        for j in range(A_HEADS // 2):
            cs = slice(j * 128, (j + 1) * 128)
            q2 = q_ref[0, rs, cs]
            kc, vc = kc_ref[0, rs, cs], vc_ref[0, rs, cs]
            if sb == 0:
                kp, vp = kp_ref[0, :, cs], vp_ref[0, :, cs]
            else:
                ps = slice((sb - 1) * A_BLOCK, sb * A_BLOCK)
                kp, vp = kc_ref[0, ps, cs], vc_ref[0, ps, cs]
            k_st = jnp.concatenate([kp * keep_side[0], kc * keep_side[0],
                                    kp * keep_side[1], kc * keep_side[1]], axis=0)
            s = _dot_nt(q2, k_st)
            p_parts = []
            for side in range(2):
                h = 2 * j + side
                c0 = 2 * side * A_BLOCK
                sp = s[:, c0:c0 + A_BLOCK] + tab_ref[0, h, :, 0:A_BLOCK] + first
                sc = s[:, c0 + A_BLOCK:c0 + 2 * A_BLOCK] + tab_ref[0, h, :, A_BLOCK:2 * A_BLOCK]
                m = jnp.max(jnp.maximum(sp, sc), axis=1, keepdims=True)
                p_parts += [jnp.exp(sp - m).astype(BF16), jnp.exp(sc - m).astype(BF16)]
                m_tile = jnp.where(lane == A_DIM + h, m, m_tile)
            v_st = jnp.concatenate(
                [jnp.concatenate([v * keep_side[side], ones_side[side]], axis=1)
                 for side in range(2) for v in (vp, vc)], axis=0)
            r = _dot(jnp.concatenate(p_parts, axis=1), v_st)
            unnorm.append(r[:, 0:128])
            pair = (lane == A_DIM + 2 * j) | (lane == A_DIM + 2 * j + 1)
            l_tile = jnp.where(pair, r[:, 128:256], l_tile)
        lse = m_tile + jnp.log(l_tile)
        if has_prev:
            lp = lp_ref[0, rs, :]
            mm = jnp.maximum(lp, lse)
            wp = jnp.exp(lp - mm)
            wc = jnp.exp(lse - mm)
            tot = wp + wc
            scale_prev = jnp.where(stat, wp / tot, 0.0)
            scale_cur = jnp.where(stat, wc / (tot * l_tile), 0.0)
            lse = mm + jnp.log(tot)
        else:
            scale_cur = jnp.where(stat, 1.0 / l_tile, 0.0)
        o = jnp.concatenate(unnorm, axis=1) * spread(scale_cur)
        if has_prev:
            o = o + op_ref[0, rs, :] * spread(scale_prev)
        if is_last:
            o_ref[0, rs, :] = o.astype(o_ref.dtype)
        else:
            o_scr, l_scr = scratch[0].at[sb], scratch[1].at[sb]
            _lane_chunks_store(o_scr, o)
            l_scr[0] = jnp.where(stat, lse, 0.0)
            orows = slice(sb * (A_BLOCK // 4), (sb + 1) * (A_BLOCK // 4))

            def put_o(cols, piece, orows=orows):
                o_ref[0, orows, cols] = piece

            def put_l(cols, piece, orows=orows):
                outs[1][0, orows, cols] = piece

            _to_residue_layout(o_scr, put_o, 4)
            _to_residue_layout(l_scr, put_l, 4)


def _dilated_call(q, k, v, tab, prev, pattern_idx, dilation, B, S, is_last):
    d = dilation
    L = S // d
    nb = L // A_BLOCK
    sub = max(s for s in (8, 4, 2, 1) if nb % s == 0)
    r3 = lambda a: a.reshape(B, L, a.shape[-1])
    cur = lambda b, r, n: (b, n, r)
    prv = lambda b, r, n: (b, jnp.maximum(sub * n - 1, 0), r)
    blk = pl.BlockSpec((1, sub * A_BLOCK, A_W), cur)
    in_specs = [blk, pl.BlockSpec((1, A_BLOCK, A_W), prv), blk,
                pl.BlockSpec((1, A_BLOCK, A_W), prv), blk,
                pl.BlockSpec((1, 8, A_BLOCK, 2 * A_BLOCK), lambda b, r, n: (pattern_idx, 0, 0, 0))]
    args = [r3(q), r3(k), r3(k), r3(v), r3(v), tab]
    has_prev = prev is not None
    if has_prev:
        in_specs += [blk, pl.BlockSpec((1, sub * A_BLOCK, 128), cur)]
        args += [r3(prev[0]), r3(prev[1])]
    scratch = []
    if is_last:
        out_shape = [jax.ShapeDtypeStruct((B, L, d * A_W), BF16)]
        out_specs = [blk]
    else:
        d2, rows = 4 * d, sub * A_BLOCK // 4
        nxt = lambda b, r, n: (b, n, r)
        out_shape = [jax.ShapeDtypeStruct((B, S // d2, d2 * A_W), F32),
                     jax.ShapeDtypeStruct((B, S // d2, d2 * 128), F32)]
        out_specs = [pl.BlockSpec((1, rows, 4 * A_W), nxt), pl.BlockSpec((1, rows, 4 * 128), nxt)]
        scratch = [pltpu.VMEM((sub, A_W // 128, A_BLOCK, 128), F32),
                   pltpu.VMEM((sub, 1, A_BLOCK, 128), F32)]
    res = pl.pallas_call(
        functools.partial(_dilated_kernel, has_prev=has_prev, is_last=is_last),
        grid=(B, d, nb // sub),
        in_specs=in_specs, out_specs=out_specs, out_shape=out_shape, scratch_shapes=scratch,
        compiler_params=_params(("parallel", "parallel", "arbitrary")),
        name="dilated_d%d" % d,
    )(*args)
    return [r.reshape(-1, r.shape[-1]) for r in res]


def _dilated_attention(qkv_by_dilation, tab, B, S):
    prev = None
    for p, (window, d) in enumerate(A_PATTERNS):
        assert window // d == A_BLOCK and S % (d * A_BLOCK) == 0
        assert p == 0 or d == 4 * A_PATTERNS[p - 1][1]
        last = p == len(A_PATTERNS) - 1
        q, k, v = qkv_by_dilation[d]
        prev = _dilated_call(q, k, v, tab, prev, p, d, B, S, last)
    return prev[0]


def _split3(x):
    hi = x.astype(BF16)
    r = x - hi.astype(F32)
    mid = r.astype(BF16)
    lo = (r - mid.astype(F32)).astype(BF16)
    return hi, mid, lo


def _mlstm_kernel(q_ref, k_ref, v_ref, g_ref, bo_ref, hg_ref, y_ref, c_ref, m_ref):
    L = B_CHUNK
    c = pl.program_id(1)

    @pl.when(c == 0)
    def _():
        c_ref[...] = jnp.zeros_like(c_ref)
        m_ref[...] = jnp.zeros_like(m_ref)

    lane = lax.broadcasted_iota(jnp.int32, (L, 128), 1)
    row = lax.broadcasted_iota(jnp.int32, (L, L), 0)
    col = lax.broadcasted_iota(jnp.int32, (L, L), 1)
    tri = row >= col
    is_f = (lane >= B_HEADS) & (lane < 2 * B_HEADS)
    tril = jnp.where(tri, 1.0, 0.0).astype(BF16)
    ones = jnp.ones((L, B_DIM), BF16)
    bi = 0
    g = g_ref[bi]
    logf = jnp.minimum(g, 0.0) - jnp.log(1.0 + jnp.exp(-jnp.abs(g)))
    gl = jnp.where(is_f, logf, jnp.where(lane < B_HEADS, g, 0.0))
    hi, mid, lo = _split3(gl)
    cum = _dot(tril, hi) + _dot(tril, mid) + _dot(tril, lo)
    cum_t = cum.T
    gl_t = gl.T
    for h in range(B_HEADS):
        st = h
        cs = slice(h * B_DIM, (h + 1) * B_DIM)
        q = q_ref[bi, :, cs]
        k = k_ref[bi, :, cs]
        v_ext = jnp.concatenate([v_ref[bi, :, cs], ones], axis=1)
        b_col = cum[:, B_HEADS + h:B_HEADS + h + 1]
        b_row = cum_t[B_HEADS + h:B_HEADS + h + 1, :]
        i_col = gl[:, h:h + 1]
        i_row = gl_t[h:h + 1, :]
        m_prev = m_ref[st, 0:1, 0:1]
        dm = jnp.where(tri, b_col - b_row + i_row, NEG)
        inter = b_col + m_prev
        m_t = jnp.maximum(inter, jnp.max(dm, axis=1, keepdims=True))
        p = jnp.exp(dm - m_t)
        sqk = _dot_nt(q, k) * p
        sc = jnp.exp(inter - m_t)
        lhs = jnp.concatenate([(sc * q.astype(F32)).astype(BF16), sqk.astype(BF16)], axis=1)
        c_ext = c_ref[st]
        rhs = jnp.concatenate([c_ext.astype(BF16), v_ext], axis=0)
        res = _dot(lhs, rhs)
        num = res[:, :B_DIM]
        den = res[:, B_DIM:]
        hh = num / jnp.maximum(jnp.abs(den), jnp.exp(-m_t))
        b_last = b_col[L - 1:L, :]
        gk = b_last - b_col + i_col
        m_new = jnp.maximum(b_last + m_prev, jnp.max(gk, axis=0, keepdims=True))
        wk = jnp.exp(gk - m_new)
        decay = jnp.exp(b_last + m_prev - m_new)
        kw_t = (wk * k.astype(F32)).T.astype(BF16)
        c_ref[st] = decay * c_ext + _dot(kw_t, v_ext)
        m_ref[st] = jnp.broadcast_to(m_new, (8, 128))
        mu = jnp.mean(hh, axis=1, keepdims=True)
        xc = hh - mu
        var = jnp.mean(xc * xc, axis=1, keepdims=True)
        hn = xc * lax.rsqrt(var + LN_EPS) * hg_ref[:, cs]
        bo = bo_ref[bi, :, cs]
        y_ref[bi, :, cs] = (hn / (1.0 + jnp.exp(-bo))).astype(BF16)


def _mlstm(bq, bk, bv, gates, bo, head_g, B, S):
    nc = S // B_CHUNK
    r3 = lambda a: a.reshape(B, S, a.shape[-1])
    blk = pl.BlockSpec((1, B_CHUNK, B_W), lambda b, c: (b, c, 0))
    y = pl.pallas_call(
        _mlstm_kernel,
        grid=(B, nc),
        in_specs=[blk, blk, blk,
                  pl.BlockSpec((1, B_CHUNK, 128), lambda b, c: (b, c, 0)),
                  blk,
                  pl.BlockSpec((1, B_W), lambda b, c: (0, 0))],
        out_specs=blk,
        out_shape=jax.ShapeDtypeStruct((B, S, B_W), BF16),
        scratch_shapes=[pltpu.VMEM((B_HEADS, B_DIM, 2 * B_DIM), F32),
                        pltpu.VMEM((B_HEADS, 8, 128), F32)],
        compiler_params=_params(("parallel", "arbitrary")),
        name="mlstm",
    )(r3(bq), r3(bk), r3(bv), r3(gates), r3(bo), head_g.reshape(1, B_W).astype(F32))
    return y.reshape(B * S, B_W)


def _layer_norm(z, g, b):
    mu = jnp.mean(z, axis=1, keepdims=True)
    zc = z - mu
    var = jnp.mean(zc * zc, axis=1, keepdims=True)
    return zc * lax.rsqrt(var + LN_EPS) * g + b


def _route(logits, cnt_ref):
    tm = logits.shape[0]
    lt = logits.T
    col = lambda c: lt[c:c + 1, :]
    gl = [col(c) for c in range(N_GROUPS)]
    gmax = functools.reduce(jnp.maximum, gl)
    gsum = sum(jnp.exp(x - gmax) for x in gl)
    g_idx = jnp.full(gmax.shape, N_GROUPS - 1, jnp.int32)
    for c in range(N_GROUPS - 2, -1, -1):
        g_idx = jnp.where(gl[c] == gmax, c, g_idx)
    g_w = 1.0 / gsum
    el = []
    for k in range(EPG):
        x = col(N_GROUPS + (N_GROUPS - 1) * EPG + k)
        for g in range(N_GROUPS - 2, -1, -1):
            x = jnp.where(g_idx == g, col(N_GROUPS + g * EPG + k), x)
        el.append(x)
    v1 = functools.reduce(jnp.maximum, el)
    i1 = jnp.full(v1.shape, EPG - 1, jnp.int32)
    for k in range(EPG - 2, -1, -1):
        i1 = jnp.where(el[k] == v1, k, i1)
    el2 = [jnp.where(i1 == k, -jnp.inf, el[k]) for k in range(EPG)]
    v2 = functools.reduce(jnp.maximum, el2)
    i2 = jnp.full(v2.shape, EPG - 1, jnp.int32)
    for k in range(EPG - 2, -1, -1):
        i2 = jnp.where((el2[k] == v2) & (i1 != k), k, i2)
    t = jnp.exp(v2 - v1)
    w1 = g_w / (1.0 + t)
    w2 = w1 * t
    a = jnp.minimum(i1, i2)
    b = jnp.maximum(i1, i2)
    pair = jnp.where(a == 0, b - 1, jnp.where(a == 1, b + 1, 5))
    bucket = (g_idx * 6 + pair).astype(F32)
    w_lo = jnp.where(i1 < i2, w1, w2)
    w_hi = jnp.where(i1 < i2, w2, w1)
    sub = lax.broadcasted_iota(jnp.int32, (128, tm), 0)
    onehot_t = jnp.where(sub.astype(F32) == bucket, 1.0, 0.0)
    srow = lax.broadcasted_iota(jnp.int32, (tm, tm), 0)
    scol = lax.broadcasted_iota(jnp.int32, (tm, tm), 1)
    before = jnp.where(srow < scol, 1.0, 0.0).astype(BF16)
    oh = onehot_t.astype(BF16)
    carry = cnt_ref[...]
    prior = _dot(oh, before) + jnp.concatenate([carry] * (tm // 128), axis=1)
    rank = jnp.sum(onehot_t * prior, axis=0, keepdims=True)
    cnt_ref[...] = carry + _dot(oh, jnp.ones((tm, 128), BF16))
    out_t = jnp.where(sub == 0, bucket, jnp.where(sub == 1, w_lo, jnp.where(sub == 2, w_hi,
                      jnp.where(sub == 3, rank, 0.0))))
    return out_t.T


def _out_ln_route_kernel(*refs, n_in, layouts):
    y_refs = refs[:n_in]
    w_refs = refs[n_in:2 * n_in]
    x_ref, g_ref, b_ref, wrh_ref, wrl_ref, br_ref, h_ref, c_ref, cnt_ref = refs[2 * n_in:2 * n_in + 9]
    pos_refs = list(refs[2 * n_in + 9:])
    tm = x_ref.shape[0]

    @pl.when(pl.program_id(0) == 0)
    def _():
        cnt_ref[...] = jnp.zeros_like(cnt_ref)

    y = None
    for i in range(n_in):
        d = layouts[i]
        if d == 1:
            lhs = y_refs[i][...]
        else:
            s_ref = pos_refs.pop(0)
            nc = s_ref.shape[0]
            for r in range(d):
                cb = _residue_col(d, r)
                for c in range(nc):
                    col = (cb * nc + c) * 128
                    s_ref[c, pl.ds(r, tm // d, stride=d), :] = y_refs[i][:, col:col + 128].astype(F32)
            lhs = jnp.concatenate([s_ref[c] for c in range(nc)], axis=1).astype(BF16)
        t = _dot(lhs, w_refs[i][...])
        y = t if y is None else y + t
    hn = _layer_norm(ALPHA * x_ref[...] + y, g_ref[...], b_ref[...])
    h_ref[:, 0:D_MODEL] = hn
    hi = hn.astype(BF16)
    lo = (hn - hi.astype(F32)).astype(BF16)
    logits = (_dot_nt(hi, wrh_ref[...]) + _dot_nt(lo, wrh_ref[...]) + _dot_nt(hi, wrl_ref[...])
              + br_ref[...])
    h_ref[:, D_MODEL:D_MODEL + 128] = _route(logits, cnt_ref)
    c_ref[...] = cnt_ref[...]


def _out_ln_route(ys, layouts, ws, x2, ln_g, ln_b, wr, br):
    N = x2.shape[0]
    tm = TM
    row = lambda i: (i, 0)
    fix = lambda i: (0, 0)
    wr_hi = wr.astype(BF16)
    wr_lo = (wr - wr_hi.astype(F32)).astype(BF16)
    in_specs = ([pl.BlockSpec((tm // d, y.shape[1]), row) for y, d in zip(ys, layouts)]
                + [pl.BlockSpec(w.shape, fix) for w in ws]
                + [pl.BlockSpec((tm, D_MODEL), row),
                   pl.BlockSpec((1, D_MODEL), fix), pl.BlockSpec((1, D_MODEL), fix),
                   pl.BlockSpec((128, D_MODEL), fix), pl.BlockSpec((128, D_MODEL), fix),
                   pl.BlockSpec((1, 128), fix)])
    return pl.pallas_call(
        functools.partial(_out_ln_route_kernel, n_in=len(ys), layouts=tuple(layouts)),
        grid=(N // tm,),
        in_specs=in_specs,
        out_specs=[pl.BlockSpec((tm, D_MODEL + 128), row), pl.BlockSpec((128, 128), fix)],
        out_shape=[jax.ShapeDtypeStruct((N, D_MODEL + 128), F32), jax.ShapeDtypeStruct((128, 128), F32)],
        scratch_shapes=[pltpu.VMEM((128, 128), F32)] + [
            pltpu.VMEM((y.shape[1] // d // 128, tm, 128), F32) for y, d in zip(ys, layouts) if d != 1],
        compiler_params=_params(("arbitrary",)),
        name="out_ln_route",
    )(*ys, *ws, x2, ln_g.reshape(1, -1), ln_b.reshape(1, -1), wr_hi, wr_lo, br)


def _router_weights(wr_g, br_g, wr_e, br_e):
    we = wr_e.transpose(0, 2, 1).reshape(N_GROUPS * EPG, D_MODEL)
    w = jnp.concatenate([wr_g.T, we], axis=0)
    w = jnp.pad(w, ((0, 128 - w.shape[0]), (0, 0)))
    b = jnp.concatenate([br_g, br_e.reshape(-1)])
    b = jnp.pad(b, (0, 128 - b.shape[0])).reshape(1, 128)
    return w.astype(F32), b.astype(F32)


_PAIRS = ((0, 1), (0, 2), (0, 3), (1, 2), (1, 3), (2, 3))


def _moe_kernel(elo_ref, ehi_ref, chg_ref, nt_ref,
                x_ref, wgl_ref, wul_ref, wdl_ref, wgh_ref, wuh_ref, wdh_ref,
                g_ref, b_ref, o_ref, wg_s, wu_s, wd_s):
    t = pl.program_id(0)

    @pl.when(chg_ref[t] == 1)
    def _():
        wg_s[0] = wgl_ref[0, 0].astype(BF16)
        wu_s[0] = wul_ref[0, 0].astype(BF16)
        wd_s[0] = wdl_ref[0, 0].astype(BF16)
        wg_s[1] = wgh_ref[0, 0].astype(BF16)
        wu_s[1] = wuh_ref[0, 0].astype(BF16)
        wd_s[1] = wdh_ref[0, 0].astype(BF16)

    @pl.when(t < nt_ref[0])
    def _():
        x = x_ref[:, 0:D_MODEL]
        xb = x.astype(BF16)
        r = x_ref[:, D_MODEL:D_MODEL + 128]
        acc = None
        for e in range(2):
            a = _dot(xb, wg_s[e])
            u = _dot(xb, wu_s[e])
            hcur = (a / (1.0 + jnp.exp(-a))) * u * r[:, 1 + e:2 + e]
            y = _dot(hcur.astype(BF16), wd_s[e])
            acc = y if acc is None else acc + y
        o_ref[...] = _layer_norm(ALPHA * x + acc, g_ref[...], b_ref[...])

    @pl.when(t >= nt_ref[0])
    def _():
        o_ref[...] = jnp.zeros_like(o_ref)


def _moe(hx, cnt, layer, w_gate, w_up, w_down, ln_g, ln_b):
    N = hx.shape[0]
    tm = TM_MOE
    n_tiles = N // tm + N_BUCKETS
    n_pad = n_tiles * tm
    bucket = hx[:, D_MODEL].astype(jnp.int32)
    rank = hx[:, D_MODEL + 3].astype(jnp.int32)
    counts = cnt[:N_BUCKETS, 0].astype(jnp.int32)
    padded = ((counts + tm - 1) // tm) * tm
    ends = jnp.cumsum(padded)
    offs = ends - padded
    b2 = bucket.reshape(-1, 128)
    off2 = functools.reduce(lambda acc, b: jnp.where(b2 == b, offs[b], acc), range(N_BUCKETS),
                            jnp.zeros_like(b2))
    dest = off2.reshape(-1) + rank
    src = (jnp.arange(n_pad, dtype=jnp.int32) % N).at[dest].set(
        jnp.arange(N, dtype=jnp.int32), mode="promise_in_bounds", unique_indices=True)
    tile_start = jnp.arange(n_tiles, dtype=jnp.int32) * tm
    n_used = (ends[-1] // tm).astype(jnp.int32)
    tb = jnp.sum((tile_start[:, None] >= ends[None, :]).astype(jnp.int32), axis=1)
    tb_last = jnp.take(tb, jnp.maximum(n_used - 1, 0))
    tb = jnp.where(tile_start < ends[-1], tb, tb_last)
    pairs = jnp.asarray(_PAIRS, jnp.int32)
    elo = (tb // 6) * EPG + pairs[tb % 6, 0]
    ehi = (tb // 6) * EPG + pairs[tb % 6, 1]
    chg = jnp.concatenate([jnp.ones((1,), jnp.int32), (tb[1:] != tb[:-1]).astype(jnp.int32)])
    xs = hx.at[src].get(mode="promise_in_bounds")

    row = lambda t, *_: (t, 0)
    fix = lambda t, *_: (0, 0)
    wlo = lambda t, elo, ehi, chg, nt: (layer, elo[t], 0, 0)
    whi = lambda t, elo, ehi, chg, nt: (layer, ehi[t], 0, 0)
    up_spec = lambda im: pl.BlockSpec((1, 1, D_MODEL, E_HID), im)
    dn_spec = lambda im: pl.BlockSpec((1, 1, E_HID, D_MODEL), im)
    grid_spec = pltpu.PrefetchScalarGridSpec(
        num_scalar_prefetch=4,
        grid=(n_tiles,),
        in_specs=[pl.BlockSpec((tm, D_MODEL + 128), row),
                  up_spec(wlo), up_spec(wlo), dn_spec(wlo),
                  up_spec(whi), up_spec(whi), dn_spec(whi),
                  pl.BlockSpec((1, D_MODEL), fix), pl.BlockSpec((1, D_MODEL), fix)],
        out_specs=pl.BlockSpec((tm, D_MODEL), row),
        scratch_shapes=[pltpu.VMEM((2, D_MODEL, E_HID), BF16),
                        pltpu.VMEM((2, D_MODEL, E_HID), BF16),
                        pltpu.VMEM((2, E_HID, D_MODEL), BF16)])
    out_sorted = pl.pallas_call(
        _moe_kernel,
        grid_spec=grid_spec,
        out_shape=jax.ShapeDtypeStruct((n_pad, D_MODEL), F32),
        compiler_params=_params(("arbitrary",)),
        name="moe",
    )(elo, ehi, chg, n_used.reshape(1), xs, w_gate, w_up, w_down, w_gate, w_up, w_down,
      ln_g.reshape(1, -1), ln_b.reshape(1, -1))
    return out_sorted.at[dest].get(mode="promise_in_bounds", unique_indices=True)


def _c_proj_kernel(x_ref, w_ref, gb_ref, q_ref, kc_ref, kv_ref, g_ref, tmp_ref):
    xb = x_ref[...].astype(BF16)
    q_ref[...] = (_dot(xb, w_ref[:, 0:C_W]) * (C_DIM ** -0.5 * LOG2E)).astype(BF16)
    for i in range(4):
        tmp_ref[0] = _dot(xb, w_ref[:, C_W + i * 128:C_W + (i + 1) * 128])

        def put(cols, piece, i=i):
            kc_ref[i, :, cols] = piece.astype(BF16)

        _to_residue_layout(tmp_ref, put, CMP_STRIDE, col_of=lambda d, r: r)
    kv_ref[...] = _dot(xb, w_ref[:, C_W + 512:C_W + 1536]).astype(BF16)
    z = _dot(xb, w_ref[:, C_W + 1536:C_W + 1792]) + gb_ref[...]
    g_ref[...] = 1.0 / (1.0 + jnp.exp(-z))


def _c_proj(x2, w_pad, gb_pad):
    N = x2.shape[0]
    tm = TM
    row = lambda i: (i, 0)
    fix = lambda i: (0, 0)
    wcols = w_pad.shape[1]
    return pl.pallas_call(
        _c_proj_kernel,
        grid=(N // tm,),
        in_specs=[pl.BlockSpec((tm, D_MODEL), row), pl.BlockSpec((D_MODEL, wcols), fix),
                  pl.BlockSpec((1, 256), fix)],
        out_specs=[pl.BlockSpec((tm, C_W), row),
                   pl.BlockSpec((4, tm // CMP_STRIDE, CMP_STRIDE * C_DIM), lambda i: (0, i, 0)),
                   pl.BlockSpec((tm, 1024), row), pl.BlockSpec((tm, 256), row)],
        out_shape=[jax.ShapeDtypeStruct((N, C_W), BF16),
                   jax.ShapeDtypeStruct((4, N // CMP_STRIDE, CMP_STRIDE * C_DIM), BF16),
                   jax.ShapeDtypeStruct((N, 1024), BF16), jax.ShapeDtypeStruct((N, 256), F32)],
        scratch_shapes=[pltpu.VMEM((1, tm, 128), F32)],
        compiler_params=_params(("parallel",)),
        name="c_proj",
    )(x2, w_pad, gb_pad)


def _c_weights(w_in, gate_b):
    gcols = []
    gb = []
    for g in range(C_GROUPS):
        idx = [C_PROJ - 3 * C_HEADS + br * C_HEADS + g * C_HPG + j for br in range(3) for j in range(C_HPG)]
        gcols.append(jnp.pad(w_in[:, np.asarray(idx)], ((0, 0), (0, 128 - len(idx)))))
        gb.append(jnp.pad(gate_b[np.asarray(idx) - (C_PROJ - 3 * C_HEADS)], (0, 128 - len(idx))))
    w = jnp.concatenate([w_in[:, :C_PROJ - 3 * C_HEADS]] + gcols, axis=1).astype(BF16)
    return w, jnp.concatenate(gb).reshape(1, 256).astype(F32)


def _compress_kernel(seg_ref, w1_ref, pos_ref, w1f_ref, w2_ref, o_ref):
    n_seg = seg_ref.shape[1]
    ul = _dot(seg_ref[0], w1_ref[0])
    u = ul[:, :CMP_HIDDEN]
    lnext = pltpu.roll(ul[:, CMP_HIDDEN:], n_seg - 1, 0)
    cpos = _dot(pos_ref[0], w1f_ref[0])[0:1, :]
    pre = u + lnext + cpos
    act = 0.5 * pre * (1.0 + jnp.tanh(math.sqrt(2.0 / math.pi) * (pre + 0.044715 * pre * pre * pre)))
    o_ref[0, 0, 0:CMP_PAD, :] = jnp.zeros((CMP_PAD, C_DIM), BF16)
    o_ref[0, 0, CMP_PAD:CMP_PAD + n_seg, :] = _dot(act.astype(BF16), w2_ref[0]).astype(BF16)


def _compress(kc, cmp_pos, cmp_w1, cmp_w2, B, S):
    n_seg = S // CMP_STRIDE
    half = CMP_STRIDE * C_DIM
    seg = kc.reshape(4 * B, n_seg, half)
    w1 = cmp_w1.astype(BF16)
    w1_ul = jnp.concatenate([w1[:, :half], w1[:, half:]], axis=2)
    pos = jnp.broadcast_to(cmp_pos.reshape(2, 1, CMP_BLOCK * C_DIM), (2, 8, CMP_BLOCK * C_DIM)).astype(BF16)
    out = pl.pallas_call(
        _compress_kernel,
        grid=(4, B),
        in_specs=[pl.BlockSpec((1, n_seg, half), lambda i, b: (i * B + b, 0, 0)),
                  pl.BlockSpec((1, half, 2 * CMP_HIDDEN), lambda i, b: (i // 2, 0, 0)),
                  pl.BlockSpec((1, 8, CMP_BLOCK * C_DIM), lambda i, b: (i // 2, 0, 0)),
                  pl.BlockSpec((1, CMP_BLOCK * C_DIM, CMP_HIDDEN), lambda i, b: (i // 2, 0, 0)),
                  pl.BlockSpec((1, CMP_HIDDEN, C_DIM), lambda i, b: (i // 2, 0, 0))],
        out_specs=pl.BlockSpec((1, 1, CMP_PAD + n_seg, C_DIM), lambda i, b: (i, b, 0, 0)),
        out_shape=jax.ShapeDtypeStruct((4, B, CMP_PAD + n_seg, C_DIM), BF16),
        compiler_params=_params(("parallel", "parallel")),
        name="compress",
    )(seg, w1_ul, pos, w1, cmp_w2.astype(BF16))
    return out


def _overlap_np(n_cmp_pad, n_slc):
    i = np.arange(n_cmp_pad)[:, None] - CMP_PAD
    m = np.arange(n_slc)[None, :]
    start = i * CMP_STRIDE
    ov = (start < (m + 1) * SLC_BLOCK) & (start + CMP_BLOCK - 1 >= m * SLC_BLOCK) & (i >= 0)
    return ov.astype(np.float32)


def _nsa_kernel(q_ref, ks_ref, vs_ref, kw_ref, vw_ref, kc_ref, vc_ref, ov_ref, mk_ref,
                tsel_ref, twin_ref, tcmp_ref, g_ref, o_ref,
                m_ref, acc_ref, p_ref, a_ref, pc_ref, pw_ref, sa_ref, sb_ref, oc_ref, ow_ref, qa_ref,
                *, n_sel_tab, top_n):
    H = C_HPG
    NB = ov_ref.shape[1]
    n_tiles = q_ref.shape[0] // TQ
    NT = 4
    TK = NT * TQ

    def reset():
        m_ref[...] = jnp.full(m_ref.shape, NEG, F32)
        acc_ref[...] = jnp.zeros(acc_ref.shape, F32)

    def softmax_rows(s):
        nw = s.shape[1] // 128
        smax = functools.reduce(jnp.maximum, [s[:, i * 128:(i + 1) * 128] for i in range(nw)])
        m = jnp.broadcast_to(jnp.max(smax, axis=1, keepdims=True), (TQ, 128))
        return jnp.exp2(s - jnp.concatenate([m] * nw, axis=1)), m

    def single_step_branches(a):
        qb = pl.program_id(2) * n_tiles + a
        rows = slice(a * TQ, (a + 1) * TQ)
        q_all = jnp.concatenate([q_ref[rows, h * C_DIM:(h + 1) * C_DIM] for h in range(H)], axis=0)

        t0 = qb // CMP_CLASSES + jnp.where(qb % CMP_CLASSES >= CMP_SPLIT, 1, 0)
        n_ct = kc_ref.shape[2] // TQ
        tile_kind = [3] + [jnp.where(t == t0, 0, jnp.where(t == t0 + 1, 1, jnp.where(t < t0, 2, 3)))
                           for t in range(1, n_ct)]
        s_c = _dot_nt(q_all, kc_ref[0, 0])
        vext_c = jnp.concatenate([vc_ref[0, 0], ov_ref[...]], axis=1)
        inv_c = []
        for h in range(H):
            bias = jnp.concatenate([tcmp_ref[a, tile_kind[t], h] for t in range(n_ct)], axis=1)
            p, m = softmax_rows(s_c[h * TQ:(h + 1) * TQ] + bias)
            l = jnp.sum(p, axis=1, keepdims=True)
            inv_c.append(jnp.where(m > 0.5 * NEG, 1.0 / l, 0.0))
            pc_ref[a, h * TQ:(h + 1) * TQ, :] = p.astype(BF16)
        res_c = _dot(pc_ref[a], vext_c)
        imp = None
        for h in range(H):
            r = res_c[h * TQ:(h + 1) * TQ] * jnp.concatenate([inv_c[h], inv_c[h]], axis=1)
            oc_ref[a, h * TQ:(h + 1) * TQ, :] = r[:, 0:C_DIM]
            imp = r[:, C_DIM:] if imp is None else imp + r[:, C_DIM:]

        n_wt = WIN // TQ + 1
        ww = n_wt * TQ
        st = jnp.maximum(qb - (n_wt - 1), 0)
        r0w = pl.multiple_of(st * TQ, TQ)
        s_w = _dot_nt(q_all, kw_ref[pl.ds(r0w, ww), :])
        vext_w = jnp.concatenate([vw_ref[pl.ds(r0w, ww), :], jnp.ones((ww, C_DIM), BF16)], axis=1)
        widx = [jnp.maximum(qb - (st + c) + 1, 0) for c in range(n_wt)]
        for h in range(H):
            bias = jnp.concatenate([twin_ref[i, h] for i in widx], axis=1)
            p, _ = softmax_rows(s_w[h * TQ:(h + 1) * TQ] + bias)
            pw_ref[a, h * TQ:(h + 1) * TQ, :] = p.astype(BF16)
        res_w = _dot(pw_ref[a], vext_w)
        ow_ref[a] = res_w[:, 0:C_DIM] / res_w[:, C_DIM:]

        shift = SLC_BLOCK.bit_length() - 1
        qpos = qb * TQ + lax.broadcasted_iota(jnp.int32, (TQ, NB), 0)
        mblk = lax.broadcasted_iota(jnp.int32, (TQ, NB), 1)
        qblk = jnp.right_shift(qpos, shift)
        forced = (mblk == 0) | (mblk == qblk) | (mblk == qblk - 1)
        score = jnp.where(forced, 3e38, jnp.where(jnp.left_shift(mblk, shift) <= qpos, imp, NEG))
        score_t = score.T
        blk_t = lax.broadcasted_iota(jnp.int32, (NB, TQ), 0).astype(F32)
        sel_t = jnp.zeros((NB, TQ), F32)
        for _ in range(top_n):
            mx = jnp.max(score_t, axis=0, keepdims=True)
            idx = jnp.min(jnp.where(score_t == mx, blk_t, float(NB)), axis=0, keepdims=True)
            pick = blk_t == idx
            sel_t = jnp.where(pick, 1.0, sel_t)
            score_t = jnp.where(pick, -3e38, score_t)
        unsel = (1.0 - sel_t.T).astype(BF16)
        qa_ref[a * H * TQ:(a + 1) * H * TQ, :] = jnp.concatenate(
            [q_all, jnp.concatenate([unsel] * H, axis=0)], axis=1)

    for a in range(n_tiles):
        single_step_branches(a)

    ones_k = jnp.ones((TK, C_DIM), BF16)
    n_fuse = m_ref.shape[0] // (H * TQ)
    for a in range(0, n_tiles, n_fuse):
        _nsa_selected(a, n_fuse, pl.program_id(2) * n_tiles + a, ks_ref, vs_ref, mk_ref, tsel_ref, g_ref,
                      o_ref, m_ref, acc_ref, p_ref, a_ref, sa_ref, sb_ref, oc_ref, ow_ref, qa_ref, ones_k,
                      reset, n_sel_tab)


def _nsa_selected(a0, n_fuse, qb0, ks_ref, vs_ref, mk_ref, tsel_ref, g_ref, o_ref,
                  m_ref, acc_ref, p_ref, a_ref, sa_ref, sb_ref, oc_ref, ow_ref, qa_ref, ones_k,
                  reset, n_sel_tab):
    H = C_HPG
    NT = 4
    TK = NT * TQ
    hq = H * TQ
    reset()
    n_steps = (qb0 + n_fuse - 1) // NT + 1

    def sel_logits(kq, s_ref, near):
        kc = jnp.minimum(kq, n_steps - 1)
        r0 = pl.multiple_of(kc * TK, TK)
        k_aug = jnp.concatenate([ks_ref[pl.ds(r0, TK), :], mk_ref[pl.ds(r0, TK), :]], axis=1)
        s = _dot_nt(qa_ref[a0 * hq:(a0 + n_fuse) * hq, :], k_aug)
        if not near:
            s_ref[...] = s
            return
        for i in range(n_fuse):
            idx = [jnp.where(kq < n_steps, jnp.clip(qb0 + i - (NT * kc + c) + 1, 0, n_sel_tab - 1), 0)
                   for c in range(NT)]
            for h in range(H):
                rs = slice((i * H + h) * TQ, (i * H + h + 1) * TQ)
                s_ref[rs, :] = s[rs] + jnp.concatenate([tsel_ref[j, h] for j in idx], axis=1)

    def sel_softmax(s_ref, slot):
        for h in range(n_fuse * H):
            rs = slice(h * TQ, (h + 1) * TQ)
            s = s_ref[rs, :]
            m_prev = m_ref[rs, :]
            smax = functools.reduce(jnp.maximum, [s[:, i * 128:(i + 1) * 128] for i in range(NT)])
            m_new = jnp.maximum(m_prev, jnp.max(smax, axis=1, keepdims=True))
            a_ref[slot, rs, :] = jnp.exp2(m_prev - m_new)
            p_ref[slot, rs, :] = jnp.exp2(s - jnp.concatenate([m_new] * NT, axis=1)).astype(BF16)
            m_ref[rs, :] = m_new

    def sel_pv(kq, slot):
        r0 = pl.multiple_of(jnp.clip(kq, 0, n_steps - 1) * TK, TK)
        vext = jnp.concatenate([vs_ref[pl.ds(r0, TK), :], ones_k], axis=1)
        a = a_ref[slot]
        acc_ref[...] = jnp.concatenate([a, a], axis=1) * acc_ref[...] + _dot(p_ref[slot], vext)

    def sel_run(k_lo, count, near):
        @pl.when(count > 0)
        def _():
            sel_logits(k_lo, sa_ref, near)

        def body(j, carry):
            k = k_lo + 2 * j
            sel_logits(k + 1, sb_ref, near)
            sel_softmax(sa_ref, 0)
            sel_pv(k, 0)
            sel_logits(k + 2, sa_ref, near)
            sel_softmax(sb_ref, 1)
            sel_pv(k + 1, 1)
            return carry

        lax.fori_loop(0, count // 2, body, 0)

        @pl.when(count % 2 == 1)
        def _():
            sel_softmax(sa_ref, 0)
            sel_pv(k_lo + count - 1, 0)

    n_far = jnp.maximum((qb0 + 1 - (n_sel_tab - 2)) // NT, 0)
    sel_run(0, n_far, False)
    sel_run(n_far, n_steps - n_far, True)
    for i in range(n_fuse):
        a = a0 + i
        rows = slice(a * TQ, (a + 1) * TQ)
        g = g_ref[rows, :]
        for h in range(H):
            rs = slice(h * TQ, (h + 1) * TQ)
            fr = slice((i * H + h) * TQ, (i * H + h + 1) * TQ)
            out_s = acc_ref[fr, 0:C_DIM] / acc_ref[fr, C_DIM:2 * C_DIM]
            o = (g[:, h:h + 1] * oc_ref[a, rs, :] + g[:, H + h:H + h + 1] * out_s
                 + g[:, 2 * H + h:2 * H + h + 1] * ow_ref[a, rs, :])
            o_ref[rows, h * C_DIM:(h + 1) * C_DIM] = o.astype(BF16)


def _nsa_attention(q, kv, kvc, gates, tsel, twin, tcmp, B, S):
    N = B * S
    QT = S // TQ
    n_slc = S // SLC_BLOCK
    n_cmp_pad = kvc.shape[2]
    NB = 128
    assert n_slc <= NB and n_cmp_pad % TQ == 0 and QT % 4 == 0
    ov = jnp.asarray(_overlap_np(n_cmp_pad, NB), BF16)
    mk = jnp.asarray(np.where(np.arange(S)[:, None] // SLC_BLOCK == np.arange(NB)[None, :], NEG, 0.0), BF16)
    n_sel_delta = tsel.shape[0]
    once = pl.Buffered(1)
    kvspec = lambda c: pl.BlockSpec((S, C_DIM), lambda b, g, t: (b, c + g), pipeline_mode=once)
    cspec = lambda kvi: pl.BlockSpec((1, 1, n_cmp_pad, C_DIM), lambda b, g, t: (kvi * 2 + g, b, 0, 0),
                                     pipeline_mode=once)
    npt = NSA_TILES
    nf = NSA_FUSE
    assert QT % npt == 0 and CMP_CLASSES % npt == 0 and npt % nf == 0 and 4 % npt == 0
    rowblk = lambda b, g, t: (b * (QT // npt) + t, g)
    hq = C_HPG * TQ
    return pl.pallas_call(
        functools.partial(_nsa_kernel, n_sel_tab=n_sel_delta, top_n=min(SLC_TOP_N, n_slc)),
        grid=(B, C_GROUPS, QT // npt),
        in_specs=[pl.BlockSpec((npt * TQ, C_HPG * C_DIM), rowblk),
                  kvspec(0), kvspec(2), kvspec(4), kvspec(6),
                  cspec(0), cspec(1),
                  pl.BlockSpec((n_cmp_pad, NB), lambda b, g, t: (0, 0), pipeline_mode=once),
                  pl.BlockSpec((S, NB), lambda b, g, t: (0, 0), pipeline_mode=once),
                  pl.BlockSpec((n_sel_delta, C_HPG, TQ, TQ), lambda b, g, t: (0, g, 0, 0),
                               pipeline_mode=once),
                  pl.BlockSpec((twin.shape[0], C_HPG, TQ, TQ), lambda b, g, t: (0, g, 0, 0),
                               pipeline_mode=once),
                  pl.BlockSpec((npt, 4, C_HPG, TQ, TQ),
                               lambda b, g, t: (t % (CMP_CLASSES // npt), 0, g, 0, 0)),
                  pl.BlockSpec((npt * TQ, 128), rowblk)],
        out_specs=pl.BlockSpec((npt * TQ, C_HPG * C_DIM), rowblk),
        out_shape=jax.ShapeDtypeStruct((N, C_W), BF16),
        scratch_shapes=[pltpu.VMEM((nf * hq, 128), F32),
                        pltpu.VMEM((nf * hq, 2 * C_DIM), F32),
                        pltpu.VMEM((2, nf * hq, 4 * TQ), BF16),
                        pltpu.VMEM((2, nf * hq, 128), F32),
                        pltpu.VMEM((npt, hq, n_cmp_pad), BF16),
                        pltpu.VMEM((npt, hq, WIN + TQ), BF16),
                        pltpu.VMEM((nf * hq, 4 * TQ), F32), pltpu.VMEM((nf * hq, 4 * TQ), F32),
                        pltpu.VMEM((npt, hq, C_DIM), F32), pltpu.VMEM((npt, hq, C_DIM), F32),
                        pltpu.VMEM((npt * hq, 2 * C_DIM), BF16)],
        compiler_params=_params(("parallel", "parallel", "arbitrary"), NSA_VMEM_LIMIT),
        name="nsa",
    )(q, kv, kv, kv, kv, kvc, kvc, ov, mk, tsel, twin, tcmp, gates)


def _layer_ab(h2, B, S, w_in, gate_b, conv_w, head_g, w_out, dil_tab):
    w_pad = jnp.pad(w_in, ((0, 0), (0, AB_PAD - AB_PROJ))).astype(BF16)
    gb_pad = jnp.pad(gate_b, (0, 128 - gate_b.shape[0])).reshape(1, 128).astype(F32)
    (aq, ak, av, aq4, ak4, av4, aq16, ak16, av16,
     bq, bk, bv, bo, gates) = _ab_proj(h2, w_pad, conv_w.astype(F32), gb_pad, S)
    ya = _dilated_attention({1: (aq, ak, av), 4: (aq4, ak4, av4), 16: (aq16, ak16, av16)}, dil_tab, B, S)
    yb = _mlstm(bq, bk, bv, gates, bo, head_g, B, S)
    wo = w_out.astype(BF16)
    return [ya, yb], [A_PATTERNS[-1][1], 1], [wo[:A_W], wo[A_W:]]


def _layer_c(h2, B, S, w_in, gate_b, cmp_pos, cmp_w1, cmp_w2, w_out, tsel, twin, tcmp):
    w_pad, gb_pad = _c_weights(w_in, gate_b)
    q, kc, kv, gates = _c_proj(h2, w_pad, gb_pad)
    kvc = _compress(kc, cmp_pos, cmp_w1, cmp_w2, B, S)
    out = _nsa_attention(q, kv, kvc, gates, tsel, twin, tcmp, B, S)
    return [out], [1], [w_out.astype(BF16)]


def kernel(x, rel_bias, ln_g, ln_b, ab_w_in, ab_gate_b, ab_conv, ab_head_norm, ab_w_out,
           c_w_in, c_gate_b, c_cmp_pos, c_cmp_w1, c_cmp_w2, c_w_out,
           moe_wr_g, moe_br_g, moe_wr_e, moe_br_e, moe_w_gate, moe_w_up, moe_w_down):
    B, S, D = x.shape
    assert D == D_MODEL and S % (TM) == 0 and S % (16 * A_BLOCK) == 0
    _check_cmp_windows(S)
    dil_tab = _bias_tables(rel_bias, _dilated_idx(), shift=False)
    tsel, twin, tcmp = _nsa_tables(rel_bias)
    h = x.reshape(B * S, D)
    for layer in range(DEPTH):
        j = layer // 2
        if layer % 2 == 0:
            ys, lays, ws = _layer_ab(h, B, S, ab_w_in[j], ab_gate_b[j], ab_conv[j], ab_head_norm[j],
                                     ab_w_out[j], dil_tab)
        else:
            ys, lays, ws = _layer_c(h, B, S, c_w_in[j], c_gate_b[j], c_cmp_pos[j], c_cmp_w1[j],
                                    c_cmp_w2[j], c_w_out[j], tsel, twin, tcmp)
        wr, br = _router_weights(moe_wr_g[layer], moe_br_g[layer], moe_wr_e[layer], moe_br_e[layer])
        hx, cnt = _out_ln_route(ys, lays, ws, h, ln_g[layer, 0], ln_b[layer, 0], wr, br)
        h = _moe(hx, cnt, layer, moe_w_gate, moe_w_up, moe_w_down, ln_g[layer, 1], ln_b[layer, 1])
    return h.reshape(B, S, D)
```
